```python
import math
import jax, jax.numpy as jnp
from jax import lax
import numpy as np

D_MODEL = 1024
BATCH = 8
SEQ = 16384
DEPTH = 2

N_MIXERS = 2
N_GDN_LAYERS = (DEPTH + 1) // 2
N_SSD_LAYERS = DEPTH // 2
CONV_K = 4
EPS = 1e-6

GDN_H_QK = 8
GDN_H_V = 16
GDN_DK = 128
GDN_DV = 128
GDN_QK_W = GDN_H_QK * GDN_DK
GDN_V_W = GDN_H_V * GDN_DV
GDN_CONV_C = 2 * GDN_QK_W + GDN_V_W
GDN_IN_W = GDN_CONV_C + GDN_V_W + 2 * GDN_H_V
GDN_CHUNK = 64

SSD_D_INNER = 2 * D_MODEL
SSD_HEADDIM = 64
SSD_H = SSD_D_INNER // SSD_HEADDIM
SSD_N = 128
SSD_G = 8
SSD_R = SSD_H // SSD_G
SSD_CONV_C = SSD_D_INNER + 2 * SSD_G * SSD_N
SSD_IN_W = SSD_D_INNER + SSD_CONV_C + SSD_H
SSD_CHUNK = 128

kernel_name = "hybrid_gdn_mamba2_interleaved"


def _rmsnorm(x, w):
    xf = x.astype(jnp.float32)
    y = xf * lax.rsqrt(jnp.mean(xf * xf, axis=-1, keepdims=True) + EPS)
    return (y * w.astype(jnp.float32)).astype(x.dtype)


def _l2norm(x):
    xf = x.astype(jnp.float32)
    return xf * lax.rsqrt(jnp.sum(xf * xf, axis=-1, keepdims=True) + EPS)


def _causal_conv(x, w):
    k = w.shape[0]
    c = x.shape[-1]
    return lax.conv_general_dilated(
        x, w[:, None, :].astype(x.dtype), (1,), [(k - 1, 0)],
        dimension_numbers=("NWC", "WIO", "NWC"), feature_group_count=c)


def _chunk_gated_delta_rule(q, k, v, g, beta):
    b, t, h, dk = q.shape
    dv = v.shape[-1]
    c = GDN_CHUNK
    n = t // c
    f32 = jnp.float32
    def blk(a):
        return a.astype(f32).reshape(b, n, c, h, a.shape[-1]).transpose(0, 3, 1, 2, 4)
    q, k, v = blk(q), blk(k), blk(v)
    g = g.astype(f32).reshape(b, n, c, h).transpose(0, 3, 1, 2)
    beta = beta.astype(f32).reshape(b, n, c, h).transpose(0, 3, 1, 2)
    gc = jnp.cumsum(g, axis=-1)
    tril = jnp.tril(jnp.ones((c, c), dtype=bool))
    strict = jnp.tril(jnp.ones((c, c), dtype=bool), k=-1)
    L = jnp.exp(jnp.where(tril, gc[..., :, None] - gc[..., None, :], -jnp.inf))
    kb = k * beta[..., None]
    vb = v * beta[..., None]
    kkt = jnp.einsum("bhncd,bhnsd->bhncs", kb, k) * L
    a_mat = jnp.where(strict, kkt, 0.0) + jnp.eye(c, dtype=f32)
    rhs = jnp.concatenate([vb, kb * jnp.exp(gc)[..., None]], axis=-1)
    sol = lax.linalg.triangular_solve(a_mat, rhs, left_side=True, lower=True,
                                      unit_diagonal=True)
    u, w = sol[..., :dv], sol[..., dv:]
    attn = jnp.einsum("bhncd,bhnsd->bhncs", q, k) * L
    q_dec = q * jnp.exp(gc)[..., None]
    k_dec = k * jnp.exp(gc[..., -1:] - gc)[..., None]
    g_last = jnp.exp(gc[..., -1])

    def step(S, inp):
        qd, kd, u_c, w_c, a_c, gl = inp
        v_new = u_c - jnp.einsum("bhcd,bhde->bhce", w_c, S)
        o = jnp.einsum("bhcd,bhde->bhce", qd, S) + jnp.einsum("bhcs,bhse->bhce", a_c, v_new)
        S = S * gl[..., None, None] + jnp.einsum("bhcd,bhce->bhde", kd, v_new)
        return S, o

    mv = lambda a: jnp.moveaxis(a, 2, 0)
    S0 = jnp.zeros((b, h, dk, dv), f32)
    _, o = lax.scan(step, S0, (mv(q_dec), mv(k_dec), mv(u), mv(w), mv(attn), mv(g_last)))
    return o.transpose(1, 0, 3, 2, 4).reshape(b, t, h, dv)


def _gated_deltanet(hid, w_in, conv_w, a_log, dt_bias, norm_w, w_out):
    b, t, _ = hid.shape
    proj = hid @ w_in
    qkv, z, b_raw, a_raw = jnp.split(
        proj, [GDN_CONV_C, GDN_CONV_C + GDN_V_W, GDN_CONV_C + GDN_V_W + GDN_H_V], axis=-1)
    qkv = jax.nn.silu(_causal_conv(qkv, conv_w))
    q, k, v = jnp.split(qkv, [GDN_QK_W, 2 * GDN_QK_W], axis=-1)
    rep = GDN_H_V // GDN_H_QK
    q = jnp.repeat(_l2norm(q.reshape(b, t, GDN_H_QK, GDN_DK)) * (GDN_DK ** -0.5), rep, axis=2)
    k = jnp.repeat(_l2norm(k.reshape(b, t, GDN_H_QK, GDN_DK)), rep, axis=2)
    v = v.reshape(b, t, GDN_H_V, GDN_DV)
    beta = jax.nn.sigmoid(b_raw.astype(jnp.float32))
    g = -jnp.exp(a_log.astype(jnp.float32)) * jax.nn.softplus(
        a_raw.astype(jnp.float32) + dt_bias.astype(jnp.float32))
    o = _chunk_gated_delta_rule(q, k, v, g, beta).astype(hid.dtype)
    o = _rmsnorm(o, norm_w) * jax.nn.silu(z.reshape(b, t, GDN_H_V, GDN_DV))
    return o.reshape(b, t, GDN_V_W) @ w_out


def _ssd_chunked(X, adt, Bm, Cm):
    b, t, h, p = X.shape
    q = SSD_CHUNK
    n = t // q
    f32 = jnp.float32
    X = X.astype(f32).reshape(b, n, q, SSD_G, SSD_R, p)
    adt = adt.astype(f32).reshape(b, n, q, SSD_G, SSD_R).transpose(0, 3, 4, 1, 2)
    Bm = Bm.astype(f32).reshape(b, n, q, SSD_G, SSD_N)
    Cm = Cm.astype(f32).reshape(b, n, q, SSD_G, SSD_N)
    acs = jnp.cumsum(adt, axis=-1)
    tril = jnp.tril(jnp.ones((q, q), dtype=bool))
    Lm = jnp.exp(jnp.where(tril, acs[..., :, None] - acs[..., None, :], -jnp.inf))
    cb = jnp.einsum("bnlgd,bnsgd->bgnls", Cm, Bm)
    y_diag = jnp.einsum("bgrnls,bnsgrp->bnlgrp", cb[:, :, None] * Lm, X)
    decay_states = jnp.exp(acs[..., -1:] - acs)
    states = jnp.einsum("bnsgd,bgrns,bnsgrp->bngrpd", Bm, decay_states, X)
    chunk_decay = jnp.exp(acs[..., -1])

    def step(S, inp):
        st, dec = inp
        return S * dec[..., None, None] + st, S

    S0 = jnp.zeros((b, SSD_G, SSD_R, p, SSD_N), f32)
    _, prev = lax.scan(step, S0, (jnp.moveaxis(states, 1, 0), jnp.moveaxis(chunk_decay, 3, 0)))
    prev = jnp.moveaxis(prev, 0, 1)
    y_off = jnp.einsum("bnlgd,bngrpd,bgrnl->bnlgrp", Cm, prev, jnp.exp(acs))
    return (y_diag + y_off).reshape(b, t, h, p)


def _mamba2(hid, w_in, conv_w, conv_b, dt_bias, a_log, d_skip, norm_w, w_out):
    b, t, _ = hid.shape
    proj = hid @ w_in
    z, xbc, dt = jnp.split(proj, [SSD_D_INNER, SSD_D_INNER + SSD_CONV_C], axis=-1)
    xbc = jax.nn.silu(_causal_conv(xbc, conv_w) + conv_b)
    xs, Bm, Cm = jnp.split(xbc, [SSD_D_INNER, SSD_D_INNER + SSD_G * SSD_N], axis=-1)
    xs = xs.reshape(b, t, SSD_H, SSD_HEADDIM)
    Bm = Bm.reshape(b, t, SSD_G, SSD_N)
    Cm = Cm.reshape(b, t, SSD_G, SSD_N)
    dt = jax.nn.softplus(dt.astype(jnp.float32) + dt_bias.astype(jnp.float32))
    A = -jnp.exp(a_log.astype(jnp.float32))
    y = _ssd_chunked(xs.astype(jnp.float32) * dt[..., None], A * dt, Bm, Cm)
    y = (y + xs.astype(jnp.float32) * d_skip.astype(jnp.float32)[:, None]).astype(hid.dtype)
    y = y.reshape(b, t, SSD_D_INNER) * jax.nn.silu(z)
    gs = SSD_D_INNER // SSD_G
    y = _rmsnorm(y.reshape(b, t, SSD_G, gs), norm_w.reshape(SSD_G, gs)).reshape(b, t, SSD_D_INNER)
    return y @ w_out


def _dt_bias_init(key, n):
    dt = jnp.exp(jax.random.uniform(key, (n,), minval=math.log(1e-3), maxval=math.log(1e-1)))
    return dt + jnp.log(-jnp.expm1(-dt))


def _fwd_setup_inputs(seed: int = 0) -> dict:
    key = jax.random.key(seed)
    ks = jax.random.split(key, 20)
    nA, nB = N_GDN_LAYERS, N_SSD_LAYERS
    nrm = jax.random.normal
    return {
        "x": nrm(ks[0], (BATCH, SEQ, D_MODEL), jnp.float32),
        "norm_w": 1.0 + 0.01 * nrm(ks[1], (DEPTH, D_MODEL), jnp.float32),
        "gdn_w_in": nrm(ks[2], (nA, D_MODEL, GDN_IN_W), jnp.float32) * D_MODEL ** -0.5,
        "gdn_conv_w": nrm(ks[3], (nA, CONV_K, GDN_CONV_C), jnp.float32) * 0.5,
        "gdn_a_log": jnp.log(jax.random.uniform(ks[4], (nA, GDN_H_V), minval=1.0, maxval=16.0)),
        "gdn_dt_bias": jnp.stack([_dt_bias_init(kk, GDN_H_V) for kk in jax.random.split(ks[5], nA)]),
        "gdn_norm_w": 1.0 + 0.01 * nrm(ks[6], (nA, GDN_DV), jnp.float32),
        "gdn_w_out": nrm(ks[7], (nA, GDN_V_W, D_MODEL), jnp.float32) * GDN_V_W ** -0.5,
        "ssd_w_in": nrm(ks[8], (nB, D_MODEL, SSD_IN_W), jnp.float32) * D_MODEL ** -0.5,
        "ssd_conv_w": nrm(ks[9], (nB, CONV_K, SSD_CONV_C), jnp.float32) * 0.5,
        "ssd_conv_b": 0.01 * nrm(ks[10], (nB, SSD_CONV_C), jnp.float32),
        "ssd_dt_bias": jnp.stack([_dt_bias_init(kk, SSD_H) for kk in jax.random.split(ks[11], nB)]),
        "ssd_a_log": jnp.log(jax.random.uniform(ks[12], (nB, SSD_H), minval=1.0, maxval=16.0)),
        "ssd_d": 1.0 + 0.01 * nrm(ks[13], (nB, SSD_H), jnp.float32),
        "ssd_norm_w": 1.0 + 0.01 * nrm(ks[14], (nB, SSD_D_INNER), jnp.float32),
        "ssd_w_out": nrm(ks[15], (nB, SSD_D_INNER, D_MODEL), jnp.float32) * SSD_D_INNER ** -0.5,
        "final_norm_w": 1.0 + 0.01 * nrm(ks[16], (D_MODEL,), jnp.float32),
    }


def _fwd_reference(x, norm_w, gdn_w_in, gdn_conv_w, gdn_a_log, gdn_dt_bias, gdn_norm_w, gdn_w_out,
              ssd_w_in, ssd_conv_w, ssd_conv_b, ssd_dt_bias, ssd_a_log, ssd_d, ssd_norm_w,
              ssd_w_out, final_norm_w):
    for i in range(DEPTH):
        hid = _rmsnorm(x, norm_w[i])
        j = i // N_MIXERS
        if i % N_MIXERS == 0:
            x = x + _gated_deltanet(hid, gdn_w_in[j], gdn_conv_w[j], gdn_a_log[j],
                                    gdn_dt_bias[j], gdn_norm_w[j], gdn_w_out[j])
        else:
            x = x + _mamba2(hid, ssd_w_in[j], ssd_conv_w[j], ssd_conv_b[j], ssd_dt_bias[j],
                            ssd_a_log[j], ssd_d[j], ssd_norm_w[j], ssd_w_out[j])
    return _rmsnorm(x, final_norm_w)


import jax as _jax
import jax.numpy as _jnp

TWIN_FORMAT = 'train_step'
FWD_PARAMS = ['x', 'norm_w', 'gdn_w_in', 'gdn_conv_w', 'gdn_a_log', 'gdn_dt_bias', 'gdn_norm_w', 'gdn_w_out', 'ssd_w_in', 'ssd_conv_w', 'ssd_conv_b', 'ssd_dt_bias', 'ssd_a_log', 'ssd_d', 'ssd_norm_w', 'ssd_w_out', 'final_norm_w']
TWIN_WEIGHTS = ['norm_w', 'gdn_w_in', 'gdn_conv_w', 'gdn_a_log', 'gdn_dt_bias', 'gdn_norm_w', 'gdn_w_out', 'ssd_w_in', 'ssd_conv_w', 'ssd_conv_b', 'ssd_dt_bias', 'ssd_a_log', 'ssd_d', 'ssd_norm_w', 'ssd_w_out', 'final_norm_w']
TWIN_DIFF_INPUT = 'x'
TWIN_INPUTS = ['x', 'norm_w', 'gdn_w_in', 'gdn_conv_w', 'gdn_a_log', 'gdn_dt_bias', 'gdn_norm_w', 'gdn_w_out', 'ssd_w_in', 'ssd_conv_w', 'ssd_conv_b', 'ssd_dt_bias', 'ssd_a_log', 'ssd_d', 'ssd_norm_w', 'ssd_w_out', 'final_norm_w', 'loss_target', 'm_norm_w', 'm_gdn_w_in', 'm_gdn_conv_w', 'm_gdn_a_log', 'm_gdn_dt_bias', 'm_gdn_norm_w', 'm_gdn_w_out', 'm_ssd_w_in', 'm_ssd_conv_w', 'm_ssd_conv_b', 'm_ssd_dt_bias', 'm_ssd_a_log', 'm_ssd_d', 'm_ssd_norm_w', 'm_ssd_w_out', 'm_final_norm_w', 'v_norm_w', 'v_gdn_w_in', 'v_gdn_conv_w', 'v_gdn_a_log', 'v_gdn_dt_bias', 'v_gdn_norm_w', 'v_gdn_w_out', 'v_ssd_w_in', 'v_ssd_conv_w', 'v_ssd_conv_b', 'v_ssd_dt_bias', 'v_ssd_a_log', 'v_ssd_d', 'v_ssd_norm_w', 'v_ssd_w_out', 'v_final_norm_w']
TWIN_OUTPUTS = ['loss', 'grad_x', 'grad_norm_w', 'grad_gdn_w_in', 'grad_gdn_conv_w', 'grad_gdn_a_log', 'grad_gdn_dt_bias', 'grad_gdn_norm_w', 'grad_gdn_w_out', 'grad_ssd_w_in', 'grad_ssd_conv_w', 'grad_ssd_conv_b', 'grad_ssd_dt_bias', 'grad_ssd_a_log', 'grad_ssd_d', 'grad_ssd_norm_w', 'grad_ssd_w_out', 'grad_final_norm_w', 'delta_norm_w', 'delta_gdn_w_in', 'delta_gdn_conv_w', 'delta_gdn_a_log', 'delta_gdn_dt_bias', 'delta_gdn_norm_w', 'delta_gdn_w_out', 'delta_ssd_w_in', 'delta_ssd_conv_w', 'delta_ssd_conv_b', 'delta_ssd_dt_bias', 'delta_ssd_a_log', 'delta_ssd_d', 'delta_ssd_norm_w', 'delta_ssd_w_out', 'delta_final_norm_w', 'new_m_norm_w', 'new_m_gdn_w_in', 'new_m_gdn_conv_w', 'new_m_gdn_a_log', 'new_m_gdn_dt_bias', 'new_m_gdn_norm_w', 'new_m_gdn_w_out', 'new_m_ssd_w_in', 'new_m_ssd_conv_w', 'new_m_ssd_conv_b', 'new_m_ssd_dt_bias', 'new_m_ssd_a_log', 'new_m_ssd_d', 'new_m_ssd_norm_w', 'new_m_ssd_w_out', 'new_m_final_norm_w', 'new_v_norm_w', 'new_v_gdn_w_in', 'new_v_gdn_conv_w', 'new_v_gdn_a_log', 'new_v_gdn_dt_bias', 'new_v_gdn_norm_w', 'new_v_gdn_w_out', 'new_v_ssd_w_in', 'new_v_ssd_conv_w', 'new_v_ssd_conv_b', 'new_v_ssd_dt_bias', 'new_v_ssd_a_log', 'new_v_ssd_d', 'new_v_ssd_norm_w', 'new_v_ssd_w_out', 'new_v_final_norm_w']
TWIN_LEAF_KINDS = {'loss': 'loss', 'grad_x': 'grad_x', 'grad_norm_w': 'grad_w', 'grad_gdn_w_in': 'grad_w', 'grad_gdn_conv_w': 'grad_w', 'grad_gdn_a_log': 'grad_w', 'grad_gdn_dt_bias': 'grad_w', 'grad_gdn_norm_w': 'grad_w', 'grad_gdn_w_out': 'grad_w', 'grad_ssd_w_in': 'grad_w', 'grad_ssd_conv_w': 'grad_w', 'grad_ssd_conv_b': 'grad_w', 'grad_ssd_dt_bias': 'grad_w', 'grad_ssd_a_log': 'grad_w', 'grad_ssd_d': 'grad_w', 'grad_ssd_norm_w': 'grad_w', 'grad_ssd_w_out': 'grad_w', 'grad_final_norm_w': 'grad_w', 'delta_norm_w': 'delta_w', 'delta_gdn_w_in': 'delta_w', 'delta_gdn_conv_w': 'delta_w', 'delta_gdn_a_log': 'delta_w', 'delta_gdn_dt_bias': 'delta_w', 'delta_gdn_norm_w': 'delta_w', 'delta_gdn_w_out': 'delta_w', 'delta_ssd_w_in': 'delta_w', 'delta_ssd_conv_w': 'delta_w', 'delta_ssd_conv_b': 'delta_w', 'delta_ssd_dt_bias': 'delta_w', 'delta_ssd_a_log': 'delta_w', 'delta_ssd_d': 'delta_w', 'delta_ssd_norm_w': 'delta_w', 'delta_ssd_w_out': 'delta_w', 'delta_final_norm_w': 'delta_w', 'new_m_norm_w': 'new_m', 'new_m_gdn_w_in': 'new_m', 'new_m_gdn_conv_w': 'new_m', 'new_m_gdn_a_log': 'new_m', 'new_m_gdn_dt_bias': 'new_m', 'new_m_gdn_norm_w': 'new_m', 'new_m_gdn_w_out': 'new_m', 'new_m_ssd_w_in': 'new_m', 'new_m_ssd_conv_w': 'new_m', 'new_m_ssd_conv_b': 'new_m', 'new_m_ssd_dt_bias': 'new_m', 'new_m_ssd_a_log': 'new_m', 'new_m_ssd_d': 'new_m', 'new_m_ssd_norm_w': 'new_m', 'new_m_ssd_w_out': 'new_m', 'new_m_final_norm_w': 'new_m', 'new_v_norm_w': 'new_v', 'new_v_gdn_w_in': 'new_v', 'new_v_gdn_conv_w': 'new_v', 'new_v_gdn_a_log': 'new_v', 'new_v_gdn_dt_bias': 'new_v', 'new_v_gdn_norm_w': 'new_v', 'new_v_gdn_w_out': 'new_v', 'new_v_ssd_w_in': 'new_v', 'new_v_ssd_conv_w': 'new_v', 'new_v_ssd_conv_b': 'new_v', 'new_v_ssd_dt_bias': 'new_v', 'new_v_ssd_a_log': 'new_v', 'new_v_ssd_d': 'new_v', 'new_v_ssd_norm_w': 'new_v', 'new_v_ssd_w_out': 'new_v', 'new_v_final_norm_w': 'new_v'}


def _forward(args):
    return _fwd_reference(*[args[k] for k in FWD_PARAMS])


def _output_shape():
    def fwd():
        inp = _fwd_setup_inputs(0)
        return _fwd_reference(*[inp[k] for k in FWD_PARAMS])
    out = _jax.eval_shape(fwd)
    return out.shape, out.dtype

N_MICROBATCH = 1
ADAM_LR = 0.001
ADAM_B1 = 0.9
ADAM_B2 = 0.999
ADAM_EPS = 1e-08
ADAM_WD = 0.01
ADAM_STEP = 10
PER_EXAMPLE_BATCH_AXIS = {'x': 0, 'loss_target': 0}
SHARED_INPUTS = []
_WEIGHT_DTYPES = {'norm_w': _jnp.float32, 'gdn_w_in': _jnp.float32, 'gdn_conv_w': _jnp.float32, 'gdn_a_log': _jnp.float32, 'gdn_dt_bias': _jnp.float32, 'gdn_norm_w': _jnp.float32, 'gdn_w_out': _jnp.float32, 'ssd_w_in': _jnp.float32, 'ssd_conv_w': _jnp.float32, 'ssd_conv_b': _jnp.float32, 'ssd_dt_bias': _jnp.float32, 'ssd_a_log': _jnp.float32, 'ssd_d': _jnp.float32, 'ssd_norm_w': _jnp.float32, 'ssd_w_out': _jnp.float32, 'final_norm_w': _jnp.float32}
MOMENT_SCALE = {'norm_w': 3.733485e-01, 'gdn_w_in': 1.520189e-01, 'gdn_conv_w': 1.484077e-01, 'gdn_a_log': 1.605752e+00, 'gdn_dt_bias': 1.539276e+00, 'gdn_norm_w': 7.605584e-01, 'gdn_w_out': 2.260703e-01, 'ssd_w_in': 1.426327e-01, 'ssd_conv_w': 1.228580e-01, 'ssd_conv_b': 1.772236e-01, 'ssd_dt_bias': 3.051806e-01, 'ssd_a_log': 5.986246e-01, 'ssd_d': 7.893546e-01, 'ssd_norm_w': 1.679426e-01, 'ssd_w_out': 2.318793e-01, 'final_norm_w': 1.280421e+02}


def _to_microbatches(a, axis):
    t = _jnp.moveaxis(a, axis, 0)
    t = t.reshape((N_MICROBATCH, t.shape[0] // N_MICROBATCH) + t.shape[1:])
    return _jnp.moveaxis(t, 1, axis + 1)


def setup_inputs(seed: int = 0) -> dict:
    inp = _fwd_setup_inputs(seed)
    key = _jax.random.fold_in(_jax.random.key(seed), 7919)
    shape, _ = _output_shape()
    out = dict(inp)
    out["loss_target"] = _jax.random.normal(_jax.random.fold_in(key, 0), shape, _jnp.float32)
    for i, name in enumerate(TWIN_WEIGHTS):
        w = inp[name].astype(_jnp.float32)
        if MOMENT_SCALE is None:
            s = _jnp.sqrt(_jnp.mean(_jnp.square(w)) + 1e-30)
        else:
            s = MOMENT_SCALE[name]
        km, kv = _jax.random.split(_jax.random.fold_in(key, i + 1))
        out[name] = w
        out["m_" + name] = s * _jax.random.normal(km, w.shape, _jnp.float32)
        out["v_" + name] = (s * s) * _jax.random.uniform(kv, w.shape, _jnp.float32, 0.5, 1.5)
    if N_MICROBATCH > 1:
        for name, axis in PER_EXAMPLE_BATCH_AXIS.items():
            out[name] = _to_microbatches(out[name], axis)
    return {'x': out['x'], 'norm_w': out['norm_w'], 'gdn_w_in': out['gdn_w_in'], 'gdn_conv_w': out['gdn_conv_w'], 'gdn_a_log': out['gdn_a_log'], 'gdn_dt_bias': out['gdn_dt_bias'], 'gdn_norm_w': out['gdn_norm_w'], 'gdn_w_out': out['gdn_w_out'], 'ssd_w_in': out['ssd_w_in'], 'ssd_conv_w': out['ssd_conv_w'], 'ssd_conv_b': out['ssd_conv_b'], 'ssd_dt_bias': out['ssd_dt_bias'], 'ssd_a_log': out['ssd_a_log'], 'ssd_d': out['ssd_d'], 'ssd_norm_w': out['ssd_norm_w'], 'ssd_w_out': out['ssd_w_out'], 'final_norm_w': out['final_norm_w'], 'loss_target': out['loss_target'], 'm_norm_w': out['m_norm_w'], 'm_gdn_w_in': out['m_gdn_w_in'], 'm_gdn_conv_w': out['m_gdn_conv_w'], 'm_gdn_a_log': out['m_gdn_a_log'], 'm_gdn_dt_bias': out['m_gdn_dt_bias'], 'm_gdn_norm_w': out['m_gdn_norm_w'], 'm_gdn_w_out': out['m_gdn_w_out'], 'm_ssd_w_in': out['m_ssd_w_in'], 'm_ssd_conv_w': out['m_ssd_conv_w'], 'm_ssd_conv_b': out['m_ssd_conv_b'], 'm_ssd_dt_bias': out['m_ssd_dt_bias'], 'm_ssd_a_log': out['m_ssd_a_log'], 'm_ssd_d': out['m_ssd_d'], 'm_ssd_norm_w': out['m_ssd_norm_w'], 'm_ssd_w_out': out['m_ssd_w_out'], 'm_final_norm_w': out['m_final_norm_w'], 'v_norm_w': out['v_norm_w'], 'v_gdn_w_in': out['v_gdn_w_in'], 'v_gdn_conv_w': out['v_gdn_conv_w'], 'v_gdn_a_log': out['v_gdn_a_log'], 'v_gdn_dt_bias': out['v_gdn_dt_bias'], 'v_gdn_norm_w': out['v_gdn_norm_w'], 'v_gdn_w_out': out['v_gdn_w_out'], 'v_ssd_w_in': out['v_ssd_w_in'], 'v_ssd_conv_w': out['v_ssd_conv_w'], 'v_ssd_conv_b': out['v_ssd_conv_b'], 'v_ssd_dt_bias': out['v_ssd_dt_bias'], 'v_ssd_a_log': out['v_ssd_a_log'], 'v_ssd_d': out['v_ssd_d'], 'v_ssd_norm_w': out['v_ssd_norm_w'], 'v_ssd_w_out': out['v_ssd_w_out'], 'v_final_norm_w': out['v_final_norm_w']}


def _loss(weights, diff, rest, loss_target):
    with _jax.named_scope("forward"):
        args = {**rest, TWIN_DIFF_INPUT: diff, **{k: w.astype(_WEIGHT_DTYPES[k]) for k, w in weights.items()}}
        y = _forward(args)
    with _jax.named_scope("loss_head"):
        err = _jnp.square(y.astype(_jnp.float32) - loss_target)
        return 0.5 * _jnp.sum(_jnp.mean(err, axis=-1)) if err.ndim else 0.5 * err


def _adamw(w, g, m, v):
    m = ADAM_B1 * m + (1.0 - ADAM_B1) * g
    v = ADAM_B2 * v + (1.0 - ADAM_B2) * _jnp.square(g)
    m_hat = m / (1.0 - ADAM_B1 ** ADAM_STEP)
    v_hat = v / (1.0 - ADAM_B2 ** ADAM_STEP)
    delta = -ADAM_LR * (m_hat / (_jnp.sqrt(v_hat) + ADAM_EPS) + ADAM_WD * w)
    return delta, m, v


def reference(x, norm_w, gdn_w_in, gdn_conv_w, gdn_a_log, gdn_dt_bias, gdn_norm_w, gdn_w_out, ssd_w_in, ssd_conv_w, ssd_conv_b, ssd_dt_bias, ssd_a_log, ssd_d, ssd_norm_w, ssd_w_out, final_norm_w, loss_target, m_norm_w, m_gdn_w_in, m_gdn_conv_w, m_gdn_a_log, m_gdn_dt_bias, m_gdn_norm_w, m_gdn_w_out, m_ssd_w_in, m_ssd_conv_w, m_ssd_conv_b, m_ssd_dt_bias, m_ssd_a_log, m_ssd_d, m_ssd_norm_w, m_ssd_w_out, m_final_norm_w, v_norm_w, v_gdn_w_in, v_gdn_conv_w, v_gdn_a_log, v_gdn_dt_bias, v_gdn_norm_w, v_gdn_w_out, v_ssd_w_in, v_ssd_conv_w, v_ssd_conv_b, v_ssd_dt_bias, v_ssd_a_log, v_ssd_d, v_ssd_norm_w, v_ssd_w_out, v_final_norm_w):
    given = dict(x=x, norm_w=norm_w, gdn_w_in=gdn_w_in, gdn_conv_w=gdn_conv_w, gdn_a_log=gdn_a_log, gdn_dt_bias=gdn_dt_bias, gdn_norm_w=gdn_norm_w, gdn_w_out=gdn_w_out, ssd_w_in=ssd_w_in, ssd_conv_w=ssd_conv_w, ssd_conv_b=ssd_conv_b, ssd_dt_bias=ssd_dt_bias, ssd_a_log=ssd_a_log, ssd_d=ssd_d, ssd_norm_w=ssd_norm_w, ssd_w_out=ssd_w_out, final_norm_w=final_norm_w, loss_target=loss_target, m_norm_w=m_norm_w, m_gdn_w_in=m_gdn_w_in, m_gdn_conv_w=m_gdn_conv_w, m_gdn_a_log=m_gdn_a_log, m_gdn_dt_bias=m_gdn_dt_bias, m_gdn_norm_w=m_gdn_norm_w, m_gdn_w_out=m_gdn_w_out, m_ssd_w_in=m_ssd_w_in, m_ssd_conv_w=m_ssd_conv_w, m_ssd_conv_b=m_ssd_conv_b, m_ssd_dt_bias=m_ssd_dt_bias, m_ssd_a_log=m_ssd_a_log, m_ssd_d=m_ssd_d, m_ssd_norm_w=m_ssd_norm_w, m_ssd_w_out=m_ssd_w_out, m_final_norm_w=m_final_norm_w, v_norm_w=v_norm_w, v_gdn_w_in=v_gdn_w_in, v_gdn_conv_w=v_gdn_conv_w, v_gdn_a_log=v_gdn_a_log, v_gdn_dt_bias=v_gdn_dt_bias, v_gdn_norm_w=v_gdn_norm_w, v_gdn_w_out=v_gdn_w_out, v_ssd_w_in=v_ssd_w_in, v_ssd_conv_w=v_ssd_conv_w, v_ssd_conv_b=v_ssd_conv_b, v_ssd_dt_bias=v_ssd_dt_bias, v_ssd_a_log=v_ssd_a_log, v_ssd_d=v_ssd_d, v_ssd_norm_w=v_ssd_norm_w, v_ssd_w_out=v_ssd_w_out, v_final_norm_w=v_final_norm_w)
    weights = {n: given[n] for n in TWIN_WEIGHTS}
    shared = {n: given[n] for n in SHARED_INPUTS}
    per_example = {n: given[n] for n in ['x']}
    grad_fn = _jax.value_and_grad(_loss, argnums=(0, 1))

    def one_microbatch(ex, loss_target):
        ex = dict(ex)
        diff = ex.pop(TWIN_DIFF_INPUT)
        return grad_fn(weights, diff, {**shared, **ex}, loss_target)

    if N_MICROBATCH == 1:
        loss, (grad_w, grad_x) = one_microbatch(per_example, given["loss_target"])
    else:
        def body(carry, xs):
            loss_sum, grad_sum = carry
            l_k, (gw_k, gx_k) = one_microbatch(xs[0], xs[1])
            with _jax.named_scope("update"):
                return (loss_sum + l_k, _jax.tree.map(_jnp.add, grad_sum, gw_k)), gx_k

        init = (_jnp.zeros((), _jnp.float32), _jax.tree.map(_jnp.zeros_like, weights))
        (loss, grad_w), grad_x = _jax.lax.scan(body, init, (per_example, given["loss_target"]))
    with _jax.named_scope("update"):
        delta_w, new_m, new_v = {}, {}, {}
        for n in TWIN_WEIGHTS:
            delta_w[n], new_m[n], new_v[n] = _adamw(weights[n], grad_w[n], given["m_" + n], given["v_" + n])
    return (loss, grad_x, *[grad_w[n] for n in TWIN_WEIGHTS], *[delta_w[n] for n in TWIN_WEIGHTS],
            *[new_m[n] for n in TWIN_WEIGHTS], *[new_v[n] for n in TWIN_WEIGHTS])
```

```python
import functools

import jax
import jax.numpy as jnp
from jax import lax
from jax.experimental import pallas as pl
from jax.experimental.pallas import tpu as pltpu

F32 = jnp.float32
BF16 = jnp.bfloat16
SDS = jax.ShapeDtypeStruct
MESH = pl.DeviceIdType.MESH

D = 1024
EPS = 1e-6
CONV_K = 4
N_CHIPS = 4
GH = 16
GHQ = 8
GDK = 128
GCH = 64
G_QK = 1024
G_V = 2048
G_CONV = 4096
G_MAIN = 6144
G_IN = 6176
SH = 32
SP = 64
SN = 128
SG = 8
SR = 4
SCH = 128
S_INNER = 2048
S_MAIN = 6144
S_IN = 6176
IN_SHARD = 1544
OUT_SHARD = 512
SMALL_W = 128

ADAM_LR = 0.001
ADAM_B1 = 0.9
ADAM_B2 = 0.999
ADAM_EPS = 1e-08
ADAM_WD = 0.01
ADAM_STEP = 10

VMEM_LIMIT = 48 * 1024 * 1024
NEG = -1e30


def _pcall(body, **kw):
    return pl.pallas_call(body, **kw)


def _cp(sem=None, vmem=VMEM_LIMIT):
    return pltpu.CompilerParams(dimension_semantics=sem, vmem_limit_bytes=vmem)


def _sigmoid(x):
    return 1.0 / (1.0 + jnp.exp(-x))


def _silu(x):
    return x * _sigmoid(x)


def _softplus(x):
    return jnp.maximum(x, 0.0) + jnp.log(1.0 + jnp.exp(-jnp.abs(x)))


def _rms(x, w):
    return x * lax.rsqrt(jnp.mean(x * x, axis=-1, keepdims=True) + EPS) * w


_DIMS = {"nn": (((2,), (1,)), ((0,), (0,))),
         "nt": (((2,), (2,)), ((0,), (0,))),
         "tn": (((1,), (1,)), ((0,), (0,)))}


def _bdot(a, b, spec):
    return lax.dot_general(a.astype(BF16), b.astype(BF16), _DIMS[spec], preferred_element_type=F32)


@functools.partial(jax.custom_vjp, nondiff_argnums=(2,))
def _bmm(a, b, spec):
    return _bdot(a, b, spec)


def _bmm_fwd(a, b, spec):
    return _bdot(a, b, spec), (a, b)


def _bmm_bwd(spec, res, g):
    a, b = res
    if spec == "nn":
        return _bdot(g, b, "nt"), _bdot(a, g, "tn")
    if spec == "nt":
        return _bdot(g, b, "nn"), _bdot(g, a, "tn")
    return _bdot(b, g, "nt"), _bdot(a, g, "nn")


_bmm.defvjp(_bmm_fwd, _bmm_bwd)


@jax.custom_vjp
def _tri_inv(n):
    t = -n
    p = n
    steps = (n.shape[-1] - 1).bit_length() - 1
    r = lax.broadcasted_iota(jnp.int32, n.shape, 1)
    c = lax.broadcasted_iota(jnp.int32, n.shape, 2)
    t = t + jnp.where(r == c, 1.0, 0.0)
    for _ in range(steps):
        p = _bdot(p, p, "nn")
        t = t + _bdot(t, p, "nn")
    return t


def _tri_inv_fwd(n):
    t = _tri_inv(n)
    return t, t


def _tri_inv_bwd(t, g):
    return (-_bdot(_bdot(t, g, "tn"), t, "nt"),)


_tri_inv.defvjp(_tri_inv_fwd, _tri_inv_bwd)


def _masks(c, lead=1):
    r = lax.broadcasted_iota(jnp.int32, (lead, c, c), 1)
    s = lax.broadcasted_iota(jnp.int32, (lead, c, c), 2)
    return r >= s, r > s, r == s, r <= s


def _row_to_col(row, eye):
    return jnp.sum(jnp.where(eye, row, 0.0), axis=2, keepdims=True)


def _gdn_chunk(q, k, v, braw, araw, alog, dtb, s):
    h = v.shape[0]
    c = v.shape[1]
    rep = h // q.shape[0]
    tril, strict, eye, triu = _masks(c)
    qq = jnp.broadcast_to(q[:, None], (q.shape[0], rep) + q.shape[1:]).reshape(v.shape)
    kk = jnp.broadcast_to(k[:, None], (k.shape[0], rep) + k.shape[1:]).reshape(v.shape)
    beta_row = _sigmoid(braw)
    g_row = -jnp.exp(alog) * _softplus(araw + dtb)
    beta_col = _row_to_col(beta_row, eye)
    g_col = _row_to_col(g_row, eye)
    gc_col = jnp.sum(jnp.where(tril, g_row, 0.0), axis=2, keepdims=True)
    gc_row = jnp.sum(jnp.where(triu, g_col, 0.0), axis=1, keepdims=True)
    gc_last = jnp.sum(g_row, axis=2, keepdims=True)
    lmat = jnp.exp(jnp.where(tril, gc_col - gc_row, NEG))
    kb = kk * beta_col
    vb = v * beta_col
    n = jnp.where(strict, _bmm(kb, kk, "nt") * lmat, 0.0)
    t = _tri_inv(n)
    e_col = jnp.exp(gc_col)
    u = _bmm(t, vb, "nn")
    w = _bmm(t, kb * e_col, "nn")
    attn = _bmm(qq, kk, "nt") * lmat
    q_dec = qq * e_col
    k_dec = kk * jnp.exp(gc_last - gc_col)
    v_new = u - _bmm(w, s, "nn")
    o = _bmm(q_dec, s, "nn") + _bmm(attn, v_new, "nn")
    s_new = s * jnp.exp(gc_last) + _bmm(k_dec, v_new, "tn")
    return o, s_new


def _ssd_chunk(xs, bm, cm, dtraw, dtb, alog, dskip, s):
    c = xs.shape[1]
    tril, _, eye, triu = _masks(c)
    lane = lax.broadcasted_iota(jnp.int32, (1, 1, SR * SP), 2)
    prow = lax.broadcasted_iota(jnp.int32, (1, SR * SP, 1), 1)
    cb = _bmm(cm, bm, "nt")
    cs = _bmm(cm, s, "nt")
    y = jnp.zeros_like(xs)
    e_full = jnp.zeros_like(xs)
    skip = jnp.zeros((xs.shape[0], 1, SR * SP), F32)
    dec_rows = jnp.zeros((xs.shape[0], SR * SP, 1), F32)
    states = jnp.zeros_like(s)
    for r in range(SR):
        m_lane = jnp.where((lane >= r * SP) & (lane < (r + 1) * SP), 1.0, 0.0)
        m_row = jnp.where((prow >= r * SP) & (prow < (r + 1) * SP), 1.0, 0.0)
        dt_row = _softplus(dtraw[r] + dtb[r])
        adt_row = -jnp.exp(alog[r]) * dt_row
        dt_col = _row_to_col(dt_row, eye)
        adt_col = _row_to_col(adt_row, eye)
        acs_col = jnp.sum(jnp.where(tril, adt_row, 0.0), axis=2, keepdims=True)
        acs_row = jnp.sum(jnp.where(triu, adt_col, 0.0), axis=1, keepdims=True)
        acs_last = jnp.sum(adt_row, axis=2, keepdims=True)
        lmat = jnp.exp(jnp.where(tril, acs_col - acs_row, NEG))
        x_r = xs * (dt_col * m_lane)
        y = y + _bmm(cb * lmat, x_r, "nn")
        states = states + _bmm(x_r, bm * jnp.exp(acs_last - acs_col), "tn")
        e_full = e_full + m_lane * jnp.exp(acs_col)
        dec_rows = dec_rows + m_row * jnp.exp(acs_last)
        skip = skip + m_lane * dskip[r]
    y = y + cs * e_full + xs * skip
    s_new = s * dec_rows + states
    return y, s_new


def _matmul(a, b, mode, name, out_dtype=F32, add=None, tm=512, tn=512, tk=512):
    if mode == "nn":
        (m, k), n = a.shape, b.shape[1]
    elif mode == "nt":
        (m, k), n = a.shape, b.shape[0]
    else:
        (k, m), n = a.shape, b.shape[1]
    tm, tn, tk = min(tm, m), min(tn, n), min(tk, k)
    assert m % tm == 0 and n % tn == 0 and k % tk == 0, (name, m, n, k)
    nk = k // tk
    dims = {"nn": (((1,), (0,)), ((), ())), "nt": (((1,), (1,)), ((), ())),
            "tn": (((0,), (0,)), ((), ()))}[mode]
    a_spec = {"nn": pl.BlockSpec((tm, tk), lambda i, j, kk: (i, kk)),
              "nt": pl.BlockSpec((tm, tk), lambda i, j, kk: (i, kk)),
              "tn": pl.BlockSpec((tk, tm), lambda i, j, kk: (kk, i))}[mode]
    b_spec = {"nn": pl.BlockSpec((tk, tn), lambda i, j, kk: (kk, j)),
              "nt": pl.BlockSpec((tn, tk), lambda i, j, kk: (j, kk)),
              "tn": pl.BlockSpec((tk, tn), lambda i, j, kk: (kk, j))}[mode]
    o_spec = pl.BlockSpec((tm, tn), lambda i, j, kk: (i, j))
    has_add = add is not None

    def body(*refs):
        if has_add:
            a_ref, b_ref, add_ref, o_ref, acc_ref = refs
        else:
            a_ref, b_ref, o_ref, acc_ref = refs
        kk = pl.program_id(2)

        @pl.when(kk == 0)
        def _():
            acc_ref[...] = jnp.zeros_like(acc_ref)

        acc_ref[...] += lax.dot_general(a_ref[...].astype(BF16), b_ref[...].astype(BF16), dims,
                                        preferred_element_type=F32)

        @pl.when(kk == nk - 1)
        def _():
            r = acc_ref[...]
            if has_add:
                r = r + add_ref[...].astype(F32)
            o_ref[...] = r.astype(o_ref.dtype)

    ins = [a, b] + ([add] if has_add else [])
    in_specs = [a_spec, b_spec] + ([o_spec] if has_add else [])
    return _pcall(body, grid=(m // tm, n // tn, nk), in_specs=in_specs, out_specs=o_spec,
                  out_shape=SDS((m, n), out_dtype), scratch_shapes=[pltpu.VMEM((tm, tn), F32)],
                  compiler_params=_cp(("parallel", "parallel", "arbitrary")), name=name)(*ins)


def _ew(fn, ins, in_specs, out_shape, out_specs, grid, name):
    n_in = len(ins)

    def body(*refs):
        outs = fn(*[r[...] for r in refs[:n_in]])
        for r, o in zip(refs[n_in:], outs):
            r[...] = o.astype(r.dtype)

    return _pcall(body, grid=grid, in_specs=in_specs, out_specs=out_specs, out_shape=out_shape,
                  compiler_params=_cp(("arbitrary",) * len(grid)), name=name)(*ins)


def _ew_vjp(fn, ins, in_specs, cts, ct_specs, wrt, g_shape, g_specs, acc, grid, name):
    n_in, n_ct = len(ins), len(cts)

    def body(*refs):
        vals = [r[...].astype(F32) for r in refs[:n_in]]
        outs, vjp = jax.vjp(fn, *vals)
        g_all = vjp(tuple(r[...].astype(F32) for r in refs[n_in:n_in + n_ct]))
        for pos, (i, g_ref) in enumerate(zip(wrt, refs[n_in + n_ct:])):
            g = g_all[i]
            if pos in acc:
                first = functools.reduce(
                    jnp.logical_and, [pl.program_id(ax) == 0 for ax in range(acc[pos], len(grid))])

                @pl.when(first)
                def _():
                    g_ref[...] = jnp.zeros_like(g_ref)

                g_ref[...] += g.astype(g_ref.dtype)
            else:
                g_ref[...] = g.astype(g_ref.dtype)

    return _pcall(body, grid=grid, in_specs=list(in_specs) + list(ct_specs), out_specs=g_specs,
                  out_shape=g_shape, compiler_params=_cp(("arbitrary",) * len(grid)),
                  name=name)(*ins, *cts)


def _row_spec(tm, n):
    return pl.BlockSpec((tm, n), lambda i: (i, 0))


def _par_spec(n):
    return pl.BlockSpec((1, n), lambda i: (0, 0))


def _rms_fwd(x, w, name):
    t = x.shape[0]
    tm = min(t, 512)
    return _ew(lambda xv, wv: (_rms(xv, wv),), [x, w], [_row_spec(tm, D), _par_spec(D)],
               [SDS((t, D), BF16)], [_row_spec(tm, D)], (t // tm,), name)[0]


def _rms_bwd(x, w, dres, dhid, name):
    t = x.shape[0]
    tm = min(t, 512)
    return _ew_vjp(lambda xv, wv: (xv, _rms(xv, wv)), [x, w], [_row_spec(tm, D), _par_spec(D)],
                   [dres, dhid], [_row_spec(tm, D), _row_spec(tm, D)], (0, 1),
                   [SDS((t, D), F32), SDS((1, D), F32)], [_row_spec(tm, D), _par_spec(D)],
                   {1: 0}, (t // tm,), name)


def _final_loss(x, w, tgt, name):
    t = x.shape[0]
    tm = min(t, 512)

    def body(x_ref, w_ref, t_ref, dx_ref, dw_ref, l_ref):
        @pl.when(pl.program_id(0) == 0)
        def _():
            dw_ref[...] = jnp.zeros_like(dw_ref)
            l_ref[...] = jnp.zeros_like(l_ref)

        xv, wv = x_ref[...], w_ref[...]
        rstd = lax.rsqrt(jnp.mean(xv * xv, axis=-1, keepdims=True) + EPS)
        xh = xv * rstd
        err = xh * wv - t_ref[...]
        l_ref[...] += 0.5 * jnp.sum(jnp.mean(err * err, axis=-1, keepdims=True), axis=0, keepdims=True)
        dy = err * (1.0 / D)
        dw_ref[...] += jnp.sum(dy * xh, axis=0, keepdims=True)
        dxh = dy * wv
        dx_ref[...] = rstd * (dxh - xh * jnp.mean(dxh * xh, axis=-1, keepdims=True))

    return _pcall(body, grid=(t // tm,), in_specs=[_row_spec(tm, D), _par_spec(D), _row_spec(tm, D)],
                  out_specs=[_row_spec(tm, D), _par_spec(D), _par_spec(128)],
                  out_shape=[SDS((t, D), F32), SDS((1, D), F32), SDS((1, 128), F32)],
                  compiler_params=_cp(("arbitrary",)), name=name)(x, w, tgt)


def _conv_taps(ext, w_ref, tm, lo):
    n = ext.shape[0]
    acc = None
    for j in range(CONV_K):
        shift = (CONV_K - 1 - j) if lo else (n - (CONV_K - 1 - j)) % n
        rolled = pltpu.roll(ext, shift, 0) if shift else ext
        term = w_ref[pl.ds(j, 1), :] * rolled[lo:lo + tm]
        acc = term if acc is None else acc + term
    return acc


def _conv_pre_specs(tm, ct, col0):
    hb = tm // 8
    return [pl.BlockSpec((tm, ct), lambda j, i: (i, col0 + j)),
            pl.BlockSpec((8, ct), lambda j, i: (jnp.maximum(i * hb - 1, 0), col0 + j)),
            pl.BlockSpec((CONV_K, ct), lambda j, i: (0, j)),
            pl.BlockSpec((1, ct), lambda j, i: (0, j))]


def _conv_pre_value(x_ref, xh_ref, w_ref, b_ref, tm):
    halo = jnp.where(pl.program_id(1) > 0, xh_ref[...], 0.0)
    ext = jnp.concatenate([halo, x_ref[...]], axis=0)
    return _conv_taps(ext, w_ref, tm, 8) + b_ref[...], ext


def _conv_fwd(x, col0, w, b, ct, post, name):
    t = x.shape[0]
    tm = min(t, 512)
    nt = w.shape[1] // ct

    def body(x_ref, xh_ref, w_ref, b_ref, o_ref):
        pre, _ = _conv_pre_value(x_ref, xh_ref, w_ref, b_ref, tm)
        o_ref[0] = post(pre)

    return _pcall(body, grid=(nt, t // tm), in_specs=_conv_pre_specs(tm, ct, col0 // ct),
                  out_specs=pl.BlockSpec((1, tm, ct), lambda j, i: (j, i, 0)),
                  out_shape=SDS((nt, t, ct), F32), compiler_params=_cp(("arbitrary", "arbitrary")),
                  name=name)(x, x, w, b)


def _conv_dpre(x, col0, w, b, ct, post, dout, name):
    t = x.shape[0]
    tm = min(t, 512)
    nt = w.shape[1] // ct

    def body(x_ref, xh_ref, w_ref, b_ref, do_ref, dp_ref, dw_ref, db_ref):
        pre, ext = _conv_pre_value(x_ref, xh_ref, w_ref, b_ref, tm)
        _, vjp = jax.vjp(post, pre)
        dpre = vjp(do_ref[0])[0]
        dp_ref[...] = dpre

        @pl.when(pl.program_id(1) == 0)
        def _():
            dw_ref[...] = jnp.zeros_like(dw_ref)
            db_ref[...] = jnp.zeros_like(db_ref)

        for j in range(CONV_K):
            xs = (pltpu.roll(ext, CONV_K - 1 - j, 0) if j < CONV_K - 1 else ext)[8:8 + tm]
            dw_ref[pl.ds(j, 1), :] += jnp.sum(dpre * xs, axis=0, keepdims=True)
        db_ref[...] += jnp.sum(dpre, axis=0, keepdims=True)

    c = w.shape[1]
    return _pcall(body, grid=(nt, t // tm),
                  in_specs=_conv_pre_specs(tm, ct, col0 // ct)
                  + [pl.BlockSpec((1, tm, ct), lambda j, i: (j, i, 0))],
                  out_specs=[pl.BlockSpec((tm, ct), lambda j, i: (i, j)),
                             pl.BlockSpec((CONV_K, ct), lambda j, i: (0, j)),
                             pl.BlockSpec((1, ct), lambda j, i: (0, j))],
                  out_shape=[SDS((t, c), F32), SDS((CONV_K, c), F32), SDS((1, c), F32)],
                  compiler_params=_cp(("arbitrary", "arbitrary")), name=name)(x, x, w, b, dout)


def _conv_t(dpre, w, name):
    t, c = dpre.shape
    tm = min(t, 512)
    ct = min(c, 512)
    hb = tm // 8
    last = t // tm - 1

    def body(d_ref, dh_ref, w_ref, o_ref):
        halo = jnp.where(pl.program_id(1) < last, dh_ref[...], 0.0)
        ext = jnp.concatenate([d_ref[...], halo], axis=0)
        o_ref[...] = _conv_taps(ext, w_ref, tm, 0).astype(o_ref.dtype)

    return _pcall(body, grid=(c // ct, t // tm),
                  in_specs=[pl.BlockSpec((tm, ct), lambda j, i: (i, j)),
                            pl.BlockSpec((8, ct), lambda j, i: (jnp.minimum((i + 1) * hb, t // 8 - 1), j)),
                            pl.BlockSpec((CONV_K, ct), lambda j, i: (0, j))],
                  out_specs=pl.BlockSpec((tm, ct), lambda j, i: (i, j)),
                  out_shape=SDS((t, c), BF16), compiler_params=_cp(("arbitrary", "arbitrary")),
                  name=name)(dpre, dpre, w)


def _l2norm_scaled(scale):
    def post(pre):
        a = _silu(pre)
        return a * lax.rsqrt(jnp.sum(a * a, axis=-1, keepdims=True) + EPS) * scale
    return post


GDN_HB = 8
GDN_CPB = 2
SSD_GB = 2
SSD_CPB = 1


def _gdn_specs(nb, rev):
    hb, cpb, tc = GDN_HB, GDN_CPB, GDN_CPB * GCH
    blk = (lambda n: nb - 1 - n) if rev else (lambda n: n)
    seq = lambda h: pl.BlockSpec((h, tc, GDK), lambda g, n: (g, blk(n), 0))
    gate = pl.BlockSpec((1, hb, cpb, GCH), lambda g, n: (blk(n), g, 0, 0))
    par = pl.BlockSpec((hb, 1, 1), lambda g, n: (g, 0, 0))
    state = pl.BlockSpec((cpb, hb, GDK, GDK), lambda g, n: (blk(n), g, 0, 0))
    return seq, gate, par, state


def _gdn_fwd(q, k, v, braw, araw, alog, dtb):
    t = v.shape[1]
    hb, cpb = GDN_HB, GDN_CPB
    nb = t // (cpb * GCH)
    seq, gate, par, state = _gdn_specs(nb, False)

    def body(q_ref, k_ref, v_ref, b_ref, a_ref, al_ref, dt_ref, o_ref, sall_ref, s_ref):
        @pl.when(pl.program_id(1) == 0)
        def _():
            s_ref[...] = jnp.zeros_like(s_ref)

        s = s_ref[...]
        for c in range(cpb):
            rows = pl.ds(c * GCH, GCH)
            sall_ref[c] = s
            o, s = _gdn_chunk(q_ref[:, rows, :], k_ref[:, rows, :], v_ref[:, rows, :],
                              b_ref[0, :, pl.ds(c, 1), :], a_ref[0, :, pl.ds(c, 1), :],
                              al_ref[...], dt_ref[...], s)
            o_ref[:, rows, :] = o
        s_ref[...] = s

    return _pcall(body, grid=(GH // hb, nb),
                  in_specs=[seq(hb // 2), seq(hb // 2), seq(hb), gate, gate, par, par],
                  out_specs=[seq(hb), state],
                  out_shape=[SDS((GH, t, GDK), F32), SDS((t // GCH, GH, GDK, GDK), F32)],
                  scratch_shapes=[pltpu.VMEM((hb, GDK, GDK), F32)],
                  compiler_params=_cp(("arbitrary", "arbitrary")), name="gdn_chunk_fwd")(
                      q, k, v, braw, araw, alog, dtb)


def _gdn_bwd(q, k, v, braw, araw, alog, dtb, sall, do):
    t = v.shape[1]
    hb, cpb = GDN_HB, GDN_CPB
    nb = t // (cpb * GCH)
    seq, gate, par, state = _gdn_specs(nb, True)

    def body(q_ref, k_ref, v_ref, b_ref, a_ref, al_ref, dt_ref, sall_ref, do_ref,
             dq_ref, dk_ref, dv_ref, db_ref, da_ref, dal_ref, ddt_ref, ds_ref):
        @pl.when(pl.program_id(1) == 0)
        def _():
            ds_ref[...] = jnp.zeros_like(ds_ref)
            dal_ref[...] = jnp.zeros_like(dal_ref)
            ddt_ref[...] = jnp.zeros_like(ddt_ref)

        ds = ds_ref[...]
        for c in reversed(range(cpb)):
            rows = pl.ds(c * GCH, GCH)
            _, vjp = jax.vjp(_gdn_chunk, q_ref[:, rows, :], k_ref[:, rows, :], v_ref[:, rows, :],
                             b_ref[0, :, pl.ds(c, 1), :], a_ref[0, :, pl.ds(c, 1), :],
                             al_ref[...], dt_ref[...], sall_ref[c])
            dq, dk, dv, db, da, dal, ddt, ds = vjp((do_ref[:, rows, :], ds))
            dq_ref[:, rows, :] = dq
            dk_ref[:, rows, :] = dk
            dv_ref[:, rows, :] = dv
            db_ref[0, :, pl.ds(c, 1), :] = db
            da_ref[0, :, pl.ds(c, 1), :] = da
            dal_ref[...] += dal
            ddt_ref[...] += ddt
        ds_ref[...] = ds

    return _pcall(body, grid=(GH // hb, nb),
                  in_specs=[seq(hb // 2), seq(hb // 2), seq(hb), gate, gate, par, par, state, seq(hb)],
                  out_specs=[seq(hb // 2), seq(hb // 2), seq(hb), gate, gate, par, par],
                  out_shape=[SDS(q.shape, F32), SDS(k.shape, F32), SDS(v.shape, F32),
                             SDS(braw.shape, F32), SDS(araw.shape, F32),
                             SDS((GH, 1, 1), F32), SDS((GH, 1, 1), F32)],
                  scratch_shapes=[pltpu.VMEM((hb, GDK, GDK), F32)],
                  compiler_params=_cp(("arbitrary", "arbitrary")), name="gdn_chunk_bwd")(
                      q, k, v, braw, araw, alog, dtb, sall, do)


def _ssd_specs(nb, rev):
    gb, cpb, tc = SSD_GB, SSD_CPB, SSD_CPB * SCH
    blk = (lambda n: nb - 1 - n) if rev else (lambda n: n)
    seq = lambda w: pl.BlockSpec((gb, tc, w), lambda g, n: (g, blk(n), 0))
    gate = pl.BlockSpec((1, cpb * SR, gb, 1, SCH), lambda g, n: (blk(n), 0, g, 0, 0))
    par = pl.BlockSpec((SR, gb, 1, 1), lambda g, n: (0, g, 0, 0))
    state = pl.BlockSpec((cpb, gb, SR * SP, SN), lambda g, n: (blk(n), g, 0, 0))
    return seq, gate, par, state


def _ssd_fwd(xs, bm, cm, dtraw, dtb, alog, dskip):
    t = xs.shape[1]
    gb, cpb = SSD_GB, SSD_CPB
    nb = t // (cpb * SCH)
    seq, gate, par, state = _ssd_specs(nb, False)

    def body(x_ref, b_ref, c_ref, dt_ref, dtb_ref, al_ref, dk_ref, y_ref, sall_ref, s_ref):
        @pl.when(pl.program_id(1) == 0)
        def _():
            s_ref[...] = jnp.zeros_like(s_ref)

        s = s_ref[...]
        for c in range(cpb):
            rows = pl.ds(c * SCH, SCH)
            sall_ref[c] = s
            y, s = _ssd_chunk(x_ref[:, rows, :], b_ref[:, rows, :], c_ref[:, rows, :],
                              dt_ref[0, pl.ds(c * SR, SR)], dtb_ref[...], al_ref[...],
                              dk_ref[...], s)
            y_ref[:, rows, :] = y
        s_ref[...] = s

    return _pcall(body, grid=(SG // gb, nb),
                  in_specs=[seq(SR * SP), seq(SN), seq(SN), gate, par, par, par],
                  out_specs=[seq(SR * SP), state],
                  out_shape=[SDS((SG, t, SR * SP), F32), SDS((t // SCH, SG, SR * SP, SN), F32)],
                  scratch_shapes=[pltpu.VMEM((gb, SR * SP, SN), F32)],
                  compiler_params=_cp(("arbitrary", "arbitrary")), name="ssd_chunk_fwd")(
                      xs, bm, cm, dtraw, dtb, alog, dskip)


def _ssd_bwd(xs, bm, cm, dtraw, dtb, alog, dskip, sall, dy):
    t = xs.shape[1]
    gb, cpb = SSD_GB, SSD_CPB
    nb = t // (cpb * SCH)
    seq, gate, par, state = _ssd_specs(nb, True)

    def body(x_ref, b_ref, c_ref, dt_ref, dtb_ref, al_ref, dk_ref, sall_ref, dy_ref,
             dx_ref, dbm_ref, dcm_ref, ddt_ref, ddtb_ref, dal_ref, ddk_ref, ds_ref):
        @pl.when(pl.program_id(1) == 0)
        def _():
            ds_ref[...] = jnp.zeros_like(ds_ref)
            ddtb_ref[...] = jnp.zeros_like(ddtb_ref)
            dal_ref[...] = jnp.zeros_like(dal_ref)
            ddk_ref[...] = jnp.zeros_like(ddk_ref)

        ds = ds_ref[...]
        for c in reversed(range(cpb)):
            rows = pl.ds(c * SCH, SCH)
            _, vjp = jax.vjp(_ssd_chunk, x_ref[:, rows, :], b_ref[:, rows, :], c_ref[:, rows, :],
                             dt_ref[0, pl.ds(c * SR, SR)], dtb_ref[...], al_ref[...],
                             dk_ref[...], sall_ref[c])
            dx, dbm, dcm, ddt, ddtb, dal, ddk, ds = vjp((dy_ref[:, rows, :], ds))
            dx_ref[:, rows, :] = dx
            dbm_ref[:, rows, :] = dbm
            dcm_ref[:, rows, :] = dcm
            ddt_ref[0, pl.ds(c * SR, SR)] = ddt
            ddtb_ref[...] += ddtb
            dal_ref[...] += dal
            ddk_ref[...] += ddk
        ds_ref[...] = ds

    return _pcall(body, grid=(SG // gb, nb),
                  in_specs=[seq(SR * SP), seq(SN), seq(SN), gate, par, par, par, state, seq(SR * SP)],
                  out_specs=[seq(SR * SP), seq(SN), seq(SN), gate, par, par, par],
                  out_shape=[SDS(xs.shape, F32), SDS(bm.shape, F32), SDS(cm.shape, F32),
                             SDS(dtraw.shape, F32), SDS((SR, SG, 1, 1), F32), SDS((SR, SG, 1, 1), F32),
                             SDS((SR, SG, 1, 1), F32)],
                  scratch_shapes=[pltpu.VMEM((gb, SR * SP, SN), F32)],
                  compiler_params=_cp(("arbitrary", "arbitrary")), name="ssd_chunk_bwd")(
                      xs, bm, cm, dtraw, dtb, alog, dskip, sall, dy)


def _gate_specs(tm, ct, zcol0, per_tile_w):
    z0 = zcol0 // ct
    return [pl.BlockSpec((1, tm, ct), lambda i, j: (j, i, 0)),
            pl.BlockSpec((tm, ct), lambda i, j: (i, z0 + j)),
            pl.BlockSpec((1, ct), (lambda i, j: (0, j)) if per_tile_w else (lambda i, j: (0, 0)))]


def _gdn_gate(o, z, w):
    return (_rms(o[0], w) * _silu(z),)


def _ssd_gate(y, z, w):
    return (_rms(y[0] * _silu(z), w),)


def _gate_fwd(fn, o, proj, zcol0, w, name):
    nt, t, ct = o.shape
    tm = min(t, 512)
    return _ew(fn, [o, proj, w], _gate_specs(tm, ct, zcol0, w.shape[1] > ct), [SDS((t, nt * ct), BF16)],
               [pl.BlockSpec((tm, ct), lambda i, j: (i, j))], (t // tm, nt), name)[0]


def _gate_bwd(fn, o, proj, zcol0, w, dy, wacc, name):
    nt, t, ct = o.shape
    tm = min(t, 512)
    specs = _gate_specs(tm, ct, zcol0, wacc)
    out_spec = pl.BlockSpec((tm, ct), lambda i, j: (i, j))
    if wacc:
        flip = lambda s: pl.BlockSpec(s.block_shape, lambda j, i, f=s.index_map: f(i, j))
        specs = [flip(s) for s in specs]
        out_spec = flip(out_spec)
        grid, acc = (nt, t // tm), {2: 1}
    else:
        grid, acc = (t // tm, nt), {2: 0}
    return _ew_vjp(fn, [o, proj, w], specs, [dy], [out_spec], (0, 1, 2),
                   [SDS(o.shape, F32), SDS((t, nt * ct), BF16), SDS(w.shape, F32)],
                   [specs[0], out_spec, specs[2]], acc, grid, name)


def _adamw_math(w, g, m, v):
    m = ADAM_B1 * m + (1.0 - ADAM_B1) * g
    v = ADAM_B2 * v + (1.0 - ADAM_B2) * jnp.square(g)
    m_hat = m / (1.0 - ADAM_B1 ** ADAM_STEP)
    v_hat = v / (1.0 - ADAM_B2 ** ADAM_STEP)
    delta = -ADAM_LR * (m_hat / (jnp.sqrt(v_hat) + ADAM_EPS) + ADAM_WD * w)
    return delta, m, v


def _adamw(w, g, m, v, name):
    shape = w.shape
    w2, g2, m2, v2 = [a.reshape(-1, shape[-1]) for a in (w, g, m, v)]
    r, c = w2.shape
    tr = 256 if r % 256 == 0 else r
    spec = pl.BlockSpec((tr, c), lambda i: (i, 0))
    outs = _ew(_adamw_math, [w2, g2, m2, v2], [spec] * 4, [SDS((r, c), F32)] * 3, [spec] * 3,
               (r // tr,), name)
    return [o.reshape(shape) for o in outs]


def _coords():
    return lax.axis_index("x"), lax.axis_index("y"), lax.axis_index("c")


def _other_chips(x, y):
    return [(1 - x, y), (x, 1 - y), (1 - x, 1 - y)]


ANY = pl.BlockSpec(memory_space=pl.ANY)


def _gather_weights(big, small):
    r = big.shape[0]
    half = r // 2

    def body(big_ref, small_ref, obig_ref, osmall_ref, send_sems, recv_sems, local_sems):
        x, y, c = _coords()
        me = 2 * x + y
        sibling = (x, y, 1 - c)
        chips = _other_chips(x, y)
        mine = pl.ds(c * half, half)
        theirs = pl.ds((1 - c) * half, half)

        loc_b = pltpu.make_async_copy(big_ref, obig_ref.at[me], local_sems.at[0])
        loc_s = pltpu.make_async_copy(small_ref, osmall_ref.at[me], local_sems.at[1])
        loc_b.start()
        loc_s.start()

        def rcopy(src, dst, k, to):
            return pltpu.make_async_remote_copy(src_ref=src, dst_ref=dst, send_sem=send_sems.at[k],
                                                recv_sem=recv_sems.at[k], device_id=to,
                                                device_id_type=MESH)

        first = []
        for j, (px, py) in enumerate(chips):
            first.append(rcopy(big_ref.at[mine], obig_ref.at[me, mine], j, (px, py, c)))
            first.append(rcopy(small_ref, osmall_ref.at[me], 6 + j, (px, py, c)))
        for cp in first:
            cp.start()
        passed = []
        for j, (px, py) in enumerate(chips):
            src = 2 * px + py
            rcopy(big_ref.at[mine], obig_ref.at[src, mine], j, (px, py, c)).wait_recv()
            fw = rcopy(obig_ref.at[src, mine], obig_ref.at[src, mine], 3 + j, sibling)
            fw.start()
            passed.append(fw)
        for j, (px, py) in enumerate(chips):
            src = 2 * px + py
            rcopy(big_ref.at[theirs], obig_ref.at[src, theirs], 3 + j, sibling).wait_recv()
            rcopy(small_ref, osmall_ref.at[src], 6 + j, (px, py, c)).wait_recv()
        for cp in first + passed:
            cp.wait_send()
        loc_b.wait()
        loc_s.wait()

    return _pcall(body, in_specs=[ANY, ANY], out_specs=[ANY, ANY],
                  out_shape=[SDS((N_CHIPS,) + big.shape, big.dtype),
                             SDS((N_CHIPS,) + small.shape, small.dtype)],
                  scratch_shapes=[pltpu.SemaphoreType.DMA((9,)), pltpu.SemaphoreType.DMA((9,)),
                                  pltpu.SemaphoreType.DMA((2,))],
                  name="gather_weights")(big, small)


def _swap_sibling(a, name):
    def body(a_ref, o_ref, send_sem, recv_sem):
        x, y, c = _coords()
        cp = pltpu.make_async_remote_copy(src_ref=a_ref, dst_ref=o_ref, send_sem=send_sem,
                                          recv_sem=recv_sem, device_id=(x, y, 1 - c),
                                          device_id_type=MESH)
        cp.start()
        cp.wait()

    return _pcall(body, in_specs=[ANY], out_specs=ANY, out_shape=SDS(a.shape, a.dtype),
                  scratch_shapes=[pltpu.SemaphoreType.DMA, pltpu.SemaphoreType.DMA], name=name)(a)


def _scatter_chips(h):
    def body(h_ref, o_ref, send_sems, recv_sems):
        x, y, c = _coords()
        cps = []
        for j, (px, py) in enumerate(_other_chips(x, y)):
            cps.append(pltpu.make_async_remote_copy(
                src_ref=h_ref.at[2 * px + py], dst_ref=o_ref.at[j], send_sem=send_sems.at[j],
                recv_sem=recv_sems.at[j], device_id=(px, py, c), device_id_type=MESH))
        for cp in cps:
            cp.start()
        for cp in cps:
            cp.wait()

    return _pcall(body, in_specs=[ANY], out_specs=ANY, out_shape=SDS((3,) + h.shape[1:], h.dtype),
                  scratch_shapes=[pltpu.SemaphoreType.DMA((3,)), pltpu.SemaphoreType.DMA((3,))],
                  name="scatter_chips")(h)


def _add_n(arrs, name):
    n, c = arrs[0].shape
    tr = 2056 if n % 2056 == 0 else n
    spec = pl.BlockSpec((tr, c), lambda i: (i, 0))

    def fn(*vals):
        acc = vals[0]
        for v in vals[1:]:
            acc = acc + v
        return (acc,)

    return _ew(fn, list(arrs), [spec] * len(arrs), [SDS((n, c), F32)], [spec], (n // tr,), name)[0]


def _allreduce_small(buf):
    rows = buf.shape[0]

    def body(b_ref, o_ref, g_ref, send_sems, recv_sems):
        x, y, c = _coords()
        me = 4 * x + 2 * y + c
        g_ref[me] = b_ref[...]
        cps = []
        for k in range(1, 8):
            px = 1 - x if k & 4 else x
            py = 1 - y if k & 2 else y
            pc = 1 - c if k & 1 else c
            cps.append(pltpu.make_async_remote_copy(
                src_ref=b_ref, dst_ref=g_ref.at[me], send_sem=send_sems.at[k - 1],
                recv_sem=recv_sems.at[k - 1], device_id=(px, py, pc), device_id_type=MESH))
        for cp in cps:
            cp.start()
        for cp in cps:
            cp.wait()
        acc = g_ref[0]
        for d in range(1, 8):
            acc = acc + g_ref[d]
        o_ref[...] = acc

    vm = pl.BlockSpec(memory_space=pltpu.VMEM)
    return _pcall(body, in_specs=[vm], out_specs=vm, out_shape=SDS(buf.shape, F32),
                  scratch_shapes=[pltpu.VMEM((8, rows, 128), F32), pltpu.SemaphoreType.DMA((7,)),
                                  pltpu.SemaphoreType.DMA((7,))],
                  compiler_params=pltpu.CompilerParams(vmem_limit_bytes=VMEM_LIMIT),
                  name="allreduce_small")(buf)


def _flat128(a):
    return a.reshape(-1, 128)


def _pad_rows(a, mult):
    r = (-a.shape[0]) % mult
    return a if r == 0 else jnp.concatenate([a, jnp.zeros((r, a.shape[1]), a.dtype)], axis=0)


def _pack_flat(parts, mult):
    rows, offs, r0 = [], [], 0
    for p in parts:
        f = p.reshape(-1)
        pad = (-f.shape[0]) % 128
        if pad:
            f = jnp.concatenate([f, jnp.zeros((pad,), f.dtype)])
        f = f.reshape(-1, 128)
        rows.append(f)
        offs.append((r0, p.shape))
        r0 += f.shape[0]
    return _pad_rows(jnp.concatenate(rows, axis=0), mult), offs


def _unpack_flat(buf, offs):
    out = []
    for r0, shape in offs:
        n = 1
        for s in shape:
            n *= s
        nr = -(-n // 128)
        out.append(buf[r0:r0 + nr].reshape(-1)[:n].reshape(shape))
    return out


def _gates_to_rows(a, heads, chunk, cpb):
    t = a.shape[0]
    return a.reshape(t // (chunk * cpb), cpb, chunk, heads).transpose(0, 3, 1, 2)


def _rows_to_gates(a):
    nb, heads, cpb, chunk = a.shape
    return a.transpose(0, 2, 3, 1).reshape(nb * cpb * chunk, heads)


def kernel(x, norm_w, gdn_w_in, gdn_conv_w, gdn_a_log, gdn_dt_bias, gdn_norm_w, gdn_w_out, ssd_w_in, ssd_conv_w, ssd_conv_b, ssd_dt_bias, ssd_a_log, ssd_d, ssd_norm_w, ssd_w_out, final_norm_w, loss_target, m_norm_w, m_gdn_w_in, m_gdn_conv_w, m_gdn_a_log, m_gdn_dt_bias, m_gdn_norm_w, m_gdn_w_out, m_ssd_w_in, m_ssd_conv_w, m_ssd_conv_b, m_ssd_dt_bias, m_ssd_a_log, m_ssd_d, m_ssd_norm_w, m_ssd_w_out, m_final_norm_w, v_norm_w, v_gdn_w_in, v_gdn_conv_w, v_gdn_a_log, v_gdn_dt_bias, v_gdn_norm_w, v_gdn_w_out, v_ssd_w_in, v_ssd_conv_w, v_ssd_conv_b, v_ssd_dt_bias, v_ssd_a_log, v_ssd_d, v_ssd_norm_w, v_ssd_w_out, v_final_norm_w):
    ws = dict(norm_w=norm_w, gdn_w_in=gdn_w_in, gdn_conv_w=gdn_conv_w, gdn_a_log=gdn_a_log,
              gdn_dt_bias=gdn_dt_bias, gdn_norm_w=gdn_norm_w, gdn_w_out=gdn_w_out, ssd_w_in=ssd_w_in,
              ssd_conv_w=ssd_conv_w, ssd_conv_b=ssd_conv_b, ssd_dt_bias=ssd_dt_bias,
              ssd_a_log=ssd_a_log, ssd_d=ssd_d, ssd_norm_w=ssd_norm_w, ssd_w_out=ssd_w_out,
              final_norm_w=final_norm_w)
    ms = dict(norm_w=m_norm_w, gdn_w_in=m_gdn_w_in, gdn_conv_w=m_gdn_conv_w, gdn_a_log=m_gdn_a_log,
              gdn_dt_bias=m_gdn_dt_bias, gdn_norm_w=m_gdn_norm_w, gdn_w_out=m_gdn_w_out,
              ssd_w_in=m_ssd_w_in, ssd_conv_w=m_ssd_conv_w, ssd_conv_b=m_ssd_conv_b,
              ssd_dt_bias=m_ssd_dt_bias, ssd_a_log=m_ssd_a_log, ssd_d=m_ssd_d,
              ssd_norm_w=m_ssd_norm_w, ssd_w_out=m_ssd_w_out, final_norm_w=m_final_norm_w)
    vs = dict(norm_w=v_norm_w, gdn_w_in=v_gdn_w_in, gdn_conv_w=v_gdn_conv_w, gdn_a_log=v_gdn_a_log,
              gdn_dt_bias=v_gdn_dt_bias, gdn_norm_w=v_gdn_norm_w, gdn_w_out=v_gdn_w_out,
              ssd_w_in=v_ssd_w_in, ssd_conv_w=v_ssd_conv_w, ssd_conv_b=v_ssd_conv_b,
              ssd_dt_bias=v_ssd_dt_bias, ssd_a_log=v_ssd_a_log, ssd_d=v_ssd_d,
              ssd_norm_w=v_ssd_norm_w, ssd_w_out=v_ssd_w_out, final_norm_w=v_final_norm_w)
    names = list(ws)
    cx, cy, cc = _coords()
    chip = 2 * cx + cy
    t = x.shape[1]
    x0 = x.reshape(t, D)
    tgt = loss_target.reshape(t, D)

    big_parts = [gdn_w_in[0], gdn_w_out[0], ssd_w_in[0], ssd_w_out[0]]
    big = jnp.concatenate([_flat128(p.astype(BF16)) for p in big_parts], axis=0)
    small, small_offs = _pack_flat([gdn_conv_w[0], ssd_conv_w[0], ssd_conv_b[0], ssd_norm_w[0]], 8)
    gbig, gsmall = _gather_weights(big, small)
    r_in = D * IN_SHARD // 128
    r_out = OUT_SHARD * D // 128
    offs, r0 = [], 0
    for n in (r_in, r_out, r_in, r_out):
        offs.append(r0)
        r0 += n
    w_gi = jnp.concatenate([gbig[b, offs[0]:offs[0] + r_in].reshape(D, IN_SHARD) for b in range(4)], axis=1)
    w_go = jnp.concatenate([gbig[b, offs[1]:offs[1] + r_out].reshape(OUT_SHARD, D) for b in range(4)], axis=0)
    w_si = jnp.concatenate([gbig[b, offs[2]:offs[2] + r_in].reshape(D, IN_SHARD) for b in range(4)], axis=1)
    w_so = jnp.concatenate([gbig[b, offs[3]:offs[3] + r_out].reshape(OUT_SHARD, D) for b in range(4)], axis=0)
    sm = [_unpack_flat(gsmall[b], small_offs) for b in range(4)]
    g_cw = jnp.concatenate([sm[b][0] for b in range(4)], axis=1)
    s_cw = jnp.concatenate([sm[b][1] for b in range(4)], axis=1)
    s_cb = jnp.concatenate([sm[b][2] for b in range(4)], axis=0)[None]
    s_nw = jnp.concatenate([sm[b][3] for b in range(4)], axis=0)[None]

    def pad_small(w):
        return jnp.concatenate([w, jnp.zeros((D, SMALL_W - w.shape[1]), w.dtype)], axis=1)

    wg_main, wg_small = w_gi[:, :G_MAIN], pad_small(w_gi[:, G_MAIN:])
    ws_main, ws_small = w_si[:, :S_MAIN], pad_small(w_si[:, S_MAIN:])
    zero_b = jnp.zeros((1, G_CONV), F32)
    nw0, nw1 = norm_w[0:1], norm_w[1:2]
    fw = final_norm_w[None]
    g_alog = gdn_a_log.reshape(GH, 1, 1)
    g_dtb = gdn_dt_bias.reshape(GH, 1, 1)
    g_nw = gdn_norm_w.reshape(1, GDK)
    s_dtb = ssd_dt_bias.reshape(SG, SR).T.reshape(SR, SG, 1, 1)
    s_alog = ssd_a_log.reshape(SG, SR).T.reshape(SR, SG, 1, 1)
    s_d = ssd_d.reshape(SG, SR).T.reshape(SR, SG, 1, 1)

    hid0 = _rms_fwd(x0, nw0, "rms0")
    pg = _matmul(hid0, wg_main, "nn", "gdn_in_proj")
    pg_small = _matmul(hid0, wg_small, "nn", "gdn_in_proj_small", tn=SMALL_W)
    post_q = _l2norm_scaled(GDK ** -0.5)
    post_k = _l2norm_scaled(1.0)
    q = _conv_fwd(pg, 0, g_cw[:, :G_QK], zero_b[:, :G_QK], GDK, post_q, "gdn_conv_q")
    k = _conv_fwd(pg, G_QK, g_cw[:, G_QK:2 * G_QK], zero_b[:, :G_QK], GDK, post_k, "gdn_conv_k")
    v = _conv_fwd(pg, 2 * G_QK, g_cw[:, 2 * G_QK:], zero_b[:, :G_V], GDK, _silu, "gdn_conv_v")
    braw = _gates_to_rows(pg_small[:, :GH], GH, GCH, GDN_CPB)
    araw = _gates_to_rows(pg_small[:, GH:2 * GH], GH, GCH, GDN_CPB)
    o, g_sall = _gdn_fwd(q, k, v, braw, araw, g_alog, g_dtb)
    y0 = _gate_fwd(_gdn_gate, o, pg, G_CONV, g_nw, "gdn_gate")
    x1 = _matmul(y0, w_go, "nn", "gdn_out_proj", add=x0)

    hid1 = _rms_fwd(x1, nw1, "rms1")
    ps = _matmul(hid1, ws_main, "nn", "ssd_in_proj")
    ps_small = _matmul(hid1, ws_small, "nn", "ssd_in_proj_small", tn=SMALL_W)
    c_x, c_b, c_c = S_INNER, 2 * S_INNER, 2 * S_INNER + SG * SN
    post_s = _silu
    xs = _conv_fwd(ps, c_x, s_cw[:, :S_INNER], s_cb[:, :S_INNER], SR * SP, post_s, "ssd_conv_x")
    bm = _conv_fwd(ps, c_b, s_cw[:, S_INNER:S_INNER + SG * SN], s_cb[:, S_INNER:S_INNER + SG * SN], SN,
                   post_s, "ssd_conv_b")
    cm = _conv_fwd(ps, c_c, s_cw[:, S_INNER + SG * SN:], s_cb[:, S_INNER + SG * SN:], SN, post_s,
                   "ssd_conv_c")
    nbs = t // (SCH * SSD_CPB)
    dtraw = _gates_to_rows(ps_small[:, :SH], SH, SCH, SSD_CPB)
    dtraw = dtraw.reshape(nbs, SG, SR, SSD_CPB, SCH).transpose(0, 3, 2, 1, 4).reshape(nbs, SSD_CPB * SR, SG, 1, SCH)
    yss, s_sall = _ssd_fwd(xs, bm, cm, dtraw, s_dtb, s_alog, s_d)
    y1 = _gate_fwd(_ssd_gate, yss, ps, 0, s_nw, "ssd_gate")
    x2 = _matmul(y1, w_so, "nn", "ssd_out_proj", add=x1)

    dx2, d_fw, loss_row = _final_loss(x2, fw, tgt, "final_loss")

    dy1 = _matmul(dx2, w_so, "nt", "ssd_out_dx", out_dtype=BF16)
    d_wso = _matmul(y1, dx2, "tn", "ssd_out_dw")
    dyss, dz_s, d_snw = _gate_bwd(_ssd_gate, yss, ps, 0, s_nw, dy1, True, "ssd_gate_bwd")
    dxs, dbm, dcm, ddtraw, d_sdtb, d_salog, d_sd = _ssd_bwd(xs, bm, cm, dtraw, s_dtb, s_alog, s_d, s_sall, dyss)
    dpx, dwx, dbx = _conv_dpre(ps, c_x, s_cw[:, :S_INNER], s_cb[:, :S_INNER], SR * SP, post_s, dxs, "ssd_dpre_x")
    dpb, dwb, dbb = _conv_dpre(ps, c_b, s_cw[:, S_INNER:S_INNER + SG * SN], s_cb[:, S_INNER:S_INNER + SG * SN],
                               SN, post_s, dbm, "ssd_dpre_b")
    dpc, dwc, dbc = _conv_dpre(ps, c_c, s_cw[:, S_INNER + SG * SN:], s_cb[:, S_INNER + SG * SN:], SN, post_s,
                               dcm, "ssd_dpre_c")
    d_scw = jnp.concatenate([dwx, dwb, dwc], axis=1)
    d_scb = jnp.concatenate([dbx, dbb, dbc], axis=1)
    dxbc = _conv_t(jnp.concatenate([dpx, dpb, dpc], axis=1), s_cw, "ssd_conv_t")
    ddt = ddtraw.reshape(nbs, SSD_CPB, SR, SG, SCH).transpose(0, 3, 2, 1, 4).reshape(nbs, SH, SSD_CPB, SCH)
    ddt = _rows_to_gates(ddt)
    dsm_s = jnp.concatenate([ddt, jnp.zeros((t, SMALL_W - SH), F32)], axis=1).astype(BF16)
    dhid1 = _matmul(dz_s, ws_main[:, :S_INNER], "nt", "ssd_in_dx_z")
    dhid1 = _matmul(dxbc, ws_main[:, S_INNER:], "nt", "ssd_in_dx_xbc", add=dhid1)
    dhid1 = _matmul(dsm_s, ws_small, "nt", "ssd_in_dx_dt", add=dhid1, tk=SMALL_W)
    d_wsi = jnp.concatenate([_matmul(hid1, dz_s, "tn", "ssd_in_dw_z"),
                             _matmul(hid1, dxbc, "tn", "ssd_in_dw_xbc"),
                             _matmul(hid1, dsm_s, "tn", "ssd_in_dw_dt", tn=SMALL_W)[:, :SH]], axis=1)
    dx1, d_nw1 = _rms_bwd(x1, nw1, dx2, dhid1, "rms1_bwd")

    dy0 = _matmul(dx1, w_go, "nt", "gdn_out_dx", out_dtype=BF16)
    d_wgo = _matmul(y0, dx1, "tn", "gdn_out_dw")
    do, dz_g, d_gnw = _gate_bwd(_gdn_gate, o, pg, G_CONV, g_nw, dy0, False, "gdn_gate_bwd")
    dq, dk, dv, dbraw, daraw, d_galog, d_gdtb = _gdn_bwd(q, k, v, braw, araw, g_alog, g_dtb, g_sall, do)
    dpq, dwq, _ = _conv_dpre(pg, 0, g_cw[:, :G_QK], zero_b[:, :G_QK], GDK, post_q, dq, "gdn_dpre_q")
    dpk, dwk, _ = _conv_dpre(pg, G_QK, g_cw[:, G_QK:2 * G_QK], zero_b[:, :G_QK], GDK, post_k, dk, "gdn_dpre_k")
    dpv, dwv, _ = _conv_dpre(pg, 2 * G_QK, g_cw[:, 2 * G_QK:], zero_b[:, :G_V], GDK, _silu, dv, "gdn_dpre_v")
    d_gcw = jnp.concatenate([dwq, dwk, dwv], axis=1)
    dqkv = _conv_t(jnp.concatenate([dpq, dpk, dpv], axis=1), g_cw, "gdn_conv_t")
    dsm_g = jnp.concatenate([_rows_to_gates(dbraw), _rows_to_gates(daraw),
                             jnp.zeros((t, SMALL_W - 2 * GH), F32)], axis=1).astype(BF16)
    dhid0 = _matmul(dqkv, wg_main[:, :G_CONV], "nt", "gdn_in_dx_qkv")
    dhid0 = _matmul(dz_g, wg_main[:, G_CONV:], "nt", "gdn_in_dx_z", add=dhid0)
    dhid0 = _matmul(dsm_g, wg_small, "nt", "gdn_in_dx_ba", add=dhid0, tk=SMALL_W)
    d_wgi = jnp.concatenate([_matmul(hid0, dqkv, "tn", "gdn_in_dw_qkv"),
                             _matmul(hid0, dz_g, "tn", "gdn_in_dw_z"),
                             _matmul(hid0, dsm_g, "tn", "gdn_in_dw_ba", tn=SMALL_W)[:, :2 * GH]], axis=1)
    dx0, d_nw0 = _rms_bwd(x0, nw0, dx1, dhid0, "rms0_bwd")

    blocks = []
    for b in range(4):
        blocks.append(jnp.concatenate([
            _flat128(d_wgi[:, b * IN_SHARD:(b + 1) * IN_SHARD]), _flat128(d_wgo[b * OUT_SHARD:(b + 1) * OUT_SHARD]),
            _flat128(d_wsi[:, b * IN_SHARD:(b + 1) * IN_SHARD]), _flat128(d_wso[b * OUT_SHARD:(b + 1) * OUT_SHARD])],
            axis=0))
    rows = blocks[0].shape[0]
    half = rows // 2
    g4 = jnp.stack(blocks).reshape(4, 2, half, 128)
    g_mine = lax.dynamic_index_in_dim(g4, cc, axis=1, keepdims=False).reshape(4 * half, 128)
    g_other = lax.dynamic_index_in_dim(g4, 1 - cc, axis=1, keepdims=False).reshape(4 * half, 128)
    from_sib = _swap_sibling(g_other, "swap_halves")
    hsum = _add_n([g_mine, from_sib], "sum_cores").reshape(4, half, 128)
    recv = _scatter_chips(hsum)
    own = lax.dynamic_index_in_dim(hsum, chip, axis=0, keepdims=False)
    tot = _add_n([own, recv[0], recv[1], recv[2]], "sum_chips")
    tot_sib = _swap_sibling(tot, "swap_totals")
    lo = jnp.where(cc == 0, tot, tot_sib)
    hi = jnp.where(cc == 0, tot_sib, tot)
    full = jnp.concatenate([lo, hi], axis=0)
    grads = dict(
        gdn_w_in=full[offs[0]:offs[0] + r_in].reshape(1, D, IN_SHARD),
        gdn_w_out=full[offs[1]:offs[1] + r_out].reshape(1, OUT_SHARD, D),
        ssd_w_in=full[offs[2]:offs[2] + r_in].reshape(1, D, IN_SHARD),
        ssd_w_out=full[offs[3]:offs[3] + r_out].reshape(1, OUT_SHARD, D))

    small_parts = [loss_row, jnp.concatenate([d_nw0, d_nw1], axis=0), d_gcw, d_galog, d_gdtb, d_gnw, d_scw, d_scb,
                   d_sdtb, d_salog, d_sd, d_snw, d_fw]
    sbuf, soffs = _pack_flat(small_parts, 8)
    ssum = _unpack_flat(_allreduce_small(sbuf), soffs)
    (loss_s, g_nw_all, g_gcw, g_galog, g_gdtb, g_gnw, g_scw, g_scb, g_sdtb, g_salog, g_sd, g_snw, g_fw) = ssum

    def my_cols(a, width):
        return lax.dynamic_slice_in_dim(a, chip * width, width, axis=a.ndim - 1)

    grads.update(
        norm_w=g_nw_all, gdn_conv_w=my_cols(g_gcw, 1024)[None], gdn_a_log=g_galog.reshape(1, GH),
        gdn_dt_bias=g_gdtb.reshape(1, GH), gdn_norm_w=g_gnw.reshape(1, GDK),
        ssd_conv_w=my_cols(g_scw, 1024)[None], ssd_conv_b=my_cols(g_scb, 1024),
        ssd_dt_bias=g_sdtb.reshape(SR, SG).T.reshape(1, SH), ssd_a_log=g_salog.reshape(SR, SG).T.reshape(1, SH),
        ssd_d=g_sd.reshape(SR, SG).T.reshape(1, SH),
        ssd_norm_w=my_cols(g_snw, 512), final_norm_w=g_fw.reshape(D))
    loss = loss_s[0, 0]

    big_names = ("gdn_w_in", "gdn_w_out", "ssd_w_in", "ssd_w_out")
    deltas, new_m, new_v = {}, {}, {}
    for n in big_names:
        deltas[n], new_m[n], new_v[n] = _adamw(ws[n], grads[n], ms[n], vs[n], "adamw_" + n)
    rest = [n for n in names if n not in big_names]
    packs = [_pack_flat([d[n] for n in rest], 8) for d in (ws, grads, ms, vs)]
    outs = _adamw(*[p[0] for p in packs], "adamw_small")
    for d, buf in zip((deltas, new_m, new_v), outs):
        for n, a in zip(rest, _unpack_flat(buf, packs[0][1])):
            d[n] = a

    grad_x = dx0.reshape(1, t, D)
    return (loss, grad_x, *[grads[n] for n in names], *[deltas[n] for n in names],
            *[new_m[n] for n in names], *[new_v[n] for n in names])
```

```python
import functools

import jax
import jax.numpy as jnp
from jax import lax
from jax.experimental import pallas as pl
from jax.experimental.pallas import tpu as pltpu

F32 = jnp.float32
BF16 = jnp.bfloat16
SDS = jax.ShapeDtypeStruct
MESH = pl.DeviceIdType.MESH

D = 1024
EPS = 1e-6
CONV_K = 4
N_CHIPS = 4
GH = 16
GHQ = 8
GDK = 128
GCH = 64
G_QK = 1024
G_V = 2048
G_CONV = 4096
G_MAIN = 6144
G_IN = 6176
SH = 32
SP = 64
SN = 128
SG = 8
SR = 4
SCH = 128
S_INNER = 2048
S_MAIN = 6144
S_IN = 6176
IN_SHARD = 1544
OUT_SHARD = 512
SMALL_W = 128

ADAM_LR = 0.001
ADAM_B1 = 0.9
ADAM_B2 = 0.999
ADAM_EPS = 1e-08
ADAM_WD = 0.01
ADAM_STEP = 10

VMEM_LIMIT = 48 * 1024 * 1024
BLOCK_ELEMS = 256 * 1024
NEG = -1e30


def _pcall(body, **kw):
    return pl.pallas_call(body, **kw)


def _cp(sem=None, vmem=VMEM_LIMIT):
    return pltpu.CompilerParams(dimension_semantics=sem, vmem_limit_bytes=vmem)


def _sigmoid(x):
    return 1.0 / (1.0 + jnp.exp(-x))


def _silu(x):
    return x * _sigmoid(x)


def _softplus(x):
    return jnp.maximum(x, 0.0) + jnp.log(1.0 + jnp.exp(-jnp.abs(x)))


def _rms(x, w):
    return x * lax.rsqrt(jnp.mean(x * x, axis=-1, keepdims=True) + EPS) * w


_DIMS = {"nn": (((2,), (1,)), ((0,), (0,))),
         "nt": (((2,), (2,)), ((0,), (0,))),
         "tn": (((1,), (1,)), ((0,), (0,)))}


def _bdot(a, b, spec):
    return lax.dot_general(a.astype(BF16), b.astype(BF16), _DIMS[spec], preferred_element_type=F32)


@functools.partial(jax.custom_vjp, nondiff_argnums=(2,))
def _bmm(a, b, spec):
    return _bdot(a, b, spec)


def _bmm_fwd(a, b, spec):
    return _bdot(a, b, spec), (a, b)


def _bmm_bwd(spec, res, g):
    a, b = res
    if spec == "nn":
        return _bdot(g, b, "nt"), _bdot(a, g, "tn")
    if spec == "nt":
        return _bdot(g, b, "nn"), _bdot(g, a, "tn")
    return _bdot(b, g, "nt"), _bdot(a, g, "nn")


_bmm.defvjp(_bmm_fwd, _bmm_bwd)


@jax.custom_vjp
def _tri_inv(n):
    t = -n
    p = n
    steps = (n.shape[-1] - 1).bit_length() - 1
    r = lax.broadcasted_iota(jnp.int32, n.shape, 1)
    c = lax.broadcasted_iota(jnp.int32, n.shape, 2)
    t = t + jnp.where(r == c, 1.0, 0.0)
    for _ in range(steps):
        p = _bdot(p, p, "nn")
        t = t + _bdot(t, p, "nn")
    return t


def _tri_inv_fwd(n):
    t = _tri_inv(n)
    return t, t


def _tri_inv_bwd(t, g):
    return (-_bdot(_bdot(t, g, "tn"), t, "nt"),)


_tri_inv.defvjp(_tri_inv_fwd, _tri_inv_bwd)


def _masks(c, lead=1):
    r = lax.broadcasted_iota(jnp.int32, (lead, c, c), 1)
    s = lax.broadcasted_iota(jnp.int32, (lead, c, c), 2)
    return r >= s, r > s, r == s, r <= s


def _row_to_col(row, eye):
    return jnp.sum(jnp.where(eye, row, 0.0), axis=2, keepdims=True)


def _gdn_chunk(q, k, v, braw, araw, alog, dtb, s):
    h = v.shape[0]
    c = v.shape[1]
    rep = h // q.shape[0]
    tril, strict, eye, triu = _masks(c)
    qq = jnp.broadcast_to(q[:, None], (q.shape[0], rep) + q.shape[1:]).reshape(v.shape)
    kk = jnp.broadcast_to(k[:, None], (k.shape[0], rep) + k.shape[1:]).reshape(v.shape)
    beta_row = _sigmoid(braw)
    g_row = -jnp.exp(alog) * _softplus(araw + dtb)
    beta_col = _row_to_col(beta_row, eye)
    g_col = _row_to_col(g_row, eye)
    gc_col = jnp.sum(jnp.where(tril, g_row, 0.0), axis=2, keepdims=True)
    gc_row = jnp.sum(jnp.where(triu, g_col, 0.0), axis=1, keepdims=True)
    gc_last = jnp.sum(g_row, axis=2, keepdims=True)
    lmat = jnp.exp(jnp.where(tril, gc_col - gc_row, NEG))
    kb = kk * beta_col
    vb = v * beta_col
    n = jnp.where(strict, _bmm(kb, kk, "nt") * lmat, 0.0)
    t = _tri_inv(n)
    e_col = jnp.exp(gc_col)
    u = _bmm(t, vb, "nn")
    w = _bmm(t, kb * e_col, "nn")
    attn = _bmm(qq, kk, "nt") * lmat
    q_dec = qq * e_col
    k_dec = kk * jnp.exp(gc_last - gc_col)
    v_new = u - _bmm(w, s, "nn")
    o = _bmm(q_dec, s, "nn") + _bmm(attn, v_new, "nn")
    s_new = s * jnp.exp(gc_last) + _bmm(k_dec, v_new, "tn")
    return o, s_new


def _ssd_chunk(xs, bm, cm, dtraw, dtb, alog, dskip, s):
    c = xs.shape[1]
    tril, _, eye, triu = _masks(c)
    lane = lax.broadcasted_iota(jnp.int32, (1, 1, SR * SP), 2)
    prow = lax.broadcasted_iota(jnp.int32, (1, SR * SP, 1), 1)
    cb = _bmm(cm, bm, "nt")
    cs = _bmm(cm, s, "nt")
    y = jnp.zeros_like(xs)
    e_full = jnp.zeros_like(xs)
    skip = jnp.zeros((xs.shape[0], 1, SR * SP), F32)
    dec_rows = jnp.zeros((xs.shape[0], SR * SP, 1), F32)
    states = jnp.zeros_like(s)
    for r in range(SR):
        m_lane = jnp.where((lane >= r * SP) & (lane < (r + 1) * SP), 1.0, 0.0)
        m_row = jnp.where((prow >= r * SP) & (prow < (r + 1) * SP), 1.0, 0.0)
        dt_row = _softplus(dtraw[r] + dtb[r])
        adt_row = -jnp.exp(alog[r]) * dt_row
        dt_col = _row_to_col(dt_row, eye)
        adt_col = _row_to_col(adt_row, eye)
        acs_col = jnp.sum(jnp.where(tril, adt_row, 0.0), axis=2, keepdims=True)
        acs_row = jnp.sum(jnp.where(triu, adt_col, 0.0), axis=1, keepdims=True)
        acs_last = jnp.sum(adt_row, axis=2, keepdims=True)
        lmat = jnp.exp(jnp.where(tril, acs_col - acs_row, NEG))
        x_r = xs * (dt_col * m_lane)
        y = y + _bmm(cb * lmat, x_r, "nn")
        states = states + _bmm(x_r, bm * jnp.exp(acs_last - acs_col), "tn")
        e_full = e_full + m_lane * jnp.exp(acs_col)
        dec_rows = dec_rows + m_row * jnp.exp(acs_last)
        skip = skip + m_lane * dskip[r]
    y = y + cs * e_full + xs * skip
    s_new = s * dec_rows + states
    return y, s_new


def _matmul(a, b, mode, name, out_dtype=F32, add=None, tm=1024, tn=1024, tk=2048):
    if mode == "nn":
        (m, k), n = a.shape, b.shape[1]
    elif mode == "nt":
        (m, k), n = a.shape, b.shape[0]
    else:
        (k, m), n = a.shape, b.shape[1]
    tm, tn, tk = min(tm, m), min(tn, n), min(tk, k)
    assert m % tm == 0 and n % tn == 0 and k % tk == 0, (name, m, n, k)
    nk = k // tk
    dims = {"nn": (((1,), (0,)), ((), ())), "nt": (((1,), (1,)), ((), ())),
            "tn": (((0,), (0,)), ((), ()))}[mode]
    a_spec = {"nn": pl.BlockSpec((tm, tk), lambda i, j, kk: (i, kk)),
              "nt": pl.BlockSpec((tm, tk), lambda i, j, kk: (i, kk)),
              "tn": pl.BlockSpec((tk, tm), lambda i, j, kk: (kk, i))}[mode]
    b_spec = {"nn": pl.BlockSpec((tk, tn), lambda i, j, kk: (kk, j)),
              "nt": pl.BlockSpec((tn, tk), lambda i, j, kk: (j, kk)),
              "tn": pl.BlockSpec((tk, tn), lambda i, j, kk: (kk, j))}[mode]
    o_spec = pl.BlockSpec((tm, tn), lambda i, j, kk: (i, j))
    has_add = add is not None

    def body(*refs):
        a_ref, b_ref = refs[:2]
        add_ref = refs[2] if has_add else None
        o_ref = refs[2 + has_add]
        part = lax.dot_general(a_ref[...].astype(BF16), b_ref[...].astype(BF16), dims,
                               preferred_element_type=F32)

        def finish(r):
            if has_add:
                r = r + add_ref[...].astype(F32)
            o_ref[...] = r.astype(o_ref.dtype)

        if nk == 1:
            finish(part)
            return
        acc_ref = refs[3 + has_add]
        kk = pl.program_id(2)

        @pl.when(kk == 0)
        def _():
            acc_ref[...] = part

        @pl.when(jnp.logical_and(kk > 0, kk < nk - 1))
        def _():
            acc_ref[...] += part

        @pl.when(kk == nk - 1)
        def _():
            finish(acc_ref[...] + part)

    ins = [a, b] + ([add] if has_add else [])
    in_specs = [a_spec, b_spec] + ([o_spec] if has_add else [])
    scratch = [] if nk == 1 else [pltpu.VMEM((tm, tn), F32)]
    return _pcall(body, grid=(m // tm, n // tn, nk), in_specs=in_specs, out_specs=o_spec,
                  out_shape=SDS((m, n), out_dtype), scratch_shapes=scratch,
                  compiler_params=_cp(("parallel", "parallel", "arbitrary")), name=name)(*ins)


def _ew(fn, ins, in_specs, out_shape, out_specs, grid, name):
    n_in = len(ins)

    def body(*refs):
        outs = fn(*[r[...] for r in refs[:n_in]])
        for r, o in zip(refs[n_in:], outs):
            r[...] = o.astype(r.dtype)

    return _pcall(body, grid=grid, in_specs=in_specs, out_specs=out_specs, out_shape=out_shape,
                  compiler_params=_cp(("arbitrary",) * len(grid)), name=name)(*ins)


def _ew_vjp(fn, ins, in_specs, cts, ct_specs, wrt, g_shape, g_specs, acc, grid, name):
    n_in, n_ct = len(ins), len(cts)

    def body(*refs):
        vals = [r[...].astype(F32) for r in refs[:n_in]]
        outs, vjp = jax.vjp(fn, *vals)
        g_all = vjp(tuple(r[...].astype(F32) for r in refs[n_in:n_in + n_ct]))
        for pos, (i, g_ref) in enumerate(zip(wrt, refs[n_in + n_ct:])):
            g = g_all[i]
            if pos in acc:
                first = functools.reduce(
                    jnp.logical_and, [pl.program_id(ax) == 0 for ax in range(acc[pos], len(grid))])

                @pl.when(first)
                def _():
                    g_ref[...] = jnp.zeros_like(g_ref)

                g_ref[...] += g.astype(g_ref.dtype)
            else:
                g_ref[...] = g.astype(g_ref.dtype)

    return _pcall(body, grid=grid, in_specs=list(in_specs) + list(ct_specs), out_specs=g_specs,
                  out_shape=g_shape, compiler_params=_cp(("arbitrary",) * len(grid)),
                  name=name)(*ins, *cts)


def _row_spec(tm, n):
    return pl.BlockSpec((tm, n), lambda i: (i, 0))


def _par_spec(n):
    return pl.BlockSpec((1, n), lambda i: (0, 0))


def _rms_fwd(x, w, name):
    t = x.shape[0]
    tm = min(t, 512)
    return _ew(lambda xv, wv: (_rms(xv, wv),), [x, w], [_row_spec(tm, D), _par_spec(D)],
               [SDS((t, D), BF16)], [_row_spec(tm, D)], (t // tm,), name)[0]


def _rms_bwd(x, w, dres, dhid, name):
    t = x.shape[0]
    tm = min(t, 512)
    return _ew_vjp(lambda xv, wv: (xv, _rms(xv, wv)), [x, w], [_row_spec(tm, D), _par_spec(D)],
                   [dres, dhid], [_row_spec(tm, D), _row_spec(tm, D)], (0, 1),
                   [SDS((t, D), F32), SDS((1, D), F32)], [_row_spec(tm, D), _par_spec(D)],
                   {1: 0}, (t // tm,), name)


def _final_loss(x, w, tgt, name):
    t = x.shape[0]
    tm = min(t, 512)

    def body(x_ref, w_ref, t_ref, dx_ref, dw_ref, l_ref):
        @pl.when(pl.program_id(0) == 0)
        def _():
            dw_ref[...] = jnp.zeros_like(dw_ref)
            l_ref[...] = jnp.zeros_like(l_ref)

        xv, wv = x_ref[...], w_ref[...]
        rstd = lax.rsqrt(jnp.mean(xv * xv, axis=-1, keepdims=True) + EPS)
        xh = xv * rstd
        err = xh * wv - t_ref[...]
        l_ref[...] += 0.5 * jnp.sum(jnp.mean(err * err, axis=-1, keepdims=True), axis=0, keepdims=True)
        dy = err * (1.0 / D)
        dw_ref[...] += jnp.sum(dy * xh, axis=0, keepdims=True)
        dxh = dy * wv
        dx_ref[...] = rstd * (dxh - xh * jnp.mean(dxh * xh, axis=-1, keepdims=True))

    return _pcall(body, grid=(t // tm,), in_specs=[_row_spec(tm, D), _par_spec(D), _row_spec(tm, D)],
                  out_specs=[_row_spec(tm, D), _par_spec(D), _par_spec(128)],
                  out_shape=[SDS((t, D), F32), SDS((1, D), F32), SDS((1, 128), F32)],
                  compiler_params=_cp(("arbitrary",)), name=name)(x, w, tgt)


def _conv_taps(ext, w_ref, tm, lo):
    n = ext.shape[0]
    acc = None
    for j in range(CONV_K):
        shift = (CONV_K - 1 - j) if lo else (n - (CONV_K - 1 - j)) % n
        rolled = pltpu.roll(ext, shift, 0) if shift else ext
        term = w_ref[pl.ds(j, 1), :] * rolled[lo:lo + tm]
        acc = term if acc is None else acc + term
    return acc


def _conv_pre_specs(tm, ct, col0):
    hb = tm // 8
    return [pl.BlockSpec((tm, ct), lambda j, i: (i, col0 + j)),
            pl.BlockSpec((8, ct), lambda j, i: (jnp.maximum(i * hb - 1, 0), col0 + j)),
            pl.BlockSpec((CONV_K, ct), lambda j, i: (0, j)),
            pl.BlockSpec((1, ct), lambda j, i: (0, j))]


def _conv_pre_value(x_ref, xh_ref, w_ref, b_ref, tm):
    halo = jnp.where(pl.program_id(1) > 0, xh_ref[...], 0.0)
    ext = jnp.concatenate([halo, x_ref[...]], axis=0)
    return _conv_taps(ext, w_ref, tm, 8) + b_ref[...], ext


def _conv_fwd(x, col0, w, b, ct, post, name):
    t = x.shape[0]
    tm = min(t, BLOCK_ELEMS // ct)
    nt = w.shape[1] // ct

    def body(x_ref, xh_ref, w_ref, b_ref, o_ref):
        pre, _ = _conv_pre_value(x_ref, xh_ref, w_ref, b_ref, tm)
        o_ref[0] = post(pre)

    return _pcall(body, grid=(nt, t // tm), in_specs=_conv_pre_specs(tm, ct, col0 // ct),
                  out_specs=pl.BlockSpec((1, tm, ct), lambda j, i: (j, i, 0)),
                  out_shape=SDS((nt, t, ct), F32), compiler_params=_cp(("arbitrary", "arbitrary")),
                  name=name)(x, x, w, b)


def _conv_dpre(x, col0, w, b, ct, post, dout, name, c_total, c_off, into=None):
    t = x.shape[0]
    tm = min(t, BLOCK_ELEMS // ct)
    nt = w.shape[1] // ct
    chained = into is not None

    def body(*refs):
        x_ref, xh_ref, w_ref, b_ref, do_ref = refs[:5]
        dp_ref, dw_ref, db_ref = refs[5 + chained:]
        pre, ext = _conv_pre_value(x_ref, xh_ref, w_ref, b_ref, tm)
        _, vjp = jax.vjp(post, pre)
        dpre = vjp(do_ref[0])[0]
        dp_ref[...] = dpre

        @pl.when(pl.program_id(1) == 0)
        def _():
            dw_ref[...] = jnp.zeros_like(dw_ref)
            db_ref[...] = jnp.zeros_like(db_ref)

        for j in range(CONV_K):
            xs = (pltpu.roll(ext, CONV_K - 1 - j, 0) if j < CONV_K - 1 else ext)[8:8 + tm]
            dw_ref[pl.ds(j, 1), :] += jnp.sum(dpre * xs, axis=0, keepdims=True)
        db_ref[...] += jnp.sum(dpre, axis=0, keepdims=True)

    c = w.shape[1]
    o0 = c_off // ct
    return _pcall(body, grid=(nt, t // tm),
                  in_specs=_conv_pre_specs(tm, ct, col0 // ct)
                  + [pl.BlockSpec((1, tm, ct), lambda j, i: (j, i, 0))] + ([ANY] if chained else []),
                  out_specs=[pl.BlockSpec((tm, ct), lambda j, i: (i, o0 + j)),
                             pl.BlockSpec((CONV_K, ct), lambda j, i: (0, j)),
                             pl.BlockSpec((1, ct), lambda j, i: (0, j))],
                  out_shape=[SDS((t, c_total), F32), SDS((CONV_K, c), F32), SDS((1, c), F32)],
                  input_output_aliases={5: 0} if chained else {},
                  compiler_params=_cp(("arbitrary", "arbitrary")), name=name)(
                      x, x, w, b, dout, *([into] if chained else []))


def _conv_t(dpre, w, name):
    t, c = dpre.shape
    tm = min(t, 512)
    ct = min(c, 512)
    hb = tm // 8
    last = t // tm - 1

    def body(d_ref, dh_ref, w_ref, o_ref):
        halo = jnp.where(pl.program_id(1) < last, dh_ref[...], 0.0)
        ext = jnp.concatenate([d_ref[...], halo], axis=0)
        o_ref[...] = _conv_taps(ext, w_ref, tm, 0).astype(o_ref.dtype)

    return _pcall(body, grid=(c // ct, t // tm),
                  in_specs=[pl.BlockSpec((tm, ct), lambda j, i: (i, j)),
                            pl.BlockSpec((8, ct), lambda j, i: (jnp.minimum((i + 1) * hb, t // 8 - 1), j)),
                            pl.BlockSpec((CONV_K, ct), lambda j, i: (0, j))],
                  out_specs=pl.BlockSpec((tm, ct), lambda j, i: (i, j)),
                  out_shape=SDS((t, c), BF16), compiler_params=_cp(("arbitrary", "arbitrary")),
                  name=name)(dpre, dpre, w)


def _l2norm_scaled(scale):
    def post(pre):
        a = _silu(pre)
        return a * lax.rsqrt(jnp.sum(a * a, axis=-1, keepdims=True) + EPS) * scale
    return post


GDN_HB = 8
GDN_CPB = 2
SSD_GB = 2
SSD_CPB = 1


def _gdn_specs(nb, rev):
    hb, cpb, tc = GDN_HB, GDN_CPB, GDN_CPB * GCH
    blk = (lambda n: nb - 1 - n) if rev else (lambda n: n)
    seq = lambda h: pl.BlockSpec((h, tc, GDK), lambda g, n: (g, blk(n), 0))
    gate = pl.BlockSpec((1, hb, cpb, GCH), lambda g, n: (blk(n), g, 0, 0))
    par = pl.BlockSpec((hb, 1, 1), lambda g, n: (g, 0, 0))
    state = pl.BlockSpec((cpb, hb, GDK, GDK), lambda g, n: (blk(n), g, 0, 0))
    return seq, gate, par, state


def _gdn_fwd(q, k, v, braw, araw, alog, dtb):
    t = v.shape[1]
    hb, cpb = GDN_HB, GDN_CPB
    nb = t // (cpb * GCH)
    seq, gate, par, state = _gdn_specs(nb, False)

    def body(q_ref, k_ref, v_ref, b_ref, a_ref, al_ref, dt_ref, o_ref, sall_ref, s_ref):
        @pl.when(pl.program_id(1) == 0)
        def _():
            s_ref[...] = jnp.zeros_like(s_ref)

        s = s_ref[...]
        for c in range(cpb):
            rows = pl.ds(c * GCH, GCH)
            sall_ref[c] = s
            o, s = _gdn_chunk(q_ref[:, rows, :], k_ref[:, rows, :], v_ref[:, rows, :],
                              b_ref[0, :, pl.ds(c, 1), :], a_ref[0, :, pl.ds(c, 1), :],
                              al_ref[...], dt_ref[...], s)
            o_ref[:, rows, :] = o
        s_ref[...] = s

    return _pcall(body, grid=(GH // hb, nb),
                  in_specs=[seq(hb // 2), seq(hb // 2), seq(hb), gate, gate, par, par],
                  out_specs=[seq(hb), state],
                  out_shape=[SDS((GH, t, GDK), F32), SDS((t // GCH, GH, GDK, GDK), F32)],
                  scratch_shapes=[pltpu.VMEM((hb, GDK, GDK), F32)],
                  compiler_params=_cp(("arbitrary", "arbitrary")), name="gdn_chunk_fwd")(
                      q, k, v, braw, araw, alog, dtb)


def _gdn_bwd(q, k, v, braw, araw, alog, dtb, sall, do):
    t = v.shape[1]
    hb, cpb = GDN_HB, GDN_CPB
    nb = t // (cpb * GCH)
    seq, gate, par, state = _gdn_specs(nb, True)

    def body(q_ref, k_ref, v_ref, b_ref, a_ref, al_ref, dt_ref, sall_ref, do_ref,
             dq_ref, dk_ref, dv_ref, db_ref, da_ref, dal_ref, ddt_ref, ds_ref):
        @pl.when(pl.program_id(1) == 0)
        def _():
            ds_ref[...] = jnp.zeros_like(ds_ref)
            dal_ref[...] = jnp.zeros_like(dal_ref)
            ddt_ref[...] = jnp.zeros_like(ddt_ref)

        ds = ds_ref[...]
        for c in reversed(range(cpb)):
            rows = pl.ds(c * GCH, GCH)
            _, vjp = jax.vjp(_gdn_chunk, q_ref[:, rows, :], k_ref[:, rows, :], v_ref[:, rows, :],
                             b_ref[0, :, pl.ds(c, 1), :], a_ref[0, :, pl.ds(c, 1), :],
                             al_ref[...], dt_ref[...], sall_ref[c])
            dq, dk, dv, db, da, dal, ddt, ds = vjp((do_ref[:, rows, :], ds))
            dq_ref[:, rows, :] = dq
            dk_ref[:, rows, :] = dk
            dv_ref[:, rows, :] = dv
            db_ref[0, :, pl.ds(c, 1), :] = db
            da_ref[0, :, pl.ds(c, 1), :] = da
            dal_ref[...] += dal
            ddt_ref[...] += ddt
        ds_ref[...] = ds

    return _pcall(body, grid=(GH // hb, nb),
                  in_specs=[seq(hb // 2), seq(hb // 2), seq(hb), gate, gate, par, par, state, seq(hb)],
                  out_specs=[seq(hb // 2), seq(hb // 2), seq(hb), gate, gate, par, par],
                  out_shape=[SDS(q.shape, F32), SDS(k.shape, F32), SDS(v.shape, F32),
                             SDS(braw.shape, F32), SDS(araw.shape, F32),
                             SDS((GH, 1, 1), F32), SDS((GH, 1, 1), F32)],
                  scratch_shapes=[pltpu.VMEM((hb, GDK, GDK), F32)],
                  compiler_params=_cp(("arbitrary", "arbitrary")), name="gdn_chunk_bwd")(
                      q, k, v, braw, araw, alog, dtb, sall, do)


def _ssd_specs(nb, rev):
    gb, cpb, tc = SSD_GB, SSD_CPB, SSD_CPB * SCH
    blk = (lambda n: nb - 1 - n) if rev else (lambda n: n)
    seq = lambda w: pl.BlockSpec((gb, tc, w), lambda g, n: (g, blk(n), 0))
    gate = pl.BlockSpec((1, cpb * SR, gb, 1, SCH), lambda g, n: (blk(n), 0, g, 0, 0))
    par = pl.BlockSpec((SR, gb, 1, 1), lambda g, n: (0, g, 0, 0))
    state = pl.BlockSpec((cpb, gb, SR * SP, SN), lambda g, n: (blk(n), g, 0, 0))
    return seq, gate, par, state


def _ssd_fwd(xs, bm, cm, dtraw, dtb, alog, dskip):
    t = xs.shape[1]
    gb, cpb = SSD_GB, SSD_CPB
    nb = t // (cpb * SCH)
    seq, gate, par, state = _ssd_specs(nb, False)

    def body(x_ref, b_ref, c_ref, dt_ref, dtb_ref, al_ref, dk_ref, y_ref, sall_ref, s_ref):
        @pl.when(pl.program_id(1) == 0)
        def _():
            s_ref[...] = jnp.zeros_like(s_ref)

        s = s_ref[...]
        for c in range(cpb):
            rows = pl.ds(c * SCH, SCH)
            sall_ref[c] = s
            y, s = _ssd_chunk(x_ref[:, rows, :], b_ref[:, rows, :], c_ref[:, rows, :],
                              dt_ref[0, pl.ds(c * SR, SR)], dtb_ref[...], al_ref[...],
                              dk_ref[...], s)
            y_ref[:, rows, :] = y
        s_ref[...] = s

    return _pcall(body, grid=(SG // gb, nb),
                  in_specs=[seq(SR * SP), seq(SN), seq(SN), gate, par, par, par],
                  out_specs=[seq(SR * SP), state],
                  out_shape=[SDS((SG, t, SR * SP), F32), SDS((t // SCH, SG, SR * SP, SN), F32)],
                  scratch_shapes=[pltpu.VMEM((gb, SR * SP, SN), F32)],
                  compiler_params=_cp(("arbitrary", "arbitrary")), name="ssd_chunk_fwd")(
                      xs, bm, cm, dtraw, dtb, alog, dskip)


def _ssd_bwd(xs, bm, cm, dtraw, dtb, alog, dskip, sall, dy):
    t = xs.shape[1]
    gb, cpb = SSD_GB, SSD_CPB
    nb = t // (cpb * SCH)
    seq, gate, par, state = _ssd_specs(nb, True)

    def body(x_ref, b_ref, c_ref, dt_ref, dtb_ref, al_ref, dk_ref, sall_ref, dy_ref,
             dx_ref, dbm_ref, dcm_ref, ddt_ref, ddtb_ref, dal_ref, ddk_ref, ds_ref):
        @pl.when(pl.program_id(1) == 0)
        def _():
            ds_ref[...] = jnp.zeros_like(ds_ref)
            ddtb_ref[...] = jnp.zeros_like(ddtb_ref)
            dal_ref[...] = jnp.zeros_like(dal_ref)
            ddk_ref[...] = jnp.zeros_like(ddk_ref)

        ds = ds_ref[...]
        for c in reversed(range(cpb)):
            rows = pl.ds(c * SCH, SCH)
            _, vjp = jax.vjp(_ssd_chunk, x_ref[:, rows, :], b_ref[:, rows, :], c_ref[:, rows, :],
                             dt_ref[0, pl.ds(c * SR, SR)], dtb_ref[...], al_ref[...],
                             dk_ref[...], sall_ref[c])
            dx, dbm, dcm, ddt, ddtb, dal, ddk, ds = vjp((dy_ref[:, rows, :], ds))
            dx_ref[:, rows, :] = dx
            dbm_ref[:, rows, :] = dbm
            dcm_ref[:, rows, :] = dcm
            ddt_ref[0, pl.ds(c * SR, SR)] = ddt
            ddtb_ref[...] += ddtb
            dal_ref[...] += dal
            ddk_ref[...] += ddk
        ds_ref[...] = ds

    return _pcall(body, grid=(SG // gb, nb),
                  in_specs=[seq(SR * SP), seq(SN), seq(SN), gate, par, par, par, state, seq(SR * SP)],
                  out_specs=[seq(SR * SP), seq(SN), seq(SN), gate, par, par, par],
                  out_shape=[SDS(xs.shape, F32), SDS(bm.shape, F32), SDS(cm.shape, F32),
                             SDS(dtraw.shape, F32), SDS((SR, SG, 1, 1), F32), SDS((SR, SG, 1, 1), F32),
                             SDS((SR, SG, 1, 1), F32)],
                  scratch_shapes=[pltpu.VMEM((gb, SR * SP, SN), F32)],
                  compiler_params=_cp(("arbitrary", "arbitrary")), name="ssd_chunk_bwd")(
                      xs, bm, cm, dtraw, dtb, alog, dskip, sall, dy)


def _gate_specs(tm, ct, zcol0, per_tile_w):
    z0 = zcol0 // ct
    return [pl.BlockSpec((1, tm, ct), lambda i, j: (j, i, 0)),
            pl.BlockSpec((tm, ct), lambda i, j: (i, z0 + j)),
            pl.BlockSpec((1, ct), (lambda i, j: (0, j)) if per_tile_w else (lambda i, j: (0, 0)))]


def _gdn_gate(o, z, w):
    return (_rms(o[0], w) * _silu(z),)


def _ssd_gate(y, z, w):
    return (_rms(y[0] * _silu(z), w),)


def _gate_fwd(fn, o, proj, zcol0, w, name):
    nt, t, ct = o.shape
    tm = min(t, BLOCK_ELEMS // ct)
    return _ew(fn, [o, proj, w], _gate_specs(tm, ct, zcol0, w.shape[1] > ct), [SDS((t, nt * ct), BF16)],
               [pl.BlockSpec((tm, ct), lambda i, j: (i, j))], (t // tm, nt), name)[0]


def _gate_bwd(fn, o, proj, zcol0, w, dy, wacc, name):
    nt, t, ct = o.shape
    tm = min(t, BLOCK_ELEMS // ct)
    specs = _gate_specs(tm, ct, zcol0, wacc)
    out_spec = pl.BlockSpec((tm, ct), lambda i, j: (i, j))
    if wacc:
        flip = lambda s: pl.BlockSpec(s.block_shape, lambda j, i, f=s.index_map: f(i, j))
        specs = [flip(s) for s in specs]
        out_spec = flip(out_spec)
        grid, acc = (nt, t // tm), {2: 1}
    else:
        grid, acc = (t // tm, nt), {2: 0}
    return _ew_vjp(fn, [o, proj, w], specs, [dy], [out_spec], (0, 1, 2),
                   [SDS(o.shape, F32), SDS((t, nt * ct), BF16), SDS(w.shape, F32)],
                   [specs[0], out_spec, specs[2]], acc, grid, name)


def _adamw_math(w, g, m, v):
    m = ADAM_B1 * m + (1.0 - ADAM_B1) * g
    v = ADAM_B2 * v + (1.0 - ADAM_B2) * jnp.square(g)
    m_hat = m / (1.0 - ADAM_B1 ** ADAM_STEP)
    v_hat = v / (1.0 - ADAM_B2 ** ADAM_STEP)
    delta = -ADAM_LR * (m_hat / (jnp.sqrt(v_hat) + ADAM_EPS) + ADAM_WD * w)
    return delta, m, v


def _adamw(w, g, m, v, name):
    shape = w.shape
    w2, g2, m2, v2 = [a.reshape(-1, shape[-1]) for a in (w, g, m, v)]
    r, c = w2.shape
    tr = 256 if r % 256 == 0 else r
    spec = pl.BlockSpec((tr, c), lambda i: (i, 0))
    outs = _ew(_adamw_math, [w2, g2, m2, v2], [spec] * 4, [SDS((r, c), F32)] * 3, [spec] * 3,
               (r // tr,), name)
    return [o.reshape(shape) for o in outs]


def _coords():
    return lax.axis_index("x"), lax.axis_index("y"), lax.axis_index("c")


def _other_chips(x, y):
    return [(1 - x, y), (x, 1 - y), (1 - x, 1 - y)]


ANY = pl.BlockSpec(memory_space=pl.ANY)


def _gather_weights(big, small):
    r = big.shape[0]
    half = r // 2

    def body(big_ref, small_ref, obig_ref, osmall_ref, send_sems, recv_sems, local_sems):
        x, y, c = _coords()
        me = 2 * x + y
        sibling = (x, y, 1 - c)
        chips = _other_chips(x, y)
        mine = pl.ds(c * half, half)
        theirs = pl.ds((1 - c) * half, half)

        loc_b = pltpu.make_async_copy(big_ref, obig_ref.at[me], local_sems.at[0])
        loc_s = pltpu.make_async_copy(small_ref, osmall_ref.at[me], local_sems.at[1])
        loc_b.start()
        loc_s.start()

        def rcopy(src, dst, k, to):
            return pltpu.make_async_remote_copy(src_ref=src, dst_ref=dst, send_sem=send_sems.at[k],
                                                recv_sem=recv_sems.at[k], device_id=to,
                                                device_id_type=MESH)

        first = []
        for j, (px, py) in enumerate(chips):
            first.append(rcopy(big_ref.at[mine], obig_ref.at[me, mine], j, (px, py, c)))
            first.append(rcopy(small_ref, osmall_ref.at[me], 6 + j, (px, py, c)))
        for cp in first:
            cp.start()
        passed = []
        for j, (px, py) in enumerate(chips):
            src = 2 * px + py
            rcopy(big_ref.at[mine], obig_ref.at[src, mine], j, (px, py, c)).wait_recv()
            fw = rcopy(obig_ref.at[src, mine], obig_ref.at[src, mine], 3 + j, sibling)
            fw.start()
            passed.append(fw)
        for j, (px, py) in enumerate(chips):
            src = 2 * px + py
            rcopy(big_ref.at[theirs], obig_ref.at[src, theirs], 3 + j, sibling).wait_recv()
            rcopy(small_ref, osmall_ref.at[src], 6 + j, (px, py, c)).wait_recv()
        for cp in first + passed:
            cp.wait_send()
        loc_b.wait()
        loc_s.wait()

    return _pcall(body, in_specs=[ANY, ANY], out_specs=[ANY, ANY],
                  out_shape=[SDS((N_CHIPS,) + big.shape, big.dtype),
                             SDS((N_CHIPS,) + small.shape, small.dtype)],
                  scratch_shapes=[pltpu.SemaphoreType.DMA((9,)), pltpu.SemaphoreType.DMA((9,)),
                                  pltpu.SemaphoreType.DMA((2,))],
                  name="gather_weights")(big, small)


def _swap_sibling(a, name):
    def body(a_ref, o_ref, send_sem, recv_sem):
        x, y, c = _coords()
        cp = pltpu.make_async_remote_copy(src_ref=a_ref, dst_ref=o_ref, send_sem=send_sem,
                                          recv_sem=recv_sem, device_id=(x, y, 1 - c),
                                          device_id_type=MESH)
        cp.start()
        cp.wait()

    return _pcall(body, in_specs=[ANY], out_specs=ANY, out_shape=SDS(a.shape, a.dtype),
                  scratch_shapes=[pltpu.SemaphoreType.DMA, pltpu.SemaphoreType.DMA], name=name)(a)


def _scatter_chips(h):
    def body(h_ref, o_ref, send_sems, recv_sems):
        x, y, c = _coords()
        cps = []
        for j, (px, py) in enumerate(_other_chips(x, y)):
            cps.append(pltpu.make_async_remote_copy(
                src_ref=h_ref.at[2 * px + py], dst_ref=o_ref.at[j], send_sem=send_sems.at[j],
                recv_sem=recv_sems.at[j], device_id=(px, py, c), device_id_type=MESH))
        for cp in cps:
            cp.start()
        for cp in cps:
            cp.wait()

    return _pcall(body, in_specs=[ANY], out_specs=ANY, out_shape=SDS((3,) + h.shape[1:], h.dtype),
                  scratch_shapes=[pltpu.SemaphoreType.DMA((3,)), pltpu.SemaphoreType.DMA((3,))],
                  name="scatter_chips")(h)


def _add_n(arrs, name, dtypes=(F32,)):
    n, c = arrs[0].shape
    tr = 4112 if n % 4112 == 0 else n
    spec = pl.BlockSpec((tr, c), lambda i: (i, 0))

    def fn(*vals):
        acc = vals[0].astype(F32)
        for v in vals[1:]:
            acc = acc + v.astype(F32)
        return (acc,) * len(dtypes)

    return _ew(fn, list(arrs), [spec] * len(arrs), [SDS((n, c), d) for d in dtypes],
               [spec] * len(dtypes), (n // tr,), name)


def _allreduce_small(buf):
    rows = buf.shape[0]

    def body(b_ref, o_ref, g_ref, send_sems, recv_sems):
        x, y, c = _coords()
        me = 4 * x + 2 * y + c
        g_ref[me] = b_ref[...]
        cps = []
        for k in range(1, 8):
            px = 1 - x if k & 4 else x
            py = 1 - y if k & 2 else y
            pc = 1 - c if k & 1 else c
            cps.append(pltpu.make_async_remote_copy(
                src_ref=b_ref, dst_ref=g_ref.at[me], send_sem=send_sems.at[k - 1],
                recv_sem=recv_sems.at[k - 1], device_id=(px, py, pc), device_id_type=MESH))
        for cp in cps:
            cp.start()
        for cp in cps:
            cp.wait()
        acc = g_ref[0]
        for d in range(1, 8):
            acc = acc + g_ref[d]
        o_ref[...] = acc

    vm = pl.BlockSpec(memory_space=pltpu.VMEM)
    return _pcall(body, in_specs=[vm], out_specs=vm, out_shape=SDS(buf.shape, F32),
                  scratch_shapes=[pltpu.VMEM((8, rows, 128), F32), pltpu.SemaphoreType.DMA((7,)),
                                  pltpu.SemaphoreType.DMA((7,))],
                  compiler_params=pltpu.CompilerParams(vmem_limit_bytes=VMEM_LIMIT),
                  name="allreduce_small")(buf)


def _flat128(a):
    return a.reshape(-1, 128)


def _pad_rows(a, mult):
    r = (-a.shape[0]) % mult
    return a if r == 0 else jnp.concatenate([a, jnp.zeros((r, a.shape[1]), a.dtype)], axis=0)


def _pack_flat(parts, mult):
    rows, offs, r0 = [], [], 0
    for p in parts:
        f = p.reshape(-1)
        pad = (-f.shape[0]) % 128
        if pad:
            f = jnp.concatenate([f, jnp.zeros((pad,), f.dtype)])
        f = f.reshape(-1, 128)
        rows.append(f)
        offs.append((r0, p.shape))
        r0 += f.shape[0]
    return _pad_rows(jnp.concatenate(rows, axis=0), mult), offs


def _unpack_flat(buf, offs):
    out = []
    for r0, shape in offs:
        n = 1
        for s in shape:
            n *= s
        nr = -(-n // 128)
        out.append(buf[r0:r0 + nr].reshape(-1)[:n].reshape(shape))
    return out


def _gates_to_rows(a, heads, chunk, cpb):
    t = a.shape[0]
    return a.reshape(t // (chunk * cpb), cpb, chunk, heads).transpose(0, 3, 1, 2)


def _rows_to_gates(a):
    nb, heads, cpb, chunk = a.shape
    return a.transpose(0, 2, 3, 1).reshape(nb * cpb * chunk, heads)


def kernel(x, norm_w, gdn_w_in, gdn_conv_w, gdn_a_log, gdn_dt_bias, gdn_norm_w, gdn_w_out, ssd_w_in, ssd_conv_w, ssd_conv_b, ssd_dt_bias, ssd_a_log, ssd_d, ssd_norm_w, ssd_w_out, final_norm_w, loss_target, m_norm_w, m_gdn_w_in, m_gdn_conv_w, m_gdn_a_log, m_gdn_dt_bias, m_gdn_norm_w, m_gdn_w_out, m_ssd_w_in, m_ssd_conv_w, m_ssd_conv_b, m_ssd_dt_bias, m_ssd_a_log, m_ssd_d, m_ssd_norm_w, m_ssd_w_out, m_final_norm_w, v_norm_w, v_gdn_w_in, v_gdn_conv_w, v_gdn_a_log, v_gdn_dt_bias, v_gdn_norm_w, v_gdn_w_out, v_ssd_w_in, v_ssd_conv_w, v_ssd_conv_b, v_ssd_dt_bias, v_ssd_a_log, v_ssd_d, v_ssd_norm_w, v_ssd_w_out, v_final_norm_w):
    ws = dict(norm_w=norm_w, gdn_w_in=gdn_w_in, gdn_conv_w=gdn_conv_w, gdn_a_log=gdn_a_log,
              gdn_dt_bias=gdn_dt_bias, gdn_norm_w=gdn_norm_w, gdn_w_out=gdn_w_out, ssd_w_in=ssd_w_in,
              ssd_conv_w=ssd_conv_w, ssd_conv_b=ssd_conv_b, ssd_dt_bias=ssd_dt_bias,
              ssd_a_log=ssd_a_log, ssd_d=ssd_d, ssd_norm_w=ssd_norm_w, ssd_w_out=ssd_w_out,
              final_norm_w=final_norm_w)
    ms = dict(norm_w=m_norm_w, gdn_w_in=m_gdn_w_in, gdn_conv_w=m_gdn_conv_w, gdn_a_log=m_gdn_a_log,
              gdn_dt_bias=m_gdn_dt_bias, gdn_norm_w=m_gdn_norm_w, gdn_w_out=m_gdn_w_out,
              ssd_w_in=m_ssd_w_in, ssd_conv_w=m_ssd_conv_w, ssd_conv_b=m_ssd_conv_b,
              ssd_dt_bias=m_ssd_dt_bias, ssd_a_log=m_ssd_a_log, ssd_d=m_ssd_d,
              ssd_norm_w=m_ssd_norm_w, ssd_w_out=m_ssd_w_out, final_norm_w=m_final_norm_w)
    vs = dict(norm_w=v_norm_w, gdn_w_in=v_gdn_w_in, gdn_conv_w=v_gdn_conv_w, gdn_a_log=v_gdn_a_log,
              gdn_dt_bias=v_gdn_dt_bias, gdn_norm_w=v_gdn_norm_w, gdn_w_out=v_gdn_w_out,
              ssd_w_in=v_ssd_w_in, ssd_conv_w=v_ssd_conv_w, ssd_conv_b=v_ssd_conv_b,
              ssd_dt_bias=v_ssd_dt_bias, ssd_a_log=v_ssd_a_log, ssd_d=v_ssd_d,
              ssd_norm_w=v_ssd_norm_w, ssd_w_out=v_ssd_w_out, final_norm_w=v_final_norm_w)
    names = list(ws)
    cx, cy, cc = _coords()
    chip = 2 * cx + cy
    t = x.shape[1]
    x0 = x.reshape(t, D)
    tgt = loss_target.reshape(t, D)

    big_parts = [gdn_w_in[0], gdn_w_out[0], ssd_w_in[0], ssd_w_out[0]]
    big = jnp.concatenate([_flat128(p.astype(BF16)) for p in big_parts], axis=0)
    small, small_offs = _pack_flat([gdn_conv_w[0], ssd_conv_w[0], ssd_conv_b[0], ssd_norm_w[0]], 8)
    gbig, gsmall = _gather_weights(big, small)
    r_in = D * IN_SHARD // 128
    r_out = OUT_SHARD * D // 128
    offs, r0 = [], 0
    for n in (r_in, r_out, r_in, r_out):
        offs.append(r0)
        r0 += n
    w_gi = jnp.concatenate([gbig[b, offs[0]:offs[0] + r_in].reshape(D, IN_SHARD) for b in range(4)], axis=1)
    w_go = jnp.concatenate([gbig[b, offs[1]:offs[1] + r_out].reshape(OUT_SHARD, D) for b in range(4)], axis=0)
    w_si = jnp.concatenate([gbig[b, offs[2]:offs[2] + r_in].reshape(D, IN_SHARD) for b in range(4)], axis=1)
    w_so = jnp.concatenate([gbig[b, offs[3]:offs[3] + r_out].reshape(OUT_SHARD, D) for b in range(4)], axis=0)
    sm = [_unpack_flat(gsmall[b], small_offs) for b in range(4)]
    g_cw = jnp.concatenate([sm[b][0] for b in range(4)], axis=1)
    s_cw = jnp.concatenate([sm[b][1] for b in range(4)], axis=1)
    s_cb = jnp.concatenate([sm[b][2] for b in range(4)], axis=0)[None]
    s_nw = jnp.concatenate([sm[b][3] for b in range(4)], axis=0)[None]

    def pad_small(w):
        return jnp.concatenate([w, jnp.zeros((D, SMALL_W - w.shape[1]), w.dtype)], axis=1)

    wg_main, wg_small = w_gi[:, :G_MAIN], pad_small(w_gi[:, G_MAIN:])
    ws_main, ws_small = w_si[:, :S_MAIN], pad_small(w_si[:, S_MAIN:])
    zero_b = jnp.zeros((1, G_CONV), F32)
    nw0, nw1 = norm_w[0:1], norm_w[1:2]
    fw = final_norm_w[None]
    g_alog = gdn_a_log.reshape(GH, 1, 1)
    g_dtb = gdn_dt_bias.reshape(GH, 1, 1)
    g_nw = gdn_norm_w.reshape(1, GDK)
    s_dtb = ssd_dt_bias.reshape(SG, SR).T.reshape(SR, SG, 1, 1)
    s_alog = ssd_a_log.reshape(SG, SR).T.reshape(SR, SG, 1, 1)
    s_d = ssd_d.reshape(SG, SR).T.reshape(SR, SG, 1, 1)

    hid0 = _rms_fwd(x0, nw0, "rms0")
    pg = _matmul(hid0, wg_main, "nn", "gdn_in_proj", tn=1536)
    pg_small = _matmul(hid0, wg_small, "nn", "gdn_in_proj_small", tn=SMALL_W)
    post_q = _l2norm_scaled(GDK ** -0.5)
    post_k = _l2norm_scaled(1.0)
    q = _conv_fwd(pg, 0, g_cw[:, :G_QK], zero_b[:, :G_QK], GDK, post_q, "gdn_conv_q")
    k = _conv_fwd(pg, G_QK, g_cw[:, G_QK:2 * G_QK], zero_b[:, :G_QK], GDK, post_k, "gdn_conv_k")
    v = _conv_fwd(pg, 2 * G_QK, g_cw[:, 2 * G_QK:], zero_b[:, :G_V], GDK, _silu, "gdn_conv_v")
    braw = _gates_to_rows(pg_small[:, :GH], GH, GCH, GDN_CPB)
    araw = _gates_to_rows(pg_small[:, GH:2 * GH], GH, GCH, GDN_CPB)
    o, g_sall = _gdn_fwd(q, k, v, braw, araw, g_alog, g_dtb)
    y0 = _gate_fwd(_gdn_gate, o, pg, G_CONV, g_nw, "gdn_gate")
    x1 = _matmul(y0, w_go, "nn", "gdn_out_proj", add=x0)

    hid1 = _rms_fwd(x1, nw1, "rms1")
    ps = _matmul(hid1, ws_main, "nn", "ssd_in_proj", tn=1536)
    ps_small = _matmul(hid1, ws_small, "nn", "ssd_in_proj_small", tn=SMALL_W)
    c_x, c_b, c_c = S_INNER, 2 * S_INNER, 2 * S_INNER + SG * SN
    post_s = _silu
    xs = _conv_fwd(ps, c_x, s_cw[:, :S_INNER], s_cb[:, :S_INNER], SR * SP, post_s, "ssd_conv_x")
    bm = _conv_fwd(ps, c_b, s_cw[:, S_INNER:S_INNER + SG * SN], s_cb[:, S_INNER:S_INNER + SG * SN], SN,
                   post_s, "ssd_conv_b")
    cm = _conv_fwd(ps, c_c, s_cw[:, S_INNER + SG * SN:], s_cb[:, S_INNER + SG * SN:], SN, post_s,
                   "ssd_conv_c")
    nbs = t // (SCH * SSD_CPB)
    dtraw = _gates_to_rows(ps_small[:, :SH], SH, SCH, SSD_CPB)
    dtraw = dtraw.reshape(nbs, SG, SR, SSD_CPB, SCH).transpose(0, 3, 2, 1, 4).reshape(nbs, SSD_CPB * SR, SG, 1, SCH)
    yss, s_sall = _ssd_fwd(xs, bm, cm, dtraw, s_dtb, s_alog, s_d)
    y1 = _gate_fwd(_ssd_gate, yss, ps, 0, s_nw, "ssd_gate")
    x2 = _matmul(y1, w_so, "nn", "ssd_out_proj", add=x1)

    dx2, d_fw, loss_row = _final_loss(x2, fw, tgt, "final_loss")

    dy1 = _matmul(dx2, w_so, "nt", "ssd_out_dx", out_dtype=BF16, tn=2048)
    d_wso = _matmul(y1, dx2, "tn", "ssd_out_dw")
    dyss, dz_s, d_snw = _gate_bwd(_ssd_gate, yss, ps, 0, s_nw, dy1, True, "ssd_gate_bwd")
    dxs, dbm, dcm, ddtraw, d_sdtb, d_salog, d_sd = _ssd_bwd(xs, bm, cm, dtraw, s_dtb, s_alog, s_d, s_sall, dyss)
    dps, dwx, dbx = _conv_dpre(ps, c_x, s_cw[:, :S_INNER], s_cb[:, :S_INNER], SR * SP, post_s, dxs, "ssd_dpre_x",
                               G_CONV, 0)
    dps, dwb, dbb = _conv_dpre(ps, c_b, s_cw[:, S_INNER:S_INNER + SG * SN], s_cb[:, S_INNER:S_INNER + SG * SN],
                               SN, post_s, dbm, "ssd_dpre_b", G_CONV, S_INNER, dps)
    dps, dwc, dbc = _conv_dpre(ps, c_c, s_cw[:, S_INNER + SG * SN:], s_cb[:, S_INNER + SG * SN:], SN, post_s,
                               dcm, "ssd_dpre_c", G_CONV, S_INNER + SG * SN, dps)
    d_scw = jnp.concatenate([dwx, dwb, dwc], axis=1)
    d_scb = jnp.concatenate([dbx, dbb, dbc], axis=1)
    dxbc = _conv_t(dps, s_cw, "ssd_conv_t")
    ddt = ddtraw.reshape(nbs, SSD_CPB, SR, SG, SCH).transpose(0, 3, 2, 1, 4).reshape(nbs, SH, SSD_CPB, SCH)
    ddt = _rows_to_gates(ddt)
    dsm_s = jnp.concatenate([ddt, jnp.zeros((t, SMALL_W - SH), F32)], axis=1).astype(BF16)
    dhid1 = _matmul(dz_s, ws_main[:, :S_INNER], "nt", "ssd_in_dx_z")
    dhid1 = _matmul(dxbc, ws_main[:, S_INNER:], "nt", "ssd_in_dx_xbc", add=dhid1)
    dhid1 = _matmul(dsm_s, ws_small, "nt", "ssd_in_dx_dt", add=dhid1, tk=SMALL_W)
    d_wsi = jnp.concatenate([_matmul(hid1, dz_s, "tn", "ssd_in_dw_z"),
                             _matmul(hid1, dxbc, "tn", "ssd_in_dw_xbc"),
                             _matmul(hid1, dsm_s, "tn", "ssd_in_dw_dt", tn=SMALL_W)[:, :SH]], axis=1)
    dx1, d_nw1 = _rms_bwd(x1, nw1, dx2, dhid1, "rms1_bwd")

    dy0 = _matmul(dx1, w_go, "nt", "gdn_out_dx", out_dtype=BF16, tn=2048)
    d_wgo = _matmul(y0, dx1, "tn", "gdn_out_dw")
    do, dz_g, d_gnw = _gate_bwd(_gdn_gate, o, pg, G_CONV, g_nw, dy0, False, "gdn_gate_bwd")
    dq, dk, dv, dbraw, daraw, d_galog, d_gdtb = _gdn_bwd(q, k, v, braw, araw, g_alog, g_dtb, g_sall, do)
    dpg, dwq, _ = _conv_dpre(pg, 0, g_cw[:, :G_QK], zero_b[:, :G_QK], GDK, post_q, dq, "gdn_dpre_q", G_CONV, 0)
    dpg, dwk, _ = _conv_dpre(pg, G_QK, g_cw[:, G_QK:2 * G_QK], zero_b[:, :G_QK], GDK, post_k, dk, "gdn_dpre_k",
                             G_CONV, G_QK, dpg)
    dpg, dwv, _ = _conv_dpre(pg, 2 * G_QK, g_cw[:, 2 * G_QK:], zero_b[:, :G_V], GDK, _silu, dv, "gdn_dpre_v",
                             G_CONV, 2 * G_QK, dpg)
    d_gcw = jnp.concatenate([dwq, dwk, dwv], axis=1)
    dqkv = _conv_t(dpg, g_cw, "gdn_conv_t")
    dsm_g = jnp.concatenate([_rows_to_gates(dbraw), _rows_to_gates(daraw),
                             jnp.zeros((t, SMALL_W - 2 * GH), F32)], axis=1).astype(BF16)
    dhid0 = _matmul(dqkv, wg_main[:, :G_CONV], "nt", "gdn_in_dx_qkv")
    dhid0 = _matmul(dz_g, wg_main[:, G_CONV:], "nt", "gdn_in_dx_z", add=dhid0)
    dhid0 = _matmul(dsm_g, wg_small, "nt", "gdn_in_dx_ba", add=dhid0, tk=SMALL_W)
    d_wgi = jnp.concatenate([_matmul(hid0, dqkv, "tn", "gdn_in_dw_qkv"),
                             _matmul(hid0, dz_g, "tn", "gdn_in_dw_z"),
                             _matmul(hid0, dsm_g, "tn", "gdn_in_dw_ba", tn=SMALL_W)[:, :2 * GH]], axis=1)
    dx0, d_nw0 = _rms_bwd(x0, nw0, dx1, dhid0, "rms0_bwd")

    blocks = []
    for b in range(4):
        blocks.append(jnp.concatenate([
            _flat128(d_wgi[:, b * IN_SHARD:(b + 1) * IN_SHARD]), _flat128(d_wgo[b * OUT_SHARD:(b + 1) * OUT_SHARD]),
            _flat128(d_wsi[:, b * IN_SHARD:(b + 1) * IN_SHARD]), _flat128(d_wso[b * OUT_SHARD:(b + 1) * OUT_SHARD])],
            axis=0))
    rows = blocks[0].shape[0]
    half = rows // 2
    g4 = jnp.stack(blocks).reshape(4, 2, half, 128)
    g_mine = lax.dynamic_index_in_dim(g4, cc, axis=1, keepdims=False).reshape(4 * half, 128)
    g_other = lax.dynamic_index_in_dim(g4, 1 - cc, axis=1, keepdims=False).reshape(4 * half, 128)
    from_sib = _swap_sibling(g_other, "swap_halves")
    hsum, hsum_bf = _add_n([g_mine, from_sib], "sum_cores", (F32, BF16))
    recv = _scatter_chips(hsum_bf.reshape(4, half, 128))
    own = lax.dynamic_index_in_dim(hsum.reshape(4, half, 128), chip, axis=0, keepdims=False)
    tot = _add_n([own, recv[0], recv[1], recv[2]], "sum_chips")[0]
    tot_sib = _swap_sibling(tot, "swap_totals")
    lo = jnp.where(cc == 0, tot, tot_sib)
    hi = jnp.where(cc == 0, tot_sib, tot)
    full = jnp.concatenate([lo, hi], axis=0)
    grads = dict(
        gdn_w_in=full[offs[0]:offs[0] + r_in].reshape(1, D, IN_SHARD),
        gdn_w_out=full[offs[1]:offs[1] + r_out].reshape(1, OUT_SHARD, D),
        ssd_w_in=full[offs[2]:offs[2] + r_in].reshape(1, D, IN_SHARD),
        ssd_w_out=full[offs[3]:offs[3] + r_out].reshape(1, OUT_SHARD, D))

    small_parts = [loss_row, jnp.concatenate([d_nw0, d_nw1], axis=0), d_gcw, d_galog, d_gdtb, d_gnw, d_scw, d_scb,
                   d_sdtb, d_salog, d_sd, d_snw, d_fw]
    sbuf, soffs = _pack_flat(small_parts, 8)
    ssum = _unpack_flat(_allreduce_small(sbuf), soffs)
    (loss_s, g_nw_all, g_gcw, g_galog, g_gdtb, g_gnw, g_scw, g_scb, g_sdtb, g_salog, g_sd, g_snw, g_fw) = ssum

    def my_cols(a, width):
        return lax.dynamic_slice_in_dim(a, chip * width, width, axis=a.ndim - 1)

    grads.update(
        norm_w=g_nw_all, gdn_conv_w=my_cols(g_gcw, 1024)[None], gdn_a_log=g_galog.reshape(1, GH),
        gdn_dt_bias=g_gdtb.reshape(1, GH), gdn_norm_w=g_gnw.reshape(1, GDK),
        ssd_conv_w=my_cols(g_scw, 1024)[None], ssd_conv_b=my_cols(g_scb, 1024),
        ssd_dt_bias=g_sdtb.reshape(SR, SG).T.reshape(1, SH), ssd_a_log=g_salog.reshape(SR, SG).T.reshape(1, SH),
        ssd_d=g_sd.reshape(SR, SG).T.reshape(1, SH),
        ssd_norm_w=my_cols(g_snw, 512), final_norm_w=g_fw.reshape(D))
    loss = loss_s[0, 0]

    big_names = ("gdn_w_in", "gdn_w_out", "ssd_w_in", "ssd_w_out")
    deltas, new_m, new_v = {}, {}, {}
    for n in big_names:
        deltas[n], new_m[n], new_v[n] = _adamw(ws[n], grads[n], ms[n], vs[n], "adamw_" + n)
    rest = [n for n in names if n not in big_names]
    packs = [_pack_flat([d[n] for n in rest], 8) for d in (ws, grads, ms, vs)]
    outs = _adamw(*[p[0] for p in packs], "adamw_small")
    for d, buf in zip((deltas, new_m, new_v), outs):
        for n, a in zip(rest, _unpack_flat(buf, packs[0][1])):
            d[n] = a

    grad_x = dx0.reshape(1, t, D)
    return (loss, grad_x, *[grads[n] for n in names], *[deltas[n] for n in names],
            *[new_m[n] for n in names], *[new_v[n] for n in names])
```

```python
import functools

import jax
import jax.numpy as jnp
from jax import lax
from jax.experimental import pallas as pl
from jax.experimental.pallas import tpu as pltpu

F32 = jnp.float32
BF16 = jnp.bfloat16
SDS = jax.ShapeDtypeStruct
MESH = pl.DeviceIdType.MESH

D = 1024
EPS = 1e-6
CONV_K = 4
N_CHIPS = 4
GH = 16
GHQ = 8
GDK = 128
GCH = 64
G_QK = 1024
G_V = 2048
G_CONV = 4096
G_MAIN = 6144
G_IN = 6176
SH = 32
SP = 64
SN = 128
SG = 8
SR = 4
SCH = 128
S_INNER = 2048
S_MAIN = 6144
S_IN = 6176
IN_SHARD = 1544
OUT_SHARD = 512
SMALL_W = 128

ADAM_LR = 0.001
ADAM_B1 = 0.9
ADAM_B2 = 0.999
ADAM_EPS = 1e-08
ADAM_WD = 0.01
ADAM_STEP = 10

VMEM_LIMIT = 48 * 1024 * 1024
BLOCK_ELEMS = 256 * 1024
NEG = -1e30


def _pcall(body, **kw):
    return pl.pallas_call(body, **kw)


def _cp(sem=None, vmem=VMEM_LIMIT):
    return pltpu.CompilerParams(dimension_semantics=sem, vmem_limit_bytes=vmem)


def _sigmoid(x):
    return 1.0 / (1.0 + jnp.exp(-x))


def _silu(x):
    return x * _sigmoid(x)


def _softplus(x):
    return jnp.maximum(x, 0.0) + jnp.log(1.0 + jnp.exp(-jnp.abs(x)))


def _rms(x, w):
    return x * lax.rsqrt(jnp.mean(x * x, axis=-1, keepdims=True) + EPS) * w


_DIMS = {"nn": (((2,), (1,)), ((0,), (0,))),
         "nt": (((2,), (2,)), ((0,), (0,))),
         "tn": (((1,), (1,)), ((0,), (0,)))}


def _bdot(a, b, spec):
    return lax.dot_general(a.astype(BF16), b.astype(BF16), _DIMS[spec], preferred_element_type=F32)


@functools.partial(jax.custom_vjp, nondiff_argnums=(2,))
def _bmm(a, b, spec):
    return _bdot(a, b, spec)


def _bmm_fwd(a, b, spec):
    return _bdot(a, b, spec), (a, b)


def _bmm_bwd(spec, res, g):
    a, b = res
    if spec == "nn":
        return _bdot(g, b, "nt"), _bdot(a, g, "tn")
    if spec == "nt":
        return _bdot(g, b, "nn"), _bdot(g, a, "tn")
    return _bdot(b, g, "nt"), _bdot(a, g, "nn")


_bmm.defvjp(_bmm_fwd, _bmm_bwd)


@jax.custom_vjp
def _tri_inv(n):
    t = -n
    p = n
    steps = (n.shape[-1] - 1).bit_length() - 1
    r = lax.broadcasted_iota(jnp.int32, n.shape, 1)
    c = lax.broadcasted_iota(jnp.int32, n.shape, 2)
    t = t + jnp.where(r == c, 1.0, 0.0)
    for _ in range(steps):
        p = _bdot(p, p, "nn")
        t = t + _bdot(t, p, "nn")
    return t


def _tri_inv_fwd(n):
    t = _tri_inv(n)
    return t, t


def _tri_inv_bwd(t, g):
    return (-_bdot(_bdot(t, g, "tn"), t, "nt"),)


_tri_inv.defvjp(_tri_inv_fwd, _tri_inv_bwd)


@jax.custom_vjp
def _tri_inv_known(n, t):
    del n
    return t


def _tri_inv_known_fwd(n, t):
    del n
    return t, t


def _tri_inv_known_bwd(t, g):
    return _tri_inv_bwd(t, g)[0], jnp.zeros_like(t)


_tri_inv_known.defvjp(_tri_inv_known_fwd, _tri_inv_known_bwd)


def _masks(c, lead=1):
    r = lax.broadcasted_iota(jnp.int32, (lead, c, c), 1)
    s = lax.broadcasted_iota(jnp.int32, (lead, c, c), 2)
    return r >= s, r > s, r == s, r <= s


def _row_to_col(row, eye):
    return jnp.sum(jnp.where(eye, row, 0.0), axis=2, keepdims=True)


def _gdn_chunk(q, k, v, braw, araw, alog, dtb, s, t_known=None, want_t=False):
    h = v.shape[0]
    c = v.shape[1]
    rep = h // q.shape[0]
    tril, strict, eye, triu = _masks(c)
    qq = jnp.broadcast_to(q[:, None], (q.shape[0], rep) + q.shape[1:]).reshape(v.shape)
    kk = jnp.broadcast_to(k[:, None], (k.shape[0], rep) + k.shape[1:]).reshape(v.shape)
    beta_row = _sigmoid(braw)
    g_row = -jnp.exp(alog) * _softplus(araw + dtb)
    beta_col = _row_to_col(beta_row, eye)
    g_col = _row_to_col(g_row, eye)
    gc_col = jnp.sum(jnp.where(tril, g_row, 0.0), axis=2, keepdims=True)
    gc_row = jnp.sum(jnp.where(triu, g_col, 0.0), axis=1, keepdims=True)
    gc_last = jnp.sum(g_row, axis=2, keepdims=True)
    lmat = jnp.exp(jnp.where(tril, gc_col - gc_row, NEG))
    kb = kk * beta_col
    vb = v * beta_col
    n = jnp.where(strict, _bmm(kb, kk, "nt") * lmat, 0.0)
    t = _tri_inv(n) if t_known is None else _tri_inv_known(n, t_known)
    e_col = jnp.exp(gc_col)
    u = _bmm(t, vb, "nn")
    w = _bmm(t, kb * e_col, "nn")
    attn = _bmm(qq, kk, "nt") * lmat
    q_dec = qq * e_col
    k_dec = kk * jnp.exp(gc_last - gc_col)
    v_new = u - _bmm(w, s, "nn")
    o = _bmm(q_dec, s, "nn") + _bmm(attn, v_new, "nn")
    s_new = s * jnp.exp(gc_last) + _bmm(k_dec, v_new, "tn")
    return (o, s_new, t) if want_t else (o, s_new)


def _ssd_chunk(xs, bm, cm, dtraw, dtb, alog, dskip, s):
    c = xs.shape[1]
    tril, _, eye, triu = _masks(c)
    lane = lax.broadcasted_iota(jnp.int32, (1, 1, SR * SP), 2)
    prow = lax.broadcasted_iota(jnp.int32, (1, SR * SP, 1), 1)
    cb = _bmm(cm, bm, "nt")
    cs = _bmm(cm, s, "nt")

    def per_head(vals, idx):
        out = vals[SR - 1]
        for r in reversed(range(SR - 1)):
            out = jnp.where(idx < (r + 1) * SP, vals[r], out)
        return out

    dt_cols, e_cols, lmats, dstates, declast = [], [], [], [], []
    for r in range(SR):
        dt_row = _softplus(dtraw[r] + dtb[r])
        adt_row = -jnp.exp(alog[r]) * dt_row
        dt_cols.append(_row_to_col(dt_row, eye))
        adt_col = _row_to_col(adt_row, eye)
        acs_col = jnp.sum(jnp.where(tril, adt_row, 0.0), axis=2, keepdims=True)
        acs_row = jnp.sum(jnp.where(triu, adt_col, 0.0), axis=1, keepdims=True)
        acs_last = jnp.sum(adt_row, axis=2, keepdims=True)
        lmats.append(jnp.exp(jnp.where(tril, acs_col - acs_row, NEG)))
        e_cols.append(jnp.exp(acs_col))
        dstates.append(jnp.exp(acs_last - acs_col))
        declast.append(jnp.exp(acs_last))
    xd = xs * per_head(dt_cols, lane)
    y = per_head([_bmm(cb * lmats[r], xd, "nn") for r in range(SR)], lane)
    states = per_head([_bmm(xd, bm * dstates[r], "tn") for r in range(SR)], prow)
    y = y + cs * per_head(e_cols, lane) + xs * per_head([dskip[r] for r in range(SR)], lane)
    s_new = s * per_head(declast, prow) + states
    return y, s_new


def _matmul(a, b, mode, name, out_dtype=F32, add=None, tm=1024, tn=1024, tk=2048):
    if mode == "nn":
        (m, k), n = a.shape, b.shape[1]
    elif mode == "nt":
        (m, k), n = a.shape, b.shape[0]
    else:
        (k, m), n = a.shape, b.shape[1]
    tm, tn, tk = min(tm, m), min(tn, n), min(tk, k)
    assert m % tm == 0 and n % tn == 0 and k % tk == 0, (name, m, n, k)
    nk = k // tk
    dims = {"nn": (((1,), (0,)), ((), ())), "nt": (((1,), (1,)), ((), ())),
            "tn": (((0,), (0,)), ((), ()))}[mode]
    a_spec = {"nn": pl.BlockSpec((tm, tk), lambda i, j, kk: (i, kk)),
              "nt": pl.BlockSpec((tm, tk), lambda i, j, kk: (i, kk)),
              "tn": pl.BlockSpec((tk, tm), lambda i, j, kk: (kk, i))}[mode]
    b_spec = {"nn": pl.BlockSpec((tk, tn), lambda i, j, kk: (kk, j)),
              "nt": pl.BlockSpec((tn, tk), lambda i, j, kk: (j, kk)),
              "tn": pl.BlockSpec((tk, tn), lambda i, j, kk: (kk, j))}[mode]
    o_spec = pl.BlockSpec((tm, tn), lambda i, j, kk: (i, j))
    has_add = add is not None

    def body(*refs):
        a_ref, b_ref = refs[:2]
        add_ref = refs[2] if has_add else None
        o_ref = refs[2 + has_add]
        part = lax.dot_general(a_ref[...].astype(BF16), b_ref[...].astype(BF16), dims,
                               preferred_element_type=F32)

        def finish(r):
            if has_add:
                r = r + add_ref[...].astype(F32)
            o_ref[...] = r.astype(o_ref.dtype)

        if nk == 1:
            finish(part)
            return
        acc_ref = refs[3 + has_add]
        kk = pl.program_id(2)

        @pl.when(kk == 0)
        def _():
            acc_ref[...] = part

        @pl.when(jnp.logical_and(kk > 0, kk < nk - 1))
        def _():
            acc_ref[...] += part

        @pl.when(kk == nk - 1)
        def _():
            finish(acc_ref[...] + part)

    ins = [a, b] + ([add] if has_add else [])
    in_specs = [a_spec, b_spec] + ([o_spec] if has_add else [])
    scratch = [] if nk == 1 else [pltpu.VMEM((tm, tn), F32)]
    return _pcall(body, grid=(m // tm, n // tn, nk), in_specs=in_specs, out_specs=o_spec,
                  out_shape=SDS((m, n), out_dtype), scratch_shapes=scratch,
                  compiler_params=_cp(("parallel", "parallel", "arbitrary")), name=name)(*ins)


def _ew(fn, ins, in_specs, out_shape, out_specs, grid, name):
    n_in = len(ins)

    def body(*refs):
        outs = fn(*[r[...] for r in refs[:n_in]])
        for r, o in zip(refs[n_in:], outs):
            r[...] = o.astype(r.dtype)

    return _pcall(body, grid=grid, in_specs=in_specs, out_specs=out_specs, out_shape=out_shape,
                  compiler_params=_cp(("arbitrary",) * len(grid)), name=name)(*ins)


def _ew_vjp(fn, ins, in_specs, cts, ct_specs, wrt, g_shape, g_specs, acc, grid, name):
    n_in, n_ct = len(ins), len(cts)

    def body(*refs):
        vals = [r[...].astype(F32) for r in refs[:n_in]]
        outs, vjp = jax.vjp(fn, *vals)
        g_all = vjp(tuple(r[...].astype(F32) for r in refs[n_in:n_in + n_ct]))
        for pos, (i, g_ref) in enumerate(zip(wrt, refs[n_in + n_ct:])):
            g = g_all[i]
            if pos in acc:
                first = functools.reduce(
                    jnp.logical_and, [pl.program_id(ax) == 0 for ax in range(acc[pos], len(grid))])

                @pl.when(first)
                def _():
                    g_ref[...] = jnp.zeros_like(g_ref)

                g_ref[...] += g.astype(g_ref.dtype)
            else:
                g_ref[...] = g.astype(g_ref.dtype)

    return _pcall(body, grid=grid, in_specs=list(in_specs) + list(ct_specs), out_specs=g_specs,
                  out_shape=g_shape, compiler_params=_cp(("arbitrary",) * len(grid)),
                  name=name)(*ins, *cts)


def _row_spec(tm, n):
    return pl.BlockSpec((tm, n), lambda i: (i, 0))


def _par_spec(n):
    return pl.BlockSpec((1, n), lambda i: (0, 0))


def _rms_fwd(x, w, name):
    t = x.shape[0]
    tm = min(t, 512)
    return _ew(lambda xv, wv: (_rms(xv, wv),), [x, w], [_row_spec(tm, D), _par_spec(D)],
               [SDS((t, D), BF16)], [_row_spec(tm, D)], (t // tm,), name)[0]


def _rms_bwd(x, w, dres, dhid, name):
    t = x.shape[0]
    tm = min(t, 512)
    return _ew_vjp(lambda xv, wv: (xv, _rms(xv, wv)), [x, w], [_row_spec(tm, D), _par_spec(D)],
                   [dres, dhid], [_row_spec(tm, D), _row_spec(tm, D)], (0, 1),
                   [SDS((t, D), F32), SDS((1, D), F32)], [_row_spec(tm, D), _par_spec(D)],
                   {1: 0}, (t // tm,), name)


def _final_loss(x, w, tgt, name):
    t = x.shape[0]
    tm = min(t, 512)

    def body(x_ref, w_ref, t_ref, dx_ref, dw_ref, l_ref):
        @pl.when(pl.program_id(0) == 0)
        def _():
            dw_ref[...] = jnp.zeros_like(dw_ref)
            l_ref[...] = jnp.zeros_like(l_ref)

        xv, wv = x_ref[...], w_ref[...]
        rstd = lax.rsqrt(jnp.mean(xv * xv, axis=-1, keepdims=True) + EPS)
        xh = xv * rstd
        err = xh * wv - t_ref[...]
        l_ref[...] += 0.5 * jnp.sum(jnp.mean(err * err, axis=-1, keepdims=True), axis=0, keepdims=True)
        dy = err * (1.0 / D)
        dw_ref[...] += jnp.sum(dy * xh, axis=0, keepdims=True)
        dxh = dy * wv
        dx_ref[...] = rstd * (dxh - xh * jnp.mean(dxh * xh, axis=-1, keepdims=True))

    return _pcall(body, grid=(t // tm,), in_specs=[_row_spec(tm, D), _par_spec(D), _row_spec(tm, D)],
                  out_specs=[_row_spec(tm, D), _par_spec(D), _par_spec(128)],
                  out_shape=[SDS((t, D), F32), SDS((1, D), F32), SDS((1, 128), F32)],
                  compiler_params=_cp(("arbitrary",)), name=name)(x, w, tgt)


def _conv_taps(ext, w_ref, tm, lo):
    n = ext.shape[0]
    acc = None
    for j in range(CONV_K):
        shift = (CONV_K - 1 - j) if lo else (n - (CONV_K - 1 - j)) % n
        rolled = pltpu.roll(ext, shift, 0) if shift else ext
        term = w_ref[pl.ds(j, 1), :] * rolled[lo:lo + tm]
        acc = term if acc is None else acc + term
    return acc


def _conv_pre_specs(tm, ct, col0):
    hb = tm // 8
    return [pl.BlockSpec((tm, ct), lambda j, i: (i, col0 + j)),
            pl.BlockSpec((8, ct), lambda j, i: (jnp.maximum(i * hb - 1, 0), col0 + j)),
            pl.BlockSpec((CONV_K, ct), lambda j, i: (0, j)),
            pl.BlockSpec((1, ct), lambda j, i: (0, j))]


def _conv_pre_value(x_ref, xh_ref, w_ref, b_ref, tm):
    halo = jnp.where(pl.program_id(1) > 0, xh_ref[...], 0.0)
    ext = jnp.concatenate([halo, x_ref[...]], axis=0)
    return _conv_taps(ext, w_ref, tm, 8) + b_ref[...], ext


def _conv_fwd(x, col0, w, b, ct, post, name):
    t = x.shape[0]
    tm = min(t, BLOCK_ELEMS // ct)
    nt = w.shape[1] // ct

    def body(x_ref, xh_ref, w_ref, b_ref, o_ref):
        pre, _ = _conv_pre_value(x_ref, xh_ref, w_ref, b_ref, tm)
        o_ref[0] = post(pre)

    return _pcall(body, grid=(nt, t // tm), in_specs=_conv_pre_specs(tm, ct, col0 // ct),
                  out_specs=pl.BlockSpec((1, tm, ct), lambda j, i: (j, i, 0)),
                  out_shape=SDS((nt, t, ct), F32), compiler_params=_cp(("arbitrary", "arbitrary")),
                  name=name)(x, x, w, b)


def _conv_dpre(x, col0, w, b, ct, post, dout, name, c_total, c_off, into=None):
    t = x.shape[0]
    tm = min(t, BLOCK_ELEMS // ct)
    nt = w.shape[1] // ct
    chained = into is not None

    def body(*refs):
        x_ref, xh_ref, w_ref, b_ref, do_ref = refs[:5]
        dp_ref, dw_ref, db_ref = refs[5 + chained:]
        pre, ext = _conv_pre_value(x_ref, xh_ref, w_ref, b_ref, tm)
        _, vjp = jax.vjp(post, pre)
        dpre = vjp(do_ref[0])[0]
        dp_ref[...] = dpre

        @pl.when(pl.program_id(1) == 0)
        def _():
            dw_ref[...] = jnp.zeros_like(dw_ref)
            db_ref[...] = jnp.zeros_like(db_ref)

        for j in range(CONV_K):
            xs = (pltpu.roll(ext, CONV_K - 1 - j, 0) if j < CONV_K - 1 else ext)[8:8 + tm]
            dw_ref[pl.ds(j, 1), :] += jnp.sum(dpre * xs, axis=0, keepdims=True)
        db_ref[...] += jnp.sum(dpre, axis=0, keepdims=True)

    c = w.shape[1]
    o0 = c_off // ct
    return _pcall(body, grid=(nt, t // tm),
                  in_specs=_conv_pre_specs(tm, ct, col0 // ct)
                  + [pl.BlockSpec((1, tm, ct), lambda j, i: (j, i, 0))] + ([ANY] if chained else []),
                  out_specs=[pl.BlockSpec((tm, ct), lambda j, i: (i, o0 + j)),
                             pl.BlockSpec((CONV_K, ct), lambda j, i: (0, j)),
                             pl.BlockSpec((1, ct), lambda j, i: (0, j))],
                  out_shape=[SDS((t, c_total), F32), SDS((CONV_K, c), F32), SDS((1, c), F32)],
                  input_output_aliases={5: 0} if chained else {},
                  compiler_params=_cp(("arbitrary", "arbitrary")), name=name)(
                      x, x, w, b, dout, *([into] if chained else []))


def _conv_t(dpre, w, name):
    t, c = dpre.shape
    tm = min(t, 512)
    ct = min(c, 512)
    hb = tm // 8
    last = t // tm - 1

    def body(d_ref, dh_ref, w_ref, o_ref):
        halo = jnp.where(pl.program_id(1) < last, dh_ref[...], 0.0)
        ext = jnp.concatenate([d_ref[...], halo], axis=0)
        o_ref[...] = _conv_taps(ext, w_ref, tm, 0).astype(o_ref.dtype)

    return _pcall(body, grid=(c // ct, t // tm),
                  in_specs=[pl.BlockSpec((tm, ct), lambda j, i: (i, j)),
                            pl.BlockSpec((8, ct), lambda j, i: (jnp.minimum((i + 1) * hb, t // 8 - 1), j)),
                            pl.BlockSpec((CONV_K, ct), lambda j, i: (0, j))],
                  out_specs=pl.BlockSpec((tm, ct), lambda j, i: (i, j)),
                  out_shape=SDS((t, c), BF16), compiler_params=_cp(("arbitrary", "arbitrary")),
                  name=name)(dpre, dpre, w)


def _l2norm_scaled(scale):
    def post(pre):
        a = _silu(pre)
        return a * lax.rsqrt(jnp.sum(a * a, axis=-1, keepdims=True) + EPS) * scale
    return post


GDN_HB = 16
GDN_CPB = 2
SSD_GB = 8
SSD_CPB = 1


def _gdn_specs(nb, rev):
    hb, cpb, tc = GDN_HB, GDN_CPB, GDN_CPB * GCH
    blk = (lambda n: nb - 1 - n) if rev else (lambda n: n)
    seq = lambda h: pl.BlockSpec((h, tc, GDK), lambda g, n: (g, blk(n), 0))
    gate = pl.BlockSpec((1, hb, cpb, GCH), lambda g, n: (blk(n), g, 0, 0))
    par = pl.BlockSpec((hb, 1, 1), lambda g, n: (g, 0, 0))
    state = pl.BlockSpec((cpb, hb, GDK, GDK), lambda g, n: (blk(n), g, 0, 0))
    tinv = pl.BlockSpec((cpb, hb, GCH, GCH), lambda g, n: (blk(n), g, 0, 0))
    return seq, gate, par, state, tinv


def _gdn_fwd(q, k, v, braw, araw, alog, dtb):
    t = v.shape[1]
    hb, cpb = GDN_HB, GDN_CPB
    nb = t // (cpb * GCH)
    seq, gate, par, state, tinv = _gdn_specs(nb, False)

    def body(q_ref, k_ref, v_ref, b_ref, a_ref, al_ref, dt_ref, o_ref, sall_ref, tall_ref, s_ref):
        @pl.when(pl.program_id(1) == 0)
        def _():
            s_ref[...] = jnp.zeros_like(s_ref)

        s = s_ref[...]
        for c in range(cpb):
            rows = pl.ds(c * GCH, GCH)
            sall_ref[c] = s
            o, s, tmat = _gdn_chunk(q_ref[:, rows, :], k_ref[:, rows, :], v_ref[:, rows, :],
                                    b_ref[0, :, pl.ds(c, 1), :], a_ref[0, :, pl.ds(c, 1), :],
                                    al_ref[...], dt_ref[...], s, want_t=True)
            o_ref[:, rows, :] = o
            tall_ref[c] = tmat.astype(BF16)
        s_ref[...] = s

    return _pcall(body, grid=(GH // hb, nb),
                  in_specs=[seq(hb // 2), seq(hb // 2), seq(hb), gate, gate, par, par],
                  out_specs=[seq(hb), state, tinv],
                  out_shape=[SDS((GH, t, GDK), F32), SDS((t // GCH, GH, GDK, GDK), F32),
                             SDS((t // GCH, GH, GCH, GCH), BF16)],
                  scratch_shapes=[pltpu.VMEM((hb, GDK, GDK), F32)],
                  compiler_params=_cp(("arbitrary", "arbitrary")), name="gdn_chunk_fwd")(
                      q, k, v, braw, araw, alog, dtb)


def _gdn_bwd(q, k, v, braw, araw, alog, dtb, sall, tall, do):
    t = v.shape[1]
    hb, cpb = GDN_HB, GDN_CPB
    nb = t // (cpb * GCH)
    seq, gate, par, state, tinv = _gdn_specs(nb, True)

    def body(q_ref, k_ref, v_ref, b_ref, a_ref, al_ref, dt_ref, sall_ref, tall_ref, do_ref,
             dq_ref, dk_ref, dv_ref, db_ref, da_ref, dal_ref, ddt_ref, ds_ref):
        @pl.when(pl.program_id(1) == 0)
        def _():
            ds_ref[...] = jnp.zeros_like(ds_ref)
            dal_ref[...] = jnp.zeros_like(dal_ref)
            ddt_ref[...] = jnp.zeros_like(ddt_ref)

        ds = ds_ref[...]
        for c in reversed(range(cpb)):
            rows = pl.ds(c * GCH, GCH)
            fn = functools.partial(_gdn_chunk, t_known=tall_ref[c].astype(F32))
            _, vjp = jax.vjp(fn, q_ref[:, rows, :], k_ref[:, rows, :], v_ref[:, rows, :],
                             b_ref[0, :, pl.ds(c, 1), :], a_ref[0, :, pl.ds(c, 1), :],
                             al_ref[...], dt_ref[...], sall_ref[c])
            dq, dk, dv, db, da, dal, ddt, ds = vjp((do_ref[:, rows, :], ds))
            dq_ref[:, rows, :] = dq
            dk_ref[:, rows, :] = dk
            dv_ref[:, rows, :] = dv
            db_ref[0, :, pl.ds(c, 1), :] = db
            da_ref[0, :, pl.ds(c, 1), :] = da
            dal_ref[...] += dal
            ddt_ref[...] += ddt
        ds_ref[...] = ds

    return _pcall(body, grid=(GH // hb, nb),
                  in_specs=[seq(hb // 2), seq(hb // 2), seq(hb), gate, gate, par, par, state, tinv, seq(hb)],
                  out_specs=[seq(hb // 2), seq(hb // 2), seq(hb), gate, gate, par, par],
                  out_shape=[SDS(q.shape, F32), SDS(k.shape, F32), SDS(v.shape, F32),
                             SDS(braw.shape, F32), SDS(araw.shape, F32),
                             SDS((GH, 1, 1), F32), SDS((GH, 1, 1), F32)],
                  scratch_shapes=[pltpu.VMEM((hb, GDK, GDK), F32)],
                  compiler_params=_cp(("arbitrary", "arbitrary")), name="gdn_chunk_bwd")(
                      q, k, v, braw, araw, alog, dtb, sall, tall, do)


def _ssd_specs(nb, rev):
    gb, cpb, tc = SSD_GB, SSD_CPB, SSD_CPB * SCH
    blk = (lambda n: nb - 1 - n) if rev else (lambda n: n)
    seq = lambda w: pl.BlockSpec((gb, tc, w), lambda g, n: (g, blk(n), 0))
    gate = pl.BlockSpec((1, cpb * SR, gb, 1, SCH), lambda g, n: (blk(n), 0, g, 0, 0))
    par = pl.BlockSpec((SR, gb, 1, 1), lambda g, n: (0, g, 0, 0))
    state = pl.BlockSpec((cpb, gb, SR * SP, SN), lambda g, n: (blk(n), g, 0, 0))
    return seq, gate, par, state


def _ssd_fwd(xs, bm, cm, dtraw, dtb, alog, dskip):
    t = xs.shape[1]
    gb, cpb = SSD_GB, SSD_CPB
    nb = t // (cpb * SCH)
    seq, gate, par, state = _ssd_specs(nb, False)

    def body(x_ref, b_ref, c_ref, dt_ref, dtb_ref, al_ref, dk_ref, y_ref, sall_ref, s_ref):
        @pl.when(pl.program_id(1) == 0)
        def _():
            s_ref[...] = jnp.zeros_like(s_ref)

        s = s_ref[...]
        for c in range(cpb):
            rows = pl.ds(c * SCH, SCH)
            sall_ref[c] = s
            y, s = _ssd_chunk(x_ref[:, rows, :], b_ref[:, rows, :], c_ref[:, rows, :],
                              dt_ref[0, pl.ds(c * SR, SR)], dtb_ref[...], al_ref[...],
                              dk_ref[...], s)
            y_ref[:, rows, :] = y
        s_ref[...] = s

    return _pcall(body, grid=(SG // gb, nb),
                  in_specs=[seq(SR * SP), seq(SN), seq(SN), gate, par, par, par],
                  out_specs=[seq(SR * SP), state],
                  out_shape=[SDS((SG, t, SR * SP), F32), SDS((t // SCH, SG, SR * SP, SN), F32)],
                  scratch_shapes=[pltpu.VMEM((gb, SR * SP, SN), F32)],
                  compiler_params=_cp(("arbitrary", "arbitrary")), name="ssd_chunk_fwd")(
                      xs, bm, cm, dtraw, dtb, alog, dskip)


def _ssd_bwd(xs, bm, cm, dtraw, dtb, alog, dskip, sall, dy):
    t = xs.shape[1]
    gb, cpb = SSD_GB, SSD_CPB
    nb = t // (cpb * SCH)
    seq, gate, par, state = _ssd_specs(nb, True)

    def body(x_ref, b_ref, c_ref, dt_ref, dtb_ref, al_ref, dk_ref, sall_ref, dy_ref,
             dx_ref, dbm_ref, dcm_ref, ddt_ref, ddtb_ref, dal_ref, ddk_ref, ds_ref):
        @pl.when(pl.program_id(1) == 0)
        def _():
            ds_ref[...] = jnp.zeros_like(ds_ref)
            ddtb_ref[...] = jnp.zeros_like(ddtb_ref)
            dal_ref[...] = jnp.zeros_like(dal_ref)
            ddk_ref[...] = jnp.zeros_like(ddk_ref)

        ds = ds_ref[...]
        for c in reversed(range(cpb)):
            rows = pl.ds(c * SCH, SCH)
            _, vjp = jax.vjp(_ssd_chunk, x_ref[:, rows, :], b_ref[:, rows, :], c_ref[:, rows, :],
                             dt_ref[0, pl.ds(c * SR, SR)], dtb_ref[...], al_ref[...],
                             dk_ref[...], sall_ref[c])
            dx, dbm, dcm, ddt, ddtb, dal, ddk, ds = vjp((dy_ref[:, rows, :], ds))
            dx_ref[:, rows, :] = dx
            dbm_ref[:, rows, :] = dbm
            dcm_ref[:, rows, :] = dcm
            ddt_ref[0, pl.ds(c * SR, SR)] = ddt
            ddtb_ref[...] += ddtb
            dal_ref[...] += dal
            ddk_ref[...] += ddk
        ds_ref[...] = ds

    return _pcall(body, grid=(SG // gb, nb),
                  in_specs=[seq(SR * SP), seq(SN), seq(SN), gate, par, par, par, state, seq(SR * SP)],
                  out_specs=[seq(SR * SP), seq(SN), seq(SN), gate, par, par, par],
                  out_shape=[SDS(xs.shape, F32), SDS(bm.shape, F32), SDS(cm.shape, F32),
                             SDS(dtraw.shape, F32), SDS((SR, SG, 1, 1), F32), SDS((SR, SG, 1, 1), F32),
                             SDS((SR, SG, 1, 1), F32)],
                  scratch_shapes=[pltpu.VMEM((gb, SR * SP, SN), F32)],
                  compiler_params=_cp(("arbitrary", "arbitrary")), name="ssd_chunk_bwd")(
                      xs, bm, cm, dtraw, dtb, alog, dskip, sall, dy)


def _gate_specs(tm, ct, zcol0, per_tile_w):
    z0 = zcol0 // ct
    return [pl.BlockSpec((1, tm, ct), lambda i, j: (j, i, 0)),
            pl.BlockSpec((tm, ct), lambda i, j: (i, z0 + j)),
            pl.BlockSpec((1, ct), (lambda i, j: (0, j)) if per_tile_w else (lambda i, j: (0, 0)))]


def _gdn_gate(o, z, w):
    return (_rms(o[0], w) * _silu(z),)


def _ssd_gate(y, z, w):
    return (_rms(y[0] * _silu(z), w),)


def _gate_fwd(fn, o, proj, zcol0, w, name):
    nt, t, ct = o.shape
    tm = min(t, BLOCK_ELEMS // ct)
    return _ew(fn, [o, proj, w], _gate_specs(tm, ct, zcol0, w.shape[1] > ct), [SDS((t, nt * ct), BF16)],
               [pl.BlockSpec((tm, ct), lambda i, j: (i, j))], (t // tm, nt), name)[0]


def _gate_bwd(fn, o, proj, zcol0, w, dy, wacc, name):
    nt, t, ct = o.shape
    tm = min(t, BLOCK_ELEMS // ct)
    specs = _gate_specs(tm, ct, zcol0, wacc)
    out_spec = pl.BlockSpec((tm, ct), lambda i, j: (i, j))
    if wacc:
        flip = lambda s: pl.BlockSpec(s.block_shape, lambda j, i, f=s.index_map: f(i, j))
        specs = [flip(s) for s in specs]
        out_spec = flip(out_spec)
        grid, acc = (nt, t // tm), {2: 1}
    else:
        grid, acc = (t // tm, nt), {2: 0}
    return _ew_vjp(fn, [o, proj, w], specs, [dy], [out_spec], (0, 1, 2),
                   [SDS(o.shape, F32), SDS((t, nt * ct), BF16), SDS(w.shape, F32)],
                   [specs[0], out_spec, specs[2]], acc, grid, name)


def _adamw_math(w, g, m, v):
    m = ADAM_B1 * m + (1.0 - ADAM_B1) * g
    v = ADAM_B2 * v + (1.0 - ADAM_B2) * jnp.square(g)
    m_hat = m / (1.0 - ADAM_B1 ** ADAM_STEP)
    v_hat = v / (1.0 - ADAM_B2 ** ADAM_STEP)
    delta = -ADAM_LR * (m_hat / (jnp.sqrt(v_hat) + ADAM_EPS) + ADAM_WD * w)
    return delta, m, v


def _adamw(w, g, m, v, name):
    shape = w.shape
    w2, g2, m2, v2 = [a.reshape(-1, shape[-1]) for a in (w, g, m, v)]
    r, c = w2.shape
    tr = 256 if r % 256 == 0 else r
    spec = pl.BlockSpec((tr, c), lambda i: (i, 0))
    outs = _ew(_adamw_math, [w2, g2, m2, v2], [spec] * 4, [SDS((r, c), F32)] * 3, [spec] * 3,
               (r // tr,), name)
    return [o.reshape(shape) for o in outs]


def _coords():
    return lax.axis_index("x"), lax.axis_index("y"), lax.axis_index("c")


def _other_chips(x, y):
    return [(1 - x, y), (x, 1 - y), (1 - x, 1 - y)]


ANY = pl.BlockSpec(memory_space=pl.ANY)


def _gather_weights(big, small):
    r = big.shape[0]
    half = r // 2

    def body(big_ref, small_ref, obig_ref, osmall_ref, send_sems, recv_sems, local_sems):
        x, y, c = _coords()
        me = 2 * x + y
        sibling = (x, y, 1 - c)
        chips = _other_chips(x, y)
        mine = pl.ds(c * half, half)
        theirs = pl.ds((1 - c) * half, half)

        loc_b = pltpu.make_async_copy(big_ref, obig_ref.at[me], local_sems.at[0])
        loc_s = pltpu.make_async_copy(small_ref, osmall_ref.at[me], local_sems.at[1])
        loc_b.start()
        loc_s.start()

        def rcopy(src, dst, k, to):
            return pltpu.make_async_remote_copy(src_ref=src, dst_ref=dst, send_sem=send_sems.at[k],
                                                recv_sem=recv_sems.at[k], device_id=to,
                                                device_id_type=MESH)

        first = []
        for j, (px, py) in enumerate(chips):
            first.append(rcopy(big_ref.at[mine], obig_ref.at[me, mine], j, (px, py, c)))
            first.append(rcopy(small_ref, osmall_ref.at[me], 6 + j, (px, py, c)))
        for cp in first:
            cp.start()
        passed = []
        for j, (px, py) in enumerate(chips):
            src = 2 * px + py
            rcopy(big_ref.at[mine], obig_ref.at[src, mine], j, (px, py, c)).wait_recv()
            fw = rcopy(obig_ref.at[src, mine], obig_ref.at[src, mine], 3 + j, sibling)
            fw.start()
            passed.append(fw)
        for j, (px, py) in enumerate(chips):
            src = 2 * px + py
            rcopy(big_ref.at[theirs], obig_ref.at[src, theirs], 3 + j, sibling).wait_recv()
            rcopy(small_ref, osmall_ref.at[src], 6 + j, (px, py, c)).wait_recv()
        for cp in first + passed:
            cp.wait_send()
        loc_b.wait()
        loc_s.wait()

    return _pcall(body, in_specs=[ANY, ANY], out_specs=[ANY, ANY],
                  out_shape=[SDS((N_CHIPS,) + big.shape, big.dtype),
                             SDS((N_CHIPS,) + small.shape, small.dtype)],
                  scratch_shapes=[pltpu.SemaphoreType.DMA((9,)), pltpu.SemaphoreType.DMA((9,)),
                                  pltpu.SemaphoreType.DMA((2,))],
                  name="gather_weights")(big, small)


def _swap_sibling(a, name):
    def body(a_ref, o_ref, send_sem, recv_sem):
        x, y, c = _coords()
        cp = pltpu.make_async_remote_copy(src_ref=a_ref, dst_ref=o_ref, send_sem=send_sem,
                                          recv_sem=recv_sem, device_id=(x, y, 1 - c),
                                          device_id_type=MESH)
        cp.start()
        cp.wait()

    return _pcall(body, in_specs=[ANY], out_specs=ANY, out_shape=SDS(a.shape, a.dtype),
                  scratch_shapes=[pltpu.SemaphoreType.DMA, pltpu.SemaphoreType.DMA], name=name)(a)


def _scatter_chips(h):
    def body(h_ref, o_ref, send_sems, recv_sems):
        x, y, c = _coords()
        cps = []
        for j, (px, py) in enumerate(_other_chips(x, y)):
            cps.append(pltpu.make_async_remote_copy(
                src_ref=h_ref.at[2 * px + py], dst_ref=o_ref.at[j], send_sem=send_sems.at[j],
                recv_sem=recv_sems.at[j], device_id=(px, py, c), device_id_type=MESH))
        for cp in cps:
            cp.start()
        for cp in cps:
            cp.wait()

    return _pcall(body, in_specs=[ANY], out_specs=ANY, out_shape=SDS((3,) + h.shape[1:], h.dtype),
                  scratch_shapes=[pltpu.SemaphoreType.DMA((3,)), pltpu.SemaphoreType.DMA((3,))],
                  name="scatter_chips")(h)


def _add_n(arrs, name, dtypes=(F32,)):
    n, c = arrs[0].shape
    tr = 4112 if n % 4112 == 0 else n
    spec = pl.BlockSpec((tr, c), lambda i: (i, 0))

    def fn(*vals):
        acc = vals[0].astype(F32)
        for v in vals[1:]:
            acc = acc + v.astype(F32)
        return (acc,) * len(dtypes)

    return _ew(fn, list(arrs), [spec] * len(arrs), [SDS((n, c), d) for d in dtypes],
               [spec] * len(dtypes), (n // tr,), name)


def _allreduce_small(buf):
    rows = buf.shape[0]

    def body(b_ref, o_ref, g_ref, send_sems, recv_sems):
        x, y, c = _coords()
        me = 4 * x + 2 * y + c
        g_ref[me] = b_ref[...]
        cps = []
        for k in range(1, 8):
            px = 1 - x if k & 4 else x
            py = 1 - y if k & 2 else y
            pc = 1 - c if k & 1 else c
            cps.append(pltpu.make_async_remote_copy(
                src_ref=b_ref, dst_ref=g_ref.at[me], send_sem=send_sems.at[k - 1],
                recv_sem=recv_sems.at[k - 1], device_id=(px, py, pc), device_id_type=MESH))
        for cp in cps:
            cp.start()
        for cp in cps:
            cp.wait()
        acc = g_ref[0]
        for d in range(1, 8):
            acc = acc + g_ref[d]
        o_ref[...] = acc

    vm = pl.BlockSpec(memory_space=pltpu.VMEM)
    return _pcall(body, in_specs=[vm], out_specs=vm, out_shape=SDS(buf.shape, F32),
                  scratch_shapes=[pltpu.VMEM((8, rows, 128), F32), pltpu.SemaphoreType.DMA((7,)),
                                  pltpu.SemaphoreType.DMA((7,))],
                  compiler_params=pltpu.CompilerParams(vmem_limit_bytes=VMEM_LIMIT),
                  name="allreduce_small")(buf)


def _flat128(a):
    return a.reshape(-1, 128)


def _pad_rows(a, mult):
    r = (-a.shape[0]) % mult
    return a if r == 0 else jnp.concatenate([a, jnp.zeros((r, a.shape[1]), a.dtype)], axis=0)


def _pack_flat(parts, mult):
    rows, offs, r0 = [], [], 0
    for p in parts:
        f = p.reshape(-1)
        pad = (-f.shape[0]) % 128
        if pad:
            f = jnp.concatenate([f, jnp.zeros((pad,), f.dtype)])
        f = f.reshape(-1, 128)
        rows.append(f)
        offs.append((r0, p.shape))
        r0 += f.shape[0]
    return _pad_rows(jnp.concatenate(rows, axis=0), mult), offs


def _unpack_flat(buf, offs):
    out = []
    for r0, shape in offs:
        n = 1
        for s in shape:
            n *= s
        nr = -(-n // 128)
        out.append(buf[r0:r0 + nr].reshape(-1)[:n].reshape(shape))
    return out


def _gates_to_rows(a, heads, chunk, cpb):
    t = a.shape[0]
    return a.reshape(t // (chunk * cpb), cpb, chunk, heads).transpose(0, 3, 1, 2)


def _rows_to_gates(a):
    nb, heads, cpb, chunk = a.shape
    return a.transpose(0, 2, 3, 1).reshape(nb * cpb * chunk, heads)


def kernel(x, norm_w, gdn_w_in, gdn_conv_w, gdn_a_log, gdn_dt_bias, gdn_norm_w, gdn_w_out, ssd_w_in, ssd_conv_w, ssd_conv_b, ssd_dt_bias, ssd_a_log, ssd_d, ssd_norm_w, ssd_w_out, final_norm_w, loss_target, m_norm_w, m_gdn_w_in, m_gdn_conv_w, m_gdn_a_log, m_gdn_dt_bias, m_gdn_norm_w, m_gdn_w_out, m_ssd_w_in, m_ssd_conv_w, m_ssd_conv_b, m_ssd_dt_bias, m_ssd_a_log, m_ssd_d, m_ssd_norm_w, m_ssd_w_out, m_final_norm_w, v_norm_w, v_gdn_w_in, v_gdn_conv_w, v_gdn_a_log, v_gdn_dt_bias, v_gdn_norm_w, v_gdn_w_out, v_ssd_w_in, v_ssd_conv_w, v_ssd_conv_b, v_ssd_dt_bias, v_ssd_a_log, v_ssd_d, v_ssd_norm_w, v_ssd_w_out, v_final_norm_w):
    ws = dict(norm_w=norm_w, gdn_w_in=gdn_w_in, gdn_conv_w=gdn_conv_w, gdn_a_log=gdn_a_log,
              gdn_dt_bias=gdn_dt_bias, gdn_norm_w=gdn_norm_w, gdn_w_out=gdn_w_out, ssd_w_in=ssd_w_in,
              ssd_conv_w=ssd_conv_w, ssd_conv_b=ssd_conv_b, ssd_dt_bias=ssd_dt_bias,
              ssd_a_log=ssd_a_log, ssd_d=ssd_d, ssd_norm_w=ssd_norm_w, ssd_w_out=ssd_w_out,
              final_norm_w=final_norm_w)
    ms = dict(norm_w=m_norm_w, gdn_w_in=m_gdn_w_in, gdn_conv_w=m_gdn_conv_w, gdn_a_log=m_gdn_a_log,
              gdn_dt_bias=m_gdn_dt_bias, gdn_norm_w=m_gdn_norm_w, gdn_w_out=m_gdn_w_out,
              ssd_w_in=m_ssd_w_in, ssd_conv_w=m_ssd_conv_w, ssd_conv_b=m_ssd_conv_b,
              ssd_dt_bias=m_ssd_dt_bias, ssd_a_log=m_ssd_a_log, ssd_d=m_ssd_d,
              ssd_norm_w=m_ssd_norm_w, ssd_w_out=m_ssd_w_out, final_norm_w=m_final_norm_w)
    vs = dict(norm_w=v_norm_w, gdn_w_in=v_gdn_w_in, gdn_conv_w=v_gdn_conv_w, gdn_a_log=v_gdn_a_log,
              gdn_dt_bias=v_gdn_dt_bias, gdn_norm_w=v_gdn_norm_w, gdn_w_out=v_gdn_w_out,
              ssd_w_in=v_ssd_w_in, ssd_conv_w=v_ssd_conv_w, ssd_conv_b=v_ssd_conv_b,
              ssd_dt_bias=v_ssd_dt_bias, ssd_a_log=v_ssd_a_log, ssd_d=v_ssd_d,
              ssd_norm_w=v_ssd_norm_w, ssd_w_out=v_ssd_w_out, final_norm_w=v_final_norm_w)
    names = list(ws)
    cx, cy, cc = _coords()
    chip = 2 * cx + cy
    t = x.shape[1]
    x0 = x.reshape(t, D)
    tgt = loss_target.reshape(t, D)

    big_parts = [gdn_w_in[0], gdn_w_out[0], ssd_w_in[0], ssd_w_out[0]]
    big = jnp.concatenate([_flat128(p.astype(BF16)) for p in big_parts], axis=0)
    small, small_offs = _pack_flat([gdn_conv_w[0], ssd_conv_w[0], ssd_conv_b[0], ssd_norm_w[0]], 8)
    gbig, gsmall = _gather_weights(big, small)
    r_in = D * IN_SHARD // 128
    r_out = OUT_SHARD * D // 128
    offs, r0 = [], 0
    for n in (r_in, r_out, r_in, r_out):
        offs.append(r0)
        r0 += n
    w_gi = jnp.concatenate([gbig[b, offs[0]:offs[0] + r_in].reshape(D, IN_SHARD) for b in range(4)], axis=1)
    w_go = jnp.concatenate([gbig[b, offs[1]:offs[1] + r_out].reshape(OUT_SHARD, D) for b in range(4)], axis=0)
    w_si = jnp.concatenate([gbig[b, offs[2]:offs[2] + r_in].reshape(D, IN_SHARD) for b in range(4)], axis=1)
    w_so = jnp.concatenate([gbig[b, offs[3]:offs[3] + r_out].reshape(OUT_SHARD, D) for b in range(4)], axis=0)
    sm = [_unpack_flat(gsmall[b], small_offs) for b in range(4)]
    g_cw = jnp.concatenate([sm[b][0] for b in range(4)], axis=1)
    s_cw = jnp.concatenate([sm[b][1] for b in range(4)], axis=1)
    s_cb = jnp.concatenate([sm[b][2] for b in range(4)], axis=0)[None]
    s_nw = jnp.concatenate([sm[b][3] for b in range(4)], axis=0)[None]

    def pad_small(w):
        return jnp.concatenate([w, jnp.zeros((D, SMALL_W - w.shape[1]), w.dtype)], axis=1)

    wg_main, wg_small = w_gi[:, :G_MAIN], pad_small(w_gi[:, G_MAIN:])
    ws_main, ws_small = w_si[:, :S_MAIN], pad_small(w_si[:, S_MAIN:])
    zero_b = jnp.zeros((1, G_CONV), F32)
    nw0, nw1 = norm_w[0:1], norm_w[1:2]
    fw = final_norm_w[None]
    g_alog = gdn_a_log.reshape(GH, 1, 1)
    g_dtb = gdn_dt_bias.reshape(GH, 1, 1)
    g_nw = gdn_norm_w.reshape(1, GDK)
    s_dtb = ssd_dt_bias.reshape(SG, SR).T.reshape(SR, SG, 1, 1)
    s_alog = ssd_a_log.reshape(SG, SR).T.reshape(SR, SG, 1, 1)
    s_d = ssd_d.reshape(SG, SR).T.reshape(SR, SG, 1, 1)

    hid0 = _rms_fwd(x0, nw0, "rms0")
    pg = _matmul(hid0, wg_main, "nn", "gdn_in_proj", tn=1536)
    pg_small = _matmul(hid0, wg_small, "nn", "gdn_in_proj_small", tn=SMALL_W)
    post_q = _l2norm_scaled(GDK ** -0.5)
    post_k = _l2norm_scaled(1.0)
    q = _conv_fwd(pg, 0, g_cw[:, :G_QK], zero_b[:, :G_QK], GDK, post_q, "gdn_conv_q")
    k = _conv_fwd(pg, G_QK, g_cw[:, G_QK:2 * G_QK], zero_b[:, :G_QK], GDK, post_k, "gdn_conv_k")
    v = _conv_fwd(pg, 2 * G_QK, g_cw[:, 2 * G_QK:], zero_b[:, :G_V], GDK, _silu, "gdn_conv_v")
    braw = _gates_to_rows(pg_small[:, :GH], GH, GCH, GDN_CPB)
    araw = _gates_to_rows(pg_small[:, GH:2 * GH], GH, GCH, GDN_CPB)
    o, g_sall, g_tall = _gdn_fwd(q, k, v, braw, araw, g_alog, g_dtb)
    y0 = _gate_fwd(_gdn_gate, o, pg, G_CONV, g_nw, "gdn_gate")
    x1 = _matmul(y0, w_go, "nn", "gdn_out_proj", add=x0)

    hid1 = _rms_fwd(x1, nw1, "rms1")
    ps = _matmul(hid1, ws_main, "nn", "ssd_in_proj", tn=1536)
    ps_small = _matmul(hid1, ws_small, "nn", "ssd_in_proj_small", tn=SMALL_W)
    c_x, c_b, c_c = S_INNER, 2 * S_INNER, 2 * S_INNER + SG * SN
    post_s = _silu
    xs = _conv_fwd(ps, c_x, s_cw[:, :S_INNER], s_cb[:, :S_INNER], SR * SP, post_s, "ssd_conv_x")
    bm = _conv_fwd(ps, c_b, s_cw[:, S_INNER:S_INNER + SG * SN], s_cb[:, S_INNER:S_INNER + SG * SN], SN,
                   post_s, "ssd_conv_b")
    cm = _conv_fwd(ps, c_c, s_cw[:, S_INNER + SG * SN:], s_cb[:, S_INNER + SG * SN:], SN, post_s,
                   "ssd_conv_c")
    nbs = t // (SCH * SSD_CPB)
    dtraw = _gates_to_rows(ps_small[:, :SH], SH, SCH, SSD_CPB)
    dtraw = dtraw.reshape(nbs, SG, SR, SSD_CPB, SCH).transpose(0, 3, 2, 1, 4).reshape(nbs, SSD_CPB * SR, SG, 1, SCH)
    yss, s_sall = _ssd_fwd(xs, bm, cm, dtraw, s_dtb, s_alog, s_d)
    y1 = _gate_fwd(_ssd_gate, yss, ps, 0, s_nw, "ssd_gate")
    x2 = _matmul(y1, w_so, "nn", "ssd_out_proj", add=x1)

    dx2, d_fw, loss_row = _final_loss(x2, fw, tgt, "final_loss")

    dy1 = _matmul(dx2, w_so, "nt", "ssd_out_dx", out_dtype=BF16, tn=2048)
    d_wso = _matmul(y1, dx2, "tn", "ssd_out_dw")
    dyss, dz_s, d_snw = _gate_bwd(_ssd_gate, yss, ps, 0, s_nw, dy1, True, "ssd_gate_bwd")
    dxs, dbm, dcm, ddtraw, d_sdtb, d_salog, d_sd = _ssd_bwd(xs, bm, cm, dtraw, s_dtb, s_alog, s_d, s_sall, dyss)
    dps, dwx, dbx = _conv_dpre(ps, c_x, s_cw[:, :S_INNER], s_cb[:, :S_INNER], SR * SP, post_s, dxs, "ssd_dpre_x",
                               G_CONV, 0)
    dps, dwb, dbb = _conv_dpre(ps, c_b, s_cw[:, S_INNER:S_INNER + SG * SN], s_cb[:, S_INNER:S_INNER + SG * SN],
                               SN, post_s, dbm, "ssd_dpre_b", G_CONV, S_INNER, dps)
    dps, dwc, dbc = _conv_dpre(ps, c_c, s_cw[:, S_INNER + SG * SN:], s_cb[:, S_INNER + SG * SN:], SN, post_s,
                               dcm, "ssd_dpre_c", G_CONV, S_INNER + SG * SN, dps)
    d_scw = jnp.concatenate([dwx, dwb, dwc], axis=1)
    d_scb = jnp.concatenate([dbx, dbb, dbc], axis=1)
    dxbc = _conv_t(dps, s_cw, "ssd_conv_t")
    ddt = ddtraw.reshape(nbs, SSD_CPB, SR, SG, SCH).transpose(0, 3, 2, 1, 4).reshape(nbs, SH, SSD_CPB, SCH)
    ddt = _rows_to_gates(ddt)
    dsm_s = jnp.concatenate([ddt, jnp.zeros((t, SMALL_W - SH), F32)], axis=1).astype(BF16)
    dhid1 = _matmul(dz_s, ws_main[:, :S_INNER], "nt", "ssd_in_dx_z")
    dhid1 = _matmul(dxbc, ws_main[:, S_INNER:], "nt", "ssd_in_dx_xbc", add=dhid1)
    dhid1 = _matmul(dsm_s, ws_small, "nt", "ssd_in_dx_dt", add=dhid1, tk=SMALL_W)
    d_wsi = jnp.concatenate([_matmul(hid1, dz_s, "tn", "ssd_in_dw_z"),
                             _matmul(hid1, dxbc, "tn", "ssd_in_dw_xbc"),
                             _matmul(hid1, dsm_s, "tn", "ssd_in_dw_dt", tn=SMALL_W)[:, :SH]], axis=1)
    dx1, d_nw1 = _rms_bwd(x1, nw1, dx2, dhid1, "rms1_bwd")

    dy0 = _matmul(dx1, w_go, "nt", "gdn_out_dx", out_dtype=BF16, tn=2048)
    d_wgo = _matmul(y0, dx1, "tn", "gdn_out_dw")
    do, dz_g, d_gnw = _gate_bwd(_gdn_gate, o, pg, G_CONV, g_nw, dy0, False, "gdn_gate_bwd")
    dq, dk, dv, dbraw, daraw, d_galog, d_gdtb = _gdn_bwd(q, k, v, braw, araw, g_alog, g_dtb, g_sall, g_tall, do)
    dpg, dwq, _ = _conv_dpre(pg, 0, g_cw[:, :G_QK], zero_b[:, :G_QK], GDK, post_q, dq, "gdn_dpre_q", G_CONV, 0)
    dpg, dwk, _ = _conv_dpre(pg, G_QK, g_cw[:, G_QK:2 * G_QK], zero_b[:, :G_QK], GDK, post_k, dk, "gdn_dpre_k",
                             G_CONV, G_QK, dpg)
    dpg, dwv, _ = _conv_dpre(pg, 2 * G_QK, g_cw[:, 2 * G_QK:], zero_b[:, :G_V], GDK, _silu, dv, "gdn_dpre_v",
                             G_CONV, 2 * G_QK, dpg)
    d_gcw = jnp.concatenate([dwq, dwk, dwv], axis=1)
    dqkv = _conv_t(dpg, g_cw, "gdn_conv_t")
    dsm_g = jnp.concatenate([_rows_to_gates(dbraw), _rows_to_gates(daraw),
                             jnp.zeros((t, SMALL_W - 2 * GH), F32)], axis=1).astype(BF16)
    dhid0 = _matmul(dqkv, wg_main[:, :G_CONV], "nt", "gdn_in_dx_qkv")
    dhid0 = _matmul(dz_g, wg_main[:, G_CONV:], "nt", "gdn_in_dx_z", add=dhid0)
    dhid0 = _matmul(dsm_g, wg_small, "nt", "gdn_in_dx_ba", add=dhid0, tk=SMALL_W)
    d_wgi = jnp.concatenate([_matmul(hid0, dqkv, "tn", "gdn_in_dw_qkv"),
                             _matmul(hid0, dz_g, "tn", "gdn_in_dw_z"),
                             _matmul(hid0, dsm_g, "tn", "gdn_in_dw_ba", tn=SMALL_W)[:, :2 * GH]], axis=1)
    dx0, d_nw0 = _rms_bwd(x0, nw0, dx1, dhid0, "rms0_bwd")

    blocks = []
    for b in range(4):
        blocks.append(jnp.concatenate([
            _flat128(d_wgi[:, b * IN_SHARD:(b + 1) * IN_SHARD]), _flat128(d_wgo[b * OUT_SHARD:(b + 1) * OUT_SHARD]),
            _flat128(d_wsi[:, b * IN_SHARD:(b + 1) * IN_SHARD]), _flat128(d_wso[b * OUT_SHARD:(b + 1) * OUT_SHARD])],
            axis=0))
    rows = blocks[0].shape[0]
    half = rows // 2
    g4 = jnp.stack(blocks).reshape(4, 2, half, 128)
    g_mine = lax.dynamic_index_in_dim(g4, cc, axis=1, keepdims=False).reshape(4 * half, 128)
    g_other = lax.dynamic_index_in_dim(g4, 1 - cc, axis=1, keepdims=False).reshape(4 * half, 128)
    from_sib = _swap_sibling(g_other, "swap_halves")
    hsum, hsum_bf = _add_n([g_mine, from_sib], "sum_cores", (F32, BF16))
    recv = _scatter_chips(hsum_bf.reshape(4, half, 128))
    own = lax.dynamic_index_in_dim(hsum.reshape(4, half, 128), chip, axis=0, keepdims=False)
    tot = _add_n([own, recv[0], recv[1], recv[2]], "sum_chips")[0]
    tot_sib = _swap_sibling(tot, "swap_totals")
    lo = jnp.where(cc == 0, tot, tot_sib)
    hi = jnp.where(cc == 0, tot_sib, tot)
    full = jnp.concatenate([lo, hi], axis=0)
    grads = dict(
        gdn_w_in=full[offs[0]:offs[0] + r_in].reshape(1, D, IN_SHARD),
        gdn_w_out=full[offs[1]:offs[1] + r_out].reshape(1, OUT_SHARD, D),
        ssd_w_in=full[offs[2]:offs[2] + r_in].reshape(1, D, IN_SHARD),
        ssd_w_out=full[offs[3]:offs[3] + r_out].reshape(1, OUT_SHARD, D))

    small_parts = [loss_row, jnp.concatenate([d_nw0, d_nw1], axis=0), d_gcw, d_galog, d_gdtb, d_gnw, d_scw, d_scb,
                   d_sdtb, d_salog, d_sd, d_snw, d_fw]
    sbuf, soffs = _pack_flat(small_parts, 8)
    ssum = _unpack_flat(_allreduce_small(sbuf), soffs)
    (loss_s, g_nw_all, g_gcw, g_galog, g_gdtb, g_gnw, g_scw, g_scb, g_sdtb, g_salog, g_sd, g_snw, g_fw) = ssum

    def my_cols(a, width):
        return lax.dynamic_slice_in_dim(a, chip * width, width, axis=a.ndim - 1)

    grads.update(
        norm_w=g_nw_all, gdn_conv_w=my_cols(g_gcw, 1024)[None], gdn_a_log=g_galog.reshape(1, GH),
        gdn_dt_bias=g_gdtb.reshape(1, GH), gdn_norm_w=g_gnw.reshape(1, GDK),
        ssd_conv_w=my_cols(g_scw, 1024)[None], ssd_conv_b=my_cols(g_scb, 1024),
        ssd_dt_bias=g_sdtb.reshape(SR, SG).T.reshape(1, SH), ssd_a_log=g_salog.reshape(SR, SG).T.reshape(1, SH),
        ssd_d=g_sd.reshape(SR, SG).T.reshape(1, SH),
        ssd_norm_w=my_cols(g_snw, 512), final_norm_w=g_fw.reshape(D))
    loss = loss_s[0, 0]

    big_names = ("gdn_w_in", "gdn_w_out", "ssd_w_in", "ssd_w_out")
    deltas, new_m, new_v = {}, {}, {}
    for n in big_names:
        deltas[n], new_m[n], new_v[n] = _adamw(ws[n], grads[n], ms[n], vs[n], "adamw_" + n)
    rest = [n for n in names if n not in big_names]
    packs = [_pack_flat([d[n] for n in rest], 8) for d in (ws, grads, ms, vs)]
    outs = _adamw(*[p[0] for p in packs], "adamw_small")
    for d, buf in zip((deltas, new_m, new_v), outs):
        for n, a in zip(rest, _unpack_flat(buf, packs[0][1])):
            d[n] = a

    grad_x = dx0.reshape(1, t, D)
    return (loss, grad_x, *[grads[n] for n in names], *[deltas[n] for n in names],
            *[new_m[n] for n in names], *[new_v[n] for n in names])
```

```python
import functools

import jax
import jax.numpy as jnp
from jax import lax
from jax.experimental import pallas as pl
from jax.experimental.pallas import tpu as pltpu

F32 = jnp.float32
BF16 = jnp.bfloat16
SDS = jax.ShapeDtypeStruct
MESH = pl.DeviceIdType.MESH

D = 1024
EPS = 1e-6
CONV_K = 4
N_CHIPS = 4
GH = 16
GHQ = 8
GDK = 128
GCH = 64
G_QK = 1024
G_V = 2048
G_CONV = 4096
G_MAIN = 6144
G_IN = 6176
SH = 32
SP = 64
SN = 128
SG = 8
SR = 4
SCH = 128
S_INNER = 2048
S_MAIN = 6144
S_IN = 6176
IN_SHARD = 1544
OUT_SHARD = 512
SMALL_W = 128

ADAM_LR = 0.001
ADAM_B1 = 0.9
ADAM_B2 = 0.999
ADAM_EPS = 1e-08
ADAM_WD = 0.01
ADAM_STEP = 10

VMEM_LIMIT = 48 * 1024 * 1024
BLOCK_ELEMS = 256 * 1024
NEG = -1e30


def _pcall(body, **kw):
    return pl.pallas_call(body, **kw)


def _cp(sem=None, vmem=VMEM_LIMIT):
    return pltpu.CompilerParams(dimension_semantics=sem, vmem_limit_bytes=vmem)


def _sigmoid(x):
    return 1.0 / (1.0 + jnp.exp(-x))


def _silu(x):
    return x * _sigmoid(x)


def _softplus(x):
    return jnp.maximum(x, 0.0) + jnp.log(1.0 + jnp.exp(-jnp.abs(x)))


def _rms(x, w):
    return x * lax.rsqrt(jnp.mean(x * x, axis=-1, keepdims=True) + EPS) * w


_DIMS = {"nn": (((2,), (1,)), ((0,), (0,))),
         "nt": (((2,), (2,)), ((0,), (0,))),
         "tn": (((1,), (1,)), ((0,), (0,)))}


def _bdot(a, b, spec):
    return lax.dot_general(a.astype(BF16), b.astype(BF16), _DIMS[spec], preferred_element_type=F32)


@functools.partial(jax.custom_vjp, nondiff_argnums=(2,))
def _bmm(a, b, spec):
    return _bdot(a, b, spec)


def _bmm_fwd(a, b, spec):
    return _bdot(a, b, spec), (a, b)


def _bmm_bwd(spec, res, g):
    a, b = res
    if spec == "nn":
        return _bdot(g, b, "nt"), _bdot(a, g, "tn")
    if spec == "nt":
        return _bdot(g, b, "nn"), _bdot(g, a, "tn")
    return _bdot(b, g, "nt"), _bdot(a, g, "nn")


_bmm.defvjp(_bmm_fwd, _bmm_bwd)


@jax.custom_vjp
def _tri_inv(n):
    t = -n
    p = n
    steps = (n.shape[-1] - 1).bit_length() - 1
    r = lax.broadcasted_iota(jnp.int32, n.shape, 1)
    c = lax.broadcasted_iota(jnp.int32, n.shape, 2)
    t = t + jnp.where(r == c, 1.0, 0.0)
    for _ in range(steps):
        p = _bdot(p, p, "nn")
        t = t + _bdot(t, p, "nn")
    return t


def _tri_inv_fwd(n):
    t = _tri_inv(n)
    return t, t


def _tri_inv_bwd(t, g):
    return (-_bdot(_bdot(t, g, "tn"), t, "nt"),)


_tri_inv.defvjp(_tri_inv_fwd, _tri_inv_bwd)


@jax.custom_vjp
def _tri_inv_known(n, t):
    del n
    return t


def _tri_inv_known_fwd(n, t):
    del n
    return t, t


def _tri_inv_known_bwd(t, g):
    return _tri_inv_bwd(t, g)[0], jnp.zeros_like(t)


_tri_inv_known.defvjp(_tri_inv_known_fwd, _tri_inv_known_bwd)


def _masks(c, lead=1):
    r = lax.broadcasted_iota(jnp.int32, (lead, c, c), 1)
    s = lax.broadcasted_iota(jnp.int32, (lead, c, c), 2)
    return r >= s, r > s, r == s, r <= s


def _row_to_col(row, eye):
    return jnp.sum(jnp.where(eye, row, 0.0), axis=2, keepdims=True)


def _gdn_chunk(q, k, v, braw, araw, alog, dtb, s, t_known=None, want_t=False):
    h = v.shape[0]
    c = v.shape[1]
    rep = h // q.shape[0]
    tril, strict, eye, triu = _masks(c)
    qq = jnp.broadcast_to(q[:, None], (q.shape[0], rep) + q.shape[1:]).reshape(v.shape)
    kk = jnp.broadcast_to(k[:, None], (k.shape[0], rep) + k.shape[1:]).reshape(v.shape)
    beta_row = _sigmoid(braw)
    g_row = -jnp.exp(alog) * _softplus(araw + dtb)
    beta_col = _row_to_col(beta_row, eye)
    g_col = _row_to_col(g_row, eye)
    gc_col = jnp.sum(jnp.where(tril, g_row, 0.0), axis=2, keepdims=True)
    gc_row = jnp.sum(jnp.where(triu, g_col, 0.0), axis=1, keepdims=True)
    gc_last = jnp.sum(g_row, axis=2, keepdims=True)
    lmat = jnp.exp(jnp.where(tril, gc_col - gc_row, NEG))
    kb = kk * beta_col
    vb = v * beta_col
    n = jnp.where(strict, _bmm(kb, kk, "nt") * lmat, 0.0)
    t = _tri_inv(n) if t_known is None else _tri_inv_known(n, t_known)
    e_col = jnp.exp(gc_col)
    u = _bmm(t, vb, "nn")
    w = _bmm(t, kb * e_col, "nn")
    attn = _bmm(qq, kk, "nt") * lmat
    q_dec = qq * e_col
    k_dec = kk * jnp.exp(gc_last - gc_col)
    v_new = u - _bmm(w, s, "nn")
    o = _bmm(q_dec, s, "nn") + _bmm(attn, v_new, "nn")
    s_new = s * jnp.exp(gc_last) + _bmm(k_dec, v_new, "tn")
    return (o, s_new, t) if want_t else (o, s_new)


def _ssd_chunk(xs, bm, cm, dtraw, dtb, alog, dskip, s):
    c = xs.shape[1]
    tril, _, eye, triu = _masks(c)
    lane = lax.broadcasted_iota(jnp.int32, (1, 1, SR * SP), 2)
    prow = lax.broadcasted_iota(jnp.int32, (1, SR * SP, 1), 1)
    cb = _bmm(cm, bm, "nt")
    cs = _bmm(cm, s, "nt")

    def per_head(vals, idx):
        out = vals[SR - 1]
        for r in reversed(range(SR - 1)):
            out = jnp.where(idx < (r + 1) * SP, vals[r], out)
        return out

    dt_cols, e_cols, lmats, dstates, declast = [], [], [], [], []
    for r in range(SR):
        dt_row = _softplus(dtraw[r] + dtb[r])
        adt_row = -jnp.exp(alog[r]) * dt_row
        dt_cols.append(_row_to_col(dt_row, eye))
        adt_col = _row_to_col(adt_row, eye)
        acs_col = jnp.sum(jnp.where(tril, adt_row, 0.0), axis=2, keepdims=True)
        acs_row = jnp.sum(jnp.where(triu, adt_col, 0.0), axis=1, keepdims=True)
        acs_last = jnp.sum(adt_row, axis=2, keepdims=True)
        lmats.append(jnp.exp(jnp.where(tril, acs_col - acs_row, NEG)))
        e_cols.append(jnp.exp(acs_col))
        dstates.append(jnp.exp(acs_last - acs_col))
        declast.append(jnp.exp(acs_last))
    xd = xs * per_head(dt_cols, lane)
    y = per_head([_bmm(cb * lmats[r], xd, "nn") for r in range(SR)], lane)
    states = per_head([_bmm(xd, bm * dstates[r], "tn") for r in range(SR)], prow)
    y = y + cs * per_head(e_cols, lane) + xs * per_head([dskip[r] for r in range(SR)], lane)
    s_new = s * per_head(declast, prow) + states
    return y, s_new


def _matmul(a, b, mode, name, out_dtype=F32, add=None, tm=1024, tn=1024, tk=2048):
    if mode == "nn":
        (m, k), n = a.shape, b.shape[1]
    elif mode == "nt":
        (m, k), n = a.shape, b.shape[0]
    else:
        (k, m), n = a.shape, b.shape[1]
    tm, tn, tk = min(tm, m), min(tn, n), min(tk, k)
    assert m % tm == 0 and n % tn == 0 and k % tk == 0, (name, m, n, k)
    nk = k // tk
    dims = {"nn": (((1,), (0,)), ((), ())), "nt": (((1,), (1,)), ((), ())),
            "tn": (((0,), (0,)), ((), ()))}[mode]
    a_spec = {"nn": pl.BlockSpec((tm, tk), lambda i, j, kk: (i, kk)),
              "nt": pl.BlockSpec((tm, tk), lambda i, j, kk: (i, kk)),
              "tn": pl.BlockSpec((tk, tm), lambda i, j, kk: (kk, i))}[mode]
    b_spec = {"nn": pl.BlockSpec((tk, tn), lambda i, j, kk: (kk, j)),
              "nt": pl.BlockSpec((tn, tk), lambda i, j, kk: (j, kk)),
              "tn": pl.BlockSpec((tk, tn), lambda i, j, kk: (kk, j))}[mode]
    o_spec = pl.BlockSpec((tm, tn), lambda i, j, kk: (i, j))
    has_add = add is not None

    def body(*refs):
        a_ref, b_ref = refs[:2]
        add_ref = refs[2] if has_add else None
        o_ref = refs[2 + has_add]
        part = lax.dot_general(a_ref[...].astype(BF16), b_ref[...].astype(BF16), dims,
                               preferred_element_type=F32)

        def finish(r):
            if has_add:
                r = r + add_ref[...].astype(F32)
            o_ref[...] = r.astype(o_ref.dtype)

        if nk == 1:
            finish(part)
            return
        acc_ref = refs[3 + has_add]
        kk = pl.program_id(2)

        @pl.when(kk == 0)
        def _():
            acc_ref[...] = part

        @pl.when(jnp.logical_and(kk > 0, kk < nk - 1))
        def _():
            acc_ref[...] += part

        @pl.when(kk == nk - 1)
        def _():
            finish(acc_ref[...] + part)

    ins = [a, b] + ([add] if has_add else [])
    in_specs = [a_spec, b_spec] + ([o_spec] if has_add else [])
    scratch = [] if nk == 1 else [pltpu.VMEM((tm, tn), F32)]
    return _pcall(body, grid=(m // tm, n // tn, nk), in_specs=in_specs, out_specs=o_spec,
                  out_shape=SDS((m, n), out_dtype), scratch_shapes=scratch,
                  compiler_params=_cp(("parallel", "parallel", "arbitrary")), name=name)(*ins)


def _ew(fn, ins, in_specs, out_shape, out_specs, grid, name):
    n_in = len(ins)

    def body(*refs):
        outs = fn(*[r[...] for r in refs[:n_in]])
        for r, o in zip(refs[n_in:], outs):
            r[...] = o.astype(r.dtype)

    return _pcall(body, grid=grid, in_specs=in_specs, out_specs=out_specs, out_shape=out_shape,
                  compiler_params=_cp(("arbitrary",) * len(grid)), name=name)(*ins)


def _ew_vjp(fn, ins, in_specs, cts, ct_specs, wrt, g_shape, g_specs, acc, grid, name):
    n_in, n_ct = len(ins), len(cts)

    def body(*refs):
        vals = [r[...].astype(F32) for r in refs[:n_in]]
        outs, vjp = jax.vjp(fn, *vals)
        g_all = vjp(tuple(r[...].astype(F32) for r in refs[n_in:n_in + n_ct]))
        for pos, (i, g_ref) in enumerate(zip(wrt, refs[n_in + n_ct:])):
            g = g_all[i]
            if pos in acc:
                first = functools.reduce(
                    jnp.logical_and, [pl.program_id(ax) == 0 for ax in range(acc[pos], len(grid))])

                @pl.when(first)
                def _():
                    g_ref[...] = jnp.zeros_like(g_ref)

                g_ref[...] += g.astype(g_ref.dtype)
            else:
                g_ref[...] = g.astype(g_ref.dtype)

    return _pcall(body, grid=grid, in_specs=list(in_specs) + list(ct_specs), out_specs=g_specs,
                  out_shape=g_shape, compiler_params=_cp(("arbitrary",) * len(grid)),
                  name=name)(*ins, *cts)


def _row_spec(tm, n):
    return pl.BlockSpec((tm, n), lambda i: (i, 0))


def _par_spec(n):
    return pl.BlockSpec((1, n), lambda i: (0, 0))


def _rms_fwd(x, w, name):
    t = x.shape[0]
    tm = min(t, 512)
    return _ew(lambda xv, wv: (_rms(xv, wv),), [x, w], [_row_spec(tm, D), _par_spec(D)],
               [SDS((t, D), BF16)], [_row_spec(tm, D)], (t // tm,), name)[0]


def _rms_bwd(x, w, dres, dhid, name):
    t = x.shape[0]
    tm = min(t, 512)
    return _ew_vjp(lambda xv, wv: (xv, _rms(xv, wv)), [x, w], [_row_spec(tm, D), _par_spec(D)],
                   [dres, dhid], [_row_spec(tm, D), _row_spec(tm, D)], (0, 1),
                   [SDS((t, D), F32), SDS((1, D), F32)], [_row_spec(tm, D), _par_spec(D)],
                   {1: 0}, (t // tm,), name)


def _final_loss(x, w, tgt, name):
    t = x.shape[0]
    tm = min(t, 512)

    def body(x_ref, w_ref, t_ref, dx_ref, dw_ref, l_ref):
        @pl.when(pl.program_id(0) == 0)
        def _():
            dw_ref[...] = jnp.zeros_like(dw_ref)
            l_ref[...] = jnp.zeros_like(l_ref)

        xv, wv = x_ref[...], w_ref[...]
        rstd = lax.rsqrt(jnp.mean(xv * xv, axis=-1, keepdims=True) + EPS)
        xh = xv * rstd
        err = xh * wv - t_ref[...]
        l_ref[...] += 0.5 * jnp.sum(jnp.mean(err * err, axis=-1, keepdims=True), axis=0, keepdims=True)
        dy = err * (1.0 / D)
        dw_ref[...] += jnp.sum(dy * xh, axis=0, keepdims=True)
        dxh = dy * wv
        dx_ref[...] = rstd * (dxh - xh * jnp.mean(dxh * xh, axis=-1, keepdims=True))

    return _pcall(body, grid=(t // tm,), in_specs=[_row_spec(tm, D), _par_spec(D), _row_spec(tm, D)],
                  out_specs=[_row_spec(tm, D), _par_spec(D), _par_spec(128)],
                  out_shape=[SDS((t, D), F32), SDS((1, D), F32), SDS((1, 128), F32)],
                  compiler_params=_cp(("arbitrary",)), name=name)(x, w, tgt)


def _conv_taps(ext, w_ref, tm, lo):
    n = ext.shape[0]
    acc = None
    for j in range(CONV_K):
        shift = (CONV_K - 1 - j) if lo else (n - (CONV_K - 1 - j)) % n
        rolled = pltpu.roll(ext, shift, 0) if shift else ext
        term = w_ref[pl.ds(j, 1), :] * rolled[lo:lo + tm]
        acc = term if acc is None else acc + term
    return acc


def _conv_pre_specs(tm, ct, col0):
    hb = tm // 8
    return [pl.BlockSpec((tm, ct), lambda j, i: (i, col0 + j)),
            pl.BlockSpec((8, ct), lambda j, i: (jnp.maximum(i * hb - 1, 0), col0 + j)),
            pl.BlockSpec((CONV_K, ct), lambda j, i: (0, j)),
            pl.BlockSpec((1, ct), lambda j, i: (0, j))]


def _conv_pre_value(x_ref, xh_ref, w_ref, b_ref, tm):
    halo = jnp.where(pl.program_id(1) > 0, xh_ref[...], 0.0)
    ext = jnp.concatenate([halo, x_ref[...]], axis=0)
    return _conv_taps(ext, w_ref, tm, 8) + b_ref[...], ext


def _conv_fwd(x, col0, w, b, ct, post, name):
    t = x.shape[0]
    tm = min(t, BLOCK_ELEMS // ct)
    nt = w.shape[1] // ct

    def body(x_ref, xh_ref, w_ref, b_ref, o_ref):
        pre, _ = _conv_pre_value(x_ref, xh_ref, w_ref, b_ref, tm)
        o_ref[0] = post(pre)

    return _pcall(body, grid=(nt, t // tm), in_specs=_conv_pre_specs(tm, ct, col0 // ct),
                  out_specs=pl.BlockSpec((1, tm, ct), lambda j, i: (j, i, 0)),
                  out_shape=SDS((nt, t, ct), F32), compiler_params=_cp(("arbitrary", "arbitrary")),
                  name=name)(x, x, w, b)


def _conv_dpre(x, col0, w, b, ct, post, dout, name, c_total, c_off, into=None):
    t = x.shape[0]
    tm = min(t, BLOCK_ELEMS // ct)
    nt = w.shape[1] // ct
    chained = into is not None

    def body(*refs):
        x_ref, xh_ref, w_ref, b_ref, do_ref = refs[:5]
        dp_ref, dw_ref, db_ref = refs[5 + chained:]
        pre, ext = _conv_pre_value(x_ref, xh_ref, w_ref, b_ref, tm)
        _, vjp = jax.vjp(post, pre)
        dpre = vjp(do_ref[0])[0]
        dp_ref[...] = dpre

        @pl.when(pl.program_id(1) == 0)
        def _():
            dw_ref[...] = jnp.zeros_like(dw_ref)
            db_ref[...] = jnp.zeros_like(db_ref)

        for j in range(CONV_K):
            xs = (pltpu.roll(ext, CONV_K - 1 - j, 0) if j < CONV_K - 1 else ext)[8:8 + tm]
            dw_ref[pl.ds(j, 1), :] += jnp.sum(dpre * xs, axis=0, keepdims=True)
        db_ref[...] += jnp.sum(dpre, axis=0, keepdims=True)

    c = w.shape[1]
    o0 = c_off // ct
    return _pcall(body, grid=(nt, t // tm),
                  in_specs=_conv_pre_specs(tm, ct, col0 // ct)
                  + [pl.BlockSpec((1, tm, ct), lambda j, i: (j, i, 0))] + ([ANY] if chained else []),
                  out_specs=[pl.BlockSpec((tm, ct), lambda j, i: (i, o0 + j)),
                             pl.BlockSpec((CONV_K, ct), lambda j, i: (0, j)),
                             pl.BlockSpec((1, ct), lambda j, i: (0, j))],
                  out_shape=[SDS((t, c_total), F32), SDS((CONV_K, c), F32), SDS((1, c), F32)],
                  input_output_aliases={5: 0} if chained else {},
                  compiler_params=_cp(("arbitrary", "arbitrary")), name=name)(
                      x, x, w, b, dout, *([into] if chained else []))


def _conv_t(dpre, w, name):
    t, c = dpre.shape
    tm = min(t, 512)
    ct = min(c, 512)
    hb = tm // 8
    last = t // tm - 1

    def body(d_ref, dh_ref, w_ref, o_ref):
        halo = jnp.where(pl.program_id(1) < last, dh_ref[...], 0.0)
        ext = jnp.concatenate([d_ref[...], halo], axis=0)
        o_ref[...] = _conv_taps(ext, w_ref, tm, 0).astype(o_ref.dtype)

    return _pcall(body, grid=(c // ct, t // tm),
                  in_specs=[pl.BlockSpec((tm, ct), lambda j, i: (i, j)),
                            pl.BlockSpec((8, ct), lambda j, i: (jnp.minimum((i + 1) * hb, t // 8 - 1), j)),
                            pl.BlockSpec((CONV_K, ct), lambda j, i: (0, j))],
                  out_specs=pl.BlockSpec((tm, ct), lambda j, i: (i, j)),
                  out_shape=SDS((t, c), BF16), compiler_params=_cp(("arbitrary", "arbitrary")),
                  name=name)(dpre, dpre, w)


def _l2norm_scaled(scale):
    def post(pre):
        a = _silu(pre)
        return a * lax.rsqrt(jnp.sum(a * a, axis=-1, keepdims=True) + EPS) * scale
    return post


GDN_HB = 16
GDN_CPB = 2
SSD_GB = 8
SSD_CPB = 1


def _gdn_specs(nb, rev):
    hb, cpb, tc = GDN_HB, GDN_CPB, GDN_CPB * GCH
    blk = (lambda n: nb - 1 - n) if rev else (lambda n: n)
    seq = lambda h: pl.BlockSpec((h, tc, GDK), lambda g, n: (g, blk(n), 0))
    gate = pl.BlockSpec((1, hb, cpb, GCH), lambda g, n: (blk(n), g, 0, 0))
    par = pl.BlockSpec((hb, 1, 1), lambda g, n: (g, 0, 0))
    state = pl.BlockSpec((cpb, hb, GDK, GDK), lambda g, n: (blk(n), g, 0, 0))
    tinv = pl.BlockSpec((cpb, hb, GCH, GCH), lambda g, n: (blk(n), g, 0, 0))
    return seq, gate, par, state, tinv


def _gdn_fwd(q, k, v, braw, araw, alog, dtb):
    t = v.shape[1]
    hb, cpb = GDN_HB, GDN_CPB
    nb = t // (cpb * GCH)
    seq, gate, par, state, tinv = _gdn_specs(nb, False)

    def body(q_ref, k_ref, v_ref, b_ref, a_ref, al_ref, dt_ref, o_ref, sall_ref, tall_ref, s_ref):
        @pl.when(pl.program_id(1) == 0)
        def _():
            s_ref[...] = jnp.zeros_like(s_ref)

        s = s_ref[...]
        for c in range(cpb):
            rows = pl.ds(c * GCH, GCH)
            sall_ref[c] = s
            o, s, tmat = _gdn_chunk(q_ref[:, rows, :], k_ref[:, rows, :], v_ref[:, rows, :],
                                    b_ref[0, :, pl.ds(c, 1), :], a_ref[0, :, pl.ds(c, 1), :],
                                    al_ref[...], dt_ref[...], s, want_t=True)
            o_ref[:, rows, :] = o
            tall_ref[c] = tmat.astype(BF16)
        s_ref[...] = s

    return _pcall(body, grid=(GH // hb, nb),
                  in_specs=[seq(hb // 2), seq(hb // 2), seq(hb), gate, gate, par, par],
                  out_specs=[seq(hb), state, tinv],
                  out_shape=[SDS((GH, t, GDK), F32), SDS((t // GCH, GH, GDK, GDK), F32),
                             SDS((t // GCH, GH, GCH, GCH), BF16)],
                  scratch_shapes=[pltpu.VMEM((hb, GDK, GDK), F32)],
                  compiler_params=_cp(("arbitrary", "arbitrary")), name="gdn_chunk_fwd")(
                      q, k, v, braw, araw, alog, dtb)


def _gdn_bwd(q, k, v, braw, araw, alog, dtb, sall, tall, do):
    t = v.shape[1]
    hb, cpb = GDN_HB, GDN_CPB
    nb = t // (cpb * GCH)
    seq, gate, par, state, tinv = _gdn_specs(nb, True)

    def body(q_ref, k_ref, v_ref, b_ref, a_ref, al_ref, dt_ref, sall_ref, tall_ref, do_ref,
             dq_ref, dk_ref, dv_ref, db_ref, da_ref, dal_ref, ddt_ref, ds_ref):
        @pl.when(pl.program_id(1) == 0)
        def _():
            ds_ref[...] = jnp.zeros_like(ds_ref)
            dal_ref[...] = jnp.zeros_like(dal_ref)
            ddt_ref[...] = jnp.zeros_like(ddt_ref)

        ds = ds_ref[...]
        for c in reversed(range(cpb)):
            rows = pl.ds(c * GCH, GCH)
            fn = functools.partial(_gdn_chunk, t_known=tall_ref[c].astype(F32))
            _, vjp = jax.vjp(fn, q_ref[:, rows, :], k_ref[:, rows, :], v_ref[:, rows, :],
                             b_ref[0, :, pl.ds(c, 1), :], a_ref[0, :, pl.ds(c, 1), :],
                             al_ref[...], dt_ref[...], sall_ref[c])
            dq, dk, dv, db, da, dal, ddt, ds = vjp((do_ref[:, rows, :], ds))
            dq_ref[:, rows, :] = dq
            dk_ref[:, rows, :] = dk
            dv_ref[:, rows, :] = dv
            db_ref[0, :, pl.ds(c, 1), :] = db
            da_ref[0, :, pl.ds(c, 1), :] = da
            dal_ref[...] += dal
            ddt_ref[...] += ddt
        ds_ref[...] = ds

    return _pcall(body, grid=(GH // hb, nb),
                  in_specs=[seq(hb // 2), seq(hb // 2), seq(hb), gate, gate, par, par, state, tinv, seq(hb)],
                  out_specs=[seq(hb // 2), seq(hb // 2), seq(hb), gate, gate, par, par],
                  out_shape=[SDS(q.shape, F32), SDS(k.shape, F32), SDS(v.shape, F32),
                             SDS(braw.shape, F32), SDS(araw.shape, F32),
                             SDS((GH, 1, 1), F32), SDS((GH, 1, 1), F32)],
                  scratch_shapes=[pltpu.VMEM((hb, GDK, GDK), F32)],
                  compiler_params=_cp(("arbitrary", "arbitrary")), name="gdn_chunk_bwd")(
                      q, k, v, braw, araw, alog, dtb, sall, tall, do)


def _ssd_specs(nb, rev):
    gb, cpb, tc = SSD_GB, SSD_CPB, SSD_CPB * SCH
    blk = (lambda n: nb - 1 - n) if rev else (lambda n: n)
    seq = lambda w: pl.BlockSpec((gb, tc, w), lambda g, n: (g, blk(n), 0))
    gate = pl.BlockSpec((1, cpb * SR, gb, 1, SCH), lambda g, n: (blk(n), 0, g, 0, 0))
    par = pl.BlockSpec((SR, gb, 1, 1), lambda g, n: (0, g, 0, 0))
    state = pl.BlockSpec((cpb, gb, SR * SP, SN), lambda g, n: (blk(n), g, 0, 0))
    return seq, gate, par, state


def _ssd_fwd(xs, bm, cm, dtraw, dtb, alog, dskip):
    t = xs.shape[1]
    gb, cpb = SSD_GB, SSD_CPB
    nb = t // (cpb * SCH)
    seq, gate, par, state = _ssd_specs(nb, False)

    def body(x_ref, b_ref, c_ref, dt_ref, dtb_ref, al_ref, dk_ref, y_ref, sall_ref, s_ref):
        @pl.when(pl.program_id(1) == 0)
        def _():
            s_ref[...] = jnp.zeros_like(s_ref)

        s = s_ref[...]
        for c in range(cpb):
            rows = pl.ds(c * SCH, SCH)
            sall_ref[c] = s
            y, s = _ssd_chunk(x_ref[:, rows, :], b_ref[:, rows, :], c_ref[:, rows, :],
                              dt_ref[0, pl.ds(c * SR, SR)], dtb_ref[...], al_ref[...],
                              dk_ref[...], s)
            y_ref[:, rows, :] = y
        s_ref[...] = s

    return _pcall(body, grid=(SG // gb, nb),
                  in_specs=[seq(SR * SP), seq(SN), seq(SN), gate, par, par, par],
                  out_specs=[seq(SR * SP), state],
                  out_shape=[SDS((SG, t, SR * SP), F32), SDS((t // SCH, SG, SR * SP, SN), F32)],
                  scratch_shapes=[pltpu.VMEM((gb, SR * SP, SN), F32)],
                  compiler_params=_cp(("arbitrary", "arbitrary")), name="ssd_chunk_fwd")(
                      xs, bm, cm, dtraw, dtb, alog, dskip)


def _ssd_bwd(xs, bm, cm, dtraw, dtb, alog, dskip, sall, dy):
    t = xs.shape[1]
    gb, cpb = SSD_GB, SSD_CPB
    nb = t // (cpb * SCH)
    seq, gate, par, state = _ssd_specs(nb, True)

    def body(x_ref, b_ref, c_ref, dt_ref, dtb_ref, al_ref, dk_ref, sall_ref, dy_ref,
             dx_ref, dbm_ref, dcm_ref, ddt_ref, ddtb_ref, dal_ref, ddk_ref, ds_ref):
        @pl.when(pl.program_id(1) == 0)
        def _():
            ds_ref[...] = jnp.zeros_like(ds_ref)
            ddtb_ref[...] = jnp.zeros_like(ddtb_ref)
            dal_ref[...] = jnp.zeros_like(dal_ref)
            ddk_ref[...] = jnp.zeros_like(ddk_ref)

        ds = ds_ref[...]
        for c in reversed(range(cpb)):
            rows = pl.ds(c * SCH, SCH)
            _, vjp = jax.vjp(_ssd_chunk, x_ref[:, rows, :], b_ref[:, rows, :], c_ref[:, rows, :],
                             dt_ref[0, pl.ds(c * SR, SR)], dtb_ref[...], al_ref[...],
                             dk_ref[...], sall_ref[c])
            dx, dbm, dcm, ddt, ddtb, dal, ddk, ds = vjp((dy_ref[:, rows, :], ds))
            dx_ref[:, rows, :] = dx
            dbm_ref[:, rows, :] = dbm
            dcm_ref[:, rows, :] = dcm
            ddt_ref[0, pl.ds(c * SR, SR)] = ddt
            ddtb_ref[...] += ddtb
            dal_ref[...] += dal
            ddk_ref[...] += ddk
        ds_ref[...] = ds

    return _pcall(body, grid=(SG // gb, nb),
                  in_specs=[seq(SR * SP), seq(SN), seq(SN), gate, par, par, par, state, seq(SR * SP)],
                  out_specs=[seq(SR * SP), seq(SN), seq(SN), gate, par, par, par],
                  out_shape=[SDS(xs.shape, F32), SDS(bm.shape, F32), SDS(cm.shape, F32),
                             SDS(dtraw.shape, F32), SDS((SR, SG, 1, 1), F32), SDS((SR, SG, 1, 1), F32),
                             SDS((SR, SG, 1, 1), F32)],
                  scratch_shapes=[pltpu.VMEM((gb, SR * SP, SN), F32)],
                  compiler_params=_cp(("arbitrary", "arbitrary")), name="ssd_chunk_bwd")(
                      xs, bm, cm, dtraw, dtb, alog, dskip, sall, dy)


def _gate_specs(tm, ct, zcol0, per_tile_w):
    z0 = zcol0 // ct
    return [pl.BlockSpec((1, tm, ct), lambda i, j: (j, i, 0)),
            pl.BlockSpec((tm, ct), lambda i, j: (i, z0 + j)),
            pl.BlockSpec((1, ct), (lambda i, j: (0, j)) if per_tile_w else (lambda i, j: (0, 0)))]


def _gdn_gate(o, z, w):
    return (_rms(o[0], w) * _silu(z),)


def _ssd_gate(y, z, w):
    return (_rms(y[0] * _silu(z), w),)


def _gate_fwd(fn, o, proj, zcol0, w, name):
    nt, t, ct = o.shape
    tm = min(t, BLOCK_ELEMS // ct)
    return _ew(fn, [o, proj, w], _gate_specs(tm, ct, zcol0, w.shape[1] > ct), [SDS((t, nt * ct), BF16)],
               [pl.BlockSpec((tm, ct), lambda i, j: (i, j))], (t // tm, nt), name)[0]


def _gate_bwd(fn, o, proj, zcol0, w, dy, wacc, name):
    nt, t, ct = o.shape
    tm = min(t, BLOCK_ELEMS // ct)
    specs = _gate_specs(tm, ct, zcol0, wacc)
    out_spec = pl.BlockSpec((tm, ct), lambda i, j: (i, j))
    if wacc:
        flip = lambda s: pl.BlockSpec(s.block_shape, lambda j, i, f=s.index_map: f(i, j))
        specs = [flip(s) for s in specs]
        out_spec = flip(out_spec)
        grid, acc = (nt, t // tm), {2: 1}
    else:
        grid, acc = (t // tm, nt), {2: 0}
    return _ew_vjp(fn, [o, proj, w], specs, [dy], [out_spec], (0, 1, 2),
                   [SDS(o.shape, F32), SDS((t, nt * ct), BF16), SDS(w.shape, F32)],
                   [specs[0], out_spec, specs[2]], acc, grid, name)


def _adamw_math(w, g, m, v):
    m = ADAM_B1 * m + (1.0 - ADAM_B1) * g
    v = ADAM_B2 * v + (1.0 - ADAM_B2) * jnp.square(g)
    m_hat = m / (1.0 - ADAM_B1 ** ADAM_STEP)
    v_hat = v / (1.0 - ADAM_B2 ** ADAM_STEP)
    delta = -ADAM_LR * (m_hat / (jnp.sqrt(v_hat) + ADAM_EPS) + ADAM_WD * w)
    return delta, m, v


def _adamw(w, g, m, v, name):
    shape = w.shape
    w2, g2, m2, v2 = [a.reshape(-1, shape[-1]) for a in (w, g, m, v)]
    r, c = w2.shape
    tr = 256 if r % 256 == 0 else r
    spec = pl.BlockSpec((tr, c), lambda i: (i, 0))
    outs = _ew(_adamw_math, [w2, g2, m2, v2], [spec] * 4, [SDS((r, c), F32)] * 3, [spec] * 3,
               (r // tr,), name)
    return [o.reshape(shape) for o in outs]


def _coords():
    return lax.axis_index("x"), lax.axis_index("y"), lax.axis_index("c")


def _other_chips(x, y):
    return [(1 - x, y), (x, 1 - y), (1 - x, 1 - y)]


ANY = pl.BlockSpec(memory_space=pl.ANY)


def _rcopy(src, dst, send_sems, recv_sems, k, to):
    return pltpu.make_async_remote_copy(src_ref=src, dst_ref=dst, send_sem=send_sems.at[k],
                                        recv_sem=recv_sems.at[k], device_id=to, device_id_type=MESH)


def _gather_weights(bigs, small):
    n = len(bigs)

    def body(*refs):
        in_refs, small_ref = refs[:n], refs[n]
        out_refs, osmall_ref = refs[n + 1:2 * n + 1], refs[2 * n + 1]
        send_sems, recv_sems, local_sems = refs[2 * n + 2:]
        x, y, c = _coords()
        me = 2 * x + y
        sibling = (x, y, 1 - c)
        chips = _other_chips(x, y)
        halves = [b.shape[0] // 2 for b in bigs]
        mine = [pl.ds(c * h, h) for h in halves]
        theirs = [pl.ds((1 - c) * h, h) for h in halves]
        rc = functools.partial(_rcopy, send_sems=send_sems, recv_sems=recv_sems)

        local = [pltpu.make_async_copy(in_refs[a], out_refs[a].at[me], local_sems.at[a]) for a in range(n)]
        local.append(pltpu.make_async_copy(small_ref, osmall_ref.at[me], local_sems.at[n]))
        for cp in local:
            cp.start()
        first = []
        for a in range(n):
            for j, (px, py) in enumerate(chips):
                first.append(rc(in_refs[a].at[mine[a]], out_refs[a].at[me, mine[a]], k=3 * a + j, to=(px, py, c)))
        for j, (px, py) in enumerate(chips):
            first.append(rc(small_ref, osmall_ref.at[me], k=6 * n + j, to=(px, py, c)))
        for cp in first:
            cp.start()
        passed = []
        for a in range(n):
            for j, (px, py) in enumerate(chips):
                landed = out_refs[a].at[2 * px + py, mine[a]]
                rc(landed, landed, k=3 * a + j, to=(px, py, c)).wait_recv()
                fw = rc(landed, landed, k=3 * n + 3 * a + j, to=sibling)
                fw.start()
                passed.append(fw)
        for a in range(n):
            for j, (px, py) in enumerate(chips):
                landed = out_refs[a].at[2 * px + py, theirs[a]]
                rc(landed, landed, k=3 * n + 3 * a + j, to=sibling).wait_recv()
        for j, (px, py) in enumerate(chips):
            rc(small_ref, osmall_ref.at[2 * px + py], k=6 * n + j, to=(px, py, c)).wait_recv()
        for cp in first + passed:
            cp.wait_send()
        for cp in local:
            cp.wait()

    return _pcall(body, in_specs=[ANY] * (n + 1), out_specs=[ANY] * (n + 1),
                  out_shape=[SDS((N_CHIPS,) + b.shape, b.dtype) for b in bigs]
                  + [SDS((N_CHIPS,) + small.shape, small.dtype)],
                  scratch_shapes=[pltpu.SemaphoreType.DMA((6 * n + 3,)), pltpu.SemaphoreType.DMA((6 * n + 3,)),
                                  pltpu.SemaphoreType.DMA((n + 1,))],
                  name="gather_weights")(*bigs, small)


def _swap_halves(gs):
    n = len(gs)

    def body(*refs):
        send_sems, recv_sems = refs[2 * n:]
        x, y, c = _coords()
        cps = [_rcopy(refs[a].at[:, 1 - c], refs[n + a], send_sems, recv_sems, a, (x, y, 1 - c))
               for a in range(n)]
        for cp in cps:
            cp.start()
        for cp in cps:
            cp.wait()

    return _pcall(body, in_specs=[ANY] * n, out_specs=[ANY] * n,
                  out_shape=[SDS((N_CHIPS,) + g.shape[2:], g.dtype) for g in gs],
                  scratch_shapes=[pltpu.SemaphoreType.DMA((n,)), pltpu.SemaphoreType.DMA((n,))],
                  name="swap_halves")(*gs)


def _sum_cores(gs, rs, half_idx):
    n = len(gs)
    ns = 2
    in_specs, out_specs, out_shape = [], [], []
    for g in gs:
        _, _, h, w = g.shape
        in_specs.append(pl.BlockSpec((1, 1, h // ns, w), lambda b, i, c_ref: (b, c_ref[0], i, 0)))
    for g in gs:
        _, _, h, w = g.shape
        spec = pl.BlockSpec((1, h // ns, w), lambda b, i, c_ref: (b, i, 0))
        in_specs.append(spec)
        out_specs += [spec, spec]
        out_shape += [SDS((N_CHIPS, h, w), F32), SDS((N_CHIPS, h, w), BF16)]

    def body(c_ref, *refs):
        del c_ref
        for a in range(n):
            tot = refs[a][0] + refs[n + a][...]
            refs[2 * n + 2 * a][...] = tot
            refs[2 * n + 2 * a + 1][...] = tot.astype(BF16)

    outs = _pcall(body, grid_spec=pltpu.PrefetchScalarGridSpec(
        num_scalar_prefetch=1, grid=(N_CHIPS, ns), in_specs=in_specs, out_specs=out_specs),
        out_shape=out_shape, compiler_params=_cp(("arbitrary", "arbitrary")), name="sum_cores")(
            half_idx, *gs, *rs)
    return outs[0::2], outs[1::2]


def _scatter_chips(hs):
    n = len(hs)

    def body(*refs):
        send_sems, recv_sems = refs[2 * n:]
        x, y, c = _coords()
        cps = []
        for a in range(n):
            for j, (px, py) in enumerate(_other_chips(x, y)):
                cps.append(_rcopy(refs[a].at[2 * px + py], refs[n + a].at[j], send_sems, recv_sems,
                                  3 * a + j, (px, py, c)))
        for cp in cps:
            cp.start()
        for cp in cps:
            cp.wait()

    return _pcall(body, in_specs=[ANY] * n, out_specs=[ANY] * n,
                  out_shape=[SDS((3,) + h.shape[1:], h.dtype) for h in hs],
                  scratch_shapes=[pltpu.SemaphoreType.DMA((3 * n,)), pltpu.SemaphoreType.DMA((3 * n,))],
                  name="scatter_chips")(*hs)


def _sum_chips(hs, xs, chip_idx):
    n = len(hs)
    ns = 2
    in_specs, out_specs, out_shape = [], [], []
    for h_arr in hs:
        _, h, w = h_arr.shape
        in_specs.append(pl.BlockSpec((1, h // ns, w), lambda i, c_ref: (c_ref[0], i, 0)))
    for h_arr in hs:
        _, h, w = h_arr.shape
        in_specs.append(pl.BlockSpec((3, h // ns, w), lambda i, c_ref: (0, i, 0)))
        out_specs.append(pl.BlockSpec((h // ns, w), lambda i, c_ref: (i, 0)))
        out_shape.append(SDS((h, w), F32))

    def body(c_ref, *refs):
        del c_ref
        for a in range(n):
            x_ref = refs[n + a]
            refs[2 * n + a][...] = (refs[a][0] + x_ref[0].astype(F32) + x_ref[1].astype(F32)
                                    + x_ref[2].astype(F32))

    return _pcall(body, grid_spec=pltpu.PrefetchScalarGridSpec(
        num_scalar_prefetch=1, grid=(ns,), in_specs=in_specs, out_specs=out_specs),
        out_shape=out_shape, compiler_params=_cp(("arbitrary",)), name="sum_chips")(chip_idx, *hs, *xs)


def _swap_totals(tots):
    n = len(tots)

    def body(*refs):
        send_sems, recv_sems, local_sems = refs[2 * n:]
        x, y, c = _coords()
        local = [pltpu.make_async_copy(refs[a], refs[n + a].at[c], local_sems.at[a]) for a in range(n)]
        cps = [_rcopy(refs[a], refs[n + a].at[c], send_sems, recv_sems, a, (x, y, 1 - c)) for a in range(n)]
        for cp in local + cps:
            cp.start()
        for cp in cps:
            cp.wait_send()
        for a in range(n):
            _rcopy(refs[a], refs[n + a].at[1 - c], send_sems, recv_sems, a, (x, y, 1 - c)).wait_recv()
        for cp in local:
            cp.wait()

    return _pcall(body, in_specs=[ANY] * n, out_specs=[ANY] * n,
                  out_shape=[SDS((2,) + t.shape, t.dtype) for t in tots],
                  scratch_shapes=[pltpu.SemaphoreType.DMA((n,)), pltpu.SemaphoreType.DMA((n,)),
                                  pltpu.SemaphoreType.DMA((n,))],
                  name="swap_totals")(*tots)


def _allreduce_small(buf):
    rows = buf.shape[0]

    def body(b_ref, o_ref, g_ref, send_sems, recv_sems):
        x, y, c = _coords()
        me = 4 * x + 2 * y + c
        g_ref[me] = b_ref[...]
        cps = []
        for k in range(1, 8):
            px = 1 - x if k & 4 else x
            py = 1 - y if k & 2 else y
            pc = 1 - c if k & 1 else c
            cps.append(pltpu.make_async_remote_copy(
                src_ref=b_ref, dst_ref=g_ref.at[me], send_sem=send_sems.at[k - 1],
                recv_sem=recv_sems.at[k - 1], device_id=(px, py, pc), device_id_type=MESH))
        for cp in cps:
            cp.start()
        for cp in cps:
            cp.wait()
        acc = g_ref[0]
        for d in range(1, 8):
            acc = acc + g_ref[d]
        o_ref[...] = acc

    vm = pl.BlockSpec(memory_space=pltpu.VMEM)
    return _pcall(body, in_specs=[vm], out_specs=vm, out_shape=SDS(buf.shape, F32),
                  scratch_shapes=[pltpu.VMEM((8, rows, 128), F32), pltpu.SemaphoreType.DMA((7,)),
                                  pltpu.SemaphoreType.DMA((7,))],
                  compiler_params=pltpu.CompilerParams(vmem_limit_bytes=VMEM_LIMIT),
                  name="allreduce_small")(buf)


def _flat128(a):
    return a.reshape(-1, 128)


def _pad_rows(a, mult):
    r = (-a.shape[0]) % mult
    return a if r == 0 else jnp.concatenate([a, jnp.zeros((r, a.shape[1]), a.dtype)], axis=0)


def _pack_flat(parts, mult):
    rows, offs, r0 = [], [], 0
    for p in parts:
        f = p.reshape(-1)
        pad = (-f.shape[0]) % 128
        if pad:
            f = jnp.concatenate([f, jnp.zeros((pad,), f.dtype)])
        f = f.reshape(-1, 128)
        rows.append(f)
        offs.append((r0, p.shape))
        r0 += f.shape[0]
    return _pad_rows(jnp.concatenate(rows, axis=0), mult), offs


def _unpack_flat(buf, offs):
    out = []
    for r0, shape in offs:
        n = 1
        for s in shape:
            n *= s
        nr = -(-n // 128)
        out.append(buf[r0:r0 + nr].reshape(-1)[:n].reshape(shape))
    return out


def _gates_to_rows(a, heads, chunk, cpb):
    t = a.shape[0]
    return a.reshape(t // (chunk * cpb), cpb, chunk, heads).transpose(0, 3, 1, 2)


def _rows_to_gates(a):
    nb, heads, cpb, chunk = a.shape
    return a.transpose(0, 2, 3, 1).reshape(nb * cpb * chunk, heads)


def kernel(x, norm_w, gdn_w_in, gdn_conv_w, gdn_a_log, gdn_dt_bias, gdn_norm_w, gdn_w_out, ssd_w_in, ssd_conv_w, ssd_conv_b, ssd_dt_bias, ssd_a_log, ssd_d, ssd_norm_w, ssd_w_out, final_norm_w, loss_target, m_norm_w, m_gdn_w_in, m_gdn_conv_w, m_gdn_a_log, m_gdn_dt_bias, m_gdn_norm_w, m_gdn_w_out, m_ssd_w_in, m_ssd_conv_w, m_ssd_conv_b, m_ssd_dt_bias, m_ssd_a_log, m_ssd_d, m_ssd_norm_w, m_ssd_w_out, m_final_norm_w, v_norm_w, v_gdn_w_in, v_gdn_conv_w, v_gdn_a_log, v_gdn_dt_bias, v_gdn_norm_w, v_gdn_w_out, v_ssd_w_in, v_ssd_conv_w, v_ssd_conv_b, v_ssd_dt_bias, v_ssd_a_log, v_ssd_d, v_ssd_norm_w, v_ssd_w_out, v_final_norm_w):
    ws = dict(norm_w=norm_w, gdn_w_in=gdn_w_in, gdn_conv_w=gdn_conv_w, gdn_a_log=gdn_a_log,
              gdn_dt_bias=gdn_dt_bias, gdn_norm_w=gdn_norm_w, gdn_w_out=gdn_w_out, ssd_w_in=ssd_w_in,
              ssd_conv_w=ssd_conv_w, ssd_conv_b=ssd_conv_b, ssd_dt_bias=ssd_dt_bias,
              ssd_a_log=ssd_a_log, ssd_d=ssd_d, ssd_norm_w=ssd_norm_w, ssd_w_out=ssd_w_out,
              final_norm_w=final_norm_w)
    ms = dict(norm_w=m_norm_w, gdn_w_in=m_gdn_w_in, gdn_conv_w=m_gdn_conv_w, gdn_a_log=m_gdn_a_log,
              gdn_dt_bias=m_gdn_dt_bias, gdn_norm_w=m_gdn_norm_w, gdn_w_out=m_gdn_w_out,
              ssd_w_in=m_ssd_w_in, ssd_conv_w=m_ssd_conv_w, ssd_conv_b=m_ssd_conv_b,
              ssd_dt_bias=m_ssd_dt_bias, ssd_a_log=m_ssd_a_log, ssd_d=m_ssd_d,
              ssd_norm_w=m_ssd_norm_w, ssd_w_out=m_ssd_w_out, final_norm_w=m_final_norm_w)
    vs = dict(norm_w=v_norm_w, gdn_w_in=v_gdn_w_in, gdn_conv_w=v_gdn_conv_w, gdn_a_log=v_gdn_a_log,
              gdn_dt_bias=v_gdn_dt_bias, gdn_norm_w=v_gdn_norm_w, gdn_w_out=v_gdn_w_out,
              ssd_w_in=v_ssd_w_in, ssd_conv_w=v_ssd_conv_w, ssd_conv_b=v_ssd_conv_b,
              ssd_dt_bias=v_ssd_dt_bias, ssd_a_log=v_ssd_a_log, ssd_d=v_ssd_d,
              ssd_norm_w=v_ssd_norm_w, ssd_w_out=v_ssd_w_out, final_norm_w=v_final_norm_w)
    names = list(ws)
    cx, cy, cc = _coords()
    chip = 2 * cx + cy
    t = x.shape[1]
    x0 = x.reshape(t, D)
    tgt = loss_target.reshape(t, D)

    bigs = [p[0].astype(BF16) for p in (gdn_w_in, gdn_w_out, ssd_w_in, ssd_w_out)]
    small, small_offs = _pack_flat([gdn_conv_w[0], ssd_conv_w[0], ssd_conv_b[0], ssd_norm_w[0]], 8)
    a_gi, a_go, a_si, a_so, gsmall = _gather_weights(bigs, small)
    w_gi = jnp.concatenate([a_gi[b] for b in range(4)], axis=1)
    w_si = jnp.concatenate([a_si[b] for b in range(4)], axis=1)
    w_go = a_go.reshape(4 * OUT_SHARD, D)
    w_so = a_so.reshape(4 * OUT_SHARD, D)
    sm = [_unpack_flat(gsmall[b], small_offs) for b in range(4)]
    g_cw = jnp.concatenate([sm[b][0] for b in range(4)], axis=1)
    s_cw = jnp.concatenate([sm[b][1] for b in range(4)], axis=1)
    s_cb = jnp.concatenate([sm[b][2] for b in range(4)], axis=0)[None]
    s_nw = jnp.concatenate([sm[b][3] for b in range(4)], axis=0)[None]

    def pad_small(w):
        return jnp.concatenate([w, jnp.zeros((D, SMALL_W - w.shape[1]), w.dtype)], axis=1)

    wg_main, wg_small = w_gi[:, :G_MAIN], pad_small(w_gi[:, G_MAIN:])
    ws_main, ws_small = w_si[:, :S_MAIN], pad_small(w_si[:, S_MAIN:])
    zero_b = jnp.zeros((1, G_CONV), F32)
    nw0, nw1 = norm_w[0:1], norm_w[1:2]
    fw = final_norm_w[None]
    g_alog = gdn_a_log.reshape(GH, 1, 1)
    g_dtb = gdn_dt_bias.reshape(GH, 1, 1)
    g_nw = gdn_norm_w.reshape(1, GDK)
    s_dtb = ssd_dt_bias.reshape(SG, SR).T.reshape(SR, SG, 1, 1)
    s_alog = ssd_a_log.reshape(SG, SR).T.reshape(SR, SG, 1, 1)
    s_d = ssd_d.reshape(SG, SR).T.reshape(SR, SG, 1, 1)

    hid0 = _rms_fwd(x0, nw0, "rms0")
    pg = _matmul(hid0, wg_main, "nn", "gdn_in_proj", tn=1536)
    pg_small = _matmul(hid0, wg_small, "nn", "gdn_in_proj_small", tn=SMALL_W)
    post_q = _l2norm_scaled(GDK ** -0.5)
    post_k = _l2norm_scaled(1.0)
    q = _conv_fwd(pg, 0, g_cw[:, :G_QK], zero_b[:, :G_QK], GDK, post_q, "gdn_conv_q")
    k = _conv_fwd(pg, G_QK, g_cw[:, G_QK:2 * G_QK], zero_b[:, :G_QK], GDK, post_k, "gdn_conv_k")
    v = _conv_fwd(pg, 2 * G_QK, g_cw[:, 2 * G_QK:], zero_b[:, :G_V], GDK, _silu, "gdn_conv_v")
    braw = _gates_to_rows(pg_small[:, :GH], GH, GCH, GDN_CPB)
    araw = _gates_to_rows(pg_small[:, GH:2 * GH], GH, GCH, GDN_CPB)
    o, g_sall, g_tall = _gdn_fwd(q, k, v, braw, araw, g_alog, g_dtb)
    y0 = _gate_fwd(_gdn_gate, o, pg, G_CONV, g_nw, "gdn_gate")
    x1 = _matmul(y0, w_go, "nn", "gdn_out_proj", add=x0)

    hid1 = _rms_fwd(x1, nw1, "rms1")
    ps = _matmul(hid1, ws_main, "nn", "ssd_in_proj", tn=1536)
    ps_small = _matmul(hid1, ws_small, "nn", "ssd_in_proj_small", tn=SMALL_W)
    c_x, c_b, c_c = S_INNER, 2 * S_INNER, 2 * S_INNER + SG * SN
    post_s = _silu
    xs = _conv_fwd(ps, c_x, s_cw[:, :S_INNER], s_cb[:, :S_INNER], SR * SP, post_s, "ssd_conv_x")
    bm = _conv_fwd(ps, c_b, s_cw[:, S_INNER:S_INNER + SG * SN], s_cb[:, S_INNER:S_INNER + SG * SN], SN,
                   post_s, "ssd_conv_b")
    cm = _conv_fwd(ps, c_c, s_cw[:, S_INNER + SG * SN:], s_cb[:, S_INNER + SG * SN:], SN, post_s,
                   "ssd_conv_c")
    nbs = t // (SCH * SSD_CPB)
    dtraw = _gates_to_rows(ps_small[:, :SH], SH, SCH, SSD_CPB)
    dtraw = dtraw.reshape(nbs, SG, SR, SSD_CPB, SCH).transpose(0, 3, 2, 1, 4).reshape(nbs, SSD_CPB * SR, SG, 1, SCH)
    yss, s_sall = _ssd_fwd(xs, bm, cm, dtraw, s_dtb, s_alog, s_d)
    y1 = _gate_fwd(_ssd_gate, yss, ps, 0, s_nw, "ssd_gate")
    x2 = _matmul(y1, w_so, "nn", "ssd_out_proj", add=x1)

    dx2, d_fw, loss_row = _final_loss(x2, fw, tgt, "final_loss")

    dy1 = _matmul(dx2, w_so, "nt", "ssd_out_dx", out_dtype=BF16, tn=2048)
    d_wso = _matmul(y1, dx2, "tn", "ssd_out_dw")
    dyss, dz_s, d_snw = _gate_bwd(_ssd_gate, yss, ps, 0, s_nw, dy1, True, "ssd_gate_bwd")
    dxs, dbm, dcm, ddtraw, d_sdtb, d_salog, d_sd = _ssd_bwd(xs, bm, cm, dtraw, s_dtb, s_alog, s_d, s_sall, dyss)
    dps, dwx, dbx = _conv_dpre(ps, c_x, s_cw[:, :S_INNER], s_cb[:, :S_INNER], SR * SP, post_s, dxs, "ssd_dpre_x",
                               G_CONV, 0)
    dps, dwb, dbb = _conv_dpre(ps, c_b, s_cw[:, S_INNER:S_INNER + SG * SN], s_cb[:, S_INNER:S_INNER + SG * SN],
                               SN, post_s, dbm, "ssd_dpre_b", G_CONV, S_INNER, dps)
    dps, dwc, dbc = _conv_dpre(ps, c_c, s_cw[:, S_INNER + SG * SN:], s_cb[:, S_INNER + SG * SN:], SN, post_s,
                               dcm, "ssd_dpre_c", G_CONV, S_INNER + SG * SN, dps)
    d_scw = jnp.concatenate([dwx, dwb, dwc], axis=1)
    d_scb = jnp.concatenate([dbx, dbb, dbc], axis=1)
    dxbc = _conv_t(dps, s_cw, "ssd_conv_t")
    ddt = ddtraw.reshape(nbs, SSD_CPB, SR, SG, SCH).transpose(0, 3, 2, 1, 4).reshape(nbs, SH, SSD_CPB, SCH)
    ddt = _rows_to_gates(ddt)
    dsm_s = jnp.concatenate([ddt, jnp.zeros((t, SMALL_W - SH), F32)], axis=1).astype(BF16)
    dhid1 = _matmul(dz_s, ws_main[:, :S_INNER], "nt", "ssd_in_dx_z")
    dhid1 = _matmul(dxbc, ws_main[:, S_INNER:], "nt", "ssd_in_dx_xbc", add=dhid1)
    dhid1 = _matmul(dsm_s, ws_small, "nt", "ssd_in_dx_dt", add=dhid1, tk=SMALL_W)
    d_wsi = jnp.concatenate([_matmul(hid1, dz_s, "tn", "ssd_in_dw_z"),
                             _matmul(hid1, dxbc, "tn", "ssd_in_dw_xbc"),
                             _matmul(hid1, dsm_s, "tn", "ssd_in_dw_dt", tn=SMALL_W)[:, :SH]], axis=1)
    dx1, d_nw1 = _rms_bwd(x1, nw1, dx2, dhid1, "rms1_bwd")

    dy0 = _matmul(dx1, w_go, "nt", "gdn_out_dx", out_dtype=BF16, tn=2048)
    d_wgo = _matmul(y0, dx1, "tn", "gdn_out_dw")
    do, dz_g, d_gnw = _gate_bwd(_gdn_gate, o, pg, G_CONV, g_nw, dy0, False, "gdn_gate_bwd")
    dq, dk, dv, dbraw, daraw, d_galog, d_gdtb = _gdn_bwd(q, k, v, braw, araw, g_alog, g_dtb, g_sall, g_tall, do)
    dpg, dwq, _ = _conv_dpre(pg, 0, g_cw[:, :G_QK], zero_b[:, :G_QK], GDK, post_q, dq, "gdn_dpre_q", G_CONV, 0)
    dpg, dwk, _ = _conv_dpre(pg, G_QK, g_cw[:, G_QK:2 * G_QK], zero_b[:, :G_QK], GDK, post_k, dk, "gdn_dpre_k",
                             G_CONV, G_QK, dpg)
    dpg, dwv, _ = _conv_dpre(pg, 2 * G_QK, g_cw[:, 2 * G_QK:], zero_b[:, :G_V], GDK, _silu, dv, "gdn_dpre_v",
                             G_CONV, 2 * G_QK, dpg)
    d_gcw = jnp.concatenate([dwq, dwk, dwv], axis=1)
    dqkv = _conv_t(dpg, g_cw, "gdn_conv_t")
    dsm_g = jnp.concatenate([_rows_to_gates(dbraw), _rows_to_gates(daraw),
                             jnp.zeros((t, SMALL_W - 2 * GH), F32)], axis=1).astype(BF16)
    dhid0 = _matmul(dqkv, wg_main[:, :G_CONV], "nt", "gdn_in_dx_qkv")
    dhid0 = _matmul(dz_g, wg_main[:, G_CONV:], "nt", "gdn_in_dx_z", add=dhid0)
    dhid0 = _matmul(dsm_g, wg_small, "nt", "gdn_in_dx_ba", add=dhid0, tk=SMALL_W)
    d_wgi = jnp.concatenate([_matmul(hid0, dqkv, "tn", "gdn_in_dw_qkv"),
                             _matmul(hid0, dz_g, "tn", "gdn_in_dw_z"),
                             _matmul(hid0, dsm_g, "tn", "gdn_in_dw_ba", tn=SMALL_W)[:, :2 * GH]], axis=1)
    dx0, d_nw0 = _rms_bwd(x0, nw0, dx1, dhid0, "rms0_bwd")

    def in_blocks(dw):
        return dw.reshape(D, N_CHIPS, IN_SHARD).transpose(1, 0, 2).reshape(N_CHIPS, 2, D // 2, IN_SHARD)

    def out_blocks(dw):
        return dw.reshape(N_CHIPS, 2, OUT_SHARD // 2, D)

    gs = [in_blocks(d_wgi), out_blocks(d_wgo), in_blocks(d_wsi), out_blocks(d_wso)]
    from_sib = _swap_halves(gs)
    hsum, hsum_bf = _sum_cores(gs, from_sib, cc.astype(jnp.int32).reshape(1))
    recv = _scatter_chips(hsum_bf)
    tots = _sum_chips(hsum, recv, chip.astype(jnp.int32).reshape(1))
    full = _swap_totals(tots)
    grads = dict(
        gdn_w_in=full[0].reshape(1, D, IN_SHARD), gdn_w_out=full[1].reshape(1, OUT_SHARD, D),
        ssd_w_in=full[2].reshape(1, D, IN_SHARD), ssd_w_out=full[3].reshape(1, OUT_SHARD, D))

    small_parts = [loss_row, jnp.concatenate([d_nw0, d_nw1], axis=0), d_gcw, d_galog, d_gdtb, d_gnw, d_scw, d_scb,
                   d_sdtb, d_salog, d_sd, d_snw, d_fw]
    sbuf, soffs = _pack_flat(small_parts, 8)
    ssum = _unpack_flat(_allreduce_small(sbuf), soffs)
    (loss_s, g_nw_all, g_gcw, g_galog, g_gdtb, g_gnw, g_scw, g_scb, g_sdtb, g_salog, g_sd, g_snw, g_fw) = ssum

    def my_cols(a, width):
        return lax.dynamic_slice_in_dim(a, chip * width, width, axis=a.ndim - 1)

    grads.update(
        norm_w=g_nw_all, gdn_conv_w=my_cols(g_gcw, 1024)[None], gdn_a_log=g_galog.reshape(1, GH),
        gdn_dt_bias=g_gdtb.reshape(1, GH), gdn_norm_w=g_gnw.reshape(1, GDK),
        ssd_conv_w=my_cols(g_scw, 1024)[None], ssd_conv_b=my_cols(g_scb, 1024),
        ssd_dt_bias=g_sdtb.reshape(SR, SG).T.reshape(1, SH), ssd_a_log=g_salog.reshape(SR, SG).T.reshape(1, SH),
        ssd_d=g_sd.reshape(SR, SG).T.reshape(1, SH),
        ssd_norm_w=my_cols(g_snw, 512), final_norm_w=g_fw.reshape(D))
    loss = loss_s[0, 0]

    big_names = ("gdn_w_in", "gdn_w_out", "ssd_w_in", "ssd_w_out")
    deltas, new_m, new_v = {}, {}, {}
    for n in big_names:
        deltas[n], new_m[n], new_v[n] = _adamw(ws[n], grads[n], ms[n], vs[n], "adamw_" + n)
    rest = [n for n in names if n not in big_names]
    packs = [_pack_flat([d[n] for n in rest], 8) for d in (ws, grads, ms, vs)]
    outs = _adamw(*[p[0] for p in packs], "adamw_small")
    for d, buf in zip((deltas, new_m, new_v), outs):
        for n, a in zip(rest, _unpack_flat(buf, packs[0][1])):
            d[n] = a

    grad_x = dx0.reshape(1, t, D)
    return (loss, grad_x, *[grads[n] for n in names], *[deltas[n] for n in names],
            *[new_m[n] for n in names], *[new_v[n] for n in names])
```

```python
import functools

import jax
import jax.numpy as jnp
from jax import lax
from jax.experimental import pallas as pl
from jax.experimental.pallas import tpu as pltpu

F32 = jnp.float32
BF16 = jnp.bfloat16
SDS = jax.ShapeDtypeStruct
MESH = pl.DeviceIdType.MESH

D = 1024
EPS = 1e-6
CONV_K = 4
N_CHIPS = 4
GH = 16
GHQ = 8
GDK = 128
GCH = 64
G_QK = 1024
G_V = 2048
G_CONV = 4096
G_MAIN = 6144
G_IN = 6176
SH = 32
SP = 64
SN = 128
SG = 8
SR = 4
SCH = 128
S_INNER = 2048
S_MAIN = 6144
S_IN = 6176
IN_SHARD = 1544
OUT_SHARD = 512
SMALL_W = 128

ADAM_LR = 0.001
ADAM_B1 = 0.9
ADAM_B2 = 0.999
ADAM_EPS = 1e-08
ADAM_WD = 0.01
ADAM_STEP = 10

VMEM_LIMIT = 48 * 1024 * 1024
BLOCK_ELEMS = 256 * 1024
NEG = -1e30


def _pcall(body, **kw):
    return pl.pallas_call(body, **kw)


def _cp(sem=None, vmem=VMEM_LIMIT):
    return pltpu.CompilerParams(dimension_semantics=sem, vmem_limit_bytes=vmem)


def _sigmoid(x):
    return 1.0 / (1.0 + jnp.exp(-x))


def _silu(x):
    return x * _sigmoid(x)


def _softplus(x):
    return jnp.maximum(x, 0.0) + jnp.log(1.0 + jnp.exp(-jnp.abs(x)))


def _rms(x, w):
    return x * lax.rsqrt(jnp.mean(x * x, axis=-1, keepdims=True) + EPS) * w


_DIMS = {"nn": (((2,), (1,)), ((0,), (0,))),
         "nt": (((2,), (2,)), ((0,), (0,))),
         "tn": (((1,), (1,)), ((0,), (0,)))}


def _bdot(a, b, spec):
    return lax.dot_general(a.astype(BF16), b.astype(BF16), _DIMS[spec], preferred_element_type=F32)


@functools.partial(jax.custom_vjp, nondiff_argnums=(2,))
def _bmm(a, b, spec):
    return _bdot(a, b, spec)


def _bmm_fwd(a, b, spec):
    return _bdot(a, b, spec), (a, b)


def _bmm_bwd(spec, res, g):
    a, b = res
    if spec == "nn":
        return _bdot(g, b, "nt"), _bdot(a, g, "tn")
    if spec == "nt":
        return _bdot(g, b, "nn"), _bdot(g, a, "tn")
    return _bdot(b, g, "nt"), _bdot(a, g, "nn")


_bmm.defvjp(_bmm_fwd, _bmm_bwd)


@jax.custom_vjp
def _tri_inv(n):
    t = -n
    p = n
    steps = (n.shape[-1] - 1).bit_length() - 1
    r = lax.broadcasted_iota(jnp.int32, n.shape, 1)
    c = lax.broadcasted_iota(jnp.int32, n.shape, 2)
    t = t + jnp.where(r == c, 1.0, 0.0)
    for _ in range(steps):
        p = _bdot(p, p, "nn")
        t = t + _bdot(t, p, "nn")
    return t


def _tri_inv_fwd(n):
    t = _tri_inv(n)
    return t, t


def _tri_inv_bwd(t, g):
    return (-_bdot(_bdot(t, g, "tn"), t, "nt"),)


_tri_inv.defvjp(_tri_inv_fwd, _tri_inv_bwd)


@jax.custom_vjp
def _tri_inv_known(n, t):
    del n
    return t


def _tri_inv_known_fwd(n, t):
    del n
    return t, t


def _tri_inv_known_bwd(t, g):
    return _tri_inv_bwd(t, g)[0], jnp.zeros_like(t)


_tri_inv_known.defvjp(_tri_inv_known_fwd, _tri_inv_known_bwd)


def _masks(c, lead=1):
    r = lax.broadcasted_iota(jnp.int32, (lead, c, c), 1)
    s = lax.broadcasted_iota(jnp.int32, (lead, c, c), 2)
    return r >= s, r > s, r == s, r <= s


def _row_to_col(row, eye):
    return jnp.sum(jnp.where(eye, row, 0.0), axis=2, keepdims=True)


def _gdn_chunk(q, k, v, braw, araw, alog, dtb, s, t_known=None, want_t=False):
    h = v.shape[0]
    c = v.shape[1]
    rep = h // q.shape[0]
    tril, strict, eye, triu = _masks(c)
    qq = jnp.broadcast_to(q[:, None], (q.shape[0], rep) + q.shape[1:]).reshape(v.shape)
    kk = jnp.broadcast_to(k[:, None], (k.shape[0], rep) + k.shape[1:]).reshape(v.shape)
    beta_row = _sigmoid(braw)
    g_row = -jnp.exp(alog) * _softplus(araw + dtb)
    beta_col = _row_to_col(beta_row, eye)
    g_col = _row_to_col(g_row, eye)
    gc_col = jnp.sum(jnp.where(tril, g_row, 0.0), axis=2, keepdims=True)
    gc_row = jnp.sum(jnp.where(triu, g_col, 0.0), axis=1, keepdims=True)
    gc_last = jnp.sum(g_row, axis=2, keepdims=True)
    lmat = jnp.exp(jnp.where(tril, gc_col - gc_row, NEG))
    kb = kk * beta_col
    vb = v * beta_col
    n = jnp.where(strict, _bmm(kb, kk, "nt") * lmat, 0.0)
    t = _tri_inv(n) if t_known is None else _tri_inv_known(n, t_known)
    e_col = jnp.exp(gc_col)
    u = _bmm(t, vb, "nn")
    w = _bmm(t, kb * e_col, "nn")
    attn = _bmm(qq, kk, "nt") * lmat
    q_dec = qq * e_col
    k_dec = kk * jnp.exp(gc_last - gc_col)
    v_new = u - _bmm(w, s, "nn")
    o = _bmm(q_dec, s, "nn") + _bmm(attn, v_new, "nn")
    s_new = s * jnp.exp(gc_last) + _bmm(k_dec, v_new, "tn")
    return (o, s_new, t) if want_t else (o, s_new)


def _ssd_chunk(xs, bm, cm, dtraw, dtb, alog, dskip, s):
    c = xs.shape[1]
    tril, _, eye, triu = _masks(c)
    lane = lax.broadcasted_iota(jnp.int32, (1, 1, SR * SP), 2)
    prow = lax.broadcasted_iota(jnp.int32, (1, SR * SP, 1), 1)
    cb = _bmm(cm, bm, "nt")
    cs = _bmm(cm, s, "nt")

    def per_head(vals, idx):
        out = vals[SR - 1]
        for r in reversed(range(SR - 1)):
            out = jnp.where(idx < (r + 1) * SP, vals[r], out)
        return out

    dt_cols, e_cols, lmats, dstates, declast = [], [], [], [], []
    for r in range(SR):
        dt_row = _softplus(dtraw[r] + dtb[r])
        adt_row = -jnp.exp(alog[r]) * dt_row
        dt_cols.append(_row_to_col(dt_row, eye))
        adt_col = _row_to_col(adt_row, eye)
        acs_col = jnp.sum(jnp.where(tril, adt_row, 0.0), axis=2, keepdims=True)
        acs_row = jnp.sum(jnp.where(triu, adt_col, 0.0), axis=1, keepdims=True)
        acs_last = jnp.sum(adt_row, axis=2, keepdims=True)
        lmats.append(jnp.exp(jnp.where(tril, acs_col - acs_row, NEG)))
        e_cols.append(jnp.exp(acs_col))
        dstates.append(jnp.exp(acs_last - acs_col))
        declast.append(jnp.exp(acs_last))
    xd = xs * per_head(dt_cols, lane)
    y = per_head([_bmm(cb * lmats[r], xd, "nn") for r in range(SR)], lane)
    states = per_head([_bmm(xd, bm * dstates[r], "tn") for r in range(SR)], prow)
    y = y + cs * per_head(e_cols, lane) + xs * per_head([dskip[r] for r in range(SR)], lane)
    s_new = s * per_head(declast, prow) + states
    return y, s_new


def _matmul(a, b, mode, name, out_dtype=F32, add=None, tm=1024, tn=1024, tk=2048):
    if mode == "nn":
        (m, k), n = a.shape, b.shape[1]
    elif mode == "nt":
        (m, k), n = a.shape, b.shape[0]
    else:
        (k, m), n = a.shape, b.shape[1]
    tm, tn, tk = min(tm, m), min(tn, n), min(tk, k)
    assert m % tm == 0 and n % tn == 0 and k % tk == 0, (name, m, n, k)
    nk = k // tk
    dims = {"nn": (((1,), (0,)), ((), ())), "nt": (((1,), (1,)), ((), ())),
            "tn": (((0,), (0,)), ((), ()))}[mode]
    a_spec = {"nn": pl.BlockSpec((tm, tk), lambda i, j, kk: (i, kk)),
              "nt": pl.BlockSpec((tm, tk), lambda i, j, kk: (i, kk)),
              "tn": pl.BlockSpec((tk, tm), lambda i, j, kk: (kk, i))}[mode]
    b_spec = {"nn": pl.BlockSpec((tk, tn), lambda i, j, kk: (kk, j)),
              "nt": pl.BlockSpec((tn, tk), lambda i, j, kk: (j, kk)),
              "tn": pl.BlockSpec((tk, tn), lambda i, j, kk: (kk, j))}[mode]
    o_spec = pl.BlockSpec((tm, tn), lambda i, j, kk: (i, j))
    has_add = add is not None

    def body(*refs):
        a_ref, b_ref = refs[:2]
        add_ref = refs[2] if has_add else None
        o_ref = refs[2 + has_add]
        part = lax.dot_general(a_ref[...].astype(BF16), b_ref[...].astype(BF16), dims,
                               preferred_element_type=F32)

        def finish(r):
            if has_add:
                r = r + add_ref[...].astype(F32)
            o_ref[...] = r.astype(o_ref.dtype)

        if nk == 1:
            finish(part)
            return
        acc_ref = refs[3 + has_add]
        kk = pl.program_id(2)

        @pl.when(kk == 0)
        def _():
            acc_ref[...] = part

        @pl.when(jnp.logical_and(kk > 0, kk < nk - 1))
        def _():
            acc_ref[...] += part

        @pl.when(kk == nk - 1)
        def _():
            finish(acc_ref[...] + part)

    ins = [a, b] + ([add] if has_add else [])
    in_specs = [a_spec, b_spec] + ([o_spec] if has_add else [])
    scratch = [] if nk == 1 else [pltpu.VMEM((tm, tn), F32)]
    return _pcall(body, grid=(m // tm, n // tn, nk), in_specs=in_specs, out_specs=o_spec,
                  out_shape=SDS((m, n), out_dtype), scratch_shapes=scratch,
                  compiler_params=_cp(("parallel", "parallel", "arbitrary")), name=name)(*ins)


def _ew(fn, ins, in_specs, out_shape, out_specs, grid, name):
    n_in = len(ins)

    def body(*refs):
        outs = fn(*[r[...] for r in refs[:n_in]])
        for r, o in zip(refs[n_in:], outs):
            r[...] = o.astype(r.dtype)

    return _pcall(body, grid=grid, in_specs=in_specs, out_specs=out_specs, out_shape=out_shape,
                  compiler_params=_cp(("arbitrary",) * len(grid)), name=name)(*ins)


def _ew_vjp(fn, ins, in_specs, cts, ct_specs, wrt, g_shape, g_specs, acc, grid, name):
    n_in, n_ct = len(ins), len(cts)

    def body(*refs):
        vals = [r[...].astype(F32) for r in refs[:n_in]]
        outs, vjp = jax.vjp(fn, *vals)
        g_all = vjp(tuple(r[...].astype(F32) for r in refs[n_in:n_in + n_ct]))
        for pos, (i, g_ref) in enumerate(zip(wrt, refs[n_in + n_ct:])):
            g = g_all[i]
            if pos in acc:
                first = functools.reduce(
                    jnp.logical_and, [pl.program_id(ax) == 0 for ax in range(acc[pos], len(grid))])

                @pl.when(first)
                def _():
                    g_ref[...] = jnp.zeros_like(g_ref)

                g_ref[...] += g.astype(g_ref.dtype)
            else:
                g_ref[...] = g.astype(g_ref.dtype)

    return _pcall(body, grid=grid, in_specs=list(in_specs) + list(ct_specs), out_specs=g_specs,
                  out_shape=g_shape, compiler_params=_cp(("arbitrary",) * len(grid)),
                  name=name)(*ins, *cts)


def _row_spec(tm, n):
    return pl.BlockSpec((tm, n), lambda i: (i, 0))


def _par_spec(n):
    return pl.BlockSpec((1, n), lambda i: (0, 0))


def _rms_fwd(x, w, name):
    t = x.shape[0]
    tm = min(t, 512)
    return _ew(lambda xv, wv: (_rms(xv, wv),), [x, w], [_row_spec(tm, D), _par_spec(D)],
               [SDS((t, D), BF16)], [_row_spec(tm, D)], (t // tm,), name)[0]


def _rms_bwd(x, w, dres, dhid, name):
    t = x.shape[0]
    tm = min(t, 512)
    return _ew_vjp(lambda xv, wv: (xv, _rms(xv, wv)), [x, w], [_row_spec(tm, D), _par_spec(D)],
                   [dres, dhid], [_row_spec(tm, D), _row_spec(tm, D)], (0, 1),
                   [SDS((t, D), F32), SDS((1, D), F32)], [_row_spec(tm, D), _par_spec(D)],
                   {1: 0}, (t // tm,), name)


def _final_loss(x, w, tgt, name):
    t = x.shape[0]
    tm = min(t, 512)

    def body(x_ref, w_ref, t_ref, dx_ref, dw_ref, l_ref):
        @pl.when(pl.program_id(0) == 0)
        def _():
            dw_ref[...] = jnp.zeros_like(dw_ref)
            l_ref[...] = jnp.zeros_like(l_ref)

        xv, wv = x_ref[...], w_ref[...]
        rstd = lax.rsqrt(jnp.mean(xv * xv, axis=-1, keepdims=True) + EPS)
        xh = xv * rstd
        err = xh * wv - t_ref[...]
        l_ref[...] += 0.5 * jnp.sum(jnp.mean(err * err, axis=-1, keepdims=True), axis=0, keepdims=True)
        dy = err * (1.0 / D)
        dw_ref[...] += jnp.sum(dy * xh, axis=0, keepdims=True)
        dxh = dy * wv
        dx_ref[...] = rstd * (dxh - xh * jnp.mean(dxh * xh, axis=-1, keepdims=True))

    return _pcall(body, grid=(t // tm,), in_specs=[_row_spec(tm, D), _par_spec(D), _row_spec(tm, D)],
                  out_specs=[_row_spec(tm, D), _par_spec(D), _par_spec(128)],
                  out_shape=[SDS((t, D), F32), SDS((1, D), F32), SDS((1, 128), F32)],
                  compiler_params=_cp(("arbitrary",)), name=name)(x, w, tgt)


def _conv_taps(ext, w_ref, tm, lo):
    n = ext.shape[0]
    acc = None
    for j in range(CONV_K):
        shift = (CONV_K - 1 - j) if lo else (n - (CONV_K - 1 - j)) % n
        rolled = pltpu.roll(ext, shift, 0) if shift else ext
        term = w_ref[pl.ds(j, 1), :] * rolled[lo:lo + tm]
        acc = term if acc is None else acc + term
    return acc


def _conv_pre_specs(tm, ct, col0):
    hb = tm // 8
    return [pl.BlockSpec((tm, ct), lambda j, i: (i, col0 + j)),
            pl.BlockSpec((8, ct), lambda j, i: (jnp.maximum(i * hb - 1, 0), col0 + j)),
            pl.BlockSpec((CONV_K, ct), lambda j, i: (0, j)),
            pl.BlockSpec((1, ct), lambda j, i: (0, j))]


def _conv_pre_value(x_ref, xh_ref, w_ref, b_ref, tm):
    halo = jnp.where(pl.program_id(1) > 0, xh_ref[...], 0.0)
    ext = jnp.concatenate([halo, x_ref[...]], axis=0)
    return _conv_taps(ext, w_ref, tm, 8) + b_ref[...], ext


def _conv_fwd(x, col0, w, b, ct, post, name):
    t = x.shape[0]
    tm = min(t, BLOCK_ELEMS // ct)
    nt = w.shape[1] // ct

    def body(x_ref, xh_ref, w_ref, b_ref, o_ref):
        pre, _ = _conv_pre_value(x_ref, xh_ref, w_ref, b_ref, tm)
        o_ref[0] = post(pre)

    return _pcall(body, grid=(nt, t // tm), in_specs=_conv_pre_specs(tm, ct, col0 // ct),
                  out_specs=pl.BlockSpec((1, tm, ct), lambda j, i: (j, i, 0)),
                  out_shape=SDS((nt, t, ct), F32), compiler_params=_cp(("arbitrary", "arbitrary")),
                  name=name)(x, x, w, b)


def _conv_dpre(x, col0, w, b, ct, post, dout, name, c_total, c_off, into=None):
    t = x.shape[0]
    tm = min(t, BLOCK_ELEMS // ct)
    nt = w.shape[1] // ct
    chained = into is not None

    def body(*refs):
        x_ref, xh_ref, w_ref, b_ref, do_ref = refs[:5]
        dp_ref, dw_ref, db_ref = refs[5 + chained:]
        pre, ext = _conv_pre_value(x_ref, xh_ref, w_ref, b_ref, tm)
        _, vjp = jax.vjp(post, pre)
        dpre = vjp(do_ref[0])[0]
        dp_ref[...] = dpre

        @pl.when(pl.program_id(1) == 0)
        def _():
            dw_ref[...] = jnp.zeros_like(dw_ref)
            db_ref[...] = jnp.zeros_like(db_ref)

        for j in range(CONV_K):
            xs = (pltpu.roll(ext, CONV_K - 1 - j, 0) if j < CONV_K - 1 else ext)[8:8 + tm]
            dw_ref[pl.ds(j, 1), :] += jnp.sum(dpre * xs, axis=0, keepdims=True)
        db_ref[...] += jnp.sum(dpre, axis=0, keepdims=True)

    c = w.shape[1]
    o0 = c_off // ct
    return _pcall(body, grid=(nt, t // tm),
                  in_specs=_conv_pre_specs(tm, ct, col0 // ct)
                  + [pl.BlockSpec((1, tm, ct), lambda j, i: (j, i, 0))] + ([ANY] if chained else []),
                  out_specs=[pl.BlockSpec((tm, ct), lambda j, i: (i, o0 + j)),
                             pl.BlockSpec((CONV_K, ct), lambda j, i: (0, j)),
                             pl.BlockSpec((1, ct), lambda j, i: (0, j))],
                  out_shape=[SDS((t, c_total), F32), SDS((CONV_K, c), F32), SDS((1, c), F32)],
                  input_output_aliases={5: 0} if chained else {},
                  compiler_params=_cp(("arbitrary", "arbitrary")), name=name)(
                      x, x, w, b, dout, *([into] if chained else []))


def _conv_t(dpre, w, name):
    t, c = dpre.shape
    tm = min(t, 512)
    ct = min(c, 512)
    hb = tm // 8
    last = t // tm - 1

    def body(d_ref, dh_ref, w_ref, o_ref):
        halo = jnp.where(pl.program_id(1) < last, dh_ref[...], 0.0)
        ext = jnp.concatenate([d_ref[...], halo], axis=0)
        o_ref[...] = _conv_taps(ext, w_ref, tm, 0).astype(o_ref.dtype)

    return _pcall(body, grid=(c // ct, t // tm),
                  in_specs=[pl.BlockSpec((tm, ct), lambda j, i: (i, j)),
                            pl.BlockSpec((8, ct), lambda j, i: (jnp.minimum((i + 1) * hb, t // 8 - 1), j)),
                            pl.BlockSpec((CONV_K, ct), lambda j, i: (0, j))],
                  out_specs=pl.BlockSpec((tm, ct), lambda j, i: (i, j)),
                  out_shape=SDS((t, c), BF16), compiler_params=_cp(("arbitrary", "arbitrary")),
                  name=name)(dpre, dpre, w)


def _l2norm_scaled(scale):
    def post(pre):
        a = _silu(pre)
        return a * lax.rsqrt(jnp.sum(a * a, axis=-1, keepdims=True) + EPS) * scale
    return post


GDN_HB = 16
GDN_CPB = 2
SSD_GB = 8
SSD_CPB = 1


def _gdn_specs(nb, rev):
    hb, cpb, tc = GDN_HB, GDN_CPB, GDN_CPB * GCH
    blk = (lambda n: nb - 1 - n) if rev else (lambda n: n)
    seq = lambda h: pl.BlockSpec((h, tc, GDK), lambda g, n: (g, blk(n), 0))
    gate = pl.BlockSpec((1, hb, cpb, GCH), lambda g, n: (blk(n), g, 0, 0))
    par = pl.BlockSpec((hb, 1, 1), lambda g, n: (g, 0, 0))
    state = pl.BlockSpec((cpb, hb, GDK, GDK), lambda g, n: (blk(n), g, 0, 0))
    tinv = pl.BlockSpec((cpb, hb, GCH, GCH), lambda g, n: (blk(n), g, 0, 0))
    return seq, gate, par, state, tinv


def _gdn_fwd(q, k, v, braw, araw, alog, dtb):
    t = v.shape[1]
    hb, cpb = GDN_HB, GDN_CPB
    nb = t // (cpb * GCH)
    seq, gate, par, state, tinv = _gdn_specs(nb, False)

    def body(q_ref, k_ref, v_ref, b_ref, a_ref, al_ref, dt_ref, o_ref, sall_ref, tall_ref, s_ref):
        @pl.when(pl.program_id(1) == 0)
        def _():
            s_ref[...] = jnp.zeros_like(s_ref)

        s = s_ref[...]
        for c in range(cpb):
            rows = pl.ds(c * GCH, GCH)
            sall_ref[c] = s
            o, s, tmat = _gdn_chunk(q_ref[:, rows, :], k_ref[:, rows, :], v_ref[:, rows, :],
                                    b_ref[0, :, pl.ds(c, 1), :], a_ref[0, :, pl.ds(c, 1), :],
                                    al_ref[...], dt_ref[...], s, want_t=True)
            o_ref[:, rows, :] = o
            tall_ref[c] = tmat.astype(BF16)
        s_ref[...] = s

    return _pcall(body, grid=(GH // hb, nb),
                  in_specs=[seq(hb // 2), seq(hb // 2), seq(hb), gate, gate, par, par],
                  out_specs=[seq(hb), state, tinv],
                  out_shape=[SDS((GH, t, GDK), F32), SDS((t // GCH, GH, GDK, GDK), F32),
                             SDS((t // GCH, GH, GCH, GCH), BF16)],
                  scratch_shapes=[pltpu.VMEM((hb, GDK, GDK), F32)],
                  compiler_params=_cp(("arbitrary", "arbitrary")), name="gdn_chunk_fwd")(
                      q, k, v, braw, araw, alog, dtb)


def _gdn_bwd(q, k, v, braw, araw, alog, dtb, sall, tall, do):
    t = v.shape[1]
    hb, cpb = GDN_HB, GDN_CPB
    nb = t // (cpb * GCH)
    seq, gate, par, state, tinv = _gdn_specs(nb, True)

    def body(q_ref, k_ref, v_ref, b_ref, a_ref, al_ref, dt_ref, sall_ref, tall_ref, do_ref,
             dq_ref, dk_ref, dv_ref, db_ref, da_ref, dal_ref, ddt_ref, ds_ref):
        @pl.when(pl.program_id(1) == 0)
        def _():
            ds_ref[...] = jnp.zeros_like(ds_ref)
            dal_ref[...] = jnp.zeros_like(dal_ref)
            ddt_ref[...] = jnp.zeros_like(ddt_ref)

        ds = ds_ref[...]
        for c in reversed(range(cpb)):
            rows = pl.ds(c * GCH, GCH)
            fn = functools.partial(_gdn_chunk, t_known=tall_ref[c].astype(F32))
            _, vjp = jax.vjp(fn, q_ref[:, rows, :], k_ref[:, rows, :], v_ref[:, rows, :],
                             b_ref[0, :, pl.ds(c, 1), :], a_ref[0, :, pl.ds(c, 1), :],
                             al_ref[...], dt_ref[...], sall_ref[c])
            dq, dk, dv, db, da, dal, ddt, ds = vjp((do_ref[:, rows, :], ds))
            dq_ref[:, rows, :] = dq
            dk_ref[:, rows, :] = dk
            dv_ref[:, rows, :] = dv
            db_ref[0, :, pl.ds(c, 1), :] = db
            da_ref[0, :, pl.ds(c, 1), :] = da
            dal_ref[...] += dal
            ddt_ref[...] += ddt
        ds_ref[...] = ds

    return _pcall(body, grid=(GH // hb, nb),
                  in_specs=[seq(hb // 2), seq(hb // 2), seq(hb), gate, gate, par, par, state, tinv, seq(hb)],
                  out_specs=[seq(hb // 2), seq(hb // 2), seq(hb), gate, gate, par, par],
                  out_shape=[SDS(q.shape, F32), SDS(k.shape, F32), SDS(v.shape, F32),
                             SDS(braw.shape, F32), SDS(araw.shape, F32),
                             SDS((GH, 1, 1), F32), SDS((GH, 1, 1), F32)],
                  scratch_shapes=[pltpu.VMEM((hb, GDK, GDK), F32)],
                  compiler_params=_cp(("arbitrary", "arbitrary")), name="gdn_chunk_bwd")(
                      q, k, v, braw, araw, alog, dtb, sall, tall, do)


def _ssd_specs(nb, rev):
    gb, cpb, tc = SSD_GB, SSD_CPB, SSD_CPB * SCH
    blk = (lambda n: nb - 1 - n) if rev else (lambda n: n)
    seq = lambda w: pl.BlockSpec((gb, tc, w), lambda g, n: (g, blk(n), 0))
    gate = pl.BlockSpec((1, cpb * SR, gb, 1, SCH), lambda g, n: (blk(n), 0, g, 0, 0))
    par = pl.BlockSpec((SR, gb, 1, 1), lambda g, n: (0, g, 0, 0))
    state = pl.BlockSpec((cpb, gb, SR * SP, SN), lambda g, n: (blk(n), g, 0, 0))
    return seq, gate, par, state


def _ssd_fwd(xs, bm, cm, dtraw, dtb, alog, dskip):
    t = xs.shape[1]
    gb, cpb = SSD_GB, SSD_CPB
    nb = t // (cpb * SCH)
    seq, gate, par, state = _ssd_specs(nb, False)

    def body(x_ref, b_ref, c_ref, dt_ref, dtb_ref, al_ref, dk_ref, y_ref, sall_ref, s_ref):
        @pl.when(pl.program_id(1) == 0)
        def _():
            s_ref[...] = jnp.zeros_like(s_ref)

        s = s_ref[...]
        for c in range(cpb):
            rows = pl.ds(c * SCH, SCH)
            sall_ref[c] = s
            y, s = _ssd_chunk(x_ref[:, rows, :], b_ref[:, rows, :], c_ref[:, rows, :],
                              dt_ref[0, pl.ds(c * SR, SR)], dtb_ref[...], al_ref[...],
                              dk_ref[...], s)
            y_ref[:, rows, :] = y
        s_ref[...] = s

    return _pcall(body, grid=(SG // gb, nb),
                  in_specs=[seq(SR * SP), seq(SN), seq(SN), gate, par, par, par],
                  out_specs=[seq(SR * SP), state],
                  out_shape=[SDS((SG, t, SR * SP), F32), SDS((t // SCH, SG, SR * SP, SN), F32)],
                  scratch_shapes=[pltpu.VMEM((gb, SR * SP, SN), F32)],
                  compiler_params=_cp(("arbitrary", "arbitrary")), name="ssd_chunk_fwd")(
                      xs, bm, cm, dtraw, dtb, alog, dskip)


def _ssd_bwd(xs, bm, cm, dtraw, dtb, alog, dskip, sall, dy):
    t = xs.shape[1]
    gb, cpb = SSD_GB, SSD_CPB
    nb = t // (cpb * SCH)
    seq, gate, par, state = _ssd_specs(nb, True)

    def body(x_ref, b_ref, c_ref, dt_ref, dtb_ref, al_ref, dk_ref, sall_ref, dy_ref,
             dx_ref, dbm_ref, dcm_ref, ddt_ref, ddtb_ref, dal_ref, ddk_ref, ds_ref):
        @pl.when(pl.program_id(1) == 0)
        def _():
            ds_ref[...] = jnp.zeros_like(ds_ref)
            ddtb_ref[...] = jnp.zeros_like(ddtb_ref)
            dal_ref[...] = jnp.zeros_like(dal_ref)
            ddk_ref[...] = jnp.zeros_like(ddk_ref)

        ds = ds_ref[...]
        for c in reversed(range(cpb)):
            rows = pl.ds(c * SCH, SCH)
            _, vjp = jax.vjp(_ssd_chunk, x_ref[:, rows, :], b_ref[:, rows, :], c_ref[:, rows, :],
                             dt_ref[0, pl.ds(c * SR, SR)], dtb_ref[...], al_ref[...],
                             dk_ref[...], sall_ref[c])
            dx, dbm, dcm, ddt, ddtb, dal, ddk, ds = vjp((dy_ref[:, rows, :], ds))
            dx_ref[:, rows, :] = dx
            dbm_ref[:, rows, :] = dbm
            dcm_ref[:, rows, :] = dcm
            ddt_ref[0, pl.ds(c * SR, SR)] = ddt
            ddtb_ref[...] += ddtb
            dal_ref[...] += dal
            ddk_ref[...] += ddk
        ds_ref[...] = ds

    return _pcall(body, grid=(SG // gb, nb),
                  in_specs=[seq(SR * SP), seq(SN), seq(SN), gate, par, par, par, state, seq(SR * SP)],
                  out_specs=[seq(SR * SP), seq(SN), seq(SN), gate, par, par, par],
                  out_shape=[SDS(xs.shape, F32), SDS(bm.shape, F32), SDS(cm.shape, F32),
                             SDS(dtraw.shape, F32), SDS((SR, SG, 1, 1), F32), SDS((SR, SG, 1, 1), F32),
                             SDS((SR, SG, 1, 1), F32)],
                  scratch_shapes=[pltpu.VMEM((gb, SR * SP, SN), F32)],
                  compiler_params=_cp(("arbitrary", "arbitrary")), name="ssd_chunk_bwd")(
                      xs, bm, cm, dtraw, dtb, alog, dskip, sall, dy)


def _gate_specs(tm, ct, zcol0, per_tile_w):
    z0 = zcol0 // ct
    return [pl.BlockSpec((1, tm, ct), lambda i, j: (j, i, 0)),
            pl.BlockSpec((tm, ct), lambda i, j: (i, z0 + j)),
            pl.BlockSpec((1, ct), (lambda i, j: (0, j)) if per_tile_w else (lambda i, j: (0, 0)))]


def _gdn_gate(o, z, w):
    return (_rms(o[0], w) * _silu(z),)


def _ssd_gate(y, z, w):
    return (_rms(y[0] * _silu(z), w),)


def _gate_fwd(fn, o, proj, zcol0, w, name):
    nt, t, ct = o.shape
    tm = min(t, BLOCK_ELEMS // ct)
    return _ew(fn, [o, proj, w], _gate_specs(tm, ct, zcol0, w.shape[1] > ct), [SDS((t, nt * ct), BF16)],
               [pl.BlockSpec((tm, ct), lambda i, j: (i, j))], (t // tm, nt), name)[0]


def _gate_bwd(fn, o, proj, zcol0, w, dy, wacc, name):
    nt, t, ct = o.shape
    tm = min(t, BLOCK_ELEMS // ct)
    specs = _gate_specs(tm, ct, zcol0, wacc)
    out_spec = pl.BlockSpec((tm, ct), lambda i, j: (i, j))
    if wacc:
        flip = lambda s: pl.BlockSpec(s.block_shape, lambda j, i, f=s.index_map: f(i, j))
        specs = [flip(s) for s in specs]
        out_spec = flip(out_spec)
        grid, acc = (nt, t // tm), {2: 1}
    else:
        grid, acc = (t // tm, nt), {2: 0}
    return _ew_vjp(fn, [o, proj, w], specs, [dy], [out_spec], (0, 1, 2),
                   [SDS(o.shape, F32), SDS((t, nt * ct), BF16), SDS(w.shape, F32)],
                   [specs[0], out_spec, specs[2]], acc, grid, name)


def _adamw_math(w, g, m, v):
    m = ADAM_B1 * m + (1.0 - ADAM_B1) * g
    v = ADAM_B2 * v + (1.0 - ADAM_B2) * jnp.square(g)
    m_hat = m / (1.0 - ADAM_B1 ** ADAM_STEP)
    v_hat = v / (1.0 - ADAM_B2 ** ADAM_STEP)
    delta = -ADAM_LR * (m_hat / (jnp.sqrt(v_hat) + ADAM_EPS) + ADAM_WD * w)
    return delta, m, v


def _adamw(w, g, m, v, name):
    shape = w.shape
    w2, g2, m2, v2 = [a.reshape(-1, shape[-1]) for a in (w, g, m, v)]
    r, c = w2.shape
    tr = 256 if r % 256 == 0 else r
    spec = pl.BlockSpec((tr, c), lambda i: (i, 0))
    outs = _ew(_adamw_math, [w2, g2, m2, v2], [spec] * 4, [SDS((r, c), F32)] * 3, [spec] * 3,
               (r // tr,), name)
    return [o.reshape(shape) for o in outs]


def _coords():
    return lax.axis_index("x"), lax.axis_index("y"), lax.axis_index("c")


def _other_chips(x, y):
    return [(1 - x, y), (x, 1 - y), (1 - x, 1 - y)]


ANY = pl.BlockSpec(memory_space=pl.ANY)


def _rcopy(src, dst, send_sems, recv_sems, k, to):
    return pltpu.make_async_remote_copy(src_ref=src, dst_ref=dst, send_sem=send_sems.at[k],
                                        recv_sem=recv_sems.at[k], device_id=to, device_id_type=MESH)


def _gather_weights(bigs, small):
    n = len(bigs)

    def body(*refs):
        in_refs, small_ref = refs[:n], refs[n]
        out_refs, osmall_ref = refs[n + 1:2 * n + 1], refs[2 * n + 1]
        send_sems, recv_sems = refs[2 * n + 2:]
        x, y, c = _coords()
        me = 2 * x + y
        sibling = (x, y, 1 - c)
        chips = _other_chips(x, y)
        halves = [b.shape[0] // 2 for b in bigs]
        mine = [pl.ds(c * h, h) for h in halves]
        theirs = [pl.ds((1 - c) * h, h) for h in halves]
        rc = functools.partial(_rcopy, send_sems=send_sems, recv_sems=recv_sems)

        first = []
        for a in range(n):
            for j, (px, py) in enumerate(chips):
                first.append(rc(in_refs[a].at[mine[a]], out_refs[a].at[me, mine[a]], k=3 * a + j, to=(px, py, c)))
        for j, (px, py) in enumerate(chips):
            first.append(rc(small_ref, osmall_ref.at[me], k=6 * n + j, to=(px, py, c)))
        for cp in first:
            cp.start()
        passed = []
        for a in range(n):
            for j, (px, py) in enumerate(chips):
                landed = out_refs[a].at[2 * px + py, mine[a]]
                rc(landed, landed, k=3 * a + j, to=(px, py, c)).wait_recv()
                fw = rc(landed, landed, k=3 * n + 3 * a + j, to=sibling)
                fw.start()
                passed.append(fw)
        for a in range(n):
            for j, (px, py) in enumerate(chips):
                landed = out_refs[a].at[2 * px + py, theirs[a]]
                rc(landed, landed, k=3 * n + 3 * a + j, to=sibling).wait_recv()
        for j, (px, py) in enumerate(chips):
            rc(small_ref, osmall_ref.at[2 * px + py], k=6 * n + j, to=(px, py, c)).wait_recv()
        for cp in first + passed:
            cp.wait_send()

    outs = _pcall(body, in_specs=[ANY] * (n + 1), out_specs=[ANY] * (n + 1),
                  out_shape=[SDS((N_CHIPS,) + b.shape, b.dtype) for b in bigs]
                  + [SDS((N_CHIPS,) + small.shape, small.dtype)],
                  scratch_shapes=[pltpu.SemaphoreType.DMA((6 * n + 3,)), pltpu.SemaphoreType.DMA((6 * n + 3,))],
                  name="gather_weights")(*bigs, small)
    me = 2 * lax.axis_index("x") + lax.axis_index("y")
    return [lax.dynamic_update_index_in_dim(o, own, me, 0) for o, own in zip(outs, list(bigs) + [small])]


def _swap_halves(gs):
    n = len(gs)

    def body(*refs):
        send_sems, recv_sems = refs[2 * n:]
        x, y, c = _coords()
        cps = [_rcopy(refs[a].at[:, 1 - c], refs[n + a], send_sems, recv_sems, a, (x, y, 1 - c))
               for a in range(n)]
        for cp in cps:
            cp.start()
        for cp in cps:
            cp.wait()

    return _pcall(body, in_specs=[ANY] * n, out_specs=[ANY] * n,
                  out_shape=[SDS((N_CHIPS,) + g.shape[2:], g.dtype) for g in gs],
                  scratch_shapes=[pltpu.SemaphoreType.DMA((n,)), pltpu.SemaphoreType.DMA((n,))],
                  name="swap_halves")(*gs)


def _sum_cores(gs, rs, half_idx):
    n = len(gs)
    ns = 2
    in_specs, out_specs, out_shape = [], [], []
    for g in gs:
        _, _, h, w = g.shape
        in_specs.append(pl.BlockSpec((1, 1, h // ns, w), lambda b, i, c_ref: (b, c_ref[0], i, 0)))
    for g in gs:
        _, _, h, w = g.shape
        spec = pl.BlockSpec((1, h // ns, w), lambda b, i, c_ref: (b, i, 0))
        in_specs.append(spec)
        out_specs += [spec, spec]
        out_shape += [SDS((N_CHIPS, h, w), F32), SDS((N_CHIPS, h, w), BF16)]

    def body(c_ref, *refs):
        del c_ref
        for a in range(n):
            tot = refs[a][0] + refs[n + a][...]
            refs[2 * n + 2 * a][...] = tot
            refs[2 * n + 2 * a + 1][...] = tot.astype(BF16)

    outs = _pcall(body, grid_spec=pltpu.PrefetchScalarGridSpec(
        num_scalar_prefetch=1, grid=(N_CHIPS, ns), in_specs=in_specs, out_specs=out_specs),
        out_shape=out_shape, compiler_params=_cp(("arbitrary", "arbitrary")), name="sum_cores")(
            half_idx, *gs, *rs)
    return outs[0::2], outs[1::2]


def _scatter_chips(hs):
    n = len(hs)

    def body(*refs):
        send_sems, recv_sems = refs[2 * n:]
        x, y, c = _coords()
        cps = []
        for a in range(n):
            for j, (px, py) in enumerate(_other_chips(x, y)):
                cps.append(_rcopy(refs[a].at[2 * px + py], refs[n + a].at[j], send_sems, recv_sems,
                                  3 * a + j, (px, py, c)))
        for cp in cps:
            cp.start()
        for cp in cps:
            cp.wait()

    return _pcall(body, in_specs=[ANY] * n, out_specs=[ANY] * n,
                  out_shape=[SDS((3,) + h.shape[1:], h.dtype) for h in hs],
                  scratch_shapes=[pltpu.SemaphoreType.DMA((3 * n,)), pltpu.SemaphoreType.DMA((3 * n,))],
                  name="scatter_chips")(*hs)


def _sum_chips(hs, xs, chip_idx):
    n = len(hs)
    ns = 2
    in_specs, out_specs, out_shape = [], [], []
    for h_arr in hs:
        _, h, w = h_arr.shape
        in_specs.append(pl.BlockSpec((1, h // ns, w), lambda i, c_ref: (c_ref[0], i, 0)))
    for h_arr in hs:
        _, h, w = h_arr.shape
        in_specs.append(pl.BlockSpec((3, h // ns, w), lambda i, c_ref: (0, i, 0)))
        out_specs.append(pl.BlockSpec((h // ns, w), lambda i, c_ref: (i, 0)))
        out_shape.append(SDS((h, w), F32))

    def body(c_ref, *refs):
        del c_ref
        for a in range(n):
            x_ref = refs[n + a]
            refs[2 * n + a][...] = (refs[a][0] + x_ref[0].astype(F32) + x_ref[1].astype(F32)
                                    + x_ref[2].astype(F32))

    return _pcall(body, grid_spec=pltpu.PrefetchScalarGridSpec(
        num_scalar_prefetch=1, grid=(ns,), in_specs=in_specs, out_specs=out_specs),
        out_shape=out_shape, compiler_params=_cp(("arbitrary",)), name="sum_chips")(chip_idx, *hs, *xs)


def _swap_totals(tots):
    n = len(tots)

    def body(*refs):
        send_sems, recv_sems = refs[2 * n:]
        x, y, c = _coords()
        cps = [_rcopy(refs[a], refs[n + a], send_sems, recv_sems, a, (x, y, 1 - c)) for a in range(n)]
        for cp in cps:
            cp.start()
        for cp in cps:
            cp.wait()

    return _pcall(body, in_specs=[ANY] * n, out_specs=[ANY] * n,
                  out_shape=[SDS(t.shape, t.dtype) for t in tots],
                  scratch_shapes=[pltpu.SemaphoreType.DMA((n,)), pltpu.SemaphoreType.DMA((n,))],
                  name="swap_totals")(*tots)


def _allreduce_small(buf):
    rows = buf.shape[0]

    def body(b_ref, o_ref, g_ref, send_sems, recv_sems):
        x, y, c = _coords()
        me = 4 * x + 2 * y + c
        g_ref[me] = b_ref[...]
        cps = []
        for k in range(1, 8):
            px = 1 - x if k & 4 else x
            py = 1 - y if k & 2 else y
            pc = 1 - c if k & 1 else c
            cps.append(pltpu.make_async_remote_copy(
                src_ref=b_ref, dst_ref=g_ref.at[me], send_sem=send_sems.at[k - 1],
                recv_sem=recv_sems.at[k - 1], device_id=(px, py, pc), device_id_type=MESH))
        for cp in cps:
            cp.start()
        for cp in cps:
            cp.wait()
        acc = g_ref[0]
        for d in range(1, 8):
            acc = acc + g_ref[d]
        o_ref[...] = acc

    vm = pl.BlockSpec(memory_space=pltpu.VMEM)
    return _pcall(body, in_specs=[vm], out_specs=vm, out_shape=SDS(buf.shape, F32),
                  scratch_shapes=[pltpu.VMEM((8, rows, 128), F32), pltpu.SemaphoreType.DMA((7,)),
                                  pltpu.SemaphoreType.DMA((7,))],
                  compiler_params=pltpu.CompilerParams(vmem_limit_bytes=VMEM_LIMIT),
                  name="allreduce_small")(buf)


def _flat128(a):
    return a.reshape(-1, 128)


def _pad_rows(a, mult):
    r = (-a.shape[0]) % mult
    return a if r == 0 else jnp.concatenate([a, jnp.zeros((r, a.shape[1]), a.dtype)], axis=0)


def _pack_flat(parts, mult):
    rows, offs, r0 = [], [], 0
    for p in parts:
        f = p.reshape(-1)
        pad = (-f.shape[0]) % 128
        if pad:
            f = jnp.concatenate([f, jnp.zeros((pad,), f.dtype)])
        f = f.reshape(-1, 128)
        rows.append(f)
        offs.append((r0, p.shape))
        r0 += f.shape[0]
    return _pad_rows(jnp.concatenate(rows, axis=0), mult), offs


def _unpack_flat(buf, offs):
    out = []
    for r0, shape in offs:
        n = 1
        for s in shape:
            n *= s
        nr = -(-n // 128)
        out.append(buf[r0:r0 + nr].reshape(-1)[:n].reshape(shape))
    return out


def _gates_to_rows(a, heads, chunk, cpb):
    t = a.shape[0]
    return a.reshape(t // (chunk * cpb), cpb, chunk, heads).transpose(0, 3, 1, 2)


def _rows_to_gates(a):
    nb, heads, cpb, chunk = a.shape
    return a.transpose(0, 2, 3, 1).reshape(nb * cpb * chunk, heads)


def kernel(x, norm_w, gdn_w_in, gdn_conv_w, gdn_a_log, gdn_dt_bias, gdn_norm_w, gdn_w_out, ssd_w_in, ssd_conv_w, ssd_conv_b, ssd_dt_bias, ssd_a_log, ssd_d, ssd_norm_w, ssd_w_out, final_norm_w, loss_target, m_norm_w, m_gdn_w_in, m_gdn_conv_w, m_gdn_a_log, m_gdn_dt_bias, m_gdn_norm_w, m_gdn_w_out, m_ssd_w_in, m_ssd_conv_w, m_ssd_conv_b, m_ssd_dt_bias, m_ssd_a_log, m_ssd_d, m_ssd_norm_w, m_ssd_w_out, m_final_norm_w, v_norm_w, v_gdn_w_in, v_gdn_conv_w, v_gdn_a_log, v_gdn_dt_bias, v_gdn_norm_w, v_gdn_w_out, v_ssd_w_in, v_ssd_conv_w, v_ssd_conv_b, v_ssd_dt_bias, v_ssd_a_log, v_ssd_d, v_ssd_norm_w, v_ssd_w_out, v_final_norm_w):
    ws = dict(norm_w=norm_w, gdn_w_in=gdn_w_in, gdn_conv_w=gdn_conv_w, gdn_a_log=gdn_a_log,
              gdn_dt_bias=gdn_dt_bias, gdn_norm_w=gdn_norm_w, gdn_w_out=gdn_w_out, ssd_w_in=ssd_w_in,
              ssd_conv_w=ssd_conv_w, ssd_conv_b=ssd_conv_b, ssd_dt_bias=ssd_dt_bias,
              ssd_a_log=ssd_a_log, ssd_d=ssd_d, ssd_norm_w=ssd_norm_w, ssd_w_out=ssd_w_out,
              final_norm_w=final_norm_w)
    ms = dict(norm_w=m_norm_w, gdn_w_in=m_gdn_w_in, gdn_conv_w=m_gdn_conv_w, gdn_a_log=m_gdn_a_log,
              gdn_dt_bias=m_gdn_dt_bias, gdn_norm_w=m_gdn_norm_w, gdn_w_out=m_gdn_w_out,
              ssd_w_in=m_ssd_w_in, ssd_conv_w=m_ssd_conv_w, ssd_conv_b=m_ssd_conv_b,
              ssd_dt_bias=m_ssd_dt_bias, ssd_a_log=m_ssd_a_log, ssd_d=m_ssd_d,
              ssd_norm_w=m_ssd_norm_w, ssd_w_out=m_ssd_w_out, final_norm_w=m_final_norm_w)
    vs = dict(norm_w=v_norm_w, gdn_w_in=v_gdn_w_in, gdn_conv_w=v_gdn_conv_w, gdn_a_log=v_gdn_a_log,
              gdn_dt_bias=v_gdn_dt_bias, gdn_norm_w=v_gdn_norm_w, gdn_w_out=v_gdn_w_out,
              ssd_w_in=v_ssd_w_in, ssd_conv_w=v_ssd_conv_w, ssd_conv_b=v_ssd_conv_b,
              ssd_dt_bias=v_ssd_dt_bias, ssd_a_log=v_ssd_a_log, ssd_d=v_ssd_d,
              ssd_norm_w=v_ssd_norm_w, ssd_w_out=v_ssd_w_out, final_norm_w=v_final_norm_w)
    names = list(ws)
    cx, cy, cc = _coords()
    chip = 2 * cx + cy
    t = x.shape[1]
    x0 = x.reshape(t, D)
    tgt = loss_target.reshape(t, D)

    bigs = [p[0].astype(BF16) for p in (gdn_w_in, gdn_w_out, ssd_w_in, ssd_w_out)]
    small, small_offs = _pack_flat([gdn_conv_w[0], ssd_conv_w[0], ssd_conv_b[0], ssd_norm_w[0]], 8)
    a_gi, a_go, a_si, a_so, gsmall = _gather_weights(bigs, small)
    w_gi = jnp.concatenate([a_gi[b] for b in range(4)], axis=1)
    w_si = jnp.concatenate([a_si[b] for b in range(4)], axis=1)
    w_go = a_go.reshape(4 * OUT_SHARD, D)
    w_so = a_so.reshape(4 * OUT_SHARD, D)
    sm = [_unpack_flat(gsmall[b], small_offs) for b in range(4)]
    g_cw = jnp.concatenate([sm[b][0] for b in range(4)], axis=1)
    s_cw = jnp.concatenate([sm[b][1] for b in range(4)], axis=1)
    s_cb = jnp.concatenate([sm[b][2] for b in range(4)], axis=0)[None]
    s_nw = jnp.concatenate([sm[b][3] for b in range(4)], axis=0)[None]

    def pad_small(w):
        return jnp.concatenate([w, jnp.zeros((D, SMALL_W - w.shape[1]), w.dtype)], axis=1)

    wg_main, wg_small = w_gi[:, :G_MAIN], pad_small(w_gi[:, G_MAIN:])
    ws_main, ws_small = w_si[:, :S_MAIN], pad_small(w_si[:, S_MAIN:])
    zero_b = jnp.zeros((1, G_CONV), F32)
    nw0, nw1 = norm_w[0:1], norm_w[1:2]
    fw = final_norm_w[None]
    g_alog = gdn_a_log.reshape(GH, 1, 1)
    g_dtb = gdn_dt_bias.reshape(GH, 1, 1)
    g_nw = gdn_norm_w.reshape(1, GDK)
    s_dtb = ssd_dt_bias.reshape(SG, SR).T.reshape(SR, SG, 1, 1)
    s_alog = ssd_a_log.reshape(SG, SR).T.reshape(SR, SG, 1, 1)
    s_d = ssd_d.reshape(SG, SR).T.reshape(SR, SG, 1, 1)

    hid0 = _rms_fwd(x0, nw0, "rms0")
    pg = _matmul(hid0, wg_main, "nn", "gdn_in_proj", tn=1536)
    pg_small = _matmul(hid0, wg_small, "nn", "gdn_in_proj_small", tn=SMALL_W)
    post_q = _l2norm_scaled(GDK ** -0.5)
    post_k = _l2norm_scaled(1.0)
    q = _conv_fwd(pg, 0, g_cw[:, :G_QK], zero_b[:, :G_QK], GDK, post_q, "gdn_conv_q")
    k = _conv_fwd(pg, G_QK, g_cw[:, G_QK:2 * G_QK], zero_b[:, :G_QK], GDK, post_k, "gdn_conv_k")
    v = _conv_fwd(pg, 2 * G_QK, g_cw[:, 2 * G_QK:], zero_b[:, :G_V], GDK, _silu, "gdn_conv_v")
    braw = _gates_to_rows(pg_small[:, :GH], GH, GCH, GDN_CPB)
    araw = _gates_to_rows(pg_small[:, GH:2 * GH], GH, GCH, GDN_CPB)
    o, g_sall, g_tall = _gdn_fwd(q, k, v, braw, araw, g_alog, g_dtb)
    y0 = _gate_fwd(_gdn_gate, o, pg, G_CONV, g_nw, "gdn_gate")
    x1 = _matmul(y0, w_go, "nn", "gdn_out_proj", add=x0)

    hid1 = _rms_fwd(x1, nw1, "rms1")
    ps = _matmul(hid1, ws_main, "nn", "ssd_in_proj", tn=1536)
    ps_small = _matmul(hid1, ws_small, "nn", "ssd_in_proj_small", tn=SMALL_W)
    c_x, c_b, c_c = S_INNER, 2 * S_INNER, 2 * S_INNER + SG * SN
    post_s = _silu
    xs = _conv_fwd(ps, c_x, s_cw[:, :S_INNER], s_cb[:, :S_INNER], SR * SP, post_s, "ssd_conv_x")
    bm = _conv_fwd(ps, c_b, s_cw[:, S_INNER:S_INNER + SG * SN], s_cb[:, S_INNER:S_INNER + SG * SN], SN,
                   post_s, "ssd_conv_b")
    cm = _conv_fwd(ps, c_c, s_cw[:, S_INNER + SG * SN:], s_cb[:, S_INNER + SG * SN:], SN, post_s,
                   "ssd_conv_c")
    nbs = t // (SCH * SSD_CPB)
    dtraw = _gates_to_rows(ps_small[:, :SH], SH, SCH, SSD_CPB)
    dtraw = dtraw.reshape(nbs, SG, SR, SSD_CPB, SCH).transpose(0, 3, 2, 1, 4).reshape(nbs, SSD_CPB * SR, SG, 1, SCH)
    yss, s_sall = _ssd_fwd(xs, bm, cm, dtraw, s_dtb, s_alog, s_d)
    y1 = _gate_fwd(_ssd_gate, yss, ps, 0, s_nw, "ssd_gate")
    x2 = _matmul(y1, w_so, "nn", "ssd_out_proj", add=x1)

    dx2, d_fw, loss_row = _final_loss(x2, fw, tgt, "final_loss")

    dy1 = _matmul(dx2, w_so, "nt", "ssd_out_dx", out_dtype=BF16, tn=2048)
    d_wso = _matmul(y1, dx2, "tn", "ssd_out_dw")
    dyss, dz_s, d_snw = _gate_bwd(_ssd_gate, yss, ps, 0, s_nw, dy1, True, "ssd_gate_bwd")
    dxs, dbm, dcm, ddtraw, d_sdtb, d_salog, d_sd = _ssd_bwd(xs, bm, cm, dtraw, s_dtb, s_alog, s_d, s_sall, dyss)
    dps, dwx, dbx = _conv_dpre(ps, c_x, s_cw[:, :S_INNER], s_cb[:, :S_INNER], SR * SP, post_s, dxs, "ssd_dpre_x",
                               G_CONV, 0)
    dps, dwb, dbb = _conv_dpre(ps, c_b, s_cw[:, S_INNER:S_INNER + SG * SN], s_cb[:, S_INNER:S_INNER + SG * SN],
                               SN, post_s, dbm, "ssd_dpre_b", G_CONV, S_INNER, dps)
    dps, dwc, dbc = _conv_dpre(ps, c_c, s_cw[:, S_INNER + SG * SN:], s_cb[:, S_INNER + SG * SN:], SN, post_s,
                               dcm, "ssd_dpre_c", G_CONV, S_INNER + SG * SN, dps)
    d_scw = jnp.concatenate([dwx, dwb, dwc], axis=1)
    d_scb = jnp.concatenate([dbx, dbb, dbc], axis=1)
    dxbc = _conv_t(dps, s_cw, "ssd_conv_t")
    ddt = ddtraw.reshape(nbs, SSD_CPB, SR, SG, SCH).transpose(0, 3, 2, 1, 4).reshape(nbs, SH, SSD_CPB, SCH)
    ddt = _rows_to_gates(ddt)
    dsm_s = jnp.concatenate([ddt, jnp.zeros((t, SMALL_W - SH), F32)], axis=1).astype(BF16)
    dhid1 = _matmul(dz_s, ws_main[:, :S_INNER], "nt", "ssd_in_dx_z")
    dhid1 = _matmul(dxbc, ws_main[:, S_INNER:], "nt", "ssd_in_dx_xbc", add=dhid1)
    dhid1 = _matmul(dsm_s, ws_small, "nt", "ssd_in_dx_dt", add=dhid1, tk=SMALL_W)
    d_wsi = jnp.concatenate([_matmul(hid1, dz_s, "tn", "ssd_in_dw_z"),
                             _matmul(hid1, dxbc, "tn", "ssd_in_dw_xbc"),
                             _matmul(hid1, dsm_s, "tn", "ssd_in_dw_dt", tn=SMALL_W)[:, :SH]], axis=1)
    dx1, d_nw1 = _rms_bwd(x1, nw1, dx2, dhid1, "rms1_bwd")

    dy0 = _matmul(dx1, w_go, "nt", "gdn_out_dx", out_dtype=BF16, tn=2048)
    d_wgo = _matmul(y0, dx1, "tn", "gdn_out_dw")
    do, dz_g, d_gnw = _gate_bwd(_gdn_gate, o, pg, G_CONV, g_nw, dy0, False, "gdn_gate_bwd")
    dq, dk, dv, dbraw, daraw, d_galog, d_gdtb = _gdn_bwd(q, k, v, braw, araw, g_alog, g_dtb, g_sall, g_tall, do)
    dpg, dwq, _ = _conv_dpre(pg, 0, g_cw[:, :G_QK], zero_b[:, :G_QK], GDK, post_q, dq, "gdn_dpre_q", G_CONV, 0)
    dpg, dwk, _ = _conv_dpre(pg, G_QK, g_cw[:, G_QK:2 * G_QK], zero_b[:, :G_QK], GDK, post_k, dk, "gdn_dpre_k",
                             G_CONV, G_QK, dpg)
    dpg, dwv, _ = _conv_dpre(pg, 2 * G_QK, g_cw[:, 2 * G_QK:], zero_b[:, :G_V], GDK, _silu, dv, "gdn_dpre_v",
                             G_CONV, 2 * G_QK, dpg)
    d_gcw = jnp.concatenate([dwq, dwk, dwv], axis=1)
    dqkv = _conv_t(dpg, g_cw, "gdn_conv_t")
    dsm_g = jnp.concatenate([_rows_to_gates(dbraw), _rows_to_gates(daraw),
                             jnp.zeros((t, SMALL_W - 2 * GH), F32)], axis=1).astype(BF16)
    dhid0 = _matmul(dqkv, wg_main[:, :G_CONV], "nt", "gdn_in_dx_qkv")
    dhid0 = _matmul(dz_g, wg_main[:, G_CONV:], "nt", "gdn_in_dx_z", add=dhid0)
    dhid0 = _matmul(dsm_g, wg_small, "nt", "gdn_in_dx_ba", add=dhid0, tk=SMALL_W)
    d_wgi = jnp.concatenate([_matmul(hid0, dqkv, "tn", "gdn_in_dw_qkv"),
                             _matmul(hid0, dz_g, "tn", "gdn_in_dw_z"),
                             _matmul(hid0, dsm_g, "tn", "gdn_in_dw_ba", tn=SMALL_W)[:, :2 * GH]], axis=1)
    dx0, d_nw0 = _rms_bwd(x0, nw0, dx1, dhid0, "rms0_bwd")

    def in_blocks(dw):
        return dw.reshape(D, N_CHIPS, IN_SHARD).transpose(1, 0, 2).reshape(N_CHIPS, 2, D // 2, IN_SHARD)

    def out_blocks(dw):
        return dw.reshape(N_CHIPS, 2, OUT_SHARD // 2, D)

    gs = [in_blocks(d_wgi), out_blocks(d_wgo), in_blocks(d_wsi), out_blocks(d_wso)]
    from_sib = _swap_halves(gs)
    hsum, hsum_bf = _sum_cores(gs, from_sib, cc.astype(jnp.int32).reshape(1))
    recv = _scatter_chips(hsum_bf)
    tots = _sum_chips(hsum, recv, chip.astype(jnp.int32).reshape(1))
    sib_tots = _swap_totals(tots)
    full = [jnp.concatenate([jnp.where(cc == 0, mine, sib), jnp.where(cc == 0, sib, mine)], axis=0)
            for mine, sib in zip(tots, sib_tots)]
    grads = dict(
        gdn_w_in=full[0].reshape(1, D, IN_SHARD), gdn_w_out=full[1].reshape(1, OUT_SHARD, D),
        ssd_w_in=full[2].reshape(1, D, IN_SHARD), ssd_w_out=full[3].reshape(1, OUT_SHARD, D))

    small_parts = [loss_row, jnp.concatenate([d_nw0, d_nw1], axis=0), d_gcw, d_galog, d_gdtb, d_gnw, d_scw, d_scb,
                   d_sdtb, d_salog, d_sd, d_snw, d_fw]
    sbuf, soffs = _pack_flat(small_parts, 8)
    ssum = _unpack_flat(_allreduce_small(sbuf), soffs)
    (loss_s, g_nw_all, g_gcw, g_galog, g_gdtb, g_gnw, g_scw, g_scb, g_sdtb, g_salog, g_sd, g_snw, g_fw) = ssum

    def my_cols(a, width):
        return lax.dynamic_slice_in_dim(a, chip * width, width, axis=a.ndim - 1)

    grads.update(
        norm_w=g_nw_all, gdn_conv_w=my_cols(g_gcw, 1024)[None], gdn_a_log=g_galog.reshape(1, GH),
        gdn_dt_bias=g_gdtb.reshape(1, GH), gdn_norm_w=g_gnw.reshape(1, GDK),
        ssd_conv_w=my_cols(g_scw, 1024)[None], ssd_conv_b=my_cols(g_scb, 1024),
        ssd_dt_bias=g_sdtb.reshape(SR, SG).T.reshape(1, SH), ssd_a_log=g_salog.reshape(SR, SG).T.reshape(1, SH),
        ssd_d=g_sd.reshape(SR, SG).T.reshape(1, SH),
        ssd_norm_w=my_cols(g_snw, 512), final_norm_w=g_fw.reshape(D))
    loss = loss_s[0, 0]

    big_names = ("gdn_w_in", "gdn_w_out", "ssd_w_in", "ssd_w_out")
    deltas, new_m, new_v = {}, {}, {}
    for n in big_names:
        deltas[n], new_m[n], new_v[n] = _adamw(ws[n], grads[n], ms[n], vs[n], "adamw_" + n)
    rest = [n for n in names if n not in big_names]
    packs = [_pack_flat([d[n] for n in rest], 8) for d in (ws, grads, ms, vs)]
    outs = _adamw(*[p[0] for p in packs], "adamw_small")
    for d, buf in zip((deltas, new_m, new_v), outs):
        for n, a in zip(rest, _unpack_flat(buf, packs[0][1])):
            d[n] = a

    grad_x = dx0.reshape(1, t, D)
    return (loss, grad_x, *[grads[n] for n in names], *[deltas[n] for n in names],
            *[new_m[n] for n in names], *[new_v[n] for n in names])
```

```python
import functools

import jax
import jax.numpy as jnp
from jax import lax
from jax.experimental import pallas as pl
from jax.experimental.pallas import tpu as pltpu

F32 = jnp.float32
BF16 = jnp.bfloat16
SDS = jax.ShapeDtypeStruct
MESH = pl.DeviceIdType.MESH

D = 1024
EPS = 1e-6
CONV_K = 4
N_CHIPS = 4
GH = 16
GHQ = 8
GDK = 128
GCH = 64
G_QK = 1024
G_V = 2048
G_CONV = 4096
G_MAIN = 6144
G_IN = 6176
SH = 32
SP = 64
SN = 128
SG = 8
SR = 4
SCH = 128
S_INNER = 2048
S_MAIN = 6144
S_IN = 6176
IN_SHARD = 1544
OUT_SHARD = 512
SMALL_W = 128

ADAM_LR = 0.001
ADAM_B1 = 0.9
ADAM_B2 = 0.999
ADAM_EPS = 1e-08
ADAM_WD = 0.01
ADAM_STEP = 10

VMEM_LIMIT = 48 * 1024 * 1024
BLOCK_ELEMS = 256 * 1024
FWD_COLS = 512
NEG = -1e30


def _pcall(body, **kw):
    return pl.pallas_call(body, **kw)


def _cp(sem=None, vmem=VMEM_LIMIT):
    return pltpu.CompilerParams(dimension_semantics=sem, vmem_limit_bytes=vmem)


@jax.custom_jvp
def _sigmoid(x):
    return 1.0 / (1.0 + jnp.exp(-x))


@_sigmoid.defjvp
def _sigmoid_jvp(primals, tangents):
    s = _sigmoid(primals[0])
    return s, tangents[0] * (s * (1.0 - s))


@jax.custom_jvp
def _silu(x):
    return x * _sigmoid(x)


@_silu.defjvp
def _silu_jvp(primals, tangents):
    x = primals[0]
    s = _sigmoid(x)
    return x * s, tangents[0] * (s * (1.0 + x * (1.0 - s)))


def _softplus(x):
    return jnp.maximum(x, 0.0) + jnp.log(1.0 + jnp.exp(-jnp.abs(x)))


def _rms(x, w):
    return x * lax.rsqrt(jnp.mean(x * x, axis=-1, keepdims=True) + EPS) * w


_DIMS = {"nn": (((2,), (1,)), ((0,), (0,))),
         "nt": (((2,), (2,)), ((0,), (0,))),
         "tn": (((1,), (1,)), ((0,), (0,)))}


def _bdot(a, b, spec):
    return lax.dot_general(a.astype(BF16), b.astype(BF16), _DIMS[spec], preferred_element_type=F32)


@functools.partial(jax.custom_vjp, nondiff_argnums=(2,))
def _bmm(a, b, spec):
    return _bdot(a, b, spec)


def _bmm_fwd(a, b, spec):
    return _bdot(a, b, spec), (a, b)


def _bmm_bwd(spec, res, g):
    a, b = res
    if spec == "nn":
        return _bdot(g, b, "nt"), _bdot(a, g, "tn")
    if spec == "nt":
        return _bdot(g, b, "nn"), _bdot(g, a, "tn")
    return _bdot(b, g, "nt"), _bdot(a, g, "nn")


_bmm.defvjp(_bmm_fwd, _bmm_bwd)


@jax.custom_vjp
def _tri_inv(n):
    t = -n
    p = n
    steps = (n.shape[-1] - 1).bit_length() - 1
    r = lax.broadcasted_iota(jnp.int32, n.shape, 1)
    c = lax.broadcasted_iota(jnp.int32, n.shape, 2)
    t = t + jnp.where(r == c, 1.0, 0.0)
    for _ in range(steps):
        p = _bdot(p, p, "nn")
        t = t + _bdot(t, p, "nn")
    return t


def _tri_inv_fwd(n):
    t = _tri_inv(n)
    return t, t


def _tri_inv_bwd(t, g):
    return (-_bdot(_bdot(t, g, "tn"), t, "nt"),)


_tri_inv.defvjp(_tri_inv_fwd, _tri_inv_bwd)


@jax.custom_vjp
def _tri_inv_known(n, t):
    del n
    return t


def _tri_inv_known_fwd(n, t):
    del n
    return t, t


def _tri_inv_known_bwd(t, g):
    return _tri_inv_bwd(t, g)[0], jnp.zeros_like(t)


_tri_inv_known.defvjp(_tri_inv_known_fwd, _tri_inv_known_bwd)


def _masks(c, lead=1):
    r = lax.broadcasted_iota(jnp.int32, (lead, c, c), 1)
    s = lax.broadcasted_iota(jnp.int32, (lead, c, c), 2)
    return r >= s, r > s, r == s, r <= s


def _row_to_col(row, eye):
    return jnp.sum(jnp.where(eye, row, 0.0), axis=2, keepdims=True)


def _gdn_chunk(q, k, v, braw, araw, alog, dtb, s, t_known=None, want_t=False):
    h = v.shape[0]
    c = v.shape[1]
    rep = h // q.shape[0]
    tril, strict, eye, triu = _masks(c)
    qq = jnp.broadcast_to(q[:, None], (q.shape[0], rep) + q.shape[1:]).reshape(v.shape)
    kk = jnp.broadcast_to(k[:, None], (k.shape[0], rep) + k.shape[1:]).reshape(v.shape)
    beta_row = _sigmoid(braw)
    g_row = -jnp.exp(alog) * _softplus(araw + dtb)
    beta_col = _row_to_col(beta_row, eye)
    g_col = _row_to_col(g_row, eye)
    gc_col = jnp.sum(jnp.where(tril, g_row, 0.0), axis=2, keepdims=True)
    gc_row = jnp.sum(jnp.where(triu, g_col, 0.0), axis=1, keepdims=True)
    gc_last = jnp.sum(g_row, axis=2, keepdims=True)
    lmat = jnp.exp(jnp.where(tril, gc_col - gc_row, NEG))
    kb = kk * beta_col
    vb = v * beta_col
    n = jnp.where(strict, _bmm(kb, kk, "nt") * lmat, 0.0)
    t = _tri_inv(n) if t_known is None else _tri_inv_known(n, t_known)
    e_col = jnp.exp(gc_col)
    u = _bmm(t, vb, "nn")
    w = _bmm(t, kb * e_col, "nn")
    attn = _bmm(qq, kk, "nt") * lmat
    q_dec = qq * e_col
    k_dec = kk * jnp.exp(gc_last - gc_col)
    v_new = u - _bmm(w, s, "nn")
    o = _bmm(q_dec, s, "nn") + _bmm(attn, v_new, "nn")
    s_new = s * jnp.exp(gc_last) + _bmm(k_dec, v_new, "tn")
    return (o, s_new, t) if want_t else (o, s_new)


def _ssd_chunk(xs, bm, cm, dtraw, dtb, alog, dskip, s):
    c = xs.shape[1]
    tril, _, eye, triu = _masks(c)
    lane = lax.broadcasted_iota(jnp.int32, (1, 1, SR * SP), 2)
    prow = lax.broadcasted_iota(jnp.int32, (1, SR * SP, 1), 1)
    cb = _bmm(cm, bm, "nt")
    cs = _bmm(cm, s, "nt")

    def per_head(vals, idx):
        out = vals[SR - 1]
        for r in reversed(range(SR - 1)):
            out = jnp.where(idx < (r + 1) * SP, vals[r], out)
        return out

    dt_cols, e_cols, lmats, dstates, declast = [], [], [], [], []
    for r in range(SR):
        dt_row = _softplus(dtraw[r] + dtb[r])
        adt_row = -jnp.exp(alog[r]) * dt_row
        dt_cols.append(_row_to_col(dt_row, eye))
        adt_col = _row_to_col(adt_row, eye)
        acs_col = jnp.sum(jnp.where(tril, adt_row, 0.0), axis=2, keepdims=True)
        acs_row = jnp.sum(jnp.where(triu, adt_col, 0.0), axis=1, keepdims=True)
        acs_last = jnp.sum(adt_row, axis=2, keepdims=True)
        lmats.append(jnp.exp(jnp.where(tril, acs_col - acs_row, NEG)))
        e_cols.append(jnp.exp(acs_col))
        dstates.append(jnp.exp(acs_last - acs_col))
        declast.append(jnp.exp(acs_last))
    xd = xs * per_head(dt_cols, lane)
    y = per_head([_bmm(cb * lmats[r], xd, "nn") for r in range(SR)], lane)
    states = per_head([_bmm(xd, bm * dstates[r], "tn") for r in range(SR)], prow)
    y = y + cs * per_head(e_cols, lane) + xs * per_head([dskip[r] for r in range(SR)], lane)
    s_new = s * per_head(declast, prow) + states
    return y, s_new


def _matmul(a, b, mode, name, out_dtype=F32, add=None, tm=1024, tn=1024, tk=2048):
    if mode == "nn":
        (m, k), n = a.shape, b.shape[1]
    elif mode == "nt":
        (m, k), n = a.shape, b.shape[0]
    else:
        (k, m), n = a.shape, b.shape[1]
    tm, tn, tk = min(tm, m), min(tn, n), min(tk, k)
    assert m % tm == 0 and n % tn == 0 and k % tk == 0, (name, m, n, k)
    nk = k // tk
    dims = {"nn": (((1,), (0,)), ((), ())), "nt": (((1,), (1,)), ((), ())),
            "tn": (((0,), (0,)), ((), ()))}[mode]
    a_spec = {"nn": pl.BlockSpec((tm, tk), lambda i, j, kk: (i, kk)),
              "nt": pl.BlockSpec((tm, tk), lambda i, j, kk: (i, kk)),
              "tn": pl.BlockSpec((tk, tm), lambda i, j, kk: (kk, i))}[mode]
    b_spec = {"nn": pl.BlockSpec((tk, tn), lambda i, j, kk: (kk, j)),
              "nt": pl.BlockSpec((tn, tk), lambda i, j, kk: (j, kk)),
              "tn": pl.BlockSpec((tk, tn), lambda i, j, kk: (kk, j))}[mode]
    o_spec = pl.BlockSpec((tm, tn), lambda i, j, kk: (i, j))
    has_add = add is not None

    def body(*refs):
        a_ref, b_ref = refs[:2]
        add_ref = refs[2] if has_add else None
        o_ref = refs[2 + has_add]
        part = lax.dot_general(a_ref[...].astype(BF16), b_ref[...].astype(BF16), dims,
                               preferred_element_type=F32)

        def finish(r):
            if has_add:
                r = r + add_ref[...].astype(F32)
            o_ref[...] = r.astype(o_ref.dtype)

        if nk == 1:
            finish(part)
            return
        acc_ref = refs[3 + has_add]
        kk = pl.program_id(2)

        @pl.when(kk == 0)
        def _():
            acc_ref[...] = part

        @pl.when(jnp.logical_and(kk > 0, kk < nk - 1))
        def _():
            acc_ref[...] += part

        @pl.when(kk == nk - 1)
        def _():
            finish(acc_ref[...] + part)

    ins = [a, b] + ([add] if has_add else [])
    in_specs = [a_spec, b_spec] + ([o_spec] if has_add else [])
    scratch = [] if nk == 1 else [pltpu.VMEM((tm, tn), F32)]
    return _pcall(body, grid=(m // tm, n // tn, nk), in_specs=in_specs, out_specs=o_spec,
                  out_shape=SDS((m, n), out_dtype), scratch_shapes=scratch,
                  compiler_params=_cp(("parallel", "parallel", "arbitrary")), name=name)(*ins)


def _ew(fn, ins, in_specs, out_shape, out_specs, grid, name):
    n_in = len(ins)

    def body(*refs):
        outs = fn(*[r[...] for r in refs[:n_in]])
        for r, o in zip(refs[n_in:], outs):
            r[...] = o.astype(r.dtype)

    return _pcall(body, grid=grid, in_specs=in_specs, out_specs=out_specs, out_shape=out_shape,
                  compiler_params=_cp(("arbitrary",) * len(grid)), name=name)(*ins)


def _ew_vjp(fn, ins, in_specs, cts, ct_specs, wrt, g_shape, g_specs, acc, grid, name):
    n_in, n_ct = len(ins), len(cts)

    def body(*refs):
        vals = [r[...].astype(F32) for r in refs[:n_in]]
        outs, vjp = jax.vjp(fn, *vals)
        g_all = vjp(tuple(r[...].astype(F32) for r in refs[n_in:n_in + n_ct]))
        for pos, (i, g_ref) in enumerate(zip(wrt, refs[n_in + n_ct:])):
            g = g_all[i]
            if pos in acc:
                first = functools.reduce(
                    jnp.logical_and, [pl.program_id(ax) == 0 for ax in range(acc[pos], len(grid))])

                @pl.when(first)
                def _():
                    g_ref[...] = jnp.zeros_like(g_ref)

                g_ref[...] += g.astype(g_ref.dtype)
            else:
                g_ref[...] = g.astype(g_ref.dtype)

    return _pcall(body, grid=grid, in_specs=list(in_specs) + list(ct_specs), out_specs=g_specs,
                  out_shape=g_shape, compiler_params=_cp(("arbitrary",) * len(grid)),
                  name=name)(*ins, *cts)


def _row_spec(tm, n):
    return pl.BlockSpec((tm, n), lambda i: (i, 0))


def _par_spec(n):
    return pl.BlockSpec((1, n), lambda i: (0, 0))


def _rms_fwd(x, w, name):
    t = x.shape[0]
    tm = min(t, 512)
    return _ew(lambda xv, wv: (_rms(xv, wv),), [x, w], [_row_spec(tm, D), _par_spec(D)],
               [SDS((t, D), BF16)], [_row_spec(tm, D)], (t // tm,), name)[0]


def _rms_bwd(x, w, dres, dhid, name):
    t = x.shape[0]
    tm = min(t, 512)
    return _ew_vjp(lambda xv, wv: (xv, _rms(xv, wv)), [x, w], [_row_spec(tm, D), _par_spec(D)],
                   [dres, dhid], [_row_spec(tm, D), _row_spec(tm, D)], (0, 1),
                   [SDS((t, D), F32), SDS((1, D), F32)], [_row_spec(tm, D), _par_spec(D)],
                   {1: 0}, (t // tm,), name)


def _final_loss(x, w, tgt, name):
    t = x.shape[0]
    tm = min(t, 512)

    def body(x_ref, w_ref, t_ref, dx_ref, dw_ref, l_ref):
        @pl.when(pl.program_id(0) == 0)
        def _():
            dw_ref[...] = jnp.zeros_like(dw_ref)
            l_ref[...] = jnp.zeros_like(l_ref)

        xv, wv = x_ref[...], w_ref[...]
        rstd = lax.rsqrt(jnp.mean(xv * xv, axis=-1, keepdims=True) + EPS)
        xh = xv * rstd
        err = xh * wv - t_ref[...]
        l_ref[...] += 0.5 * jnp.sum(jnp.mean(err * err, axis=-1, keepdims=True), axis=0, keepdims=True)
        dy = err * (1.0 / D)
        dw_ref[...] += jnp.sum(dy * xh, axis=0, keepdims=True)
        dxh = dy * wv
        dx_ref[...] = rstd * (dxh - xh * jnp.mean(dxh * xh, axis=-1, keepdims=True))

    return _pcall(body, grid=(t // tm,), in_specs=[_row_spec(tm, D), _par_spec(D), _row_spec(tm, D)],
                  out_specs=[_row_spec(tm, D), _par_spec(D), _par_spec(128)],
                  out_shape=[SDS((t, D), F32), SDS((1, D), F32), SDS((1, 128), F32)],
                  compiler_params=_cp(("arbitrary",)), name=name)(x, w, tgt)


def _conv_taps(ext, w_ref, tm, lo):
    n = ext.shape[0]
    acc = None
    for j in range(CONV_K):
        shift = (CONV_K - 1 - j) if lo else (n - (CONV_K - 1 - j)) % n
        rolled = pltpu.roll(ext, shift, 0) if shift else ext
        term = w_ref[pl.ds(j, 1), :] * rolled[lo:lo + tm]
        acc = term if acc is None else acc + term
    return acc


def _conv_pre_specs(tm, ct, col0):
    hb = tm // 8
    return [pl.BlockSpec((tm, ct), lambda j, i: (i, col0 + j)),
            pl.BlockSpec((8, ct), lambda j, i: (jnp.maximum(i * hb - 1, 0), col0 + j)),
            pl.BlockSpec((CONV_K, ct), lambda j, i: (0, j)),
            pl.BlockSpec((1, ct), lambda j, i: (0, j))]


def _conv_pre_value(x_ref, xh_ref, w_ref, b_ref, tm):
    halo = jnp.where(pl.program_id(1) > 0, xh_ref[...], 0.0)
    ext = jnp.concatenate([halo, x_ref[...]], axis=0)
    return _conv_taps(ext, w_ref, tm, 8) + b_ref[...], ext


def _conv_fwd(x, col0, w, b, ct, post, name):
    t = x.shape[0]
    tpb = FWD_COLS // ct
    wide = tpb * ct
    tm = min(t, BLOCK_ELEMS // wide)
    nt = w.shape[1] // ct
    assert col0 % wide == 0 and nt % tpb == 0, name

    def body(x_ref, xh_ref, w_ref, b_ref, o_ref):
        pre, _ = _conv_pre_value(x_ref, xh_ref, w_ref, b_ref, tm)
        for k in range(tpb):
            o_ref[k] = post(pre[:, k * ct:(k + 1) * ct])

    return _pcall(body, grid=(nt // tpb, t // tm), in_specs=_conv_pre_specs(tm, wide, col0 // wide),
                  out_specs=pl.BlockSpec((tpb, tm, ct), lambda j, i: (j, i, 0)),
                  out_shape=SDS((nt, t, ct), F32), compiler_params=_cp(("arbitrary", "arbitrary")),
                  name=name)(x, x, w, b)


def _conv_dpre(x, col0, w, b, ct, post, dout, name, c_total, c_off, into=None):
    t = x.shape[0]
    tm = min(t, BLOCK_ELEMS // ct)
    nt = w.shape[1] // ct
    chained = into is not None

    def body(*refs):
        x_ref, xh_ref, w_ref, b_ref, do_ref = refs[:5]
        dp_ref, dw_ref, db_ref = refs[5 + chained:]
        pre, ext = _conv_pre_value(x_ref, xh_ref, w_ref, b_ref, tm)
        _, vjp = jax.vjp(post, pre)
        dpre = vjp(do_ref[0])[0]
        dp_ref[...] = dpre

        @pl.when(pl.program_id(1) == 0)
        def _():
            dw_ref[...] = jnp.zeros_like(dw_ref)
            db_ref[...] = jnp.zeros_like(db_ref)

        for j in range(CONV_K):
            xs = (pltpu.roll(ext, CONV_K - 1 - j, 0) if j < CONV_K - 1 else ext)[8:8 + tm]
            dw_ref[pl.ds(j, 1), :] += jnp.sum(dpre * xs, axis=0, keepdims=True)
        db_ref[...] += jnp.sum(dpre, axis=0, keepdims=True)

    c = w.shape[1]
    o0 = c_off // ct
    return _pcall(body, grid=(nt, t // tm),
                  in_specs=_conv_pre_specs(tm, ct, col0 // ct)
                  + [pl.BlockSpec((1, tm, ct), lambda j, i: (j, i, 0))] + ([ANY] if chained else []),
                  out_specs=[pl.BlockSpec((tm, ct), lambda j, i: (i, o0 + j)),
                             pl.BlockSpec((CONV_K, ct), lambda j, i: (0, j)),
                             pl.BlockSpec((1, ct), lambda j, i: (0, j))],
                  out_shape=[SDS((t, c_total), F32), SDS((CONV_K, c), F32), SDS((1, c), F32)],
                  input_output_aliases={5: 0} if chained else {},
                  compiler_params=_cp(("arbitrary", "arbitrary")), name=name)(
                      x, x, w, b, dout, *([into] if chained else []))


def _conv_t(dpre, w, name):
    t, c = dpre.shape
    tm = min(t, 512)
    ct = min(c, 512)
    hb = tm // 8
    last = t // tm - 1

    def body(d_ref, dh_ref, w_ref, o_ref):
        halo = jnp.where(pl.program_id(1) < last, dh_ref[...], 0.0)
        ext = jnp.concatenate([d_ref[...], halo], axis=0)
        o_ref[...] = _conv_taps(ext, w_ref, tm, 0).astype(o_ref.dtype)

    return _pcall(body, grid=(c // ct, t // tm),
                  in_specs=[pl.BlockSpec((tm, ct), lambda j, i: (i, j)),
                            pl.BlockSpec((8, ct), lambda j, i: (jnp.minimum((i + 1) * hb, t // 8 - 1), j)),
                            pl.BlockSpec((CONV_K, ct), lambda j, i: (0, j))],
                  out_specs=pl.BlockSpec((tm, ct), lambda j, i: (i, j)),
                  out_shape=SDS((t, c), BF16), compiler_params=_cp(("arbitrary", "arbitrary")),
                  name=name)(dpre, dpre, w)


def _l2norm_scaled(scale):
    def post(pre):
        a = _silu(pre)
        return a * lax.rsqrt(jnp.sum(a * a, axis=-1, keepdims=True) + EPS) * scale
    return post


GDN_HB = 16
GDN_CPB = 2
SSD_GB = 8
SSD_CPB = 1


def _gdn_specs(nb, rev):
    hb, cpb, tc = GDN_HB, GDN_CPB, GDN_CPB * GCH
    blk = (lambda n: nb - 1 - n) if rev else (lambda n: n)
    seq = lambda h: pl.BlockSpec((h, tc, GDK), lambda g, n: (g, blk(n), 0))
    gate = pl.BlockSpec((1, hb, cpb, GCH), lambda g, n: (blk(n), g, 0, 0))
    par = pl.BlockSpec((hb, 1, 1), lambda g, n: (g, 0, 0))
    state = pl.BlockSpec((cpb, hb, GDK, GDK), lambda g, n: (blk(n), g, 0, 0))
    tinv = pl.BlockSpec((cpb, hb, GCH, GCH), lambda g, n: (blk(n), g, 0, 0))
    return seq, gate, par, state, tinv


def _gdn_fwd(q, k, v, braw, araw, alog, dtb):
    t = v.shape[1]
    hb, cpb = GDN_HB, GDN_CPB
    nb = t // (cpb * GCH)
    seq, gate, par, state, tinv = _gdn_specs(nb, False)

    def body(q_ref, k_ref, v_ref, b_ref, a_ref, al_ref, dt_ref, o_ref, sall_ref, tall_ref, s_ref):
        @pl.when(pl.program_id(1) == 0)
        def _():
            s_ref[...] = jnp.zeros_like(s_ref)

        s = s_ref[...]
        for c in range(cpb):
            rows = pl.ds(c * GCH, GCH)
            sall_ref[c] = s
            o, s, tmat = _gdn_chunk(q_ref[:, rows, :], k_ref[:, rows, :], v_ref[:, rows, :],
                                    b_ref[0, :, pl.ds(c, 1), :], a_ref[0, :, pl.ds(c, 1), :],
                                    al_ref[...], dt_ref[...], s, want_t=True)
            o_ref[:, rows, :] = o
            tall_ref[c] = tmat.astype(BF16)
        s_ref[...] = s

    return _pcall(body, grid=(GH // hb, nb),
                  in_specs=[seq(hb // 2), seq(hb // 2), seq(hb), gate, gate, par, par],
                  out_specs=[seq(hb), state, tinv],
                  out_shape=[SDS((GH, t, GDK), F32), SDS((t // GCH, GH, GDK, GDK), F32),
                             SDS((t // GCH, GH, GCH, GCH), BF16)],
                  scratch_shapes=[pltpu.VMEM((hb, GDK, GDK), F32)],
                  compiler_params=_cp(("arbitrary", "arbitrary")), name="gdn_chunk_fwd")(
                      q, k, v, braw, araw, alog, dtb)


def _gdn_bwd(q, k, v, braw, araw, alog, dtb, sall, tall, do):
    t = v.shape[1]
    hb, cpb = GDN_HB, GDN_CPB
    nb = t // (cpb * GCH)
    seq, gate, par, state, tinv = _gdn_specs(nb, True)

    def body(q_ref, k_ref, v_ref, b_ref, a_ref, al_ref, dt_ref, sall_ref, tall_ref, do_ref,
             dq_ref, dk_ref, dv_ref, db_ref, da_ref, dal_ref, ddt_ref, ds_ref):
        @pl.when(pl.program_id(1) == 0)
        def _():
            ds_ref[...] = jnp.zeros_like(ds_ref)
            dal_ref[...] = jnp.zeros_like(dal_ref)
            ddt_ref[...] = jnp.zeros_like(ddt_ref)

        ds = ds_ref[...]
        for c in reversed(range(cpb)):
            rows = pl.ds(c * GCH, GCH)
            fn = functools.partial(_gdn_chunk, t_known=tall_ref[c].astype(F32))
            _, vjp = jax.vjp(fn, q_ref[:, rows, :], k_ref[:, rows, :], v_ref[:, rows, :],
                             b_ref[0, :, pl.ds(c, 1), :], a_ref[0, :, pl.ds(c, 1), :],
                             al_ref[...], dt_ref[...], sall_ref[c])
            dq, dk, dv, db, da, dal, ddt, ds = vjp((do_ref[:, rows, :], ds))
            dq_ref[:, rows, :] = dq
            dk_ref[:, rows, :] = dk
            dv_ref[:, rows, :] = dv
            db_ref[0, :, pl.ds(c, 1), :] = db
            da_ref[0, :, pl.ds(c, 1), :] = da
            dal_ref[...] += dal
            ddt_ref[...] += ddt
        ds_ref[...] = ds

    return _pcall(body, grid=(GH // hb, nb),
                  in_specs=[seq(hb // 2), seq(hb // 2), seq(hb), gate, gate, par, par, state, tinv, seq(hb)],
                  out_specs=[seq(hb // 2), seq(hb // 2), seq(hb), gate, gate, par, par],
                  out_shape=[SDS(q.shape, F32), SDS(k.shape, F32), SDS(v.shape, F32),
                             SDS(braw.shape, F32), SDS(araw.shape, F32),
                             SDS((GH, 1, 1), F32), SDS((GH, 1, 1), F32)],
                  scratch_shapes=[pltpu.VMEM((hb, GDK, GDK), F32)],
                  compiler_params=_cp(("arbitrary", "arbitrary")), name="gdn_chunk_bwd")(
                      q, k, v, braw, araw, alog, dtb, sall, tall, do)


def _ssd_specs(nb, rev):
    gb, cpb, tc = SSD_GB, SSD_CPB, SSD_CPB * SCH
    blk = (lambda n: nb - 1 - n) if rev else (lambda n: n)
    seq = lambda w: pl.BlockSpec((gb, tc, w), lambda g, n: (g, blk(n), 0))
    gate = pl.BlockSpec((1, cpb * SR, gb, 1, SCH), lambda g, n: (blk(n), 0, g, 0, 0))
    par = pl.BlockSpec((SR, gb, 1, 1), lambda g, n: (0, g, 0, 0))
    state = pl.BlockSpec((cpb, gb, SR * SP, SN), lambda g, n: (blk(n), g, 0, 0))
    return seq, gate, par, state


def _ssd_fwd(xs, bm, cm, dtraw, dtb, alog, dskip):
    t = xs.shape[1]
    gb, cpb = SSD_GB, SSD_CPB
    nb = t // (cpb * SCH)
    seq, gate, par, state = _ssd_specs(nb, False)

    def body(x_ref, b_ref, c_ref, dt_ref, dtb_ref, al_ref, dk_ref, y_ref, sall_ref, s_ref):
        @pl.when(pl.program_id(1) == 0)
        def _():
            s_ref[...] = jnp.zeros_like(s_ref)

        s = s_ref[...]
        for c in range(cpb):
            rows = pl.ds(c * SCH, SCH)
            sall_ref[c] = s
            y, s = _ssd_chunk(x_ref[:, rows, :], b_ref[:, rows, :], c_ref[:, rows, :],
                              dt_ref[0, pl.ds(c * SR, SR)], dtb_ref[...], al_ref[...],
                              dk_ref[...], s)
            y_ref[:, rows, :] = y
        s_ref[...] = s

    return _pcall(body, grid=(SG // gb, nb),
                  in_specs=[seq(SR * SP), seq(SN), seq(SN), gate, par, par, par],
                  out_specs=[seq(SR * SP), state],
                  out_shape=[SDS((SG, t, SR * SP), F32), SDS((t // SCH, SG, SR * SP, SN), F32)],
                  scratch_shapes=[pltpu.VMEM((gb, SR * SP, SN), F32)],
                  compiler_params=_cp(("arbitrary", "arbitrary")), name="ssd_chunk_fwd")(
                      xs, bm, cm, dtraw, dtb, alog, dskip)


def _ssd_bwd(xs, bm, cm, dtraw, dtb, alog, dskip, sall, dy):
    t = xs.shape[1]
    gb, cpb = SSD_GB, SSD_CPB
    nb = t // (cpb * SCH)
    seq, gate, par, state = _ssd_specs(nb, True)

    def body(x_ref, b_ref, c_ref, dt_ref, dtb_ref, al_ref, dk_ref, sall_ref, dy_ref,
             dx_ref, dbm_ref, dcm_ref, ddt_ref, ddtb_ref, dal_ref, ddk_ref, ds_ref):
        @pl.when(pl.program_id(1) == 0)
        def _():
            ds_ref[...] = jnp.zeros_like(ds_ref)
            ddtb_ref[...] = jnp.zeros_like(ddtb_ref)
            dal_ref[...] = jnp.zeros_like(dal_ref)
            ddk_ref[...] = jnp.zeros_like(ddk_ref)

        ds = ds_ref[...]
        for c in reversed(range(cpb)):
            rows = pl.ds(c * SCH, SCH)
            _, vjp = jax.vjp(_ssd_chunk, x_ref[:, rows, :], b_ref[:, rows, :], c_ref[:, rows, :],
                             dt_ref[0, pl.ds(c * SR, SR)], dtb_ref[...], al_ref[...],
                             dk_ref[...], sall_ref[c])
            dx, dbm, dcm, ddt, ddtb, dal, ddk, ds = vjp((dy_ref[:, rows, :], ds))
            dx_ref[:, rows, :] = dx
            dbm_ref[:, rows, :] = dbm
            dcm_ref[:, rows, :] = dcm
            ddt_ref[0, pl.ds(c * SR, SR)] = ddt
            ddtb_ref[...] += ddtb
            dal_ref[...] += dal
            ddk_ref[...] += ddk
        ds_ref[...] = ds

    return _pcall(body, grid=(SG // gb, nb),
                  in_specs=[seq(SR * SP), seq(SN), seq(SN), gate, par, par, par, state, seq(SR * SP)],
                  out_specs=[seq(SR * SP), seq(SN), seq(SN), gate, par, par, par],
                  out_shape=[SDS(xs.shape, F32), SDS(bm.shape, F32), SDS(cm.shape, F32),
                             SDS(dtraw.shape, F32), SDS((SR, SG, 1, 1), F32), SDS((SR, SG, 1, 1), F32),
                             SDS((SR, SG, 1, 1), F32)],
                  scratch_shapes=[pltpu.VMEM((gb, SR * SP, SN), F32)],
                  compiler_params=_cp(("arbitrary", "arbitrary")), name="ssd_chunk_bwd")(
                      xs, bm, cm, dtraw, dtb, alog, dskip, sall, dy)


def _gate_specs(tm, ct, zcol0, per_tile_w):
    z0 = zcol0 // ct
    return [pl.BlockSpec((1, tm, ct), lambda i, j: (j, i, 0)),
            pl.BlockSpec((tm, ct), lambda i, j: (i, z0 + j)),
            pl.BlockSpec((1, ct), (lambda i, j: (0, j)) if per_tile_w else (lambda i, j: (0, 0)))]


def _gdn_gate(o, z, w):
    return (_rms(o[0], w) * _silu(z),)


def _ssd_gate(y, z, w):
    return (_rms(y[0] * _silu(z), w),)


def _gate_fwd(fn, o, proj, zcol0, w, name):
    nt, t, ct = o.shape
    tpb = FWD_COLS // ct
    wide = tpb * ct
    tm = min(t, BLOCK_ELEMS // wide)
    per_tile_w = w.shape[1] > ct
    assert zcol0 % wide == 0 and nt % tpb == 0, name
    z0 = zcol0 // wide
    specs = [pl.BlockSpec((tpb, tm, ct), lambda i, j: (j, i, 0)),
             pl.BlockSpec((tm, wide), lambda i, j: (i, z0 + j)),
             pl.BlockSpec((1, wide), lambda i, j: (0, j)) if per_tile_w
             else pl.BlockSpec((1, ct), lambda i, j: (0, 0))]

    def fn_wide(ov, zv, wv):
        outs = []
        for k in range(tpb):
            cols = slice(k * ct, (k + 1) * ct)
            outs.append(fn(ov[k:k + 1], zv[:, cols], wv[:, cols] if per_tile_w else wv)[0])
        return (jnp.concatenate(outs, axis=1),)

    return _ew(fn_wide, [o, proj, w], specs, [SDS((t, nt * ct), BF16)],
               [pl.BlockSpec((tm, wide), lambda i, j: (i, j))], (t // tm, nt // tpb), name)[0]


def _gate_bwd(fn, o, proj, zcol0, w, dy, wacc, name):
    nt, t, ct = o.shape
    tm = min(t, BLOCK_ELEMS // ct)
    specs = _gate_specs(tm, ct, zcol0, wacc)
    out_spec = pl.BlockSpec((tm, ct), lambda i, j: (i, j))
    if wacc:
        flip = lambda s: pl.BlockSpec(s.block_shape, lambda j, i, f=s.index_map: f(i, j))
        specs = [flip(s) for s in specs]
        out_spec = flip(out_spec)
        grid, acc = (nt, t // tm), {2: 1}
    else:
        grid, acc = (t // tm, nt), {2: 0}
    return _ew_vjp(fn, [o, proj, w], specs, [dy], [out_spec], (0, 1, 2),
                   [SDS(o.shape, F32), SDS((t, nt * ct), BF16), SDS(w.shape, F32)],
                   [specs[0], out_spec, specs[2]], acc, grid, name)


def _adamw_math(w, g, m, v):
    m = ADAM_B1 * m + (1.0 - ADAM_B1) * g
    v = ADAM_B2 * v + (1.0 - ADAM_B2) * jnp.square(g)
    m_hat = m / (1.0 - ADAM_B1 ** ADAM_STEP)
    v_hat = v / (1.0 - ADAM_B2 ** ADAM_STEP)
    delta = -ADAM_LR * (m_hat / (jnp.sqrt(v_hat) + ADAM_EPS) + ADAM_WD * w)
    return delta, m, v


def _adamw(w, g, m, v, name):
    shape = w.shape
    w2, g2, m2, v2 = [a.reshape(-1, shape[-1]) for a in (w, g, m, v)]
    r, c = w2.shape
    tr = 256 if r % 256 == 0 else r
    spec = pl.BlockSpec((tr, c), lambda i: (i, 0))
    outs = _ew(_adamw_math, [w2, g2, m2, v2], [spec] * 4, [SDS((r, c), F32)] * 3, [spec] * 3,
               (r // tr,), name)
    return [o.reshape(shape) for o in outs]


def _coords():
    return lax.axis_index("x"), lax.axis_index("y"), lax.axis_index("c")


def _other_chips(x, y):
    return [(1 - x, y), (x, 1 - y), (1 - x, 1 - y)]


ANY = pl.BlockSpec(memory_space=pl.ANY)


def _rcopy(src, dst, send_sems, recv_sems, k, to):
    return pltpu.make_async_remote_copy(src_ref=src, dst_ref=dst, send_sem=send_sems.at[k],
                                        recv_sem=recv_sems.at[k], device_id=to, device_id_type=MESH)


def _gather_weights(bigs, small):
    n = len(bigs)

    def body(*refs):
        in_refs, small_ref = refs[:n], refs[n]
        out_refs, osmall_ref = refs[n + 1:2 * n + 1], refs[2 * n + 1]
        send_sems, recv_sems = refs[2 * n + 2:]
        x, y, c = _coords()
        me = 2 * x + y
        sibling = (x, y, 1 - c)
        chips = _other_chips(x, y)
        halves = [b.shape[0] // 2 for b in bigs]
        mine = [pl.ds(c * h, h) for h in halves]
        theirs = [pl.ds((1 - c) * h, h) for h in halves]
        rc = functools.partial(_rcopy, send_sems=send_sems, recv_sems=recv_sems)

        first = []
        for a in range(n):
            for j, (px, py) in enumerate(chips):
                first.append(rc(in_refs[a].at[mine[a]], out_refs[a].at[me, mine[a]], k=3 * a + j, to=(px, py, c)))
        for j, (px, py) in enumerate(chips):
            first.append(rc(small_ref, osmall_ref.at[me], k=6 * n + j, to=(px, py, c)))
        for cp in first:
            cp.start()
        passed = []
        for a in range(n):
            for j, (px, py) in enumerate(chips):
                landed = out_refs[a].at[2 * px + py, mine[a]]
                rc(landed, landed, k=3 * a + j, to=(px, py, c)).wait_recv()
                fw = rc(landed, landed, k=3 * n + 3 * a + j, to=sibling)
                fw.start()
                passed.append(fw)
        for a in range(n):
            for j, (px, py) in enumerate(chips):
                landed = out_refs[a].at[2 * px + py, theirs[a]]
                rc(landed, landed, k=3 * n + 3 * a + j, to=sibling).wait_recv()
        for j, (px, py) in enumerate(chips):
            rc(small_ref, osmall_ref.at[2 * px + py], k=6 * n + j, to=(px, py, c)).wait_recv()
        for cp in first + passed:
            cp.wait_send()

    outs = _pcall(body, in_specs=[ANY] * (n + 1), out_specs=[ANY] * (n + 1),
                  out_shape=[SDS((N_CHIPS,) + b.shape, b.dtype) for b in bigs]
                  + [SDS((N_CHIPS,) + small.shape, small.dtype)],
                  scratch_shapes=[pltpu.SemaphoreType.DMA((6 * n + 3,)), pltpu.SemaphoreType.DMA((6 * n + 3,))],
                  name="gather_weights")(*bigs, small)
    me = 2 * lax.axis_index("x") + lax.axis_index("y")
    return [lax.dynamic_update_index_in_dim(o, own, me, 0) for o, own in zip(outs, list(bigs) + [small])]


def _swap_halves(gs):
    n = len(gs)

    def body(*refs):
        send_sems, recv_sems = refs[2 * n:]
        x, y, c = _coords()
        cps = [_rcopy(refs[a].at[:, 1 - c], refs[n + a], send_sems, recv_sems, a, (x, y, 1 - c))
               for a in range(n)]
        for cp in cps:
            cp.start()
        for cp in cps:
            cp.wait()

    return _pcall(body, in_specs=[ANY] * n, out_specs=[ANY] * n,
                  out_shape=[SDS((N_CHIPS,) + g.shape[2:], g.dtype) for g in gs],
                  scratch_shapes=[pltpu.SemaphoreType.DMA((n,)), pltpu.SemaphoreType.DMA((n,))],
                  name="swap_halves")(*gs)


def _sum_cores(gs, rs, half_idx):
    n = len(gs)
    ns = 2
    in_specs, out_specs, out_shape = [], [], []
    for g in gs:
        _, _, h, w = g.shape
        in_specs.append(pl.BlockSpec((1, 1, h // ns, w), lambda b, i, c_ref: (b, c_ref[0], i, 0)))
    for g in gs:
        _, _, h, w = g.shape
        spec = pl.BlockSpec((1, h // ns, w), lambda b, i, c_ref: (b, i, 0))
        in_specs.append(spec)
        out_specs += [spec, spec]
        out_shape += [SDS((N_CHIPS, h, w), F32), SDS((N_CHIPS, h, w), BF16)]

    def body(c_ref, *refs):
        del c_ref
        for a in range(n):
            tot = refs[a][0] + refs[n + a][...]
            refs[2 * n + 2 * a][...] = tot
            refs[2 * n + 2 * a + 1][...] = tot.astype(BF16)

    outs = _pcall(body, grid_spec=pltpu.PrefetchScalarGridSpec(
        num_scalar_prefetch=1, grid=(N_CHIPS, ns), in_specs=in_specs, out_specs=out_specs),
        out_shape=out_shape, compiler_params=_cp(("arbitrary", "arbitrary")), name="sum_cores")(
            half_idx, *gs, *rs)
    return outs[0::2], outs[1::2]


def _scatter_chips(hs):
    n = len(hs)

    def body(*refs):
        send_sems, recv_sems = refs[2 * n:]
        x, y, c = _coords()
        cps = []
        for a in range(n):
            for j, (px, py) in enumerate(_other_chips(x, y)):
                cps.append(_rcopy(refs[a].at[2 * px + py], refs[n + a].at[j], send_sems, recv_sems,
                                  3 * a + j, (px, py, c)))
        for cp in cps:
            cp.start()
        for cp in cps:
            cp.wait()

    return _pcall(body, in_specs=[ANY] * n, out_specs=[ANY] * n,
                  out_shape=[SDS((3,) + h.shape[1:], h.dtype) for h in hs],
                  scratch_shapes=[pltpu.SemaphoreType.DMA((3 * n,)), pltpu.SemaphoreType.DMA((3 * n,))],
                  name="scatter_chips")(*hs)


def _sum_chips(hs, xs, chip_idx):
    n = len(hs)
    ns = 2
    in_specs, out_specs, out_shape = [], [], []
    for h_arr in hs:
        _, h, w = h_arr.shape
        in_specs.append(pl.BlockSpec((1, h // ns, w), lambda i, c_ref: (c_ref[0], i, 0)))
    for h_arr in hs:
        _, h, w = h_arr.shape
        in_specs.append(pl.BlockSpec((3, h // ns, w), lambda i, c_ref: (0, i, 0)))
        out_specs.append(pl.BlockSpec((h // ns, w), lambda i, c_ref: (i, 0)))
        out_shape.append(SDS((h, w), F32))

    def body(c_ref, *refs):
        del c_ref
        for a in range(n):
            x_ref = refs[n + a]
            refs[2 * n + a][...] = (refs[a][0] + x_ref[0].astype(F32) + x_ref[1].astype(F32)
                                    + x_ref[2].astype(F32))

    return _pcall(body, grid_spec=pltpu.PrefetchScalarGridSpec(
        num_scalar_prefetch=1, grid=(ns,), in_specs=in_specs, out_specs=out_specs),
        out_shape=out_shape, compiler_params=_cp(("arbitrary",)), name="sum_chips")(chip_idx, *hs, *xs)


def _swap_totals(tots):
    n = len(tots)

    def body(*refs):
        send_sems, recv_sems = refs[2 * n:]
        x, y, c = _coords()
        cps = [_rcopy(refs[a], refs[n + a], send_sems, recv_sems, a, (x, y, 1 - c)) for a in range(n)]
        for cp in cps:
            cp.start()
        for cp in cps:
            cp.wait()

    return _pcall(body, in_specs=[ANY] * n, out_specs=[ANY] * n,
                  out_shape=[SDS(t.shape, t.dtype) for t in tots],
                  scratch_shapes=[pltpu.SemaphoreType.DMA((n,)), pltpu.SemaphoreType.DMA((n,))],
                  name="swap_totals")(*tots)


def _allreduce_small(buf):
    rows = buf.shape[0]

    def body(b_ref, o_ref, g_ref, send_sems, recv_sems):
        x, y, c = _coords()
        me = 4 * x + 2 * y + c
        g_ref[me] = b_ref[...]
        cps = []
        for k in range(1, 8):
            px = 1 - x if k & 4 else x
            py = 1 - y if k & 2 else y
            pc = 1 - c if k & 1 else c
            cps.append(pltpu.make_async_remote_copy(
                src_ref=b_ref, dst_ref=g_ref.at[me], send_sem=send_sems.at[k - 1],
                recv_sem=recv_sems.at[k - 1], device_id=(px, py, pc), device_id_type=MESH))
        for cp in cps:
            cp.start()
        for cp in cps:
            cp.wait()
        acc = g_ref[0]
        for d in range(1, 8):
            acc = acc + g_ref[d]
        o_ref[...] = acc

    vm = pl.BlockSpec(memory_space=pltpu.VMEM)
    return _pcall(body, in_specs=[vm], out_specs=vm, out_shape=SDS(buf.shape, F32),
                  scratch_shapes=[pltpu.VMEM((8, rows, 128), F32), pltpu.SemaphoreType.DMA((7,)),
                                  pltpu.SemaphoreType.DMA((7,))],
                  compiler_params=pltpu.CompilerParams(vmem_limit_bytes=VMEM_LIMIT),
                  name="allreduce_small")(buf)


def _flat128(a):
    return a.reshape(-1, 128)


def _pad_rows(a, mult):
    r = (-a.shape[0]) % mult
    return a if r == 0 else jnp.concatenate([a, jnp.zeros((r, a.shape[1]), a.dtype)], axis=0)


def _pack_flat(parts, mult):
    rows, offs, r0 = [], [], 0
    for p in parts:
        f = p.reshape(-1)
        pad = (-f.shape[0]) % 128
        if pad:
            f = jnp.concatenate([f, jnp.zeros((pad,), f.dtype)])
        f = f.reshape(-1, 128)
        rows.append(f)
        offs.append((r0, p.shape))
        r0 += f.shape[0]
    return _pad_rows(jnp.concatenate(rows, axis=0), mult), offs


def _unpack_flat(buf, offs):
    out = []
    for r0, shape in offs:
        n = 1
        for s in shape:
            n *= s
        nr = -(-n // 128)
        out.append(buf[r0:r0 + nr].reshape(-1)[:n].reshape(shape))
    return out


def _gates_to_rows(a, heads, chunk, cpb):
    t = a.shape[0]
    return a.reshape(t // (chunk * cpb), cpb, chunk, heads).transpose(0, 3, 1, 2)


def _rows_to_gates(a):
    nb, heads, cpb, chunk = a.shape
    return a.transpose(0, 2, 3, 1).reshape(nb * cpb * chunk, heads)


def kernel(x, norm_w, gdn_w_in, gdn_conv_w, gdn_a_log, gdn_dt_bias, gdn_norm_w, gdn_w_out, ssd_w_in, ssd_conv_w, ssd_conv_b, ssd_dt_bias, ssd_a_log, ssd_d, ssd_norm_w, ssd_w_out, final_norm_w, loss_target, m_norm_w, m_gdn_w_in, m_gdn_conv_w, m_gdn_a_log, m_gdn_dt_bias, m_gdn_norm_w, m_gdn_w_out, m_ssd_w_in, m_ssd_conv_w, m_ssd_conv_b, m_ssd_dt_bias, m_ssd_a_log, m_ssd_d, m_ssd_norm_w, m_ssd_w_out, m_final_norm_w, v_norm_w, v_gdn_w_in, v_gdn_conv_w, v_gdn_a_log, v_gdn_dt_bias, v_gdn_norm_w, v_gdn_w_out, v_ssd_w_in, v_ssd_conv_w, v_ssd_conv_b, v_ssd_dt_bias, v_ssd_a_log, v_ssd_d, v_ssd_norm_w, v_ssd_w_out, v_final_norm_w):
    ws = dict(norm_w=norm_w, gdn_w_in=gdn_w_in, gdn_conv_w=gdn_conv_w, gdn_a_log=gdn_a_log,
              gdn_dt_bias=gdn_dt_bias, gdn_norm_w=gdn_norm_w, gdn_w_out=gdn_w_out, ssd_w_in=ssd_w_in,
              ssd_conv_w=ssd_conv_w, ssd_conv_b=ssd_conv_b, ssd_dt_bias=ssd_dt_bias,
              ssd_a_log=ssd_a_log, ssd_d=ssd_d, ssd_norm_w=ssd_norm_w, ssd_w_out=ssd_w_out,
              final_norm_w=final_norm_w)
    ms = dict(norm_w=m_norm_w, gdn_w_in=m_gdn_w_in, gdn_conv_w=m_gdn_conv_w, gdn_a_log=m_gdn_a_log,
              gdn_dt_bias=m_gdn_dt_bias, gdn_norm_w=m_gdn_norm_w, gdn_w_out=m_gdn_w_out,
              ssd_w_in=m_ssd_w_in, ssd_conv_w=m_ssd_conv_w, ssd_conv_b=m_ssd_conv_b,
              ssd_dt_bias=m_ssd_dt_bias, ssd_a_log=m_ssd_a_log, ssd_d=m_ssd_d,
              ssd_norm_w=m_ssd_norm_w, ssd_w_out=m_ssd_w_out, final_norm_w=m_final_norm_w)
    vs = dict(norm_w=v_norm_w, gdn_w_in=v_gdn_w_in, gdn_conv_w=v_gdn_conv_w, gdn_a_log=v_gdn_a_log,
              gdn_dt_bias=v_gdn_dt_bias, gdn_norm_w=v_gdn_norm_w, gdn_w_out=v_gdn_w_out,
              ssd_w_in=v_ssd_w_in, ssd_conv_w=v_ssd_conv_w, ssd_conv_b=v_ssd_conv_b,
              ssd_dt_bias=v_ssd_dt_bias, ssd_a_log=v_ssd_a_log, ssd_d=v_ssd_d,
              ssd_norm_w=v_ssd_norm_w, ssd_w_out=v_ssd_w_out, final_norm_w=v_final_norm_w)
    names = list(ws)
    cx, cy, cc = _coords()
    chip = 2 * cx + cy
    t = x.shape[1]
    x0 = x.reshape(t, D)
    tgt = loss_target.reshape(t, D)

    bigs = [p[0].astype(BF16) for p in (gdn_w_in, gdn_w_out, ssd_w_in, ssd_w_out)]
    small, small_offs = _pack_flat([gdn_conv_w[0], ssd_conv_w[0], ssd_conv_b[0], ssd_norm_w[0]], 8)
    a_gi, a_go, a_si, a_so, gsmall = _gather_weights(bigs, small)
    w_gi = jnp.concatenate([a_gi[b] for b in range(4)], axis=1)
    w_si = jnp.concatenate([a_si[b] for b in range(4)], axis=1)
    w_go = a_go.reshape(4 * OUT_SHARD, D)
    w_so = a_so.reshape(4 * OUT_SHARD, D)
    sm = [_unpack_flat(gsmall[b], small_offs) for b in range(4)]
    g_cw = jnp.concatenate([sm[b][0] for b in range(4)], axis=1)
    s_cw = jnp.concatenate([sm[b][1] for b in range(4)], axis=1)
    s_cb = jnp.concatenate([sm[b][2] for b in range(4)], axis=0)[None]
    s_nw = jnp.concatenate([sm[b][3] for b in range(4)], axis=0)[None]

    def pad_small(w):
        return jnp.concatenate([w, jnp.zeros((D, SMALL_W - w.shape[1]), w.dtype)], axis=1)

    wg_main, wg_small = w_gi[:, :G_MAIN], pad_small(w_gi[:, G_MAIN:])
    ws_main, ws_small = w_si[:, :S_MAIN], pad_small(w_si[:, S_MAIN:])
    zero_b = jnp.zeros((1, G_CONV), F32)
    nw0, nw1 = norm_w[0:1], norm_w[1:2]
    fw = final_norm_w[None]
    g_alog = gdn_a_log.reshape(GH, 1, 1)
    g_dtb = gdn_dt_bias.reshape(GH, 1, 1)
    g_nw = gdn_norm_w.reshape(1, GDK)
    s_dtb = ssd_dt_bias.reshape(SG, SR).T.reshape(SR, SG, 1, 1)
    s_alog = ssd_a_log.reshape(SG, SR).T.reshape(SR, SG, 1, 1)
    s_d = ssd_d.reshape(SG, SR).T.reshape(SR, SG, 1, 1)

    hid0 = _rms_fwd(x0, nw0, "rms0")
    pg = _matmul(hid0, wg_main, "nn", "gdn_in_proj", tn=1536)
    pg_small = _matmul(hid0, wg_small, "nn", "gdn_in_proj_small", tn=SMALL_W)
    post_q = _l2norm_scaled(GDK ** -0.5)
    post_k = _l2norm_scaled(1.0)
    q = _conv_fwd(pg, 0, g_cw[:, :G_QK], zero_b[:, :G_QK], GDK, post_q, "gdn_conv_q")
    k = _conv_fwd(pg, G_QK, g_cw[:, G_QK:2 * G_QK], zero_b[:, :G_QK], GDK, post_k, "gdn_conv_k")
    v = _conv_fwd(pg, 2 * G_QK, g_cw[:, 2 * G_QK:], zero_b[:, :G_V], GDK, _silu, "gdn_conv_v")
    braw = _gates_to_rows(pg_small[:, :GH], GH, GCH, GDN_CPB)
    araw = _gates_to_rows(pg_small[:, GH:2 * GH], GH, GCH, GDN_CPB)
    o, g_sall, g_tall = _gdn_fwd(q, k, v, braw, araw, g_alog, g_dtb)
    y0 = _gate_fwd(_gdn_gate, o, pg, G_CONV, g_nw, "gdn_gate")
    x1 = _matmul(y0, w_go, "nn", "gdn_out_proj", add=x0)

    hid1 = _rms_fwd(x1, nw1, "rms1")
    ps = _matmul(hid1, ws_main, "nn", "ssd_in_proj", tn=1536)
    ps_small = _matmul(hid1, ws_small, "nn", "ssd_in_proj_small", tn=SMALL_W)
    c_x, c_b, c_c = S_INNER, 2 * S_INNER, 2 * S_INNER + SG * SN
    post_s = _silu
    xs = _conv_fwd(ps, c_x, s_cw[:, :S_INNER], s_cb[:, :S_INNER], SR * SP, post_s, "ssd_conv_x")
    bm = _conv_fwd(ps, c_b, s_cw[:, S_INNER:S_INNER + SG * SN], s_cb[:, S_INNER:S_INNER + SG * SN], SN,
                   post_s, "ssd_conv_b")
    cm = _conv_fwd(ps, c_c, s_cw[:, S_INNER + SG * SN:], s_cb[:, S_INNER + SG * SN:], SN, post_s,
                   "ssd_conv_c")
    nbs = t // (SCH * SSD_CPB)
    dtraw = _gates_to_rows(ps_small[:, :SH], SH, SCH, SSD_CPB)
    dtraw = dtraw.reshape(nbs, SG, SR, SSD_CPB, SCH).transpose(0, 3, 2, 1, 4).reshape(nbs, SSD_CPB * SR, SG, 1, SCH)
    yss, s_sall = _ssd_fwd(xs, bm, cm, dtraw, s_dtb, s_alog, s_d)
    y1 = _gate_fwd(_ssd_gate, yss, ps, 0, s_nw, "ssd_gate")
    x2 = _matmul(y1, w_so, "nn", "ssd_out_proj", add=x1)

    dx2, d_fw, loss_row = _final_loss(x2, fw, tgt, "final_loss")

    dy1 = _matmul(dx2, w_so, "nt", "ssd_out_dx", out_dtype=BF16, tn=2048)
    d_wso = _matmul(y1, dx2, "tn", "ssd_out_dw")
    dyss, dz_s, d_snw = _gate_bwd(_ssd_gate, yss, ps, 0, s_nw, dy1, True, "ssd_gate_bwd")
    dxs, dbm, dcm, ddtraw, d_sdtb, d_salog, d_sd = _ssd_bwd(xs, bm, cm, dtraw, s_dtb, s_alog, s_d, s_sall, dyss)
    dps, dwx, dbx = _conv_dpre(ps, c_x, s_cw[:, :S_INNER], s_cb[:, :S_INNER], SR * SP, post_s, dxs, "ssd_dpre_x",
                               G_CONV, 0)
    dps, dwb, dbb = _conv_dpre(ps, c_b, s_cw[:, S_INNER:S_INNER + SG * SN], s_cb[:, S_INNER:S_INNER + SG * SN],
                               SN, post_s, dbm, "ssd_dpre_b", G_CONV, S_INNER, dps)
    dps, dwc, dbc = _conv_dpre(ps, c_c, s_cw[:, S_INNER + SG * SN:], s_cb[:, S_INNER + SG * SN:], SN, post_s,
                               dcm, "ssd_dpre_c", G_CONV, S_INNER + SG * SN, dps)
    d_scw = jnp.concatenate([dwx, dwb, dwc], axis=1)
    d_scb = jnp.concatenate([dbx, dbb, dbc], axis=1)
    dxbc = _conv_t(dps, s_cw, "ssd_conv_t")
    ddt = ddtraw.reshape(nbs, SSD_CPB, SR, SG, SCH).transpose(0, 3, 2, 1, 4).reshape(nbs, SH, SSD_CPB, SCH)
    ddt = _rows_to_gates(ddt)
    dsm_s = jnp.concatenate([ddt, jnp.zeros((t, SMALL_W - SH), F32)], axis=1).astype(BF16)
    dhid1 = _matmul(dz_s, ws_main[:, :S_INNER], "nt", "ssd_in_dx_z")
    dhid1 = _matmul(dxbc, ws_main[:, S_INNER:], "nt", "ssd_in_dx_xbc", add=dhid1)
    dhid1 = _matmul(dsm_s, ws_small, "nt", "ssd_in_dx_dt", add=dhid1, tk=SMALL_W)
    d_wsi = jnp.concatenate([_matmul(hid1, dz_s, "tn", "ssd_in_dw_z"),
                             _matmul(hid1, dxbc, "tn", "ssd_in_dw_xbc"),
                             _matmul(hid1, dsm_s, "tn", "ssd_in_dw_dt", tn=SMALL_W)[:, :SH]], axis=1)
    dx1, d_nw1 = _rms_bwd(x1, nw1, dx2, dhid1, "rms1_bwd")

    dy0 = _matmul(dx1, w_go, "nt", "gdn_out_dx", out_dtype=BF16, tn=2048)
    d_wgo = _matmul(y0, dx1, "tn", "gdn_out_dw")
    do, dz_g, d_gnw = _gate_bwd(_gdn_gate, o, pg, G_CONV, g_nw, dy0, False, "gdn_gate_bwd")
    dq, dk, dv, dbraw, daraw, d_galog, d_gdtb = _gdn_bwd(q, k, v, braw, araw, g_alog, g_dtb, g_sall, g_tall, do)
    dpg, dwq, _ = _conv_dpre(pg, 0, g_cw[:, :G_QK], zero_b[:, :G_QK], GDK, post_q, dq, "gdn_dpre_q", G_CONV, 0)
    dpg, dwk, _ = _conv_dpre(pg, G_QK, g_cw[:, G_QK:2 * G_QK], zero_b[:, :G_QK], GDK, post_k, dk, "gdn_dpre_k",
                             G_CONV, G_QK, dpg)
    dpg, dwv, _ = _conv_dpre(pg, 2 * G_QK, g_cw[:, 2 * G_QK:], zero_b[:, :G_V], GDK, _silu, dv, "gdn_dpre_v",
                             G_CONV, 2 * G_QK, dpg)
    d_gcw = jnp.concatenate([dwq, dwk, dwv], axis=1)
    dqkv = _conv_t(dpg, g_cw, "gdn_conv_t")
    dsm_g = jnp.concatenate([_rows_to_gates(dbraw), _rows_to_gates(daraw),
                             jnp.zeros((t, SMALL_W - 2 * GH), F32)], axis=1).astype(BF16)
    dhid0 = _matmul(dqkv, wg_main[:, :G_CONV], "nt", "gdn_in_dx_qkv")
    dhid0 = _matmul(dz_g, wg_main[:, G_CONV:], "nt", "gdn_in_dx_z", add=dhid0)
    dhid0 = _matmul(dsm_g, wg_small, "nt", "gdn_in_dx_ba", add=dhid0, tk=SMALL_W)
    d_wgi = jnp.concatenate([_matmul(hid0, dqkv, "tn", "gdn_in_dw_qkv"),
                             _matmul(hid0, dz_g, "tn", "gdn_in_dw_z"),
                             _matmul(hid0, dsm_g, "tn", "gdn_in_dw_ba", tn=SMALL_W)[:, :2 * GH]], axis=1)
    dx0, d_nw0 = _rms_bwd(x0, nw0, dx1, dhid0, "rms0_bwd")

    def in_blocks(dw):
        return dw.reshape(D, N_CHIPS, IN_SHARD).transpose(1, 0, 2).reshape(N_CHIPS, 2, D // 2, IN_SHARD)

    def out_blocks(dw):
        return dw.reshape(N_CHIPS, 2, OUT_SHARD // 2, D)

    gs = [in_blocks(d_wgi), out_blocks(d_wgo), in_blocks(d_wsi), out_blocks(d_wso)]
    from_sib = _swap_halves(gs)
    hsum, hsum_bf = _sum_cores(gs, from_sib, cc.astype(jnp.int32).reshape(1))
    recv = _scatter_chips(hsum_bf)
    tots = _sum_chips(hsum, recv, chip.astype(jnp.int32).reshape(1))
    sib_tots = _swap_totals(tots)
    full = [jnp.concatenate([jnp.where(cc == 0, mine, sib), jnp.where(cc == 0, sib, mine)], axis=0)
            for mine, sib in zip(tots, sib_tots)]
    grads = dict(
        gdn_w_in=full[0].reshape(1, D, IN_SHARD), gdn_w_out=full[1].reshape(1, OUT_SHARD, D),
        ssd_w_in=full[2].reshape(1, D, IN_SHARD), ssd_w_out=full[3].reshape(1, OUT_SHARD, D))

    small_parts = [loss_row, jnp.concatenate([d_nw0, d_nw1], axis=0), d_gcw, d_galog, d_gdtb, d_gnw, d_scw, d_scb,
                   d_sdtb, d_salog, d_sd, d_snw, d_fw]
    sbuf, soffs = _pack_flat(small_parts, 8)
    ssum = _unpack_flat(_allreduce_small(sbuf), soffs)
    (loss_s, g_nw_all, g_gcw, g_galog, g_gdtb, g_gnw, g_scw, g_scb, g_sdtb, g_salog, g_sd, g_snw, g_fw) = ssum

    def my_cols(a, width):
        return lax.dynamic_slice_in_dim(a, chip * width, width, axis=a.ndim - 1)

    grads.update(
        norm_w=g_nw_all, gdn_conv_w=my_cols(g_gcw, 1024)[None], gdn_a_log=g_galog.reshape(1, GH),
        gdn_dt_bias=g_gdtb.reshape(1, GH), gdn_norm_w=g_gnw.reshape(1, GDK),
        ssd_conv_w=my_cols(g_scw, 1024)[None], ssd_conv_b=my_cols(g_scb, 1024),
        ssd_dt_bias=g_sdtb.reshape(SR, SG).T.reshape(1, SH), ssd_a_log=g_salog.reshape(SR, SG).T.reshape(1, SH),
        ssd_d=g_sd.reshape(SR, SG).T.reshape(1, SH),
        ssd_norm_w=my_cols(g_snw, 512), final_norm_w=g_fw.reshape(D))
    loss = loss_s[0, 0]

    big_names = ("gdn_w_in", "gdn_w_out", "ssd_w_in", "ssd_w_out")
    deltas, new_m, new_v = {}, {}, {}
    for n in big_names:
        deltas[n], new_m[n], new_v[n] = _adamw(ws[n], grads[n], ms[n], vs[n], "adamw_" + n)
    rest = [n for n in names if n not in big_names]
    packs = [_pack_flat([d[n] for n in rest], 8) for d in (ws, grads, ms, vs)]
    outs = _adamw(*[p[0] for p in packs], "adamw_small")
    for d, buf in zip((deltas, new_m, new_v), outs):
        for n, a in zip(rest, _unpack_flat(buf, packs[0][1])):
            d[n] = a

    grad_x = dx0.reshape(1, t, D)
    return (loss, grad_x, *[grads[n] for n in names], *[deltas[n] for n in names],
            *[new_m[n] for n in names], *[new_v[n] for n in names])
```

```python
import functools

import jax
import jax.numpy as jnp
from jax import lax
from jax.experimental import pallas as pl
from jax.experimental.pallas import tpu as pltpu

F32 = jnp.float32
BF16 = jnp.bfloat16
SDS = jax.ShapeDtypeStruct
MESH = pl.DeviceIdType.MESH

D = 1024
EPS = 1e-6
CONV_K = 4
N_CHIPS = 4
GH = 16
GHQ = 8
GDK = 128
GCH = 64
G_QK = 1024
G_V = 2048
G_CONV = 4096
G_MAIN = 6144
G_IN = 6176
SH = 32
SP = 64
SN = 128
SG = 8
SR = 4
SCH = 128
S_INNER = 2048
S_MAIN = 6144
S_IN = 6176
IN_SHARD = 1544
OUT_SHARD = 512
SMALL_W = 128

ADAM_LR = 0.001
ADAM_B1 = 0.9
ADAM_B2 = 0.999
ADAM_EPS = 1e-08
ADAM_WD = 0.01
ADAM_STEP = 10

VMEM_LIMIT = 48 * 1024 * 1024
BLOCK_ELEMS = 512 * 1024
FWD_COLS = 512
NEG = -1e30


def _pcall(body, **kw):
    return pl.pallas_call(body, **kw)


def _cp(sem=None, vmem=VMEM_LIMIT):
    return pltpu.CompilerParams(dimension_semantics=sem, vmem_limit_bytes=vmem)


@jax.custom_jvp
def _sigmoid(x):
    return 1.0 / (1.0 + jnp.exp(-x))


@_sigmoid.defjvp
def _sigmoid_jvp(primals, tangents):
    s = _sigmoid(primals[0])
    return s, tangents[0] * (s * (1.0 - s))


@jax.custom_jvp
def _silu(x):
    return x * _sigmoid(x)


@_silu.defjvp
def _silu_jvp(primals, tangents):
    x = primals[0]
    s = _sigmoid(x)
    return x * s, tangents[0] * (s * (1.0 + x * (1.0 - s)))


def _softplus(x):
    return jnp.maximum(x, 0.0) + jnp.log(1.0 + jnp.exp(-jnp.abs(x)))


def _rms(x, w):
    return x * lax.rsqrt(jnp.mean(x * x, axis=-1, keepdims=True) + EPS) * w


_DIMS = {"nn": (((2,), (1,)), ((0,), (0,))),
         "nt": (((2,), (2,)), ((0,), (0,))),
         "tn": (((1,), (1,)), ((0,), (0,)))}


def _bdot(a, b, spec):
    return lax.dot_general(a.astype(BF16), b.astype(BF16), _DIMS[spec], preferred_element_type=F32)


@functools.partial(jax.custom_vjp, nondiff_argnums=(2,))
def _bmm(a, b, spec):
    return _bdot(a, b, spec)


def _bmm_fwd(a, b, spec):
    return _bdot(a, b, spec), (a, b)


def _bmm_bwd(spec, res, g):
    a, b = res
    if spec == "nn":
        return _bdot(g, b, "nt"), _bdot(a, g, "tn")
    if spec == "nt":
        return _bdot(g, b, "nn"), _bdot(g, a, "tn")
    return _bdot(b, g, "nt"), _bdot(a, g, "nn")


_bmm.defvjp(_bmm_fwd, _bmm_bwd)


@jax.custom_vjp
def _tri_inv(n):
    t = -n
    p = n
    steps = (n.shape[-1] - 1).bit_length() - 1
    r = lax.broadcasted_iota(jnp.int32, n.shape, 1)
    c = lax.broadcasted_iota(jnp.int32, n.shape, 2)
    t = t + jnp.where(r == c, 1.0, 0.0)
    for _ in range(steps):
        p = _bdot(p, p, "nn")
        t = t + _bdot(t, p, "nn")
    return t


def _tri_inv_fwd(n):
    t = _tri_inv(n)
    return t, t


def _tri_inv_bwd(t, g):
    return (-_bdot(_bdot(t, g, "tn"), t, "nt"),)


_tri_inv.defvjp(_tri_inv_fwd, _tri_inv_bwd)


@jax.custom_vjp
def _tri_inv_known(n, t):
    del n
    return t


def _tri_inv_known_fwd(n, t):
    del n
    return t, t


def _tri_inv_known_bwd(t, g):
    return _tri_inv_bwd(t, g)[0], jnp.zeros_like(t)


_tri_inv_known.defvjp(_tri_inv_known_fwd, _tri_inv_known_bwd)


def _masks(c, lead=1):
    r = lax.broadcasted_iota(jnp.int32, (lead, c, c), 1)
    s = lax.broadcasted_iota(jnp.int32, (lead, c, c), 2)
    return r >= s, r > s, r == s, r <= s


def _row_to_col(row, eye):
    return jnp.sum(jnp.where(eye, row, 0.0), axis=2, keepdims=True)


def _gdn_chunk(q, k, v, braw, araw, alog, dtb, s, t_known=None, want_t=False):
    h = v.shape[0]
    c = v.shape[1]
    rep = h // q.shape[0]
    tril, strict, eye, triu = _masks(c)
    qq = jnp.broadcast_to(q[:, None], (q.shape[0], rep) + q.shape[1:]).reshape(v.shape)
    kk = jnp.broadcast_to(k[:, None], (k.shape[0], rep) + k.shape[1:]).reshape(v.shape)
    beta_row = _sigmoid(braw)
    g_row = -jnp.exp(alog) * _softplus(araw + dtb)
    beta_col = _row_to_col(beta_row, eye)
    g_col = _row_to_col(g_row, eye)
    gc_col = jnp.sum(jnp.where(tril, g_row, 0.0), axis=2, keepdims=True)
    gc_row = jnp.sum(jnp.where(triu, g_col, 0.0), axis=1, keepdims=True)
    gc_last = jnp.sum(g_row, axis=2, keepdims=True)
    lmat = jnp.exp(jnp.where(tril, gc_col - gc_row, NEG))
    kb = kk * beta_col
    vb = v * beta_col
    n = jnp.where(strict, _bmm(kb, kk, "nt") * lmat, 0.0)
    t = _tri_inv(n) if t_known is None else _tri_inv_known(n, t_known)
    e_col = jnp.exp(gc_col)
    u = _bmm(t, vb, "nn")
    w = _bmm(t, kb * e_col, "nn")
    attn = _bmm(qq, kk, "nt") * lmat
    q_dec = qq * e_col
    k_dec = kk * jnp.exp(gc_last - gc_col)
    v_new = u - _bmm(w, s, "nn")
    o = _bmm(q_dec, s, "nn") + _bmm(attn, v_new, "nn")
    s_new = s * jnp.exp(gc_last) + _bmm(k_dec, v_new, "tn")
    return (o, s_new, t) if want_t else (o, s_new)


def _ssd_chunk(xs, bm, cm, dtraw, dtb, alog, dskip, s):
    c = xs.shape[1]
    tril, _, eye, triu = _masks(c)
    lane = lax.broadcasted_iota(jnp.int32, (1, 1, SR * SP), 2)
    prow = lax.broadcasted_iota(jnp.int32, (1, SR * SP, 1), 1)
    cb = _bmm(cm, bm, "nt")
    cs = _bmm(cm, s, "nt")

    def per_head(vals, idx):
        out = vals[SR - 1]
        for r in reversed(range(SR - 1)):
            out = jnp.where(idx < (r + 1) * SP, vals[r], out)
        return out

    dt_cols, e_cols, lmats, dstates, declast = [], [], [], [], []
    for r in range(SR):
        dt_row = _softplus(dtraw[r] + dtb[r])
        adt_row = -jnp.exp(alog[r]) * dt_row
        dt_cols.append(_row_to_col(dt_row, eye))
        adt_col = _row_to_col(adt_row, eye)
        acs_col = jnp.sum(jnp.where(tril, adt_row, 0.0), axis=2, keepdims=True)
        acs_row = jnp.sum(jnp.where(triu, adt_col, 0.0), axis=1, keepdims=True)
        acs_last = jnp.sum(adt_row, axis=2, keepdims=True)
        lmats.append(jnp.exp(jnp.where(tril, acs_col - acs_row, NEG)))
        e_cols.append(jnp.exp(acs_col))
        dstates.append(jnp.exp(acs_last - acs_col))
        declast.append(jnp.exp(acs_last))
    xd = xs * per_head(dt_cols, lane)
    y = per_head([_bmm(cb * lmats[r], xd, "nn") for r in range(SR)], lane)
    states = per_head([_bmm(xd, bm * dstates[r], "tn") for r in range(SR)], prow)
    y = y + cs * per_head(e_cols, lane) + xs * per_head([dskip[r] for r in range(SR)], lane)
    s_new = s * per_head(declast, prow) + states
    return y, s_new


def _matmul(a, b, mode, name, out_dtype=F32, add=None, tm=1024, tn=1024, tk=2048):
    if mode == "nn":
        (m, k), n = a.shape, b.shape[1]
    elif mode == "nt":
        (m, k), n = a.shape, b.shape[0]
    else:
        (k, m), n = a.shape, b.shape[1]
    tm, tn, tk = min(tm, m), min(tn, n), min(tk, k)
    assert m % tm == 0 and n % tn == 0 and k % tk == 0, (name, m, n, k)
    nk = k // tk
    dims = {"nn": (((1,), (0,)), ((), ())), "nt": (((1,), (1,)), ((), ())),
            "tn": (((0,), (0,)), ((), ()))}[mode]
    a_spec = {"nn": pl.BlockSpec((tm, tk), lambda i, j, kk: (i, kk)),
              "nt": pl.BlockSpec((tm, tk), lambda i, j, kk: (i, kk)),
              "tn": pl.BlockSpec((tk, tm), lambda i, j, kk: (kk, i))}[mode]
    b_spec = {"nn": pl.BlockSpec((tk, tn), lambda i, j, kk: (kk, j)),
              "nt": pl.BlockSpec((tn, tk), lambda i, j, kk: (j, kk)),
              "tn": pl.BlockSpec((tk, tn), lambda i, j, kk: (kk, j))}[mode]
    o_spec = pl.BlockSpec((tm, tn), lambda i, j, kk: (i, j))
    has_add = add is not None

    def body(*refs):
        a_ref, b_ref = refs[:2]
        add_ref = refs[2] if has_add else None
        o_ref = refs[2 + has_add]
        part = lax.dot_general(a_ref[...].astype(BF16), b_ref[...].astype(BF16), dims,
                               preferred_element_type=F32)

        def finish(r):
            if has_add:
                r = r + add_ref[...].astype(F32)
            o_ref[...] = r.astype(o_ref.dtype)

        if nk == 1:
            finish(part)
            return
        acc_ref = refs[3 + has_add]
        kk = pl.program_id(2)

        @pl.when(kk == 0)
        def _():
            acc_ref[...] = part

        @pl.when(jnp.logical_and(kk > 0, kk < nk - 1))
        def _():
            acc_ref[...] += part

        @pl.when(kk == nk - 1)
        def _():
            finish(acc_ref[...] + part)

    ins = [a, b] + ([add] if has_add else [])
    in_specs = [a_spec, b_spec] + ([o_spec] if has_add else [])
    scratch = [] if nk == 1 else [pltpu.VMEM((tm, tn), F32)]
    return _pcall(body, grid=(m // tm, n // tn, nk), in_specs=in_specs, out_specs=o_spec,
                  out_shape=SDS((m, n), out_dtype), scratch_shapes=scratch,
                  compiler_params=_cp(("parallel", "parallel", "arbitrary")), name=name)(*ins)


def _ew(fn, ins, in_specs, out_shape, out_specs, grid, name):
    n_in = len(ins)

    def body(*refs):
        outs = fn(*[r[...] for r in refs[:n_in]])
        for r, o in zip(refs[n_in:], outs):
            r[...] = o.astype(r.dtype)

    return _pcall(body, grid=grid, in_specs=in_specs, out_specs=out_specs, out_shape=out_shape,
                  compiler_params=_cp(("arbitrary",) * len(grid)), name=name)(*ins)


def _ew_vjp(fn, ins, in_specs, cts, ct_specs, wrt, g_shape, g_specs, acc, grid, name):
    n_in, n_ct = len(ins), len(cts)

    def body(*refs):
        vals = [r[...].astype(F32) for r in refs[:n_in]]
        outs, vjp = jax.vjp(fn, *vals)
        g_all = vjp(tuple(r[...].astype(F32) for r in refs[n_in:n_in + n_ct]))
        for pos, (i, g_ref) in enumerate(zip(wrt, refs[n_in + n_ct:])):
            g = g_all[i]
            if pos in acc:
                first = functools.reduce(
                    jnp.logical_and, [pl.program_id(ax) == 0 for ax in range(acc[pos], len(grid))])

                @pl.when(first)
                def _():
                    g_ref[...] = jnp.zeros_like(g_ref)

                g_ref[...] += g.astype(g_ref.dtype)
            else:
                g_ref[...] = g.astype(g_ref.dtype)

    return _pcall(body, grid=grid, in_specs=list(in_specs) + list(ct_specs), out_specs=g_specs,
                  out_shape=g_shape, compiler_params=_cp(("arbitrary",) * len(grid)),
                  name=name)(*ins, *cts)


def _row_spec(tm, n):
    return pl.BlockSpec((tm, n), lambda i: (i, 0))


def _par_spec(n):
    return pl.BlockSpec((1, n), lambda i: (0, 0))


def _rms_fwd(x, w, name):
    t = x.shape[0]
    tm = min(t, 512)
    return _ew(lambda xv, wv: (_rms(xv, wv),), [x, w], [_row_spec(tm, D), _par_spec(D)],
               [SDS((t, D), BF16)], [_row_spec(tm, D)], (t // tm,), name)[0]


def _rms_bwd(x, w, dres, dhid, name):
    t = x.shape[0]
    tm = min(t, 512)
    return _ew_vjp(lambda xv, wv: (xv, _rms(xv, wv)), [x, w], [_row_spec(tm, D), _par_spec(D)],
                   [dres, dhid], [_row_spec(tm, D), _row_spec(tm, D)], (0, 1),
                   [SDS((t, D), F32), SDS((1, D), F32)], [_row_spec(tm, D), _par_spec(D)],
                   {1: 0}, (t // tm,), name)


def _final_loss(x, w, tgt, name):
    t = x.shape[0]
    tm = min(t, 512)

    def body(x_ref, w_ref, t_ref, dx_ref, dw_ref, l_ref):
        @pl.when(pl.program_id(0) == 0)
        def _():
            dw_ref[...] = jnp.zeros_like(dw_ref)
            l_ref[...] = jnp.zeros_like(l_ref)

        xv, wv = x_ref[...], w_ref[...]
        rstd = lax.rsqrt(jnp.mean(xv * xv, axis=-1, keepdims=True) + EPS)
        xh = xv * rstd
        err = xh * wv - t_ref[...]
        l_ref[...] += 0.5 * jnp.sum(jnp.mean(err * err, axis=-1, keepdims=True), axis=0, keepdims=True)
        dy = err * (1.0 / D)
        dw_ref[...] += jnp.sum(dy * xh, axis=0, keepdims=True)
        dxh = dy * wv
        dx_ref[...] = rstd * (dxh - xh * jnp.mean(dxh * xh, axis=-1, keepdims=True))

    return _pcall(body, grid=(t // tm,), in_specs=[_row_spec(tm, D), _par_spec(D), _row_spec(tm, D)],
                  out_specs=[_row_spec(tm, D), _par_spec(D), _par_spec(128)],
                  out_shape=[SDS((t, D), F32), SDS((1, D), F32), SDS((1, 128), F32)],
                  compiler_params=_cp(("arbitrary",)), name=name)(x, w, tgt)


def _conv_taps(ext, w_ref, tm, lo):
    n = ext.shape[0]
    acc = None
    for j in range(CONV_K):
        shift = (CONV_K - 1 - j) if lo else (n - (CONV_K - 1 - j)) % n
        rolled = pltpu.roll(ext, shift, 0) if shift else ext
        term = w_ref[pl.ds(j, 1), :] * rolled[lo:lo + tm]
        acc = term if acc is None else acc + term
    return acc


def _conv_pre_specs(tm, ct, col0):
    hb = tm // 8
    return [pl.BlockSpec((tm, ct), lambda j, i: (i, col0 + j)),
            pl.BlockSpec((8, ct), lambda j, i: (jnp.maximum(i * hb - 1, 0), col0 + j)),
            pl.BlockSpec((CONV_K, ct), lambda j, i: (0, j)),
            pl.BlockSpec((1, ct), lambda j, i: (0, j))]


def _conv_pre_value(x_ref, xh_ref, w_ref, b_ref, tm):
    halo = jnp.where(pl.program_id(1) > 0, xh_ref[...], 0.0)
    ext = jnp.concatenate([halo, x_ref[...]], axis=0)
    return _conv_taps(ext, w_ref, tm, 8) + b_ref[...], ext


def _conv_fwd(x, col0, w, b, ct, post, name):
    t = x.shape[0]
    tpb = FWD_COLS // ct
    wide = tpb * ct
    tm = min(t, BLOCK_ELEMS // wide)
    nt = w.shape[1] // ct
    assert col0 % wide == 0 and nt % tpb == 0, name

    def body(x_ref, xh_ref, w_ref, b_ref, o_ref):
        pre, _ = _conv_pre_value(x_ref, xh_ref, w_ref, b_ref, tm)
        for k in range(tpb):
            o_ref[k] = post(pre[:, k * ct:(k + 1) * ct])

    return _pcall(body, grid=(nt // tpb, t // tm), in_specs=_conv_pre_specs(tm, wide, col0 // wide),
                  out_specs=pl.BlockSpec((tpb, tm, ct), lambda j, i: (j, i, 0)),
                  out_shape=SDS((nt, t, ct), F32), compiler_params=_cp(("arbitrary", "arbitrary")),
                  name=name)(x, x, w, b)


def _conv_dpre(x, col0, w, b, ct, post, dout, name, c_total, c_off, into=None):
    t = x.shape[0]
    tm = min(t, BLOCK_ELEMS // ct)
    nt = w.shape[1] // ct
    chained = into is not None

    def body(*refs):
        x_ref, xh_ref, w_ref, b_ref, do_ref = refs[:5]
        dp_ref, dw_ref, db_ref = refs[5 + chained:]
        pre, ext = _conv_pre_value(x_ref, xh_ref, w_ref, b_ref, tm)
        _, vjp = jax.vjp(post, pre)
        dpre = vjp(do_ref[0])[0]
        dp_ref[...] = dpre

        @pl.when(pl.program_id(1) == 0)
        def _():
            dw_ref[...] = jnp.zeros_like(dw_ref)
            db_ref[...] = jnp.zeros_like(db_ref)

        for j in range(CONV_K):
            xs = (pltpu.roll(ext, CONV_K - 1 - j, 0) if j < CONV_K - 1 else ext)[8:8 + tm]
            dw_ref[pl.ds(j, 1), :] += jnp.sum(dpre * xs, axis=0, keepdims=True)
        db_ref[...] += jnp.sum(dpre, axis=0, keepdims=True)

    c = w.shape[1]
    o0 = c_off // ct
    return _pcall(body, grid=(nt, t // tm),
                  in_specs=_conv_pre_specs(tm, ct, col0 // ct)
                  + [pl.BlockSpec((1, tm, ct), lambda j, i: (j, i, 0))] + ([ANY] if chained else []),
                  out_specs=[pl.BlockSpec((tm, ct), lambda j, i: (i, o0 + j)),
                             pl.BlockSpec((CONV_K, ct), lambda j, i: (0, j)),
                             pl.BlockSpec((1, ct), lambda j, i: (0, j))],
                  out_shape=[SDS((t, c_total), F32), SDS((CONV_K, c), F32), SDS((1, c), F32)],
                  input_output_aliases={5: 0} if chained else {},
                  compiler_params=_cp(("arbitrary", "arbitrary")), name=name)(
                      x, x, w, b, dout, *([into] if chained else []))


def _conv_t(dpre, w, name):
    t, c = dpre.shape
    ct = min(c, 512)
    tm = min(t, BLOCK_ELEMS // ct)
    hb = tm // 8
    last = t // tm - 1

    def body(d_ref, dh_ref, w_ref, o_ref):
        halo = jnp.where(pl.program_id(1) < last, dh_ref[...], 0.0)
        ext = jnp.concatenate([d_ref[...], halo], axis=0)
        o_ref[...] = _conv_taps(ext, w_ref, tm, 0).astype(o_ref.dtype)

    return _pcall(body, grid=(c // ct, t // tm),
                  in_specs=[pl.BlockSpec((tm, ct), lambda j, i: (i, j)),
                            pl.BlockSpec((8, ct), lambda j, i: (jnp.minimum((i + 1) * hb, t // 8 - 1), j)),
                            pl.BlockSpec((CONV_K, ct), lambda j, i: (0, j))],
                  out_specs=pl.BlockSpec((tm, ct), lambda j, i: (i, j)),
                  out_shape=SDS((t, c), BF16), compiler_params=_cp(("arbitrary", "arbitrary")),
                  name=name)(dpre, dpre, w)


def _l2norm_scaled(scale):
    def post(pre):
        a = _silu(pre)
        return a * lax.rsqrt(jnp.sum(a * a, axis=-1, keepdims=True) + EPS) * scale
    return post


GDN_HB = 16
GDN_CPB = 2
SSD_GB = 8
SSD_CPB = 1


def _gdn_specs(nb, rev):
    hb, cpb, tc = GDN_HB, GDN_CPB, GDN_CPB * GCH
    blk = (lambda n: nb - 1 - n) if rev else (lambda n: n)
    seq = lambda h: pl.BlockSpec((h, tc, GDK), lambda g, n: (g, blk(n), 0))
    gate = pl.BlockSpec((1, hb, cpb, GCH), lambda g, n: (blk(n), g, 0, 0))
    par = pl.BlockSpec((hb, 1, 1), lambda g, n: (g, 0, 0))
    state = pl.BlockSpec((cpb, hb, GDK, GDK), lambda g, n: (blk(n), g, 0, 0))
    tinv = pl.BlockSpec((cpb, hb, GCH, GCH), lambda g, n: (blk(n), g, 0, 0))
    return seq, gate, par, state, tinv


def _gdn_fwd(q, k, v, braw, araw, alog, dtb):
    t = v.shape[1]
    hb, cpb = GDN_HB, GDN_CPB
    nb = t // (cpb * GCH)
    seq, gate, par, state, tinv = _gdn_specs(nb, False)

    def body(q_ref, k_ref, v_ref, b_ref, a_ref, al_ref, dt_ref, o_ref, sall_ref, tall_ref, s_ref):
        @pl.when(pl.program_id(1) == 0)
        def _():
            s_ref[...] = jnp.zeros_like(s_ref)

        s = s_ref[...]
        for c in range(cpb):
            rows = pl.ds(c * GCH, GCH)
            sall_ref[c] = s
            o, s, tmat = _gdn_chunk(q_ref[:, rows, :], k_ref[:, rows, :], v_ref[:, rows, :],
                                    b_ref[0, :, pl.ds(c, 1), :], a_ref[0, :, pl.ds(c, 1), :],
                                    al_ref[...], dt_ref[...], s, want_t=True)
            o_ref[:, rows, :] = o
            tall_ref[c] = tmat.astype(BF16)
        s_ref[...] = s

    return _pcall(body, grid=(GH // hb, nb),
                  in_specs=[seq(hb // 2), seq(hb // 2), seq(hb), gate, gate, par, par],
                  out_specs=[seq(hb), state, tinv],
                  out_shape=[SDS((GH, t, GDK), F32), SDS((t // GCH, GH, GDK, GDK), F32),
                             SDS((t // GCH, GH, GCH, GCH), BF16)],
                  scratch_shapes=[pltpu.VMEM((hb, GDK, GDK), F32)],
                  compiler_params=_cp(("arbitrary", "arbitrary")), name="gdn_chunk_fwd")(
                      q, k, v, braw, araw, alog, dtb)


def _gdn_bwd(q, k, v, braw, araw, alog, dtb, sall, tall, do):
    t = v.shape[1]
    hb, cpb = GDN_HB, GDN_CPB
    nb = t // (cpb * GCH)
    seq, gate, par, state, tinv = _gdn_specs(nb, True)

    def body(q_ref, k_ref, v_ref, b_ref, a_ref, al_ref, dt_ref, sall_ref, tall_ref, do_ref,
             dq_ref, dk_ref, dv_ref, db_ref, da_ref, dal_ref, ddt_ref, ds_ref):
        @pl.when(pl.program_id(1) == 0)
        def _():
            ds_ref[...] = jnp.zeros_like(ds_ref)
            dal_ref[...] = jnp.zeros_like(dal_ref)
            ddt_ref[...] = jnp.zeros_like(ddt_ref)

        ds = ds_ref[...]
        for c in reversed(range(cpb)):
            rows = pl.ds(c * GCH, GCH)
            fn = functools.partial(_gdn_chunk, t_known=tall_ref[c].astype(F32))
            _, vjp = jax.vjp(fn, q_ref[:, rows, :], k_ref[:, rows, :], v_ref[:, rows, :],
                             b_ref[0, :, pl.ds(c, 1), :], a_ref[0, :, pl.ds(c, 1), :],
                             al_ref[...], dt_ref[...], sall_ref[c])
            dq, dk, dv, db, da, dal, ddt, ds = vjp((do_ref[:, rows, :], ds))
            dq_ref[:, rows, :] = dq
            dk_ref[:, rows, :] = dk
            dv_ref[:, rows, :] = dv
            db_ref[0, :, pl.ds(c, 1), :] = db
            da_ref[0, :, pl.ds(c, 1), :] = da
            dal_ref[...] += dal
            ddt_ref[...] += ddt
        ds_ref[...] = ds

    return _pcall(body, grid=(GH // hb, nb),
                  in_specs=[seq(hb // 2), seq(hb // 2), seq(hb), gate, gate, par, par, state, tinv, seq(hb)],
                  out_specs=[seq(hb // 2), seq(hb // 2), seq(hb), gate, gate, par, par],
                  out_shape=[SDS(q.shape, F32), SDS(k.shape, F32), SDS(v.shape, F32),
                             SDS(braw.shape, F32), SDS(araw.shape, F32),
                             SDS((GH, 1, 1), F32), SDS((GH, 1, 1), F32)],
                  scratch_shapes=[pltpu.VMEM((hb, GDK, GDK), F32)],
                  compiler_params=_cp(("arbitrary", "arbitrary")), name="gdn_chunk_bwd")(
                      q, k, v, braw, araw, alog, dtb, sall, tall, do)


def _ssd_specs(nb, rev):
    gb, cpb, tc = SSD_GB, SSD_CPB, SSD_CPB * SCH
    blk = (lambda n: nb - 1 - n) if rev else (lambda n: n)
    seq = lambda w: pl.BlockSpec((gb, tc, w), lambda g, n: (g, blk(n), 0))
    gate = pl.BlockSpec((1, cpb * SR, gb, 1, SCH), lambda g, n: (blk(n), 0, g, 0, 0))
    par = pl.BlockSpec((SR, gb, 1, 1), lambda g, n: (0, g, 0, 0))
    state = pl.BlockSpec((cpb, gb, SR * SP, SN), lambda g, n: (blk(n), g, 0, 0))
    return seq, gate, par, state


def _ssd_fwd(xs, bm, cm, dtraw, dtb, alog, dskip):
    t = xs.shape[1]
    gb, cpb = SSD_GB, SSD_CPB
    nb = t // (cpb * SCH)
    seq, gate, par, state = _ssd_specs(nb, False)

    def body(x_ref, b_ref, c_ref, dt_ref, dtb_ref, al_ref, dk_ref, y_ref, sall_ref, s_ref):
        @pl.when(pl.program_id(1) == 0)
        def _():
            s_ref[...] = jnp.zeros_like(s_ref)

        s = s_ref[...]
        for c in range(cpb):
            rows = pl.ds(c * SCH, SCH)
            sall_ref[c] = s
            y, s = _ssd_chunk(x_ref[:, rows, :], b_ref[:, rows, :], c_ref[:, rows, :],
                              dt_ref[0, pl.ds(c * SR, SR)], dtb_ref[...], al_ref[...],
                              dk_ref[...], s)
            y_ref[:, rows, :] = y
        s_ref[...] = s

    return _pcall(body, grid=(SG // gb, nb),
                  in_specs=[seq(SR * SP), seq(SN), seq(SN), gate, par, par, par],
                  out_specs=[seq(SR * SP), state],
                  out_shape=[SDS((SG, t, SR * SP), F32), SDS((t // SCH, SG, SR * SP, SN), F32)],
                  scratch_shapes=[pltpu.VMEM((gb, SR * SP, SN), F32)],
                  compiler_params=_cp(("arbitrary", "arbitrary")), name="ssd_chunk_fwd")(
                      xs, bm, cm, dtraw, dtb, alog, dskip)


def _ssd_bwd(xs, bm, cm, dtraw, dtb, alog, dskip, sall, dy):
    t = xs.shape[1]
    gb, cpb = SSD_GB, SSD_CPB
    nb = t // (cpb * SCH)
    seq, gate, par, state = _ssd_specs(nb, True)

    def body(x_ref, b_ref, c_ref, dt_ref, dtb_ref, al_ref, dk_ref, sall_ref, dy_ref,
             dx_ref, dbm_ref, dcm_ref, ddt_ref, ddtb_ref, dal_ref, ddk_ref, ds_ref):
        @pl.when(pl.program_id(1) == 0)
        def _():
            ds_ref[...] = jnp.zeros_like(ds_ref)
            ddtb_ref[...] = jnp.zeros_like(ddtb_ref)
            dal_ref[...] = jnp.zeros_like(dal_ref)
            ddk_ref[...] = jnp.zeros_like(ddk_ref)

        ds = ds_ref[...]
        for c in reversed(range(cpb)):
            rows = pl.ds(c * SCH, SCH)
            _, vjp = jax.vjp(_ssd_chunk, x_ref[:, rows, :], b_ref[:, rows, :], c_ref[:, rows, :],
                             dt_ref[0, pl.ds(c * SR, SR)], dtb_ref[...], al_ref[...],
                             dk_ref[...], sall_ref[c])
            dx, dbm, dcm, ddt, ddtb, dal, ddk, ds = vjp((dy_ref[:, rows, :], ds))
            dx_ref[:, rows, :] = dx
            dbm_ref[:, rows, :] = dbm
            dcm_ref[:, rows, :] = dcm
            ddt_ref[0, pl.ds(c * SR, SR)] = ddt
            ddtb_ref[...] += ddtb
            dal_ref[...] += dal
            ddk_ref[...] += ddk
        ds_ref[...] = ds

    return _pcall(body, grid=(SG // gb, nb),
                  in_specs=[seq(SR * SP), seq(SN), seq(SN), gate, par, par, par, state, seq(SR * SP)],
                  out_specs=[seq(SR * SP), seq(SN), seq(SN), gate, par, par, par],
                  out_shape=[SDS(xs.shape, F32), SDS(bm.shape, F32), SDS(cm.shape, F32),
                             SDS(dtraw.shape, F32), SDS((SR, SG, 1, 1), F32), SDS((SR, SG, 1, 1), F32),
                             SDS((SR, SG, 1, 1), F32)],
                  scratch_shapes=[pltpu.VMEM((gb, SR * SP, SN), F32)],
                  compiler_params=_cp(("arbitrary", "arbitrary")), name="ssd_chunk_bwd")(
                      xs, bm, cm, dtraw, dtb, alog, dskip, sall, dy)


def _gate_specs(tm, ct, zcol0, per_tile_w):
    z0 = zcol0 // ct
    return [pl.BlockSpec((1, tm, ct), lambda i, j: (j, i, 0)),
            pl.BlockSpec((tm, ct), lambda i, j: (i, z0 + j)),
            pl.BlockSpec((1, ct), (lambda i, j: (0, j)) if per_tile_w else (lambda i, j: (0, 0)))]


def _gdn_gate(o, z, w):
    return (_rms(o[0], w) * _silu(z),)


def _ssd_gate(y, z, w):
    return (_rms(y[0] * _silu(z), w),)


def _gate_fwd(fn, o, proj, zcol0, w, name):
    nt, t, ct = o.shape
    tpb = FWD_COLS // ct
    wide = tpb * ct
    tm = min(t, BLOCK_ELEMS // wide)
    per_tile_w = w.shape[1] > ct
    assert zcol0 % wide == 0 and nt % tpb == 0, name
    z0 = zcol0 // wide
    specs = [pl.BlockSpec((tpb, tm, ct), lambda i, j: (j, i, 0)),
             pl.BlockSpec((tm, wide), lambda i, j: (i, z0 + j)),
             pl.BlockSpec((1, wide), lambda i, j: (0, j)) if per_tile_w
             else pl.BlockSpec((1, ct), lambda i, j: (0, 0))]

    def fn_wide(ov, zv, wv):
        outs = []
        for k in range(tpb):
            cols = slice(k * ct, (k + 1) * ct)
            outs.append(fn(ov[k:k + 1], zv[:, cols], wv[:, cols] if per_tile_w else wv)[0])
        return (jnp.concatenate(outs, axis=1),)

    return _ew(fn_wide, [o, proj, w], specs, [SDS((t, nt * ct), BF16)],
               [pl.BlockSpec((tm, wide), lambda i, j: (i, j))], (t // tm, nt // tpb), name)[0]


def _gate_bwd(fn, o, proj, zcol0, w, dy, wacc, name):
    nt, t, ct = o.shape
    tm = min(t, BLOCK_ELEMS // ct)
    specs = _gate_specs(tm, ct, zcol0, wacc)
    out_spec = pl.BlockSpec((tm, ct), lambda i, j: (i, j))
    if wacc:
        flip = lambda s: pl.BlockSpec(s.block_shape, lambda j, i, f=s.index_map: f(i, j))
        specs = [flip(s) for s in specs]
        out_spec = flip(out_spec)
        grid, acc = (nt, t // tm), {2: 1}
    else:
        grid, acc = (t // tm, nt), {2: 0}
    return _ew_vjp(fn, [o, proj, w], specs, [dy], [out_spec], (0, 1, 2),
                   [SDS(o.shape, F32), SDS((t, nt * ct), BF16), SDS(w.shape, F32)],
                   [specs[0], out_spec, specs[2]], acc, grid, name)


def _adamw_math(w, g, m, v):
    m = ADAM_B1 * m + (1.0 - ADAM_B1) * g
    v = ADAM_B2 * v + (1.0 - ADAM_B2) * jnp.square(g)
    m_hat = m / (1.0 - ADAM_B1 ** ADAM_STEP)
    v_hat = v / (1.0 - ADAM_B2 ** ADAM_STEP)
    delta = -ADAM_LR * (m_hat / (jnp.sqrt(v_hat) + ADAM_EPS) + ADAM_WD * w)
    return delta, m, v


def _adamw(w, g, m, v, name):
    shape = w.shape
    w2, g2, m2, v2 = [a.reshape(-1, shape[-1]) for a in (w, g, m, v)]
    r, c = w2.shape
    tr = 256 if r % 256 == 0 else r
    spec = pl.BlockSpec((tr, c), lambda i: (i, 0))
    outs = _ew(_adamw_math, [w2, g2, m2, v2], [spec] * 4, [SDS((r, c), F32)] * 3, [spec] * 3,
               (r // tr,), name)
    return [o.reshape(shape) for o in outs]


def _coords():
    return lax.axis_index("x"), lax.axis_index("y"), lax.axis_index("c")


def _other_chips(x, y):
    return [(1 - x, y), (x, 1 - y), (1 - x, 1 - y)]


ANY = pl.BlockSpec(memory_space=pl.ANY)


def _rcopy(src, dst, send_sems, recv_sems, k, to):
    return pltpu.make_async_remote_copy(src_ref=src, dst_ref=dst, send_sem=send_sems.at[k],
                                        recv_sem=recv_sems.at[k], device_id=to, device_id_type=MESH)


def _gather_weights(bigs, small):
    n = len(bigs)

    def body(*refs):
        in_refs, small_ref = refs[:n], refs[n]
        out_refs, osmall_ref = refs[n + 1:2 * n + 1], refs[2 * n + 1]
        send_sems, recv_sems = refs[2 * n + 2:]
        x, y, c = _coords()
        me = 2 * x + y
        sibling = (x, y, 1 - c)
        chips = _other_chips(x, y)
        halves = [b.shape[0] // 2 for b in bigs]
        mine = [pl.ds(c * h, h) for h in halves]
        theirs = [pl.ds((1 - c) * h, h) for h in halves]
        rc = functools.partial(_rcopy, send_sems=send_sems, recv_sems=recv_sems)

        first = []
        for a in range(n):
            for j, (px, py) in enumerate(chips):
                first.append(rc(in_refs[a].at[mine[a]], out_refs[a].at[me, mine[a]], k=3 * a + j, to=(px, py, c)))
        for j, (px, py) in enumerate(chips):
            first.append(rc(small_ref, osmall_ref.at[me], k=6 * n + j, to=(px, py, c)))
        for cp in first:
            cp.start()
        passed = []
        for a in range(n):
            for j, (px, py) in enumerate(chips):
                landed = out_refs[a].at[2 * px + py, mine[a]]
                rc(landed, landed, k=3 * a + j, to=(px, py, c)).wait_recv()
                fw = rc(landed, landed, k=3 * n + 3 * a + j, to=sibling)
                fw.start()
                passed.append(fw)
        for a in range(n):
            for j, (px, py) in enumerate(chips):
                landed = out_refs[a].at[2 * px + py, theirs[a]]
                rc(landed, landed, k=3 * n + 3 * a + j, to=sibling).wait_recv()
        for j, (px, py) in enumerate(chips):
            rc(small_ref, osmall_ref.at[2 * px + py], k=6 * n + j, to=(px, py, c)).wait_recv()
        for cp in first + passed:
            cp.wait_send()

    outs = _pcall(body, in_specs=[ANY] * (n + 1), out_specs=[ANY] * (n + 1),
                  out_shape=[SDS((N_CHIPS,) + b.shape, b.dtype) for b in bigs]
                  + [SDS((N_CHIPS,) + small.shape, small.dtype)],
                  scratch_shapes=[pltpu.SemaphoreType.DMA((6 * n + 3,)), pltpu.SemaphoreType.DMA((6 * n + 3,))],
                  name="gather_weights")(*bigs, small)
    me = 2 * lax.axis_index("x") + lax.axis_index("y")
    return [lax.dynamic_update_index_in_dim(o, own, me, 0) for o, own in zip(outs, list(bigs) + [small])]


def _swap_halves(gs):
    n = len(gs)

    def body(*refs):
        send_sems, recv_sems = refs[2 * n:]
        x, y, c = _coords()
        cps = [_rcopy(refs[a].at[:, 1 - c], refs[n + a], send_sems, recv_sems, a, (x, y, 1 - c))
               for a in range(n)]
        for cp in cps:
            cp.start()
        for cp in cps:
            cp.wait()

    return _pcall(body, in_specs=[ANY] * n, out_specs=[ANY] * n,
                  out_shape=[SDS((N_CHIPS,) + g.shape[2:], g.dtype) for g in gs],
                  scratch_shapes=[pltpu.SemaphoreType.DMA((n,)), pltpu.SemaphoreType.DMA((n,))],
                  name="swap_halves")(*gs)


def _sum_cores(gs, rs, half_idx):
    n = len(gs)
    ns = 2
    in_specs, out_specs, out_shape = [], [], []
    for g in gs:
        _, _, h, w = g.shape
        in_specs.append(pl.BlockSpec((1, 1, h // ns, w), lambda b, i, c_ref: (b, c_ref[0], i, 0)))
    for g in gs:
        _, _, h, w = g.shape
        spec = pl.BlockSpec((1, h // ns, w), lambda b, i, c_ref: (b, i, 0))
        in_specs.append(spec)
        out_specs += [spec, spec]
        out_shape += [SDS((N_CHIPS, h, w), F32), SDS((N_CHIPS, h, w), BF16)]

    def body(c_ref, *refs):
        del c_ref
        for a in range(n):
            tot = refs[a][0] + refs[n + a][...]
            refs[2 * n + 2 * a][...] = tot
            refs[2 * n + 2 * a + 1][...] = tot.astype(BF16)

    outs = _pcall(body, grid_spec=pltpu.PrefetchScalarGridSpec(
        num_scalar_prefetch=1, grid=(N_CHIPS, ns), in_specs=in_specs, out_specs=out_specs),
        out_shape=out_shape, compiler_params=_cp(("arbitrary", "arbitrary")), name="sum_cores")(
            half_idx, *gs, *rs)
    return outs[0::2], outs[1::2]


def _scatter_chips(hs):
    n = len(hs)

    def body(*refs):
        send_sems, recv_sems = refs[2 * n:]
        x, y, c = _coords()
        cps = []
        for a in range(n):
            for j, (px, py) in enumerate(_other_chips(x, y)):
                cps.append(_rcopy(refs[a].at[2 * px + py], refs[n + a].at[j], send_sems, recv_sems,
                                  3 * a + j, (px, py, c)))
        for cp in cps:
            cp.start()
        for cp in cps:
            cp.wait()

    return _pcall(body, in_specs=[ANY] * n, out_specs=[ANY] * n,
                  out_shape=[SDS((3,) + h.shape[1:], h.dtype) for h in hs],
                  scratch_shapes=[pltpu.SemaphoreType.DMA((3 * n,)), pltpu.SemaphoreType.DMA((3 * n,))],
                  name="scatter_chips")(*hs)


def _sum_chips(hs, xs, chip_idx):
    n = len(hs)
    ns = 2
    in_specs, out_specs, out_shape = [], [], []
    for h_arr in hs:
        _, h, w = h_arr.shape
        in_specs.append(pl.BlockSpec((1, h // ns, w), lambda i, c_ref: (c_ref[0], i, 0)))
    for h_arr in hs:
        _, h, w = h_arr.shape
        in_specs.append(pl.BlockSpec((3, h // ns, w), lambda i, c_ref: (0, i, 0)))
        out_specs.append(pl.BlockSpec((h // ns, w), lambda i, c_ref: (i, 0)))
        out_shape.append(SDS((h, w), F32))

    def body(c_ref, *refs):
        del c_ref
        for a in range(n):
            x_ref = refs[n + a]
            refs[2 * n + a][...] = (refs[a][0] + x_ref[0].astype(F32) + x_ref[1].astype(F32)
                                    + x_ref[2].astype(F32))

    return _pcall(body, grid_spec=pltpu.PrefetchScalarGridSpec(
        num_scalar_prefetch=1, grid=(ns,), in_specs=in_specs, out_specs=out_specs),
        out_shape=out_shape, compiler_params=_cp(("arbitrary",)), name="sum_chips")(chip_idx, *hs, *xs)


def _swap_totals(tots):
    n = len(tots)

    def body(*refs):
        send_sems, recv_sems = refs[2 * n:]
        x, y, c = _coords()
        cps = [_rcopy(refs[a], refs[n + a], send_sems, recv_sems, a, (x, y, 1 - c)) for a in range(n)]
        for cp in cps:
            cp.start()
        for cp in cps:
            cp.wait()

    return _pcall(body, in_specs=[ANY] * n, out_specs=[ANY] * n,
                  out_shape=[SDS(t.shape, t.dtype) for t in tots],
                  scratch_shapes=[pltpu.SemaphoreType.DMA((n,)), pltpu.SemaphoreType.DMA((n,))],
                  name="swap_totals")(*tots)


def _allreduce_small(buf):
    rows = buf.shape[0]

    def body(b_ref, o_ref, g_ref, send_sems, recv_sems):
        x, y, c = _coords()
        me = 4 * x + 2 * y + c
        g_ref[me] = b_ref[...]
        cps = []
        for k in range(1, 8):
            px = 1 - x if k & 4 else x
            py = 1 - y if k & 2 else y
            pc = 1 - c if k & 1 else c
            cps.append(pltpu.make_async_remote_copy(
                src_ref=b_ref, dst_ref=g_ref.at[me], send_sem=send_sems.at[k - 1],
                recv_sem=recv_sems.at[k - 1], device_id=(px, py, pc), device_id_type=MESH))
        for cp in cps:
            cp.start()
        for cp in cps:
            cp.wait()
        acc = g_ref[0]
        for d in range(1, 8):
            acc = acc + g_ref[d]
        o_ref[...] = acc

    vm = pl.BlockSpec(memory_space=pltpu.VMEM)
    return _pcall(body, in_specs=[vm], out_specs=vm, out_shape=SDS(buf.shape, F32),
                  scratch_shapes=[pltpu.VMEM((8, rows, 128), F32), pltpu.SemaphoreType.DMA((7,)),
                                  pltpu.SemaphoreType.DMA((7,))],
                  compiler_params=pltpu.CompilerParams(vmem_limit_bytes=VMEM_LIMIT),
                  name="allreduce_small")(buf)


def _flat128(a):
    return a.reshape(-1, 128)


def _pad_rows(a, mult):
    r = (-a.shape[0]) % mult
    return a if r == 0 else jnp.concatenate([a, jnp.zeros((r, a.shape[1]), a.dtype)], axis=0)


def _pack_flat(parts, mult):
    rows, offs, r0 = [], [], 0
    for p in parts:
        f = p.reshape(-1)
        pad = (-f.shape[0]) % 128
        if pad:
            f = jnp.concatenate([f, jnp.zeros((pad,), f.dtype)])
        f = f.reshape(-1, 128)
        rows.append(f)
        offs.append((r0, p.shape))
        r0 += f.shape[0]
    return _pad_rows(jnp.concatenate(rows, axis=0), mult), offs


def _unpack_flat(buf, offs):
    out = []
    for r0, shape in offs:
        n = 1
        for s in shape:
            n *= s
        nr = -(-n // 128)
        out.append(buf[r0:r0 + nr].reshape(-1)[:n].reshape(shape))
    return out


def _gates_to_rows(a, heads, chunk, cpb):
    t = a.shape[0]
    return a.reshape(t // (chunk * cpb), cpb, chunk, heads).transpose(0, 3, 1, 2)


def _rows_to_gates(a):
    nb, heads, cpb, chunk = a.shape
    return a.transpose(0, 2, 3, 1).reshape(nb * cpb * chunk, heads)


def kernel(x, norm_w, gdn_w_in, gdn_conv_w, gdn_a_log, gdn_dt_bias, gdn_norm_w, gdn_w_out, ssd_w_in, ssd_conv_w, ssd_conv_b, ssd_dt_bias, ssd_a_log, ssd_d, ssd_norm_w, ssd_w_out, final_norm_w, loss_target, m_norm_w, m_gdn_w_in, m_gdn_conv_w, m_gdn_a_log, m_gdn_dt_bias, m_gdn_norm_w, m_gdn_w_out, m_ssd_w_in, m_ssd_conv_w, m_ssd_conv_b, m_ssd_dt_bias, m_ssd_a_log, m_ssd_d, m_ssd_norm_w, m_ssd_w_out, m_final_norm_w, v_norm_w, v_gdn_w_in, v_gdn_conv_w, v_gdn_a_log, v_gdn_dt_bias, v_gdn_norm_w, v_gdn_w_out, v_ssd_w_in, v_ssd_conv_w, v_ssd_conv_b, v_ssd_dt_bias, v_ssd_a_log, v_ssd_d, v_ssd_norm_w, v_ssd_w_out, v_final_norm_w):
    ws = dict(norm_w=norm_w, gdn_w_in=gdn_w_in, gdn_conv_w=gdn_conv_w, gdn_a_log=gdn_a_log,
              gdn_dt_bias=gdn_dt_bias, gdn_norm_w=gdn_norm_w, gdn_w_out=gdn_w_out, ssd_w_in=ssd_w_in,
              ssd_conv_w=ssd_conv_w, ssd_conv_b=ssd_conv_b, ssd_dt_bias=ssd_dt_bias,
              ssd_a_log=ssd_a_log, ssd_d=ssd_d, ssd_norm_w=ssd_norm_w, ssd_w_out=ssd_w_out,
              final_norm_w=final_norm_w)
    ms = dict(norm_w=m_norm_w, gdn_w_in=m_gdn_w_in, gdn_conv_w=m_gdn_conv_w, gdn_a_log=m_gdn_a_log,
              gdn_dt_bias=m_gdn_dt_bias, gdn_norm_w=m_gdn_norm_w, gdn_w_out=m_gdn_w_out,
              ssd_w_in=m_ssd_w_in, ssd_conv_w=m_ssd_conv_w, ssd_conv_b=m_ssd_conv_b,
              ssd_dt_bias=m_ssd_dt_bias, ssd_a_log=m_ssd_a_log, ssd_d=m_ssd_d,
              ssd_norm_w=m_ssd_norm_w, ssd_w_out=m_ssd_w_out, final_norm_w=m_final_norm_w)
    vs = dict(norm_w=v_norm_w, gdn_w_in=v_gdn_w_in, gdn_conv_w=v_gdn_conv_w, gdn_a_log=v_gdn_a_log,
              gdn_dt_bias=v_gdn_dt_bias, gdn_norm_w=v_gdn_norm_w, gdn_w_out=v_gdn_w_out,
              ssd_w_in=v_ssd_w_in, ssd_conv_w=v_ssd_conv_w, ssd_conv_b=v_ssd_conv_b,
              ssd_dt_bias=v_ssd_dt_bias, ssd_a_log=v_ssd_a_log, ssd_d=v_ssd_d,
              ssd_norm_w=v_ssd_norm_w, ssd_w_out=v_ssd_w_out, final_norm_w=v_final_norm_w)
    names = list(ws)
    cx, cy, cc = _coords()
    chip = 2 * cx + cy
    t = x.shape[1]
    x0 = x.reshape(t, D)
    tgt = loss_target.reshape(t, D)

    bigs = [p[0].astype(BF16) for p in (gdn_w_in, gdn_w_out, ssd_w_in, ssd_w_out)]
    small, small_offs = _pack_flat([gdn_conv_w[0], ssd_conv_w[0], ssd_conv_b[0], ssd_norm_w[0]], 8)
    a_gi, a_go, a_si, a_so, gsmall = _gather_weights(bigs, small)
    w_gi = jnp.concatenate([a_gi[b] for b in range(4)], axis=1)
    w_si = jnp.concatenate([a_si[b] for b in range(4)], axis=1)
    w_go = a_go.reshape(4 * OUT_SHARD, D)
    w_so = a_so.reshape(4 * OUT_SHARD, D)
    sm = [_unpack_flat(gsmall[b], small_offs) for b in range(4)]
    g_cw = jnp.concatenate([sm[b][0] for b in range(4)], axis=1)
    s_cw = jnp.concatenate([sm[b][1] for b in range(4)], axis=1)
    s_cb = jnp.concatenate([sm[b][2] for b in range(4)], axis=0)[None]
    s_nw = jnp.concatenate([sm[b][3] for b in range(4)], axis=0)[None]

    def pad_small(w):
        return jnp.concatenate([w, jnp.zeros((D, SMALL_W - w.shape[1]), w.dtype)], axis=1)

    wg_main, wg_small = w_gi[:, :G_MAIN], pad_small(w_gi[:, G_MAIN:])
    ws_main, ws_small = w_si[:, :S_MAIN], pad_small(w_si[:, S_MAIN:])
    zero_b = jnp.zeros((1, G_CONV), F32)
    nw0, nw1 = norm_w[0:1], norm_w[1:2]
    fw = final_norm_w[None]
    g_alog = gdn_a_log.reshape(GH, 1, 1)
    g_dtb = gdn_dt_bias.reshape(GH, 1, 1)
    g_nw = gdn_norm_w.reshape(1, GDK)
    s_dtb = ssd_dt_bias.reshape(SG, SR).T.reshape(SR, SG, 1, 1)
    s_alog = ssd_a_log.reshape(SG, SR).T.reshape(SR, SG, 1, 1)
    s_d = ssd_d.reshape(SG, SR).T.reshape(SR, SG, 1, 1)

    hid0 = _rms_fwd(x0, nw0, "rms0")
    pg = _matmul(hid0, wg_main, "nn", "gdn_in_proj", tn=1536)
    pg_small = _matmul(hid0, wg_small, "nn", "gdn_in_proj_small", tn=SMALL_W)
    post_q = _l2norm_scaled(GDK ** -0.5)
    post_k = _l2norm_scaled(1.0)
    q = _conv_fwd(pg, 0, g_cw[:, :G_QK], zero_b[:, :G_QK], GDK, post_q, "gdn_conv_q")
    k = _conv_fwd(pg, G_QK, g_cw[:, G_QK:2 * G_QK], zero_b[:, :G_QK], GDK, post_k, "gdn_conv_k")
    v = _conv_fwd(pg, 2 * G_QK, g_cw[:, 2 * G_QK:], zero_b[:, :G_V], GDK, _silu, "gdn_conv_v")
    braw = _gates_to_rows(pg_small[:, :GH], GH, GCH, GDN_CPB)
    araw = _gates_to_rows(pg_small[:, GH:2 * GH], GH, GCH, GDN_CPB)
    o, g_sall, g_tall = _gdn_fwd(q, k, v, braw, araw, g_alog, g_dtb)
    y0 = _gate_fwd(_gdn_gate, o, pg, G_CONV, g_nw, "gdn_gate")
    x1 = _matmul(y0, w_go, "nn", "gdn_out_proj", add=x0)

    hid1 = _rms_fwd(x1, nw1, "rms1")
    ps = _matmul(hid1, ws_main, "nn", "ssd_in_proj", tn=1536)
    ps_small = _matmul(hid1, ws_small, "nn", "ssd_in_proj_small", tn=SMALL_W)
    c_x, c_b, c_c = S_INNER, 2 * S_INNER, 2 * S_INNER + SG * SN
    post_s = _silu
    xs = _conv_fwd(ps, c_x, s_cw[:, :S_INNER], s_cb[:, :S_INNER], SR * SP, post_s, "ssd_conv_x")
    bm = _conv_fwd(ps, c_b, s_cw[:, S_INNER:S_INNER + SG * SN], s_cb[:, S_INNER:S_INNER + SG * SN], SN,
                   post_s, "ssd_conv_b")
    cm = _conv_fwd(ps, c_c, s_cw[:, S_INNER + SG * SN:], s_cb[:, S_INNER + SG * SN:], SN, post_s,
                   "ssd_conv_c")
    nbs = t // (SCH * SSD_CPB)
    dtraw = _gates_to_rows(ps_small[:, :SH], SH, SCH, SSD_CPB)
    dtraw = dtraw.reshape(nbs, SG, SR, SSD_CPB, SCH).transpose(0, 3, 2, 1, 4).reshape(nbs, SSD_CPB * SR, SG, 1, SCH)
    yss, s_sall = _ssd_fwd(xs, bm, cm, dtraw, s_dtb, s_alog, s_d)
    y1 = _gate_fwd(_ssd_gate, yss, ps, 0, s_nw, "ssd_gate")
    x2 = _matmul(y1, w_so, "nn", "ssd_out_proj", add=x1)

    dx2, d_fw, loss_row = _final_loss(x2, fw, tgt, "final_loss")

    dy1 = _matmul(dx2, w_so, "nt", "ssd_out_dx", out_dtype=BF16, tn=2048)
    d_wso = _matmul(y1, dx2, "tn", "ssd_out_dw")
    dyss, dz_s, d_snw = _gate_bwd(_ssd_gate, yss, ps, 0, s_nw, dy1, True, "ssd_gate_bwd")
    dxs, dbm, dcm, ddtraw, d_sdtb, d_salog, d_sd = _ssd_bwd(xs, bm, cm, dtraw, s_dtb, s_alog, s_d, s_sall, dyss)
    dps, dwx, dbx = _conv_dpre(ps, c_x, s_cw[:, :S_INNER], s_cb[:, :S_INNER], SR * SP, post_s, dxs, "ssd_dpre_x",
                               G_CONV, 0)
    dps, dwb, dbb = _conv_dpre(ps, c_b, s_cw[:, S_INNER:S_INNER + SG * SN], s_cb[:, S_INNER:S_INNER + SG * SN],
                               SN, post_s, dbm, "ssd_dpre_b", G_CONV, S_INNER, dps)
    dps, dwc, dbc = _conv_dpre(ps, c_c, s_cw[:, S_INNER + SG * SN:], s_cb[:, S_INNER + SG * SN:], SN, post_s,
                               dcm, "ssd_dpre_c", G_CONV, S_INNER + SG * SN, dps)
    d_scw = jnp.concatenate([dwx, dwb, dwc], axis=1)
    d_scb = jnp.concatenate([dbx, dbb, dbc], axis=1)
    dxbc = _conv_t(dps, s_cw, "ssd_conv_t")
    ddt = ddtraw.reshape(nbs, SSD_CPB, SR, SG, SCH).transpose(0, 3, 2, 1, 4).reshape(nbs, SH, SSD_CPB, SCH)
    ddt = _rows_to_gates(ddt)
    dsm_s = jnp.concatenate([ddt, jnp.zeros((t, SMALL_W - SH), F32)], axis=1).astype(BF16)
    dhid1 = _matmul(dz_s, ws_main[:, :S_INNER], "nt", "ssd_in_dx_z")
    dhid1 = _matmul(dxbc, ws_main[:, S_INNER:], "nt", "ssd_in_dx_xbc", add=dhid1)
    dhid1 = _matmul(dsm_s, ws_small, "nt", "ssd_in_dx_dt", add=dhid1, tk=SMALL_W)
    d_wsi = jnp.concatenate([_matmul(hid1, dz_s, "tn", "ssd_in_dw_z"),
                             _matmul(hid1, dxbc, "tn", "ssd_in_dw_xbc"),
                             _matmul(hid1, dsm_s, "tn", "ssd_in_dw_dt", tn=SMALL_W)[:, :SH]], axis=1)
    dx1, d_nw1 = _rms_bwd(x1, nw1, dx2, dhid1, "rms1_bwd")

    dy0 = _matmul(dx1, w_go, "nt", "gdn_out_dx", out_dtype=BF16, tn=2048)
    d_wgo = _matmul(y0, dx1, "tn", "gdn_out_dw")
    do, dz_g, d_gnw = _gate_bwd(_gdn_gate, o, pg, G_CONV, g_nw, dy0, False, "gdn_gate_bwd")
    dq, dk, dv, dbraw, daraw, d_galog, d_gdtb = _gdn_bwd(q, k, v, braw, araw, g_alog, g_dtb, g_sall, g_tall, do)
    dpg, dwq, _ = _conv_dpre(pg, 0, g_cw[:, :G_QK], zero_b[:, :G_QK], GDK, post_q, dq, "gdn_dpre_q", G_CONV, 0)
    dpg, dwk, _ = _conv_dpre(pg, G_QK, g_cw[:, G_QK:2 * G_QK], zero_b[:, :G_QK], GDK, post_k, dk, "gdn_dpre_k",
                             G_CONV, G_QK, dpg)
    dpg, dwv, _ = _conv_dpre(pg, 2 * G_QK, g_cw[:, 2 * G_QK:], zero_b[:, :G_V], GDK, _silu, dv, "gdn_dpre_v",
                             G_CONV, 2 * G_QK, dpg)
    d_gcw = jnp.concatenate([dwq, dwk, dwv], axis=1)
    dqkv = _conv_t(dpg, g_cw, "gdn_conv_t")
    dsm_g = jnp.concatenate([_rows_to_gates(dbraw), _rows_to_gates(daraw),
                             jnp.zeros((t, SMALL_W - 2 * GH), F32)], axis=1).astype(BF16)
    dhid0 = _matmul(dqkv, wg_main[:, :G_CONV], "nt", "gdn_in_dx_qkv")
    dhid0 = _matmul(dz_g, wg_main[:, G_CONV:], "nt", "gdn_in_dx_z", add=dhid0)
    dhid0 = _matmul(dsm_g, wg_small, "nt", "gdn_in_dx_ba", add=dhid0, tk=SMALL_W)
    d_wgi = jnp.concatenate([_matmul(hid0, dqkv, "tn", "gdn_in_dw_qkv"),
                             _matmul(hid0, dz_g, "tn", "gdn_in_dw_z"),
                             _matmul(hid0, dsm_g, "tn", "gdn_in_dw_ba", tn=SMALL_W)[:, :2 * GH]], axis=1)
    dx0, d_nw0 = _rms_bwd(x0, nw0, dx1, dhid0, "rms0_bwd")

    def in_blocks(dw):
        return dw.reshape(D, N_CHIPS, IN_SHARD).transpose(1, 0, 2).reshape(N_CHIPS, 2, D // 2, IN_SHARD)

    def out_blocks(dw):
        return dw.reshape(N_CHIPS, 2, OUT_SHARD // 2, D)

    gs = [in_blocks(d_wgi), out_blocks(d_wgo), in_blocks(d_wsi), out_blocks(d_wso)]
    from_sib = _swap_halves(gs)
    hsum, hsum_bf = _sum_cores(gs, from_sib, cc.astype(jnp.int32).reshape(1))
    recv = _scatter_chips(hsum_bf)
    tots = _sum_chips(hsum, recv, chip.astype(jnp.int32).reshape(1))
    sib_tots = _swap_totals(tots)
    full = [jnp.concatenate([jnp.where(cc == 0, mine, sib), jnp.where(cc == 0, sib, mine)], axis=0)
            for mine, sib in zip(tots, sib_tots)]
    grads = dict(
        gdn_w_in=full[0].reshape(1, D, IN_SHARD), gdn_w_out=full[1].reshape(1, OUT_SHARD, D),
        ssd_w_in=full[2].reshape(1, D, IN_SHARD), ssd_w_out=full[3].reshape(1, OUT_SHARD, D))

    small_parts = [loss_row, jnp.concatenate([d_nw0, d_nw1], axis=0), d_gcw, d_galog, d_gdtb, d_gnw, d_scw, d_scb,
                   d_sdtb, d_salog, d_sd, d_snw, d_fw]
    sbuf, soffs = _pack_flat(small_parts, 8)
    ssum = _unpack_flat(_allreduce_small(sbuf), soffs)
    (loss_s, g_nw_all, g_gcw, g_galog, g_gdtb, g_gnw, g_scw, g_scb, g_sdtb, g_salog, g_sd, g_snw, g_fw) = ssum

    def my_cols(a, width):
        return lax.dynamic_slice_in_dim(a, chip * width, width, axis=a.ndim - 1)

    grads.update(
        norm_w=g_nw_all, gdn_conv_w=my_cols(g_gcw, 1024)[None], gdn_a_log=g_galog.reshape(1, GH),
        gdn_dt_bias=g_gdtb.reshape(1, GH), gdn_norm_w=g_gnw.reshape(1, GDK),
        ssd_conv_w=my_cols(g_scw, 1024)[None], ssd_conv_b=my_cols(g_scb, 1024),
        ssd_dt_bias=g_sdtb.reshape(SR, SG).T.reshape(1, SH), ssd_a_log=g_salog.reshape(SR, SG).T.reshape(1, SH),
        ssd_d=g_sd.reshape(SR, SG).T.reshape(1, SH),
        ssd_norm_w=my_cols(g_snw, 512), final_norm_w=g_fw.reshape(D))
    loss = loss_s[0, 0]

    big_names = ("gdn_w_in", "gdn_w_out", "ssd_w_in", "ssd_w_out")
    deltas, new_m, new_v = {}, {}, {}
    for n in big_names:
        deltas[n], new_m[n], new_v[n] = _adamw(ws[n], grads[n], ms[n], vs[n], "adamw_" + n)
    rest = [n for n in names if n not in big_names]
    packs = [_pack_flat([d[n] for n in rest], 8) for d in (ws, grads, ms, vs)]
    outs = _adamw(*[p[0] for p in packs], "adamw_small")
    for d, buf in zip((deltas, new_m, new_v), outs):
        for n, a in zip(rest, _unpack_flat(buf, packs[0][1])):
            d[n] = a

    grad_x = dx0.reshape(1, t, D)
    return (loss, grad_x, *[grads[n] for n in names], *[deltas[n] for n in names],
            *[new_m[n] for n in names], *[new_v[n] for n in names])
```

```python
import functools

import jax
import jax.numpy as jnp
from jax import lax
from jax.experimental import pallas as pl
from jax.experimental.pallas import tpu as pltpu

F32 = jnp.float32
BF16 = jnp.bfloat16
SDS = jax.ShapeDtypeStruct
MESH = pl.DeviceIdType.MESH

D = 1024
EPS = 1e-6
CONV_K = 4
N_CHIPS = 4
GH = 16
GHQ = 8
GDK = 128
GCH = 64
G_QK = 1024
G_V = 2048
G_CONV = 4096
G_MAIN = 6144
G_IN = 6176
SH = 32
SP = 64
SN = 128
SG = 8
SR = 4
SCH = 128
S_INNER = 2048
S_MAIN = 6144
S_IN = 6176
IN_SHARD = 1544
OUT_SHARD = 512
SMALL_W = 128

ADAM_LR = 0.001
ADAM_B1 = 0.9
ADAM_B2 = 0.999
ADAM_EPS = 1e-08
ADAM_WD = 0.01
ADAM_STEP = 10

VMEM_LIMIT = 56 * 1024 * 1024
BLOCK_ELEMS = 512 * 1024
FWD_COLS = 512
NEG = -1e30


def _pcall(body, **kw):
    return pl.pallas_call(body, **kw)


def _cp(sem=None, vmem=VMEM_LIMIT):
    return pltpu.CompilerParams(dimension_semantics=sem, vmem_limit_bytes=vmem)


@jax.custom_jvp
def _sigmoid(x):
    return 1.0 / (1.0 + jnp.exp(-x))


@_sigmoid.defjvp
def _sigmoid_jvp(primals, tangents):
    s = _sigmoid(primals[0])
    return s, tangents[0] * (s * (1.0 - s))


@jax.custom_jvp
def _silu(x):
    return x * _sigmoid(x)


@_silu.defjvp
def _silu_jvp(primals, tangents):
    x = primals[0]
    s = _sigmoid(x)
    return x * s, tangents[0] * (s * (1.0 + x * (1.0 - s)))


def _softplus(x):
    return jnp.maximum(x, 0.0) + jnp.log(1.0 + jnp.exp(-jnp.abs(x)))


def _rms(x, w):
    return x * lax.rsqrt(jnp.mean(x * x, axis=-1, keepdims=True) + EPS) * w


_DIMS = {"nn": (((2,), (1,)), ((0,), (0,))),
         "nt": (((2,), (2,)), ((0,), (0,))),
         "tn": (((1,), (1,)), ((0,), (0,)))}


def _bdot(a, b, spec):
    return lax.dot_general(a.astype(BF16), b.astype(BF16), _DIMS[spec], preferred_element_type=F32)


@functools.partial(jax.custom_vjp, nondiff_argnums=(2,))
def _bmm(a, b, spec):
    return _bdot(a, b, spec)


def _bmm_fwd(a, b, spec):
    return _bdot(a, b, spec), (a, b)


def _bmm_bwd(spec, res, g):
    a, b = res
    if spec == "nn":
        return _bdot(g, b, "nt"), _bdot(a, g, "tn")
    if spec == "nt":
        return _bdot(g, b, "nn"), _bdot(g, a, "tn")
    return _bdot(b, g, "nt"), _bdot(a, g, "nn")


_bmm.defvjp(_bmm_fwd, _bmm_bwd)


@jax.custom_vjp
def _tri_inv(n):
    t = -n
    p = n
    steps = (n.shape[-1] - 1).bit_length() - 1
    r = lax.broadcasted_iota(jnp.int32, n.shape, 1)
    c = lax.broadcasted_iota(jnp.int32, n.shape, 2)
    t = t + jnp.where(r == c, 1.0, 0.0)
    for _ in range(steps):
        p = _bdot(p, p, "nn")
        t = t + _bdot(t, p, "nn")
    return t


def _tri_inv_fwd(n):
    t = _tri_inv(n)
    return t, t


def _tri_inv_bwd(t, g):
    return (-_bdot(_bdot(t, g, "tn"), t, "nt"),)


_tri_inv.defvjp(_tri_inv_fwd, _tri_inv_bwd)


@jax.custom_vjp
def _tri_inv_known(n, t):
    del n
    return t


def _tri_inv_known_fwd(n, t):
    del n
    return t, t


def _tri_inv_known_bwd(t, g):
    return _tri_inv_bwd(t, g)[0], jnp.zeros_like(t)


_tri_inv_known.defvjp(_tri_inv_known_fwd, _tri_inv_known_bwd)


def _masks(c, lead=1):
    r = lax.broadcasted_iota(jnp.int32, (lead, c, c), 1)
    s = lax.broadcasted_iota(jnp.int32, (lead, c, c), 2)
    return r >= s, r > s, r == s, r <= s


def _row_to_col(row, eye):
    return jnp.sum(jnp.where(eye, row, 0.0), axis=2, keepdims=True)


def _gdn_chunk(q, k, v, braw, araw, alog, dtb, s, t_known=None, want_t=False):
    h = v.shape[0]
    c = v.shape[1]
    rep = h // q.shape[0]
    tril, strict, eye, triu = _masks(c)
    qq = jnp.broadcast_to(q[:, None], (q.shape[0], rep) + q.shape[1:]).reshape(v.shape)
    kk = jnp.broadcast_to(k[:, None], (k.shape[0], rep) + k.shape[1:]).reshape(v.shape)
    beta_row = _sigmoid(braw)
    g_row = -jnp.exp(alog) * _softplus(araw + dtb)
    beta_col = _row_to_col(beta_row, eye)
    g_col = _row_to_col(g_row, eye)
    gc_col = jnp.sum(jnp.where(tril, g_row, 0.0), axis=2, keepdims=True)
    gc_row = jnp.sum(jnp.where(triu, g_col, 0.0), axis=1, keepdims=True)
    gc_last = jnp.sum(g_row, axis=2, keepdims=True)
    lmat = jnp.exp(jnp.where(tril, gc_col - gc_row, NEG))
    kb = kk * beta_col
    vb = v * beta_col
    n = jnp.where(strict, _bmm(kb, kk, "nt") * lmat, 0.0)
    t = _tri_inv(n) if t_known is None else _tri_inv_known(n, t_known)
    e_col = jnp.exp(gc_col)
    u = _bmm(t, vb, "nn")
    w = _bmm(t, kb * e_col, "nn")
    attn = _bmm(qq, kk, "nt") * lmat
    q_dec = qq * e_col
    k_dec = kk * jnp.exp(gc_last - gc_col)
    v_new = u - _bmm(w, s, "nn")
    o = _bmm(q_dec, s, "nn") + _bmm(attn, v_new, "nn")
    s_new = s * jnp.exp(gc_last) + _bmm(k_dec, v_new, "tn")
    return (o, s_new, t) if want_t else (o, s_new)


def _ssd_chunk(xs, bm, cm, dtraw, dtb, alog, dskip, s):
    c = xs.shape[1]
    tril, _, eye, triu = _masks(c)
    lane = lax.broadcasted_iota(jnp.int32, (1, 1, SR * SP), 2)
    prow = lax.broadcasted_iota(jnp.int32, (1, SR * SP, 1), 1)
    cb = _bmm(cm, bm, "nt")
    cs = _bmm(cm, s, "nt")

    def per_head(vals, idx):
        out = vals[SR - 1]
        for r in reversed(range(SR - 1)):
            out = jnp.where(idx < (r + 1) * SP, vals[r], out)
        return out

    dt_cols, e_cols, lmats, dstates, declast = [], [], [], [], []
    for r in range(SR):
        dt_row = _softplus(dtraw[r] + dtb[r])
        adt_row = -jnp.exp(alog[r]) * dt_row
        dt_cols.append(_row_to_col(dt_row, eye))
        adt_col = _row_to_col(adt_row, eye)
        acs_col = jnp.sum(jnp.where(tril, adt_row, 0.0), axis=2, keepdims=True)
        acs_row = jnp.sum(jnp.where(triu, adt_col, 0.0), axis=1, keepdims=True)
        acs_last = jnp.sum(adt_row, axis=2, keepdims=True)
        lmats.append(jnp.exp(jnp.where(tril, acs_col - acs_row, NEG)))
        e_cols.append(jnp.exp(acs_col))
        dstates.append(jnp.exp(acs_last - acs_col))
        declast.append(jnp.exp(acs_last))
    xd = xs * per_head(dt_cols, lane)
    y = per_head([_bmm(cb * lmats[r], xd, "nn") for r in range(SR)], lane)
    states = per_head([_bmm(xd, bm * dstates[r], "tn") for r in range(SR)], prow)
    y = y + cs * per_head(e_cols, lane) + xs * per_head([dskip[r] for r in range(SR)], lane)
    s_new = s * per_head(declast, prow) + states
    return y, s_new


def _matmul(a, b, mode, name, out_dtype=F32, add=None, tm=1024, tn=1024, tk=2048):
    if mode == "nn":
        (m, k), n = a.shape, b.shape[1]
    elif mode == "nt":
        (m, k), n = a.shape, b.shape[0]
    else:
        (k, m), n = a.shape, b.shape[1]
    tm, tn, tk = min(tm, m), min(tn, n), min(tk, k)
    assert m % tm == 0 and n % tn == 0 and k % tk == 0, (name, m, n, k)
    nk = k // tk
    dims = {"nn": (((1,), (0,)), ((), ())), "nt": (((1,), (1,)), ((), ())),
            "tn": (((0,), (0,)), ((), ()))}[mode]
    a_spec = {"nn": pl.BlockSpec((tm, tk), lambda i, j, kk: (i, kk)),
              "nt": pl.BlockSpec((tm, tk), lambda i, j, kk: (i, kk)),
              "tn": pl.BlockSpec((tk, tm), lambda i, j, kk: (kk, i))}[mode]
    b_spec = {"nn": pl.BlockSpec((tk, tn), lambda i, j, kk: (kk, j)),
              "nt": pl.BlockSpec((tn, tk), lambda i, j, kk: (j, kk)),
              "tn": pl.BlockSpec((tk, tn), lambda i, j, kk: (kk, j))}[mode]
    o_spec = pl.BlockSpec((tm, tn), lambda i, j, kk: (i, j))
    has_add = add is not None

    def body(*refs):
        a_ref, b_ref = refs[:2]
        add_ref = refs[2] if has_add else None
        o_ref = refs[2 + has_add]
        part = lax.dot_general(a_ref[...].astype(BF16), b_ref[...].astype(BF16), dims,
                               preferred_element_type=F32)

        def finish(r):
            if has_add:
                r = r + add_ref[...].astype(F32)
            o_ref[...] = r.astype(o_ref.dtype)

        if nk == 1:
            finish(part)
            return
        acc_ref = refs[3 + has_add]
        kk = pl.program_id(2)

        @pl.when(kk == 0)
        def _():
            acc_ref[...] = part

        @pl.when(jnp.logical_and(kk > 0, kk < nk - 1))
        def _():
            acc_ref[...] += part

        @pl.when(kk == nk - 1)
        def _():
            finish(acc_ref[...] + part)

    ins = [a, b] + ([add] if has_add else [])
    in_specs = [a_spec, b_spec] + ([o_spec] if has_add else [])
    scratch = [] if nk == 1 else [pltpu.VMEM((tm, tn), F32)]
    return _pcall(body, grid=(m // tm, n // tn, nk), in_specs=in_specs, out_specs=o_spec,
                  out_shape=SDS((m, n), out_dtype), scratch_shapes=scratch,
                  compiler_params=_cp(("parallel", "parallel", "arbitrary")), name=name)(*ins)


def _ew(fn, ins, in_specs, out_shape, out_specs, grid, name):
    n_in = len(ins)

    def body(*refs):
        outs = fn(*[r[...] for r in refs[:n_in]])
        for r, o in zip(refs[n_in:], outs):
            r[...] = o.astype(r.dtype)

    return _pcall(body, grid=grid, in_specs=in_specs, out_specs=out_specs, out_shape=out_shape,
                  compiler_params=_cp(("arbitrary",) * len(grid)), name=name)(*ins)


def _ew_vjp(fn, ins, in_specs, cts, ct_specs, wrt, g_shape, g_specs, acc, grid, name):
    n_in, n_ct = len(ins), len(cts)

    def body(*refs):
        vals = [r[...].astype(F32) for r in refs[:n_in]]
        outs, vjp = jax.vjp(fn, *vals)
        g_all = vjp(tuple(r[...].astype(F32) for r in refs[n_in:n_in + n_ct]))
        for pos, (i, g_ref) in enumerate(zip(wrt, refs[n_in + n_ct:])):
            g = g_all[i]
            if pos in acc:
                first = functools.reduce(
                    jnp.logical_and, [pl.program_id(ax) == 0 for ax in range(acc[pos], len(grid))])

                @pl.when(first)
                def _():
                    g_ref[...] = jnp.zeros_like(g_ref)

                g_ref[...] += g.astype(g_ref.dtype)
            else:
                g_ref[...] = g.astype(g_ref.dtype)

    return _pcall(body, grid=grid, in_specs=list(in_specs) + list(ct_specs), out_specs=g_specs,
                  out_shape=g_shape, compiler_params=_cp(("arbitrary",) * len(grid)),
                  name=name)(*ins, *cts)


def _row_spec(tm, n):
    return pl.BlockSpec((tm, n), lambda i: (i, 0))


def _par_spec(n):
    return pl.BlockSpec((1, n), lambda i: (0, 0))


def _rms_fwd(x, w, name):
    t = x.shape[0]
    tm = min(t, 512)
    return _ew(lambda xv, wv: (_rms(xv, wv),), [x, w], [_row_spec(tm, D), _par_spec(D)],
               [SDS((t, D), BF16)], [_row_spec(tm, D)], (t // tm,), name)[0]


def _rms_bwd(x, w, dres, dhid, name):
    t = x.shape[0]
    tm = min(t, 512)
    return _ew_vjp(lambda xv, wv: (xv, _rms(xv, wv)), [x, w], [_row_spec(tm, D), _par_spec(D)],
                   [dres, dhid], [_row_spec(tm, D), _row_spec(tm, D)], (0, 1),
                   [SDS((t, D), F32), SDS((1, D), F32)], [_row_spec(tm, D), _par_spec(D)],
                   {1: 0}, (t // tm,), name)


def _final_loss(x, w, tgt, name):
    t = x.shape[0]
    tm = min(t, 512)

    def body(x_ref, w_ref, t_ref, dx_ref, dw_ref, l_ref):
        @pl.when(pl.program_id(0) == 0)
        def _():
            dw_ref[...] = jnp.zeros_like(dw_ref)
            l_ref[...] = jnp.zeros_like(l_ref)

        xv, wv = x_ref[...], w_ref[...]
        rstd = lax.rsqrt(jnp.mean(xv * xv, axis=-1, keepdims=True) + EPS)
        xh = xv * rstd
        err = xh * wv - t_ref[...]
        l_ref[...] += 0.5 * jnp.sum(jnp.mean(err * err, axis=-1, keepdims=True), axis=0, keepdims=True)
        dy = err * (1.0 / D)
        dw_ref[...] += jnp.sum(dy * xh, axis=0, keepdims=True)
        dxh = dy * wv
        dx_ref[...] = rstd * (dxh - xh * jnp.mean(dxh * xh, axis=-1, keepdims=True))

    return _pcall(body, grid=(t // tm,), in_specs=[_row_spec(tm, D), _par_spec(D), _row_spec(tm, D)],
                  out_specs=[_row_spec(tm, D), _par_spec(D), _par_spec(128)],
                  out_shape=[SDS((t, D), F32), SDS((1, D), F32), SDS((1, 128), F32)],
                  compiler_params=_cp(("arbitrary",)), name=name)(x, w, tgt)


def _conv_taps(ext, w_ref, tm, lo):
    n = ext.shape[0]
    acc = None
    for j in range(CONV_K):
        shift = (CONV_K - 1 - j) if lo else (n - (CONV_K - 1 - j)) % n
        rolled = pltpu.roll(ext, shift, 0) if shift else ext
        term = w_ref[pl.ds(j, 1), :] * rolled[lo:lo + tm]
        acc = term if acc is None else acc + term
    return acc


def _conv_pre_specs(tm, ct, col0):
    hb = tm // 8
    return [pl.BlockSpec((tm, ct), lambda j, i: (i, col0 + j)),
            pl.BlockSpec((8, ct), lambda j, i: (jnp.maximum(i * hb - 1, 0), col0 + j)),
            pl.BlockSpec((CONV_K, ct), lambda j, i: (0, j)),
            pl.BlockSpec((1, ct), lambda j, i: (0, j))]


def _conv_pre_value(x_ref, xh_ref, w_ref, b_ref, tm):
    halo = jnp.where(pl.program_id(1) > 0, xh_ref[...], 0.0)
    ext = jnp.concatenate([halo, x_ref[...]], axis=0)
    return _conv_taps(ext, w_ref, tm, 8) + b_ref[...], ext


def _conv_fwd(x, col0, w, b, ct, post, name):
    t = x.shape[0]
    tpb = FWD_COLS // ct
    wide = tpb * ct
    tm = min(t, BLOCK_ELEMS // wide)
    nt = w.shape[1] // ct
    assert col0 % wide == 0 and nt % tpb == 0, name

    def body(x_ref, xh_ref, w_ref, b_ref, o_ref):
        pre, _ = _conv_pre_value(x_ref, xh_ref, w_ref, b_ref, tm)
        for k in range(tpb):
            o_ref[k] = post(pre[:, k * ct:(k + 1) * ct])

    return _pcall(body, grid=(nt // tpb, t // tm), in_specs=_conv_pre_specs(tm, wide, col0 // wide),
                  out_specs=pl.BlockSpec((tpb, tm, ct), lambda j, i: (j, i, 0)),
                  out_shape=SDS((nt, t, ct), F32), compiler_params=_cp(("arbitrary", "arbitrary")),
                  name=name)(x, x, w, b)


def _conv_dpre(x, col0, w, b, ct, post, dout, name, c_total, c_off, into=None):
    t = x.shape[0]
    tm = min(t, BLOCK_ELEMS // ct)
    nt = w.shape[1] // ct
    chained = into is not None

    def body(*refs):
        x_ref, xh_ref, w_ref, b_ref, do_ref = refs[:5]
        dp_ref, dw_ref, db_ref = refs[5 + chained:]
        pre, ext = _conv_pre_value(x_ref, xh_ref, w_ref, b_ref, tm)
        _, vjp = jax.vjp(post, pre)
        dpre = vjp(do_ref[0])[0]
        dp_ref[...] = dpre

        @pl.when(pl.program_id(1) == 0)
        def _():
            dw_ref[...] = jnp.zeros_like(dw_ref)
            db_ref[...] = jnp.zeros_like(db_ref)

        for j in range(CONV_K):
            xs = (pltpu.roll(ext, CONV_K - 1 - j, 0) if j < CONV_K - 1 else ext)[8:8 + tm]
            dw_ref[pl.ds(j, 1), :] += jnp.sum(dpre * xs, axis=0, keepdims=True)
        db_ref[...] += jnp.sum(dpre, axis=0, keepdims=True)

    c = w.shape[1]
    o0 = c_off // ct
    return _pcall(body, grid=(nt, t // tm),
                  in_specs=_conv_pre_specs(tm, ct, col0 // ct)
                  + [pl.BlockSpec((1, tm, ct), lambda j, i: (j, i, 0))] + ([ANY] if chained else []),
                  out_specs=[pl.BlockSpec((tm, ct), lambda j, i: (i, o0 + j)),
                             pl.BlockSpec((CONV_K, ct), lambda j, i: (0, j)),
                             pl.BlockSpec((1, ct), lambda j, i: (0, j))],
                  out_shape=[SDS((t, c_total), F32), SDS((CONV_K, c), F32), SDS((1, c), F32)],
                  input_output_aliases={5: 0} if chained else {},
                  compiler_params=_cp(("arbitrary", "arbitrary")), name=name)(
                      x, x, w, b, dout, *([into] if chained else []))


def _conv_t(dpre, w, name):
    t, c = dpre.shape
    ct = min(c, 512)
    tm = min(t, BLOCK_ELEMS // ct)
    hb = tm // 8
    last = t // tm - 1

    def body(d_ref, dh_ref, w_ref, o_ref):
        halo = jnp.where(pl.program_id(1) < last, dh_ref[...], 0.0)
        ext = jnp.concatenate([d_ref[...], halo], axis=0)
        o_ref[...] = _conv_taps(ext, w_ref, tm, 0).astype(o_ref.dtype)

    return _pcall(body, grid=(c // ct, t // tm),
                  in_specs=[pl.BlockSpec((tm, ct), lambda j, i: (i, j)),
                            pl.BlockSpec((8, ct), lambda j, i: (jnp.minimum((i + 1) * hb, t // 8 - 1), j)),
                            pl.BlockSpec((CONV_K, ct), lambda j, i: (0, j))],
                  out_specs=pl.BlockSpec((tm, ct), lambda j, i: (i, j)),
                  out_shape=SDS((t, c), BF16), compiler_params=_cp(("arbitrary", "arbitrary")),
                  name=name)(dpre, dpre, w)


def _l2norm_scaled(scale):
    def post(pre):
        a = _silu(pre)
        return a * lax.rsqrt(jnp.sum(a * a, axis=-1, keepdims=True) + EPS) * scale
    return post


GDN_HB = 16
GDN_CPB = 4
SSD_GB = 8
SSD_CPB = 1


def _gdn_specs(nb, rev):
    hb, cpb, tc = GDN_HB, GDN_CPB, GDN_CPB * GCH
    blk = (lambda n: nb - 1 - n) if rev else (lambda n: n)
    seq = lambda h: pl.BlockSpec((h, tc, GDK), lambda g, n: (g, blk(n), 0))
    gate = pl.BlockSpec((1, hb, cpb, GCH), lambda g, n: (blk(n), g, 0, 0))
    par = pl.BlockSpec((hb, 1, 1), lambda g, n: (g, 0, 0))
    state = pl.BlockSpec((cpb, hb, GDK, GDK), lambda g, n: (blk(n), g, 0, 0))
    tinv = pl.BlockSpec((cpb, hb, GCH, GCH), lambda g, n: (blk(n), g, 0, 0))
    return seq, gate, par, state, tinv


def _gdn_fwd(q, k, v, braw, araw, alog, dtb):
    t = v.shape[1]
    hb, cpb = GDN_HB, GDN_CPB
    nb = t // (cpb * GCH)
    seq, gate, par, state, tinv = _gdn_specs(nb, False)

    def body(q_ref, k_ref, v_ref, b_ref, a_ref, al_ref, dt_ref, o_ref, sall_ref, tall_ref, s_ref):
        @pl.when(pl.program_id(1) == 0)
        def _():
            s_ref[...] = jnp.zeros_like(s_ref)

        s = s_ref[...]
        for c in range(cpb):
            rows = pl.ds(c * GCH, GCH)
            sall_ref[c] = s
            o, s, tmat = _gdn_chunk(q_ref[:, rows, :], k_ref[:, rows, :], v_ref[:, rows, :],
                                    b_ref[0, :, pl.ds(c, 1), :], a_ref[0, :, pl.ds(c, 1), :],
                                    al_ref[...], dt_ref[...], s, want_t=True)
            o_ref[:, rows, :] = o
            tall_ref[c] = tmat.astype(BF16)
        s_ref[...] = s

    return _pcall(body, grid=(GH // hb, nb),
                  in_specs=[seq(hb // 2), seq(hb // 2), seq(hb), gate, gate, par, par],
                  out_specs=[seq(hb), state, tinv],
                  out_shape=[SDS((GH, t, GDK), F32), SDS((t // GCH, GH, GDK, GDK), F32),
                             SDS((t // GCH, GH, GCH, GCH), BF16)],
                  scratch_shapes=[pltpu.VMEM((hb, GDK, GDK), F32)],
                  compiler_params=_cp(("arbitrary", "arbitrary")), name="gdn_chunk_fwd")(
                      q, k, v, braw, araw, alog, dtb)


def _gdn_bwd(q, k, v, braw, araw, alog, dtb, sall, tall, do):
    t = v.shape[1]
    hb, cpb = GDN_HB, GDN_CPB
    nb = t // (cpb * GCH)
    seq, gate, par, state, tinv = _gdn_specs(nb, True)

    def body(q_ref, k_ref, v_ref, b_ref, a_ref, al_ref, dt_ref, sall_ref, tall_ref, do_ref,
             dq_ref, dk_ref, dv_ref, db_ref, da_ref, dal_ref, ddt_ref, ds_ref):
        @pl.when(pl.program_id(1) == 0)
        def _():
            ds_ref[...] = jnp.zeros_like(ds_ref)
            dal_ref[...] = jnp.zeros_like(dal_ref)
            ddt_ref[...] = jnp.zeros_like(ddt_ref)

        ds = ds_ref[...]
        for c in reversed(range(cpb)):
            rows = pl.ds(c * GCH, GCH)
            fn = functools.partial(_gdn_chunk, t_known=tall_ref[c].astype(F32))
            _, vjp = jax.vjp(fn, q_ref[:, rows, :], k_ref[:, rows, :], v_ref[:, rows, :],
                             b_ref[0, :, pl.ds(c, 1), :], a_ref[0, :, pl.ds(c, 1), :],
                             al_ref[...], dt_ref[...], sall_ref[c])
            dq, dk, dv, db, da, dal, ddt, ds = vjp((do_ref[:, rows, :], ds))
            dq_ref[:, rows, :] = dq
            dk_ref[:, rows, :] = dk
            dv_ref[:, rows, :] = dv
            db_ref[0, :, pl.ds(c, 1), :] = db
            da_ref[0, :, pl.ds(c, 1), :] = da
            dal_ref[...] += dal
            ddt_ref[...] += ddt
        ds_ref[...] = ds

    return _pcall(body, grid=(GH // hb, nb),
                  in_specs=[seq(hb // 2), seq(hb // 2), seq(hb), gate, gate, par, par, state, tinv, seq(hb)],
                  out_specs=[seq(hb // 2), seq(hb // 2), seq(hb), gate, gate, par, par],
                  out_shape=[SDS(q.shape, F32), SDS(k.shape, F32), SDS(v.shape, F32),
                             SDS(braw.shape, F32), SDS(araw.shape, F32),
                             SDS((GH, 1, 1), F32), SDS((GH, 1, 1), F32)],
                  scratch_shapes=[pltpu.VMEM((hb, GDK, GDK), F32)],
                  compiler_params=_cp(("arbitrary", "arbitrary")), name="gdn_chunk_bwd")(
                      q, k, v, braw, araw, alog, dtb, sall, tall, do)


def _ssd_specs(nb, rev):
    gb, cpb, tc = SSD_GB, SSD_CPB, SSD_CPB * SCH
    blk = (lambda n: nb - 1 - n) if rev else (lambda n: n)
    seq = lambda w: pl.BlockSpec((gb, tc, w), lambda g, n: (g, blk(n), 0))
    gate = pl.BlockSpec((1, cpb * SR, gb, 1, SCH), lambda g, n: (blk(n), 0, g, 0, 0))
    par = pl.BlockSpec((SR, gb, 1, 1), lambda g, n: (0, g, 0, 0))
    state = pl.BlockSpec((cpb, gb, SR * SP, SN), lambda g, n: (blk(n), g, 0, 0))
    return seq, gate, par, state


def _ssd_fwd(xs, bm, cm, dtraw, dtb, alog, dskip):
    t = xs.shape[1]
    gb, cpb = SSD_GB, SSD_CPB
    nb = t // (cpb * SCH)
    seq, gate, par, state = _ssd_specs(nb, False)

    def body(x_ref, b_ref, c_ref, dt_ref, dtb_ref, al_ref, dk_ref, y_ref, sall_ref, s_ref):
        @pl.when(pl.program_id(1) == 0)
        def _():
            s_ref[...] = jnp.zeros_like(s_ref)

        s = s_ref[...]
        for c in range(cpb):
            rows = pl.ds(c * SCH, SCH)
            sall_ref[c] = s
            y, s = _ssd_chunk(x_ref[:, rows, :], b_ref[:, rows, :], c_ref[:, rows, :],
                              dt_ref[0, pl.ds(c * SR, SR)], dtb_ref[...], al_ref[...],
                              dk_ref[...], s)
            y_ref[:, rows, :] = y
        s_ref[...] = s

    return _pcall(body, grid=(SG // gb, nb),
                  in_specs=[seq(SR * SP), seq(SN), seq(SN), gate, par, par, par],
                  out_specs=[seq(SR * SP), state],
                  out_shape=[SDS((SG, t, SR * SP), F32), SDS((t // SCH, SG, SR * SP, SN), F32)],
                  scratch_shapes=[pltpu.VMEM((gb, SR * SP, SN), F32)],
                  compiler_params=_cp(("arbitrary", "arbitrary")), name="ssd_chunk_fwd")(
                      xs, bm, cm, dtraw, dtb, alog, dskip)


def _ssd_bwd(xs, bm, cm, dtraw, dtb, alog, dskip, sall, dy):
    t = xs.shape[1]
    gb, cpb = SSD_GB, SSD_CPB
    nb = t // (cpb * SCH)
    seq, gate, par, state = _ssd_specs(nb, True)

    def body(x_ref, b_ref, c_ref, dt_ref, dtb_ref, al_ref, dk_ref, sall_ref, dy_ref,
             dx_ref, dbm_ref, dcm_ref, ddt_ref, ddtb_ref, dal_ref, ddk_ref, ds_ref):
        @pl.when(pl.program_id(1) == 0)
        def _():
            ds_ref[...] = jnp.zeros_like(ds_ref)
            ddtb_ref[...] = jnp.zeros_like(ddtb_ref)
            dal_ref[...] = jnp.zeros_like(dal_ref)
            ddk_ref[...] = jnp.zeros_like(ddk_ref)

        ds = ds_ref[...]
        for c in reversed(range(cpb)):
            rows = pl.ds(c * SCH, SCH)
            _, vjp = jax.vjp(_ssd_chunk, x_ref[:, rows, :], b_ref[:, rows, :], c_ref[:, rows, :],
                             dt_ref[0, pl.ds(c * SR, SR)], dtb_ref[...], al_ref[...],
                             dk_ref[...], sall_ref[c])
            dx, dbm, dcm, ddt, ddtb, dal, ddk, ds = vjp((dy_ref[:, rows, :], ds))
            dx_ref[:, rows, :] = dx
            dbm_ref[:, rows, :] = dbm
            dcm_ref[:, rows, :] = dcm
            ddt_ref[0, pl.ds(c * SR, SR)] = ddt
            ddtb_ref[...] += ddtb
            dal_ref[...] += dal
            ddk_ref[...] += ddk
        ds_ref[...] = ds

    return _pcall(body, grid=(SG // gb, nb),
                  in_specs=[seq(SR * SP), seq(SN), seq(SN), gate, par, par, par, state, seq(SR * SP)],
                  out_specs=[seq(SR * SP), seq(SN), seq(SN), gate, par, par, par],
                  out_shape=[SDS(xs.shape, F32), SDS(bm.shape, F32), SDS(cm.shape, F32),
                             SDS(dtraw.shape, F32), SDS((SR, SG, 1, 1), F32), SDS((SR, SG, 1, 1), F32),
                             SDS((SR, SG, 1, 1), F32)],
                  scratch_shapes=[pltpu.VMEM((gb, SR * SP, SN), F32)],
                  compiler_params=_cp(("arbitrary", "arbitrary")), name="ssd_chunk_bwd")(
                      xs, bm, cm, dtraw, dtb, alog, dskip, sall, dy)


def _gate_specs(tm, ct, zcol0, per_tile_w):
    z0 = zcol0 // ct
    return [pl.BlockSpec((1, tm, ct), lambda i, j: (j, i, 0)),
            pl.BlockSpec((tm, ct), lambda i, j: (i, z0 + j)),
            pl.BlockSpec((1, ct), (lambda i, j: (0, j)) if per_tile_w else (lambda i, j: (0, 0)))]


def _gdn_gate(o, z, w):
    return (_rms(o[0], w) * _silu(z),)


def _ssd_gate(y, z, w):
    return (_rms(y[0] * _silu(z), w),)


def _gate_fwd(fn, o, proj, zcol0, w, name):
    nt, t, ct = o.shape
    tpb = FWD_COLS // ct
    wide = tpb * ct
    tm = min(t, BLOCK_ELEMS // wide)
    per_tile_w = w.shape[1] > ct
    assert zcol0 % wide == 0 and nt % tpb == 0, name
    z0 = zcol0 // wide
    specs = [pl.BlockSpec((tpb, tm, ct), lambda i, j: (j, i, 0)),
             pl.BlockSpec((tm, wide), lambda i, j: (i, z0 + j)),
             pl.BlockSpec((1, wide), lambda i, j: (0, j)) if per_tile_w
             else pl.BlockSpec((1, ct), lambda i, j: (0, 0))]

    def fn_wide(ov, zv, wv):
        outs = []
        for k in range(tpb):
            cols = slice(k * ct, (k + 1) * ct)
            outs.append(fn(ov[k:k + 1], zv[:, cols], wv[:, cols] if per_tile_w else wv)[0])
        return (jnp.concatenate(outs, axis=1),)

    return _ew(fn_wide, [o, proj, w], specs, [SDS((t, nt * ct), BF16)],
               [pl.BlockSpec((tm, wide), lambda i, j: (i, j))], (t // tm, nt // tpb), name)[0]


def _gate_bwd(fn, o, proj, zcol0, w, dy, wacc, name):
    nt, t, ct = o.shape
    tm = min(t, BLOCK_ELEMS // ct)
    specs = _gate_specs(tm, ct, zcol0, wacc)
    out_spec = pl.BlockSpec((tm, ct), lambda i, j: (i, j))
    if wacc:
        flip = lambda s: pl.BlockSpec(s.block_shape, lambda j, i, f=s.index_map: f(i, j))
        specs = [flip(s) for s in specs]
        out_spec = flip(out_spec)
        grid, acc = (nt, t // tm), {2: 1}
    else:
        grid, acc = (t // tm, nt), {2: 0}
    return _ew_vjp(fn, [o, proj, w], specs, [dy], [out_spec], (0, 1, 2),
                   [SDS(o.shape, F32), SDS((t, nt * ct), BF16), SDS(w.shape, F32)],
                   [specs[0], out_spec, specs[2]], acc, grid, name)


def _adamw_math(w, g, m, v):
    m = ADAM_B1 * m + (1.0 - ADAM_B1) * g
    v = ADAM_B2 * v + (1.0 - ADAM_B2) * jnp.square(g)
    m_hat = m / (1.0 - ADAM_B1 ** ADAM_STEP)
    v_hat = v / (1.0 - ADAM_B2 ** ADAM_STEP)
    delta = -ADAM_LR * (m_hat / (jnp.sqrt(v_hat) + ADAM_EPS) + ADAM_WD * w)
    return delta, m, v


def _adamw(w, g, m, v, name):
    shape = w.shape
    w2, g2, m2, v2 = [a.reshape(-1, shape[-1]) for a in (w, g, m, v)]
    r, c = w2.shape
    tr = 256 if r % 256 == 0 else r
    spec = pl.BlockSpec((tr, c), lambda i: (i, 0))
    outs = _ew(_adamw_math, [w2, g2, m2, v2], [spec] * 4, [SDS((r, c), F32)] * 3, [spec] * 3,
               (r // tr,), name)
    return [o.reshape(shape) for o in outs]


def _coords():
    return lax.axis_index("x"), lax.axis_index("y"), lax.axis_index("c")


def _other_chips(x, y):
    return [(1 - x, y), (x, 1 - y), (1 - x, 1 - y)]


ANY = pl.BlockSpec(memory_space=pl.ANY)


def _rcopy(src, dst, send_sems, recv_sems, k, to):
    return pltpu.make_async_remote_copy(src_ref=src, dst_ref=dst, send_sem=send_sems.at[k],
                                        recv_sem=recv_sems.at[k], device_id=to, device_id_type=MESH)


def _gather_weights(bigs, small):
    n = len(bigs)

    def body(*refs):
        in_refs, small_ref = refs[:n], refs[n]
        out_refs, osmall_ref = refs[n + 1:2 * n + 1], refs[2 * n + 1]
        send_sems, recv_sems = refs[2 * n + 2:]
        x, y, c = _coords()
        me = 2 * x + y
        sibling = (x, y, 1 - c)
        chips = _other_chips(x, y)
        halves = [b.shape[0] // 2 for b in bigs]
        mine = [pl.ds(c * h, h) for h in halves]
        theirs = [pl.ds((1 - c) * h, h) for h in halves]
        rc = functools.partial(_rcopy, send_sems=send_sems, recv_sems=recv_sems)

        first = []
        for a in range(n):
            for j, (px, py) in enumerate(chips):
                first.append(rc(in_refs[a].at[mine[a]], out_refs[a].at[me, mine[a]], k=3 * a + j, to=(px, py, c)))
        for j, (px, py) in enumerate(chips):
            first.append(rc(small_ref, osmall_ref.at[me], k=6 * n + j, to=(px, py, c)))
        for cp in first:
            cp.start()
        passed = []
        for a in range(n):
            for j, (px, py) in enumerate(chips):
                landed = out_refs[a].at[2 * px + py, mine[a]]
                rc(landed, landed, k=3 * a + j, to=(px, py, c)).wait_recv()
                fw = rc(landed, landed, k=3 * n + 3 * a + j, to=sibling)
                fw.start()
                passed.append(fw)
        for a in range(n):
            for j, (px, py) in enumerate(chips):
                landed = out_refs[a].at[2 * px + py, theirs[a]]
                rc(landed, landed, k=3 * n + 3 * a + j, to=sibling).wait_recv()
        for j, (px, py) in enumerate(chips):
            rc(small_ref, osmall_ref.at[2 * px + py], k=6 * n + j, to=(px, py, c)).wait_recv()
        for cp in first + passed:
            cp.wait_send()

    outs = _pcall(body, in_specs=[ANY] * (n + 1), out_specs=[ANY] * (n + 1),
                  out_shape=[SDS((N_CHIPS,) + b.shape, b.dtype) for b in bigs]
                  + [SDS((N_CHIPS,) + small.shape, small.dtype)],
                  scratch_shapes=[pltpu.SemaphoreType.DMA((6 * n + 3,)), pltpu.SemaphoreType.DMA((6 * n + 3,))],
                  name="gather_weights")(*bigs, small)
    me = 2 * lax.axis_index("x") + lax.axis_index("y")
    return [lax.dynamic_update_index_in_dim(o, own, me, 0) for o, own in zip(outs, list(bigs) + [small])]


def _swap_halves(gs):
    n = len(gs)

    def body(*refs):
        send_sems, recv_sems = refs[2 * n:]
        x, y, c = _coords()
        cps = [_rcopy(refs[a].at[:, 1 - c], refs[n + a], send_sems, recv_sems, a, (x, y, 1 - c))
               for a in range(n)]
        for cp in cps:
            cp.start()
        for cp in cps:
            cp.wait()

    return _pcall(body, in_specs=[ANY] * n, out_specs=[ANY] * n,
                  out_shape=[SDS((N_CHIPS,) + g.shape[2:], g.dtype) for g in gs],
                  scratch_shapes=[pltpu.SemaphoreType.DMA((n,)), pltpu.SemaphoreType.DMA((n,))],
                  name="swap_halves")(*gs)


def _sum_cores(gs, rs, half_idx):
    n = len(gs)
    ns = 2
    in_specs, out_specs, out_shape = [], [], []
    for g in gs:
        _, _, h, w = g.shape
        in_specs.append(pl.BlockSpec((1, 1, h // ns, w), lambda b, i, c_ref: (b, c_ref[0], i, 0)))
    for g in gs:
        _, _, h, w = g.shape
        spec = pl.BlockSpec((1, h // ns, w), lambda b, i, c_ref: (b, i, 0))
        in_specs.append(spec)
        out_specs += [spec, spec]
        out_shape += [SDS((N_CHIPS, h, w), F32), SDS((N_CHIPS, h, w), BF16)]

    def body(c_ref, *refs):
        del c_ref
        for a in range(n):
            tot = refs[a][0] + refs[n + a][...]
            refs[2 * n + 2 * a][...] = tot
            refs[2 * n + 2 * a + 1][...] = tot.astype(BF16)

    outs = _pcall(body, grid_spec=pltpu.PrefetchScalarGridSpec(
        num_scalar_prefetch=1, grid=(N_CHIPS, ns), in_specs=in_specs, out_specs=out_specs),
        out_shape=out_shape, compiler_params=_cp(("arbitrary", "arbitrary")), name="sum_cores")(
            half_idx, *gs, *rs)
    return outs[0::2], outs[1::2]


def _scatter_chips(hs):
    n = len(hs)

    def body(*refs):
        send_sems, recv_sems = refs[2 * n:]
        x, y, c = _coords()
        cps = []
        for a in range(n):
            for j, (px, py) in enumerate(_other_chips(x, y)):
                cps.append(_rcopy(refs[a].at[2 * px + py], refs[n + a].at[j], send_sems, recv_sems,
                                  3 * a + j, (px, py, c)))
        for cp in cps:
            cp.start()
        for cp in cps:
            cp.wait()

    return _pcall(body, in_specs=[ANY] * n, out_specs=[ANY] * n,
                  out_shape=[SDS((3,) + h.shape[1:], h.dtype) for h in hs],
                  scratch_shapes=[pltpu.SemaphoreType.DMA((3 * n,)), pltpu.SemaphoreType.DMA((3 * n,))],
                  name="scatter_chips")(*hs)


def _sum_chips(hs, xs, chip_idx):
    n = len(hs)
    ns = 2
    in_specs, out_specs, out_shape = [], [], []
    for h_arr in hs:
        _, h, w = h_arr.shape
        in_specs.append(pl.BlockSpec((1, h // ns, w), lambda i, c_ref: (c_ref[0], i, 0)))
    for h_arr in hs:
        _, h, w = h_arr.shape
        in_specs.append(pl.BlockSpec((3, h // ns, w), lambda i, c_ref: (0, i, 0)))
        out_specs.append(pl.BlockSpec((h // ns, w), lambda i, c_ref: (i, 0)))
        out_shape.append(SDS((h, w), F32))

    def body(c_ref, *refs):
        del c_ref
        for a in range(n):
            x_ref = refs[n + a]
            refs[2 * n + a][...] = (refs[a][0] + x_ref[0].astype(F32) + x_ref[1].astype(F32)
                                    + x_ref[2].astype(F32))

    return _pcall(body, grid_spec=pltpu.PrefetchScalarGridSpec(
        num_scalar_prefetch=1, grid=(ns,), in_specs=in_specs, out_specs=out_specs),
        out_shape=out_shape, compiler_params=_cp(("arbitrary",)), name="sum_chips")(chip_idx, *hs, *xs)


def _swap_totals(tots):
    n = len(tots)

    def body(*refs):
        send_sems, recv_sems = refs[2 * n:]
        x, y, c = _coords()
        cps = [_rcopy(refs[a], refs[n + a], send_sems, recv_sems, a, (x, y, 1 - c)) for a in range(n)]
        for cp in cps:
            cp.start()
        for cp in cps:
            cp.wait()

    return _pcall(body, in_specs=[ANY] * n, out_specs=[ANY] * n,
                  out_shape=[SDS(t.shape, t.dtype) for t in tots],
                  scratch_shapes=[pltpu.SemaphoreType.DMA((n,)), pltpu.SemaphoreType.DMA((n,))],
                  name="swap_totals")(*tots)


def _allreduce_small(buf):
    rows = buf.shape[0]

    def body(b_ref, o_ref, g_ref, send_sems, recv_sems):
        x, y, c = _coords()
        me = 4 * x + 2 * y + c
        g_ref[me] = b_ref[...]
        cps = []
        for k in range(1, 8):
            px = 1 - x if k & 4 else x
            py = 1 - y if k & 2 else y
            pc = 1 - c if k & 1 else c
            cps.append(pltpu.make_async_remote_copy(
                src_ref=b_ref, dst_ref=g_ref.at[me], send_sem=send_sems.at[k - 1],
                recv_sem=recv_sems.at[k - 1], device_id=(px, py, pc), device_id_type=MESH))
        for cp in cps:
            cp.start()
        for cp in cps:
            cp.wait()
        acc = g_ref[0]
        for d in range(1, 8):
            acc = acc + g_ref[d]
        o_ref[...] = acc

    vm = pl.BlockSpec(memory_space=pltpu.VMEM)
    return _pcall(body, in_specs=[vm], out_specs=vm, out_shape=SDS(buf.shape, F32),
                  scratch_shapes=[pltpu.VMEM((8, rows, 128), F32), pltpu.SemaphoreType.DMA((7,)),
                                  pltpu.SemaphoreType.DMA((7,))],
                  compiler_params=pltpu.CompilerParams(vmem_limit_bytes=VMEM_LIMIT),
                  name="allreduce_small")(buf)


def _pad_rows(a, mult):
    r = (-a.shape[0]) % mult
    return a if r == 0 else jnp.concatenate([a, jnp.zeros((r, a.shape[1]), a.dtype)], axis=0)


def _pack_flat(parts, mult):
    rows, offs, r0 = [], [], 0
    for p in parts:
        f = p.reshape(-1)
        pad = (-f.shape[0]) % 128
        if pad:
            f = jnp.concatenate([f, jnp.zeros((pad,), f.dtype)])
        f = f.reshape(-1, 128)
        rows.append(f)
        offs.append((r0, p.shape))
        r0 += f.shape[0]
    return _pad_rows(jnp.concatenate(rows, axis=0), mult), offs


def _unpack_flat(buf, offs):
    out = []
    for r0, shape in offs:
        n = 1
        for s in shape:
            n *= s
        nr = -(-n // 128)
        out.append(buf[r0:r0 + nr].reshape(-1)[:n].reshape(shape))
    return out


def _gates_to_rows(a, heads, chunk, cpb):
    t = a.shape[0]
    return a.reshape(t // (chunk * cpb), cpb, chunk, heads).transpose(0, 3, 1, 2)


def _rows_to_gates(a):
    nb, heads, cpb, chunk = a.shape
    return a.transpose(0, 2, 3, 1).reshape(nb * cpb * chunk, heads)


def kernel(x, norm_w, gdn_w_in, gdn_conv_w, gdn_a_log, gdn_dt_bias, gdn_norm_w, gdn_w_out, ssd_w_in, ssd_conv_w, ssd_conv_b, ssd_dt_bias, ssd_a_log, ssd_d, ssd_norm_w, ssd_w_out, final_norm_w, loss_target, m_norm_w, m_gdn_w_in, m_gdn_conv_w, m_gdn_a_log, m_gdn_dt_bias, m_gdn_norm_w, m_gdn_w_out, m_ssd_w_in, m_ssd_conv_w, m_ssd_conv_b, m_ssd_dt_bias, m_ssd_a_log, m_ssd_d, m_ssd_norm_w, m_ssd_w_out, m_final_norm_w, v_norm_w, v_gdn_w_in, v_gdn_conv_w, v_gdn_a_log, v_gdn_dt_bias, v_gdn_norm_w, v_gdn_w_out, v_ssd_w_in, v_ssd_conv_w, v_ssd_conv_b, v_ssd_dt_bias, v_ssd_a_log, v_ssd_d, v_ssd_norm_w, v_ssd_w_out, v_final_norm_w):
    ws = dict(norm_w=norm_w, gdn_w_in=gdn_w_in, gdn_conv_w=gdn_conv_w, gdn_a_log=gdn_a_log,
              gdn_dt_bias=gdn_dt_bias, gdn_norm_w=gdn_norm_w, gdn_w_out=gdn_w_out, ssd_w_in=ssd_w_in,
              ssd_conv_w=ssd_conv_w, ssd_conv_b=ssd_conv_b, ssd_dt_bias=ssd_dt_bias,
              ssd_a_log=ssd_a_log, ssd_d=ssd_d, ssd_norm_w=ssd_norm_w, ssd_w_out=ssd_w_out,
              final_norm_w=final_norm_w)
    ms = dict(norm_w=m_norm_w, gdn_w_in=m_gdn_w_in, gdn_conv_w=m_gdn_conv_w, gdn_a_log=m_gdn_a_log,
              gdn_dt_bias=m_gdn_dt_bias, gdn_norm_w=m_gdn_norm_w, gdn_w_out=m_gdn_w_out,
              ssd_w_in=m_ssd_w_in, ssd_conv_w=m_ssd_conv_w, ssd_conv_b=m_ssd_conv_b,
              ssd_dt_bias=m_ssd_dt_bias, ssd_a_log=m_ssd_a_log, ssd_d=m_ssd_d,
              ssd_norm_w=m_ssd_norm_w, ssd_w_out=m_ssd_w_out, final_norm_w=m_final_norm_w)
    vs = dict(norm_w=v_norm_w, gdn_w_in=v_gdn_w_in, gdn_conv_w=v_gdn_conv_w, gdn_a_log=v_gdn_a_log,
              gdn_dt_bias=v_gdn_dt_bias, gdn_norm_w=v_gdn_norm_w, gdn_w_out=v_gdn_w_out,
              ssd_w_in=v_ssd_w_in, ssd_conv_w=v_ssd_conv_w, ssd_conv_b=v_ssd_conv_b,
              ssd_dt_bias=v_ssd_dt_bias, ssd_a_log=v_ssd_a_log, ssd_d=v_ssd_d,
              ssd_norm_w=v_ssd_norm_w, ssd_w_out=v_ssd_w_out, final_norm_w=v_final_norm_w)
    names = list(ws)
    cx, cy, cc = _coords()
    chip = 2 * cx + cy
    t = x.shape[1]
    x0 = x.reshape(t, D)
    tgt = loss_target.reshape(t, D)

    bigs = [p[0].astype(BF16) for p in (gdn_w_in, gdn_w_out, ssd_w_in, ssd_w_out)]
    small, small_offs = _pack_flat([gdn_conv_w[0], ssd_conv_w[0], ssd_conv_b[0], ssd_norm_w[0]], 8)
    a_gi, a_go, a_si, a_so, gsmall = _gather_weights(bigs, small)
    w_gi = jnp.concatenate([a_gi[b] for b in range(4)], axis=1)
    w_si = jnp.concatenate([a_si[b] for b in range(4)], axis=1)
    w_go = a_go.reshape(4 * OUT_SHARD, D)
    w_so = a_so.reshape(4 * OUT_SHARD, D)
    sm = [_unpack_flat(gsmall[b], small_offs) for b in range(4)]
    g_cw = jnp.concatenate([sm[b][0] for b in range(4)], axis=1)
    s_cw = jnp.concatenate([sm[b][1] for b in range(4)], axis=1)
    s_cb = jnp.concatenate([sm[b][2] for b in range(4)], axis=0)[None]
    s_nw = jnp.concatenate([sm[b][3] for b in range(4)], axis=0)[None]

    def pad_small(w):
        return jnp.concatenate([w, jnp.zeros((D, SMALL_W - w.shape[1]), w.dtype)], axis=1)

    wg_main, wg_small = w_gi[:, :G_MAIN], pad_small(w_gi[:, G_MAIN:])
    ws_main, ws_small = w_si[:, :S_MAIN], pad_small(w_si[:, S_MAIN:])
    zero_b = jnp.zeros((1, G_CONV), F32)
    nw0, nw1 = norm_w[0:1], norm_w[1:2]
    fw = final_norm_w[None]
    g_alog = gdn_a_log.reshape(GH, 1, 1)
    g_dtb = gdn_dt_bias.reshape(GH, 1, 1)
    g_nw = gdn_norm_w.reshape(1, GDK)
    s_dtb = ssd_dt_bias.reshape(SG, SR).T.reshape(SR, SG, 1, 1)
    s_alog = ssd_a_log.reshape(SG, SR).T.reshape(SR, SG, 1, 1)
    s_d = ssd_d.reshape(SG, SR).T.reshape(SR, SG, 1, 1)

    hid0 = _rms_fwd(x0, nw0, "rms0")
    pg = _matmul(hid0, wg_main, "nn", "gdn_in_proj", tn=G_MAIN // 4)
    pg_small = _matmul(hid0, wg_small, "nn", "gdn_in_proj_small", tn=SMALL_W)
    post_q = _l2norm_scaled(GDK ** -0.5)
    post_k = _l2norm_scaled(1.0)
    q = _conv_fwd(pg, 0, g_cw[:, :G_QK], zero_b[:, :G_QK], GDK, post_q, "gdn_conv_q")
    k = _conv_fwd(pg, G_QK, g_cw[:, G_QK:2 * G_QK], zero_b[:, :G_QK], GDK, post_k, "gdn_conv_k")
    v = _conv_fwd(pg, 2 * G_QK, g_cw[:, 2 * G_QK:], zero_b[:, :G_V], GDK, _silu, "gdn_conv_v")
    braw = _gates_to_rows(pg_small[:, :GH], GH, GCH, GDN_CPB)
    araw = _gates_to_rows(pg_small[:, GH:2 * GH], GH, GCH, GDN_CPB)
    o, g_sall, g_tall = _gdn_fwd(q, k, v, braw, araw, g_alog, g_dtb)
    y0 = _gate_fwd(_gdn_gate, o, pg, G_CONV, g_nw, "gdn_gate")
    x1 = _matmul(y0, w_go, "nn", "gdn_out_proj", add=x0)

    hid1 = _rms_fwd(x1, nw1, "rms1")
    ps = _matmul(hid1, ws_main, "nn", "ssd_in_proj", tn=S_MAIN // 4)
    ps_small = _matmul(hid1, ws_small, "nn", "ssd_in_proj_small", tn=SMALL_W)
    c_x, c_b, c_c = S_INNER, 2 * S_INNER, 2 * S_INNER + SG * SN
    post_s = _silu
    xs = _conv_fwd(ps, c_x, s_cw[:, :S_INNER], s_cb[:, :S_INNER], SR * SP, post_s, "ssd_conv_x")
    bm = _conv_fwd(ps, c_b, s_cw[:, S_INNER:S_INNER + SG * SN], s_cb[:, S_INNER:S_INNER + SG * SN], SN,
                   post_s, "ssd_conv_b")
    cm = _conv_fwd(ps, c_c, s_cw[:, S_INNER + SG * SN:], s_cb[:, S_INNER + SG * SN:], SN, post_s,
                   "ssd_conv_c")
    nbs = t // (SCH * SSD_CPB)
    dtraw = _gates_to_rows(ps_small[:, :SH], SH, SCH, SSD_CPB)
    dtraw = dtraw.reshape(nbs, SG, SR, SSD_CPB, SCH).transpose(0, 3, 2, 1, 4).reshape(nbs, SSD_CPB * SR, SG, 1, SCH)
    yss, s_sall = _ssd_fwd(xs, bm, cm, dtraw, s_dtb, s_alog, s_d)
    y1 = _gate_fwd(_ssd_gate, yss, ps, 0, s_nw, "ssd_gate")
    x2 = _matmul(y1, w_so, "nn", "ssd_out_proj", add=x1)

    dx2, d_fw, loss_row = _final_loss(x2, fw, tgt, "final_loss")

    dy1 = _matmul(dx2, w_so, "nt", "ssd_out_dx", out_dtype=BF16, tn=S_INNER)
    d_wso = _matmul(y1, dx2, "tn", "ssd_out_dw")
    dyss, dz_s, d_snw = _gate_bwd(_ssd_gate, yss, ps, 0, s_nw, dy1, True, "ssd_gate_bwd")
    dxs, dbm, dcm, ddtraw, d_sdtb, d_salog, d_sd = _ssd_bwd(xs, bm, cm, dtraw, s_dtb, s_alog, s_d, s_sall, dyss)
    dps, dwx, dbx = _conv_dpre(ps, c_x, s_cw[:, :S_INNER], s_cb[:, :S_INNER], SR * SP, post_s, dxs, "ssd_dpre_x",
                               G_CONV, 0)
    dps, dwb, dbb = _conv_dpre(ps, c_b, s_cw[:, S_INNER:S_INNER + SG * SN], s_cb[:, S_INNER:S_INNER + SG * SN],
                               SN, post_s, dbm, "ssd_dpre_b", G_CONV, S_INNER, dps)
    dps, dwc, dbc = _conv_dpre(ps, c_c, s_cw[:, S_INNER + SG * SN:], s_cb[:, S_INNER + SG * SN:], SN, post_s,
                               dcm, "ssd_dpre_c", G_CONV, S_INNER + SG * SN, dps)
    d_scw = jnp.concatenate([dwx, dwb, dwc], axis=1)
    d_scb = jnp.concatenate([dbx, dbb, dbc], axis=1)
    dxbc = _conv_t(dps, s_cw, "ssd_conv_t")
    ddt = ddtraw.reshape(nbs, SSD_CPB, SR, SG, SCH).transpose(0, 3, 2, 1, 4).reshape(nbs, SH, SSD_CPB, SCH)
    ddt = _rows_to_gates(ddt)
    dsm_s = jnp.concatenate([ddt, jnp.zeros((t, SMALL_W - SH), F32)], axis=1).astype(BF16)
    dhid1 = _matmul(dz_s, ws_main[:, :S_INNER], "nt", "ssd_in_dx_z")
    dhid1 = _matmul(dxbc, ws_main[:, S_INNER:], "nt", "ssd_in_dx_xbc", add=dhid1)
    dhid1 = _matmul(dsm_s, ws_small, "nt", "ssd_in_dx_dt", add=dhid1, tk=SMALL_W)
    d_wsi = jnp.concatenate([_matmul(hid1, dz_s, "tn", "ssd_in_dw_z"),
                             _matmul(hid1, dxbc, "tn", "ssd_in_dw_xbc"),
                             _matmul(hid1, dsm_s, "tn", "ssd_in_dw_dt", tn=SMALL_W)[:, :SH]], axis=1)
    dx1, d_nw1 = _rms_bwd(x1, nw1, dx2, dhid1, "rms1_bwd")

    dy0 = _matmul(dx1, w_go, "nt", "gdn_out_dx", out_dtype=BF16, tn=G_V)
    d_wgo = _matmul(y0, dx1, "tn", "gdn_out_dw")
    do, dz_g, d_gnw = _gate_bwd(_gdn_gate, o, pg, G_CONV, g_nw, dy0, False, "gdn_gate_bwd")
    dq, dk, dv, dbraw, daraw, d_galog, d_gdtb = _gdn_bwd(q, k, v, braw, araw, g_alog, g_dtb, g_sall, g_tall, do)
    dpg, dwq, _ = _conv_dpre(pg, 0, g_cw[:, :G_QK], zero_b[:, :G_QK], GDK, post_q, dq, "gdn_dpre_q", G_CONV, 0)
    dpg, dwk, _ = _conv_dpre(pg, G_QK, g_cw[:, G_QK:2 * G_QK], zero_b[:, :G_QK], GDK, post_k, dk, "gdn_dpre_k",
                             G_CONV, G_QK, dpg)
    dpg, dwv, _ = _conv_dpre(pg, 2 * G_QK, g_cw[:, 2 * G_QK:], zero_b[:, :G_V], GDK, _silu, dv, "gdn_dpre_v",
                             G_CONV, 2 * G_QK, dpg)
    d_gcw = jnp.concatenate([dwq, dwk, dwv], axis=1)
    dqkv = _conv_t(dpg, g_cw, "gdn_conv_t")
    dsm_g = jnp.concatenate([_rows_to_gates(dbraw), _rows_to_gates(daraw),
                             jnp.zeros((t, SMALL_W - 2 * GH), F32)], axis=1).astype(BF16)
    dhid0 = _matmul(dqkv, wg_main[:, :G_CONV], "nt", "gdn_in_dx_qkv")
    dhid0 = _matmul(dz_g, wg_main[:, G_CONV:], "nt", "gdn_in_dx_z", add=dhid0)
    dhid0 = _matmul(dsm_g, wg_small, "nt", "gdn_in_dx_ba", add=dhid0, tk=SMALL_W)
    d_wgi = jnp.concatenate([_matmul(hid0, dqkv, "tn", "gdn_in_dw_qkv"),
                             _matmul(hid0, dz_g, "tn", "gdn_in_dw_z"),
                             _matmul(hid0, dsm_g, "tn", "gdn_in_dw_ba", tn=SMALL_W)[:, :2 * GH]], axis=1)
    dx0, d_nw0 = _rms_bwd(x0, nw0, dx1, dhid0, "rms0_bwd")

    def in_blocks(dw):
        return dw.reshape(D, N_CHIPS, IN_SHARD).transpose(1, 0, 2).reshape(N_CHIPS, 2, D // 2, IN_SHARD)

    def out_blocks(dw):
        return dw.reshape(N_CHIPS, 2, OUT_SHARD // 2, D)

    gs = [in_blocks(d_wgi), out_blocks(d_wgo), in_blocks(d_wsi), out_blocks(d_wso)]
    from_sib = _swap_halves(gs)
    hsum, hsum_bf = _sum_cores(gs, from_sib, cc.astype(jnp.int32).reshape(1))
    recv = _scatter_chips(hsum_bf)
    tots = _sum_chips(hsum, recv, chip.astype(jnp.int32).reshape(1))
    sib_tots = _swap_totals(tots)
    full = [jnp.concatenate([jnp.where(cc == 0, mine, sib), jnp.where(cc == 0, sib, mine)], axis=0)
            for mine, sib in zip(tots, sib_tots)]
    grads = dict(
        gdn_w_in=full[0].reshape(1, D, IN_SHARD), gdn_w_out=full[1].reshape(1, OUT_SHARD, D),
        ssd_w_in=full[2].reshape(1, D, IN_SHARD), ssd_w_out=full[3].reshape(1, OUT_SHARD, D))

    small_parts = [loss_row, jnp.concatenate([d_nw0, d_nw1], axis=0), d_gcw, d_galog, d_gdtb, d_gnw, d_scw, d_scb,
                   d_sdtb, d_salog, d_sd, d_snw, d_fw]
    sbuf, soffs = _pack_flat(small_parts, 8)
    ssum = _unpack_flat(_allreduce_small(sbuf), soffs)
    (loss_s, g_nw_all, g_gcw, g_galog, g_gdtb, g_gnw, g_scw, g_scb, g_sdtb, g_salog, g_sd, g_snw, g_fw) = ssum

    def my_cols(a, width):
        return lax.dynamic_slice_in_dim(a, chip * width, width, axis=a.ndim - 1)

    grads.update(
        norm_w=g_nw_all, gdn_conv_w=my_cols(g_gcw, 1024)[None], gdn_a_log=g_galog.reshape(1, GH),
        gdn_dt_bias=g_gdtb.reshape(1, GH), gdn_norm_w=g_gnw.reshape(1, GDK),
        ssd_conv_w=my_cols(g_scw, 1024)[None], ssd_conv_b=my_cols(g_scb, 1024),
        ssd_dt_bias=g_sdtb.reshape(SR, SG).T.reshape(1, SH), ssd_a_log=g_salog.reshape(SR, SG).T.reshape(1, SH),
        ssd_d=g_sd.reshape(SR, SG).T.reshape(1, SH),
        ssd_norm_w=my_cols(g_snw, 512), final_norm_w=g_fw.reshape(D))
    loss = loss_s[0, 0]

    big_names = ("gdn_w_in", "gdn_w_out", "ssd_w_in", "ssd_w_out")
    deltas, new_m, new_v = {}, {}, {}
    for n in big_names:
        deltas[n], new_m[n], new_v[n] = _adamw(ws[n], grads[n], ms[n], vs[n], "adamw_" + n)
    rest = [n for n in names if n not in big_names]
    packs = [_pack_flat([d[n] for n in rest], 8) for d in (ws, grads, ms, vs)]
    outs = _adamw(*[p[0] for p in packs], "adamw_small")
    for d, buf in zip((deltas, new_m, new_v), outs):
        for n, a in zip(rest, _unpack_flat(buf, packs[0][1])):
            d[n] = a

    grad_x = dx0.reshape(1, t, D)
    return (loss, grad_x, *[grads[n] for n in names], *[deltas[n] for n in names],
            *[new_m[n] for n in names], *[new_v[n] for n in names])
```

```python
import functools

import jax
import jax.numpy as jnp
from jax import lax
from jax.experimental import pallas as pl
from jax.experimental.pallas import tpu as pltpu

F32 = jnp.float32
BF16 = jnp.bfloat16
SDS = jax.ShapeDtypeStruct
MESH = pl.DeviceIdType.MESH

D = 1024
EPS = 1e-6
CONV_K = 4
N_CHIPS = 4
GH = 16
GHQ = 8
GDK = 128
GCH = 64
G_QK = 1024
G_V = 2048
G_CONV = 4096
G_MAIN = 6144
G_IN = 6176
SH = 32
SP = 64
SN = 128
SG = 8
SR = 4
SCH = 128
S_INNER = 2048
S_MAIN = 6144
S_IN = 6176
IN_SHARD = 1544
OUT_SHARD = 512
SMALL_W = 128
Z_EXT = 2048 + SMALL_W

ADAM_LR = 0.001
ADAM_B1 = 0.9
ADAM_B2 = 0.999
ADAM_EPS = 1e-08
ADAM_WD = 0.01
ADAM_STEP = 10

VMEM_LIMIT = 56 * 1024 * 1024
BLOCK_ELEMS = 512 * 1024
FWD_COLS = 512
NEG = -1e30


def _pcall(body, **kw):
    return pl.pallas_call(body, **kw)


def _cp(sem=None, vmem=VMEM_LIMIT):
    return pltpu.CompilerParams(dimension_semantics=sem, vmem_limit_bytes=vmem)


@jax.custom_jvp
def _sigmoid(x):
    return 1.0 / (1.0 + jnp.exp(-x))


@_sigmoid.defjvp
def _sigmoid_jvp(primals, tangents):
    s = _sigmoid(primals[0])
    return s, tangents[0] * (s * (1.0 - s))


@jax.custom_jvp
def _silu(x):
    return x * _sigmoid(x)


@_silu.defjvp
def _silu_jvp(primals, tangents):
    x = primals[0]
    s = _sigmoid(x)
    return x * s, tangents[0] * (s * (1.0 + x * (1.0 - s)))


def _softplus(x):
    return jnp.maximum(x, 0.0) + jnp.log(1.0 + jnp.exp(-jnp.abs(x)))


def _rms(x, w):
    return x * lax.rsqrt(jnp.mean(x * x, axis=-1, keepdims=True) + EPS) * w


_DIMS = {"nn": (((2,), (1,)), ((0,), (0,))),
         "nt": (((2,), (2,)), ((0,), (0,))),
         "tn": (((1,), (1,)), ((0,), (0,)))}


def _bdot(a, b, spec):
    return lax.dot_general(a.astype(BF16), b.astype(BF16), _DIMS[spec], preferred_element_type=F32)


@functools.partial(jax.custom_vjp, nondiff_argnums=(2,))
def _bmm(a, b, spec):
    return _bdot(a, b, spec)


def _bmm_fwd(a, b, spec):
    return _bdot(a, b, spec), (a, b)


def _bmm_bwd(spec, res, g):
    a, b = res
    if spec == "nn":
        return _bdot(g, b, "nt"), _bdot(a, g, "tn")
    if spec == "nt":
        return _bdot(g, b, "nn"), _bdot(g, a, "tn")
    return _bdot(b, g, "nt"), _bdot(a, g, "nn")


_bmm.defvjp(_bmm_fwd, _bmm_bwd)


@jax.custom_vjp
def _tri_inv(n):
    t = -n
    p = n
    steps = (n.shape[-1] - 1).bit_length() - 1
    r = lax.broadcasted_iota(jnp.int32, n.shape, 1)
    c = lax.broadcasted_iota(jnp.int32, n.shape, 2)
    t = t + jnp.where(r == c, 1.0, 0.0)
    for _ in range(steps):
        p = _bdot(p, p, "nn")
        t = t + _bdot(t, p, "nn")
    return t


def _tri_inv_fwd(n):
    t = _tri_inv(n)
    return t, t


def _tri_inv_bwd(t, g):
    return (-_bdot(_bdot(t, g, "tn"), t, "nt"),)


_tri_inv.defvjp(_tri_inv_fwd, _tri_inv_bwd)


@jax.custom_vjp
def _tri_inv_known(n, t):
    del n
    return t


def _tri_inv_known_fwd(n, t):
    del n
    return t, t


def _tri_inv_known_bwd(t, g):
    return _tri_inv_bwd(t, g)[0], jnp.zeros_like(t)


_tri_inv_known.defvjp(_tri_inv_known_fwd, _tri_inv_known_bwd)


def _masks(c, lead=1):
    r = lax.broadcasted_iota(jnp.int32, (lead, c, c), 1)
    s = lax.broadcasted_iota(jnp.int32, (lead, c, c), 2)
    return r >= s, r > s, r == s, r <= s


def _row_to_col(row, eye):
    return jnp.sum(jnp.where(eye, row, 0.0), axis=2, keepdims=True)


def _gdn_chunk(q, k, v, braw, araw, alog, dtb, s, t_known=None, want_t=False):
    h = v.shape[0]
    c = v.shape[1]
    rep = h // q.shape[0]
    tril, strict, eye, triu = _masks(c)
    qq = jnp.broadcast_to(q[:, None], (q.shape[0], rep) + q.shape[1:]).reshape(v.shape)
    kk = jnp.broadcast_to(k[:, None], (k.shape[0], rep) + k.shape[1:]).reshape(v.shape)
    beta_row = _sigmoid(braw)
    g_row = -jnp.exp(alog) * _softplus(araw + dtb)
    beta_col = _row_to_col(beta_row, eye)
    g_col = _row_to_col(g_row, eye)
    gc_col = jnp.sum(jnp.where(tril, g_row, 0.0), axis=2, keepdims=True)
    gc_row = jnp.sum(jnp.where(triu, g_col, 0.0), axis=1, keepdims=True)
    gc_last = jnp.sum(g_row, axis=2, keepdims=True)
    lmat = jnp.exp(jnp.where(tril, gc_col - gc_row, NEG))
    kb = kk * beta_col
    vb = v * beta_col
    n = jnp.where(strict, _bmm(kb, kk, "nt") * lmat, 0.0)
    t = _tri_inv(n) if t_known is None else _tri_inv_known(n, t_known)
    e_col = jnp.exp(gc_col)
    u = _bmm(t, vb, "nn")
    w = _bmm(t, kb * e_col, "nn")
    attn = _bmm(qq, kk, "nt") * lmat
    q_dec = qq * e_col
    k_dec = kk * jnp.exp(gc_last - gc_col)
    v_new = u - _bmm(w, s, "nn")
    o = _bmm(q_dec, s, "nn") + _bmm(attn, v_new, "nn")
    s_new = s * jnp.exp(gc_last) + _bmm(k_dec, v_new, "tn")
    return (o, s_new, t) if want_t else (o, s_new)


def _ssd_chunk(xs, bm, cm, dtraw, dtb, alog, dskip, s):
    c = xs.shape[1]
    tril, _, eye, triu = _masks(c)
    lane = lax.broadcasted_iota(jnp.int32, (1, 1, SR * SP), 2)
    prow = lax.broadcasted_iota(jnp.int32, (1, SR * SP, 1), 1)
    cb = _bmm(cm, bm, "nt")
    cs = _bmm(cm, s, "nt")

    def per_head(vals, idx):
        out = vals[SR - 1]
        for r in reversed(range(SR - 1)):
            out = jnp.where(idx < (r + 1) * SP, vals[r], out)
        return out

    dt_cols, e_cols, lmats, dstates, declast = [], [], [], [], []
    for r in range(SR):
        dt_row = _softplus(dtraw[r] + dtb[r])
        adt_row = -jnp.exp(alog[r]) * dt_row
        dt_cols.append(_row_to_col(dt_row, eye))
        adt_col = _row_to_col(adt_row, eye)
        acs_col = jnp.sum(jnp.where(tril, adt_row, 0.0), axis=2, keepdims=True)
        acs_row = jnp.sum(jnp.where(triu, adt_col, 0.0), axis=1, keepdims=True)
        acs_last = jnp.sum(adt_row, axis=2, keepdims=True)
        lmats.append(jnp.exp(jnp.where(tril, acs_col - acs_row, NEG)))
        e_cols.append(jnp.exp(acs_col))
        dstates.append(jnp.exp(acs_last - acs_col))
        declast.append(jnp.exp(acs_last))
    xd = xs * per_head(dt_cols, lane)
    y = per_head([_bmm(cb * lmats[r], xd, "nn") for r in range(SR)], lane)
    states = per_head([_bmm(xd, bm * dstates[r], "tn") for r in range(SR)], prow)
    y = y + cs * per_head(e_cols, lane) + xs * per_head([dskip[r] for r in range(SR)], lane)
    s_new = s * per_head(declast, prow) + states
    return y, s_new


def _matmul(a, b, mode, name, out_dtype=F32, add=None, tm=1024, tn=1024, tk=2048):
    if mode == "nn":
        (m, k), n = a.shape, b.shape[1]
    elif mode == "nt":
        (m, k), n = a.shape, b.shape[0]
    else:
        (k, m), n = a.shape, b.shape[1]
    tm, tn, tk = min(tm, m), min(tn, n), min(tk, k)
    assert m % tm == 0 and n % tn == 0 and k % tk == 0, (name, m, n, k)
    nk = k // tk
    dims = {"nn": (((1,), (0,)), ((), ())), "nt": (((1,), (1,)), ((), ())),
            "tn": (((0,), (0,)), ((), ()))}[mode]
    a_spec = {"nn": pl.BlockSpec((tm, tk), lambda i, j, kk: (i, kk)),
              "nt": pl.BlockSpec((tm, tk), lambda i, j, kk: (i, kk)),
              "tn": pl.BlockSpec((tk, tm), lambda i, j, kk: (kk, i))}[mode]
    b_spec = {"nn": pl.BlockSpec((tk, tn), lambda i, j, kk: (kk, j)),
              "nt": pl.BlockSpec((tn, tk), lambda i, j, kk: (j, kk)),
              "tn": pl.BlockSpec((tk, tn), lambda i, j, kk: (kk, j))}[mode]
    o_spec = pl.BlockSpec((tm, tn), lambda i, j, kk: (i, j))
    has_add = add is not None

    def body(*refs):
        a_ref, b_ref = refs[:2]
        add_ref = refs[2] if has_add else None
        o_ref = refs[2 + has_add]
        part = lax.dot_general(a_ref[...].astype(BF16), b_ref[...].astype(BF16), dims,
                               preferred_element_type=F32)

        def finish(r):
            if has_add:
                r = r + add_ref[...].astype(F32)
            o_ref[...] = r.astype(o_ref.dtype)

        if nk == 1:
            finish(part)
            return
        acc_ref = refs[3 + has_add]
        kk = pl.program_id(2)

        @pl.when(kk == 0)
        def _():
            acc_ref[...] = part

        @pl.when(jnp.logical_and(kk > 0, kk < nk - 1))
        def _():
            acc_ref[...] += part

        @pl.when(kk == nk - 1)
        def _():
            finish(acc_ref[...] + part)

    ins = [a, b] + ([add] if has_add else [])
    in_specs = [a_spec, b_spec] + ([o_spec] if has_add else [])
    scratch = [] if nk == 1 else [pltpu.VMEM((tm, tn), F32)]
    return _pcall(body, grid=(m // tm, n // tn, nk), in_specs=in_specs, out_specs=o_spec,
                  out_shape=SDS((m, n), out_dtype), scratch_shapes=scratch,
                  compiler_params=_cp(("parallel", "parallel", "arbitrary")), name=name)(*ins)


def _ew(fn, ins, in_specs, out_shape, out_specs, grid, name):
    n_in = len(ins)

    def body(*refs):
        outs = fn(*[r[...] for r in refs[:n_in]])
        for r, o in zip(refs[n_in:], outs):
            r[...] = o.astype(r.dtype)

    return _pcall(body, grid=grid, in_specs=in_specs, out_specs=out_specs, out_shape=out_shape,
                  compiler_params=_cp(("arbitrary",) * len(grid)), name=name)(*ins)


def _ew_vjp(fn, ins, in_specs, cts, ct_specs, wrt, g_shape, g_specs, acc, grid, name):
    n_in, n_ct = len(ins), len(cts)

    def body(*refs):
        vals = [r[...].astype(F32) for r in refs[:n_in]]
        outs, vjp = jax.vjp(fn, *vals)
        g_all = vjp(tuple(r[...].astype(F32) for r in refs[n_in:n_in + n_ct]))
        for pos, (i, g_ref) in enumerate(zip(wrt, refs[n_in + n_ct:])):
            g = g_all[i]
            if pos in acc:
                first = functools.reduce(
                    jnp.logical_and, [pl.program_id(ax) == 0 for ax in range(acc[pos], len(grid))])

                @pl.when(first)
                def _():
                    g_ref[...] = jnp.zeros_like(g_ref)

                g_ref[...] += g.astype(g_ref.dtype)
            else:
                g_ref[...] = g.astype(g_ref.dtype)

    return _pcall(body, grid=grid, in_specs=list(in_specs) + list(ct_specs), out_specs=g_specs,
                  out_shape=g_shape, compiler_params=_cp(("arbitrary",) * len(grid)),
                  name=name)(*ins, *cts)


def _row_spec(tm, n):
    return pl.BlockSpec((tm, n), lambda i: (i, 0))


def _par_spec(n):
    return pl.BlockSpec((1, n), lambda i: (0, 0))


def _rms_fwd(x, w, name):
    t = x.shape[0]
    tm = min(t, 512)
    return _ew(lambda xv, wv: (_rms(xv, wv),), [x, w], [_row_spec(tm, D), _par_spec(D)],
               [SDS((t, D), BF16)], [_row_spec(tm, D)], (t // tm,), name)[0]


def _rms_bwd(x, w, dres, dhid, name):
    t = x.shape[0]
    tm = min(t, 512)
    return _ew_vjp(lambda xv, wv: (xv, _rms(xv, wv)), [x, w], [_row_spec(tm, D), _par_spec(D)],
                   [dres, dhid], [_row_spec(tm, D), _row_spec(tm, D)], (0, 1),
                   [SDS((t, D), F32), SDS((1, D), F32)], [_row_spec(tm, D), _par_spec(D)],
                   {1: 0}, (t // tm,), name)


def _final_loss(x, w, tgt, name):
    t = x.shape[0]
    tm = min(t, 512)

    def body(x_ref, w_ref, t_ref, dx_ref, dw_ref, l_ref):
        @pl.when(pl.program_id(0) == 0)
        def _():
            dw_ref[...] = jnp.zeros_like(dw_ref)
            l_ref[...] = jnp.zeros_like(l_ref)

        xv, wv = x_ref[...], w_ref[...]
        rstd = lax.rsqrt(jnp.mean(xv * xv, axis=-1, keepdims=True) + EPS)
        xh = xv * rstd
        err = xh * wv - t_ref[...]
        l_ref[...] += 0.5 * jnp.sum(jnp.mean(err * err, axis=-1, keepdims=True), axis=0, keepdims=True)
        dy = err * (1.0 / D)
        dw_ref[...] += jnp.sum(dy * xh, axis=0, keepdims=True)
        dxh = dy * wv
        dx_ref[...] = rstd * (dxh - xh * jnp.mean(dxh * xh, axis=-1, keepdims=True))

    return _pcall(body, grid=(t // tm,), in_specs=[_row_spec(tm, D), _par_spec(D), _row_spec(tm, D)],
                  out_specs=[_row_spec(tm, D), _par_spec(D), _par_spec(128)],
                  out_shape=[SDS((t, D), F32), SDS((1, D), F32), SDS((1, 128), F32)],
                  compiler_params=_cp(("arbitrary",)), name=name)(x, w, tgt)


def _conv_taps(ext, w_ref, tm, lo):
    n = ext.shape[0]
    acc = None
    for j in range(CONV_K):
        shift = (CONV_K - 1 - j) if lo else (n - (CONV_K - 1 - j)) % n
        rolled = pltpu.roll(ext, shift, 0) if shift else ext
        term = w_ref[pl.ds(j, 1), :] * rolled[lo:lo + tm]
        acc = term if acc is None else acc + term
    return acc


def _conv_pre_specs(tm, ct, col0):
    hb = tm // 8
    return [pl.BlockSpec((tm, ct), lambda j, i: (i, col0 + j)),
            pl.BlockSpec((8, ct), lambda j, i: (jnp.maximum(i * hb - 1, 0), col0 + j)),
            pl.BlockSpec((CONV_K, ct), lambda j, i: (0, j)),
            pl.BlockSpec((1, ct), lambda j, i: (0, j))]


def _conv_pre_value(x_ref, xh_ref, w_ref, b_ref, tm):
    halo = jnp.where(pl.program_id(1) > 0, xh_ref[...], 0.0)
    ext = jnp.concatenate([halo, x_ref[...]], axis=0)
    return _conv_taps(ext, w_ref, tm, 8) + b_ref[...], ext


def _conv_fwd(x, col0, w, b, ct, post, name):
    t = x.shape[0]
    tpb = FWD_COLS // ct
    wide = tpb * ct
    tm = min(t, BLOCK_ELEMS // wide)
    nt = w.shape[1] // ct
    assert col0 % wide == 0 and nt % tpb == 0, name

    def body(x_ref, xh_ref, w_ref, b_ref, o_ref):
        pre, _ = _conv_pre_value(x_ref, xh_ref, w_ref, b_ref, tm)
        for k in range(tpb):
            o_ref[k] = post(pre[:, k * ct:(k + 1) * ct])

    return _pcall(body, grid=(nt // tpb, t // tm), in_specs=_conv_pre_specs(tm, wide, col0 // wide),
                  out_specs=pl.BlockSpec((tpb, tm, ct), lambda j, i: (j, i, 0)),
                  out_shape=SDS((nt, t, ct), F32), compiler_params=_cp(("arbitrary", "arbitrary")),
                  name=name)(x, x, w, b)


def _conv_dpre(x, col0, w, b, ct, post, dout, name, c_total, c_off, into=None):
    t = x.shape[0]
    tm = min(t, BLOCK_ELEMS // ct)
    nt = w.shape[1] // ct
    chained = into is not None

    def body(*refs):
        x_ref, xh_ref, w_ref, b_ref, do_ref = refs[:5]
        dp_ref, dw_ref, db_ref = refs[5 + chained:]
        pre, ext = _conv_pre_value(x_ref, xh_ref, w_ref, b_ref, tm)
        _, vjp = jax.vjp(post, pre)
        dpre = vjp(do_ref[0])[0]
        dp_ref[...] = dpre

        @pl.when(pl.program_id(1) == 0)
        def _():
            dw_ref[...] = jnp.zeros_like(dw_ref)
            db_ref[...] = jnp.zeros_like(db_ref)

        for j in range(CONV_K):
            xs = (pltpu.roll(ext, CONV_K - 1 - j, 0) if j < CONV_K - 1 else ext)[8:8 + tm]
            dw_ref[pl.ds(j, 1), :] += jnp.sum(dpre * xs, axis=0, keepdims=True)
        db_ref[...] += jnp.sum(dpre, axis=0, keepdims=True)

    c = w.shape[1]
    o0 = c_off // ct
    return _pcall(body, grid=(nt, t // tm),
                  in_specs=_conv_pre_specs(tm, ct, col0 // ct)
                  + [pl.BlockSpec((1, tm, ct), lambda j, i: (j, i, 0))] + ([ANY] if chained else []),
                  out_specs=[pl.BlockSpec((tm, ct), lambda j, i: (i, o0 + j)),
                             pl.BlockSpec((CONV_K, ct), lambda j, i: (0, j)),
                             pl.BlockSpec((1, ct), lambda j, i: (0, j))],
                  out_shape=[SDS((t, c_total), F32), SDS((CONV_K, c), F32), SDS((1, c), F32)],
                  input_output_aliases={5: 0} if chained else {},
                  compiler_params=_cp(("arbitrary", "arbitrary")), name=name)(
                      x, x, w, b, dout, *([into] if chained else []))


def _conv_t(dpre, w, name):
    t, c = dpre.shape
    ct = min(c, 512)
    tm = min(t, BLOCK_ELEMS // ct)
    hb = tm // 8
    last = t // tm - 1

    def body(d_ref, dh_ref, w_ref, o_ref):
        halo = jnp.where(pl.program_id(1) < last, dh_ref[...], 0.0)
        ext = jnp.concatenate([d_ref[...], halo], axis=0)
        o_ref[...] = _conv_taps(ext, w_ref, tm, 0).astype(o_ref.dtype)

    return _pcall(body, grid=(c // ct, t // tm),
                  in_specs=[pl.BlockSpec((tm, ct), lambda j, i: (i, j)),
                            pl.BlockSpec((8, ct), lambda j, i: (jnp.minimum((i + 1) * hb, t // 8 - 1), j)),
                            pl.BlockSpec((CONV_K, ct), lambda j, i: (0, j))],
                  out_specs=pl.BlockSpec((tm, ct), lambda j, i: (i, j)),
                  out_shape=SDS((t, c), BF16), compiler_params=_cp(("arbitrary", "arbitrary")),
                  name=name)(dpre, dpre, w)


def _l2norm_scaled(scale):
    def post(pre):
        a = _silu(pre)
        return a * lax.rsqrt(jnp.sum(a * a, axis=-1, keepdims=True) + EPS) * scale
    return post


GDN_HB = 16
GDN_CPB = 4
SSD_GB = 8
SSD_CPB = 1


def _gdn_specs(nb, rev):
    hb, cpb, tc = GDN_HB, GDN_CPB, GDN_CPB * GCH
    blk = (lambda n: nb - 1 - n) if rev else (lambda n: n)
    seq = lambda h: pl.BlockSpec((h, tc, GDK), lambda g, n: (g, blk(n), 0))
    gate = pl.BlockSpec((1, hb, cpb, GCH), lambda g, n: (blk(n), g, 0, 0))
    par = pl.BlockSpec((hb, 1, 1), lambda g, n: (g, 0, 0))
    state = pl.BlockSpec((cpb, hb, GDK, GDK), lambda g, n: (blk(n), g, 0, 0))
    tinv = pl.BlockSpec((cpb, hb, GCH, GCH), lambda g, n: (blk(n), g, 0, 0))
    return seq, gate, par, state, tinv


def _gdn_fwd(q, k, v, braw, araw, alog, dtb):
    t = v.shape[1]
    hb, cpb = GDN_HB, GDN_CPB
    nb = t // (cpb * GCH)
    seq, gate, par, state, tinv = _gdn_specs(nb, False)

    def body(q_ref, k_ref, v_ref, b_ref, a_ref, al_ref, dt_ref, o_ref, sall_ref, tall_ref, s_ref):
        @pl.when(pl.program_id(1) == 0)
        def _():
            s_ref[...] = jnp.zeros_like(s_ref)

        s = s_ref[...]
        for c in range(cpb):
            rows = pl.ds(c * GCH, GCH)
            sall_ref[c] = s
            o, s, tmat = _gdn_chunk(q_ref[:, rows, :], k_ref[:, rows, :], v_ref[:, rows, :],
                                    b_ref[0, :, pl.ds(c, 1), :], a_ref[0, :, pl.ds(c, 1), :],
                                    al_ref[...], dt_ref[...], s, want_t=True)
            o_ref[:, rows, :] = o
            tall_ref[c] = tmat.astype(BF16)
        s_ref[...] = s

    return _pcall(body, grid=(GH // hb, nb),
                  in_specs=[seq(hb // 2), seq(hb // 2), seq(hb), gate, gate, par, par],
                  out_specs=[seq(hb), state, tinv],
                  out_shape=[SDS((GH, t, GDK), F32), SDS((t // GCH, GH, GDK, GDK), F32),
                             SDS((t // GCH, GH, GCH, GCH), BF16)],
                  scratch_shapes=[pltpu.VMEM((hb, GDK, GDK), F32)],
                  compiler_params=_cp(("arbitrary", "arbitrary")), name="gdn_chunk_fwd")(
                      q, k, v, braw, araw, alog, dtb)


def _gdn_bwd(q, k, v, braw, araw, alog, dtb, sall, tall, do):
    t = v.shape[1]
    hb, cpb = GDN_HB, GDN_CPB
    nb = t // (cpb * GCH)
    seq, gate, par, state, tinv = _gdn_specs(nb, True)

    def body(q_ref, k_ref, v_ref, b_ref, a_ref, al_ref, dt_ref, sall_ref, tall_ref, do_ref,
             dq_ref, dk_ref, dv_ref, db_ref, da_ref, dal_ref, ddt_ref, ds_ref):
        @pl.when(pl.program_id(1) == 0)
        def _():
            ds_ref[...] = jnp.zeros_like(ds_ref)
            dal_ref[...] = jnp.zeros_like(dal_ref)
            ddt_ref[...] = jnp.zeros_like(ddt_ref)

        ds = ds_ref[...]
        for c in reversed(range(cpb)):
            rows = pl.ds(c * GCH, GCH)
            fn = functools.partial(_gdn_chunk, t_known=tall_ref[c].astype(F32))
            _, vjp = jax.vjp(fn, q_ref[:, rows, :], k_ref[:, rows, :], v_ref[:, rows, :],
                             b_ref[0, :, pl.ds(c, 1), :], a_ref[0, :, pl.ds(c, 1), :],
                             al_ref[...], dt_ref[...], sall_ref[c])
            dq, dk, dv, db, da, dal, ddt, ds = vjp((do_ref[:, rows, :], ds))
            dq_ref[:, rows, :] = dq
            dk_ref[:, rows, :] = dk
            dv_ref[:, rows, :] = dv
            db_ref[0, :, pl.ds(c, 1), :] = db
            da_ref[0, :, pl.ds(c, 1), :] = da
            dal_ref[...] += dal
            ddt_ref[...] += ddt
        ds_ref[...] = ds

    return _pcall(body, grid=(GH // hb, nb),
                  in_specs=[seq(hb // 2), seq(hb // 2), seq(hb), gate, gate, par, par, state, tinv, seq(hb)],
                  out_specs=[seq(hb // 2), seq(hb // 2), seq(hb), gate, gate, par, par],
                  out_shape=[SDS(q.shape, F32), SDS(k.shape, F32), SDS(v.shape, F32),
                             SDS(braw.shape, F32), SDS(araw.shape, F32),
                             SDS((GH, 1, 1), F32), SDS((GH, 1, 1), F32)],
                  scratch_shapes=[pltpu.VMEM((hb, GDK, GDK), F32)],
                  compiler_params=_cp(("arbitrary", "arbitrary")), name="gdn_chunk_bwd")(
                      q, k, v, braw, araw, alog, dtb, sall, tall, do)


def _ssd_specs(nb, rev):
    gb, cpb, tc = SSD_GB, SSD_CPB, SSD_CPB * SCH
    blk = (lambda n: nb - 1 - n) if rev else (lambda n: n)
    seq = lambda w: pl.BlockSpec((gb, tc, w), lambda g, n: (g, blk(n), 0))
    gate = pl.BlockSpec((1, cpb * SR, gb, 1, SCH), lambda g, n: (blk(n), 0, g, 0, 0))
    par = pl.BlockSpec((SR, gb, 1, 1), lambda g, n: (0, g, 0, 0))
    state = pl.BlockSpec((cpb, gb, SR * SP, SN), lambda g, n: (blk(n), g, 0, 0))
    return seq, gate, par, state


def _ssd_fwd(xs, bm, cm, dtraw, dtb, alog, dskip):
    t = xs.shape[1]
    gb, cpb = SSD_GB, SSD_CPB
    nb = t // (cpb * SCH)
    seq, gate, par, state = _ssd_specs(nb, False)

    def body(x_ref, b_ref, c_ref, dt_ref, dtb_ref, al_ref, dk_ref, y_ref, sall_ref, s_ref):
        @pl.when(pl.program_id(1) == 0)
        def _():
            s_ref[...] = jnp.zeros_like(s_ref)

        s = s_ref[...]
        for c in range(cpb):
            rows = pl.ds(c * SCH, SCH)
            sall_ref[c] = s
            y, s = _ssd_chunk(x_ref[:, rows, :], b_ref[:, rows, :], c_ref[:, rows, :],
                              dt_ref[0, pl.ds(c * SR, SR)], dtb_ref[...], al_ref[...],
                              dk_ref[...], s)
            y_ref[:, rows, :] = y
        s_ref[...] = s

    return _pcall(body, grid=(SG // gb, nb),
                  in_specs=[seq(SR * SP), seq(SN), seq(SN), gate, par, par, par],
                  out_specs=[seq(SR * SP), state],
                  out_shape=[SDS((SG, t, SR * SP), F32), SDS((t // SCH, SG, SR * SP, SN), F32)],
                  scratch_shapes=[pltpu.VMEM((gb, SR * SP, SN), F32)],
                  compiler_params=_cp(("arbitrary", "arbitrary")), name="ssd_chunk_fwd")(
                      xs, bm, cm, dtraw, dtb, alog, dskip)


def _ssd_bwd(xs, bm, cm, dtraw, dtb, alog, dskip, sall, dy):
    t = xs.shape[1]
    gb, cpb = SSD_GB, SSD_CPB
    nb = t // (cpb * SCH)
    seq, gate, par, state = _ssd_specs(nb, True)

    def body(x_ref, b_ref, c_ref, dt_ref, dtb_ref, al_ref, dk_ref, sall_ref, dy_ref,
             dx_ref, dbm_ref, dcm_ref, ddt_ref, ddtb_ref, dal_ref, ddk_ref, ds_ref):
        @pl.when(pl.program_id(1) == 0)
        def _():
            ds_ref[...] = jnp.zeros_like(ds_ref)
            ddtb_ref[...] = jnp.zeros_like(ddtb_ref)
            dal_ref[...] = jnp.zeros_like(dal_ref)
            ddk_ref[...] = jnp.zeros_like(ddk_ref)

        ds = ds_ref[...]
        for c in reversed(range(cpb)):
            rows = pl.ds(c * SCH, SCH)
            _, vjp = jax.vjp(_ssd_chunk, x_ref[:, rows, :], b_ref[:, rows, :], c_ref[:, rows, :],
                             dt_ref[0, pl.ds(c * SR, SR)], dtb_ref[...], al_ref[...],
                             dk_ref[...], sall_ref[c])
            dx, dbm, dcm, ddt, ddtb, dal, ddk, ds = vjp((dy_ref[:, rows, :], ds))
            dx_ref[:, rows, :] = dx
            dbm_ref[:, rows, :] = dbm
            dcm_ref[:, rows, :] = dcm
            ddt_ref[0, pl.ds(c * SR, SR)] = ddt
            ddtb_ref[...] += ddtb
            dal_ref[...] += dal
            ddk_ref[...] += ddk
        ds_ref[...] = ds

    return _pcall(body, grid=(SG // gb, nb),
                  in_specs=[seq(SR * SP), seq(SN), seq(SN), gate, par, par, par, state, seq(SR * SP)],
                  out_specs=[seq(SR * SP), seq(SN), seq(SN), gate, par, par, par],
                  out_shape=[SDS(xs.shape, F32), SDS(bm.shape, F32), SDS(cm.shape, F32),
                             SDS(dtraw.shape, F32), SDS((SR, SG, 1, 1), F32), SDS((SR, SG, 1, 1), F32),
                             SDS((SR, SG, 1, 1), F32)],
                  scratch_shapes=[pltpu.VMEM((gb, SR * SP, SN), F32)],
                  compiler_params=_cp(("arbitrary", "arbitrary")), name="ssd_chunk_bwd")(
                      xs, bm, cm, dtraw, dtb, alog, dskip, sall, dy)


def _gate_specs(tm, ct, zcol0, per_tile_w):
    z0 = zcol0 // ct
    return [pl.BlockSpec((1, tm, ct), lambda i, j: (j, i, 0)),
            pl.BlockSpec((tm, ct), lambda i, j: (i, z0 + j)),
            pl.BlockSpec((1, ct), (lambda i, j: (0, j)) if per_tile_w else (lambda i, j: (0, 0)))]


def _gdn_gate(o, z, w):
    return (_rms(o[0], w) * _silu(z),)


def _ssd_gate(y, z, w):
    return (_rms(y[0] * _silu(z), w),)


def _gate_fwd(fn, o, proj, zcol0, w, name):
    nt, t, ct = o.shape
    tpb = FWD_COLS // ct
    wide = tpb * ct
    tm = min(t, BLOCK_ELEMS // wide)
    per_tile_w = w.shape[1] > ct
    assert zcol0 % wide == 0 and nt % tpb == 0, name
    z0 = zcol0 // wide
    specs = [pl.BlockSpec((tpb, tm, ct), lambda i, j: (j, i, 0)),
             pl.BlockSpec((tm, wide), lambda i, j: (i, z0 + j)),
             pl.BlockSpec((1, wide), lambda i, j: (0, j)) if per_tile_w
             else pl.BlockSpec((1, ct), lambda i, j: (0, 0))]

    def fn_wide(ov, zv, wv):
        outs = []
        for k in range(tpb):
            cols = slice(k * ct, (k + 1) * ct)
            outs.append(fn(ov[k:k + 1], zv[:, cols], wv[:, cols] if per_tile_w else wv)[0])
        return (jnp.concatenate(outs, axis=1),)

    return _ew(fn_wide, [o, proj, w], specs, [SDS((t, nt * ct), BF16)],
               [pl.BlockSpec((tm, wide), lambda i, j: (i, j))], (t // tm, nt // tpb), name)[0]


def _gate_bwd(fn, o, proj, zcol0, w, dy, wacc, name, dz_cols):
    nt, t, ct = o.shape
    tm = min(t, BLOCK_ELEMS // ct)
    specs = _gate_specs(tm, ct, zcol0, wacc)
    out_spec = pl.BlockSpec((tm, ct), lambda i, j: (i, j))
    if wacc:
        flip = lambda s: pl.BlockSpec(s.block_shape, lambda j, i, f=s.index_map: f(i, j))
        specs = [flip(s) for s in specs]
        out_spec = flip(out_spec)
        grid, acc = (nt, t // tm), {2: 1}
    else:
        grid, acc = (t // tm, nt), {2: 0}
    return _ew_vjp(fn, [o, proj, w], specs, [dy], [out_spec], (0, 1, 2),
                   [SDS(o.shape, F32), SDS((t, dz_cols), BF16), SDS(w.shape, F32)],
                   [specs[0], out_spec, specs[2]], acc, grid, name)


def _adamw_math(w, g, m, v):
    m = ADAM_B1 * m + (1.0 - ADAM_B1) * g
    v = ADAM_B2 * v + (1.0 - ADAM_B2) * jnp.square(g)
    m_hat = m / (1.0 - ADAM_B1 ** ADAM_STEP)
    v_hat = v / (1.0 - ADAM_B2 ** ADAM_STEP)
    delta = -ADAM_LR * (m_hat / (jnp.sqrt(v_hat) + ADAM_EPS) + ADAM_WD * w)
    return delta, m, v


def _adamw(w, g, m, v, name):
    shape = w.shape
    w2, g2, m2, v2 = [a.reshape(-1, shape[-1]) for a in (w, g, m, v)]
    r, c = w2.shape
    tr = 256 if r % 256 == 0 else r
    spec = pl.BlockSpec((tr, c), lambda i: (i, 0))
    outs = _ew(_adamw_math, [w2, g2, m2, v2], [spec] * 4, [SDS((r, c), F32)] * 3, [spec] * 3,
               (r // tr,), name)
    return [o.reshape(shape) for o in outs]


def _coords():
    return lax.axis_index("x"), lax.axis_index("y"), lax.axis_index("c")


def _other_chips(x, y):
    return [(1 - x, y), (x, 1 - y), (1 - x, 1 - y)]


ANY = pl.BlockSpec(memory_space=pl.ANY)


def _rcopy(src, dst, send_sems, recv_sems, k, to):
    return pltpu.make_async_remote_copy(src_ref=src, dst_ref=dst, send_sem=send_sems.at[k],
                                        recv_sem=recv_sems.at[k], device_id=to, device_id_type=MESH)


def _gather_weights(bigs, small):
    n = len(bigs)

    def body(*refs):
        in_refs, small_ref = refs[:n], refs[n]
        out_refs, osmall_ref = refs[n + 1:2 * n + 1], refs[2 * n + 1]
        send_sems, recv_sems = refs[2 * n + 2:]
        x, y, c = _coords()
        me = 2 * x + y
        sibling = (x, y, 1 - c)
        chips = _other_chips(x, y)
        halves = [b.shape[0] // 2 for b in bigs]
        mine = [pl.ds(c * h, h) for h in halves]
        theirs = [pl.ds((1 - c) * h, h) for h in halves]
        rc = functools.partial(_rcopy, send_sems=send_sems, recv_sems=recv_sems)

        first = []
        for a in range(n):
            for j, (px, py) in enumerate(chips):
                first.append(rc(in_refs[a].at[mine[a]], out_refs[a].at[me, mine[a]], k=3 * a + j, to=(px, py, c)))
        for j, (px, py) in enumerate(chips):
            first.append(rc(small_ref, osmall_ref.at[me], k=6 * n + j, to=(px, py, c)))
        for cp in first:
            cp.start()
        passed = []
        for a in range(n):
            for j, (px, py) in enumerate(chips):
                landed = out_refs[a].at[2 * px + py, mine[a]]
                rc(landed, landed, k=3 * a + j, to=(px, py, c)).wait_recv()
                fw = rc(landed, landed, k=3 * n + 3 * a + j, to=sibling)
                fw.start()
                passed.append(fw)
        for a in range(n):
            for j, (px, py) in enumerate(chips):
                landed = out_refs[a].at[2 * px + py, theirs[a]]
                rc(landed, landed, k=3 * n + 3 * a + j, to=sibling).wait_recv()
        for j, (px, py) in enumerate(chips):
            rc(small_ref, osmall_ref.at[2 * px + py], k=6 * n + j, to=(px, py, c)).wait_recv()
        for cp in first + passed:
            cp.wait_send()

    outs = _pcall(body, in_specs=[ANY] * (n + 1), out_specs=[ANY] * (n + 1),
                  out_shape=[SDS((N_CHIPS,) + b.shape, b.dtype) for b in bigs]
                  + [SDS((N_CHIPS,) + small.shape, small.dtype)],
                  scratch_shapes=[pltpu.SemaphoreType.DMA((6 * n + 3,)), pltpu.SemaphoreType.DMA((6 * n + 3,))],
                  name="gather_weights")(*bigs, small)
    me = 2 * lax.axis_index("x") + lax.axis_index("y")
    return [lax.dynamic_update_index_in_dim(o, own, me, 0) for o, own in zip(outs, list(bigs) + [small])]


def _swap_halves(gs):
    n = len(gs)

    def body(*refs):
        send_sems, recv_sems = refs[2 * n:]
        x, y, c = _coords()
        cps = [_rcopy(refs[a].at[:, 1 - c], refs[n + a], send_sems, recv_sems, a, (x, y, 1 - c))
               for a in range(n)]
        for cp in cps:
            cp.start()
        for cp in cps:
            cp.wait()

    return _pcall(body, in_specs=[ANY] * n, out_specs=[ANY] * n,
                  out_shape=[SDS((N_CHIPS,) + g.shape[2:], g.dtype) for g in gs],
                  scratch_shapes=[pltpu.SemaphoreType.DMA((n,)), pltpu.SemaphoreType.DMA((n,))],
                  name="swap_halves")(*gs)


def _sum_cores(gs, rs, half_idx):
    n = len(gs)
    ns = 2
    in_specs, out_specs, out_shape = [], [], []
    for g in gs:
        _, _, h, w = g.shape
        in_specs.append(pl.BlockSpec((1, 1, h // ns, w), lambda b, i, c_ref: (b, c_ref[0], i, 0)))
    for g in gs:
        _, _, h, w = g.shape
        spec = pl.BlockSpec((1, h // ns, w), lambda b, i, c_ref: (b, i, 0))
        in_specs.append(spec)
        out_specs += [spec, spec]
        out_shape += [SDS((N_CHIPS, h, w), F32), SDS((N_CHIPS, h, w), BF16)]

    def body(c_ref, *refs):
        del c_ref
        for a in range(n):
            tot = refs[a][0] + refs[n + a][...]
            refs[2 * n + 2 * a][...] = tot
            refs[2 * n + 2 * a + 1][...] = tot.astype(BF16)

    outs = _pcall(body, grid_spec=pltpu.PrefetchScalarGridSpec(
        num_scalar_prefetch=1, grid=(N_CHIPS, ns), in_specs=in_specs, out_specs=out_specs),
        out_shape=out_shape, compiler_params=_cp(("arbitrary", "arbitrary")), name="sum_cores")(
            half_idx, *gs, *rs)
    return outs[0::2], outs[1::2]


def _scatter_chips(hs):
    n = len(hs)

    def body(*refs):
        send_sems, recv_sems = refs[2 * n:]
        x, y, c = _coords()
        cps = []
        for a in range(n):
            for j, (px, py) in enumerate(_other_chips(x, y)):
                cps.append(_rcopy(refs[a].at[2 * px + py], refs[n + a].at[j], send_sems, recv_sems,
                                  3 * a + j, (px, py, c)))
        for cp in cps:
            cp.start()
        for cp in cps:
            cp.wait()

    return _pcall(body, in_specs=[ANY] * n, out_specs=[ANY] * n,
                  out_shape=[SDS((3,) + h.shape[1:], h.dtype) for h in hs],
                  scratch_shapes=[pltpu.SemaphoreType.DMA((3 * n,)), pltpu.SemaphoreType.DMA((3 * n,))],
                  name="scatter_chips")(*hs)


def _sum_chips(hs, xs, chip_idx):
    n = len(hs)
    ns = 2
    in_specs, out_specs, out_shape = [], [], []
    for h_arr in hs:
        _, h, w = h_arr.shape
        in_specs.append(pl.BlockSpec((1, h // ns, w), lambda i, c_ref: (c_ref[0], i, 0)))
    for h_arr in hs:
        _, h, w = h_arr.shape
        in_specs.append(pl.BlockSpec((3, h // ns, w), lambda i, c_ref: (0, i, 0)))
        out_specs.append(pl.BlockSpec((h // ns, w), lambda i, c_ref: (i, 0)))
        out_shape.append(SDS((h, w), F32))

    def body(c_ref, *refs):
        del c_ref
        for a in range(n):
            x_ref = refs[n + a]
            refs[2 * n + a][...] = (refs[a][0] + x_ref[0].astype(F32) + x_ref[1].astype(F32)
                                    + x_ref[2].astype(F32))

    return _pcall(body, grid_spec=pltpu.PrefetchScalarGridSpec(
        num_scalar_prefetch=1, grid=(ns,), in_specs=in_specs, out_specs=out_specs),
        out_shape=out_shape, compiler_params=_cp(("arbitrary",)), name="sum_chips")(chip_idx, *hs, *xs)


def _swap_totals(tots):
    n = len(tots)

    def body(*refs):
        send_sems, recv_sems = refs[2 * n:]
        x, y, c = _coords()
        cps = [_rcopy(refs[a], refs[n + a], send_sems, recv_sems, a, (x, y, 1 - c)) for a in range(n)]
        for cp in cps:
            cp.start()
        for cp in cps:
            cp.wait()

    return _pcall(body, in_specs=[ANY] * n, out_specs=[ANY] * n,
                  out_shape=[SDS(t.shape, t.dtype) for t in tots],
                  scratch_shapes=[pltpu.SemaphoreType.DMA((n,)), pltpu.SemaphoreType.DMA((n,))],
                  name="swap_totals")(*tots)


def _allreduce_small(buf):
    rows = buf.shape[0]

    def body(b_ref, o_ref, g_ref, send_sems, recv_sems):
        x, y, c = _coords()
        me = 4 * x + 2 * y + c
        g_ref[me] = b_ref[...]
        cps = []
        for k in range(1, 8):
            px = 1 - x if k & 4 else x
            py = 1 - y if k & 2 else y
            pc = 1 - c if k & 1 else c
            cps.append(pltpu.make_async_remote_copy(
                src_ref=b_ref, dst_ref=g_ref.at[me], send_sem=send_sems.at[k - 1],
                recv_sem=recv_sems.at[k - 1], device_id=(px, py, pc), device_id_type=MESH))
        for cp in cps:
            cp.start()
        for cp in cps:
            cp.wait()
        acc = g_ref[0]
        for d in range(1, 8):
            acc = acc + g_ref[d]
        o_ref[...] = acc

    vm = pl.BlockSpec(memory_space=pltpu.VMEM)
    return _pcall(body, in_specs=[vm], out_specs=vm, out_shape=SDS(buf.shape, F32),
                  scratch_shapes=[pltpu.VMEM((8, rows, 128), F32), pltpu.SemaphoreType.DMA((7,)),
                                  pltpu.SemaphoreType.DMA((7,))],
                  compiler_params=pltpu.CompilerParams(vmem_limit_bytes=VMEM_LIMIT),
                  name="allreduce_small")(buf)


def _pack_flat(parts, mult):
    rows, offs, r0 = [], [], 0
    for p in parts:
        f = p.reshape(-1)
        f = jnp.pad(f, (0, (-f.shape[0]) % (mult * 128))).reshape(-1, 128)
        rows.append(f)
        offs.append((r0, p.shape))
        r0 += f.shape[0]
    return jnp.concatenate(rows, axis=0), offs


def _unpack_flat(buf, offs):
    out = []
    for r0, shape in offs:
        n = 1
        for s in shape:
            n *= s
        nr = -(-n // 128)
        out.append(buf[r0:r0 + nr].reshape(-1)[:n].reshape(shape))
    return out


def _gates_to_rows(a, heads, chunk, cpb):
    t = a.shape[0]
    return a.reshape(t // (chunk * cpb), cpb, chunk, heads).transpose(0, 3, 1, 2)


def _rows_to_gates(a):
    nb, heads, cpb, chunk = a.shape
    return a.transpose(0, 2, 3, 1).reshape(nb * cpb * chunk, heads)


def kernel(x, norm_w, gdn_w_in, gdn_conv_w, gdn_a_log, gdn_dt_bias, gdn_norm_w, gdn_w_out, ssd_w_in, ssd_conv_w, ssd_conv_b, ssd_dt_bias, ssd_a_log, ssd_d, ssd_norm_w, ssd_w_out, final_norm_w, loss_target, m_norm_w, m_gdn_w_in, m_gdn_conv_w, m_gdn_a_log, m_gdn_dt_bias, m_gdn_norm_w, m_gdn_w_out, m_ssd_w_in, m_ssd_conv_w, m_ssd_conv_b, m_ssd_dt_bias, m_ssd_a_log, m_ssd_d, m_ssd_norm_w, m_ssd_w_out, m_final_norm_w, v_norm_w, v_gdn_w_in, v_gdn_conv_w, v_gdn_a_log, v_gdn_dt_bias, v_gdn_norm_w, v_gdn_w_out, v_ssd_w_in, v_ssd_conv_w, v_ssd_conv_b, v_ssd_dt_bias, v_ssd_a_log, v_ssd_d, v_ssd_norm_w, v_ssd_w_out, v_final_norm_w):
    ws = dict(norm_w=norm_w, gdn_w_in=gdn_w_in, gdn_conv_w=gdn_conv_w, gdn_a_log=gdn_a_log,
              gdn_dt_bias=gdn_dt_bias, gdn_norm_w=gdn_norm_w, gdn_w_out=gdn_w_out, ssd_w_in=ssd_w_in,
              ssd_conv_w=ssd_conv_w, ssd_conv_b=ssd_conv_b, ssd_dt_bias=ssd_dt_bias,
              ssd_a_log=ssd_a_log, ssd_d=ssd_d, ssd_norm_w=ssd_norm_w, ssd_w_out=ssd_w_out,
              final_norm_w=final_norm_w)
    ms = dict(norm_w=m_norm_w, gdn_w_in=m_gdn_w_in, gdn_conv_w=m_gdn_conv_w, gdn_a_log=m_gdn_a_log,
              gdn_dt_bias=m_gdn_dt_bias, gdn_norm_w=m_gdn_norm_w, gdn_w_out=m_gdn_w_out,
              ssd_w_in=m_ssd_w_in, ssd_conv_w=m_ssd_conv_w, ssd_conv_b=m_ssd_conv_b,
              ssd_dt_bias=m_ssd_dt_bias, ssd_a_log=m_ssd_a_log, ssd_d=m_ssd_d,
              ssd_norm_w=m_ssd_norm_w, ssd_w_out=m_ssd_w_out, final_norm_w=m_final_norm_w)
    vs = dict(norm_w=v_norm_w, gdn_w_in=v_gdn_w_in, gdn_conv_w=v_gdn_conv_w, gdn_a_log=v_gdn_a_log,
              gdn_dt_bias=v_gdn_dt_bias, gdn_norm_w=v_gdn_norm_w, gdn_w_out=v_gdn_w_out,
              ssd_w_in=v_ssd_w_in, ssd_conv_w=v_ssd_conv_w, ssd_conv_b=v_ssd_conv_b,
              ssd_dt_bias=v_ssd_dt_bias, ssd_a_log=v_ssd_a_log, ssd_d=v_ssd_d,
              ssd_norm_w=v_ssd_norm_w, ssd_w_out=v_ssd_w_out, final_norm_w=v_final_norm_w)
    names = list(ws)
    cx, cy, cc = _coords()
    chip = 2 * cx + cy
    t = x.shape[1]
    x0 = x.reshape(t, D)
    tgt = loss_target.reshape(t, D)

    bigs = [p[0].astype(BF16) for p in (gdn_w_in, gdn_w_out, ssd_w_in, ssd_w_out)]
    small, small_offs = _pack_flat([gdn_conv_w[0], ssd_conv_w[0], ssd_conv_b[0], ssd_norm_w[0]], 8)
    a_gi, a_go, a_si, a_so, gsmall = _gather_weights(bigs, small)
    w_gi = jnp.concatenate([a_gi[b] for b in range(4)], axis=1)
    w_si = jnp.concatenate([a_si[b] for b in range(4)], axis=1)
    w_go = a_go.reshape(4 * OUT_SHARD, D)
    w_so = a_so.reshape(4 * OUT_SHARD, D)
    sm = [_unpack_flat(gsmall[b], small_offs) for b in range(4)]
    g_cw = jnp.concatenate([sm[b][0] for b in range(4)], axis=1)
    s_cw = jnp.concatenate([sm[b][1] for b in range(4)], axis=1)
    s_cb = jnp.concatenate([sm[b][2] for b in range(4)], axis=0)[None]
    s_nw = jnp.concatenate([sm[b][3] for b in range(4)], axis=0)[None]

    def pad_small(w):
        return jnp.concatenate([w, jnp.zeros((D, SMALL_W - w.shape[1]), w.dtype)], axis=1)

    wg_main, wg_small = w_gi[:, :G_MAIN], pad_small(w_gi[:, G_MAIN:])
    ws_main, ws_small = w_si[:, :S_MAIN], pad_small(w_si[:, S_MAIN:])
    zero_b = jnp.zeros((1, G_CONV), F32)
    nw0, nw1 = norm_w[0:1], norm_w[1:2]
    fw = final_norm_w[None]
    g_alog = gdn_a_log.reshape(GH, 1, 1)
    g_dtb = gdn_dt_bias.reshape(GH, 1, 1)
    g_nw = gdn_norm_w.reshape(1, GDK)
    s_dtb = ssd_dt_bias.reshape(SG, SR).T.reshape(SR, SG, 1, 1)
    s_alog = ssd_a_log.reshape(SG, SR).T.reshape(SR, SG, 1, 1)
    s_d = ssd_d.reshape(SG, SR).T.reshape(SR, SG, 1, 1)

    hid0 = _rms_fwd(x0, nw0, "rms0")
    pg = _matmul(hid0, wg_main, "nn", "gdn_in_proj", tn=G_MAIN // 4)
    pg_small = _matmul(hid0, wg_small, "nn", "gdn_in_proj_small", tn=SMALL_W)
    post_q = _l2norm_scaled(GDK ** -0.5)
    post_k = _l2norm_scaled(1.0)
    q = _conv_fwd(pg, 0, g_cw[:, :G_QK], zero_b[:, :G_QK], GDK, post_q, "gdn_conv_q")
    k = _conv_fwd(pg, G_QK, g_cw[:, G_QK:2 * G_QK], zero_b[:, :G_QK], GDK, post_k, "gdn_conv_k")
    v = _conv_fwd(pg, 2 * G_QK, g_cw[:, 2 * G_QK:], zero_b[:, :G_V], GDK, _silu, "gdn_conv_v")
    braw = _gates_to_rows(pg_small[:, :GH], GH, GCH, GDN_CPB)
    araw = _gates_to_rows(pg_small[:, GH:2 * GH], GH, GCH, GDN_CPB)
    o, g_sall, g_tall = _gdn_fwd(q, k, v, braw, araw, g_alog, g_dtb)
    y0 = _gate_fwd(_gdn_gate, o, pg, G_CONV, g_nw, "gdn_gate")
    x1 = _matmul(y0, w_go, "nn", "gdn_out_proj", add=x0)

    hid1 = _rms_fwd(x1, nw1, "rms1")
    ps = _matmul(hid1, ws_main, "nn", "ssd_in_proj", tn=S_MAIN // 4)
    ps_small = _matmul(hid1, ws_small, "nn", "ssd_in_proj_small", tn=SMALL_W)
    c_x, c_b, c_c = S_INNER, 2 * S_INNER, 2 * S_INNER + SG * SN
    post_s = _silu
    xs = _conv_fwd(ps, c_x, s_cw[:, :S_INNER], s_cb[:, :S_INNER], SR * SP, post_s, "ssd_conv_x")
    bm = _conv_fwd(ps, c_b, s_cw[:, S_INNER:S_INNER + SG * SN], s_cb[:, S_INNER:S_INNER + SG * SN], SN,
                   post_s, "ssd_conv_b")
    cm = _conv_fwd(ps, c_c, s_cw[:, S_INNER + SG * SN:], s_cb[:, S_INNER + SG * SN:], SN, post_s,
                   "ssd_conv_c")
    nbs = t // (SCH * SSD_CPB)
    dtraw = _gates_to_rows(ps_small[:, :SH], SH, SCH, SSD_CPB)
    dtraw = dtraw.reshape(nbs, SG, SR, SSD_CPB, SCH).transpose(0, 3, 2, 1, 4).reshape(nbs, SSD_CPB * SR, SG, 1, SCH)
    yss, s_sall = _ssd_fwd(xs, bm, cm, dtraw, s_dtb, s_alog, s_d)
    y1 = _gate_fwd(_ssd_gate, yss, ps, 0, s_nw, "ssd_gate")
    x2 = _matmul(y1, w_so, "nn", "ssd_out_proj", add=x1)

    dx2, d_fw, loss_row = _final_loss(x2, fw, tgt, "final_loss")

    dy1 = _matmul(dx2, w_so, "nt", "ssd_out_dx", out_dtype=BF16, tn=S_INNER)
    d_wso = _matmul(y1, dx2, "tn", "ssd_out_dw")
    dyss, dz_s, d_snw = _gate_bwd(_ssd_gate, yss, ps, 0, s_nw, dy1, True, "ssd_gate_bwd", Z_EXT)
    dxs, dbm, dcm, ddtraw, d_sdtb, d_salog, d_sd = _ssd_bwd(xs, bm, cm, dtraw, s_dtb, s_alog, s_d, s_sall, dyss)
    dps, dwx, dbx = _conv_dpre(ps, c_x, s_cw[:, :S_INNER], s_cb[:, :S_INNER], SR * SP, post_s, dxs, "ssd_dpre_x",
                               G_CONV, 0)
    dps, dwb, dbb = _conv_dpre(ps, c_b, s_cw[:, S_INNER:S_INNER + SG * SN], s_cb[:, S_INNER:S_INNER + SG * SN],
                               SN, post_s, dbm, "ssd_dpre_b", G_CONV, S_INNER, dps)
    dps, dwc, dbc = _conv_dpre(ps, c_c, s_cw[:, S_INNER + SG * SN:], s_cb[:, S_INNER + SG * SN:], SN, post_s,
                               dcm, "ssd_dpre_c", G_CONV, S_INNER + SG * SN, dps)
    d_scw = jnp.concatenate([dwx, dwb, dwc], axis=1)
    d_scb = jnp.concatenate([dbx, dbb, dbc], axis=1)
    dxbc = _conv_t(dps, s_cw, "ssd_conv_t")
    ddt = ddtraw.reshape(nbs, SSD_CPB, SR, SG, SCH).transpose(0, 3, 2, 1, 4).reshape(nbs, SH, SSD_CPB, SCH)
    ddt = _rows_to_gates(ddt)
    dsm_s = jnp.concatenate([ddt, jnp.zeros((t, SMALL_W - SH), F32)], axis=1).astype(BF16)
    dz_s = lax.dynamic_update_slice(dz_s, dsm_s, (0, S_INNER))
    ws_zx = jnp.concatenate([ws_main[:, :S_INNER], ws_small], axis=1)
    dhid1 = _matmul(dz_s, ws_zx, "nt", "ssd_in_dx_z", tk=Z_EXT)
    dhid1 = _matmul(dxbc, ws_main[:, S_INNER:], "nt", "ssd_in_dx_xbc", add=dhid1)
    dw_zx = _matmul(hid1, dz_s, "tn", "ssd_in_dw_z", tn=Z_EXT, tk=1024)
    d_wsi = jnp.concatenate([dw_zx[:, :S_INNER], _matmul(hid1, dxbc, "tn", "ssd_in_dw_xbc"),
                             dw_zx[:, S_INNER:S_INNER + SH]], axis=1)
    dx1, d_nw1 = _rms_bwd(x1, nw1, dx2, dhid1, "rms1_bwd")

    dy0 = _matmul(dx1, w_go, "nt", "gdn_out_dx", out_dtype=BF16, tn=G_V)
    d_wgo = _matmul(y0, dx1, "tn", "gdn_out_dw")
    do, dz_g, d_gnw = _gate_bwd(_gdn_gate, o, pg, G_CONV, g_nw, dy0, False, "gdn_gate_bwd", Z_EXT)
    dq, dk, dv, dbraw, daraw, d_galog, d_gdtb = _gdn_bwd(q, k, v, braw, araw, g_alog, g_dtb, g_sall, g_tall, do)
    dpg, dwq, _ = _conv_dpre(pg, 0, g_cw[:, :G_QK], zero_b[:, :G_QK], GDK, post_q, dq, "gdn_dpre_q", G_CONV, 0)
    dpg, dwk, _ = _conv_dpre(pg, G_QK, g_cw[:, G_QK:2 * G_QK], zero_b[:, :G_QK], GDK, post_k, dk, "gdn_dpre_k",
                             G_CONV, G_QK, dpg)
    dpg, dwv, _ = _conv_dpre(pg, 2 * G_QK, g_cw[:, 2 * G_QK:], zero_b[:, :G_V], GDK, _silu, dv, "gdn_dpre_v",
                             G_CONV, 2 * G_QK, dpg)
    d_gcw = jnp.concatenate([dwq, dwk, dwv], axis=1)
    dqkv = _conv_t(dpg, g_cw, "gdn_conv_t")
    dsm_g = jnp.concatenate([_rows_to_gates(dbraw), _rows_to_gates(daraw),
                             jnp.zeros((t, SMALL_W - 2 * GH), F32)], axis=1).astype(BF16)
    dz_g = lax.dynamic_update_slice(dz_g, dsm_g, (0, G_V))
    wg_zx = jnp.concatenate([wg_main[:, G_CONV:], wg_small], axis=1)
    dhid0 = _matmul(dqkv, wg_main[:, :G_CONV], "nt", "gdn_in_dx_qkv")
    dhid0 = _matmul(dz_g, wg_zx, "nt", "gdn_in_dx_z", add=dhid0, tk=Z_EXT)
    dw_zx = _matmul(hid0, dz_g, "tn", "gdn_in_dw_z", tn=Z_EXT, tk=1024)
    d_wgi = jnp.concatenate([_matmul(hid0, dqkv, "tn", "gdn_in_dw_qkv"), dw_zx[:, :G_V + 2 * GH]], axis=1)
    dx0, d_nw0 = _rms_bwd(x0, nw0, dx1, dhid0, "rms0_bwd")

    def in_blocks(dw):
        return dw.reshape(D, N_CHIPS, IN_SHARD).transpose(1, 0, 2).reshape(N_CHIPS, 2, D // 2, IN_SHARD)

    def out_blocks(dw):
        return dw.reshape(N_CHIPS, 2, OUT_SHARD // 2, D)

    gs = [in_blocks(d_wgi), out_blocks(d_wgo), in_blocks(d_wsi), out_blocks(d_wso)]
    from_sib = _swap_halves(gs)
    hsum, hsum_bf = _sum_cores(gs, from_sib, cc.astype(jnp.int32).reshape(1))
    recv = _scatter_chips(hsum_bf)
    tots = _sum_chips(hsum, recv, chip.astype(jnp.int32).reshape(1))
    sib_tots = _swap_totals(tots)
    full = [jnp.concatenate([jnp.where(cc == 0, mine, sib), jnp.where(cc == 0, sib, mine)], axis=0)
            for mine, sib in zip(tots, sib_tots)]
    grads = dict(
        gdn_w_in=full[0].reshape(1, D, IN_SHARD), gdn_w_out=full[1].reshape(1, OUT_SHARD, D),
        ssd_w_in=full[2].reshape(1, D, IN_SHARD), ssd_w_out=full[3].reshape(1, OUT_SHARD, D))

    small_parts = [loss_row, jnp.concatenate([d_nw0, d_nw1], axis=0), d_gcw, d_galog, d_gdtb, d_gnw, d_scw, d_scb,
                   d_sdtb, d_salog, d_sd, d_snw, d_fw]
    sbuf, soffs = _pack_flat(small_parts, 8)
    ssum = _unpack_flat(_allreduce_small(sbuf), soffs)
    (loss_s, g_nw_all, g_gcw, g_galog, g_gdtb, g_gnw, g_scw, g_scb, g_sdtb, g_salog, g_sd, g_snw, g_fw) = ssum

    def my_cols(a, width):
        return lax.dynamic_slice_in_dim(a, chip * width, width, axis=a.ndim - 1)

    grads.update(
        norm_w=g_nw_all, gdn_conv_w=my_cols(g_gcw, 1024)[None], gdn_a_log=g_galog.reshape(1, GH),
        gdn_dt_bias=g_gdtb.reshape(1, GH), gdn_norm_w=g_gnw.reshape(1, GDK),
        ssd_conv_w=my_cols(g_scw, 1024)[None], ssd_conv_b=my_cols(g_scb, 1024),
        ssd_dt_bias=g_sdtb.reshape(SR, SG).T.reshape(1, SH), ssd_a_log=g_salog.reshape(SR, SG).T.reshape(1, SH),
        ssd_d=g_sd.reshape(SR, SG).T.reshape(1, SH),
        ssd_norm_w=my_cols(g_snw, 512), final_norm_w=g_fw.reshape(D))
    loss = loss_s[0, 0]

    big_names = ("gdn_w_in", "gdn_w_out", "ssd_w_in", "ssd_w_out")
    deltas, new_m, new_v = {}, {}, {}
    for n in big_names:
        deltas[n], new_m[n], new_v[n] = _adamw(ws[n], grads[n], ms[n], vs[n], "adamw_" + n)
    rest = [n for n in names if n not in big_names]
    packs = [_pack_flat([d[n] for n in rest], 8) for d in (ws, grads, ms, vs)]
    outs = _adamw(*[p[0] for p in packs], "adamw_small")
    for d, buf in zip((deltas, new_m, new_v), outs):
        for n, a in zip(rest, _unpack_flat(buf, packs[0][1])):
            d[n] = a

    grad_x = dx0.reshape(1, t, D)
    return (loss, grad_x, *[grads[n] for n in names], *[deltas[n] for n in names],
            *[new_m[n] for n in names], *[new_v[n] for n in names])
```

```python
import functools

import jax
import jax.numpy as jnp
from jax import lax
from jax.experimental import pallas as pl
from jax.experimental.pallas import tpu as pltpu

F32 = jnp.float32
BF16 = jnp.bfloat16
SDS = jax.ShapeDtypeStruct
MESH = pl.DeviceIdType.MESH

D = 1024
EPS = 1e-6
CONV_K = 4
N_CHIPS = 4
GH = 16
GHQ = 8
GDK = 128
GCH = 64
G_QK = 1024
G_V = 2048
G_CONV = 4096
G_MAIN = 6144
G_IN = 6176
SH = 32
SP = 64
SN = 128
SG = 8
SR = 4
SCH = 128
S_INNER = 2048
S_MAIN = 6144
S_IN = 6176
IN_SHARD = 1544
OUT_SHARD = 512
SMALL_W = 128
Z_EXT = 2048 + SMALL_W

ADAM_LR = 0.001
ADAM_B1 = 0.9
ADAM_B2 = 0.999
ADAM_EPS = 1e-08
ADAM_WD = 0.01
ADAM_STEP = 10

VMEM_LIMIT = 56 * 1024 * 1024
BLOCK_ELEMS = 512 * 1024
PLAIN_BLOCK_ELEMS = 1024 * 1024
FWD_COLS = 512
NEG = -1e30


def _pcall(body, **kw):
    return pl.pallas_call(body, **kw)


def _cp(sem=None, vmem=VMEM_LIMIT):
    return pltpu.CompilerParams(dimension_semantics=sem, vmem_limit_bytes=vmem)


@jax.custom_jvp
def _sigmoid(x):
    return 1.0 / (1.0 + jnp.exp(-x))


@_sigmoid.defjvp
def _sigmoid_jvp(primals, tangents):
    s = _sigmoid(primals[0])
    return s, tangents[0] * (s * (1.0 - s))


@jax.custom_jvp
def _silu(x):
    return x * _sigmoid(x)


@_silu.defjvp
def _silu_jvp(primals, tangents):
    x = primals[0]
    s = _sigmoid(x)
    return x * s, tangents[0] * (s * (1.0 + x * (1.0 - s)))


def _softplus(x):
    return jnp.maximum(x, 0.0) + jnp.log(1.0 + jnp.exp(-jnp.abs(x)))


def _rms(x, w):
    return x * lax.rsqrt(jnp.mean(x * x, axis=-1, keepdims=True) + EPS) * w


_DIMS = {"nn": (((2,), (1,)), ((0,), (0,))),
         "nt": (((2,), (2,)), ((0,), (0,))),
         "tn": (((1,), (1,)), ((0,), (0,)))}


def _bdot(a, b, spec):
    return lax.dot_general(a.astype(BF16), b.astype(BF16), _DIMS[spec], preferred_element_type=F32)


@functools.partial(jax.custom_vjp, nondiff_argnums=(2,))
def _bmm(a, b, spec):
    return _bdot(a, b, spec)


def _bmm_fwd(a, b, spec):
    return _bdot(a, b, spec), (a, b)


def _bmm_bwd(spec, res, g):
    a, b = res
    if spec == "nn":
        return _bdot(g, b, "nt"), _bdot(a, g, "tn")
    if spec == "nt":
        return _bdot(g, b, "nn"), _bdot(g, a, "tn")
    return _bdot(b, g, "nt"), _bdot(a, g, "nn")


_bmm.defvjp(_bmm_fwd, _bmm_bwd)


@jax.custom_vjp
def _tri_inv(n):
    t = -n
    p = n
    steps = (n.shape[-1] - 1).bit_length() - 1
    r = lax.broadcasted_iota(jnp.int32, n.shape, 1)
    c = lax.broadcasted_iota(jnp.int32, n.shape, 2)
    t = t + jnp.where(r == c, 1.0, 0.0)
    for _ in range(steps):
        p = _bdot(p, p, "nn")
        t = t + _bdot(t, p, "nn")
    return t


def _tri_inv_fwd(n):
    t = _tri_inv(n)
    return t, t


def _tri_inv_bwd(t, g):
    return (-_bdot(_bdot(t, g, "tn"), t, "nt"),)


_tri_inv.defvjp(_tri_inv_fwd, _tri_inv_bwd)


@jax.custom_vjp
def _tri_inv_known(n, t):
    del n
    return t


def _tri_inv_known_fwd(n, t):
    del n
    return t, t


def _tri_inv_known_bwd(t, g):
    return _tri_inv_bwd(t, g)[0], jnp.zeros_like(t)


_tri_inv_known.defvjp(_tri_inv_known_fwd, _tri_inv_known_bwd)


def _masks(c, lead=1):
    r = lax.broadcasted_iota(jnp.int32, (lead, c, c), 1)
    s = lax.broadcasted_iota(jnp.int32, (lead, c, c), 2)
    return r >= s, r > s, r == s, r <= s


def _row_to_col(row, eye):
    return jnp.sum(jnp.where(eye, row, 0.0), axis=2, keepdims=True)


def _gdn_chunk(q, k, v, braw, araw, alog, dtb, s, t_known=None, want_t=False):
    h = v.shape[0]
    c = v.shape[1]
    rep = h // q.shape[0]
    tril, strict, eye, triu = _masks(c)
    qq = jnp.broadcast_to(q[:, None], (q.shape[0], rep) + q.shape[1:]).reshape(v.shape)
    kk = jnp.broadcast_to(k[:, None], (k.shape[0], rep) + k.shape[1:]).reshape(v.shape)
    beta_row = _sigmoid(braw)
    g_row = -jnp.exp(alog) * _softplus(araw + dtb)
    beta_col = _row_to_col(beta_row, eye)
    g_col = _row_to_col(g_row, eye)
    gc_col = jnp.sum(jnp.where(tril, g_row, 0.0), axis=2, keepdims=True)
    gc_row = jnp.sum(jnp.where(triu, g_col, 0.0), axis=1, keepdims=True)
    gc_last = jnp.sum(g_row, axis=2, keepdims=True)
    lmat = jnp.exp(jnp.where(tril, gc_col - gc_row, NEG))
    kb = kk * beta_col
    vb = v * beta_col
    n = jnp.where(strict, _bmm(kb, kk, "nt") * lmat, 0.0)
    t = _tri_inv(n) if t_known is None else _tri_inv_known(n, t_known)
    e_col = jnp.exp(gc_col)
    u = _bmm(t, vb, "nn")
    w = _bmm(t, kb * e_col, "nn")
    attn = _bmm(qq, kk, "nt") * lmat
    q_dec = qq * e_col
    k_dec = kk * jnp.exp(gc_last - gc_col)
    v_new = u - _bmm(w, s, "nn")
    o = _bmm(q_dec, s, "nn") + _bmm(attn, v_new, "nn")
    s_new = s * jnp.exp(gc_last) + _bmm(k_dec, v_new, "tn")
    return (o, s_new, t) if want_t else (o, s_new)


def _ssd_chunk(xs, bm, cm, dtraw, dtb, alog, dskip, s):
    c = xs.shape[1]
    tril, _, eye, triu = _masks(c)
    lane = lax.broadcasted_iota(jnp.int32, (1, 1, SR * SP), 2)
    prow = lax.broadcasted_iota(jnp.int32, (1, SR * SP, 1), 1)
    cb = _bmm(cm, bm, "nt")
    cs = _bmm(cm, s, "nt")

    def per_head(vals, idx):
        out = vals[SR - 1]
        for r in reversed(range(SR - 1)):
            out = jnp.where(idx < (r + 1) * SP, vals[r], out)
        return out

    dt_cols, e_cols, lmats, dstates, declast = [], [], [], [], []
    for r in range(SR):
        dt_row = _softplus(dtraw[r] + dtb[r])
        adt_row = -jnp.exp(alog[r]) * dt_row
        dt_cols.append(_row_to_col(dt_row, eye))
        adt_col = _row_to_col(adt_row, eye)
        acs_col = jnp.sum(jnp.where(tril, adt_row, 0.0), axis=2, keepdims=True)
        acs_row = jnp.sum(jnp.where(triu, adt_col, 0.0), axis=1, keepdims=True)
        acs_last = jnp.sum(adt_row, axis=2, keepdims=True)
        lmats.append(jnp.exp(jnp.where(tril, acs_col - acs_row, NEG)))
        e_cols.append(jnp.exp(acs_col))
        dstates.append(jnp.exp(acs_last - acs_col))
        declast.append(jnp.exp(acs_last))
    xd = xs * per_head(dt_cols, lane)
    y = per_head([_bmm(cb * lmats[r], xd, "nn") for r in range(SR)], lane)
    states = per_head([_bmm(xd, bm * dstates[r], "tn") for r in range(SR)], prow)
    y = y + cs * per_head(e_cols, lane) + xs * per_head([dskip[r] for r in range(SR)], lane)
    s_new = s * per_head(declast, prow) + states
    return y, s_new


def _matmul(a, b, mode, name, out_dtype=F32, add=None, tm=1024, tn=1024, tk=2048):
    if mode == "nn":
        (m, k), n = a.shape, b.shape[1]
    elif mode == "nt":
        (m, k), n = a.shape, b.shape[0]
    else:
        (k, m), n = a.shape, b.shape[1]
    tm, tn, tk = min(tm, m), min(tn, n), min(tk, k)
    assert m % tm == 0 and n % tn == 0 and k % tk == 0, (name, m, n, k)
    nk = k // tk
    dims = {"nn": (((1,), (0,)), ((), ())), "nt": (((1,), (1,)), ((), ())),
            "tn": (((0,), (0,)), ((), ()))}[mode]
    a_spec = {"nn": pl.BlockSpec((tm, tk), lambda i, j, kk: (i, kk)),
              "nt": pl.BlockSpec((tm, tk), lambda i, j, kk: (i, kk)),
              "tn": pl.BlockSpec((tk, tm), lambda i, j, kk: (kk, i))}[mode]
    b_spec = {"nn": pl.BlockSpec((tk, tn), lambda i, j, kk: (kk, j)),
              "nt": pl.BlockSpec((tn, tk), lambda i, j, kk: (j, kk)),
              "tn": pl.BlockSpec((tk, tn), lambda i, j, kk: (kk, j))}[mode]
    o_spec = pl.BlockSpec((tm, tn), lambda i, j, kk: (i, j))
    has_add = add is not None

    def body(*refs):
        a_ref, b_ref = refs[:2]
        add_ref = refs[2] if has_add else None
        o_ref = refs[2 + has_add]
        part = lax.dot_general(a_ref[...].astype(BF16), b_ref[...].astype(BF16), dims,
                               preferred_element_type=F32)

        def finish(r):
            if has_add:
                r = r + add_ref[...].astype(F32)
            o_ref[...] = r.astype(o_ref.dtype)

        if nk == 1:
            finish(part)
            return
        acc_ref = refs[3 + has_add]
        kk = pl.program_id(2)

        @pl.when(kk == 0)
        def _():
            acc_ref[...] = part

        @pl.when(jnp.logical_and(kk > 0, kk < nk - 1))
        def _():
            acc_ref[...] += part

        @pl.when(kk == nk - 1)
        def _():
            finish(acc_ref[...] + part)

    ins = [a, b] + ([add] if has_add else [])
    in_specs = [a_spec, b_spec] + ([o_spec] if has_add else [])
    scratch = [] if nk == 1 else [pltpu.VMEM((tm, tn), F32)]
    return _pcall(body, grid=(m // tm, n // tn, nk), in_specs=in_specs, out_specs=o_spec,
                  out_shape=SDS((m, n), out_dtype), scratch_shapes=scratch,
                  compiler_params=_cp(("parallel", "parallel", "arbitrary")), name=name)(*ins)


def _ew(fn, ins, in_specs, out_shape, out_specs, grid, name):
    n_in = len(ins)

    def body(*refs):
        outs = fn(*[r[...] for r in refs[:n_in]])
        for r, o in zip(refs[n_in:], outs):
            r[...] = o.astype(r.dtype)

    return _pcall(body, grid=grid, in_specs=in_specs, out_specs=out_specs, out_shape=out_shape,
                  compiler_params=_cp(("arbitrary",) * len(grid)), name=name)(*ins)


def _ew_vjp(fn, ins, in_specs, cts, ct_specs, wrt, g_shape, g_specs, acc, grid, name):
    n_in, n_ct = len(ins), len(cts)

    def body(*refs):
        vals = [r[...].astype(F32) for r in refs[:n_in]]
        outs, vjp = jax.vjp(fn, *vals)
        g_all = vjp(tuple(r[...].astype(F32) for r in refs[n_in:n_in + n_ct]))
        for pos, (i, g_ref) in enumerate(zip(wrt, refs[n_in + n_ct:])):
            g = g_all[i]
            if pos in acc:
                first = functools.reduce(
                    jnp.logical_and, [pl.program_id(ax) == 0 for ax in range(acc[pos], len(grid))])

                @pl.when(first)
                def _():
                    g_ref[...] = jnp.zeros_like(g_ref)

                g_ref[...] += g.astype(g_ref.dtype)
            else:
                g_ref[...] = g.astype(g_ref.dtype)

    return _pcall(body, grid=grid, in_specs=list(in_specs) + list(ct_specs), out_specs=g_specs,
                  out_shape=g_shape, compiler_params=_cp(("arbitrary",) * len(grid)),
                  name=name)(*ins, *cts)


def _row_spec(tm, n):
    return pl.BlockSpec((tm, n), lambda i: (i, 0))


def _par_spec(n):
    return pl.BlockSpec((1, n), lambda i: (0, 0))


def _rms_fwd(x, w, name):
    t = x.shape[0]
    tm = min(t, 512)
    return _ew(lambda xv, wv: (_rms(xv, wv),), [x, w], [_row_spec(tm, D), _par_spec(D)],
               [SDS((t, D), BF16)], [_row_spec(tm, D)], (t // tm,), name)[0]


def _rms_bwd(x, w, dres, dhid, name):
    t = x.shape[0]
    tm = min(t, 512)
    return _ew_vjp(lambda xv, wv: (xv, _rms(xv, wv)), [x, w], [_row_spec(tm, D), _par_spec(D)],
                   [dres, dhid], [_row_spec(tm, D), _row_spec(tm, D)], (0, 1),
                   [SDS((t, D), F32), SDS((1, D), F32)], [_row_spec(tm, D), _par_spec(D)],
                   {1: 0}, (t // tm,), name)


def _final_loss(x, w, tgt, name):
    t = x.shape[0]
    tm = min(t, 512)

    def body(x_ref, w_ref, t_ref, dx_ref, dw_ref, l_ref):
        @pl.when(pl.program_id(0) == 0)
        def _():
            dw_ref[...] = jnp.zeros_like(dw_ref)
            l_ref[...] = jnp.zeros_like(l_ref)

        xv, wv = x_ref[...], w_ref[...]
        rstd = lax.rsqrt(jnp.mean(xv * xv, axis=-1, keepdims=True) + EPS)
        xh = xv * rstd
        err = xh * wv - t_ref[...]
        l_ref[...] += 0.5 * jnp.sum(jnp.mean(err * err, axis=-1, keepdims=True), axis=0, keepdims=True)
        dy = err * (1.0 / D)
        dw_ref[...] += jnp.sum(dy * xh, axis=0, keepdims=True)
        dxh = dy * wv
        dx_ref[...] = rstd * (dxh - xh * jnp.mean(dxh * xh, axis=-1, keepdims=True))

    return _pcall(body, grid=(t // tm,), in_specs=[_row_spec(tm, D), _par_spec(D), _row_spec(tm, D)],
                  out_specs=[_row_spec(tm, D), _par_spec(D), _par_spec(128)],
                  out_shape=[SDS((t, D), F32), SDS((1, D), F32), SDS((1, 128), F32)],
                  compiler_params=_cp(("arbitrary",)), name=name)(x, w, tgt)


def _conv_taps(ext, w_ref, tm, lo):
    n = ext.shape[0]
    acc = None
    for j in range(CONV_K):
        shift = (CONV_K - 1 - j) if lo else (n - (CONV_K - 1 - j)) % n
        rolled = pltpu.roll(ext, shift, 0) if shift else ext
        term = w_ref[pl.ds(j, 1), :] * rolled[lo:lo + tm]
        acc = term if acc is None else acc + term
    return acc


def _conv_pre_specs(tm, ct, col0):
    hb = tm // 8
    return [pl.BlockSpec((tm, ct), lambda j, i: (i, col0 + j)),
            pl.BlockSpec((8, ct), lambda j, i: (jnp.maximum(i * hb - 1, 0), col0 + j)),
            pl.BlockSpec((CONV_K, ct), lambda j, i: (0, j)),
            pl.BlockSpec((1, ct), lambda j, i: (0, j))]


def _conv_pre_value(x_ref, xh_ref, w_ref, b_ref, tm):
    halo = jnp.where(pl.program_id(1) > 0, xh_ref[...], 0.0)
    ext = jnp.concatenate([halo, x_ref[...]], axis=0)
    return _conv_taps(ext, w_ref, tm, 8) + b_ref[...], ext


def _conv_fwd(x, col0, w, b, ct, post, name):
    t = x.shape[0]
    tpb = FWD_COLS // ct
    wide = tpb * ct
    tm = min(t, PLAIN_BLOCK_ELEMS // wide)
    nt = w.shape[1] // ct
    assert col0 % wide == 0 and nt % tpb == 0, name

    def body(x_ref, xh_ref, w_ref, b_ref, o_ref):
        pre, _ = _conv_pre_value(x_ref, xh_ref, w_ref, b_ref, tm)
        for k in range(tpb):
            o_ref[k] = post(pre[:, k * ct:(k + 1) * ct])

    return _pcall(body, grid=(nt // tpb, t // tm), in_specs=_conv_pre_specs(tm, wide, col0 // wide),
                  out_specs=pl.BlockSpec((tpb, tm, ct), lambda j, i: (j, i, 0)),
                  out_shape=SDS((nt, t, ct), F32), compiler_params=_cp(("arbitrary", "arbitrary")),
                  name=name)(x, x, w, b)


def _conv_dpre(x, col0, w, b, ct, post, dout, name, c_total, c_off, into=None):
    t = x.shape[0]
    tm = min(t, BLOCK_ELEMS // ct)
    nt = w.shape[1] // ct
    chained = into is not None

    def body(*refs):
        x_ref, xh_ref, w_ref, b_ref, do_ref = refs[:5]
        dp_ref, dw_ref, db_ref = refs[5 + chained:]
        pre, ext = _conv_pre_value(x_ref, xh_ref, w_ref, b_ref, tm)
        _, vjp = jax.vjp(post, pre)
        dpre = vjp(do_ref[0])[0]
        dp_ref[...] = dpre

        @pl.when(pl.program_id(1) == 0)
        def _():
            dw_ref[...] = jnp.zeros_like(dw_ref)
            db_ref[...] = jnp.zeros_like(db_ref)

        for j in range(CONV_K):
            xs = (pltpu.roll(ext, CONV_K - 1 - j, 0) if j < CONV_K - 1 else ext)[8:8 + tm]
            dw_ref[pl.ds(j, 1), :] += jnp.sum(dpre * xs, axis=0, keepdims=True)
        db_ref[...] += jnp.sum(dpre, axis=0, keepdims=True)

    c = w.shape[1]
    o0 = c_off // ct
    return _pcall(body, grid=(nt, t // tm),
                  in_specs=_conv_pre_specs(tm, ct, col0 // ct)
                  + [pl.BlockSpec((1, tm, ct), lambda j, i: (j, i, 0))] + ([ANY] if chained else []),
                  out_specs=[pl.BlockSpec((tm, ct), lambda j, i: (i, o0 + j)),
                             pl.BlockSpec((CONV_K, ct), lambda j, i: (0, j)),
                             pl.BlockSpec((1, ct), lambda j, i: (0, j))],
                  out_shape=[SDS((t, c_total), F32), SDS((CONV_K, c), F32), SDS((1, c), F32)],
                  input_output_aliases={5: 0} if chained else {},
                  compiler_params=_cp(("arbitrary", "arbitrary")), name=name)(
                      x, x, w, b, dout, *([into] if chained else []))


def _conv_t(dpre, w, name):
    t, c = dpre.shape
    ct = min(c, 512)
    tm = min(t, PLAIN_BLOCK_ELEMS // ct)
    hb = tm // 8
    last = t // tm - 1

    def body(d_ref, dh_ref, w_ref, o_ref):
        halo = jnp.where(pl.program_id(1) < last, dh_ref[...], 0.0)
        ext = jnp.concatenate([d_ref[...], halo], axis=0)
        o_ref[...] = _conv_taps(ext, w_ref, tm, 0).astype(o_ref.dtype)

    return _pcall(body, grid=(c // ct, t // tm),
                  in_specs=[pl.BlockSpec((tm, ct), lambda j, i: (i, j)),
                            pl.BlockSpec((8, ct), lambda j, i: (jnp.minimum((i + 1) * hb, t // 8 - 1), j)),
                            pl.BlockSpec((CONV_K, ct), lambda j, i: (0, j))],
                  out_specs=pl.BlockSpec((tm, ct), lambda j, i: (i, j)),
                  out_shape=SDS((t, c), BF16), compiler_params=_cp(("arbitrary", "arbitrary")),
                  name=name)(dpre, dpre, w)


def _l2norm_scaled(scale):
    def post(pre):
        a = _silu(pre)
        return a * lax.rsqrt(jnp.sum(a * a, axis=-1, keepdims=True) + EPS) * scale
    return post


GDN_HB = 16
GDN_CPB = 4
SSD_GB = 8
SSD_CPB = 1


def _gdn_specs(nb, rev):
    hb, cpb, tc = GDN_HB, GDN_CPB, GDN_CPB * GCH
    blk = (lambda n: nb - 1 - n) if rev else (lambda n: n)
    seq = lambda h: pl.BlockSpec((h, tc, GDK), lambda g, n: (g, blk(n), 0))
    gate = pl.BlockSpec((1, hb, cpb, GCH), lambda g, n: (blk(n), g, 0, 0))
    par = pl.BlockSpec((hb, 1, 1), lambda g, n: (g, 0, 0))
    state = pl.BlockSpec((cpb, hb, GDK, GDK), lambda g, n: (blk(n), g, 0, 0))
    tinv = pl.BlockSpec((cpb, hb, GCH, GCH), lambda g, n: (blk(n), g, 0, 0))
    return seq, gate, par, state, tinv


def _gdn_fwd(q, k, v, braw, araw, alog, dtb):
    t = v.shape[1]
    hb, cpb = GDN_HB, GDN_CPB
    nb = t // (cpb * GCH)
    seq, gate, par, state, tinv = _gdn_specs(nb, False)

    def body(q_ref, k_ref, v_ref, b_ref, a_ref, al_ref, dt_ref, o_ref, sall_ref, tall_ref, s_ref):
        @pl.when(pl.program_id(1) == 0)
        def _():
            s_ref[...] = jnp.zeros_like(s_ref)

        s = s_ref[...]
        for c in range(cpb):
            rows = pl.ds(c * GCH, GCH)
            sall_ref[c] = s
            o, s, tmat = _gdn_chunk(q_ref[:, rows, :], k_ref[:, rows, :], v_ref[:, rows, :],
                                    b_ref[0, :, pl.ds(c, 1), :], a_ref[0, :, pl.ds(c, 1), :],
                                    al_ref[...], dt_ref[...], s, want_t=True)
            o_ref[:, rows, :] = o
            tall_ref[c] = tmat.astype(BF16)
        s_ref[...] = s

    return _pcall(body, grid=(GH // hb, nb),
                  in_specs=[seq(hb // 2), seq(hb // 2), seq(hb), gate, gate, par, par],
                  out_specs=[seq(hb), state, tinv],
                  out_shape=[SDS((GH, t, GDK), F32), SDS((t // GCH, GH, GDK, GDK), F32),
                             SDS((t // GCH, GH, GCH, GCH), BF16)],
                  scratch_shapes=[pltpu.VMEM((hb, GDK, GDK), F32)],
                  compiler_params=_cp(("arbitrary", "arbitrary")), name="gdn_chunk_fwd")(
                      q, k, v, braw, araw, alog, dtb)


def _gdn_bwd(q, k, v, braw, araw, alog, dtb, sall, tall, do):
    t = v.shape[1]
    hb, cpb = GDN_HB, GDN_CPB
    nb = t // (cpb * GCH)
    seq, gate, par, state, tinv = _gdn_specs(nb, True)

    def body(q_ref, k_ref, v_ref, b_ref, a_ref, al_ref, dt_ref, sall_ref, tall_ref, do_ref,
             dq_ref, dk_ref, dv_ref, db_ref, da_ref, dal_ref, ddt_ref, ds_ref):
        @pl.when(pl.program_id(1) == 0)
        def _():
            ds_ref[...] = jnp.zeros_like(ds_ref)
            dal_ref[...] = jnp.zeros_like(dal_ref)
            ddt_ref[...] = jnp.zeros_like(ddt_ref)

        ds = ds_ref[...]
        for c in reversed(range(cpb)):
            rows = pl.ds(c * GCH, GCH)
            fn = functools.partial(_gdn_chunk, t_known=tall_ref[c].astype(F32))
            _, vjp = jax.vjp(fn, q_ref[:, rows, :], k_ref[:, rows, :], v_ref[:, rows, :],
                             b_ref[0, :, pl.ds(c, 1), :], a_ref[0, :, pl.ds(c, 1), :],
                             al_ref[...], dt_ref[...], sall_ref[c])
            dq, dk, dv, db, da, dal, ddt, ds = vjp((do_ref[:, rows, :], ds))
            dq_ref[:, rows, :] = dq
            dk_ref[:, rows, :] = dk
            dv_ref[:, rows, :] = dv
            db_ref[0, :, pl.ds(c, 1), :] = db
            da_ref[0, :, pl.ds(c, 1), :] = da
            dal_ref[...] += dal
            ddt_ref[...] += ddt
        ds_ref[...] = ds

    return _pcall(body, grid=(GH // hb, nb),
                  in_specs=[seq(hb // 2), seq(hb // 2), seq(hb), gate, gate, par, par, state, tinv, seq(hb)],
                  out_specs=[seq(hb // 2), seq(hb // 2), seq(hb), gate, gate, par, par],
                  out_shape=[SDS(q.shape, F32), SDS(k.shape, F32), SDS(v.shape, F32),
                             SDS(braw.shape, F32), SDS(araw.shape, F32),
                             SDS((GH, 1, 1), F32), SDS((GH, 1, 1), F32)],
                  scratch_shapes=[pltpu.VMEM((hb, GDK, GDK), F32)],
                  compiler_params=_cp(("arbitrary", "arbitrary")), name="gdn_chunk_bwd")(
                      q, k, v, braw, araw, alog, dtb, sall, tall, do)


def _ssd_specs(nb, rev):
    gb, cpb, tc = SSD_GB, SSD_CPB, SSD_CPB * SCH
    blk = (lambda n: nb - 1 - n) if rev else (lambda n: n)
    seq = lambda w: pl.BlockSpec((gb, tc, w), lambda g, n: (g, blk(n), 0))
    gate = pl.BlockSpec((1, cpb * SR, gb, 1, SCH), lambda g, n: (blk(n), 0, g, 0, 0))
    par = pl.BlockSpec((SR, gb, 1, 1), lambda g, n: (0, g, 0, 0))
    state = pl.BlockSpec((cpb, gb, SR * SP, SN), lambda g, n: (blk(n), g, 0, 0))
    return seq, gate, par, state


def _ssd_fwd(xs, bm, cm, dtraw, dtb, alog, dskip):
    t = xs.shape[1]
    gb, cpb = SSD_GB, SSD_CPB
    nb = t // (cpb * SCH)
    seq, gate, par, state = _ssd_specs(nb, False)

    def body(x_ref, b_ref, c_ref, dt_ref, dtb_ref, al_ref, dk_ref, y_ref, sall_ref, s_ref):
        @pl.when(pl.program_id(1) == 0)
        def _():
            s_ref[...] = jnp.zeros_like(s_ref)

        s = s_ref[...]
        for c in range(cpb):
            rows = pl.ds(c * SCH, SCH)
            sall_ref[c] = s
            y, s = _ssd_chunk(x_ref[:, rows, :], b_ref[:, rows, :], c_ref[:, rows, :],
                              dt_ref[0, pl.ds(c * SR, SR)], dtb_ref[...], al_ref[...],
                              dk_ref[...], s)
            y_ref[:, rows, :] = y
        s_ref[...] = s

    return _pcall(body, grid=(SG // gb, nb),
                  in_specs=[seq(SR * SP), seq(SN), seq(SN), gate, par, par, par],
                  out_specs=[seq(SR * SP), state],
                  out_shape=[SDS((SG, t, SR * SP), F32), SDS((t // SCH, SG, SR * SP, SN), F32)],
                  scratch_shapes=[pltpu.VMEM((gb, SR * SP, SN), F32)],
                  compiler_params=_cp(("arbitrary", "arbitrary")), name="ssd_chunk_fwd")(
                      xs, bm, cm, dtraw, dtb, alog, dskip)


def _ssd_bwd(xs, bm, cm, dtraw, dtb, alog, dskip, sall, dy):
    t = xs.shape[1]
    gb, cpb = SSD_GB, SSD_CPB
    nb = t // (cpb * SCH)
    seq, gate, par, state = _ssd_specs(nb, True)

    def body(x_ref, b_ref, c_ref, dt_ref, dtb_ref, al_ref, dk_ref, sall_ref, dy_ref,
             dx_ref, dbm_ref, dcm_ref, ddt_ref, ddtb_ref, dal_ref, ddk_ref, ds_ref):
        @pl.when(pl.program_id(1) == 0)
        def _():
            ds_ref[...] = jnp.zeros_like(ds_ref)
            ddtb_ref[...] = jnp.zeros_like(ddtb_ref)
            dal_ref[...] = jnp.zeros_like(dal_ref)
            ddk_ref[...] = jnp.zeros_like(ddk_ref)

        ds = ds_ref[...]
        for c in reversed(range(cpb)):
            rows = pl.ds(c * SCH, SCH)
            _, vjp = jax.vjp(_ssd_chunk, x_ref[:, rows, :], b_ref[:, rows, :], c_ref[:, rows, :],
                             dt_ref[0, pl.ds(c * SR, SR)], dtb_ref[...], al_ref[...],
                             dk_ref[...], sall_ref[c])
            dx, dbm, dcm, ddt, ddtb, dal, ddk, ds = vjp((dy_ref[:, rows, :], ds))
            dx_ref[:, rows, :] = dx
            dbm_ref[:, rows, :] = dbm
            dcm_ref[:, rows, :] = dcm
            ddt_ref[0, pl.ds(c * SR, SR)] = ddt
            ddtb_ref[...] += ddtb
            dal_ref[...] += dal
            ddk_ref[...] += ddk
        ds_ref[...] = ds

    return _pcall(body, grid=(SG // gb, nb),
                  in_specs=[seq(SR * SP), seq(SN), seq(SN), gate, par, par, par, state, seq(SR * SP)],
                  out_specs=[seq(SR * SP), seq(SN), seq(SN), gate, par, par, par],
                  out_shape=[SDS(xs.shape, F32), SDS(bm.shape, F32), SDS(cm.shape, F32),
                             SDS(dtraw.shape, F32), SDS((SR, SG, 1, 1), F32), SDS((SR, SG, 1, 1), F32),
                             SDS((SR, SG, 1, 1), F32)],
                  scratch_shapes=[pltpu.VMEM((gb, SR * SP, SN), F32)],
                  compiler_params=_cp(("arbitrary", "arbitrary")), name="ssd_chunk_bwd")(
                      xs, bm, cm, dtraw, dtb, alog, dskip, sall, dy)


def _gate_specs(tm, ct, zcol0, per_tile_w):
    z0 = zcol0 // ct
    return [pl.BlockSpec((1, tm, ct), lambda i, j: (j, i, 0)),
            pl.BlockSpec((tm, ct), lambda i, j: (i, z0 + j)),
            pl.BlockSpec((1, ct), (lambda i, j: (0, j)) if per_tile_w else (lambda i, j: (0, 0)))]


def _gdn_gate(o, z, w):
    return (_rms(o[0], w) * _silu(z),)


def _ssd_gate(y, z, w):
    return (_rms(y[0] * _silu(z), w),)


def _gate_fwd(fn, o, proj, zcol0, w, name):
    nt, t, ct = o.shape
    tpb = FWD_COLS // ct
    wide = tpb * ct
    tm = min(t, PLAIN_BLOCK_ELEMS // wide)
    per_tile_w = w.shape[1] > ct
    assert zcol0 % wide == 0 and nt % tpb == 0, name
    z0 = zcol0 // wide
    specs = [pl.BlockSpec((tpb, tm, ct), lambda i, j: (j, i, 0)),
             pl.BlockSpec((tm, wide), lambda i, j: (i, z0 + j)),
             pl.BlockSpec((1, wide), lambda i, j: (0, j)) if per_tile_w
             else pl.BlockSpec((1, ct), lambda i, j: (0, 0))]

    def fn_wide(ov, zv, wv):
        outs = []
        for k in range(tpb):
            cols = slice(k * ct, (k + 1) * ct)
            outs.append(fn(ov[k:k + 1], zv[:, cols], wv[:, cols] if per_tile_w else wv)[0])
        return (jnp.concatenate(outs, axis=1),)

    return _ew(fn_wide, [o, proj, w], specs, [SDS((t, nt * ct), BF16)],
               [pl.BlockSpec((tm, wide), lambda i, j: (i, j))], (t // tm, nt // tpb), name)[0]


def _gate_bwd(fn, o, proj, zcol0, w, dy, wacc, name, dz_cols):
    nt, t, ct = o.shape
    tm = min(t, BLOCK_ELEMS // ct)
    specs = _gate_specs(tm, ct, zcol0, wacc)
    out_spec = pl.BlockSpec((tm, ct), lambda i, j: (i, j))
    if wacc:
        flip = lambda s: pl.BlockSpec(s.block_shape, lambda j, i, f=s.index_map: f(i, j))
        specs = [flip(s) for s in specs]
        out_spec = flip(out_spec)
        grid, acc = (nt, t // tm), {2: 1}
    else:
        grid, acc = (t // tm, nt), {2: 0}
    return _ew_vjp(fn, [o, proj, w], specs, [dy], [out_spec], (0, 1, 2),
                   [SDS(o.shape, F32), SDS((t, dz_cols), BF16), SDS(w.shape, F32)],
                   [specs[0], out_spec, specs[2]], acc, grid, name)


def _adamw_math(w, g, m, v):
    m = ADAM_B1 * m + (1.0 - ADAM_B1) * g
    v = ADAM_B2 * v + (1.0 - ADAM_B2) * jnp.square(g)
    m_hat = m / (1.0 - ADAM_B1 ** ADAM_STEP)
    v_hat = v / (1.0 - ADAM_B2 ** ADAM_STEP)
    delta = -ADAM_LR * (m_hat / (jnp.sqrt(v_hat) + ADAM_EPS) + ADAM_WD * w)
    return delta, m, v


def _adamw(w, g, m, v, name):
    shape = w.shape
    w2, g2, m2, v2 = [a.reshape(-1, shape[-1]) for a in (w, g, m, v)]
    r, c = w2.shape
    tr = 256 if r % 256 == 0 else r
    spec = pl.BlockSpec((tr, c), lambda i: (i, 0))
    outs = _ew(_adamw_math, [w2, g2, m2, v2], [spec] * 4, [SDS((r, c), F32)] * 3, [spec] * 3,
               (r // tr,), name)
    return [o.reshape(shape) for o in outs]


def _coords():
    return lax.axis_index("x"), lax.axis_index("y"), lax.axis_index("c")


def _other_chips(x, y):
    return [(1 - x, y), (x, 1 - y), (1 - x, 1 - y)]


ANY = pl.BlockSpec(memory_space=pl.ANY)


def _rcopy(src, dst, send_sems, recv_sems, k, to):
    return pltpu.make_async_remote_copy(src_ref=src, dst_ref=dst, send_sem=send_sems.at[k],
                                        recv_sem=recv_sems.at[k], device_id=to, device_id_type=MESH)


def _gather_weights(bigs, small):
    n = len(bigs)

    def body(*refs):
        in_refs, small_ref = refs[:n], refs[n]
        out_refs, osmall_ref = refs[n + 1:2 * n + 1], refs[2 * n + 1]
        send_sems, recv_sems = refs[2 * n + 2:]
        x, y, c = _coords()
        me = 2 * x + y
        sibling = (x, y, 1 - c)
        chips = _other_chips(x, y)
        halves = [b.shape[0] // 2 for b in bigs]
        mine = [pl.ds(c * h, h) for h in halves]
        theirs = [pl.ds((1 - c) * h, h) for h in halves]
        rc = functools.partial(_rcopy, send_sems=send_sems, recv_sems=recv_sems)

        first = []
        for a in range(n):
            for j, (px, py) in enumerate(chips):
                first.append(rc(in_refs[a].at[mine[a]], out_refs[a].at[me, mine[a]], k=3 * a + j, to=(px, py, c)))
        for j, (px, py) in enumerate(chips):
            first.append(rc(small_ref, osmall_ref.at[me], k=6 * n + j, to=(px, py, c)))
        for cp in first:
            cp.start()
        passed = []
        for a in range(n):
            for j, (px, py) in enumerate(chips):
                landed = out_refs[a].at[2 * px + py, mine[a]]
                rc(landed, landed, k=3 * a + j, to=(px, py, c)).wait_recv()
                fw = rc(landed, landed, k=3 * n + 3 * a + j, to=sibling)
                fw.start()
                passed.append(fw)
        for a in range(n):
            for j, (px, py) in enumerate(chips):
                landed = out_refs[a].at[2 * px + py, theirs[a]]
                rc(landed, landed, k=3 * n + 3 * a + j, to=sibling).wait_recv()
        for j, (px, py) in enumerate(chips):
            rc(small_ref, osmall_ref.at[2 * px + py], k=6 * n + j, to=(px, py, c)).wait_recv()
        for cp in first + passed:
            cp.wait_send()

    outs = _pcall(body, in_specs=[ANY] * (n + 1), out_specs=[ANY] * (n + 1),
                  out_shape=[SDS((N_CHIPS,) + b.shape, b.dtype) for b in bigs]
                  + [SDS((N_CHIPS,) + small.shape, small.dtype)],
                  scratch_shapes=[pltpu.SemaphoreType.DMA((6 * n + 3,)), pltpu.SemaphoreType.DMA((6 * n + 3,))],
                  name="gather_weights")(*bigs, small)
    me = 2 * lax.axis_index("x") + lax.axis_index("y")
    return [lax.dynamic_update_index_in_dim(o, own, me, 0) for o, own in zip(outs, list(bigs) + [small])]


def _swap_halves(gs):
    n = len(gs)

    def body(*refs):
        send_sems, recv_sems = refs[2 * n:]
        x, y, c = _coords()
        cps = [_rcopy(refs[a].at[:, 1 - c], refs[n + a], send_sems, recv_sems, a, (x, y, 1 - c))
               for a in range(n)]
        for cp in cps:
            cp.start()
        for cp in cps:
            cp.wait()

    return _pcall(body, in_specs=[ANY] * n, out_specs=[ANY] * n,
                  out_shape=[SDS((N_CHIPS,) + g.shape[2:], g.dtype) for g in gs],
                  scratch_shapes=[pltpu.SemaphoreType.DMA((n,)), pltpu.SemaphoreType.DMA((n,))],
                  name="swap_halves")(*gs)


def _sum_cores(gs, rs, half_idx):
    n = len(gs)
    ns = 2
    in_specs, out_specs, out_shape = [], [], []
    for g in gs:
        _, _, h, w = g.shape
        in_specs.append(pl.BlockSpec((1, 1, h // ns, w), lambda b, i, c_ref: (b, c_ref[0], i, 0)))
    for g in gs:
        _, _, h, w = g.shape
        spec = pl.BlockSpec((1, h // ns, w), lambda b, i, c_ref: (b, i, 0))
        in_specs.append(spec)
        out_specs += [spec, spec]
        out_shape += [SDS((N_CHIPS, h, w), F32), SDS((N_CHIPS, h, w), BF16)]

    def body(c_ref, *refs):
        del c_ref
        for a in range(n):
            tot = refs[a][0] + refs[n + a][...]
            refs[2 * n + 2 * a][...] = tot
            refs[2 * n + 2 * a + 1][...] = tot.astype(BF16)

    outs = _pcall(body, grid_spec=pltpu.PrefetchScalarGridSpec(
        num_scalar_prefetch=1, grid=(N_CHIPS, ns), in_specs=in_specs, out_specs=out_specs),
        out_shape=out_shape, compiler_params=_cp(("arbitrary", "arbitrary")), name="sum_cores")(
            half_idx, *gs, *rs)
    return outs[0::2], outs[1::2]


def _scatter_chips(hs):
    n = len(hs)

    def body(*refs):
        send_sems, recv_sems = refs[2 * n:]
        x, y, c = _coords()
        cps = []
        for a in range(n):
            for j, (px, py) in enumerate(_other_chips(x, y)):
                cps.append(_rcopy(refs[a].at[2 * px + py], refs[n + a].at[j], send_sems, recv_sems,
                                  3 * a + j, (px, py, c)))
        for cp in cps:
            cp.start()
        for cp in cps:
            cp.wait()

    return _pcall(body, in_specs=[ANY] * n, out_specs=[ANY] * n,
                  out_shape=[SDS((3,) + h.shape[1:], h.dtype) for h in hs],
                  scratch_shapes=[pltpu.SemaphoreType.DMA((3 * n,)), pltpu.SemaphoreType.DMA((3 * n,))],
                  name="scatter_chips")(*hs)


def _sum_chips(hs, xs, chip_idx):
    n = len(hs)
    ns = 2
    in_specs, out_specs, out_shape = [], [], []
    for h_arr in hs:
        _, h, w = h_arr.shape
        in_specs.append(pl.BlockSpec((1, h // ns, w), lambda i, c_ref: (c_ref[0], i, 0)))
    for h_arr in hs:
        _, h, w = h_arr.shape
        in_specs.append(pl.BlockSpec((3, h // ns, w), lambda i, c_ref: (0, i, 0)))
        out_specs.append(pl.BlockSpec((h // ns, w), lambda i, c_ref: (i, 0)))
        out_shape.append(SDS((h, w), F32))

    def body(c_ref, *refs):
        del c_ref
        for a in range(n):
            x_ref = refs[n + a]
            refs[2 * n + a][...] = (refs[a][0] + x_ref[0].astype(F32) + x_ref[1].astype(F32)
                                    + x_ref[2].astype(F32))

    return _pcall(body, grid_spec=pltpu.PrefetchScalarGridSpec(
        num_scalar_prefetch=1, grid=(ns,), in_specs=in_specs, out_specs=out_specs),
        out_shape=out_shape, compiler_params=_cp(("arbitrary",)), name="sum_chips")(chip_idx, *hs, *xs)


def _swap_totals(tots):
    n = len(tots)

    def body(*refs):
        send_sems, recv_sems = refs[2 * n:]
        x, y, c = _coords()
        cps = [_rcopy(refs[a], refs[n + a], send_sems, recv_sems, a, (x, y, 1 - c)) for a in range(n)]
        for cp in cps:
            cp.start()
        for cp in cps:
            cp.wait()

    return _pcall(body, in_specs=[ANY] * n, out_specs=[ANY] * n,
                  out_shape=[SDS(t.shape, t.dtype) for t in tots],
                  scratch_shapes=[pltpu.SemaphoreType.DMA((n,)), pltpu.SemaphoreType.DMA((n,))],
                  name="swap_totals")(*tots)


def _allreduce_small(buf):
    rows = buf.shape[0]

    def body(b_ref, o_ref, g_ref, send_sems, recv_sems):
        x, y, c = _coords()
        me = 4 * x + 2 * y + c
        g_ref[me] = b_ref[...]
        cps = []
        for k in range(1, 8):
            px = 1 - x if k & 4 else x
            py = 1 - y if k & 2 else y
            pc = 1 - c if k & 1 else c
            cps.append(pltpu.make_async_remote_copy(
                src_ref=b_ref, dst_ref=g_ref.at[me], send_sem=send_sems.at[k - 1],
                recv_sem=recv_sems.at[k - 1], device_id=(px, py, pc), device_id_type=MESH))
        for cp in cps:
            cp.start()
        for cp in cps:
            cp.wait()
        acc = g_ref[0]
        for d in range(1, 8):
            acc = acc + g_ref[d]
        o_ref[...] = acc

    vm = pl.BlockSpec(memory_space=pltpu.VMEM)
    return _pcall(body, in_specs=[vm], out_specs=vm, out_shape=SDS(buf.shape, F32),
                  scratch_shapes=[pltpu.VMEM((8, rows, 128), F32), pltpu.SemaphoreType.DMA((7,)),
                                  pltpu.SemaphoreType.DMA((7,))],
                  compiler_params=pltpu.CompilerParams(vmem_limit_bytes=VMEM_LIMIT),
                  name="allreduce_small")(buf)


def _pack_flat(parts, mult):
    rows, offs, r0 = [], [], 0
    for p in parts:
        f = p.reshape(-1)
        f = jnp.pad(f, (0, (-f.shape[0]) % (mult * 128))).reshape(-1, 128)
        rows.append(f)
        offs.append((r0, p.shape))
        r0 += f.shape[0]
    return jnp.concatenate(rows, axis=0), offs


def _unpack_flat(buf, offs):
    out = []
    for r0, shape in offs:
        n = 1
        for s in shape:
            n *= s
        nr = -(-n // 128)
        out.append(buf[r0:r0 + nr].reshape(-1)[:n].reshape(shape))
    return out


def _gates_to_rows(a, heads, chunk, cpb):
    t = a.shape[0]
    return a.reshape(t // (chunk * cpb), cpb, chunk, heads).transpose(0, 3, 1, 2)


def _rows_to_gates(a):
    nb, heads, cpb, chunk = a.shape
    return a.transpose(0, 2, 3, 1).reshape(nb * cpb * chunk, heads)


def kernel(x, norm_w, gdn_w_in, gdn_conv_w, gdn_a_log, gdn_dt_bias, gdn_norm_w, gdn_w_out, ssd_w_in, ssd_conv_w, ssd_conv_b, ssd_dt_bias, ssd_a_log, ssd_d, ssd_norm_w, ssd_w_out, final_norm_w, loss_target, m_norm_w, m_gdn_w_in, m_gdn_conv_w, m_gdn_a_log, m_gdn_dt_bias, m_gdn_norm_w, m_gdn_w_out, m_ssd_w_in, m_ssd_conv_w, m_ssd_conv_b, m_ssd_dt_bias, m_ssd_a_log, m_ssd_d, m_ssd_norm_w, m_ssd_w_out, m_final_norm_w, v_norm_w, v_gdn_w_in, v_gdn_conv_w, v_gdn_a_log, v_gdn_dt_bias, v_gdn_norm_w, v_gdn_w_out, v_ssd_w_in, v_ssd_conv_w, v_ssd_conv_b, v_ssd_dt_bias, v_ssd_a_log, v_ssd_d, v_ssd_norm_w, v_ssd_w_out, v_final_norm_w):
    ws = dict(norm_w=norm_w, gdn_w_in=gdn_w_in, gdn_conv_w=gdn_conv_w, gdn_a_log=gdn_a_log,
              gdn_dt_bias=gdn_dt_bias, gdn_norm_w=gdn_norm_w, gdn_w_out=gdn_w_out, ssd_w_in=ssd_w_in,
              ssd_conv_w=ssd_conv_w, ssd_conv_b=ssd_conv_b, ssd_dt_bias=ssd_dt_bias,
              ssd_a_log=ssd_a_log, ssd_d=ssd_d, ssd_norm_w=ssd_norm_w, ssd_w_out=ssd_w_out,
              final_norm_w=final_norm_w)
    ms = dict(norm_w=m_norm_w, gdn_w_in=m_gdn_w_in, gdn_conv_w=m_gdn_conv_w, gdn_a_log=m_gdn_a_log,
              gdn_dt_bias=m_gdn_dt_bias, gdn_norm_w=m_gdn_norm_w, gdn_w_out=m_gdn_w_out,
              ssd_w_in=m_ssd_w_in, ssd_conv_w=m_ssd_conv_w, ssd_conv_b=m_ssd_conv_b,
              ssd_dt_bias=m_ssd_dt_bias, ssd_a_log=m_ssd_a_log, ssd_d=m_ssd_d,
              ssd_norm_w=m_ssd_norm_w, ssd_w_out=m_ssd_w_out, final_norm_w=m_final_norm_w)
    vs = dict(norm_w=v_norm_w, gdn_w_in=v_gdn_w_in, gdn_conv_w=v_gdn_conv_w, gdn_a_log=v_gdn_a_log,
              gdn_dt_bias=v_gdn_dt_bias, gdn_norm_w=v_gdn_norm_w, gdn_w_out=v_gdn_w_out,
              ssd_w_in=v_ssd_w_in, ssd_conv_w=v_ssd_conv_w, ssd_conv_b=v_ssd_conv_b,
              ssd_dt_bias=v_ssd_dt_bias, ssd_a_log=v_ssd_a_log, ssd_d=v_ssd_d,
              ssd_norm_w=v_ssd_norm_w, ssd_w_out=v_ssd_w_out, final_norm_w=v_final_norm_w)
    names = list(ws)
    cx, cy, cc = _coords()
    chip = 2 * cx + cy
    t = x.shape[1]
    x0 = x.reshape(t, D)
    tgt = loss_target.reshape(t, D)

    bigs = [p[0].astype(BF16) for p in (gdn_w_in, gdn_w_out, ssd_w_in, ssd_w_out)]
    small, small_offs = _pack_flat([gdn_conv_w[0], ssd_conv_w[0], ssd_conv_b[0], ssd_norm_w[0]], 8)
    a_gi, a_go, a_si, a_so, gsmall = _gather_weights(bigs, small)
    w_gi = jnp.concatenate([a_gi[b] for b in range(4)], axis=1)
    w_si = jnp.concatenate([a_si[b] for b in range(4)], axis=1)
    w_go = a_go.reshape(4 * OUT_SHARD, D)
    w_so = a_so.reshape(4 * OUT_SHARD, D)
    sm = [_unpack_flat(gsmall[b], small_offs) for b in range(4)]
    g_cw = jnp.concatenate([sm[b][0] for b in range(4)], axis=1)
    s_cw = jnp.concatenate([sm[b][1] for b in range(4)], axis=1)
    s_cb = jnp.concatenate([sm[b][2] for b in range(4)], axis=0)[None]
    s_nw = jnp.concatenate([sm[b][3] for b in range(4)], axis=0)[None]

    def pad_small(w):
        return jnp.concatenate([w, jnp.zeros((D, SMALL_W - w.shape[1]), w.dtype)], axis=1)

    wg_main, wg_small = w_gi[:, :G_MAIN], pad_small(w_gi[:, G_MAIN:])
    ws_main, ws_small = w_si[:, :S_MAIN], pad_small(w_si[:, S_MAIN:])
    zero_b = jnp.zeros((1, G_CONV), F32)
    nw0, nw1 = norm_w[0:1], norm_w[1:2]
    fw = final_norm_w[None]
    g_alog = gdn_a_log.reshape(GH, 1, 1)
    g_dtb = gdn_dt_bias.reshape(GH, 1, 1)
    g_nw = gdn_norm_w.reshape(1, GDK)
    s_dtb = ssd_dt_bias.reshape(SG, SR).T.reshape(SR, SG, 1, 1)
    s_alog = ssd_a_log.reshape(SG, SR).T.reshape(SR, SG, 1, 1)
    s_d = ssd_d.reshape(SG, SR).T.reshape(SR, SG, 1, 1)

    hid0 = _rms_fwd(x0, nw0, "rms0")
    pg = _matmul(hid0, wg_main, "nn", "gdn_in_proj", tn=G_MAIN // 4)
    pg_small = _matmul(hid0, wg_small, "nn", "gdn_in_proj_small", tn=SMALL_W)
    post_q = _l2norm_scaled(GDK ** -0.5)
    post_k = _l2norm_scaled(1.0)
    q = _conv_fwd(pg, 0, g_cw[:, :G_QK], zero_b[:, :G_QK], GDK, post_q, "gdn_conv_q")
    k = _conv_fwd(pg, G_QK, g_cw[:, G_QK:2 * G_QK], zero_b[:, :G_QK], GDK, post_k, "gdn_conv_k")
    v = _conv_fwd(pg, 2 * G_QK, g_cw[:, 2 * G_QK:], zero_b[:, :G_V], GDK, _silu, "gdn_conv_v")
    braw = _gates_to_rows(pg_small[:, :GH], GH, GCH, GDN_CPB)
    araw = _gates_to_rows(pg_small[:, GH:2 * GH], GH, GCH, GDN_CPB)
    o, g_sall, g_tall = _gdn_fwd(q, k, v, braw, araw, g_alog, g_dtb)
    y0 = _gate_fwd(_gdn_gate, o, pg, G_CONV, g_nw, "gdn_gate")
    x1 = _matmul(y0, w_go, "nn", "gdn_out_proj", add=x0)

    hid1 = _rms_fwd(x1, nw1, "rms1")
    ps = _matmul(hid1, ws_main, "nn", "ssd_in_proj", tn=S_MAIN // 4)
    ps_small = _matmul(hid1, ws_small, "nn", "ssd_in_proj_small", tn=SMALL_W)
    c_x, c_b, c_c = S_INNER, 2 * S_INNER, 2 * S_INNER + SG * SN
    post_s = _silu
    xs = _conv_fwd(ps, c_x, s_cw[:, :S_INNER], s_cb[:, :S_INNER], SR * SP, post_s, "ssd_conv_x")
    bm = _conv_fwd(ps, c_b, s_cw[:, S_INNER:S_INNER + SG * SN], s_cb[:, S_INNER:S_INNER + SG * SN], SN,
                   post_s, "ssd_conv_b")
    cm = _conv_fwd(ps, c_c, s_cw[:, S_INNER + SG * SN:], s_cb[:, S_INNER + SG * SN:], SN, post_s,
                   "ssd_conv_c")
    nbs = t // (SCH * SSD_CPB)
    dtraw = _gates_to_rows(ps_small[:, :SH], SH, SCH, SSD_CPB)
    dtraw = dtraw.reshape(nbs, SG, SR, SSD_CPB, SCH).transpose(0, 3, 2, 1, 4).reshape(nbs, SSD_CPB * SR, SG, 1, SCH)
    yss, s_sall = _ssd_fwd(xs, bm, cm, dtraw, s_dtb, s_alog, s_d)
    y1 = _gate_fwd(_ssd_gate, yss, ps, 0, s_nw, "ssd_gate")
    x2 = _matmul(y1, w_so, "nn", "ssd_out_proj", add=x1)

    dx2, d_fw, loss_row = _final_loss(x2, fw, tgt, "final_loss")

    dy1 = _matmul(dx2, w_so, "nt", "ssd_out_dx", out_dtype=BF16, tn=S_INNER)
    d_wso = _matmul(y1, dx2, "tn", "ssd_out_dw")
    dyss, dz_s, d_snw = _gate_bwd(_ssd_gate, yss, ps, 0, s_nw, dy1, True, "ssd_gate_bwd", Z_EXT)
    dxs, dbm, dcm, ddtraw, d_sdtb, d_salog, d_sd = _ssd_bwd(xs, bm, cm, dtraw, s_dtb, s_alog, s_d, s_sall, dyss)
    dps, dwx, dbx = _conv_dpre(ps, c_x, s_cw[:, :S_INNER], s_cb[:, :S_INNER], SR * SP, post_s, dxs, "ssd_dpre_x",
                               G_CONV, 0)
    dps, dwb, dbb = _conv_dpre(ps, c_b, s_cw[:, S_INNER:S_INNER + SG * SN], s_cb[:, S_INNER:S_INNER + SG * SN],
                               SN, post_s, dbm, "ssd_dpre_b", G_CONV, S_INNER, dps)
    dps, dwc, dbc = _conv_dpre(ps, c_c, s_cw[:, S_INNER + SG * SN:], s_cb[:, S_INNER + SG * SN:], SN, post_s,
                               dcm, "ssd_dpre_c", G_CONV, S_INNER + SG * SN, dps)
    d_scw = jnp.concatenate([dwx, dwb, dwc], axis=1)
    d_scb = jnp.concatenate([dbx, dbb, dbc], axis=1)
    dxbc = _conv_t(dps, s_cw, "ssd_conv_t")
    ddt = ddtraw.reshape(nbs, SSD_CPB, SR, SG, SCH).transpose(0, 3, 2, 1, 4).reshape(nbs, SH, SSD_CPB, SCH)
    ddt = _rows_to_gates(ddt)
    dsm_s = jnp.concatenate([ddt, jnp.zeros((t, SMALL_W - SH), F32)], axis=1).astype(BF16)
    dz_s = lax.dynamic_update_slice(dz_s, dsm_s, (0, S_INNER))
    ws_zx = jnp.concatenate([ws_main[:, :S_INNER], ws_small], axis=1)
    dhid1 = _matmul(dz_s, ws_zx, "nt", "ssd_in_dx_z", tk=Z_EXT)
    dhid1 = _matmul(dxbc, ws_main[:, S_INNER:], "nt", "ssd_in_dx_xbc", add=dhid1)
    dw_zx = _matmul(hid1, dz_s, "tn", "ssd_in_dw_z", tn=Z_EXT, tk=1024)
    d_wsi = jnp.concatenate([dw_zx[:, :S_INNER], _matmul(hid1, dxbc, "tn", "ssd_in_dw_xbc"),
                             dw_zx[:, S_INNER:S_INNER + SH]], axis=1)
    dx1, d_nw1 = _rms_bwd(x1, nw1, dx2, dhid1, "rms1_bwd")

    dy0 = _matmul(dx1, w_go, "nt", "gdn_out_dx", out_dtype=BF16, tn=G_V)
    d_wgo = _matmul(y0, dx1, "tn", "gdn_out_dw")
    do, dz_g, d_gnw = _gate_bwd(_gdn_gate, o, pg, G_CONV, g_nw, dy0, False, "gdn_gate_bwd", Z_EXT)
    dq, dk, dv, dbraw, daraw, d_galog, d_gdtb = _gdn_bwd(q, k, v, braw, araw, g_alog, g_dtb, g_sall, g_tall, do)
    dpg, dwq, _ = _conv_dpre(pg, 0, g_cw[:, :G_QK], zero_b[:, :G_QK], GDK, post_q, dq, "gdn_dpre_q", G_CONV, 0)
    dpg, dwk, _ = _conv_dpre(pg, G_QK, g_cw[:, G_QK:2 * G_QK], zero_b[:, :G_QK], GDK, post_k, dk, "gdn_dpre_k",
                             G_CONV, G_QK, dpg)
    dpg, dwv, _ = _conv_dpre(pg, 2 * G_QK, g_cw[:, 2 * G_QK:], zero_b[:, :G_V], GDK, _silu, dv, "gdn_dpre_v",
                             G_CONV, 2 * G_QK, dpg)
    d_gcw = jnp.concatenate([dwq, dwk, dwv], axis=1)
    dqkv = _conv_t(dpg, g_cw, "gdn_conv_t")
    dsm_g = jnp.concatenate([_rows_to_gates(dbraw), _rows_to_gates(daraw),
                             jnp.zeros((t, SMALL_W - 2 * GH), F32)], axis=1).astype(BF16)
    dz_g = lax.dynamic_update_slice(dz_g, dsm_g, (0, G_V))
    wg_zx = jnp.concatenate([wg_main[:, G_CONV:], wg_small], axis=1)
    dhid0 = _matmul(dqkv, wg_main[:, :G_CONV], "nt", "gdn_in_dx_qkv")
    dhid0 = _matmul(dz_g, wg_zx, "nt", "gdn_in_dx_z", add=dhid0, tk=Z_EXT)
    dw_zx = _matmul(hid0, dz_g, "tn", "gdn_in_dw_z", tn=Z_EXT, tk=1024)
    d_wgi = jnp.concatenate([_matmul(hid0, dqkv, "tn", "gdn_in_dw_qkv"), dw_zx[:, :G_V + 2 * GH]], axis=1)
    dx0, d_nw0 = _rms_bwd(x0, nw0, dx1, dhid0, "rms0_bwd")

    def in_blocks(dw):
        return dw.reshape(D, N_CHIPS, IN_SHARD).transpose(1, 0, 2).reshape(N_CHIPS, 2, D // 2, IN_SHARD)

    def out_blocks(dw):
        return dw.reshape(N_CHIPS, 2, OUT_SHARD // 2, D)

    gs = [in_blocks(d_wgi), out_blocks(d_wgo), in_blocks(d_wsi), out_blocks(d_wso)]
    from_sib = _swap_halves(gs)
    hsum, hsum_bf = _sum_cores(gs, from_sib, cc.astype(jnp.int32).reshape(1))
    recv = _scatter_chips(hsum_bf)
    tots = _sum_chips(hsum, recv, chip.astype(jnp.int32).reshape(1))
    sib_tots = _swap_totals(tots)
    full = [jnp.concatenate([jnp.where(cc == 0, mine, sib), jnp.where(cc == 0, sib, mine)], axis=0)
            for mine, sib in zip(tots, sib_tots)]
    grads = dict(
        gdn_w_in=full[0].reshape(1, D, IN_SHARD), gdn_w_out=full[1].reshape(1, OUT_SHARD, D),
        ssd_w_in=full[2].reshape(1, D, IN_SHARD), ssd_w_out=full[3].reshape(1, OUT_SHARD, D))

    small_parts = [loss_row, jnp.concatenate([d_nw0, d_nw1], axis=0), d_gcw, d_galog, d_gdtb, d_gnw, d_scw, d_scb,
                   d_sdtb, d_salog, d_sd, d_snw, d_fw]
    sbuf, soffs = _pack_flat(small_parts, 8)
    ssum = _unpack_flat(_allreduce_small(sbuf), soffs)
    (loss_s, g_nw_all, g_gcw, g_galog, g_gdtb, g_gnw, g_scw, g_scb, g_sdtb, g_salog, g_sd, g_snw, g_fw) = ssum

    def my_cols(a, width):
        return lax.dynamic_slice_in_dim(a, chip * width, width, axis=a.ndim - 1)

    grads.update(
        norm_w=g_nw_all, gdn_conv_w=my_cols(g_gcw, 1024)[None], gdn_a_log=g_galog.reshape(1, GH),
        gdn_dt_bias=g_gdtb.reshape(1, GH), gdn_norm_w=g_gnw.reshape(1, GDK),
        ssd_conv_w=my_cols(g_scw, 1024)[None], ssd_conv_b=my_cols(g_scb, 1024),
        ssd_dt_bias=g_sdtb.reshape(SR, SG).T.reshape(1, SH), ssd_a_log=g_salog.reshape(SR, SG).T.reshape(1, SH),
        ssd_d=g_sd.reshape(SR, SG).T.reshape(1, SH),
        ssd_norm_w=my_cols(g_snw, 512), final_norm_w=g_fw.reshape(D))
    loss = loss_s[0, 0]

    big_names = ("gdn_w_in", "gdn_w_out", "ssd_w_in", "ssd_w_out")
    deltas, new_m, new_v = {}, {}, {}
    for n in big_names:
        deltas[n], new_m[n], new_v[n] = _adamw(ws[n], grads[n], ms[n], vs[n], "adamw_" + n)
    rest = [n for n in names if n not in big_names]
    packs = [_pack_flat([d[n] for n in rest], 8) for d in (ws, grads, ms, vs)]
    outs = _adamw(*[p[0] for p in packs], "adamw_small")
    for d, buf in zip((deltas, new_m, new_v), outs):
        for n, a in zip(rest, _unpack_flat(buf, packs[0][1])):
            d[n] = a

    grad_x = dx0.reshape(1, t, D)
    return (loss, grad_x, *[grads[n] for n in names], *[deltas[n] for n in names],
            *[new_m[n] for n in names], *[new_v[n] for n in names])
```

```python
import functools

import jax
import jax.numpy as jnp
from jax import lax
from jax.experimental import pallas as pl
from jax.experimental.pallas import tpu as pltpu

F32 = jnp.float32
BF16 = jnp.bfloat16
SDS = jax.ShapeDtypeStruct
MESH = pl.DeviceIdType.MESH

D = 1024
EPS = 1e-6
CONV_K = 4
N_CHIPS = 4
GH = 16
GHQ = 8
GDK = 128
GCH = 64
G_QK = 1024
G_V = 2048
G_CONV = 4096
G_MAIN = 6144
G_IN = 6176
SH = 32
SP = 64
SN = 128
SG = 8
SR = 4
SCH = 128
S_INNER = 2048
S_MAIN = 6144
S_IN = 6176
IN_SHARD = 1544
OUT_SHARD = 512
SMALL_W = 128
Z_EXT = 2048 + SMALL_W

ADAM_LR = 0.001
ADAM_B1 = 0.9
ADAM_B2 = 0.999
ADAM_EPS = 1e-08
ADAM_WD = 0.01
ADAM_STEP = 10

VMEM_LIMIT = 56 * 1024 * 1024
BLOCK_ELEMS = 512 * 1024
PLAIN_BLOCK_ELEMS = 1024 * 1024
FWD_COLS = 512
NEG = -1e30


def _pcall(body, **kw):
    return pl.pallas_call(body, **kw)


def _cp(sem=None, vmem=VMEM_LIMIT):
    return pltpu.CompilerParams(dimension_semantics=sem, vmem_limit_bytes=vmem)


@jax.custom_jvp
def _sigmoid(x):
    return 1.0 / (1.0 + jnp.exp(-x))


@_sigmoid.defjvp
def _sigmoid_jvp(primals, tangents):
    s = _sigmoid(primals[0])
    return s, tangents[0] * (s * (1.0 - s))


@jax.custom_jvp
def _silu(x):
    return x * _sigmoid(x)


@_silu.defjvp
def _silu_jvp(primals, tangents):
    x = primals[0]
    s = _sigmoid(x)
    return x * s, tangents[0] * (s * (1.0 + x * (1.0 - s)))


def _softplus(x):
    return jnp.maximum(x, 0.0) + jnp.log(1.0 + jnp.exp(-jnp.abs(x)))


def _rms(x, w):
    return x * lax.rsqrt(jnp.mean(x * x, axis=-1, keepdims=True) + EPS) * w


_DIMS = {"nn": (((2,), (1,)), ((0,), (0,))),
         "nt": (((2,), (2,)), ((0,), (0,))),
         "tn": (((1,), (1,)), ((0,), (0,)))}


def _bdot(a, b, spec):
    return lax.dot_general(a.astype(BF16), b.astype(BF16), _DIMS[spec], preferred_element_type=F32)


@functools.partial(jax.custom_vjp, nondiff_argnums=(2,))
def _bmm(a, b, spec):
    return _bdot(a, b, spec)


def _bmm_fwd(a, b, spec):
    return _bdot(a, b, spec), (a, b)


def _bmm_bwd(spec, res, g):
    a, b = res
    if spec == "nn":
        return _bdot(g, b, "nt"), _bdot(a, g, "tn")
    if spec == "nt":
        return _bdot(g, b, "nn"), _bdot(g, a, "tn")
    return _bdot(b, g, "nt"), _bdot(a, g, "nn")


_bmm.defvjp(_bmm_fwd, _bmm_bwd)


@jax.custom_vjp
def _tri_inv(n):
    t = -n
    p = n
    steps = (n.shape[-1] - 1).bit_length() - 1
    r = lax.broadcasted_iota(jnp.int32, n.shape, 1)
    c = lax.broadcasted_iota(jnp.int32, n.shape, 2)
    t = t + jnp.where(r == c, 1.0, 0.0)
    for _ in range(steps):
        p = _bdot(p, p, "nn")
        t = t + _bdot(t, p, "nn")
    return t


def _tri_inv_fwd(n):
    t = _tri_inv(n)
    return t, t


def _tri_inv_bwd(t, g):
    return (-_bdot(_bdot(t, g, "tn"), t, "nt"),)


_tri_inv.defvjp(_tri_inv_fwd, _tri_inv_bwd)


@jax.custom_vjp
def _tri_inv_known(n, t):
    del n
    return t


def _tri_inv_known_fwd(n, t):
    del n
    return t, t


def _tri_inv_known_bwd(t, g):
    return _tri_inv_bwd(t, g)[0], jnp.zeros_like(t)


_tri_inv_known.defvjp(_tri_inv_known_fwd, _tri_inv_known_bwd)


def _masks(c, lead=1):
    r = lax.broadcasted_iota(jnp.int32, (lead, c, c), 1)
    s = lax.broadcasted_iota(jnp.int32, (lead, c, c), 2)
    return r >= s, r > s, r == s, r <= s


def _row_to_col(row, eye):
    return jnp.sum(jnp.where(eye, row, 0.0), axis=2, keepdims=True)


def _gdn_chunk(q, k, v, braw, araw, alog, dtb, s, t_known=None, want_t=False):
    h = v.shape[0]
    c = v.shape[1]
    rep = h // q.shape[0]
    tril, strict, eye, triu = _masks(c)
    qq = jnp.broadcast_to(q[:, None], (q.shape[0], rep) + q.shape[1:]).reshape(v.shape)
    kk = jnp.broadcast_to(k[:, None], (k.shape[0], rep) + k.shape[1:]).reshape(v.shape)
    beta_row = _sigmoid(braw)
    g_row = -jnp.exp(alog) * _softplus(araw + dtb)
    beta_col = _row_to_col(beta_row, eye)
    g_col = _row_to_col(g_row, eye)
    gc_col = jnp.sum(jnp.where(tril, g_row, 0.0), axis=2, keepdims=True)
    gc_row = jnp.sum(jnp.where(triu, g_col, 0.0), axis=1, keepdims=True)
    gc_last = jnp.sum(g_row, axis=2, keepdims=True)
    lmat = jnp.exp(jnp.where(tril, gc_col - gc_row, NEG))
    kb = kk * beta_col
    vb = v * beta_col
    n = jnp.where(strict, _bmm(kb, kk, "nt") * lmat, 0.0)
    t = _tri_inv(n) if t_known is None else _tri_inv_known(n, t_known)
    e_col = jnp.exp(gc_col)
    u = _bmm(t, vb, "nn")
    w = _bmm(t, kb * e_col, "nn")
    attn = _bmm(qq, kk, "nt") * lmat
    q_dec = qq * e_col
    k_dec = kk * jnp.exp(gc_last - gc_col)
    v_new = u - _bmm(w, s, "nn")
    o = _bmm(q_dec, s, "nn") + _bmm(attn, v_new, "nn")
    s_new = s * jnp.exp(gc_last) + _bmm(k_dec, v_new, "tn")
    return (o, s_new, t) if want_t else (o, s_new)


def _ssd_chunk(xs, bm, cm, dtraw, dtb, alog, dskip, s):
    c = xs.shape[1]
    tril, _, eye, triu = _masks(c)
    lane = lax.broadcasted_iota(jnp.int32, (1, 1, SR * SP), 2)
    prow = lax.broadcasted_iota(jnp.int32, (1, SR * SP, 1), 1)
    cb = _bmm(cm, bm, "nt")
    cs = _bmm(cm, s, "nt")

    def per_head(vals, idx):
        out = vals[SR - 1]
        for r in reversed(range(SR - 1)):
            out = jnp.where(idx < (r + 1) * SP, vals[r], out)
        return out

    dt_cols, e_cols, lmats, dstates, declast = [], [], [], [], []
    for r in range(SR):
        dt_row = _softplus(dtraw[r] + dtb[r])
        adt_row = -jnp.exp(alog[r]) * dt_row
        dt_cols.append(_row_to_col(dt_row, eye))
        adt_col = _row_to_col(adt_row, eye)
        acs_col = jnp.sum(jnp.where(tril, adt_row, 0.0), axis=2, keepdims=True)
        acs_row = jnp.sum(jnp.where(triu, adt_col, 0.0), axis=1, keepdims=True)
        acs_last = jnp.sum(adt_row, axis=2, keepdims=True)
        lmats.append(jnp.exp(jnp.where(tril, acs_col - acs_row, NEG)))
        e_cols.append(jnp.exp(acs_col))
        dstates.append(jnp.exp(acs_last - acs_col))
        declast.append(jnp.exp(acs_last))
    xd = xs * per_head(dt_cols, lane)
    y = per_head([_bmm(cb * lmats[r], xd, "nn") for r in range(SR)], lane)
    states = per_head([_bmm(xd, bm * dstates[r], "tn") for r in range(SR)], prow)
    y = y + cs * per_head(e_cols, lane) + xs * per_head([dskip[r] for r in range(SR)], lane)
    s_new = s * per_head(declast, prow) + states
    return y, s_new


def _matmul(a, b, mode, name, out_dtype=F32, add=None, tm=1024, tn=1024, tk=2048):
    if mode == "nn":
        (m, k), n = a.shape, b.shape[1]
    elif mode == "nt":
        (m, k), n = a.shape, b.shape[0]
    else:
        (k, m), n = a.shape, b.shape[1]
    tm, tn, tk = min(tm, m), min(tn, n), min(tk, k)
    assert m % tm == 0 and n % tn == 0 and k % tk == 0, (name, m, n, k)
    nk = k // tk
    dims = {"nn": (((1,), (0,)), ((), ())), "nt": (((1,), (1,)), ((), ())),
            "tn": (((0,), (0,)), ((), ()))}[mode]
    a_spec = {"nn": pl.BlockSpec((tm, tk), lambda i, j, kk: (i, kk)),
              "nt": pl.BlockSpec((tm, tk), lambda i, j, kk: (i, kk)),
              "tn": pl.BlockSpec((tk, tm), lambda i, j, kk: (kk, i))}[mode]
    b_spec = {"nn": pl.BlockSpec((tk, tn), lambda i, j, kk: (kk, j)),
              "nt": pl.BlockSpec((tn, tk), lambda i, j, kk: (j, kk)),
              "tn": pl.BlockSpec((tk, tn), lambda i, j, kk: (kk, j))}[mode]
    o_spec = pl.BlockSpec((tm, tn), lambda i, j, kk: (i, j))
    has_add = add is not None

    def body(*refs):
        a_ref, b_ref = refs[:2]
        add_ref = refs[2] if has_add else None
        o_ref = refs[2 + has_add]
        part = lax.dot_general(a_ref[...].astype(BF16), b_ref[...].astype(BF16), dims,
                               preferred_element_type=F32)

        def finish(r):
            if has_add:
                r = r + add_ref[...].astype(F32)
            o_ref[...] = r.astype(o_ref.dtype)

        if nk == 1:
            finish(part)
            return
        acc_ref = refs[3 + has_add]
        kk = pl.program_id(2)

        @pl.when(kk == 0)
        def _():
            acc_ref[...] = part

        @pl.when(jnp.logical_and(kk > 0, kk < nk - 1))
        def _():
            acc_ref[...] += part

        @pl.when(kk == nk - 1)
        def _():
            finish(acc_ref[...] + part)

    ins = [a, b] + ([add] if has_add else [])
    in_specs = [a_spec, b_spec] + ([o_spec] if has_add else [])
    scratch = [] if nk == 1 else [pltpu.VMEM((tm, tn), F32)]
    return _pcall(body, grid=(m // tm, n // tn, nk), in_specs=in_specs, out_specs=o_spec,
                  out_shape=SDS((m, n), out_dtype), scratch_shapes=scratch,
                  compiler_params=_cp(("parallel", "parallel", "arbitrary")), name=name)(*ins)


def _ew(fn, ins, in_specs, out_shape, out_specs, grid, name):
    n_in = len(ins)

    def body(*refs):
        outs = fn(*[r[...] for r in refs[:n_in]])
        for r, o in zip(refs[n_in:], outs):
            r[...] = o.astype(r.dtype)

    return _pcall(body, grid=grid, in_specs=in_specs, out_specs=out_specs, out_shape=out_shape,
                  compiler_params=_cp(("arbitrary",) * len(grid)), name=name)(*ins)


def _ew_vjp(fn, ins, in_specs, cts, ct_specs, wrt, g_shape, g_specs, acc, grid, name):
    n_in, n_ct = len(ins), len(cts)

    def body(*refs):
        vals = [r[...].astype(F32) for r in refs[:n_in]]
        outs, vjp = jax.vjp(fn, *vals)
        g_all = vjp(tuple(r[...].astype(F32) for r in refs[n_in:n_in + n_ct]))
        for pos, (i, g_ref) in enumerate(zip(wrt, refs[n_in + n_ct:])):
            g = g_all[i]
            if pos in acc:
                first = functools.reduce(
                    jnp.logical_and, [pl.program_id(ax) == 0 for ax in range(acc[pos], len(grid))])

                @pl.when(first)
                def _():
                    g_ref[...] = jnp.zeros_like(g_ref)

                g_ref[...] += g.astype(g_ref.dtype)
            else:
                g_ref[...] = g.astype(g_ref.dtype)

    return _pcall(body, grid=grid, in_specs=list(in_specs) + list(ct_specs), out_specs=g_specs,
                  out_shape=g_shape, compiler_params=_cp(("arbitrary",) * len(grid)),
                  name=name)(*ins, *cts)


def _row_spec(tm, n):
    return pl.BlockSpec((tm, n), lambda i: (i, 0))


def _par_spec(n):
    return pl.BlockSpec((1, n), lambda i: (0, 0))


def _rms_fwd(x, w, name):
    t = x.shape[0]
    tm = min(t, 512)

    def fn(xv, wv):
        h = _rms(xv, wv)
        return h, h.T

    return _ew(fn, [x, w], [_row_spec(tm, D), _par_spec(D)],
               [SDS((t, D), BF16), SDS((D, t), BF16)],
               [_row_spec(tm, D), pl.BlockSpec((D, tm), lambda i: (0, i))], (t // tm,), name)


def _rms_bwd(x, w, dres, dhid, name):
    t = x.shape[0]
    tm = min(t, 512)
    return _ew_vjp(lambda xv, wv: (xv, _rms(xv, wv)), [x, w], [_row_spec(tm, D), _par_spec(D)],
                   [dres, dhid], [_row_spec(tm, D), _row_spec(tm, D)], (0, 1),
                   [SDS((t, D), F32), SDS((1, D), F32)], [_row_spec(tm, D), _par_spec(D)],
                   {1: 0}, (t // tm,), name)


def _final_loss(x, w, tgt, name):
    t = x.shape[0]
    tm = min(t, 512)

    def body(x_ref, w_ref, t_ref, dx_ref, dw_ref, l_ref):
        @pl.when(pl.program_id(0) == 0)
        def _():
            dw_ref[...] = jnp.zeros_like(dw_ref)
            l_ref[...] = jnp.zeros_like(l_ref)

        xv, wv = x_ref[...], w_ref[...]
        rstd = lax.rsqrt(jnp.mean(xv * xv, axis=-1, keepdims=True) + EPS)
        xh = xv * rstd
        err = xh * wv - t_ref[...]
        l_ref[...] += 0.5 * jnp.sum(jnp.mean(err * err, axis=-1, keepdims=True), axis=0, keepdims=True)
        dy = err * (1.0 / D)
        dw_ref[...] += jnp.sum(dy * xh, axis=0, keepdims=True)
        dxh = dy * wv
        dx_ref[...] = rstd * (dxh - xh * jnp.mean(dxh * xh, axis=-1, keepdims=True))

    return _pcall(body, grid=(t // tm,), in_specs=[_row_spec(tm, D), _par_spec(D), _row_spec(tm, D)],
                  out_specs=[_row_spec(tm, D), _par_spec(D), _par_spec(128)],
                  out_shape=[SDS((t, D), F32), SDS((1, D), F32), SDS((1, 128), F32)],
                  compiler_params=_cp(("arbitrary",)), name=name)(x, w, tgt)


def _conv_taps(ext, w_ref, tm, lo):
    n = ext.shape[0]
    acc = None
    for j in range(CONV_K):
        shift = (CONV_K - 1 - j) if lo else (n - (CONV_K - 1 - j)) % n
        rolled = pltpu.roll(ext, shift, 0) if shift else ext
        term = w_ref[pl.ds(j, 1), :] * rolled[lo:lo + tm]
        acc = term if acc is None else acc + term
    return acc


def _conv_pre_specs(tm, ct, col0):
    hb = tm // 8
    return [pl.BlockSpec((tm, ct), lambda j, i: (i, col0 + j)),
            pl.BlockSpec((8, ct), lambda j, i: (jnp.maximum(i * hb - 1, 0), col0 + j)),
            pl.BlockSpec((CONV_K, ct), lambda j, i: (0, j)),
            pl.BlockSpec((1, ct), lambda j, i: (0, j))]


def _conv_pre_value(x_ref, xh_ref, w_ref, b_ref, tm):
    halo = jnp.where(pl.program_id(1) > 0, xh_ref[...], 0.0)
    ext = jnp.concatenate([halo, x_ref[...]], axis=0)
    return _conv_taps(ext, w_ref, tm, 8) + b_ref[...], ext


def _conv_fwd(x, col0, w, b, ct, post, name):
    t = x.shape[0]
    tpb = FWD_COLS // ct
    wide = tpb * ct
    tm = min(t, PLAIN_BLOCK_ELEMS // wide)
    nt = w.shape[1] // ct
    assert col0 % wide == 0 and nt % tpb == 0, name

    def body(x_ref, xh_ref, w_ref, b_ref, o_ref):
        pre, _ = _conv_pre_value(x_ref, xh_ref, w_ref, b_ref, tm)
        for k in range(tpb):
            o_ref[k] = post(pre[:, k * ct:(k + 1) * ct])

    return _pcall(body, grid=(nt // tpb, t // tm), in_specs=_conv_pre_specs(tm, wide, col0 // wide),
                  out_specs=pl.BlockSpec((tpb, tm, ct), lambda j, i: (j, i, 0)),
                  out_shape=SDS((nt, t, ct), F32), compiler_params=_cp(("arbitrary", "arbitrary")),
                  name=name)(x, x, w, b)


def _conv_dpre(x, col0, w, b, ct, post, dout, name, c_total, c_off, into=None):
    t = x.shape[0]
    tm = min(t, BLOCK_ELEMS // ct)
    nt = w.shape[1] // ct
    chained = into is not None

    def body(*refs):
        x_ref, xh_ref, w_ref, b_ref, do_ref = refs[:5]
        dp_ref, dw_ref, db_ref = refs[5 + chained:]
        pre, ext = _conv_pre_value(x_ref, xh_ref, w_ref, b_ref, tm)
        _, vjp = jax.vjp(post, pre)
        dpre = vjp(do_ref[0])[0]
        dp_ref[...] = dpre

        @pl.when(pl.program_id(1) == 0)
        def _():
            dw_ref[...] = jnp.zeros_like(dw_ref)
            db_ref[...] = jnp.zeros_like(db_ref)

        for j in range(CONV_K):
            xs = (pltpu.roll(ext, CONV_K - 1 - j, 0) if j < CONV_K - 1 else ext)[8:8 + tm]
            dw_ref[pl.ds(j, 1), :] += jnp.sum(dpre * xs, axis=0, keepdims=True)
        db_ref[...] += jnp.sum(dpre, axis=0, keepdims=True)

    c = w.shape[1]
    o0 = c_off // ct
    return _pcall(body, grid=(nt, t // tm),
                  in_specs=_conv_pre_specs(tm, ct, col0 // ct)
                  + [pl.BlockSpec((1, tm, ct), lambda j, i: (j, i, 0))] + ([ANY] if chained else []),
                  out_specs=[pl.BlockSpec((tm, ct), lambda j, i: (i, o0 + j)),
                             pl.BlockSpec((CONV_K, ct), lambda j, i: (0, j)),
                             pl.BlockSpec((1, ct), lambda j, i: (0, j))],
                  out_shape=[SDS((t, c_total), F32), SDS((CONV_K, c), F32), SDS((1, c), F32)],
                  input_output_aliases={5: 0} if chained else {},
                  compiler_params=_cp(("arbitrary", "arbitrary")), name=name)(
                      x, x, w, b, dout, *([into] if chained else []))


def _conv_t(dpre, w, name):
    t, c = dpre.shape
    ct = min(c, 512)
    tm = min(t, PLAIN_BLOCK_ELEMS // ct)
    hb = tm // 8
    last = t // tm - 1

    def body(d_ref, dh_ref, w_ref, o_ref):
        halo = jnp.where(pl.program_id(1) < last, dh_ref[...], 0.0)
        ext = jnp.concatenate([d_ref[...], halo], axis=0)
        o_ref[...] = _conv_taps(ext, w_ref, tm, 0).astype(o_ref.dtype)

    return _pcall(body, grid=(c // ct, t // tm),
                  in_specs=[pl.BlockSpec((tm, ct), lambda j, i: (i, j)),
                            pl.BlockSpec((8, ct), lambda j, i: (jnp.minimum((i + 1) * hb, t // 8 - 1), j)),
                            pl.BlockSpec((CONV_K, ct), lambda j, i: (0, j))],
                  out_specs=pl.BlockSpec((tm, ct), lambda j, i: (i, j)),
                  out_shape=SDS((t, c), BF16), compiler_params=_cp(("arbitrary", "arbitrary")),
                  name=name)(dpre, dpre, w)


def _l2norm_scaled(scale):
    def post(pre):
        a = _silu(pre)
        return a * lax.rsqrt(jnp.sum(a * a, axis=-1, keepdims=True) + EPS) * scale
    return post


GDN_HB = 16
GDN_CPB = 4
SSD_GB = 8
SSD_CPB = 1


def _gdn_specs(nb, rev):
    hb, cpb, tc = GDN_HB, GDN_CPB, GDN_CPB * GCH
    blk = (lambda n: nb - 1 - n) if rev else (lambda n: n)
    seq = lambda h: pl.BlockSpec((h, tc, GDK), lambda g, n: (g, blk(n), 0))
    gate = pl.BlockSpec((1, hb, cpb, GCH), lambda g, n: (blk(n), g, 0, 0))
    par = pl.BlockSpec((hb, 1, 1), lambda g, n: (g, 0, 0))
    state = pl.BlockSpec((cpb, hb, GDK, GDK), lambda g, n: (blk(n), g, 0, 0))
    tinv = pl.BlockSpec((cpb, hb, GCH, GCH), lambda g, n: (blk(n), g, 0, 0))
    return seq, gate, par, state, tinv


def _gdn_fwd(q, k, v, braw, araw, alog, dtb):
    t = v.shape[1]
    hb, cpb = GDN_HB, GDN_CPB
    nb = t // (cpb * GCH)
    seq, gate, par, state, tinv = _gdn_specs(nb, False)

    def body(q_ref, k_ref, v_ref, b_ref, a_ref, al_ref, dt_ref, o_ref, sall_ref, tall_ref, s_ref):
        @pl.when(pl.program_id(1) == 0)
        def _():
            s_ref[...] = jnp.zeros_like(s_ref)

        s = s_ref[...]
        for c in range(cpb):
            rows = pl.ds(c * GCH, GCH)
            sall_ref[c] = s
            o, s, tmat = _gdn_chunk(q_ref[:, rows, :], k_ref[:, rows, :], v_ref[:, rows, :],
                                    b_ref[0, :, pl.ds(c, 1), :], a_ref[0, :, pl.ds(c, 1), :],
                                    al_ref[...], dt_ref[...], s, want_t=True)
            o_ref[:, rows, :] = o
            tall_ref[c] = tmat.astype(BF16)
        s_ref[...] = s

    return _pcall(body, grid=(GH // hb, nb),
                  in_specs=[seq(hb // 2), seq(hb // 2), seq(hb), gate, gate, par, par],
                  out_specs=[seq(hb), state, tinv],
                  out_shape=[SDS((GH, t, GDK), F32), SDS((t // GCH, GH, GDK, GDK), F32),
                             SDS((t // GCH, GH, GCH, GCH), BF16)],
                  scratch_shapes=[pltpu.VMEM((hb, GDK, GDK), F32)],
                  compiler_params=_cp(("arbitrary", "arbitrary")), name="gdn_chunk_fwd")(
                      q, k, v, braw, araw, alog, dtb)


def _gdn_bwd(q, k, v, braw, araw, alog, dtb, sall, tall, do):
    t = v.shape[1]
    hb, cpb = GDN_HB, GDN_CPB
    nb = t // (cpb * GCH)
    seq, gate, par, state, tinv = _gdn_specs(nb, True)

    def body(q_ref, k_ref, v_ref, b_ref, a_ref, al_ref, dt_ref, sall_ref, tall_ref, do_ref,
             dq_ref, dk_ref, dv_ref, db_ref, da_ref, dal_ref, ddt_ref, ds_ref):
        @pl.when(pl.program_id(1) == 0)
        def _():
            ds_ref[...] = jnp.zeros_like(ds_ref)
            dal_ref[...] = jnp.zeros_like(dal_ref)
            ddt_ref[...] = jnp.zeros_like(ddt_ref)

        ds = ds_ref[...]
        for c in reversed(range(cpb)):
            rows = pl.ds(c * GCH, GCH)
            fn = functools.partial(_gdn_chunk, t_known=tall_ref[c].astype(F32))
            _, vjp = jax.vjp(fn, q_ref[:, rows, :], k_ref[:, rows, :], v_ref[:, rows, :],
                             b_ref[0, :, pl.ds(c, 1), :], a_ref[0, :, pl.ds(c, 1), :],
                             al_ref[...], dt_ref[...], sall_ref[c])
            dq, dk, dv, db, da, dal, ddt, ds = vjp((do_ref[:, rows, :], ds))
            dq_ref[:, rows, :] = dq
            dk_ref[:, rows, :] = dk
            dv_ref[:, rows, :] = dv
            db_ref[0, :, pl.ds(c, 1), :] = db
            da_ref[0, :, pl.ds(c, 1), :] = da
            dal_ref[...] += dal
            ddt_ref[...] += ddt
        ds_ref[...] = ds

    return _pcall(body, grid=(GH // hb, nb),
                  in_specs=[seq(hb // 2), seq(hb // 2), seq(hb), gate, gate, par, par, state, tinv, seq(hb)],
                  out_specs=[seq(hb // 2), seq(hb // 2), seq(hb), gate, gate, par, par],
                  out_shape=[SDS(q.shape, F32), SDS(k.shape, F32), SDS(v.shape, F32),
                             SDS(braw.shape, F32), SDS(araw.shape, F32),
                             SDS((GH, 1, 1), F32), SDS((GH, 1, 1), F32)],
                  scratch_shapes=[pltpu.VMEM((hb, GDK, GDK), F32)],
                  compiler_params=_cp(("arbitrary", "arbitrary")), name="gdn_chunk_bwd")(
                      q, k, v, braw, araw, alog, dtb, sall, tall, do)


def _ssd_specs(nb, rev):
    gb, cpb, tc = SSD_GB, SSD_CPB, SSD_CPB * SCH
    blk = (lambda n: nb - 1 - n) if rev else (lambda n: n)
    seq = lambda w: pl.BlockSpec((gb, tc, w), lambda g, n: (g, blk(n), 0))
    gate = pl.BlockSpec((1, cpb * SR, gb, 1, SCH), lambda g, n: (blk(n), 0, g, 0, 0))
    par = pl.BlockSpec((SR, gb, 1, 1), lambda g, n: (0, g, 0, 0))
    state = pl.BlockSpec((cpb, gb, SR * SP, SN), lambda g, n: (blk(n), g, 0, 0))
    return seq, gate, par, state


def _ssd_fwd(xs, bm, cm, dtraw, dtb, alog, dskip):
    t = xs.shape[1]
    gb, cpb = SSD_GB, SSD_CPB
    nb = t // (cpb * SCH)
    seq, gate, par, state = _ssd_specs(nb, False)

    def body(x_ref, b_ref, c_ref, dt_ref, dtb_ref, al_ref, dk_ref, y_ref, sall_ref, s_ref):
        @pl.when(pl.program_id(1) == 0)
        def _():
            s_ref[...] = jnp.zeros_like(s_ref)

        s = s_ref[...]
        for c in range(cpb):
            rows = pl.ds(c * SCH, SCH)
            sall_ref[c] = s
            y, s = _ssd_chunk(x_ref[:, rows, :], b_ref[:, rows, :], c_ref[:, rows, :],
                              dt_ref[0, pl.ds(c * SR, SR)], dtb_ref[...], al_ref[...],
                              dk_ref[...], s)
            y_ref[:, rows, :] = y
        s_ref[...] = s

    return _pcall(body, grid=(SG // gb, nb),
                  in_specs=[seq(SR * SP), seq(SN), seq(SN), gate, par, par, par],
                  out_specs=[seq(SR * SP), state],
                  out_shape=[SDS((SG, t, SR * SP), F32), SDS((t // SCH, SG, SR * SP, SN), F32)],
                  scratch_shapes=[pltpu.VMEM((gb, SR * SP, SN), F32)],
                  compiler_params=_cp(("arbitrary", "arbitrary")), name="ssd_chunk_fwd")(
                      xs, bm, cm, dtraw, dtb, alog, dskip)


def _ssd_bwd(xs, bm, cm, dtraw, dtb, alog, dskip, sall, dy):
    t = xs.shape[1]
    gb, cpb = SSD_GB, SSD_CPB
    nb = t // (cpb * SCH)
    seq, gate, par, state = _ssd_specs(nb, True)

    def body(x_ref, b_ref, c_ref, dt_ref, dtb_ref, al_ref, dk_ref, sall_ref, dy_ref,
             dx_ref, dbm_ref, dcm_ref, ddt_ref, ddtb_ref, dal_ref, ddk_ref, ds_ref):
        @pl.when(pl.program_id(1) == 0)
        def _():
            ds_ref[...] = jnp.zeros_like(ds_ref)
            ddtb_ref[...] = jnp.zeros_like(ddtb_ref)
            dal_ref[...] = jnp.zeros_like(dal_ref)
            ddk_ref[...] = jnp.zeros_like(ddk_ref)

        ds = ds_ref[...]
        for c in reversed(range(cpb)):
            rows = pl.ds(c * SCH, SCH)
            _, vjp = jax.vjp(_ssd_chunk, x_ref[:, rows, :], b_ref[:, rows, :], c_ref[:, rows, :],
                             dt_ref[0, pl.ds(c * SR, SR)], dtb_ref[...], al_ref[...],
                             dk_ref[...], sall_ref[c])
            dx, dbm, dcm, ddt, ddtb, dal, ddk, ds = vjp((dy_ref[:, rows, :], ds))
            dx_ref[:, rows, :] = dx
            dbm_ref[:, rows, :] = dbm
            dcm_ref[:, rows, :] = dcm
            ddt_ref[0, pl.ds(c * SR, SR)] = ddt
            ddtb_ref[...] += ddtb
            dal_ref[...] += dal
            ddk_ref[...] += ddk
        ds_ref[...] = ds

    return _pcall(body, grid=(SG // gb, nb),
                  in_specs=[seq(SR * SP), seq(SN), seq(SN), gate, par, par, par, state, seq(SR * SP)],
                  out_specs=[seq(SR * SP), seq(SN), seq(SN), gate, par, par, par],
                  out_shape=[SDS(xs.shape, F32), SDS(bm.shape, F32), SDS(cm.shape, F32),
                             SDS(dtraw.shape, F32), SDS((SR, SG, 1, 1), F32), SDS((SR, SG, 1, 1), F32),
                             SDS((SR, SG, 1, 1), F32)],
                  scratch_shapes=[pltpu.VMEM((gb, SR * SP, SN), F32)],
                  compiler_params=_cp(("arbitrary", "arbitrary")), name="ssd_chunk_bwd")(
                      xs, bm, cm, dtraw, dtb, alog, dskip, sall, dy)


def _gate_specs(tm, ct, zcol0, per_tile_w):
    z0 = zcol0 // ct
    return [pl.BlockSpec((1, tm, ct), lambda i, j: (j, i, 0)),
            pl.BlockSpec((tm, ct), lambda i, j: (i, z0 + j)),
            pl.BlockSpec((1, ct), (lambda i, j: (0, j)) if per_tile_w else (lambda i, j: (0, 0)))]


def _gdn_gate(o, z, w):
    return (_rms(o[0], w) * _silu(z),)


def _ssd_gate(y, z, w):
    return (_rms(y[0] * _silu(z), w),)


def _gate_fwd(fn, o, proj, zcol0, w, name):
    nt, t, ct = o.shape
    tpb = FWD_COLS // ct
    wide = tpb * ct
    tm = min(t, PLAIN_BLOCK_ELEMS // wide)
    per_tile_w = w.shape[1] > ct
    assert zcol0 % wide == 0 and nt % tpb == 0, name
    z0 = zcol0 // wide
    specs = [pl.BlockSpec((tpb, tm, ct), lambda i, j: (j, i, 0)),
             pl.BlockSpec((tm, wide), lambda i, j: (i, z0 + j)),
             pl.BlockSpec((1, wide), lambda i, j: (0, j)) if per_tile_w
             else pl.BlockSpec((1, ct), lambda i, j: (0, 0))]

    def fn_wide(ov, zv, wv):
        outs = []
        for k in range(tpb):
            cols = slice(k * ct, (k + 1) * ct)
            outs.append(fn(ov[k:k + 1], zv[:, cols], wv[:, cols] if per_tile_w else wv)[0])
        return (jnp.concatenate(outs, axis=1),)

    return _ew(fn_wide, [o, proj, w], specs, [SDS((t, nt * ct), BF16)],
               [pl.BlockSpec((tm, wide), lambda i, j: (i, j))], (t // tm, nt // tpb), name)[0]


def _gate_bwd(fn, o, proj, zcol0, w, dy, wacc, name, dz_cols):
    nt, t, ct = o.shape
    tm = min(t, BLOCK_ELEMS // ct)
    specs = _gate_specs(tm, ct, zcol0, wacc)
    out_spec = pl.BlockSpec((tm, ct), lambda i, j: (i, j))
    if wacc:
        flip = lambda s: pl.BlockSpec(s.block_shape, lambda j, i, f=s.index_map: f(i, j))
        specs = [flip(s) for s in specs]
        out_spec = flip(out_spec)
        grid, acc = (nt, t // tm), {2: 1}
    else:
        grid, acc = (t // tm, nt), {2: 0}
    return _ew_vjp(fn, [o, proj, w], specs, [dy], [out_spec], (0, 1, 2),
                   [SDS(o.shape, F32), SDS((t, dz_cols), BF16), SDS(w.shape, F32)],
                   [specs[0], out_spec, specs[2]], acc, grid, name)


def _adamw_math(w, g, m, v):
    m = ADAM_B1 * m + (1.0 - ADAM_B1) * g
    v = ADAM_B2 * v + (1.0 - ADAM_B2) * jnp.square(g)
    m_hat = m / (1.0 - ADAM_B1 ** ADAM_STEP)
    v_hat = v / (1.0 - ADAM_B2 ** ADAM_STEP)
    delta = -ADAM_LR * (m_hat / (jnp.sqrt(v_hat) + ADAM_EPS) + ADAM_WD * w)
    return delta, m, v


def _adamw(w, g, m, v, name):
    shape = w.shape
    w2, g2, m2, v2 = [a.reshape(-1, shape[-1]) for a in (w, g, m, v)]
    r, c = w2.shape
    tr = 256 if r % 256 == 0 else r
    spec = pl.BlockSpec((tr, c), lambda i: (i, 0))
    outs = _ew(_adamw_math, [w2, g2, m2, v2], [spec] * 4, [SDS((r, c), F32)] * 3, [spec] * 3,
               (r // tr,), name)
    return [o.reshape(shape) for o in outs]


def _coords():
    return lax.axis_index("x"), lax.axis_index("y"), lax.axis_index("c")


def _other_chips(x, y):
    return [(1 - x, y), (x, 1 - y), (1 - x, 1 - y)]


ANY = pl.BlockSpec(memory_space=pl.ANY)


def _rcopy(src, dst, send_sems, recv_sems, k, to):
    return pltpu.make_async_remote_copy(src_ref=src, dst_ref=dst, send_sem=send_sems.at[k],
                                        recv_sem=recv_sems.at[k], device_id=to, device_id_type=MESH)


def _gather_weights(bigs, small):
    n = len(bigs)

    def body(*refs):
        in_refs, small_ref = refs[:n], refs[n]
        out_refs, osmall_ref = refs[n + 1:2 * n + 1], refs[2 * n + 1]
        send_sems, recv_sems = refs[2 * n + 2:]
        x, y, c = _coords()
        me = 2 * x + y
        sibling = (x, y, 1 - c)
        chips = _other_chips(x, y)
        halves = [b.shape[0] // 2 for b in bigs]
        mine = [pl.ds(c * h, h) for h in halves]
        theirs = [pl.ds((1 - c) * h, h) for h in halves]
        rc = functools.partial(_rcopy, send_sems=send_sems, recv_sems=recv_sems)

        first = []
        for a in range(n):
            for j, (px, py) in enumerate(chips):
                first.append(rc(in_refs[a].at[mine[a]], out_refs[a].at[me, mine[a]], k=3 * a + j, to=(px, py, c)))
        for j, (px, py) in enumerate(chips):
            first.append(rc(small_ref, osmall_ref.at[me], k=6 * n + j, to=(px, py, c)))
        for cp in first:
            cp.start()
        passed = []
        for a in range(n):
            for j, (px, py) in enumerate(chips):
                landed = out_refs[a].at[2 * px + py, mine[a]]
                rc(landed, landed, k=3 * a + j, to=(px, py, c)).wait_recv()
                fw = rc(landed, landed, k=3 * n + 3 * a + j, to=sibling)
                fw.start()
                passed.append(fw)
        for a in range(n):
            for j, (px, py) in enumerate(chips):
                landed = out_refs[a].at[2 * px + py, theirs[a]]
                rc(landed, landed, k=3 * n + 3 * a + j, to=sibling).wait_recv()
        for j, (px, py) in enumerate(chips):
            rc(small_ref, osmall_ref.at[2 * px + py], k=6 * n + j, to=(px, py, c)).wait_recv()
        for cp in first + passed:
            cp.wait_send()

    outs = _pcall(body, in_specs=[ANY] * (n + 1), out_specs=[ANY] * (n + 1),
                  out_shape=[SDS((N_CHIPS,) + b.shape, b.dtype) for b in bigs]
                  + [SDS((N_CHIPS,) + small.shape, small.dtype)],
                  scratch_shapes=[pltpu.SemaphoreType.DMA((6 * n + 3,)), pltpu.SemaphoreType.DMA((6 * n + 3,))],
                  name="gather_weights")(*bigs, small)
    me = 2 * lax.axis_index("x") + lax.axis_index("y")
    return [lax.dynamic_update_index_in_dim(o, own, me, 0) for o, own in zip(outs, list(bigs) + [small])]


def _swap_halves(gs):
    n = len(gs)

    def body(*refs):
        send_sems, recv_sems = refs[2 * n:]
        x, y, c = _coords()
        cps = [_rcopy(refs[a].at[:, 1 - c], refs[n + a], send_sems, recv_sems, a, (x, y, 1 - c))
               for a in range(n)]
        for cp in cps:
            cp.start()
        for cp in cps:
            cp.wait()

    return _pcall(body, in_specs=[ANY] * n, out_specs=[ANY] * n,
                  out_shape=[SDS((N_CHIPS,) + g.shape[2:], g.dtype) for g in gs],
                  scratch_shapes=[pltpu.SemaphoreType.DMA((n,)), pltpu.SemaphoreType.DMA((n,))],
                  name="swap_halves")(*gs)


def _sum_cores(gs, rs, half_idx):
    n = len(gs)
    ns = 2
    in_specs, out_specs, out_shape = [], [], []
    for g in gs:
        _, _, h, w = g.shape
        in_specs.append(pl.BlockSpec((1, 1, h // ns, w), lambda b, i, c_ref: (b, c_ref[0], i, 0)))
    for g in gs:
        _, _, h, w = g.shape
        spec = pl.BlockSpec((1, h // ns, w), lambda b, i, c_ref: (b, i, 0))
        in_specs.append(spec)
        out_specs += [spec, spec]
        out_shape += [SDS((N_CHIPS, h, w), F32), SDS((N_CHIPS, h, w), BF16)]

    def body(c_ref, *refs):
        del c_ref
        for a in range(n):
            tot = refs[a][0] + refs[n + a][...]
            refs[2 * n + 2 * a][...] = tot
            refs[2 * n + 2 * a + 1][...] = tot.astype(BF16)

    outs = _pcall(body, grid_spec=pltpu.PrefetchScalarGridSpec(
        num_scalar_prefetch=1, grid=(N_CHIPS, ns), in_specs=in_specs, out_specs=out_specs),
        out_shape=out_shape, compiler_params=_cp(("arbitrary", "arbitrary")), name="sum_cores")(
            half_idx, *gs, *rs)
    return outs[0::2], outs[1::2]


def _scatter_chips(hs):
    n = len(hs)

    def body(*refs):
        send_sems, recv_sems = refs[2 * n:]
        x, y, c = _coords()
        cps = []
        for a in range(n):
            for j, (px, py) in enumerate(_other_chips(x, y)):
                cps.append(_rcopy(refs[a].at[2 * px + py], refs[n + a].at[j], send_sems, recv_sems,
                                  3 * a + j, (px, py, c)))
        for cp in cps:
            cp.start()
        for cp in cps:
            cp.wait()

    return _pcall(body, in_specs=[ANY] * n, out_specs=[ANY] * n,
                  out_shape=[SDS((3,) + h.shape[1:], h.dtype) for h in hs],
                  scratch_shapes=[pltpu.SemaphoreType.DMA((3 * n,)), pltpu.SemaphoreType.DMA((3 * n,))],
                  name="scatter_chips")(*hs)


def _sum_chips(hs, xs, chip_idx):
    n = len(hs)
    ns = 2
    in_specs, out_specs, out_shape = [], [], []
    for h_arr in hs:
        _, h, w = h_arr.shape
        in_specs.append(pl.BlockSpec((1, h // ns, w), lambda i, c_ref: (c_ref[0], i, 0)))
    for h_arr in hs:
        _, h, w = h_arr.shape
        in_specs.append(pl.BlockSpec((3, h // ns, w), lambda i, c_ref: (0, i, 0)))
        out_specs.append(pl.BlockSpec((h // ns, w), lambda i, c_ref: (i, 0)))
        out_shape.append(SDS((h, w), F32))

    def body(c_ref, *refs):
        del c_ref
        for a in range(n):
            x_ref = refs[n + a]
            refs[2 * n + a][...] = (refs[a][0] + x_ref[0].astype(F32) + x_ref[1].astype(F32)
                                    + x_ref[2].astype(F32))

    return _pcall(body, grid_spec=pltpu.PrefetchScalarGridSpec(
        num_scalar_prefetch=1, grid=(ns,), in_specs=in_specs, out_specs=out_specs),
        out_shape=out_shape, compiler_params=_cp(("arbitrary",)), name="sum_chips")(chip_idx, *hs, *xs)


def _swap_totals(tots):
    n = len(tots)

    def body(*refs):
        send_sems, recv_sems = refs[2 * n:]
        x, y, c = _coords()
        cps = [_rcopy(refs[a], refs[n + a], send_sems, recv_sems, a, (x, y, 1 - c)) for a in range(n)]
        for cp in cps:
            cp.start()
        for cp in cps:
            cp.wait()

    return _pcall(body, in_specs=[ANY] * n, out_specs=[ANY] * n,
                  out_shape=[SDS(t.shape, t.dtype) for t in tots],
                  scratch_shapes=[pltpu.SemaphoreType.DMA((n,)), pltpu.SemaphoreType.DMA((n,))],
                  name="swap_totals")(*tots)


def _allreduce_small(buf):
    rows = buf.shape[0]

    def body(b_ref, o_ref, g_ref, send_sems, recv_sems):
        x, y, c = _coords()
        me = 4 * x + 2 * y + c
        g_ref[me] = b_ref[...]
        cps = []
        for k in range(1, 8):
            px = 1 - x if k & 4 else x
            py = 1 - y if k & 2 else y
            pc = 1 - c if k & 1 else c
            cps.append(pltpu.make_async_remote_copy(
                src_ref=b_ref, dst_ref=g_ref.at[me], send_sem=send_sems.at[k - 1],
                recv_sem=recv_sems.at[k - 1], device_id=(px, py, pc), device_id_type=MESH))
        for cp in cps:
            cp.start()
        for cp in cps:
            cp.wait()
        acc = g_ref[0]
        for d in range(1, 8):
            acc = acc + g_ref[d]
        o_ref[...] = acc

    vm = pl.BlockSpec(memory_space=pltpu.VMEM)
    return _pcall(body, in_specs=[vm], out_specs=vm, out_shape=SDS(buf.shape, F32),
                  scratch_shapes=[pltpu.VMEM((8, rows, 128), F32), pltpu.SemaphoreType.DMA((7,)),
                                  pltpu.SemaphoreType.DMA((7,))],
                  compiler_params=pltpu.CompilerParams(vmem_limit_bytes=VMEM_LIMIT),
                  name="allreduce_small")(buf)


def _pack_flat(parts, mult):
    rows, offs, r0 = [], [], 0
    for p in parts:
        f = p.reshape(-1)
        f = jnp.pad(f, (0, (-f.shape[0]) % (mult * 128))).reshape(-1, 128)
        rows.append(f)
        offs.append((r0, p.shape))
        r0 += f.shape[0]
    return jnp.concatenate(rows, axis=0), offs


def _unpack_flat(buf, offs):
    out = []
    for r0, shape in offs:
        n = 1
        for s in shape:
            n *= s
        nr = -(-n // 128)
        out.append(buf[r0:r0 + nr].reshape(-1)[:n].reshape(shape))
    return out


def _gates_to_rows(a, heads, chunk, cpb):
    t = a.shape[0]
    return a.reshape(t // (chunk * cpb), cpb, chunk, heads).transpose(0, 3, 1, 2)


def _rows_to_gates(a):
    nb, heads, cpb, chunk = a.shape
    return a.transpose(0, 2, 3, 1).reshape(nb * cpb * chunk, heads)


def kernel(x, norm_w, gdn_w_in, gdn_conv_w, gdn_a_log, gdn_dt_bias, gdn_norm_w, gdn_w_out, ssd_w_in, ssd_conv_w, ssd_conv_b, ssd_dt_bias, ssd_a_log, ssd_d, ssd_norm_w, ssd_w_out, final_norm_w, loss_target, m_norm_w, m_gdn_w_in, m_gdn_conv_w, m_gdn_a_log, m_gdn_dt_bias, m_gdn_norm_w, m_gdn_w_out, m_ssd_w_in, m_ssd_conv_w, m_ssd_conv_b, m_ssd_dt_bias, m_ssd_a_log, m_ssd_d, m_ssd_norm_w, m_ssd_w_out, m_final_norm_w, v_norm_w, v_gdn_w_in, v_gdn_conv_w, v_gdn_a_log, v_gdn_dt_bias, v_gdn_norm_w, v_gdn_w_out, v_ssd_w_in, v_ssd_conv_w, v_ssd_conv_b, v_ssd_dt_bias, v_ssd_a_log, v_ssd_d, v_ssd_norm_w, v_ssd_w_out, v_final_norm_w):
    ws = dict(norm_w=norm_w, gdn_w_in=gdn_w_in, gdn_conv_w=gdn_conv_w, gdn_a_log=gdn_a_log,
              gdn_dt_bias=gdn_dt_bias, gdn_norm_w=gdn_norm_w, gdn_w_out=gdn_w_out, ssd_w_in=ssd_w_in,
              ssd_conv_w=ssd_conv_w, ssd_conv_b=ssd_conv_b, ssd_dt_bias=ssd_dt_bias,
              ssd_a_log=ssd_a_log, ssd_d=ssd_d, ssd_norm_w=ssd_norm_w, ssd_w_out=ssd_w_out,
              final_norm_w=final_norm_w)
    ms = dict(norm_w=m_norm_w, gdn_w_in=m_gdn_w_in, gdn_conv_w=m_gdn_conv_w, gdn_a_log=m_gdn_a_log,
              gdn_dt_bias=m_gdn_dt_bias, gdn_norm_w=m_gdn_norm_w, gdn_w_out=m_gdn_w_out,
              ssd_w_in=m_ssd_w_in, ssd_conv_w=m_ssd_conv_w, ssd_conv_b=m_ssd_conv_b,
              ssd_dt_bias=m_ssd_dt_bias, ssd_a_log=m_ssd_a_log, ssd_d=m_ssd_d,
              ssd_norm_w=m_ssd_norm_w, ssd_w_out=m_ssd_w_out, final_norm_w=m_final_norm_w)
    vs = dict(norm_w=v_norm_w, gdn_w_in=v_gdn_w_in, gdn_conv_w=v_gdn_conv_w, gdn_a_log=v_gdn_a_log,
              gdn_dt_bias=v_gdn_dt_bias, gdn_norm_w=v_gdn_norm_w, gdn_w_out=v_gdn_w_out,
              ssd_w_in=v_ssd_w_in, ssd_conv_w=v_ssd_conv_w, ssd_conv_b=v_ssd_conv_b,
              ssd_dt_bias=v_ssd_dt_bias, ssd_a_log=v_ssd_a_log, ssd_d=v_ssd_d,
              ssd_norm_w=v_ssd_norm_w, ssd_w_out=v_ssd_w_out, final_norm_w=v_final_norm_w)
    names = list(ws)
    cx, cy, cc = _coords()
    chip = 2 * cx + cy
    t = x.shape[1]
    x0 = x.reshape(t, D)
    tgt = loss_target.reshape(t, D)

    bigs = [p[0].astype(BF16) for p in (gdn_w_in, gdn_w_out, ssd_w_in, ssd_w_out)]
    small, small_offs = _pack_flat([gdn_conv_w[0], ssd_conv_w[0], ssd_conv_b[0], ssd_norm_w[0]], 8)
    a_gi, a_go, a_si, a_so, gsmall = _gather_weights(bigs, small)
    w_gi = jnp.concatenate([a_gi[b] for b in range(4)], axis=1)
    w_si = jnp.concatenate([a_si[b] for b in range(4)], axis=1)
    w_go = a_go.reshape(4 * OUT_SHARD, D)
    w_so = a_so.reshape(4 * OUT_SHARD, D)
    sm = [_unpack_flat(gsmall[b], small_offs) for b in range(4)]
    g_cw = jnp.concatenate([sm[b][0] for b in range(4)], axis=1)
    s_cw = jnp.concatenate([sm[b][1] for b in range(4)], axis=1)
    s_cb = jnp.concatenate([sm[b][2] for b in range(4)], axis=0)[None]
    s_nw = jnp.concatenate([sm[b][3] for b in range(4)], axis=0)[None]

    def pad_small(w):
        return jnp.concatenate([w, jnp.zeros((D, SMALL_W - w.shape[1]), w.dtype)], axis=1)

    wg_main, wg_small = w_gi[:, :G_MAIN], pad_small(w_gi[:, G_MAIN:])
    ws_main, ws_small = w_si[:, :S_MAIN], pad_small(w_si[:, S_MAIN:])
    zero_b = jnp.zeros((1, G_CONV), F32)
    nw0, nw1 = norm_w[0:1], norm_w[1:2]
    fw = final_norm_w[None]
    g_alog = gdn_a_log.reshape(GH, 1, 1)
    g_dtb = gdn_dt_bias.reshape(GH, 1, 1)
    g_nw = gdn_norm_w.reshape(1, GDK)
    s_dtb = ssd_dt_bias.reshape(SG, SR).T.reshape(SR, SG, 1, 1)
    s_alog = ssd_a_log.reshape(SG, SR).T.reshape(SR, SG, 1, 1)
    s_d = ssd_d.reshape(SG, SR).T.reshape(SR, SG, 1, 1)

    hid0, hid0_t = _rms_fwd(x0, nw0, "rms0")
    pg = _matmul(hid0, wg_main, "nn", "gdn_in_proj", tn=G_MAIN // 4)
    pg_small = _matmul(hid0, wg_small, "nn", "gdn_in_proj_small", tn=SMALL_W)
    post_q = _l2norm_scaled(GDK ** -0.5)
    post_k = _l2norm_scaled(1.0)
    q = _conv_fwd(pg, 0, g_cw[:, :G_QK], zero_b[:, :G_QK], GDK, post_q, "gdn_conv_q")
    k = _conv_fwd(pg, G_QK, g_cw[:, G_QK:2 * G_QK], zero_b[:, :G_QK], GDK, post_k, "gdn_conv_k")
    v = _conv_fwd(pg, 2 * G_QK, g_cw[:, 2 * G_QK:], zero_b[:, :G_V], GDK, _silu, "gdn_conv_v")
    braw = _gates_to_rows(pg_small[:, :GH], GH, GCH, GDN_CPB)
    araw = _gates_to_rows(pg_small[:, GH:2 * GH], GH, GCH, GDN_CPB)
    o, g_sall, g_tall = _gdn_fwd(q, k, v, braw, araw, g_alog, g_dtb)
    y0 = _gate_fwd(_gdn_gate, o, pg, G_CONV, g_nw, "gdn_gate")
    x1 = _matmul(y0, w_go, "nn", "gdn_out_proj", add=x0)

    hid1, hid1_t = _rms_fwd(x1, nw1, "rms1")
    ps = _matmul(hid1, ws_main, "nn", "ssd_in_proj", tn=S_MAIN // 4)
    ps_small = _matmul(hid1, ws_small, "nn", "ssd_in_proj_small", tn=SMALL_W)
    c_x, c_b, c_c = S_INNER, 2 * S_INNER, 2 * S_INNER + SG * SN
    post_s = _silu
    xs = _conv_fwd(ps, c_x, s_cw[:, :S_INNER], s_cb[:, :S_INNER], SR * SP, post_s, "ssd_conv_x")
    bm = _conv_fwd(ps, c_b, s_cw[:, S_INNER:S_INNER + SG * SN], s_cb[:, S_INNER:S_INNER + SG * SN], SN,
                   post_s, "ssd_conv_b")
    cm = _conv_fwd(ps, c_c, s_cw[:, S_INNER + SG * SN:], s_cb[:, S_INNER + SG * SN:], SN, post_s,
                   "ssd_conv_c")
    nbs = t // (SCH * SSD_CPB)
    dtraw = _gates_to_rows(ps_small[:, :SH], SH, SCH, SSD_CPB)
    dtraw = dtraw.reshape(nbs, SG, SR, SSD_CPB, SCH).transpose(0, 3, 2, 1, 4).reshape(nbs, SSD_CPB * SR, SG, 1, SCH)
    yss, s_sall = _ssd_fwd(xs, bm, cm, dtraw, s_dtb, s_alog, s_d)
    y1 = _gate_fwd(_ssd_gate, yss, ps, 0, s_nw, "ssd_gate")
    x2 = _matmul(y1, w_so, "nn", "ssd_out_proj", add=x1)

    dx2, d_fw, loss_row = _final_loss(x2, fw, tgt, "final_loss")

    dy1 = _matmul(dx2, w_so, "nt", "ssd_out_dx", out_dtype=BF16, tn=S_INNER)
    d_wso = _matmul(y1, dx2, "tn", "ssd_out_dw")
    dyss, dz_s, d_snw = _gate_bwd(_ssd_gate, yss, ps, 0, s_nw, dy1, True, "ssd_gate_bwd", Z_EXT)
    dxs, dbm, dcm, ddtraw, d_sdtb, d_salog, d_sd = _ssd_bwd(xs, bm, cm, dtraw, s_dtb, s_alog, s_d, s_sall, dyss)
    dps, dwx, dbx = _conv_dpre(ps, c_x, s_cw[:, :S_INNER], s_cb[:, :S_INNER], SR * SP, post_s, dxs, "ssd_dpre_x",
                               G_CONV, 0)
    dps, dwb, dbb = _conv_dpre(ps, c_b, s_cw[:, S_INNER:S_INNER + SG * SN], s_cb[:, S_INNER:S_INNER + SG * SN],
                               SN, post_s, dbm, "ssd_dpre_b", G_CONV, S_INNER, dps)
    dps, dwc, dbc = _conv_dpre(ps, c_c, s_cw[:, S_INNER + SG * SN:], s_cb[:, S_INNER + SG * SN:], SN, post_s,
                               dcm, "ssd_dpre_c", G_CONV, S_INNER + SG * SN, dps)
    d_scw = jnp.concatenate([dwx, dwb, dwc], axis=1)
    d_scb = jnp.concatenate([dbx, dbb, dbc], axis=1)
    dxbc = _conv_t(dps, s_cw, "ssd_conv_t")
    ddt = ddtraw.reshape(nbs, SSD_CPB, SR, SG, SCH).transpose(0, 3, 2, 1, 4).reshape(nbs, SH, SSD_CPB, SCH)
    ddt = _rows_to_gates(ddt)
    dsm_s = jnp.concatenate([ddt, jnp.zeros((t, SMALL_W - SH), F32)], axis=1).astype(BF16)
    dz_s = lax.dynamic_update_slice(dz_s, dsm_s, (0, S_INNER))
    ws_zx = jnp.concatenate([ws_main[:, :S_INNER], ws_small], axis=1)
    dhid1 = _matmul(dz_s, ws_zx, "nt", "ssd_in_dx_z", tk=Z_EXT)
    dhid1 = _matmul(dxbc, ws_main[:, S_INNER:], "nt", "ssd_in_dx_xbc", add=dhid1)
    dw_zx = _matmul(hid1_t, dz_s, "nn", "ssd_in_dw_z", tn=Z_EXT, tk=1024)
    d_wsi = jnp.concatenate([dw_zx[:, :S_INNER], _matmul(hid1_t, dxbc, "nn", "ssd_in_dw_xbc"),
                             dw_zx[:, S_INNER:S_INNER + SH]], axis=1)
    dx1, d_nw1 = _rms_bwd(x1, nw1, dx2, dhid1, "rms1_bwd")

    dy0 = _matmul(dx1, w_go, "nt", "gdn_out_dx", out_dtype=BF16, tn=G_V)
    d_wgo = _matmul(y0, dx1, "tn", "gdn_out_dw")
    do, dz_g, d_gnw = _gate_bwd(_gdn_gate, o, pg, G_CONV, g_nw, dy0, False, "gdn_gate_bwd", Z_EXT)
    dq, dk, dv, dbraw, daraw, d_galog, d_gdtb = _gdn_bwd(q, k, v, braw, araw, g_alog, g_dtb, g_sall, g_tall, do)
    dpg, dwq, _ = _conv_dpre(pg, 0, g_cw[:, :G_QK], zero_b[:, :G_QK], GDK, post_q, dq, "gdn_dpre_q", G_CONV, 0)
    dpg, dwk, _ = _conv_dpre(pg, G_QK, g_cw[:, G_QK:2 * G_QK], zero_b[:, :G_QK], GDK, post_k, dk, "gdn_dpre_k",
                             G_CONV, G_QK, dpg)
    dpg, dwv, _ = _conv_dpre(pg, 2 * G_QK, g_cw[:, 2 * G_QK:], zero_b[:, :G_V], GDK, _silu, dv, "gdn_dpre_v",
                             G_CONV, 2 * G_QK, dpg)
    d_gcw = jnp.concatenate([dwq, dwk, dwv], axis=1)
    dqkv = _conv_t(dpg, g_cw, "gdn_conv_t")
    dsm_g = jnp.concatenate([_rows_to_gates(dbraw), _rows_to_gates(daraw),
                             jnp.zeros((t, SMALL_W - 2 * GH), F32)], axis=1).astype(BF16)
    dz_g = lax.dynamic_update_slice(dz_g, dsm_g, (0, G_V))
    wg_zx = jnp.concatenate([wg_main[:, G_CONV:], wg_small], axis=1)
    dhid0 = _matmul(dqkv, wg_main[:, :G_CONV], "nt", "gdn_in_dx_qkv")
    dhid0 = _matmul(dz_g, wg_zx, "nt", "gdn_in_dx_z", add=dhid0, tk=Z_EXT)
    dw_zx = _matmul(hid0_t, dz_g, "nn", "gdn_in_dw_z", tn=Z_EXT, tk=1024)
    d_wgi = jnp.concatenate([_matmul(hid0_t, dqkv, "nn", "gdn_in_dw_qkv"), dw_zx[:, :G_V + 2 * GH]], axis=1)
    dx0, d_nw0 = _rms_bwd(x0, nw0, dx1, dhid0, "rms0_bwd")

    def in_blocks(dw):
        return dw.reshape(D, N_CHIPS, IN_SHARD).transpose(1, 0, 2).reshape(N_CHIPS, 2, D // 2, IN_SHARD)

    def out_blocks(dw):
        return dw.reshape(N_CHIPS, 2, OUT_SHARD // 2, D)

    gs = [in_blocks(d_wgi), out_blocks(d_wgo), in_blocks(d_wsi), out_blocks(d_wso)]
    from_sib = _swap_halves(gs)
    hsum, hsum_bf = _sum_cores(gs, from_sib, cc.astype(jnp.int32).reshape(1))
    recv = _scatter_chips(hsum_bf)
    tots = _sum_chips(hsum, recv, chip.astype(jnp.int32).reshape(1))
    sib_tots = _swap_totals(tots)
    full = [jnp.concatenate([jnp.where(cc == 0, mine, sib), jnp.where(cc == 0, sib, mine)], axis=0)
            for mine, sib in zip(tots, sib_tots)]
    grads = dict(
        gdn_w_in=full[0].reshape(1, D, IN_SHARD), gdn_w_out=full[1].reshape(1, OUT_SHARD, D),
        ssd_w_in=full[2].reshape(1, D, IN_SHARD), ssd_w_out=full[3].reshape(1, OUT_SHARD, D))

    small_parts = [loss_row, jnp.concatenate([d_nw0, d_nw1], axis=0), d_gcw, d_galog, d_gdtb, d_gnw, d_scw, d_scb,
                   d_sdtb, d_salog, d_sd, d_snw, d_fw]
    sbuf, soffs = _pack_flat(small_parts, 8)
    ssum = _unpack_flat(_allreduce_small(sbuf), soffs)
    (loss_s, g_nw_all, g_gcw, g_galog, g_gdtb, g_gnw, g_scw, g_scb, g_sdtb, g_salog, g_sd, g_snw, g_fw) = ssum

    def my_cols(a, width):
        return lax.dynamic_slice_in_dim(a, chip * width, width, axis=a.ndim - 1)

    grads.update(
        norm_w=g_nw_all, gdn_conv_w=my_cols(g_gcw, 1024)[None], gdn_a_log=g_galog.reshape(1, GH),
        gdn_dt_bias=g_gdtb.reshape(1, GH), gdn_norm_w=g_gnw.reshape(1, GDK),
        ssd_conv_w=my_cols(g_scw, 1024)[None], ssd_conv_b=my_cols(g_scb, 1024),
        ssd_dt_bias=g_sdtb.reshape(SR, SG).T.reshape(1, SH), ssd_a_log=g_salog.reshape(SR, SG).T.reshape(1, SH),
        ssd_d=g_sd.reshape(SR, SG).T.reshape(1, SH),
        ssd_norm_w=my_cols(g_snw, 512), final_norm_w=g_fw.reshape(D))
    loss = loss_s[0, 0]

    big_names = ("gdn_w_in", "gdn_w_out", "ssd_w_in", "ssd_w_out")
    deltas, new_m, new_v = {}, {}, {}
    for n in big_names:
        deltas[n], new_m[n], new_v[n] = _adamw(ws[n], grads[n], ms[n], vs[n], "adamw_" + n)
    rest = [n for n in names if n not in big_names]
    packs = [_pack_flat([d[n] for n in rest], 8) for d in (ws, grads, ms, vs)]
    outs = _adamw(*[p[0] for p in packs], "adamw_small")
    for d, buf in zip((deltas, new_m, new_v), outs):
        for n, a in zip(rest, _unpack_flat(buf, packs[0][1])):
            d[n] = a

    grad_x = dx0.reshape(1, t, D)
    return (loss, grad_x, *[grads[n] for n in names], *[deltas[n] for n in names],
            *[new_m[n] for n in names], *[new_v[n] for n in names])
```

```python
import functools

import jax
import jax.numpy as jnp
from jax import lax
from jax.experimental import pallas as pl
from jax.experimental.pallas import tpu as pltpu

F32 = jnp.float32
BF16 = jnp.bfloat16
SDS = jax.ShapeDtypeStruct
MESH = pl.DeviceIdType.MESH

D = 1024
EPS = 1e-6
CONV_K = 4
N_CHIPS = 4
GH = 16
GHQ = 8
GDK = 128
GCH = 64
G_QK = 1024
G_V = 2048
G_CONV = 4096
G_MAIN = 6144
G_IN = 6176
SH = 32
SP = 64
SN = 128
SG = 8
SR = 4
SCH = 128
S_INNER = 2048
S_MAIN = 6144
S_IN = 6176
IN_SHARD = 1544
OUT_SHARD = 512
SMALL_W = 128
Z_EXT = 2048 + SMALL_W

ADAM_LR = 0.001
ADAM_B1 = 0.9
ADAM_B2 = 0.999
ADAM_EPS = 1e-08
ADAM_WD = 0.01
ADAM_STEP = 10

VMEM_LIMIT = 56 * 1024 * 1024
BLOCK_ELEMS = 512 * 1024
PLAIN_BLOCK_ELEMS = 1024 * 1024
FWD_COLS = 512
NEG = -1e30


def _pcall(body, **kw):
    return pl.pallas_call(body, **kw)


def _cp(sem=None, vmem=VMEM_LIMIT):
    return pltpu.CompilerParams(dimension_semantics=sem, vmem_limit_bytes=vmem)


@jax.custom_jvp
def _sigmoid(x):
    return 1.0 / (1.0 + jnp.exp(-x))


@_sigmoid.defjvp
def _sigmoid_jvp(primals, tangents):
    s = _sigmoid(primals[0])
    return s, tangents[0] * (s * (1.0 - s))


@jax.custom_jvp
def _silu(x):
    return x * _sigmoid(x)


@_silu.defjvp
def _silu_jvp(primals, tangents):
    x = primals[0]
    s = _sigmoid(x)
    return x * s, tangents[0] * (s * (1.0 + x * (1.0 - s)))


def _softplus(x):
    return jnp.maximum(x, 0.0) + jnp.log(1.0 + jnp.exp(-jnp.abs(x)))


def _rms(x, w):
    return x * lax.rsqrt(jnp.mean(x * x, axis=-1, keepdims=True) + EPS) * w


_DIMS = {"nn": (((2,), (1,)), ((0,), (0,))),
         "nt": (((2,), (2,)), ((0,), (0,))),
         "tn": (((1,), (1,)), ((0,), (0,)))}


def _bdot(a, b, spec):
    return lax.dot_general(a.astype(BF16), b.astype(BF16), _DIMS[spec], preferred_element_type=F32)


@functools.partial(jax.custom_vjp, nondiff_argnums=(2,))
def _bmm(a, b, spec):
    return _bdot(a, b, spec)


def _bmm_fwd(a, b, spec):
    return _bdot(a, b, spec), (a, b)


def _bmm_bwd(spec, res, g):
    a, b = res
    if spec == "nn":
        return _bdot(g, b, "nt"), _bdot(a, g, "tn")
    if spec == "nt":
        return _bdot(g, b, "nn"), _bdot(g, a, "tn")
    return _bdot(b, g, "nt"), _bdot(a, g, "nn")


_bmm.defvjp(_bmm_fwd, _bmm_bwd)


@jax.custom_vjp
def _tri_inv(n):
    t = -n
    p = n
    steps = (n.shape[-1] - 1).bit_length() - 1
    r = lax.broadcasted_iota(jnp.int32, n.shape, 1)
    c = lax.broadcasted_iota(jnp.int32, n.shape, 2)
    t = t + jnp.where(r == c, 1.0, 0.0)
    for _ in range(steps):
        p = _bdot(p, p, "nn")
        t = t + _bdot(t, p, "nn")
    return t


def _tri_inv_fwd(n):
    t = _tri_inv(n)
    return t, t


def _tri_inv_bwd(t, g):
    return (-_bdot(_bdot(t, g, "tn"), t, "nt"),)


_tri_inv.defvjp(_tri_inv_fwd, _tri_inv_bwd)


@jax.custom_vjp
def _tri_inv_known(n, t):
    del n
    return t


def _tri_inv_known_fwd(n, t):
    del n
    return t, t


def _tri_inv_known_bwd(t, g):
    return _tri_inv_bwd(t, g)[0], jnp.zeros_like(t)


_tri_inv_known.defvjp(_tri_inv_known_fwd, _tri_inv_known_bwd)


def _masks(c, lead=1):
    r = lax.broadcasted_iota(jnp.int32, (lead, c, c), 1)
    s = lax.broadcasted_iota(jnp.int32, (lead, c, c), 2)
    return r >= s, r > s, r == s, r <= s


def _row_to_col(row, eye):
    return jnp.sum(jnp.where(eye, row, 0.0), axis=2, keepdims=True)


def _gdn_chunk(q, k, v, braw, araw, alog, dtb, s, t_known=None, want_t=False):
    h = v.shape[0]
    c = v.shape[1]
    rep = h // q.shape[0]
    tril, strict, eye, triu = _masks(c)
    qq = jnp.broadcast_to(q[:, None], (q.shape[0], rep) + q.shape[1:]).reshape(v.shape)
    kk = jnp.broadcast_to(k[:, None], (k.shape[0], rep) + k.shape[1:]).reshape(v.shape)
    beta_row = _sigmoid(braw)
    g_row = -jnp.exp(alog) * _softplus(araw + dtb)
    beta_col = _row_to_col(beta_row, eye)
    g_col = _row_to_col(g_row, eye)
    gc_col = jnp.sum(jnp.where(tril, g_row, 0.0), axis=2, keepdims=True)
    gc_row = jnp.sum(jnp.where(triu, g_col, 0.0), axis=1, keepdims=True)
    gc_last = jnp.sum(g_row, axis=2, keepdims=True)
    lmat = jnp.exp(jnp.where(tril, gc_col - gc_row, NEG))
    kb = kk * beta_col
    vb = v * beta_col
    n = jnp.where(strict, _bmm(kb, kk, "nt") * lmat, 0.0)
    t = _tri_inv(n) if t_known is None else _tri_inv_known(n, t_known)
    e_col = jnp.exp(gc_col)
    u = _bmm(t, vb, "nn")
    w = _bmm(t, kb * e_col, "nn")
    attn = _bmm(qq, kk, "nt") * lmat
    q_dec = qq * e_col
    k_dec = kk * jnp.exp(gc_last - gc_col)
    v_new = u - _bmm(w, s, "nn")
    o = _bmm(q_dec, s, "nn") + _bmm(attn, v_new, "nn")
    s_new = s * jnp.exp(gc_last) + _bmm(k_dec, v_new, "tn")
    return (o, s_new, t) if want_t else (o, s_new)


def _ssd_chunk(xs, bm, cm, dtraw, dtb, alog, dskip, s):
    c = xs.shape[1]
    tril, _, eye, triu = _masks(c)
    lane = lax.broadcasted_iota(jnp.int32, (1, 1, SR * SP), 2)
    prow = lax.broadcasted_iota(jnp.int32, (1, SR * SP, 1), 1)
    cb = _bmm(cm, bm, "nt")
    cs = _bmm(cm, s, "nt")

    def per_head(vals, idx):
        out = vals[SR - 1]
        for r in reversed(range(SR - 1)):
            out = jnp.where(idx < (r + 1) * SP, vals[r], out)
        return out

    dt_cols, e_cols, lmats, dstates, declast = [], [], [], [], []
    for r in range(SR):
        dt_row = _softplus(dtraw[r] + dtb[r])
        adt_row = -jnp.exp(alog[r]) * dt_row
        dt_cols.append(_row_to_col(dt_row, eye))
        adt_col = _row_to_col(adt_row, eye)
        acs_col = jnp.sum(jnp.where(tril, adt_row, 0.0), axis=2, keepdims=True)
        acs_row = jnp.sum(jnp.where(triu, adt_col, 0.0), axis=1, keepdims=True)
        acs_last = jnp.sum(adt_row, axis=2, keepdims=True)
        lmats.append(jnp.exp(jnp.where(tril, acs_col - acs_row, NEG)))
        e_cols.append(jnp.exp(acs_col))
        dstates.append(jnp.exp(acs_last - acs_col))
        declast.append(jnp.exp(acs_last))
    xd = xs * per_head(dt_cols, lane)
    y = per_head([_bmm(cb * lmats[r], xd, "nn") for r in range(SR)], lane)
    states = per_head([_bmm(xd, bm * dstates[r], "tn") for r in range(SR)], prow)
    y = y + cs * per_head(e_cols, lane) + xs * per_head([dskip[r] for r in range(SR)], lane)
    s_new = s * per_head(declast, prow) + states
    return y, s_new


def _matmul(a, b, mode, name, out_dtype=F32, add=None, tm=1024, tn=1024, tk=2048):
    if mode == "nn":
        (m, k), n = a.shape, b.shape[1]
    elif mode == "nt":
        (m, k), n = a.shape, b.shape[0]
    else:
        (k, m), n = a.shape, b.shape[1]
    tm, tn, tk = min(tm, m), min(tn, n), min(tk, k)
    assert m % tm == 0 and n % tn == 0 and k % tk == 0, (name, m, n, k)
    nk = k // tk
    dims = {"nn": (((1,), (0,)), ((), ())), "nt": (((1,), (1,)), ((), ())),
            "tn": (((0,), (0,)), ((), ()))}[mode]
    a_spec = {"nn": pl.BlockSpec((tm, tk), lambda i, j, kk: (i, kk)),
              "nt": pl.BlockSpec((tm, tk), lambda i, j, kk: (i, kk)),
              "tn": pl.BlockSpec((tk, tm), lambda i, j, kk: (kk, i))}[mode]
    b_spec = {"nn": pl.BlockSpec((tk, tn), lambda i, j, kk: (kk, j)),
              "nt": pl.BlockSpec((tn, tk), lambda i, j, kk: (j, kk)),
              "tn": pl.BlockSpec((tk, tn), lambda i, j, kk: (kk, j))}[mode]
    o_spec = pl.BlockSpec((tm, tn), lambda i, j, kk: (i, j))
    has_add = add is not None

    def body(*refs):
        a_ref, b_ref = refs[:2]
        add_ref = refs[2] if has_add else None
        o_ref = refs[2 + has_add]
        part = lax.dot_general(a_ref[...].astype(BF16), b_ref[...].astype(BF16), dims,
                               preferred_element_type=F32)

        def finish(r):
            if has_add:
                r = r + add_ref[...].astype(F32)
            o_ref[...] = r.astype(o_ref.dtype)

        if nk == 1:
            finish(part)
            return
        acc_ref = refs[3 + has_add]
        kk = pl.program_id(2)

        @pl.when(kk == 0)
        def _():
            acc_ref[...] = part

        @pl.when(jnp.logical_and(kk > 0, kk < nk - 1))
        def _():
            acc_ref[...] += part

        @pl.when(kk == nk - 1)
        def _():
            finish(acc_ref[...] + part)

    ins = [a, b] + ([add] if has_add else [])
    in_specs = [a_spec, b_spec] + ([o_spec] if has_add else [])
    scratch = [] if nk == 1 else [pltpu.VMEM((tm, tn), F32)]
    return _pcall(body, grid=(m // tm, n // tn, nk), in_specs=in_specs, out_specs=o_spec,
                  out_shape=SDS((m, n), out_dtype), scratch_shapes=scratch,
                  compiler_params=_cp(("parallel", "parallel", "arbitrary")), name=name)(*ins)


def _ew(fn, ins, in_specs, out_shape, out_specs, grid, name):
    n_in = len(ins)

    def body(*refs):
        outs = fn(*[r[...] for r in refs[:n_in]])
        for r, o in zip(refs[n_in:], outs):
            r[...] = o.astype(r.dtype)

    return _pcall(body, grid=grid, in_specs=in_specs, out_specs=out_specs, out_shape=out_shape,
                  compiler_params=_cp(("arbitrary",) * len(grid)), name=name)(*ins)


def _ew_vjp(fn, ins, in_specs, cts, ct_specs, wrt, g_shape, g_specs, acc, grid, name):
    n_in, n_ct = len(ins), len(cts)

    def body(*refs):
        vals = [r[...].astype(F32) for r in refs[:n_in]]
        outs, vjp = jax.vjp(fn, *vals)
        g_all = vjp(tuple(r[...].astype(F32) for r in refs[n_in:n_in + n_ct]))
        for pos, (i, g_ref) in enumerate(zip(wrt, refs[n_in + n_ct:])):
            g = g_all[i]
            if pos in acc:
                first = functools.reduce(
                    jnp.logical_and, [pl.program_id(ax) == 0 for ax in range(acc[pos], len(grid))])

                @pl.when(first)
                def _():
                    g_ref[...] = jnp.zeros_like(g_ref)

                g_ref[...] += g.astype(g_ref.dtype)
            else:
                g_ref[...] = g.astype(g_ref.dtype)

    return _pcall(body, grid=grid, in_specs=list(in_specs) + list(ct_specs), out_specs=g_specs,
                  out_shape=g_shape, compiler_params=_cp(("arbitrary",) * len(grid)),
                  name=name)(*ins, *cts)


def _row_spec(tm, n):
    return pl.BlockSpec((tm, n), lambda i: (i, 0))


def _par_spec(n):
    return pl.BlockSpec((1, n), lambda i: (0, 0))


def _rms_fwd(x, w, name):
    t = x.shape[0]
    tm = min(t, 512)
    return _ew(lambda xv, wv: (_rms(xv, wv),), [x, w], [_row_spec(tm, D), _par_spec(D)],
               [SDS((t, D), BF16)], [_row_spec(tm, D)], (t // tm,), name)[0]


def _rms_bwd(x, w, dres, dhid, name):
    t = x.shape[0]
    tm = min(t, 512)
    return _ew_vjp(lambda xv, wv: (xv, _rms(xv, wv)), [x, w], [_row_spec(tm, D), _par_spec(D)],
                   [dres, dhid], [_row_spec(tm, D), _row_spec(tm, D)], (0, 1),
                   [SDS((t, D), F32), SDS((1, D), F32)], [_row_spec(tm, D), _par_spec(D)],
                   {1: 0}, (t // tm,), name)


def _final_loss(x, w, tgt, name):
    t = x.shape[0]
    tm = min(t, 512)

    def body(x_ref, w_ref, t_ref, dx_ref, dw_ref, l_ref):
        @pl.when(pl.program_id(0) == 0)
        def _():
            dw_ref[...] = jnp.zeros_like(dw_ref)
            l_ref[...] = jnp.zeros_like(l_ref)

        xv, wv = x_ref[...], w_ref[...]
        rstd = lax.rsqrt(jnp.mean(xv * xv, axis=-1, keepdims=True) + EPS)
        xh = xv * rstd
        err = xh * wv - t_ref[...]
        l_ref[...] += 0.5 * jnp.sum(jnp.mean(err * err, axis=-1, keepdims=True), axis=0, keepdims=True)
        dy = err * (1.0 / D)
        dw_ref[...] += jnp.sum(dy * xh, axis=0, keepdims=True)
        dxh = dy * wv
        dx_ref[...] = rstd * (dxh - xh * jnp.mean(dxh * xh, axis=-1, keepdims=True))

    return _pcall(body, grid=(t // tm,), in_specs=[_row_spec(tm, D), _par_spec(D), _row_spec(tm, D)],
                  out_specs=[_row_spec(tm, D), _par_spec(D), _par_spec(128)],
                  out_shape=[SDS((t, D), F32), SDS((1, D), F32), SDS((1, 128), F32)],
                  compiler_params=_cp(("arbitrary",)), name=name)(x, w, tgt)


def _conv_taps(ext, w_ref, tm, lo):
    n = ext.shape[0]
    acc = None
    for j in range(CONV_K):
        shift = (CONV_K - 1 - j) if lo else (n - (CONV_K - 1 - j)) % n
        rolled = pltpu.roll(ext, shift, 0) if shift else ext
        term = w_ref[pl.ds(j, 1), :] * rolled[lo:lo + tm]
        acc = term if acc is None else acc + term
    return acc


def _conv_pre_specs(tm, ct, col0):
    hb = tm // 8
    return [pl.BlockSpec((tm, ct), lambda j, i: (i, col0 + j)),
            pl.BlockSpec((8, ct), lambda j, i: (jnp.maximum(i * hb - 1, 0), col0 + j)),
            pl.BlockSpec((CONV_K, ct), lambda j, i: (0, j)),
            pl.BlockSpec((1, ct), lambda j, i: (0, j))]


def _conv_pre_value(x_ref, xh_ref, w_ref, b_ref, tm):
    halo = jnp.where(pl.program_id(1) > 0, xh_ref[...], 0.0)
    ext = jnp.concatenate([halo, x_ref[...]], axis=0)
    return _conv_taps(ext, w_ref, tm, 8) + b_ref[...], ext


def _conv_fwd(x, col0, w, b, ct, post, name):
    t = x.shape[0]
    tpb = FWD_COLS // ct
    wide = tpb * ct
    tm = min(t, PLAIN_BLOCK_ELEMS // wide)
    nt = w.shape[1] // ct
    assert col0 % wide == 0 and nt % tpb == 0, name

    def body(x_ref, xh_ref, w_ref, b_ref, o_ref):
        pre, _ = _conv_pre_value(x_ref, xh_ref, w_ref, b_ref, tm)
        for k in range(tpb):
            o_ref[k] = post(pre[:, k * ct:(k + 1) * ct])

    return _pcall(body, grid=(nt // tpb, t // tm), in_specs=_conv_pre_specs(tm, wide, col0 // wide),
                  out_specs=pl.BlockSpec((tpb, tm, ct), lambda j, i: (j, i, 0)),
                  out_shape=SDS((nt, t, ct), F32), compiler_params=_cp(("arbitrary", "arbitrary")),
                  name=name)(x, x, w, b)


def _conv_dpre(x, col0, w, b, ct, post, dout, name, c_total, c_off, into=None):
    t = x.shape[0]
    tm = min(t, BLOCK_ELEMS // ct)
    nt = w.shape[1] // ct
    chained = into is not None

    def body(*refs):
        x_ref, xh_ref, w_ref, b_ref, do_ref = refs[:5]
        dp_ref, dw_ref, db_ref = refs[5 + chained:]
        pre, ext = _conv_pre_value(x_ref, xh_ref, w_ref, b_ref, tm)
        _, vjp = jax.vjp(post, pre)
        dpre = vjp(do_ref[0])[0]
        dp_ref[...] = dpre

        @pl.when(pl.program_id(1) == 0)
        def _():
            dw_ref[...] = jnp.zeros_like(dw_ref)
            db_ref[...] = jnp.zeros_like(db_ref)

        for j in range(CONV_K):
            xs = (pltpu.roll(ext, CONV_K - 1 - j, 0) if j < CONV_K - 1 else ext)[8:8 + tm]
            dw_ref[pl.ds(j, 1), :] += jnp.sum(dpre * xs, axis=0, keepdims=True)
        db_ref[...] += jnp.sum(dpre, axis=0, keepdims=True)

    c = w.shape[1]
    o0 = c_off // ct
    return _pcall(body, grid=(nt, t // tm),
                  in_specs=_conv_pre_specs(tm, ct, col0 // ct)
                  + [pl.BlockSpec((1, tm, ct), lambda j, i: (j, i, 0))] + ([ANY] if chained else []),
                  out_specs=[pl.BlockSpec((tm, ct), lambda j, i: (i, o0 + j)),
                             pl.BlockSpec((CONV_K, ct), lambda j, i: (0, j)),
                             pl.BlockSpec((1, ct), lambda j, i: (0, j))],
                  out_shape=[SDS((t, c_total), F32), SDS((CONV_K, c), F32), SDS((1, c), F32)],
                  input_output_aliases={5: 0} if chained else {},
                  compiler_params=_cp(("arbitrary", "arbitrary")), name=name)(
                      x, x, w, b, dout, *([into] if chained else []))


def _conv_t(dpre, w, name):
    t, c = dpre.shape
    ct = min(c, 512)
    tm = min(t, PLAIN_BLOCK_ELEMS // ct)
    hb = tm // 8
    last = t // tm - 1

    def body(d_ref, dh_ref, w_ref, o_ref):
        halo = jnp.where(pl.program_id(1) < last, dh_ref[...], 0.0)
        ext = jnp.concatenate([d_ref[...], halo], axis=0)
        o_ref[...] = _conv_taps(ext, w_ref, tm, 0).astype(o_ref.dtype)

    return _pcall(body, grid=(c // ct, t // tm),
                  in_specs=[pl.BlockSpec((tm, ct), lambda j, i: (i, j)),
                            pl.BlockSpec((8, ct), lambda j, i: (jnp.minimum((i + 1) * hb, t // 8 - 1), j)),
                            pl.BlockSpec((CONV_K, ct), lambda j, i: (0, j))],
                  out_specs=pl.BlockSpec((tm, ct), lambda j, i: (i, j)),
                  out_shape=SDS((t, c), BF16), compiler_params=_cp(("arbitrary", "arbitrary")),
                  name=name)(dpre, dpre, w)


def _l2norm_scaled(scale):
    def post(pre):
        a = _silu(pre)
        return a * lax.rsqrt(jnp.sum(a * a, axis=-1, keepdims=True) + EPS) * scale
    return post


GDN_HB = 16
GDN_CPB = 4
SSD_GB = 8
SSD_CPB_FWD = 2
SSD_CPB_BWD = 1


def _gdn_specs(nb, rev):
    hb, cpb, tc = GDN_HB, GDN_CPB, GDN_CPB * GCH
    blk = (lambda n: nb - 1 - n) if rev else (lambda n: n)
    seq = lambda h: pl.BlockSpec((h, tc, GDK), lambda g, n: (g, blk(n), 0))
    gate = pl.BlockSpec((1, hb, cpb, GCH), lambda g, n: (blk(n), g, 0, 0))
    par = pl.BlockSpec((hb, 1, 1), lambda g, n: (g, 0, 0))
    state = pl.BlockSpec((cpb, hb, GDK, GDK), lambda g, n: (blk(n), g, 0, 0))
    tinv = pl.BlockSpec((cpb, hb, GCH, GCH), lambda g, n: (blk(n), g, 0, 0))
    return seq, gate, par, state, tinv


def _gdn_fwd(q, k, v, braw, araw, alog, dtb):
    t = v.shape[1]
    hb, cpb = GDN_HB, GDN_CPB
    nb = t // (cpb * GCH)
    seq, gate, par, state, tinv = _gdn_specs(nb, False)

    def body(q_ref, k_ref, v_ref, b_ref, a_ref, al_ref, dt_ref, o_ref, sall_ref, tall_ref, s_ref):
        @pl.when(pl.program_id(1) == 0)
        def _():
            s_ref[...] = jnp.zeros_like(s_ref)

        s = s_ref[...]
        for c in range(cpb):
            rows = pl.ds(c * GCH, GCH)
            sall_ref[c] = s
            o, s, tmat = _gdn_chunk(q_ref[:, rows, :], k_ref[:, rows, :], v_ref[:, rows, :],
                                    b_ref[0, :, pl.ds(c, 1), :], a_ref[0, :, pl.ds(c, 1), :],
                                    al_ref[...], dt_ref[...], s, want_t=True)
            o_ref[:, rows, :] = o
            tall_ref[c] = tmat.astype(BF16)
        s_ref[...] = s

    return _pcall(body, grid=(GH // hb, nb),
                  in_specs=[seq(hb // 2), seq(hb // 2), seq(hb), gate, gate, par, par],
                  out_specs=[seq(hb), state, tinv],
                  out_shape=[SDS((GH, t, GDK), F32), SDS((t // GCH, GH, GDK, GDK), F32),
                             SDS((t // GCH, GH, GCH, GCH), BF16)],
                  scratch_shapes=[pltpu.VMEM((hb, GDK, GDK), F32)],
                  compiler_params=_cp(("arbitrary", "arbitrary")), name="gdn_chunk_fwd")(
                      q, k, v, braw, araw, alog, dtb)


def _gdn_bwd(q, k, v, braw, araw, alog, dtb, sall, tall, do):
    t = v.shape[1]
    hb, cpb = GDN_HB, GDN_CPB
    nb = t // (cpb * GCH)
    seq, gate, par, state, tinv = _gdn_specs(nb, True)

    def body(q_ref, k_ref, v_ref, b_ref, a_ref, al_ref, dt_ref, sall_ref, tall_ref, do_ref,
             dq_ref, dk_ref, dv_ref, db_ref, da_ref, dal_ref, ddt_ref, ds_ref):
        @pl.when(pl.program_id(1) == 0)
        def _():
            ds_ref[...] = jnp.zeros_like(ds_ref)
            dal_ref[...] = jnp.zeros_like(dal_ref)
            ddt_ref[...] = jnp.zeros_like(ddt_ref)

        ds = ds_ref[...]
        for c in reversed(range(cpb)):
            rows = pl.ds(c * GCH, GCH)
            fn = functools.partial(_gdn_chunk, t_known=tall_ref[c].astype(F32))
            _, vjp = jax.vjp(fn, q_ref[:, rows, :], k_ref[:, rows, :], v_ref[:, rows, :],
                             b_ref[0, :, pl.ds(c, 1), :], a_ref[0, :, pl.ds(c, 1), :],
                             al_ref[...], dt_ref[...], sall_ref[c])
            dq, dk, dv, db, da, dal, ddt, ds = vjp((do_ref[:, rows, :], ds))
            dq_ref[:, rows, :] = dq
            dk_ref[:, rows, :] = dk
            dv_ref[:, rows, :] = dv
            db_ref[0, :, pl.ds(c, 1), :] = db
            da_ref[0, :, pl.ds(c, 1), :] = da
            dal_ref[...] += dal
            ddt_ref[...] += ddt
        ds_ref[...] = ds

    return _pcall(body, grid=(GH // hb, nb),
                  in_specs=[seq(hb // 2), seq(hb // 2), seq(hb), gate, gate, par, par, state, tinv, seq(hb)],
                  out_specs=[seq(hb // 2), seq(hb // 2), seq(hb), gate, gate, par, par],
                  out_shape=[SDS(q.shape, F32), SDS(k.shape, F32), SDS(v.shape, F32),
                             SDS(braw.shape, F32), SDS(araw.shape, F32),
                             SDS((GH, 1, 1), F32), SDS((GH, 1, 1), F32)],
                  scratch_shapes=[pltpu.VMEM((hb, GDK, GDK), F32)],
                  compiler_params=_cp(("arbitrary", "arbitrary")), name="gdn_chunk_bwd")(
                      q, k, v, braw, araw, alog, dtb, sall, tall, do)


def _ssd_specs(nb, rev, cpb):
    gb, tc = SSD_GB, cpb * SCH
    blk = (lambda n: nb - 1 - n) if rev else (lambda n: n)
    seq = lambda w: pl.BlockSpec((gb, tc, w), lambda g, n: (g, blk(n), 0))
    gate = pl.BlockSpec((1, cpb * SR, gb, 1, SCH), lambda g, n: (blk(n), 0, g, 0, 0))
    par = pl.BlockSpec((SR, gb, 1, 1), lambda g, n: (0, g, 0, 0))
    state = pl.BlockSpec((cpb, gb, SR * SP, SN), lambda g, n: (blk(n), g, 0, 0))
    return seq, gate, par, state


def _ssd_fwd(xs, bm, cm, dtraw, dtb, alog, dskip):
    t = xs.shape[1]
    gb, cpb = SSD_GB, SSD_CPB_FWD
    nb = t // (cpb * SCH)
    seq, gate, par, state = _ssd_specs(nb, False, cpb)

    def body(x_ref, b_ref, c_ref, dt_ref, dtb_ref, al_ref, dk_ref, y_ref, sall_ref, s_ref):
        @pl.when(pl.program_id(1) == 0)
        def _():
            s_ref[...] = jnp.zeros_like(s_ref)

        s = s_ref[...]
        for c in range(cpb):
            rows = pl.ds(c * SCH, SCH)
            sall_ref[c] = s
            y, s = _ssd_chunk(x_ref[:, rows, :], b_ref[:, rows, :], c_ref[:, rows, :],
                              dt_ref[0, pl.ds(c * SR, SR)], dtb_ref[...], al_ref[...],
                              dk_ref[...], s)
            y_ref[:, rows, :] = y
        s_ref[...] = s

    return _pcall(body, grid=(SG // gb, nb),
                  in_specs=[seq(SR * SP), seq(SN), seq(SN), gate, par, par, par],
                  out_specs=[seq(SR * SP), state],
                  out_shape=[SDS((SG, t, SR * SP), F32), SDS((t // SCH, SG, SR * SP, SN), F32)],
                  scratch_shapes=[pltpu.VMEM((gb, SR * SP, SN), F32)],
                  compiler_params=_cp(("arbitrary", "arbitrary")), name="ssd_chunk_fwd")(
                      xs, bm, cm, dtraw, dtb, alog, dskip)


def _ssd_bwd(xs, bm, cm, dtraw, dtb, alog, dskip, sall, dy):
    t = xs.shape[1]
    gb, cpb = SSD_GB, SSD_CPB_BWD
    nb = t // (cpb * SCH)
    seq, gate, par, state = _ssd_specs(nb, True, cpb)

    def body(x_ref, b_ref, c_ref, dt_ref, dtb_ref, al_ref, dk_ref, sall_ref, dy_ref,
             dx_ref, dbm_ref, dcm_ref, ddt_ref, ddtb_ref, dal_ref, ddk_ref, ds_ref):
        @pl.when(pl.program_id(1) == 0)
        def _():
            ds_ref[...] = jnp.zeros_like(ds_ref)
            ddtb_ref[...] = jnp.zeros_like(ddtb_ref)
            dal_ref[...] = jnp.zeros_like(dal_ref)
            ddk_ref[...] = jnp.zeros_like(ddk_ref)

        ds = ds_ref[...]
        for c in reversed(range(cpb)):
            rows = pl.ds(c * SCH, SCH)
            _, vjp = jax.vjp(_ssd_chunk, x_ref[:, rows, :], b_ref[:, rows, :], c_ref[:, rows, :],
                             dt_ref[0, pl.ds(c * SR, SR)], dtb_ref[...], al_ref[...],
                             dk_ref[...], sall_ref[c])
            dx, dbm, dcm, ddt, ddtb, dal, ddk, ds = vjp((dy_ref[:, rows, :], ds))
            dx_ref[:, rows, :] = dx
            dbm_ref[:, rows, :] = dbm
            dcm_ref[:, rows, :] = dcm
            ddt_ref[0, pl.ds(c * SR, SR)] = ddt
            ddtb_ref[...] += ddtb
            dal_ref[...] += dal
            ddk_ref[...] += ddk
        ds_ref[...] = ds

    return _pcall(body, grid=(SG // gb, nb),
                  in_specs=[seq(SR * SP), seq(SN), seq(SN), gate, par, par, par, state, seq(SR * SP)],
                  out_specs=[seq(SR * SP), seq(SN), seq(SN), gate, par, par, par],
                  out_shape=[SDS(xs.shape, F32), SDS(bm.shape, F32), SDS(cm.shape, F32),
                             SDS(dtraw.shape, F32), SDS((SR, SG, 1, 1), F32), SDS((SR, SG, 1, 1), F32),
                             SDS((SR, SG, 1, 1), F32)],
                  scratch_shapes=[pltpu.VMEM((gb, SR * SP, SN), F32)],
                  compiler_params=_cp(("arbitrary", "arbitrary")), name="ssd_chunk_bwd")(
                      xs, bm, cm, dtraw, dtb, alog, dskip, sall, dy)


def _gate_specs(tm, ct, zcol0, per_tile_w):
    z0 = zcol0 // ct
    return [pl.BlockSpec((1, tm, ct), lambda i, j: (j, i, 0)),
            pl.BlockSpec((tm, ct), lambda i, j: (i, z0 + j)),
            pl.BlockSpec((1, ct), (lambda i, j: (0, j)) if per_tile_w else (lambda i, j: (0, 0)))]


def _gdn_gate(o, z, w):
    return (_rms(o[0], w) * _silu(z),)


def _ssd_gate(y, z, w):
    return (_rms(y[0] * _silu(z), w),)


def _gate_fwd(fn, o, proj, zcol0, w, name):
    nt, t, ct = o.shape
    tpb = FWD_COLS // ct
    wide = tpb * ct
    tm = min(t, PLAIN_BLOCK_ELEMS // wide)
    per_tile_w = w.shape[1] > ct
    assert zcol0 % wide == 0 and nt % tpb == 0, name
    z0 = zcol0 // wide
    specs = [pl.BlockSpec((tpb, tm, ct), lambda i, j: (j, i, 0)),
             pl.BlockSpec((tm, wide), lambda i, j: (i, z0 + j)),
             pl.BlockSpec((1, wide), lambda i, j: (0, j)) if per_tile_w
             else pl.BlockSpec((1, ct), lambda i, j: (0, 0))]

    def fn_wide(ov, zv, wv):
        outs = []
        for k in range(tpb):
            cols = slice(k * ct, (k + 1) * ct)
            outs.append(fn(ov[k:k + 1], zv[:, cols], wv[:, cols] if per_tile_w else wv)[0])
        return (jnp.concatenate(outs, axis=1),)

    return _ew(fn_wide, [o, proj, w], specs, [SDS((t, nt * ct), BF16)],
               [pl.BlockSpec((tm, wide), lambda i, j: (i, j))], (t // tm, nt // tpb), name)[0]


def _gate_bwd(fn, o, proj, zcol0, w, dy, wacc, name, dz_cols):
    nt, t, ct = o.shape
    tm = min(t, BLOCK_ELEMS // ct)
    specs = _gate_specs(tm, ct, zcol0, wacc)
    out_spec = pl.BlockSpec((tm, ct), lambda i, j: (i, j))
    if wacc:
        flip = lambda s: pl.BlockSpec(s.block_shape, lambda j, i, f=s.index_map: f(i, j))
        specs = [flip(s) for s in specs]
        out_spec = flip(out_spec)
        grid, acc = (nt, t // tm), {2: 1}
    else:
        grid, acc = (t // tm, nt), {2: 0}
    return _ew_vjp(fn, [o, proj, w], specs, [dy], [out_spec], (0, 1, 2),
                   [SDS(o.shape, F32), SDS((t, dz_cols), BF16), SDS(w.shape, F32)],
                   [specs[0], out_spec, specs[2]], acc, grid, name)


def _adamw_math(w, g, m, v):
    m = ADAM_B1 * m + (1.0 - ADAM_B1) * g
    v = ADAM_B2 * v + (1.0 - ADAM_B2) * jnp.square(g)
    m_hat = m / (1.0 - ADAM_B1 ** ADAM_STEP)
    v_hat = v / (1.0 - ADAM_B2 ** ADAM_STEP)
    delta = -ADAM_LR * (m_hat / (jnp.sqrt(v_hat) + ADAM_EPS) + ADAM_WD * w)
    return delta, m, v


def _adamw(w, g, m, v, name):
    shape = w.shape
    w2, g2, m2, v2 = [a.reshape(-1, shape[-1]) for a in (w, g, m, v)]
    r, c = w2.shape
    tr = 256 if r % 256 == 0 else r
    spec = pl.BlockSpec((tr, c), lambda i: (i, 0))
    outs = _ew(_adamw_math, [w2, g2, m2, v2], [spec] * 4, [SDS((r, c), F32)] * 3, [spec] * 3,
               (r // tr,), name)
    return [o.reshape(shape) for o in outs]


def _coords():
    return lax.axis_index("x"), lax.axis_index("y"), lax.axis_index("c")


def _other_chips(x, y):
    return [(1 - x, y), (x, 1 - y), (1 - x, 1 - y)]


ANY = pl.BlockSpec(memory_space=pl.ANY)


def _rcopy(src, dst, send_sems, recv_sems, k, to):
    return pltpu.make_async_remote_copy(src_ref=src, dst_ref=dst, send_sem=send_sems.at[k],
                                        recv_sem=recv_sems.at[k], device_id=to, device_id_type=MESH)


def _gather_weights(bigs, small):
    n = len(bigs)

    def body(*refs):
        in_refs, small_ref = refs[:n], refs[n]
        out_refs, osmall_ref = refs[n + 1:2 * n + 1], refs[2 * n + 1]
        send_sems, recv_sems = refs[2 * n + 2:]
        x, y, c = _coords()
        me = 2 * x + y
        sibling = (x, y, 1 - c)
        chips = _other_chips(x, y)
        halves = [b.shape[0] // 2 for b in bigs]
        mine = [pl.ds(c * h, h) for h in halves]
        theirs = [pl.ds((1 - c) * h, h) for h in halves]
        rc = functools.partial(_rcopy, send_sems=send_sems, recv_sems=recv_sems)

        first = []
        for a in range(n):
            for j, (px, py) in enumerate(chips):
                first.append(rc(in_refs[a].at[mine[a]], out_refs[a].at[me, mine[a]], k=3 * a + j, to=(px, py, c)))
        for j, (px, py) in enumerate(chips):
            first.append(rc(small_ref, osmall_ref.at[me], k=6 * n + j, to=(px, py, c)))
        for cp in first:
            cp.start()
        passed = []
        for a in range(n):
            for j, (px, py) in enumerate(chips):
                landed = out_refs[a].at[2 * px + py, mine[a]]
                rc(landed, landed, k=3 * a + j, to=(px, py, c)).wait_recv()
                fw = rc(landed, landed, k=3 * n + 3 * a + j, to=sibling)
                fw.start()
                passed.append(fw)
        for a in range(n):
            for j, (px, py) in enumerate(chips):
                landed = out_refs[a].at[2 * px + py, theirs[a]]
                rc(landed, landed, k=3 * n + 3 * a + j, to=sibling).wait_recv()
        for j, (px, py) in enumerate(chips):
            rc(small_ref, osmall_ref.at[2 * px + py], k=6 * n + j, to=(px, py, c)).wait_recv()
        for cp in first + passed:
            cp.wait_send()

    outs = _pcall(body, in_specs=[ANY] * (n + 1), out_specs=[ANY] * (n + 1),
                  out_shape=[SDS((N_CHIPS,) + b.shape, b.dtype) for b in bigs]
                  + [SDS((N_CHIPS,) + small.shape, small.dtype)],
                  scratch_shapes=[pltpu.SemaphoreType.DMA((6 * n + 3,)), pltpu.SemaphoreType.DMA((6 * n + 3,))],
                  name="gather_weights")(*bigs, small)
    me = 2 * lax.axis_index("x") + lax.axis_index("y")
    return [lax.dynamic_update_index_in_dim(o, own, me, 0) for o, own in zip(outs, list(bigs) + [small])]


def _swap_halves(gs):
    n = len(gs)

    def body(*refs):
        send_sems, recv_sems = refs[2 * n:]
        x, y, c = _coords()
        cps = [_rcopy(refs[a].at[:, 1 - c], refs[n + a], send_sems, recv_sems, a, (x, y, 1 - c))
               for a in range(n)]
        for cp in cps:
            cp.start()
        for cp in cps:
            cp.wait()

    return _pcall(body, in_specs=[ANY] * n, out_specs=[ANY] * n,
                  out_shape=[SDS((N_CHIPS,) + g.shape[2:], g.dtype) for g in gs],
                  scratch_shapes=[pltpu.SemaphoreType.DMA((n,)), pltpu.SemaphoreType.DMA((n,))],
                  name="swap_halves")(*gs)


def _sum_cores(gs, rs, half_idx):
    n = len(gs)
    ns = 2
    in_specs, out_specs, out_shape = [], [], []
    for g in gs:
        _, _, h, w = g.shape
        in_specs.append(pl.BlockSpec((1, 1, h // ns, w), lambda b, i, c_ref: (b, c_ref[0], i, 0)))
    for g in gs:
        _, _, h, w = g.shape
        spec = pl.BlockSpec((1, h // ns, w), lambda b, i, c_ref: (b, i, 0))
        in_specs.append(spec)
        out_specs += [spec, spec]
        out_shape += [SDS((N_CHIPS, h, w), F32), SDS((N_CHIPS, h, w), BF16)]

    def body(c_ref, *refs):
        del c_ref
        for a in range(n):
            tot = refs[a][0] + refs[n + a][...]
            refs[2 * n + 2 * a][...] = tot
            refs[2 * n + 2 * a + 1][...] = tot.astype(BF16)

    outs = _pcall(body, grid_spec=pltpu.PrefetchScalarGridSpec(
        num_scalar_prefetch=1, grid=(N_CHIPS, ns), in_specs=in_specs, out_specs=out_specs),
        out_shape=out_shape, compiler_params=_cp(("arbitrary", "arbitrary")), name="sum_cores")(
            half_idx, *gs, *rs)
    return outs[0::2], outs[1::2]


def _scatter_chips(hs):
    n = len(hs)

    def body(*refs):
        send_sems, recv_sems = refs[2 * n:]
        x, y, c = _coords()
        cps = []
        for a in range(n):
            for j, (px, py) in enumerate(_other_chips(x, y)):
                cps.append(_rcopy(refs[a].at[2 * px + py], refs[n + a].at[j], send_sems, recv_sems,
                                  3 * a + j, (px, py, c)))
        for cp in cps:
            cp.start()
        for cp in cps:
            cp.wait()

    return _pcall(body, in_specs=[ANY] * n, out_specs=[ANY] * n,
                  out_shape=[SDS((3,) + h.shape[1:], h.dtype) for h in hs],
                  scratch_shapes=[pltpu.SemaphoreType.DMA((3 * n,)), pltpu.SemaphoreType.DMA((3 * n,))],
                  name="scatter_chips")(*hs)


def _sum_chips(hs, xs, chip_idx):
    n = len(hs)
    ns = 2
    in_specs, out_specs, out_shape = [], [], []
    for h_arr in hs:
        _, h, w = h_arr.shape
        in_specs.append(pl.BlockSpec((1, h // ns, w), lambda i, c_ref: (c_ref[0], i, 0)))
    for h_arr in hs:
        _, h, w = h_arr.shape
        in_specs.append(pl.BlockSpec((3, h // ns, w), lambda i, c_ref: (0, i, 0)))
        out_specs.append(pl.BlockSpec((h // ns, w), lambda i, c_ref: (i, 0)))
        out_shape.append(SDS((h, w), F32))

    def body(c_ref, *refs):
        del c_ref
        for a in range(n):
            x_ref = refs[n + a]
            refs[2 * n + a][...] = (refs[a][0] + x_ref[0].astype(F32) + x_ref[1].astype(F32)
                                    + x_ref[2].astype(F32))

    return _pcall(body, grid_spec=pltpu.PrefetchScalarGridSpec(
        num_scalar_prefetch=1, grid=(ns,), in_specs=in_specs, out_specs=out_specs),
        out_shape=out_shape, compiler_params=_cp(("arbitrary",)), name="sum_chips")(chip_idx, *hs, *xs)


def _swap_totals(tots):
    n = len(tots)

    def body(*refs):
        send_sems, recv_sems = refs[2 * n:]
        x, y, c = _coords()
        cps = [_rcopy(refs[a], refs[n + a], send_sems, recv_sems, a, (x, y, 1 - c)) for a in range(n)]
        for cp in cps:
            cp.start()
        for cp in cps:
            cp.wait()

    return _pcall(body, in_specs=[ANY] * n, out_specs=[ANY] * n,
                  out_shape=[SDS(t.shape, t.dtype) for t in tots],
                  scratch_shapes=[pltpu.SemaphoreType.DMA((n,)), pltpu.SemaphoreType.DMA((n,))],
                  name="swap_totals")(*tots)


def _allreduce_small(buf):
    rows = buf.shape[0]

    def body(b_ref, o_ref, g_ref, send_sems, recv_sems):
        x, y, c = _coords()
        me = 4 * x + 2 * y + c
        g_ref[me] = b_ref[...]
        cps = []
        for k in range(1, 8):
            px = 1 - x if k & 4 else x
            py = 1 - y if k & 2 else y
            pc = 1 - c if k & 1 else c
            cps.append(pltpu.make_async_remote_copy(
                src_ref=b_ref, dst_ref=g_ref.at[me], send_sem=send_sems.at[k - 1],
                recv_sem=recv_sems.at[k - 1], device_id=(px, py, pc), device_id_type=MESH))
        for cp in cps:
            cp.start()
        for cp in cps:
            cp.wait()
        acc = g_ref[0]
        for d in range(1, 8):
            acc = acc + g_ref[d]
        o_ref[...] = acc

    vm = pl.BlockSpec(memory_space=pltpu.VMEM)
    return _pcall(body, in_specs=[vm], out_specs=vm, out_shape=SDS(buf.shape, F32),
                  scratch_shapes=[pltpu.VMEM((8, rows, 128), F32), pltpu.SemaphoreType.DMA((7,)),
                                  pltpu.SemaphoreType.DMA((7,))],
                  compiler_params=pltpu.CompilerParams(vmem_limit_bytes=VMEM_LIMIT),
                  name="allreduce_small")(buf)


def _pack_flat(parts, mult):
    rows, offs, r0 = [], [], 0
    for p in parts:
        f = p.reshape(-1)
        f = jnp.pad(f, (0, (-f.shape[0]) % (mult * 128))).reshape(-1, 128)
        rows.append(f)
        offs.append((r0, p.shape))
        r0 += f.shape[0]
    return jnp.concatenate(rows, axis=0), offs


def _unpack_flat(buf, offs):
    out = []
    for r0, shape in offs:
        n = 1
        for s in shape:
            n *= s
        nr = -(-n // 128)
        out.append(buf[r0:r0 + nr].reshape(-1)[:n].reshape(shape))
    return out


def _gates_to_rows(a, heads, chunk, cpb):
    t = a.shape[0]
    return a.reshape(t // (chunk * cpb), cpb, chunk, heads).transpose(0, 3, 1, 2)


def _rows_to_gates(a):
    nb, heads, cpb, chunk = a.shape
    return a.transpose(0, 2, 3, 1).reshape(nb * cpb * chunk, heads)


def kernel(x, norm_w, gdn_w_in, gdn_conv_w, gdn_a_log, gdn_dt_bias, gdn_norm_w, gdn_w_out, ssd_w_in, ssd_conv_w, ssd_conv_b, ssd_dt_bias, ssd_a_log, ssd_d, ssd_norm_w, ssd_w_out, final_norm_w, loss_target, m_norm_w, m_gdn_w_in, m_gdn_conv_w, m_gdn_a_log, m_gdn_dt_bias, m_gdn_norm_w, m_gdn_w_out, m_ssd_w_in, m_ssd_conv_w, m_ssd_conv_b, m_ssd_dt_bias, m_ssd_a_log, m_ssd_d, m_ssd_norm_w, m_ssd_w_out, m_final_norm_w, v_norm_w, v_gdn_w_in, v_gdn_conv_w, v_gdn_a_log, v_gdn_dt_bias, v_gdn_norm_w, v_gdn_w_out, v_ssd_w_in, v_ssd_conv_w, v_ssd_conv_b, v_ssd_dt_bias, v_ssd_a_log, v_ssd_d, v_ssd_norm_w, v_ssd_w_out, v_final_norm_w):
    ws = dict(norm_w=norm_w, gdn_w_in=gdn_w_in, gdn_conv_w=gdn_conv_w, gdn_a_log=gdn_a_log,
              gdn_dt_bias=gdn_dt_bias, gdn_norm_w=gdn_norm_w, gdn_w_out=gdn_w_out, ssd_w_in=ssd_w_in,
              ssd_conv_w=ssd_conv_w, ssd_conv_b=ssd_conv_b, ssd_dt_bias=ssd_dt_bias,
              ssd_a_log=ssd_a_log, ssd_d=ssd_d, ssd_norm_w=ssd_norm_w, ssd_w_out=ssd_w_out,
              final_norm_w=final_norm_w)
    ms = dict(norm_w=m_norm_w, gdn_w_in=m_gdn_w_in, gdn_conv_w=m_gdn_conv_w, gdn_a_log=m_gdn_a_log,
              gdn_dt_bias=m_gdn_dt_bias, gdn_norm_w=m_gdn_norm_w, gdn_w_out=m_gdn_w_out,
              ssd_w_in=m_ssd_w_in, ssd_conv_w=m_ssd_conv_w, ssd_conv_b=m_ssd_conv_b,
              ssd_dt_bias=m_ssd_dt_bias, ssd_a_log=m_ssd_a_log, ssd_d=m_ssd_d,
              ssd_norm_w=m_ssd_norm_w, ssd_w_out=m_ssd_w_out, final_norm_w=m_final_norm_w)
    vs = dict(norm_w=v_norm_w, gdn_w_in=v_gdn_w_in, gdn_conv_w=v_gdn_conv_w, gdn_a_log=v_gdn_a_log,
              gdn_dt_bias=v_gdn_dt_bias, gdn_norm_w=v_gdn_norm_w, gdn_w_out=v_gdn_w_out,
              ssd_w_in=v_ssd_w_in, ssd_conv_w=v_ssd_conv_w, ssd_conv_b=v_ssd_conv_b,
              ssd_dt_bias=v_ssd_dt_bias, ssd_a_log=v_ssd_a_log, ssd_d=v_ssd_d,
              ssd_norm_w=v_ssd_norm_w, ssd_w_out=v_ssd_w_out, final_norm_w=v_final_norm_w)
    names = list(ws)
    cx, cy, cc = _coords()
    chip = 2 * cx + cy
    t = x.shape[1]
    x0 = x.reshape(t, D)
    tgt = loss_target.reshape(t, D)

    bigs = [p[0].astype(BF16) for p in (gdn_w_in, gdn_w_out, ssd_w_in, ssd_w_out)]
    small, small_offs = _pack_flat([gdn_conv_w[0], ssd_conv_w[0], ssd_conv_b[0], ssd_norm_w[0]], 8)
    a_gi, a_go, a_si, a_so, gsmall = _gather_weights(bigs, small)
    w_gi = jnp.concatenate([a_gi[b] for b in range(4)], axis=1)
    w_si = jnp.concatenate([a_si[b] for b in range(4)], axis=1)
    w_go = a_go.reshape(4 * OUT_SHARD, D)
    w_so = a_so.reshape(4 * OUT_SHARD, D)
    sm = [_unpack_flat(gsmall[b], small_offs) for b in range(4)]
    g_cw = jnp.concatenate([sm[b][0] for b in range(4)], axis=1)
    s_cw = jnp.concatenate([sm[b][1] for b in range(4)], axis=1)
    s_cb = jnp.concatenate([sm[b][2] for b in range(4)], axis=0)[None]
    s_nw = jnp.concatenate([sm[b][3] for b in range(4)], axis=0)[None]

    def pad_small(w):
        return jnp.concatenate([w, jnp.zeros((D, SMALL_W - w.shape[1]), w.dtype)], axis=1)

    wg_main, wg_small = w_gi[:, :G_MAIN], pad_small(w_gi[:, G_MAIN:])
    ws_main, ws_small = w_si[:, :S_MAIN], pad_small(w_si[:, S_MAIN:])
    zero_b = jnp.zeros((1, G_CONV), F32)
    nw0, nw1 = norm_w[0:1], norm_w[1:2]
    fw = final_norm_w[None]
    g_alog = gdn_a_log.reshape(GH, 1, 1)
    g_dtb = gdn_dt_bias.reshape(GH, 1, 1)
    g_nw = gdn_norm_w.reshape(1, GDK)
    s_dtb = ssd_dt_bias.reshape(SG, SR).T.reshape(SR, SG, 1, 1)
    s_alog = ssd_a_log.reshape(SG, SR).T.reshape(SR, SG, 1, 1)
    s_d = ssd_d.reshape(SG, SR).T.reshape(SR, SG, 1, 1)

    hid0 = _rms_fwd(x0, nw0, "rms0")
    pg = _matmul(hid0, wg_main, "nn", "gdn_in_proj", tn=G_MAIN // 4)
    pg_small = _matmul(hid0, wg_small, "nn", "gdn_in_proj_small", tn=SMALL_W)
    post_q = _l2norm_scaled(GDK ** -0.5)
    post_k = _l2norm_scaled(1.0)
    q = _conv_fwd(pg, 0, g_cw[:, :G_QK], zero_b[:, :G_QK], GDK, post_q, "gdn_conv_q")
    k = _conv_fwd(pg, G_QK, g_cw[:, G_QK:2 * G_QK], zero_b[:, :G_QK], GDK, post_k, "gdn_conv_k")
    v = _conv_fwd(pg, 2 * G_QK, g_cw[:, 2 * G_QK:], zero_b[:, :G_V], GDK, _silu, "gdn_conv_v")
    braw = _gates_to_rows(pg_small[:, :GH], GH, GCH, GDN_CPB)
    araw = _gates_to_rows(pg_small[:, GH:2 * GH], GH, GCH, GDN_CPB)
    o, g_sall, g_tall = _gdn_fwd(q, k, v, braw, araw, g_alog, g_dtb)
    y0 = _gate_fwd(_gdn_gate, o, pg, G_CONV, g_nw, "gdn_gate")
    x1 = _matmul(y0, w_go, "nn", "gdn_out_proj", add=x0)

    hid1 = _rms_fwd(x1, nw1, "rms1")
    ps = _matmul(hid1, ws_main, "nn", "ssd_in_proj", tn=S_MAIN // 4)
    ps_small = _matmul(hid1, ws_small, "nn", "ssd_in_proj_small", tn=SMALL_W)
    c_x, c_b, c_c = S_INNER, 2 * S_INNER, 2 * S_INNER + SG * SN
    post_s = _silu
    xs = _conv_fwd(ps, c_x, s_cw[:, :S_INNER], s_cb[:, :S_INNER], SR * SP, post_s, "ssd_conv_x")
    bm = _conv_fwd(ps, c_b, s_cw[:, S_INNER:S_INNER + SG * SN], s_cb[:, S_INNER:S_INNER + SG * SN], SN,
                   post_s, "ssd_conv_b")
    cm = _conv_fwd(ps, c_c, s_cw[:, S_INNER + SG * SN:], s_cb[:, S_INNER + SG * SN:], SN, post_s,
                   "ssd_conv_c")
    def dt_rows(cpb):
        nb = t // (SCH * cpb)
        rows = _gates_to_rows(ps_small[:, :SH], SH, SCH, cpb)
        return rows.reshape(nb, SG, SR, cpb, SCH).transpose(0, 3, 2, 1, 4).reshape(nb, cpb * SR, SG, 1, SCH)

    nbs = t // (SCH * SSD_CPB_BWD)
    dtraw = dt_rows(SSD_CPB_BWD)
    yss, s_sall = _ssd_fwd(xs, bm, cm, dt_rows(SSD_CPB_FWD), s_dtb, s_alog, s_d)
    y1 = _gate_fwd(_ssd_gate, yss, ps, 0, s_nw, "ssd_gate")
    x2 = _matmul(y1, w_so, "nn", "ssd_out_proj", add=x1)

    dx2, d_fw, loss_row = _final_loss(x2, fw, tgt, "final_loss")

    dy1 = _matmul(dx2, w_so, "nt", "ssd_out_dx", out_dtype=BF16, tn=S_INNER)
    d_wso = _matmul(y1, dx2, "tn", "ssd_out_dw")
    dyss, dz_s, d_snw = _gate_bwd(_ssd_gate, yss, ps, 0, s_nw, dy1, True, "ssd_gate_bwd", Z_EXT)
    dxs, dbm, dcm, ddtraw, d_sdtb, d_salog, d_sd = _ssd_bwd(xs, bm, cm, dtraw, s_dtb, s_alog, s_d, s_sall, dyss)
    dps, dwx, dbx = _conv_dpre(ps, c_x, s_cw[:, :S_INNER], s_cb[:, :S_INNER], SR * SP, post_s, dxs, "ssd_dpre_x",
                               G_CONV, 0)
    dps, dwb, dbb = _conv_dpre(ps, c_b, s_cw[:, S_INNER:S_INNER + SG * SN], s_cb[:, S_INNER:S_INNER + SG * SN],
                               SN, post_s, dbm, "ssd_dpre_b", G_CONV, S_INNER, dps)
    dps, dwc, dbc = _conv_dpre(ps, c_c, s_cw[:, S_INNER + SG * SN:], s_cb[:, S_INNER + SG * SN:], SN, post_s,
                               dcm, "ssd_dpre_c", G_CONV, S_INNER + SG * SN, dps)
    d_scw = jnp.concatenate([dwx, dwb, dwc], axis=1)
    d_scb = jnp.concatenate([dbx, dbb, dbc], axis=1)
    dxbc = _conv_t(dps, s_cw, "ssd_conv_t")
    ddt = ddtraw.reshape(nbs, SSD_CPB_BWD, SR, SG, SCH).transpose(0, 3, 2, 1, 4).reshape(nbs, SH, SSD_CPB_BWD, SCH)
    ddt = _rows_to_gates(ddt)
    dsm_s = jnp.concatenate([ddt, jnp.zeros((t, SMALL_W - SH), F32)], axis=1).astype(BF16)
    dz_s = lax.dynamic_update_slice(dz_s, dsm_s, (0, S_INNER))
    ws_zx = jnp.concatenate([ws_main[:, :S_INNER], ws_small], axis=1)
    dhid1 = _matmul(dz_s, ws_zx, "nt", "ssd_in_dx_z", tk=Z_EXT)
    dhid1 = _matmul(dxbc, ws_main[:, S_INNER:], "nt", "ssd_in_dx_xbc", add=dhid1)
    dw_zx = _matmul(hid1, dz_s, "tn", "ssd_in_dw_z", tn=Z_EXT, tk=1024)
    d_wsi = jnp.concatenate([dw_zx[:, :S_INNER], _matmul(hid1, dxbc, "tn", "ssd_in_dw_xbc"),
                             dw_zx[:, S_INNER:S_INNER + SH]], axis=1)
    dx1, d_nw1 = _rms_bwd(x1, nw1, dx2, dhid1, "rms1_bwd")

    dy0 = _matmul(dx1, w_go, "nt", "gdn_out_dx", out_dtype=BF16, tn=G_V)
    d_wgo = _matmul(y0, dx1, "tn", "gdn_out_dw")
    do, dz_g, d_gnw = _gate_bwd(_gdn_gate, o, pg, G_CONV, g_nw, dy0, False, "gdn_gate_bwd", Z_EXT)
    dq, dk, dv, dbraw, daraw, d_galog, d_gdtb = _gdn_bwd(q, k, v, braw, araw, g_alog, g_dtb, g_sall, g_tall, do)
    dpg, dwq, _ = _conv_dpre(pg, 0, g_cw[:, :G_QK], zero_b[:, :G_QK], GDK, post_q, dq, "gdn_dpre_q", G_CONV, 0)
    dpg, dwk, _ = _conv_dpre(pg, G_QK, g_cw[:, G_QK:2 * G_QK], zero_b[:, :G_QK], GDK, post_k, dk, "gdn_dpre_k",
                             G_CONV, G_QK, dpg)
    dpg, dwv, _ = _conv_dpre(pg, 2 * G_QK, g_cw[:, 2 * G_QK:], zero_b[:, :G_V], GDK, _silu, dv, "gdn_dpre_v",
                             G_CONV, 2 * G_QK, dpg)
    d_gcw = jnp.concatenate([dwq, dwk, dwv], axis=1)
    dqkv = _conv_t(dpg, g_cw, "gdn_conv_t")
    dsm_g = jnp.concatenate([_rows_to_gates(dbraw), _rows_to_gates(daraw),
                             jnp.zeros((t, SMALL_W - 2 * GH), F32)], axis=1).astype(BF16)
    dz_g = lax.dynamic_update_slice(dz_g, dsm_g, (0, G_V))
    wg_zx = jnp.concatenate([wg_main[:, G_CONV:], wg_small], axis=1)
    dhid0 = _matmul(dqkv, wg_main[:, :G_CONV], "nt", "gdn_in_dx_qkv")
    dhid0 = _matmul(dz_g, wg_zx, "nt", "gdn_in_dx_z", add=dhid0, tk=Z_EXT)
    dw_zx = _matmul(hid0, dz_g, "tn", "gdn_in_dw_z", tn=Z_EXT, tk=1024)
    d_wgi = jnp.concatenate([_matmul(hid0, dqkv, "tn", "gdn_in_dw_qkv"), dw_zx[:, :G_V + 2 * GH]], axis=1)
    dx0, d_nw0 = _rms_bwd(x0, nw0, dx1, dhid0, "rms0_bwd")

    def in_blocks(dw):
        return dw.reshape(D, N_CHIPS, IN_SHARD).transpose(1, 0, 2).reshape(N_CHIPS, 2, D // 2, IN_SHARD)

    def out_blocks(dw):
        return dw.reshape(N_CHIPS, 2, OUT_SHARD // 2, D)

    gs = [in_blocks(d_wgi), out_blocks(d_wgo), in_blocks(d_wsi), out_blocks(d_wso)]
    from_sib = _swap_halves(gs)
    hsum, hsum_bf = _sum_cores(gs, from_sib, cc.astype(jnp.int32).reshape(1))
    recv = _scatter_chips(hsum_bf)
    tots = _sum_chips(hsum, recv, chip.astype(jnp.int32).reshape(1))
    sib_tots = _swap_totals(tots)
    full = [jnp.concatenate([jnp.where(cc == 0, mine, sib), jnp.where(cc == 0, sib, mine)], axis=0)
            for mine, sib in zip(tots, sib_tots)]
    grads = dict(
        gdn_w_in=full[0].reshape(1, D, IN_SHARD), gdn_w_out=full[1].reshape(1, OUT_SHARD, D),
        ssd_w_in=full[2].reshape(1, D, IN_SHARD), ssd_w_out=full[3].reshape(1, OUT_SHARD, D))

    small_parts = [loss_row, jnp.concatenate([d_nw0, d_nw1], axis=0), d_gcw, d_galog, d_gdtb, d_gnw, d_scw, d_scb,
                   d_sdtb, d_salog, d_sd, d_snw, d_fw]
    sbuf, soffs = _pack_flat(small_parts, 8)
    ssum = _unpack_flat(_allreduce_small(sbuf), soffs)
    (loss_s, g_nw_all, g_gcw, g_galog, g_gdtb, g_gnw, g_scw, g_scb, g_sdtb, g_salog, g_sd, g_snw, g_fw) = ssum

    def my_cols(a, width):
        return lax.dynamic_slice_in_dim(a, chip * width, width, axis=a.ndim - 1)

    grads.update(
        norm_w=g_nw_all, gdn_conv_w=my_cols(g_gcw, 1024)[None], gdn_a_log=g_galog.reshape(1, GH),
        gdn_dt_bias=g_gdtb.reshape(1, GH), gdn_norm_w=g_gnw.reshape(1, GDK),
        ssd_conv_w=my_cols(g_scw, 1024)[None], ssd_conv_b=my_cols(g_scb, 1024),
        ssd_dt_bias=g_sdtb.reshape(SR, SG).T.reshape(1, SH), ssd_a_log=g_salog.reshape(SR, SG).T.reshape(1, SH),
        ssd_d=g_sd.reshape(SR, SG).T.reshape(1, SH),
        ssd_norm_w=my_cols(g_snw, 512), final_norm_w=g_fw.reshape(D))
    loss = loss_s[0, 0]

    big_names = ("gdn_w_in", "gdn_w_out", "ssd_w_in", "ssd_w_out")
    deltas, new_m, new_v = {}, {}, {}
    for n in big_names:
        deltas[n], new_m[n], new_v[n] = _adamw(ws[n], grads[n], ms[n], vs[n], "adamw_" + n)
    rest = [n for n in names if n not in big_names]
    packs = [_pack_flat([d[n] for n in rest], 8) for d in (ws, grads, ms, vs)]
    outs = _adamw(*[p[0] for p in packs], "adamw_small")
    for d, buf in zip((deltas, new_m, new_v), outs):
        for n, a in zip(rest, _unpack_flat(buf, packs[0][1])):
            d[n] = a

    grad_x = dx0.reshape(1, t, D)
    return (loss, grad_x, *[grads[n] for n in names], *[deltas[n] for n in names],
            *[new_m[n] for n in names], *[new_v[n] for n in names])
```

```python
import functools

import jax
import jax.numpy as jnp
from jax import lax
from jax.experimental import pallas as pl
from jax.experimental.pallas import tpu as pltpu

F32 = jnp.float32
BF16 = jnp.bfloat16
SDS = jax.ShapeDtypeStruct
MESH = pl.DeviceIdType.MESH

D = 1024
EPS = 1e-6
CONV_K = 4
N_CHIPS = 4
GH = 16
GHQ = 8
GDK = 128
GCH = 64
G_QK = 1024
G_V = 2048
G_CONV = 4096
G_MAIN = 6144
G_IN = 6176
SH = 32
SP = 64
SN = 128
SG = 8
SR = 4
SCH = 128
S_INNER = 2048
S_MAIN = 6144
S_IN = 6176
IN_SHARD = 1544
OUT_SHARD = 512
SMALL_W = 128
Z_EXT = 2048 + SMALL_W

ADAM_LR = 0.001
ADAM_B1 = 0.9
ADAM_B2 = 0.999
ADAM_EPS = 1e-08
ADAM_WD = 0.01
ADAM_STEP = 10

VMEM_LIMIT = 56 * 1024 * 1024
BLOCK_ELEMS = 512 * 1024
PLAIN_BLOCK_ELEMS = 1024 * 1024
FWD_COLS = 512
NEG = -1e30


def _pcall(body, **kw):
    return pl.pallas_call(body, **kw)


def _cp(sem=None, vmem=VMEM_LIMIT):
    return pltpu.CompilerParams(dimension_semantics=sem, vmem_limit_bytes=vmem)


@jax.custom_jvp
def _sigmoid(x):
    return 1.0 / (1.0 + jnp.exp(-x))


@_sigmoid.defjvp
def _sigmoid_jvp(primals, tangents):
    s = _sigmoid(primals[0])
    return s, tangents[0] * (s * (1.0 - s))


@jax.custom_jvp
def _silu(x):
    return x * _sigmoid(x)


@_silu.defjvp
def _silu_jvp(primals, tangents):
    x = primals[0]
    s = _sigmoid(x)
    return x * s, tangents[0] * (s * (1.0 + x * (1.0 - s)))


def _softplus(x):
    return jnp.maximum(x, 0.0) + jnp.log(1.0 + jnp.exp(-jnp.abs(x)))


def _rms(x, w):
    return x * lax.rsqrt(jnp.mean(x * x, axis=-1, keepdims=True) + EPS) * w


_DIMS = {"nn": (((2,), (1,)), ((0,), (0,))),
         "nt": (((2,), (2,)), ((0,), (0,))),
         "tn": (((1,), (1,)), ((0,), (0,)))}


def _bdot(a, b, spec):
    return lax.dot_general(a.astype(BF16), b.astype(BF16), _DIMS[spec], preferred_element_type=F32)


@functools.partial(jax.custom_vjp, nondiff_argnums=(2,))
def _bmm(a, b, spec):
    return _bdot(a, b, spec)


def _bmm_fwd(a, b, spec):
    return _bdot(a, b, spec), (a, b)


def _bmm_bwd(spec, res, g):
    a, b = res
    if spec == "nn":
        return _bdot(g, b, "nt"), _bdot(a, g, "tn")
    if spec == "nt":
        return _bdot(g, b, "nn"), _bdot(g, a, "tn")
    return _bdot(b, g, "nt"), _bdot(a, g, "nn")


_bmm.defvjp(_bmm_fwd, _bmm_bwd)


@jax.custom_vjp
def _tri_inv(n):
    t = -n
    p = n
    steps = (n.shape[-1] - 1).bit_length() - 1
    r = lax.broadcasted_iota(jnp.int32, n.shape, 1)
    c = lax.broadcasted_iota(jnp.int32, n.shape, 2)
    t = t + jnp.where(r == c, 1.0, 0.0)
    for _ in range(steps):
        p = _bdot(p, p, "nn")
        t = t + _bdot(t, p, "nn")
    return t


def _tri_inv_fwd(n):
    t = _tri_inv(n)
    return t, t


def _tri_inv_bwd(t, g):
    return (-_bdot(_bdot(t, g, "tn"), t, "nt"),)


_tri_inv.defvjp(_tri_inv_fwd, _tri_inv_bwd)


@jax.custom_vjp
def _tri_inv_known(n, t):
    del n
    return t


def _tri_inv_known_fwd(n, t):
    del n
    return t, t


def _tri_inv_known_bwd(t, g):
    return _tri_inv_bwd(t, g)[0], jnp.zeros_like(t)


_tri_inv_known.defvjp(_tri_inv_known_fwd, _tri_inv_known_bwd)


def _masks(c, lead=1):
    r = lax.broadcasted_iota(jnp.int32, (lead, c, c), 1)
    s = lax.broadcasted_iota(jnp.int32, (lead, c, c), 2)
    return r >= s, r > s, r == s, r <= s


def _row_to_col(row, eye):
    return jnp.sum(jnp.where(eye, row, 0.0), axis=2, keepdims=True)


def _gdn_chunk(q, k, v, braw, araw, alog, dtb, s, t_known=None, want_t=False):
    h = v.shape[0]
    c = v.shape[1]
    rep = h // q.shape[0]
    tril, strict, eye, triu = _masks(c)
    qq = jnp.broadcast_to(q[:, None], (q.shape[0], rep) + q.shape[1:]).reshape(v.shape)
    kk = jnp.broadcast_to(k[:, None], (k.shape[0], rep) + k.shape[1:]).reshape(v.shape)
    beta_row = _sigmoid(braw)
    g_row = -jnp.exp(alog) * _softplus(araw + dtb)
    beta_col = _row_to_col(beta_row, eye)
    g_col = _row_to_col(g_row, eye)
    gc_col = jnp.sum(jnp.where(tril, g_row, 0.0), axis=2, keepdims=True)
    gc_row = jnp.sum(jnp.where(triu, g_col, 0.0), axis=1, keepdims=True)
    gc_last = jnp.sum(g_row, axis=2, keepdims=True)
    lmat = jnp.exp(jnp.where(tril, gc_col - gc_row, NEG))
    kb = kk * beta_col
    vb = v * beta_col
    n = jnp.where(strict, _bmm(kb, kk, "nt") * lmat, 0.0)
    t = _tri_inv(n) if t_known is None else _tri_inv_known(n, t_known)
    e_col = jnp.exp(gc_col)
    u = _bmm(t, vb, "nn")
    w = _bmm(t, kb * e_col, "nn")
    attn = _bmm(qq, kk, "nt") * lmat
    q_dec = qq * e_col
    k_dec = kk * jnp.exp(gc_last - gc_col)
    v_new = u - _bmm(w, s, "nn")
    o = _bmm(q_dec, s, "nn") + _bmm(attn, v_new, "nn")
    s_new = s * jnp.exp(gc_last) + _bmm(k_dec, v_new, "tn")
    return (o, s_new, t) if want_t else (o, s_new)


def _ssd_chunk(xs, bm, cm, dtraw, dtb, alog, dskip, s):
    c = xs.shape[1]
    tril, _, eye, triu = _masks(c)
    lane = lax.broadcasted_iota(jnp.int32, (1, 1, SR * SP), 2)
    prow = lax.broadcasted_iota(jnp.int32, (1, SR * SP, 1), 1)
    cb = _bmm(cm, bm, "nt")
    cs = _bmm(cm, s, "nt")

    def per_head(vals, idx):
        out = vals[SR - 1]
        for r in reversed(range(SR - 1)):
            out = jnp.where(idx < (r + 1) * SP, vals[r], out)
        return out

    dt_cols, e_cols, lmats, dstates, declast = [], [], [], [], []
    for r in range(SR):
        dt_row = _softplus(dtraw[r] + dtb[r])
        adt_row = -jnp.exp(alog[r]) * dt_row
        dt_cols.append(_row_to_col(dt_row, eye))
        adt_col = _row_to_col(adt_row, eye)
        acs_col = jnp.sum(jnp.where(tril, adt_row, 0.0), axis=2, keepdims=True)
        acs_row = jnp.sum(jnp.where(triu, adt_col, 0.0), axis=1, keepdims=True)
        acs_last = jnp.sum(adt_row, axis=2, keepdims=True)
        lmats.append(jnp.exp(jnp.where(tril, acs_col - acs_row, NEG)))
        e_cols.append(jnp.exp(acs_col))
        dstates.append(jnp.exp(acs_last - acs_col))
        declast.append(jnp.exp(acs_last))
    xd = xs * per_head(dt_cols, lane)
    y = per_head([_bmm(cb * lmats[r], xd, "nn") for r in range(SR)], lane)
    states = per_head([_bmm(xd, bm * dstates[r], "tn") for r in range(SR)], prow)
    y = y + cs * per_head(e_cols, lane) + xs * per_head([dskip[r] for r in range(SR)], lane)
    s_new = s * per_head(declast, prow) + states
    return y, s_new


def _matmul(a, b, mode, name, out_dtype=F32, add=None, tm=1024, tn=1024, tk=2048):
    if mode == "nn":
        (m, k), n = a.shape, b.shape[1]
    elif mode == "nt":
        (m, k), n = a.shape, b.shape[0]
    else:
        (k, m), n = a.shape, b.shape[1]
    tm, tn, tk = min(tm, m), min(tn, n), min(tk, k)
    assert m % tm == 0 and n % tn == 0 and k % tk == 0, (name, m, n, k)
    nk = k // tk
    dims = {"nn": (((1,), (0,)), ((), ())), "nt": (((1,), (1,)), ((), ())),
            "tn": (((0,), (0,)), ((), ()))}[mode]
    a_spec = {"nn": pl.BlockSpec((tm, tk), lambda i, j, kk: (i, kk)),
              "nt": pl.BlockSpec((tm, tk), lambda i, j, kk: (i, kk)),
              "tn": pl.BlockSpec((tk, tm), lambda i, j, kk: (kk, i))}[mode]
    b_spec = {"nn": pl.BlockSpec((tk, tn), lambda i, j, kk: (kk, j)),
              "nt": pl.BlockSpec((tn, tk), lambda i, j, kk: (j, kk)),
              "tn": pl.BlockSpec((tk, tn), lambda i, j, kk: (kk, j))}[mode]
    o_spec = pl.BlockSpec((tm, tn), lambda i, j, kk: (i, j))
    has_add = add is not None

    def body(*refs):
        a_ref, b_ref = refs[:2]
        add_ref = refs[2] if has_add else None
        o_ref = refs[2 + has_add]
        part = lax.dot_general(a_ref[...].astype(BF16), b_ref[...].astype(BF16), dims,
                               preferred_element_type=F32)

        def finish(r):
            if has_add:
                r = r + add_ref[...].astype(F32)
            o_ref[...] = r.astype(o_ref.dtype)

        if nk == 1:
            finish(part)
            return
        acc_ref = refs[3 + has_add]
        kk = pl.program_id(2)

        @pl.when(kk == 0)
        def _():
            acc_ref[...] = part

        @pl.when(jnp.logical_and(kk > 0, kk < nk - 1))
        def _():
            acc_ref[...] += part

        @pl.when(kk == nk - 1)
        def _():
            finish(acc_ref[...] + part)

    ins = [a, b] + ([add] if has_add else [])
    in_specs = [a_spec, b_spec] + ([o_spec] if has_add else [])
    scratch = [] if nk == 1 else [pltpu.VMEM((tm, tn), F32)]
    return _pcall(body, grid=(m // tm, n // tn, nk), in_specs=in_specs, out_specs=o_spec,
                  out_shape=SDS((m, n), out_dtype), scratch_shapes=scratch,
                  compiler_params=_cp(("parallel", "parallel", "arbitrary")), name=name)(*ins)


def _ew(fn, ins, in_specs, out_shape, out_specs, grid, name):
    n_in = len(ins)

    def body(*refs):
        outs = fn(*[r[...] for r in refs[:n_in]])
        for r, o in zip(refs[n_in:], outs):
            r[...] = o.astype(r.dtype)

    return _pcall(body, grid=grid, in_specs=in_specs, out_specs=out_specs, out_shape=out_shape,
                  compiler_params=_cp(("arbitrary",) * len(grid)), name=name)(*ins)


def _ew_vjp(fn, ins, in_specs, cts, ct_specs, wrt, g_shape, g_specs, acc, grid, name):
    n_in, n_ct = len(ins), len(cts)

    def body(*refs):
        vals = [r[...].astype(F32) for r in refs[:n_in]]
        outs, vjp = jax.vjp(fn, *vals)
        g_all = vjp(tuple(r[...].astype(F32) for r in refs[n_in:n_in + n_ct]))
        for pos, (i, g_ref) in enumerate(zip(wrt, refs[n_in + n_ct:])):
            g = g_all[i]
            if pos in acc:
                first = functools.reduce(
                    jnp.logical_and, [pl.program_id(ax) == 0 for ax in range(acc[pos], len(grid))])

                @pl.when(first)
                def _():
                    g_ref[...] = jnp.zeros_like(g_ref)

                g_ref[...] += g.astype(g_ref.dtype)
            else:
                g_ref[...] = g.astype(g_ref.dtype)

    return _pcall(body, grid=grid, in_specs=list(in_specs) + list(ct_specs), out_specs=g_specs,
                  out_shape=g_shape, compiler_params=_cp(("arbitrary",) * len(grid)),
                  name=name)(*ins, *cts)


def _row_spec(tm, n):
    return pl.BlockSpec((tm, n), lambda i: (i, 0))


def _par_spec(n):
    return pl.BlockSpec((1, n), lambda i: (0, 0))


def _rms_fwd(x, w, name):
    t = x.shape[0]
    tm = min(t, 512)
    return _ew(lambda xv, wv: (_rms(xv, wv),), [x, w], [_row_spec(tm, D), _par_spec(D)],
               [SDS((t, D), BF16)], [_row_spec(tm, D)], (t // tm,), name)[0]


def _rms_bwd(x, w, dres, dhid, name):
    t = x.shape[0]
    tm = min(t, 512)
    return _ew_vjp(lambda xv, wv: (xv, _rms(xv, wv)), [x, w], [_row_spec(tm, D), _par_spec(D)],
                   [dres, dhid], [_row_spec(tm, D), _row_spec(tm, D)], (0, 1),
                   [SDS((t, D), F32), SDS((1, D), F32)], [_row_spec(tm, D), _par_spec(D)],
                   {1: 0}, (t // tm,), name)


def _final_loss(x, w, tgt, name):
    t = x.shape[0]
    tm = min(t, 512)

    def body(x_ref, w_ref, t_ref, dx_ref, dw_ref, l_ref):
        @pl.when(pl.program_id(0) == 0)
        def _():
            dw_ref[...] = jnp.zeros_like(dw_ref)
            l_ref[...] = jnp.zeros_like(l_ref)

        xv, wv = x_ref[...], w_ref[...]
        rstd = lax.rsqrt(jnp.mean(xv * xv, axis=-1, keepdims=True) + EPS)
        xh = xv * rstd
        err = xh * wv - t_ref[...]
        l_ref[...] += 0.5 * jnp.sum(jnp.mean(err * err, axis=-1, keepdims=True), axis=0, keepdims=True)
        dy = err * (1.0 / D)
        dw_ref[...] += jnp.sum(dy * xh, axis=0, keepdims=True)
        dxh = dy * wv
        dx_ref[...] = rstd * (dxh - xh * jnp.mean(dxh * xh, axis=-1, keepdims=True))

    return _pcall(body, grid=(t // tm,), in_specs=[_row_spec(tm, D), _par_spec(D), _row_spec(tm, D)],
                  out_specs=[_row_spec(tm, D), _par_spec(D), _par_spec(128)],
                  out_shape=[SDS((t, D), F32), SDS((1, D), F32), SDS((1, 128), F32)],
                  compiler_params=_cp(("arbitrary",)), name=name)(x, w, tgt)


def _conv_taps(ext, w_ref, tm, lo):
    n = ext.shape[0]
    acc = None
    for j in range(CONV_K):
        shift = (CONV_K - 1 - j) if lo else (n - (CONV_K - 1 - j)) % n
        rolled = pltpu.roll(ext, shift, 0) if shift else ext
        term = w_ref[pl.ds(j, 1), :] * rolled[lo:lo + tm]
        acc = term if acc is None else acc + term
    return acc


def _conv_pre_specs(tm, ct, col0):
    hb = tm // 8
    return [pl.BlockSpec((tm, ct), lambda j, i: (i, col0 + j)),
            pl.BlockSpec((8, ct), lambda j, i: (jnp.maximum(i * hb - 1, 0), col0 + j)),
            pl.BlockSpec((CONV_K, ct), lambda j, i: (0, j)),
            pl.BlockSpec((1, ct), lambda j, i: (0, j))]


def _conv_pre_value(x_ref, xh_ref, w_ref, b_ref, tm):
    halo = jnp.where(pl.program_id(1) > 0, xh_ref[...], 0.0)
    ext = jnp.concatenate([halo, x_ref[...]], axis=0)
    return _conv_taps(ext, w_ref, tm, 8) + b_ref[...], ext


def _conv_fwd(x, col0, w, b, ct, post, name):
    t = x.shape[0]
    tpb = FWD_COLS // ct
    wide = tpb * ct
    tm = min(t, PLAIN_BLOCK_ELEMS // wide)
    nt = w.shape[1] // ct
    assert col0 % wide == 0 and nt % tpb == 0, name

    def body(x_ref, xh_ref, w_ref, b_ref, o_ref):
        pre, _ = _conv_pre_value(x_ref, xh_ref, w_ref, b_ref, tm)
        for k in range(tpb):
            o_ref[k] = post(pre[:, k * ct:(k + 1) * ct])

    return _pcall(body, grid=(nt // tpb, t // tm), in_specs=_conv_pre_specs(tm, wide, col0 // wide),
                  out_specs=pl.BlockSpec((tpb, tm, ct), lambda j, i: (j, i, 0)),
                  out_shape=SDS((nt, t, ct), F32), compiler_params=_cp(("arbitrary", "arbitrary")),
                  name=name)(x, x, w, b)


def _conv_dpre(x, col0, w, b, ct, post, dout, name, c_total, c_off, into=None):
    t = x.shape[0]
    tm = min(t, BLOCK_ELEMS // ct)
    nt = w.shape[1] // ct
    chained = into is not None

    def body(*refs):
        x_ref, xh_ref, w_ref, b_ref, do_ref = refs[:5]
        dp_ref, dw_ref, db_ref = refs[5 + chained:]
        pre, ext = _conv_pre_value(x_ref, xh_ref, w_ref, b_ref, tm)
        _, vjp = jax.vjp(post, pre)
        dpre = vjp(do_ref[0])[0]
        dp_ref[...] = dpre

        @pl.when(pl.program_id(1) == 0)
        def _():
            dw_ref[...] = jnp.zeros_like(dw_ref)
            db_ref[...] = jnp.zeros_like(db_ref)

        for j in range(CONV_K):
            xs = (pltpu.roll(ext, CONV_K - 1 - j, 0) if j < CONV_K - 1 else ext)[8:8 + tm]
            dw_ref[pl.ds(j, 1), :] += jnp.sum(dpre * xs, axis=0, keepdims=True)
        db_ref[...] += jnp.sum(dpre, axis=0, keepdims=True)

    c = w.shape[1]
    o0 = c_off // ct
    return _pcall(body, grid=(nt, t // tm),
                  in_specs=_conv_pre_specs(tm, ct, col0 // ct)
                  + [pl.BlockSpec((1, tm, ct), lambda j, i: (j, i, 0))] + ([ANY] if chained else []),
                  out_specs=[pl.BlockSpec((tm, ct), lambda j, i: (i, o0 + j)),
                             pl.BlockSpec((CONV_K, ct), lambda j, i: (0, j)),
                             pl.BlockSpec((1, ct), lambda j, i: (0, j))],
                  out_shape=[SDS((t, c_total), F32), SDS((CONV_K, c), F32), SDS((1, c), F32)],
                  input_output_aliases={5: 0} if chained else {},
                  compiler_params=_cp(("arbitrary", "arbitrary")), name=name)(
                      x, x, w, b, dout, *([into] if chained else []))


def _conv_t(dpre, w, name):
    t, c = dpre.shape
    ct = min(c, 512)
    tm = min(t, PLAIN_BLOCK_ELEMS // ct)
    hb = tm // 8
    last = t // tm - 1

    def body(d_ref, dh_ref, w_ref, o_ref):
        halo = jnp.where(pl.program_id(1) < last, dh_ref[...], 0.0)
        ext = jnp.concatenate([d_ref[...], halo], axis=0)
        o_ref[...] = _conv_taps(ext, w_ref, tm, 0).astype(o_ref.dtype)

    return _pcall(body, grid=(c // ct, t // tm),
                  in_specs=[pl.BlockSpec((tm, ct), lambda j, i: (i, j)),
                            pl.BlockSpec((8, ct), lambda j, i: (jnp.minimum((i + 1) * hb, t // 8 - 1), j)),
                            pl.BlockSpec((CONV_K, ct), lambda j, i: (0, j))],
                  out_specs=pl.BlockSpec((tm, ct), lambda j, i: (i, j)),
                  out_shape=SDS((t, c), BF16), compiler_params=_cp(("arbitrary", "arbitrary")),
                  name=name)(dpre, dpre, w)


def _l2norm_scaled(scale):
    def post(pre):
        a = _silu(pre)
        return a * lax.rsqrt(jnp.sum(a * a, axis=-1, keepdims=True) + EPS) * scale
    return post


GDN_HB = 16
GDN_CPB = 4
SSD_GB = 8
SSD_CPB_FWD = 2
SSD_CPB_BWD = 1


def _gdn_specs(nb, rev):
    hb, cpb, tc = GDN_HB, GDN_CPB, GDN_CPB * GCH
    blk = (lambda n: nb - 1 - n) if rev else (lambda n: n)
    seq = lambda h: pl.BlockSpec((h, tc, GDK), lambda g, n: (g, blk(n), 0))
    gate = pl.BlockSpec((1, hb, cpb, GCH), lambda g, n: (blk(n), g, 0, 0))
    par = pl.BlockSpec((hb, 1, 1), lambda g, n: (g, 0, 0))
    state = pl.BlockSpec((cpb, hb, GDK, GDK), lambda g, n: (blk(n), g, 0, 0))
    tinv = pl.BlockSpec((cpb, hb, GCH, GCH), lambda g, n: (blk(n), g, 0, 0))
    return seq, gate, par, state, tinv


def _gdn_fwd(q, k, v, braw, araw, alog, dtb):
    t = v.shape[1]
    hb, cpb = GDN_HB, GDN_CPB
    nb = t // (cpb * GCH)
    seq, gate, par, state, tinv = _gdn_specs(nb, False)

    def body(q_ref, k_ref, v_ref, b_ref, a_ref, al_ref, dt_ref, o_ref, sall_ref, tall_ref, s_ref):
        @pl.when(pl.program_id(1) == 0)
        def _():
            s_ref[...] = jnp.zeros_like(s_ref)

        s = s_ref[...]
        for c in range(cpb):
            rows = pl.ds(c * GCH, GCH)
            sall_ref[c] = s
            o, s, tmat = _gdn_chunk(q_ref[:, rows, :], k_ref[:, rows, :], v_ref[:, rows, :],
                                    b_ref[0, :, pl.ds(c, 1), :], a_ref[0, :, pl.ds(c, 1), :],
                                    al_ref[...], dt_ref[...], s, want_t=True)
            o_ref[:, rows, :] = o
            tall_ref[c] = tmat.astype(BF16)
        s_ref[...] = s

    return _pcall(body, grid=(GH // hb, nb),
                  in_specs=[seq(hb // 2), seq(hb // 2), seq(hb), gate, gate, par, par],
                  out_specs=[seq(hb), state, tinv],
                  out_shape=[SDS((GH, t, GDK), F32), SDS((t // GCH, GH, GDK, GDK), F32),
                             SDS((t // GCH, GH, GCH, GCH), BF16)],
                  scratch_shapes=[pltpu.VMEM((hb, GDK, GDK), F32)],
                  compiler_params=_cp(("arbitrary", "arbitrary")), name="gdn_chunk_fwd")(
                      q, k, v, braw, araw, alog, dtb)


def _gdn_bwd(q, k, v, braw, araw, alog, dtb, sall, tall, do):
    t = v.shape[1]
    hb, cpb = GDN_HB, GDN_CPB
    nb = t // (cpb * GCH)
    seq, gate, par, state, tinv = _gdn_specs(nb, True)

    def body(q_ref, k_ref, v_ref, b_ref, a_ref, al_ref, dt_ref, sall_ref, tall_ref, do_ref,
             dq_ref, dk_ref, dv_ref, db_ref, da_ref, dal_ref, ddt_ref, ds_ref):
        @pl.when(pl.program_id(1) == 0)
        def _():
            ds_ref[...] = jnp.zeros_like(ds_ref)
            dal_ref[...] = jnp.zeros_like(dal_ref)
            ddt_ref[...] = jnp.zeros_like(ddt_ref)

        ds = ds_ref[...]
        for c in reversed(range(cpb)):
            rows = pl.ds(c * GCH, GCH)
            fn = functools.partial(_gdn_chunk, t_known=tall_ref[c].astype(F32))
            _, vjp = jax.vjp(fn, q_ref[:, rows, :], k_ref[:, rows, :], v_ref[:, rows, :],
                             b_ref[0, :, pl.ds(c, 1), :], a_ref[0, :, pl.ds(c, 1), :],
                             al_ref[...], dt_ref[...], sall_ref[c])
            dq, dk, dv, db, da, dal, ddt, ds = vjp((do_ref[:, rows, :], ds))
            dq_ref[:, rows, :] = dq
            dk_ref[:, rows, :] = dk
            dv_ref[:, rows, :] = dv
            db_ref[0, :, pl.ds(c, 1), :] = db
            da_ref[0, :, pl.ds(c, 1), :] = da
            dal_ref[...] += dal
            ddt_ref[...] += ddt
        ds_ref[...] = ds

    return _pcall(body, grid=(GH // hb, nb),
                  in_specs=[seq(hb // 2), seq(hb // 2), seq(hb), gate, gate, par, par, state, tinv, seq(hb)],
                  out_specs=[seq(hb // 2), seq(hb // 2), seq(hb), gate, gate, par, par],
                  out_shape=[SDS(q.shape, F32), SDS(k.shape, F32), SDS(v.shape, F32),
                             SDS(braw.shape, F32), SDS(araw.shape, F32),
                             SDS((GH, 1, 1), F32), SDS((GH, 1, 1), F32)],
                  scratch_shapes=[pltpu.VMEM((hb, GDK, GDK), F32)],
                  compiler_params=_cp(("arbitrary", "arbitrary")), name="gdn_chunk_bwd")(
                      q, k, v, braw, araw, alog, dtb, sall, tall, do)


def _ssd_specs(nb, rev, cpb):
    gb, tc = SSD_GB, cpb * SCH
    blk = (lambda n: nb - 1 - n) if rev else (lambda n: n)
    seq = lambda w: pl.BlockSpec((gb, tc, w), lambda g, n: (g, blk(n), 0))
    gate = pl.BlockSpec((1, cpb * SR, gb, 1, SCH), lambda g, n: (blk(n), 0, g, 0, 0))
    par = pl.BlockSpec((SR, gb, 1, 1), lambda g, n: (0, g, 0, 0))
    state = pl.BlockSpec((cpb, gb, SR * SP, SN), lambda g, n: (blk(n), g, 0, 0))
    return seq, gate, par, state


def _ssd_fwd(xs, bm, cm, dtraw, dtb, alog, dskip):
    t = xs.shape[1]
    gb, cpb = SSD_GB, SSD_CPB_FWD
    nb = t // (cpb * SCH)
    seq, gate, par, state = _ssd_specs(nb, False, cpb)

    def body(x_ref, b_ref, c_ref, dt_ref, dtb_ref, al_ref, dk_ref, y_ref, sall_ref, s_ref):
        @pl.when(pl.program_id(1) == 0)
        def _():
            s_ref[...] = jnp.zeros_like(s_ref)

        s = s_ref[...]
        for c in range(cpb):
            rows = pl.ds(c * SCH, SCH)
            sall_ref[c] = s
            y, s = _ssd_chunk(x_ref[:, rows, :], b_ref[:, rows, :], c_ref[:, rows, :],
                              dt_ref[0, pl.ds(c * SR, SR)], dtb_ref[...], al_ref[...],
                              dk_ref[...], s)
            y_ref[:, rows, :] = y
        s_ref[...] = s

    return _pcall(body, grid=(SG // gb, nb),
                  in_specs=[seq(SR * SP), seq(SN), seq(SN), gate, par, par, par],
                  out_specs=[seq(SR * SP), state],
                  out_shape=[SDS((SG, t, SR * SP), F32), SDS((t // SCH, SG, SR * SP, SN), F32)],
                  scratch_shapes=[pltpu.VMEM((gb, SR * SP, SN), F32)],
                  compiler_params=_cp(("arbitrary", "arbitrary")), name="ssd_chunk_fwd")(
                      xs, bm, cm, dtraw, dtb, alog, dskip)


def _ssd_bwd(xs, bm, cm, dtraw, dtb, alog, dskip, sall, dy):
    t = xs.shape[1]
    gb, cpb = SSD_GB, SSD_CPB_BWD
    nb = t // (cpb * SCH)
    seq, gate, par, state = _ssd_specs(nb, True, cpb)

    def body(x_ref, b_ref, c_ref, dt_ref, dtb_ref, al_ref, dk_ref, sall_ref, dy_ref,
             dx_ref, dbm_ref, dcm_ref, ddt_ref, ddtb_ref, dal_ref, ddk_ref, ds_ref):
        @pl.when(pl.program_id(1) == 0)
        def _():
            ds_ref[...] = jnp.zeros_like(ds_ref)
            ddtb_ref[...] = jnp.zeros_like(ddtb_ref)
            dal_ref[...] = jnp.zeros_like(dal_ref)
            ddk_ref[...] = jnp.zeros_like(ddk_ref)

        ds = ds_ref[...]
        for c in reversed(range(cpb)):
            rows = pl.ds(c * SCH, SCH)
            _, vjp = jax.vjp(_ssd_chunk, x_ref[:, rows, :], b_ref[:, rows, :], c_ref[:, rows, :],
                             dt_ref[0, pl.ds(c * SR, SR)], dtb_ref[...], al_ref[...],
                             dk_ref[...], sall_ref[c])
            dx, dbm, dcm, ddt, ddtb, dal, ddk, ds = vjp((dy_ref[:, rows, :], ds))
            dx_ref[:, rows, :] = dx
            dbm_ref[:, rows, :] = dbm
            dcm_ref[:, rows, :] = dcm
            ddt_ref[0, pl.ds(c * SR, SR)] = ddt
            ddtb_ref[...] += ddtb
            dal_ref[...] += dal
            ddk_ref[...] += ddk
        ds_ref[...] = ds

    return _pcall(body, grid=(SG // gb, nb),
                  in_specs=[seq(SR * SP), seq(SN), seq(SN), gate, par, par, par, state, seq(SR * SP)],
                  out_specs=[seq(SR * SP), seq(SN), seq(SN), gate, par, par, par],
                  out_shape=[SDS(xs.shape, F32), SDS(bm.shape, F32), SDS(cm.shape, F32),
                             SDS(dtraw.shape, F32), SDS((SR, SG, 1, 1), F32), SDS((SR, SG, 1, 1), F32),
                             SDS((SR, SG, 1, 1), F32)],
                  scratch_shapes=[pltpu.VMEM((gb, SR * SP, SN), F32)],
                  compiler_params=_cp(("arbitrary", "arbitrary")), name="ssd_chunk_bwd")(
                      xs, bm, cm, dtraw, dtb, alog, dskip, sall, dy)


def _gate_specs(tm, ct, zcol0, per_tile_w):
    z0 = zcol0 // ct
    return [pl.BlockSpec((1, tm, ct), lambda i, j: (j, i, 0)),
            pl.BlockSpec((tm, ct), lambda i, j: (i, z0 + j)),
            pl.BlockSpec((1, ct), (lambda i, j: (0, j)) if per_tile_w else (lambda i, j: (0, 0)))]


def _gdn_gate(o, z, w):
    return (_rms(o[0], w) * _silu(z),)


def _ssd_gate(y, z, w):
    return (_rms(y[0] * _silu(z), w),)


def _gate_fwd(fn, o, proj, zcol0, w, name):
    nt, t, ct = o.shape
    tpb = FWD_COLS // ct
    wide = tpb * ct
    tm = min(t, PLAIN_BLOCK_ELEMS // wide)
    per_tile_w = w.shape[1] > ct
    assert zcol0 % wide == 0 and nt % tpb == 0, name
    z0 = zcol0 // wide
    specs = [pl.BlockSpec((tpb, tm, ct), lambda i, j: (j, i, 0)),
             pl.BlockSpec((tm, wide), lambda i, j: (i, z0 + j)),
             pl.BlockSpec((1, wide), lambda i, j: (0, j)) if per_tile_w
             else pl.BlockSpec((1, ct), lambda i, j: (0, 0))]

    def fn_wide(ov, zv, wv):
        outs = []
        for k in range(tpb):
            cols = slice(k * ct, (k + 1) * ct)
            outs.append(fn(ov[k:k + 1], zv[:, cols], wv[:, cols] if per_tile_w else wv)[0])
        return (jnp.concatenate(outs, axis=1),)

    return _ew(fn_wide, [o, proj, w], specs, [SDS((t, nt * ct), BF16)],
               [pl.BlockSpec((tm, wide), lambda i, j: (i, j))], (t // tm, nt // tpb), name)[0]


def _gate_bwd(fn, o, proj, zcol0, w, dy, wacc, name, dz_cols):
    nt, t, ct = o.shape
    tm = min(t, BLOCK_ELEMS // ct)
    specs = _gate_specs(tm, ct, zcol0, wacc)
    out_spec = pl.BlockSpec((tm, ct), lambda i, j: (i, j))
    if wacc:
        flip = lambda s: pl.BlockSpec(s.block_shape, lambda j, i, f=s.index_map: f(i, j))
        specs = [flip(s) for s in specs]
        out_spec = flip(out_spec)
        grid, acc = (nt, t // tm), {2: 1}
    else:
        grid, acc = (t // tm, nt), {2: 0}
    return _ew_vjp(fn, [o, proj, w], specs, [dy], [out_spec], (0, 1, 2),
                   [SDS(o.shape, F32), SDS((t, dz_cols), BF16), SDS(w.shape, F32)],
                   [specs[0], out_spec, specs[2]], acc, grid, name)


def _adamw_math(w, g, m, v):
    m = ADAM_B1 * m + (1.0 - ADAM_B1) * g
    v = ADAM_B2 * v + (1.0 - ADAM_B2) * jnp.square(g)
    m_hat = m / (1.0 - ADAM_B1 ** ADAM_STEP)
    v_hat = v / (1.0 - ADAM_B2 ** ADAM_STEP)
    delta = -ADAM_LR * (m_hat / (jnp.sqrt(v_hat) + ADAM_EPS) + ADAM_WD * w)
    return delta, m, v


def _adamw(w, g, m, v, name):
    shape = w.shape
    w2, g2, m2, v2 = [a.reshape(-1, shape[-1]) for a in (w, g, m, v)]
    r, c = w2.shape
    tr = 256 if r % 256 == 0 else r
    spec = pl.BlockSpec((tr, c), lambda i: (i, 0))
    outs = _ew(_adamw_math, [w2, g2, m2, v2], [spec] * 4, [SDS((r, c), F32)] * 3, [spec] * 3,
               (r // tr,), name)
    return [o.reshape(shape) for o in outs]


def _coords():
    return lax.axis_index("x"), lax.axis_index("y"), lax.axis_index("c")


def _other_chips(x, y):
    return [(1 - x, y), (x, 1 - y), (1 - x, 1 - y)]


ANY = pl.BlockSpec(memory_space=pl.ANY)


def _rcopy(src, dst, send_sems, recv_sems, k, to):
    return pltpu.make_async_remote_copy(src_ref=src, dst_ref=dst, send_sem=send_sems.at[k],
                                        recv_sem=recv_sems.at[k], device_id=to, device_id_type=MESH)


def _gather_weights(bigs, small):
    n = len(bigs)

    def body(*refs):
        in_refs, small_ref = refs[:n], refs[n]
        out_refs, osmall_ref = refs[n + 1:2 * n + 1], refs[2 * n + 1]
        send_sems, recv_sems = refs[2 * n + 2:]
        x, y, c = _coords()
        me = 2 * x + y
        sibling = (x, y, 1 - c)
        chips = _other_chips(x, y)
        halves = [b.shape[0] // 2 for b in bigs]
        mine = [pl.ds(c * h, h) for h in halves]
        theirs = [pl.ds((1 - c) * h, h) for h in halves]
        rc = functools.partial(_rcopy, send_sems=send_sems, recv_sems=recv_sems)

        first = []
        for a in range(n):
            for j, (px, py) in enumerate(chips):
                first.append(rc(in_refs[a].at[mine[a]], out_refs[a].at[me, mine[a]], k=3 * a + j, to=(px, py, c)))
        for j, (px, py) in enumerate(chips):
            first.append(rc(small_ref, osmall_ref.at[me], k=6 * n + j, to=(px, py, c)))
        for cp in first:
            cp.start()
        passed = []
        for a in range(n):
            for j, (px, py) in enumerate(chips):
                landed = out_refs[a].at[2 * px + py, mine[a]]
                rc(landed, landed, k=3 * a + j, to=(px, py, c)).wait_recv()
                fw = rc(landed, landed, k=3 * n + 3 * a + j, to=sibling)
                fw.start()
                passed.append(fw)
        for a in range(n):
            for j, (px, py) in enumerate(chips):
                landed = out_refs[a].at[2 * px + py, theirs[a]]
                rc(landed, landed, k=3 * n + 3 * a + j, to=sibling).wait_recv()
        for j, (px, py) in enumerate(chips):
            rc(small_ref, osmall_ref.at[2 * px + py], k=6 * n + j, to=(px, py, c)).wait_recv()
        for cp in first + passed:
            cp.wait_send()

    outs = _pcall(body, in_specs=[ANY] * (n + 1), out_specs=[ANY] * (n + 1),
                  out_shape=[SDS((N_CHIPS,) + b.shape, b.dtype) for b in bigs]
                  + [SDS((N_CHIPS,) + small.shape, small.dtype)],
                  scratch_shapes=[pltpu.SemaphoreType.DMA((6 * n + 3,)), pltpu.SemaphoreType.DMA((6 * n + 3,))],
                  name="gather_weights")(*bigs, small)
    me = 2 * lax.axis_index("x") + lax.axis_index("y")
    return [lax.dynamic_update_index_in_dim(o, own, me, 0) for o, own in zip(outs, list(bigs) + [small])]


def _swap_halves(gs):
    n = len(gs)

    def body(*refs):
        send_sems, recv_sems = refs[2 * n:]
        x, y, c = _coords()
        cps = [_rcopy(refs[a].at[:, 1 - c], refs[n + a], send_sems, recv_sems, a, (x, y, 1 - c))
               for a in range(n)]
        for cp in cps:
            cp.start()
        for cp in cps:
            cp.wait()

    return _pcall(body, in_specs=[ANY] * n, out_specs=[ANY] * n,
                  out_shape=[SDS((N_CHIPS,) + g.shape[2:], g.dtype) for g in gs],
                  scratch_shapes=[pltpu.SemaphoreType.DMA((n,)), pltpu.SemaphoreType.DMA((n,))],
                  name="swap_halves")(*gs)


def _sum_cores(gs, rs, half_idx):
    n = len(gs)
    ns = 2
    in_specs, out_specs, out_shape = [], [], []
    for g in gs:
        _, _, h, w = g.shape
        in_specs.append(pl.BlockSpec((1, 1, h // ns, w), lambda b, i, c_ref: (b, c_ref[0], i, 0)))
    for g in gs:
        _, _, h, w = g.shape
        spec = pl.BlockSpec((1, h // ns, w), lambda b, i, c_ref: (b, i, 0))
        in_specs.append(spec)
        out_specs += [spec, spec]
        out_shape += [SDS((N_CHIPS, h, w), F32), SDS((N_CHIPS, h, w), BF16)]

    def body(c_ref, *refs):
        del c_ref
        for a in range(n):
            tot = refs[a][0] + refs[n + a][...]
            refs[2 * n + 2 * a][...] = tot
            refs[2 * n + 2 * a + 1][...] = tot.astype(BF16)

    outs = _pcall(body, grid_spec=pltpu.PrefetchScalarGridSpec(
        num_scalar_prefetch=1, grid=(N_CHIPS, ns), in_specs=in_specs, out_specs=out_specs),
        out_shape=out_shape, compiler_params=_cp(("arbitrary", "arbitrary")), name="sum_cores")(
            half_idx, *gs, *rs)
    return outs[0::2], outs[1::2]


def _scatter_chips(hs):
    n = len(hs)

    def body(*refs):
        send_sems, recv_sems = refs[2 * n:]
        x, y, c = _coords()
        cps = []
        for a in range(n):
            for j, (px, py) in enumerate(_other_chips(x, y)):
                cps.append(_rcopy(refs[a].at[2 * px + py], refs[n + a].at[j], send_sems, recv_sems,
                                  3 * a + j, (px, py, c)))
        for cp in cps:
            cp.start()
        for cp in cps:
            cp.wait()

    return _pcall(body, in_specs=[ANY] * n, out_specs=[ANY] * n,
                  out_shape=[SDS((3,) + h.shape[1:], h.dtype) for h in hs],
                  scratch_shapes=[pltpu.SemaphoreType.DMA((3 * n,)), pltpu.SemaphoreType.DMA((3 * n,))],
                  name="scatter_chips")(*hs)


def _sum_chips(hs, xs, chip_idx):
    n = len(hs)
    ns = 2
    in_specs, out_specs, out_shape = [], [], []
    for h_arr in hs:
        _, h, w = h_arr.shape
        in_specs.append(pl.BlockSpec((1, h // ns, w), lambda i, c_ref: (c_ref[0], i, 0)))
    for h_arr in hs:
        _, h, w = h_arr.shape
        in_specs.append(pl.BlockSpec((3, h // ns, w), lambda i, c_ref: (0, i, 0)))
        out_specs.append(pl.BlockSpec((h // ns, w), lambda i, c_ref: (i, 0)))
        out_shape.append(SDS((h, w), F32))

    def body(c_ref, *refs):
        del c_ref
        for a in range(n):
            x_ref = refs[n + a]
            refs[2 * n + a][...] = (refs[a][0] + x_ref[0].astype(F32) + x_ref[1].astype(F32)
                                    + x_ref[2].astype(F32))

    return _pcall(body, grid_spec=pltpu.PrefetchScalarGridSpec(
        num_scalar_prefetch=1, grid=(ns,), in_specs=in_specs, out_specs=out_specs),
        out_shape=out_shape, compiler_params=_cp(("arbitrary",)), name="sum_chips")(chip_idx, *hs, *xs)


def _swap_totals(tots):
    n = len(tots)

    def body(*refs):
        send_sems, recv_sems = refs[2 * n:]
        x, y, c = _coords()
        cps = [_rcopy(refs[a], refs[n + a], send_sems, recv_sems, a, (x, y, 1 - c)) for a in range(n)]
        for cp in cps:
            cp.start()
        for cp in cps:
            cp.wait()

    return _pcall(body, in_specs=[ANY] * n, out_specs=[ANY] * n,
                  out_shape=[SDS(t.shape, t.dtype) for t in tots],
                  scratch_shapes=[pltpu.SemaphoreType.DMA((n,)), pltpu.SemaphoreType.DMA((n,))],
                  name="swap_totals")(*tots)


def _allreduce_small(buf):
    rows = buf.shape[0]

    def body(b_ref, o_ref, g_ref, send_sems, recv_sems):
        x, y, c = _coords()
        me = 4 * x + 2 * y + c
        g_ref[me] = b_ref[...]
        cps = []
        for k in range(1, 8):
            px = 1 - x if k & 4 else x
            py = 1 - y if k & 2 else y
            pc = 1 - c if k & 1 else c
            cps.append(pltpu.make_async_remote_copy(
                src_ref=b_ref, dst_ref=g_ref.at[me], send_sem=send_sems.at[k - 1],
                recv_sem=recv_sems.at[k - 1], device_id=(px, py, pc), device_id_type=MESH))
        for cp in cps:
            cp.start()
        for cp in cps:
            cp.wait()
        acc = g_ref[0]
        for d in range(1, 8):
            acc = acc + g_ref[d]
        o_ref[...] = acc

    vm = pl.BlockSpec(memory_space=pltpu.VMEM)
    return _pcall(body, in_specs=[vm], out_specs=vm, out_shape=SDS(buf.shape, F32),
                  scratch_shapes=[pltpu.VMEM((8, rows, 128), F32), pltpu.SemaphoreType.DMA((7,)),
                                  pltpu.SemaphoreType.DMA((7,))],
                  compiler_params=pltpu.CompilerParams(vmem_limit_bytes=VMEM_LIMIT),
                  name="allreduce_small")(buf)


def _pack_flat(parts, mult):
    rows, offs, r0 = [], [], 0
    for p in parts:
        f = p.reshape(-1)
        f = jnp.pad(f, (0, (-f.shape[0]) % (mult * 128))).reshape(-1, 128)
        rows.append(f)
        offs.append((r0, p.shape))
        r0 += f.shape[0]
    return jnp.concatenate(rows, axis=0), offs


def _unpack_flat(buf, offs):
    out = []
    for r0, shape in offs:
        n = 1
        for s in shape:
            n *= s
        nr = -(-n // 128)
        out.append(buf[r0:r0 + nr].reshape(-1)[:n].reshape(shape))
    return out


def _gates_to_rows(a, heads, chunk, cpb):
    t = a.shape[0]
    return a.reshape(t // (chunk * cpb), cpb, chunk, heads).transpose(0, 3, 1, 2)


def _rows_to_gates(a):
    nb, heads, cpb, chunk = a.shape
    return a.transpose(0, 2, 3, 1).reshape(nb * cpb * chunk, heads)


def kernel(x, norm_w, gdn_w_in, gdn_conv_w, gdn_a_log, gdn_dt_bias, gdn_norm_w, gdn_w_out, ssd_w_in, ssd_conv_w, ssd_conv_b, ssd_dt_bias, ssd_a_log, ssd_d, ssd_norm_w, ssd_w_out, final_norm_w, loss_target, m_norm_w, m_gdn_w_in, m_gdn_conv_w, m_gdn_a_log, m_gdn_dt_bias, m_gdn_norm_w, m_gdn_w_out, m_ssd_w_in, m_ssd_conv_w, m_ssd_conv_b, m_ssd_dt_bias, m_ssd_a_log, m_ssd_d, m_ssd_norm_w, m_ssd_w_out, m_final_norm_w, v_norm_w, v_gdn_w_in, v_gdn_conv_w, v_gdn_a_log, v_gdn_dt_bias, v_gdn_norm_w, v_gdn_w_out, v_ssd_w_in, v_ssd_conv_w, v_ssd_conv_b, v_ssd_dt_bias, v_ssd_a_log, v_ssd_d, v_ssd_norm_w, v_ssd_w_out, v_final_norm_w):
    ws = dict(norm_w=norm_w, gdn_w_in=gdn_w_in, gdn_conv_w=gdn_conv_w, gdn_a_log=gdn_a_log,
              gdn_dt_bias=gdn_dt_bias, gdn_norm_w=gdn_norm_w, gdn_w_out=gdn_w_out, ssd_w_in=ssd_w_in,
              ssd_conv_w=ssd_conv_w, ssd_conv_b=ssd_conv_b, ssd_dt_bias=ssd_dt_bias,
              ssd_a_log=ssd_a_log, ssd_d=ssd_d, ssd_norm_w=ssd_norm_w, ssd_w_out=ssd_w_out,
              final_norm_w=final_norm_w)
    ms = dict(norm_w=m_norm_w, gdn_w_in=m_gdn_w_in, gdn_conv_w=m_gdn_conv_w, gdn_a_log=m_gdn_a_log,
              gdn_dt_bias=m_gdn_dt_bias, gdn_norm_w=m_gdn_norm_w, gdn_w_out=m_gdn_w_out,
              ssd_w_in=m_ssd_w_in, ssd_conv_w=m_ssd_conv_w, ssd_conv_b=m_ssd_conv_b,
              ssd_dt_bias=m_ssd_dt_bias, ssd_a_log=m_ssd_a_log, ssd_d=m_ssd_d,
              ssd_norm_w=m_ssd_norm_w, ssd_w_out=m_ssd_w_out, final_norm_w=m_final_norm_w)
    vs = dict(norm_w=v_norm_w, gdn_w_in=v_gdn_w_in, gdn_conv_w=v_gdn_conv_w, gdn_a_log=v_gdn_a_log,
              gdn_dt_bias=v_gdn_dt_bias, gdn_norm_w=v_gdn_norm_w, gdn_w_out=v_gdn_w_out,
              ssd_w_in=v_ssd_w_in, ssd_conv_w=v_ssd_conv_w, ssd_conv_b=v_ssd_conv_b,
              ssd_dt_bias=v_ssd_dt_bias, ssd_a_log=v_ssd_a_log, ssd_d=v_ssd_d,
              ssd_norm_w=v_ssd_norm_w, ssd_w_out=v_ssd_w_out, final_norm_w=v_final_norm_w)
    names = list(ws)
    cx, cy, cc = _coords()
    chip = 2 * cx + cy
    t = x.shape[1]
    x0 = x.reshape(t, D)
    tgt = loss_target.reshape(t, D)

    bigs = [p[0].astype(BF16) for p in (gdn_w_in, gdn_w_out, ssd_w_in, ssd_w_out)]
    small, small_offs = _pack_flat([gdn_conv_w[0], ssd_conv_w[0], ssd_conv_b[0], ssd_norm_w[0]], 8)
    a_gi, a_go, a_si, a_so, gsmall = _gather_weights(bigs, small)
    w_gi = jnp.concatenate([a_gi[b] for b in range(4)], axis=1)
    w_si = jnp.concatenate([a_si[b] for b in range(4)], axis=1)
    w_go = a_go.reshape(4 * OUT_SHARD, D)
    w_so = a_so.reshape(4 * OUT_SHARD, D)
    sm = [_unpack_flat(gsmall[b], small_offs) for b in range(4)]
    g_cw = jnp.concatenate([sm[b][0] for b in range(4)], axis=1)
    s_cw = jnp.concatenate([sm[b][1] for b in range(4)], axis=1)
    s_cb = jnp.concatenate([sm[b][2] for b in range(4)], axis=0)[None]
    s_nw = jnp.concatenate([sm[b][3] for b in range(4)], axis=0)[None]

    def pad_small(w):
        return jnp.concatenate([w, jnp.zeros((D, SMALL_W - w.shape[1]), w.dtype)], axis=1)

    wg_main, wg_small = w_gi[:, :G_MAIN], pad_small(w_gi[:, G_MAIN:])
    ws_main, ws_small = w_si[:, :S_MAIN], pad_small(w_si[:, S_MAIN:])
    zero_b = jnp.zeros((1, G_CONV), F32)
    nw0, nw1 = norm_w[0:1], norm_w[1:2]
    fw = final_norm_w[None]
    g_alog = gdn_a_log.reshape(GH, 1, 1)
    g_dtb = gdn_dt_bias.reshape(GH, 1, 1)
    g_nw = gdn_norm_w.reshape(1, GDK)
    s_dtb = ssd_dt_bias.reshape(SG, SR).T.reshape(SR, SG, 1, 1)
    s_alog = ssd_a_log.reshape(SG, SR).T.reshape(SR, SG, 1, 1)
    s_d = ssd_d.reshape(SG, SR).T.reshape(SR, SG, 1, 1)

    hid0 = _rms_fwd(x0, nw0, "rms0")
    pg = _matmul(hid0, wg_main, "nn", "gdn_in_proj", tm=2048, tn=G_MAIN // 4)
    pg_small = _matmul(hid0, wg_small, "nn", "gdn_in_proj_small", tn=SMALL_W)
    post_q = _l2norm_scaled(GDK ** -0.5)
    post_k = _l2norm_scaled(1.0)
    q = _conv_fwd(pg, 0, g_cw[:, :G_QK], zero_b[:, :G_QK], GDK, post_q, "gdn_conv_q")
    k = _conv_fwd(pg, G_QK, g_cw[:, G_QK:2 * G_QK], zero_b[:, :G_QK], GDK, post_k, "gdn_conv_k")
    v = _conv_fwd(pg, 2 * G_QK, g_cw[:, 2 * G_QK:], zero_b[:, :G_V], GDK, _silu, "gdn_conv_v")
    braw = _gates_to_rows(pg_small[:, :GH], GH, GCH, GDN_CPB)
    araw = _gates_to_rows(pg_small[:, GH:2 * GH], GH, GCH, GDN_CPB)
    o, g_sall, g_tall = _gdn_fwd(q, k, v, braw, araw, g_alog, g_dtb)
    y0 = _gate_fwd(_gdn_gate, o, pg, G_CONV, g_nw, "gdn_gate")
    x1 = _matmul(y0, w_go, "nn", "gdn_out_proj", add=x0)

    hid1 = _rms_fwd(x1, nw1, "rms1")
    ps = _matmul(hid1, ws_main, "nn", "ssd_in_proj", tm=2048, tn=S_MAIN // 4)
    ps_small = _matmul(hid1, ws_small, "nn", "ssd_in_proj_small", tn=SMALL_W)
    c_x, c_b, c_c = S_INNER, 2 * S_INNER, 2 * S_INNER + SG * SN
    post_s = _silu
    xs = _conv_fwd(ps, c_x, s_cw[:, :S_INNER], s_cb[:, :S_INNER], SR * SP, post_s, "ssd_conv_x")
    bm = _conv_fwd(ps, c_b, s_cw[:, S_INNER:S_INNER + SG * SN], s_cb[:, S_INNER:S_INNER + SG * SN], SN,
                   post_s, "ssd_conv_b")
    cm = _conv_fwd(ps, c_c, s_cw[:, S_INNER + SG * SN:], s_cb[:, S_INNER + SG * SN:], SN, post_s,
                   "ssd_conv_c")
    def dt_rows(cpb):
        nb = t // (SCH * cpb)
        rows = _gates_to_rows(ps_small[:, :SH], SH, SCH, cpb)
        return rows.reshape(nb, SG, SR, cpb, SCH).transpose(0, 3, 2, 1, 4).reshape(nb, cpb * SR, SG, 1, SCH)

    nbs = t // (SCH * SSD_CPB_BWD)
    dtraw = dt_rows(SSD_CPB_BWD)
    yss, s_sall = _ssd_fwd(xs, bm, cm, dt_rows(SSD_CPB_FWD), s_dtb, s_alog, s_d)
    y1 = _gate_fwd(_ssd_gate, yss, ps, 0, s_nw, "ssd_gate")
    x2 = _matmul(y1, w_so, "nn", "ssd_out_proj", add=x1)

    dx2, d_fw, loss_row = _final_loss(x2, fw, tgt, "final_loss")

    dy1 = _matmul(dx2, w_so, "nt", "ssd_out_dx", out_dtype=BF16, tn=S_INNER)
    d_wso = _matmul(y1, dx2, "tn", "ssd_out_dw")
    dyss, dz_s, d_snw = _gate_bwd(_ssd_gate, yss, ps, 0, s_nw, dy1, True, "ssd_gate_bwd", Z_EXT)
    dxs, dbm, dcm, ddtraw, d_sdtb, d_salog, d_sd = _ssd_bwd(xs, bm, cm, dtraw, s_dtb, s_alog, s_d, s_sall, dyss)
    dps, dwx, dbx = _conv_dpre(ps, c_x, s_cw[:, :S_INNER], s_cb[:, :S_INNER], SR * SP, post_s, dxs, "ssd_dpre_x",
                               G_CONV, 0)
    dps, dwb, dbb = _conv_dpre(ps, c_b, s_cw[:, S_INNER:S_INNER + SG * SN], s_cb[:, S_INNER:S_INNER + SG * SN],
                               SN, post_s, dbm, "ssd_dpre_b", G_CONV, S_INNER, dps)
    dps, dwc, dbc = _conv_dpre(ps, c_c, s_cw[:, S_INNER + SG * SN:], s_cb[:, S_INNER + SG * SN:], SN, post_s,
                               dcm, "ssd_dpre_c", G_CONV, S_INNER + SG * SN, dps)
    d_scw = jnp.concatenate([dwx, dwb, dwc], axis=1)
    d_scb = jnp.concatenate([dbx, dbb, dbc], axis=1)
    dxbc = _conv_t(dps, s_cw, "ssd_conv_t")
    ddt = ddtraw.reshape(nbs, SSD_CPB_BWD, SR, SG, SCH).transpose(0, 3, 2, 1, 4).reshape(nbs, SH, SSD_CPB_BWD, SCH)
    ddt = _rows_to_gates(ddt)
    dsm_s = jnp.concatenate([ddt, jnp.zeros((t, SMALL_W - SH), F32)], axis=1).astype(BF16)
    dz_s = lax.dynamic_update_slice(dz_s, dsm_s, (0, S_INNER))
    ws_zx = jnp.concatenate([ws_main[:, :S_INNER], ws_small], axis=1)
    dhid1 = _matmul(dz_s, ws_zx, "nt", "ssd_in_dx_z", tk=Z_EXT)
    dhid1 = _matmul(dxbc, ws_main[:, S_INNER:], "nt", "ssd_in_dx_xbc", add=dhid1, tk=G_CONV)
    dw_zx = _matmul(hid1, dz_s, "tn", "ssd_in_dw_z", tn=Z_EXT, tk=1024)
    d_wsi = jnp.concatenate([dw_zx[:, :S_INNER], _matmul(hid1, dxbc, "tn", "ssd_in_dw_xbc"),
                             dw_zx[:, S_INNER:S_INNER + SH]], axis=1)
    dx1, d_nw1 = _rms_bwd(x1, nw1, dx2, dhid1, "rms1_bwd")

    dy0 = _matmul(dx1, w_go, "nt", "gdn_out_dx", out_dtype=BF16, tn=G_V)
    d_wgo = _matmul(y0, dx1, "tn", "gdn_out_dw")
    do, dz_g, d_gnw = _gate_bwd(_gdn_gate, o, pg, G_CONV, g_nw, dy0, False, "gdn_gate_bwd", Z_EXT)
    dq, dk, dv, dbraw, daraw, d_galog, d_gdtb = _gdn_bwd(q, k, v, braw, araw, g_alog, g_dtb, g_sall, g_tall, do)
    dpg, dwq, _ = _conv_dpre(pg, 0, g_cw[:, :G_QK], zero_b[:, :G_QK], GDK, post_q, dq, "gdn_dpre_q", G_CONV, 0)
    dpg, dwk, _ = _conv_dpre(pg, G_QK, g_cw[:, G_QK:2 * G_QK], zero_b[:, :G_QK], GDK, post_k, dk, "gdn_dpre_k",
                             G_CONV, G_QK, dpg)
    dpg, dwv, _ = _conv_dpre(pg, 2 * G_QK, g_cw[:, 2 * G_QK:], zero_b[:, :G_V], GDK, _silu, dv, "gdn_dpre_v",
                             G_CONV, 2 * G_QK, dpg)
    d_gcw = jnp.concatenate([dwq, dwk, dwv], axis=1)
    dqkv = _conv_t(dpg, g_cw, "gdn_conv_t")
    dsm_g = jnp.concatenate([_rows_to_gates(dbraw), _rows_to_gates(daraw),
                             jnp.zeros((t, SMALL_W - 2 * GH), F32)], axis=1).astype(BF16)
    dz_g = lax.dynamic_update_slice(dz_g, dsm_g, (0, G_V))
    wg_zx = jnp.concatenate([wg_main[:, G_CONV:], wg_small], axis=1)
    dhid0 = _matmul(dqkv, wg_main[:, :G_CONV], "nt", "gdn_in_dx_qkv", tk=G_CONV)
    dhid0 = _matmul(dz_g, wg_zx, "nt", "gdn_in_dx_z", add=dhid0, tk=Z_EXT)
    dw_zx = _matmul(hid0, dz_g, "tn", "gdn_in_dw_z", tn=Z_EXT, tk=1024)
    d_wgi = jnp.concatenate([_matmul(hid0, dqkv, "tn", "gdn_in_dw_qkv"), dw_zx[:, :G_V + 2 * GH]], axis=1)
    dx0, d_nw0 = _rms_bwd(x0, nw0, dx1, dhid0, "rms0_bwd")

    def in_blocks(dw):
        return dw.reshape(D, N_CHIPS, IN_SHARD).transpose(1, 0, 2).reshape(N_CHIPS, 2, D // 2, IN_SHARD)

    def out_blocks(dw):
        return dw.reshape(N_CHIPS, 2, OUT_SHARD // 2, D)

    gs = [in_blocks(d_wgi), out_blocks(d_wgo), in_blocks(d_wsi), out_blocks(d_wso)]
    from_sib = _swap_halves(gs)
    hsum, hsum_bf = _sum_cores(gs, from_sib, cc.astype(jnp.int32).reshape(1))
    recv = _scatter_chips(hsum_bf)
    tots = _sum_chips(hsum, recv, chip.astype(jnp.int32).reshape(1))
    sib_tots = _swap_totals(tots)
    full = [jnp.concatenate([jnp.where(cc == 0, mine, sib), jnp.where(cc == 0, sib, mine)], axis=0)
            for mine, sib in zip(tots, sib_tots)]
    grads = dict(
        gdn_w_in=full[0].reshape(1, D, IN_SHARD), gdn_w_out=full[1].reshape(1, OUT_SHARD, D),
        ssd_w_in=full[2].reshape(1, D, IN_SHARD), ssd_w_out=full[3].reshape(1, OUT_SHARD, D))

    small_parts = [loss_row, jnp.concatenate([d_nw0, d_nw1], axis=0), d_gcw, d_galog, d_gdtb, d_gnw, d_scw, d_scb,
                   d_sdtb, d_salog, d_sd, d_snw, d_fw]
    sbuf, soffs = _pack_flat(small_parts, 8)
    ssum = _unpack_flat(_allreduce_small(sbuf), soffs)
    (loss_s, g_nw_all, g_gcw, g_galog, g_gdtb, g_gnw, g_scw, g_scb, g_sdtb, g_salog, g_sd, g_snw, g_fw) = ssum

    def my_cols(a, width):
        return lax.dynamic_slice_in_dim(a, chip * width, width, axis=a.ndim - 1)

    grads.update(
        norm_w=g_nw_all, gdn_conv_w=my_cols(g_gcw, 1024)[None], gdn_a_log=g_galog.reshape(1, GH),
        gdn_dt_bias=g_gdtb.reshape(1, GH), gdn_norm_w=g_gnw.reshape(1, GDK),
        ssd_conv_w=my_cols(g_scw, 1024)[None], ssd_conv_b=my_cols(g_scb, 1024),
        ssd_dt_bias=g_sdtb.reshape(SR, SG).T.reshape(1, SH), ssd_a_log=g_salog.reshape(SR, SG).T.reshape(1, SH),
        ssd_d=g_sd.reshape(SR, SG).T.reshape(1, SH),
        ssd_norm_w=my_cols(g_snw, 512), final_norm_w=g_fw.reshape(D))
    loss = loss_s[0, 0]

    big_names = ("gdn_w_in", "gdn_w_out", "ssd_w_in", "ssd_w_out")
    deltas, new_m, new_v = {}, {}, {}
    for n in big_names:
        deltas[n], new_m[n], new_v[n] = _adamw(ws[n], grads[n], ms[n], vs[n], "adamw_" + n)
    rest = [n for n in names if n not in big_names]
    packs = [_pack_flat([d[n] for n in rest], 8) for d in (ws, grads, ms, vs)]
    outs = _adamw(*[p[0] for p in packs], "adamw_small")
    for d, buf in zip((deltas, new_m, new_v), outs):
        for n, a in zip(rest, _unpack_flat(buf, packs[0][1])):
            d[n] = a

    grad_x = dx0.reshape(1, t, D)
    return (loss, grad_x, *[grads[n] for n in names], *[deltas[n] for n in names],
            *[new_m[n] for n in names], *[new_v[n] for n in names])
```

```python
import functools

import jax
import jax.numpy as jnp
from jax import lax
from jax.experimental import pallas as pl
from jax.experimental.pallas import tpu as pltpu

F32 = jnp.float32
BF16 = jnp.bfloat16
SDS = jax.ShapeDtypeStruct
MESH = pl.DeviceIdType.MESH

D = 1024
EPS = 1e-6
CONV_K = 4
N_CHIPS = 4
GH = 16
GHQ = 8
GDK = 128
GCH = 64
G_QK = 1024
G_V = 2048
G_CONV = 4096
G_MAIN = 6144
G_IN = 6176
SH = 32
SP = 64
SN = 128
SG = 8
SR = 4
SCH = 128
S_INNER = 2048
S_MAIN = 6144
S_IN = 6176
IN_SHARD = 1544
OUT_SHARD = 512
SMALL_W = 128
Z_EXT = 2048 + SMALL_W

ADAM_LR = 0.001
ADAM_B1 = 0.9
ADAM_B2 = 0.999
ADAM_EPS = 1e-08
ADAM_WD = 0.01
ADAM_STEP = 10

VMEM_LIMIT = 56 * 1024 * 1024
BLOCK_ELEMS = 512 * 1024
PLAIN_BLOCK_ELEMS = 1024 * 1024
FWD_COLS = 512
NEG = -1e30


def _pcall(body, **kw):
    return pl.pallas_call(body, **kw)


def _cp(sem=None, vmem=VMEM_LIMIT):
    return pltpu.CompilerParams(dimension_semantics=sem, vmem_limit_bytes=vmem)


@jax.custom_jvp
def _sigmoid(x):
    return 1.0 / (1.0 + jnp.exp(-x))


@_sigmoid.defjvp
def _sigmoid_jvp(primals, tangents):
    s = _sigmoid(primals[0])
    return s, tangents[0] * (s * (1.0 - s))


@jax.custom_jvp
def _silu(x):
    return x * _sigmoid(x)


@_silu.defjvp
def _silu_jvp(primals, tangents):
    x = primals[0]
    s = _sigmoid(x)
    return x * s, tangents[0] * (s * (1.0 + x * (1.0 - s)))


def _softplus(x):
    return jnp.maximum(x, 0.0) + jnp.log(1.0 + jnp.exp(-jnp.abs(x)))


def _rms(x, w):
    return x * lax.rsqrt(jnp.mean(x * x, axis=-1, keepdims=True) + EPS) * w


_DIMS = {"nn": (((2,), (1,)), ((0,), (0,))),
         "nt": (((2,), (2,)), ((0,), (0,))),
         "tn": (((1,), (1,)), ((0,), (0,)))}


def _bdot(a, b, spec):
    return lax.dot_general(a.astype(BF16), b.astype(BF16), _DIMS[spec], preferred_element_type=F32)


@functools.partial(jax.custom_vjp, nondiff_argnums=(2,))
def _bmm(a, b, spec):
    return _bdot(a, b, spec)


def _bmm_fwd(a, b, spec):
    return _bdot(a, b, spec), (a, b)


def _bmm_bwd(spec, res, g):
    a, b = res
    if spec == "nn":
        return _bdot(g, b, "nt"), _bdot(a, g, "tn")
    if spec == "nt":
        return _bdot(g, b, "nn"), _bdot(g, a, "tn")
    return _bdot(b, g, "nt"), _bdot(a, g, "nn")


_bmm.defvjp(_bmm_fwd, _bmm_bwd)


@jax.custom_vjp
def _tri_inv(n):
    t = -n
    p = n
    steps = (n.shape[-1] - 1).bit_length() - 1
    r = lax.broadcasted_iota(jnp.int32, n.shape, 1)
    c = lax.broadcasted_iota(jnp.int32, n.shape, 2)
    t = t + jnp.where(r == c, 1.0, 0.0)
    for _ in range(steps):
        p = _bdot(p, p, "nn")
        t = t + _bdot(t, p, "nn")
    return t


def _tri_inv_fwd(n):
    t = _tri_inv(n)
    return t, t


def _tri_inv_bwd(t, g):
    return (-_bdot(_bdot(t, g, "tn"), t, "nt"),)


_tri_inv.defvjp(_tri_inv_fwd, _tri_inv_bwd)


@jax.custom_vjp
def _tri_inv_known(n, t):
    del n
    return t


def _tri_inv_known_fwd(n, t):
    del n
    return t, t


def _tri_inv_known_bwd(t, g):
    return _tri_inv_bwd(t, g)[0], jnp.zeros_like(t)


_tri_inv_known.defvjp(_tri_inv_known_fwd, _tri_inv_known_bwd)


def _masks(c, lead=1):
    r = lax.broadcasted_iota(jnp.int32, (lead, c, c), 1)
    s = lax.broadcasted_iota(jnp.int32, (lead, c, c), 2)
    return r >= s, r > s, r == s, r <= s


def _row_to_col(row, eye):
    return jnp.sum(jnp.where(eye, row, 0.0), axis=2, keepdims=True)


def _gdn_chunk(q, k, v, braw, araw, alog, dtb, s, t_known=None, want_t=False):
    h = v.shape[0]
    c = v.shape[1]
    rep = h // q.shape[0]
    tril, strict, eye, triu = _masks(c)
    qq = jnp.broadcast_to(q[:, None], (q.shape[0], rep) + q.shape[1:]).reshape(v.shape)
    kk = jnp.broadcast_to(k[:, None], (k.shape[0], rep) + k.shape[1:]).reshape(v.shape)
    beta_row = _sigmoid(braw)
    g_row = -jnp.exp(alog) * _softplus(araw + dtb)
    beta_col = _row_to_col(beta_row, eye)
    g_col = _row_to_col(g_row, eye)
    gc_col = jnp.sum(jnp.where(tril, g_row, 0.0), axis=2, keepdims=True)
    gc_row = jnp.sum(jnp.where(triu, g_col, 0.0), axis=1, keepdims=True)
    gc_last = jnp.sum(g_row, axis=2, keepdims=True)
    lmat = jnp.exp(jnp.where(tril, gc_col - gc_row, NEG))
    kb = kk * beta_col
    vb = v * beta_col
    n = jnp.where(strict, _bmm(kb, kk, "nt") * lmat, 0.0)
    t = _tri_inv(n) if t_known is None else _tri_inv_known(n, t_known)
    e_col = jnp.exp(gc_col)
    u = _bmm(t, vb, "nn")
    w = _bmm(t, kb * e_col, "nn")
    attn = _bmm(qq, kk, "nt") * lmat
    q_dec = qq * e_col
    k_dec = kk * jnp.exp(gc_last - gc_col)
    v_new = u - _bmm(w, s, "nn")
    o = _bmm(q_dec, s, "nn") + _bmm(attn, v_new, "nn")
    s_new = s * jnp.exp(gc_last) + _bmm(k_dec, v_new, "tn")
    return (o, s_new, t) if want_t else (o, s_new)


def _ssd_chunk(xs, bm, cm, dtraw, dtb, alog, dskip, s):
    c = xs.shape[1]
    tril, _, eye, triu = _masks(c)
    lane = lax.broadcasted_iota(jnp.int32, (1, 1, SR * SP), 2)
    prow = lax.broadcasted_iota(jnp.int32, (1, SR * SP, 1), 1)
    cb = _bmm(cm, bm, "nt")
    cs = _bmm(cm, s, "nt")

    def per_head(vals, idx):
        out = vals[SR - 1]
        for r in reversed(range(SR - 1)):
            out = jnp.where(idx < (r + 1) * SP, vals[r], out)
        return out

    dt_cols, e_cols, lmats, dstates, declast = [], [], [], [], []
    for r in range(SR):
        dt_row = _softplus(dtraw[r] + dtb[r])
        adt_row = -jnp.exp(alog[r]) * dt_row
        dt_cols.append(_row_to_col(dt_row, eye))
        adt_col = _row_to_col(adt_row, eye)
        acs_col = jnp.sum(jnp.where(tril, adt_row, 0.0), axis=2, keepdims=True)
        acs_row = jnp.sum(jnp.where(triu, adt_col, 0.0), axis=1, keepdims=True)
        acs_last = jnp.sum(adt_row, axis=2, keepdims=True)
        lmats.append(jnp.exp(jnp.where(tril, acs_col - acs_row, NEG)))
        e_cols.append(jnp.exp(acs_col))
        dstates.append(jnp.exp(acs_last - acs_col))
        declast.append(jnp.exp(acs_last))
    xd = xs * per_head(dt_cols, lane)
    y = per_head([_bmm(cb * lmats[r], xd, "nn") for r in range(SR)], lane)
    states = per_head([_bmm(xd, bm * dstates[r], "tn") for r in range(SR)], prow)
    y = y + cs * per_head(e_cols, lane) + xs * per_head([dskip[r] for r in range(SR)], lane)
    s_new = s * per_head(declast, prow) + states
    return y, s_new


def _matmul(a, b, mode, name, out_dtype=F32, add=None, tm=1024, tn=1024, tk=2048):
    if mode == "nn":
        (m, k), n = a.shape, b.shape[1]
    elif mode == "nt":
        (m, k), n = a.shape, b.shape[0]
    else:
        (k, m), n = a.shape, b.shape[1]
    tm, tn, tk = min(tm, m), min(tn, n), min(tk, k)
    assert m % tm == 0 and n % tn == 0 and k % tk == 0, (name, m, n, k)
    nk = k // tk
    dims = {"nn": (((1,), (0,)), ((), ())), "nt": (((1,), (1,)), ((), ())),
            "tn": (((0,), (0,)), ((), ()))}[mode]
    a_spec = {"nn": pl.BlockSpec((tm, tk), lambda i, j, kk: (i, kk)),
              "nt": pl.BlockSpec((tm, tk), lambda i, j, kk: (i, kk)),
              "tn": pl.BlockSpec((tk, tm), lambda i, j, kk: (kk, i))}[mode]
    b_spec = {"nn": pl.BlockSpec((tk, tn), lambda i, j, kk: (kk, j)),
              "nt": pl.BlockSpec((tn, tk), lambda i, j, kk: (j, kk)),
              "tn": pl.BlockSpec((tk, tn), lambda i, j, kk: (kk, j))}[mode]
    o_spec = pl.BlockSpec((tm, tn), lambda i, j, kk: (i, j))
    has_add = add is not None

    def body(*refs):
        a_ref, b_ref = refs[:2]
        add_ref = refs[2] if has_add else None
        o_ref = refs[2 + has_add]
        part = lax.dot_general(a_ref[...].astype(BF16), b_ref[...].astype(BF16), dims,
                               preferred_element_type=F32)

        def finish(r):
            if has_add:
                r = r + add_ref[...].astype(F32)
            o_ref[...] = r.astype(o_ref.dtype)

        if nk == 1:
            finish(part)
            return
        acc_ref = refs[3 + has_add]
        kk = pl.program_id(2)

        @pl.when(kk == 0)
        def _():
            acc_ref[...] = part

        @pl.when(jnp.logical_and(kk > 0, kk < nk - 1))
        def _():
            acc_ref[...] += part

        @pl.when(kk == nk - 1)
        def _():
            finish(acc_ref[...] + part)

    ins = [a, b] + ([add] if has_add else [])
    in_specs = [a_spec, b_spec] + ([o_spec] if has_add else [])
    scratch = [] if nk == 1 else [pltpu.VMEM((tm, tn), F32)]
    return _pcall(body, grid=(m // tm, n // tn, nk), in_specs=in_specs, out_specs=o_spec,
                  out_shape=SDS((m, n), out_dtype), scratch_shapes=scratch,
                  compiler_params=_cp(("parallel", "parallel", "arbitrary")), name=name)(*ins)


def _ew(fn, ins, in_specs, out_shape, out_specs, grid, name):
    n_in = len(ins)

    def body(*refs):
        outs = fn(*[r[...] for r in refs[:n_in]])
        for r, o in zip(refs[n_in:], outs):
            r[...] = o.astype(r.dtype)

    return _pcall(body, grid=grid, in_specs=in_specs, out_specs=out_specs, out_shape=out_shape,
                  compiler_params=_cp(("arbitrary",) * len(grid)), name=name)(*ins)


def _ew_vjp(fn, ins, in_specs, cts, ct_specs, wrt, g_shape, g_specs, acc, grid, name):
    n_in, n_ct = len(ins), len(cts)

    def body(*refs):
        vals = [r[...].astype(F32) for r in refs[:n_in]]
        outs, vjp = jax.vjp(fn, *vals)
        g_all = vjp(tuple(r[...].astype(F32) for r in refs[n_in:n_in + n_ct]))
        for pos, (i, g_ref) in enumerate(zip(wrt, refs[n_in + n_ct:])):
            g = g_all[i]
            if pos in acc:
                first = functools.reduce(
                    jnp.logical_and, [pl.program_id(ax) == 0 for ax in range(acc[pos], len(grid))])

                @pl.when(first)
                def _():
                    g_ref[...] = jnp.zeros_like(g_ref)

                g_ref[...] += g.astype(g_ref.dtype)
            else:
                g_ref[...] = g.astype(g_ref.dtype)

    return _pcall(body, grid=grid, in_specs=list(in_specs) + list(ct_specs), out_specs=g_specs,
                  out_shape=g_shape, compiler_params=_cp(("arbitrary",) * len(grid)),
                  name=name)(*ins, *cts)


def _row_spec(tm, n):
    return pl.BlockSpec((tm, n), lambda i: (i, 0))


def _par_spec(n):
    return pl.BlockSpec((1, n), lambda i: (0, 0))


def _rms_fwd(x, w, name):
    t = x.shape[0]
    tm = min(t, 512)
    return _ew(lambda xv, wv: (_rms(xv, wv),), [x, w], [_row_spec(tm, D), _par_spec(D)],
               [SDS((t, D), BF16)], [_row_spec(tm, D)], (t // tm,), name)[0]


def _rms_bwd(x, w, dres, dhid, name):
    t = x.shape[0]
    tm = min(t, 512)
    return _ew_vjp(lambda xv, wv: (xv, _rms(xv, wv)), [x, w], [_row_spec(tm, D), _par_spec(D)],
                   [dres, dhid], [_row_spec(tm, D), _row_spec(tm, D)], (0, 1),
                   [SDS((t, D), F32), SDS((1, D), F32)], [_row_spec(tm, D), _par_spec(D)],
                   {1: 0}, (t // tm,), name)


def _final_loss(x, w, tgt, name):
    t = x.shape[0]
    tm = min(t, 512)

    def body(x_ref, w_ref, t_ref, dx_ref, dw_ref, l_ref):
        @pl.when(pl.program_id(0) == 0)
        def _():
            dw_ref[...] = jnp.zeros_like(dw_ref)
            l_ref[...] = jnp.zeros_like(l_ref)

        xv, wv = x_ref[...], w_ref[...]
        rstd = lax.rsqrt(jnp.mean(xv * xv, axis=-1, keepdims=True) + EPS)
        xh = xv * rstd
        err = xh * wv - t_ref[...]
        l_ref[...] += 0.5 * jnp.sum(jnp.mean(err * err, axis=-1, keepdims=True), axis=0, keepdims=True)
        dy = err * (1.0 / D)
        dw_ref[...] += jnp.sum(dy * xh, axis=0, keepdims=True)
        dxh = dy * wv
        dx_ref[...] = rstd * (dxh - xh * jnp.mean(dxh * xh, axis=-1, keepdims=True))

    return _pcall(body, grid=(t // tm,), in_specs=[_row_spec(tm, D), _par_spec(D), _row_spec(tm, D)],
                  out_specs=[_row_spec(tm, D), _par_spec(D), _par_spec(128)],
                  out_shape=[SDS((t, D), F32), SDS((1, D), F32), SDS((1, 128), F32)],
                  compiler_params=_cp(("arbitrary",)), name=name)(x, w, tgt)


def _conv_taps(ext, w_ref, tm, lo):
    n = ext.shape[0]
    acc = None
    for j in range(CONV_K):
        shift = (CONV_K - 1 - j) if lo else (n - (CONV_K - 1 - j)) % n
        rolled = pltpu.roll(ext, shift, 0) if shift else ext
        term = w_ref[pl.ds(j, 1), :] * rolled[lo:lo + tm]
        acc = term if acc is None else acc + term
    return acc


def _conv_pre_specs(tm, ct, col0):
    hb = tm // 8
    return [pl.BlockSpec((tm, ct), lambda j, i: (i, col0 + j)),
            pl.BlockSpec((8, ct), lambda j, i: (jnp.maximum(i * hb - 1, 0), col0 + j)),
            pl.BlockSpec((CONV_K, ct), lambda j, i: (0, j)),
            pl.BlockSpec((1, ct), lambda j, i: (0, j))]


def _conv_pre_value(x_ref, xh_ref, w_ref, b_ref, tm):
    halo = jnp.where(pl.program_id(1) > 0, xh_ref[...], 0.0)
    ext = jnp.concatenate([halo, x_ref[...]], axis=0)
    return _conv_taps(ext, w_ref, tm, 8) + b_ref[...], ext


def _conv_fwd(x, col0, w, b, ct, post, name):
    t = x.shape[0]
    tpb = FWD_COLS // ct
    wide = tpb * ct
    tm = min(t, PLAIN_BLOCK_ELEMS // wide)
    nt = w.shape[1] // ct
    assert col0 % wide == 0 and nt % tpb == 0, name

    def body(x_ref, xh_ref, w_ref, b_ref, o_ref):
        pre, _ = _conv_pre_value(x_ref, xh_ref, w_ref, b_ref, tm)
        for k in range(tpb):
            o_ref[k] = post(pre[:, k * ct:(k + 1) * ct])

    return _pcall(body, grid=(nt // tpb, t // tm), in_specs=_conv_pre_specs(tm, wide, col0 // wide),
                  out_specs=pl.BlockSpec((tpb, tm, ct), lambda j, i: (j, i, 0)),
                  out_shape=SDS((nt, t, ct), F32), compiler_params=_cp(("arbitrary", "arbitrary")),
                  name=name)(x, x, w, b)


def _conv_dpre(x, col0, w, b, ct, post, dout, name, c_total, c_off, into=None):
    t = x.shape[0]
    tm = min(t, BLOCK_ELEMS // ct)
    nt = w.shape[1] // ct
    chained = into is not None

    def body(*refs):
        x_ref, xh_ref, w_ref, b_ref, do_ref = refs[:5]
        dp_ref, dw_ref, db_ref = refs[5 + chained:]
        pre, ext = _conv_pre_value(x_ref, xh_ref, w_ref, b_ref, tm)
        _, vjp = jax.vjp(post, pre)
        dpre = vjp(do_ref[0])[0]
        dp_ref[...] = dpre

        @pl.when(pl.program_id(1) == 0)
        def _():
            dw_ref[...] = jnp.zeros_like(dw_ref)
            db_ref[...] = jnp.zeros_like(db_ref)

        for j in range(CONV_K):
            xs = (pltpu.roll(ext, CONV_K - 1 - j, 0) if j < CONV_K - 1 else ext)[8:8 + tm]
            dw_ref[pl.ds(j, 1), :] += jnp.sum(dpre * xs, axis=0, keepdims=True)
        db_ref[...] += jnp.sum(dpre, axis=0, keepdims=True)

    c = w.shape[1]
    o0 = c_off // ct
    return _pcall(body, grid=(nt, t // tm),
                  in_specs=_conv_pre_specs(tm, ct, col0 // ct)
                  + [pl.BlockSpec((1, tm, ct), lambda j, i: (j, i, 0))] + ([ANY] if chained else []),
                  out_specs=[pl.BlockSpec((tm, ct), lambda j, i: (i, o0 + j)),
                             pl.BlockSpec((CONV_K, ct), lambda j, i: (0, j)),
                             pl.BlockSpec((1, ct), lambda j, i: (0, j))],
                  out_shape=[SDS((t, c_total), F32), SDS((CONV_K, c), F32), SDS((1, c), F32)],
                  input_output_aliases={5: 0} if chained else {},
                  compiler_params=_cp(("arbitrary", "arbitrary")), name=name)(
                      x, x, w, b, dout, *([into] if chained else []))


def _conv_t(dpre, w, name):
    t, c = dpre.shape
    ct = min(c, 512)
    tm = min(t, PLAIN_BLOCK_ELEMS // ct)
    hb = tm // 8
    last = t // tm - 1

    def body(d_ref, dh_ref, w_ref, o_ref):
        halo = jnp.where(pl.program_id(1) < last, dh_ref[...], 0.0)
        ext = jnp.concatenate([d_ref[...], halo], axis=0)
        o_ref[...] = _conv_taps(ext, w_ref, tm, 0).astype(o_ref.dtype)

    return _pcall(body, grid=(c // ct, t // tm),
                  in_specs=[pl.BlockSpec((tm, ct), lambda j, i: (i, j)),
                            pl.BlockSpec((8, ct), lambda j, i: (jnp.minimum((i + 1) * hb, t // 8 - 1), j)),
                            pl.BlockSpec((CONV_K, ct), lambda j, i: (0, j))],
                  out_specs=pl.BlockSpec((tm, ct), lambda j, i: (i, j)),
                  out_shape=SDS((t, c), BF16), compiler_params=_cp(("arbitrary", "arbitrary")),
                  name=name)(dpre, dpre, w)


def _l2norm_scaled(scale):
    def post(pre):
        a = _silu(pre)
        return a * lax.rsqrt(jnp.sum(a * a, axis=-1, keepdims=True) + EPS) * scale
    return post


GDN_HB = 16
GDN_CPB = 4
SSD_GB = 8
SSD_CPB_FWD = 2
SSD_CPB_BWD = 1


def _gdn_specs(nb, rev):
    hb, cpb, tc = GDN_HB, GDN_CPB, GDN_CPB * GCH
    blk = (lambda n: nb - 1 - n) if rev else (lambda n: n)
    seq = lambda h: pl.BlockSpec((h, tc, GDK), lambda g, n: (g, blk(n), 0))
    gate = pl.BlockSpec((1, hb, cpb, GCH), lambda g, n: (blk(n), g, 0, 0))
    par = pl.BlockSpec((hb, 1, 1), lambda g, n: (g, 0, 0))
    state = pl.BlockSpec((cpb, hb, GDK, GDK), lambda g, n: (blk(n), g, 0, 0))
    tinv = pl.BlockSpec((cpb, hb, GCH, GCH), lambda g, n: (blk(n), g, 0, 0))
    return seq, gate, par, state, tinv


def _gdn_fwd(q, k, v, braw, araw, alog, dtb):
    t = v.shape[1]
    hb, cpb = GDN_HB, GDN_CPB
    nb = t // (cpb * GCH)
    seq, gate, par, state, tinv = _gdn_specs(nb, False)

    def body(q_ref, k_ref, v_ref, b_ref, a_ref, al_ref, dt_ref, o_ref, sall_ref, tall_ref, s_ref):
        @pl.when(pl.program_id(1) == 0)
        def _():
            s_ref[...] = jnp.zeros_like(s_ref)

        s = s_ref[...]
        for c in range(cpb):
            rows = pl.ds(c * GCH, GCH)
            sall_ref[c] = s
            o, s, tmat = _gdn_chunk(q_ref[:, rows, :], k_ref[:, rows, :], v_ref[:, rows, :],
                                    b_ref[0, :, pl.ds(c, 1), :], a_ref[0, :, pl.ds(c, 1), :],
                                    al_ref[...], dt_ref[...], s, want_t=True)
            o_ref[:, rows, :] = o
            tall_ref[c] = tmat.astype(BF16)
        s_ref[...] = s

    return _pcall(body, grid=(GH // hb, nb),
                  in_specs=[seq(hb // 2), seq(hb // 2), seq(hb), gate, gate, par, par],
                  out_specs=[seq(hb), state, tinv],
                  out_shape=[SDS((GH, t, GDK), F32), SDS((t // GCH, GH, GDK, GDK), F32),
                             SDS((t // GCH, GH, GCH, GCH), BF16)],
                  scratch_shapes=[pltpu.VMEM((hb, GDK, GDK), F32)],
                  compiler_params=_cp(("arbitrary", "arbitrary")), name="gdn_chunk_fwd")(
                      q, k, v, braw, araw, alog, dtb)


def _gdn_bwd(q, k, v, braw, araw, alog, dtb, sall, tall, do):
    t = v.shape[1]
    hb, cpb = GDN_HB, GDN_CPB
    nb = t // (cpb * GCH)
    seq, gate, par, state, tinv = _gdn_specs(nb, True)

    def body(q_ref, k_ref, v_ref, b_ref, a_ref, al_ref, dt_ref, sall_ref, tall_ref, do_ref,
             dq_ref, dk_ref, dv_ref, db_ref, da_ref, dal_ref, ddt_ref, ds_ref):
        @pl.when(pl.program_id(1) == 0)
        def _():
            ds_ref[...] = jnp.zeros_like(ds_ref)
            dal_ref[...] = jnp.zeros_like(dal_ref)
            ddt_ref[...] = jnp.zeros_like(ddt_ref)

        ds = ds_ref[...]
        for c in reversed(range(cpb)):
            rows = pl.ds(c * GCH, GCH)
            fn = functools.partial(_gdn_chunk, t_known=tall_ref[c].astype(F32))
            _, vjp = jax.vjp(fn, q_ref[:, rows, :], k_ref[:, rows, :], v_ref[:, rows, :],
                             b_ref[0, :, pl.ds(c, 1), :], a_ref[0, :, pl.ds(c, 1), :],
                             al_ref[...], dt_ref[...], sall_ref[c])
            dq, dk, dv, db, da, dal, ddt, ds = vjp((do_ref[:, rows, :], ds))
            dq_ref[:, rows, :] = dq
            dk_ref[:, rows, :] = dk
            dv_ref[:, rows, :] = dv
            db_ref[0, :, pl.ds(c, 1), :] = db
            da_ref[0, :, pl.ds(c, 1), :] = da
            dal_ref[...] += dal
            ddt_ref[...] += ddt
        ds_ref[...] = ds

    return _pcall(body, grid=(GH // hb, nb),
                  in_specs=[seq(hb // 2), seq(hb // 2), seq(hb), gate, gate, par, par, state, tinv, seq(hb)],
                  out_specs=[seq(hb // 2), seq(hb // 2), seq(hb), gate, gate, par, par],
                  out_shape=[SDS(q.shape, F32), SDS(k.shape, F32), SDS(v.shape, F32),
                             SDS(braw.shape, F32), SDS(araw.shape, F32),
                             SDS((GH, 1, 1), F32), SDS((GH, 1, 1), F32)],
                  scratch_shapes=[pltpu.VMEM((hb, GDK, GDK), F32)],
                  compiler_params=_cp(("arbitrary", "arbitrary")), name="gdn_chunk_bwd")(
                      q, k, v, braw, araw, alog, dtb, sall, tall, do)


def _ssd_specs(nb, rev, cpb):
    gb, tc = SSD_GB, cpb * SCH
    blk = (lambda n: nb - 1 - n) if rev else (lambda n: n)
    seq = lambda w: pl.BlockSpec((gb, tc, w), lambda g, n: (g, blk(n), 0))
    gate = pl.BlockSpec((1, cpb * SR, gb, 1, SCH), lambda g, n: (blk(n), 0, g, 0, 0))
    par = pl.BlockSpec((SR, gb, 1, 1), lambda g, n: (0, g, 0, 0))
    state = pl.BlockSpec((cpb, gb, SR * SP, SN), lambda g, n: (blk(n), g, 0, 0))
    return seq, gate, par, state


def _ssd_fwd(xs, bm, cm, dtraw, dtb, alog, dskip):
    t = xs.shape[1]
    gb, cpb = SSD_GB, SSD_CPB_FWD
    nb = t // (cpb * SCH)
    seq, gate, par, state = _ssd_specs(nb, False, cpb)

    def body(x_ref, b_ref, c_ref, dt_ref, dtb_ref, al_ref, dk_ref, y_ref, sall_ref, s_ref):
        @pl.when(pl.program_id(1) == 0)
        def _():
            s_ref[...] = jnp.zeros_like(s_ref)

        s = s_ref[...]
        for c in range(cpb):
            rows = pl.ds(c * SCH, SCH)
            sall_ref[c] = s
            y, s = _ssd_chunk(x_ref[:, rows, :], b_ref[:, rows, :], c_ref[:, rows, :],
                              dt_ref[0, pl.ds(c * SR, SR)], dtb_ref[...], al_ref[...],
                              dk_ref[...], s)
            y_ref[:, rows, :] = y
        s_ref[...] = s

    return _pcall(body, grid=(SG // gb, nb),
                  in_specs=[seq(SR * SP), seq(SN), seq(SN), gate, par, par, par],
                  out_specs=[seq(SR * SP), state],
                  out_shape=[SDS((SG, t, SR * SP), F32), SDS((t // SCH, SG, SR * SP, SN), F32)],
                  scratch_shapes=[pltpu.VMEM((gb, SR * SP, SN), F32)],
                  compiler_params=_cp(("arbitrary", "arbitrary")), name="ssd_chunk_fwd")(
                      xs, bm, cm, dtraw, dtb, alog, dskip)


def _ssd_bwd(xs, bm, cm, dtraw, dtb, alog, dskip, sall, dy):
    t = xs.shape[1]
    gb, cpb = SSD_GB, SSD_CPB_BWD
    nb = t // (cpb * SCH)
    seq, gate, par, state = _ssd_specs(nb, True, cpb)

    def body(x_ref, b_ref, c_ref, dt_ref, dtb_ref, al_ref, dk_ref, sall_ref, dy_ref,
             dx_ref, dbm_ref, dcm_ref, ddt_ref, ddtb_ref, dal_ref, ddk_ref, ds_ref):
        @pl.when(pl.program_id(1) == 0)
        def _():
            ds_ref[...] = jnp.zeros_like(ds_ref)
            ddtb_ref[...] = jnp.zeros_like(ddtb_ref)
            dal_ref[...] = jnp.zeros_like(dal_ref)
            ddk_ref[...] = jnp.zeros_like(ddk_ref)

        ds = ds_ref[...]
        for c in reversed(range(cpb)):
            rows = pl.ds(c * SCH, SCH)
            _, vjp = jax.vjp(_ssd_chunk, x_ref[:, rows, :], b_ref[:, rows, :], c_ref[:, rows, :],
                             dt_ref[0, pl.ds(c * SR, SR)], dtb_ref[...], al_ref[...],
                             dk_ref[...], sall_ref[c])
            dx, dbm, dcm, ddt, ddtb, dal, ddk, ds = vjp((dy_ref[:, rows, :], ds))
            dx_ref[:, rows, :] = dx
            dbm_ref[:, rows, :] = dbm
            dcm_ref[:, rows, :] = dcm
            ddt_ref[0, pl.ds(c * SR, SR)] = ddt
            ddtb_ref[...] += ddtb
            dal_ref[...] += dal
            ddk_ref[...] += ddk
        ds_ref[...] = ds

    return _pcall(body, grid=(SG // gb, nb),
                  in_specs=[seq(SR * SP), seq(SN), seq(SN), gate, par, par, par, state, seq(SR * SP)],
                  out_specs=[seq(SR * SP), seq(SN), seq(SN), gate, par, par, par],
                  out_shape=[SDS(xs.shape, F32), SDS(bm.shape, F32), SDS(cm.shape, F32),
                             SDS(dtraw.shape, F32), SDS((SR, SG, 1, 1), F32), SDS((SR, SG, 1, 1), F32),
                             SDS((SR, SG, 1, 1), F32)],
                  scratch_shapes=[pltpu.VMEM((gb, SR * SP, SN), F32)],
                  compiler_params=_cp(("arbitrary", "arbitrary")), name="ssd_chunk_bwd")(
                      xs, bm, cm, dtraw, dtb, alog, dskip, sall, dy)


def _gate_specs(tm, ct, zcol0, per_tile_w):
    z0 = zcol0 // ct
    return [pl.BlockSpec((1, tm, ct), lambda i, j: (j, i, 0)),
            pl.BlockSpec((tm, ct), lambda i, j: (i, z0 + j)),
            pl.BlockSpec((1, ct), (lambda i, j: (0, j)) if per_tile_w else (lambda i, j: (0, 0)))]


def _gdn_gate(o, z, w):
    return (_rms(o[0], w) * _silu(z),)


def _ssd_gate(y, z, w):
    return (_rms(y[0] * _silu(z), w),)


def _gate_fwd(fn, o, proj, zcol0, w, name):
    nt, t, ct = o.shape
    tpb = FWD_COLS // ct
    wide = tpb * ct
    tm = min(t, PLAIN_BLOCK_ELEMS // wide)
    per_tile_w = w.shape[1] > ct
    assert zcol0 % wide == 0 and nt % tpb == 0, name
    z0 = zcol0 // wide
    specs = [pl.BlockSpec((tpb, tm, ct), lambda i, j: (j, i, 0)),
             pl.BlockSpec((tm, wide), lambda i, j: (i, z0 + j)),
             pl.BlockSpec((1, wide), lambda i, j: (0, j)) if per_tile_w
             else pl.BlockSpec((1, ct), lambda i, j: (0, 0))]

    def fn_wide(ov, zv, wv):
        outs = []
        for k in range(tpb):
            cols = slice(k * ct, (k + 1) * ct)
            outs.append(fn(ov[k:k + 1], zv[:, cols], wv[:, cols] if per_tile_w else wv)[0])
        return (jnp.concatenate(outs, axis=1),)

    return _ew(fn_wide, [o, proj, w], specs, [SDS((t, nt * ct), BF16)],
               [pl.BlockSpec((tm, wide), lambda i, j: (i, j))], (t // tm, nt // tpb), name)[0]


def _gate_bwd(fn, o, proj, zcol0, w, dy, wacc, name, dz_cols):
    nt, t, ct = o.shape
    tm = min(t, BLOCK_ELEMS // ct)
    specs = _gate_specs(tm, ct, zcol0, wacc)
    out_spec = pl.BlockSpec((tm, ct), lambda i, j: (i, j))
    if wacc:
        flip = lambda s: pl.BlockSpec(s.block_shape, lambda j, i, f=s.index_map: f(i, j))
        specs = [flip(s) for s in specs]
        out_spec = flip(out_spec)
        grid, acc = (nt, t // tm), {2: 1}
    else:
        grid, acc = (t // tm, nt), {2: 0}
    return _ew_vjp(fn, [o, proj, w], specs, [dy], [out_spec], (0, 1, 2),
                   [SDS(o.shape, F32), SDS((t, dz_cols), BF16), SDS(w.shape, F32)],
                   [specs[0], out_spec, specs[2]], acc, grid, name)


def _adamw_math(w, g, m, v):
    m = ADAM_B1 * m + (1.0 - ADAM_B1) * g
    v = ADAM_B2 * v + (1.0 - ADAM_B2) * jnp.square(g)
    m_hat = m / (1.0 - ADAM_B1 ** ADAM_STEP)
    v_hat = v / (1.0 - ADAM_B2 ** ADAM_STEP)
    delta = -ADAM_LR * (m_hat / (jnp.sqrt(v_hat) + ADAM_EPS) + ADAM_WD * w)
    return delta, m, v


def _adamw(w, g, m, v, name):
    shape = w.shape
    w2, g2, m2, v2 = [a.reshape(-1, shape[-1]) for a in (w, g, m, v)]
    r, c = w2.shape
    tr = 256 if r % 256 == 0 else r
    spec = pl.BlockSpec((tr, c), lambda i: (i, 0))
    outs = _ew(_adamw_math, [w2, g2, m2, v2], [spec] * 4, [SDS((r, c), F32)] * 3, [spec] * 3,
               (r // tr,), name)
    return [o.reshape(shape) for o in outs]


def _coords():
    return lax.axis_index("x"), lax.axis_index("y"), lax.axis_index("c")


def _other_chips(x, y):
    return [(1 - x, y), (x, 1 - y), (1 - x, 1 - y)]


ANY = pl.BlockSpec(memory_space=pl.ANY)


def _rcopy(src, dst, send_sems, recv_sems, k, to):
    return pltpu.make_async_remote_copy(src_ref=src, dst_ref=dst, send_sem=send_sems.at[k],
                                        recv_sem=recv_sems.at[k], device_id=to, device_id_type=MESH)


def _gather_weights(bigs, small):
    n = len(bigs)

    def body(*refs):
        in_refs, small_ref = refs[:n], refs[n]
        out_refs, osmall_ref = refs[n + 1:2 * n + 1], refs[2 * n + 1]
        send_sems, recv_sems = refs[2 * n + 2:]
        x, y, c = _coords()
        me = 2 * x + y
        sibling = (x, y, 1 - c)
        chips = _other_chips(x, y)
        halves = [b.shape[0] // 2 for b in bigs]
        mine = [pl.ds(c * h, h) for h in halves]
        theirs = [pl.ds((1 - c) * h, h) for h in halves]
        rc = functools.partial(_rcopy, send_sems=send_sems, recv_sems=recv_sems)

        first = []
        for a in range(n):
            for j, (px, py) in enumerate(chips):
                first.append(rc(in_refs[a].at[mine[a]], out_refs[a].at[me, mine[a]], k=3 * a + j, to=(px, py, c)))
        for j, (px, py) in enumerate(chips):
            first.append(rc(small_ref, osmall_ref.at[me], k=6 * n + j, to=(px, py, c)))
        for cp in first:
            cp.start()
        passed = []
        for a in range(n):
            for j, (px, py) in enumerate(chips):
                landed = out_refs[a].at[2 * px + py, mine[a]]
                rc(landed, landed, k=3 * a + j, to=(px, py, c)).wait_recv()
                fw = rc(landed, landed, k=3 * n + 3 * a + j, to=sibling)
                fw.start()
                passed.append(fw)
        for a in range(n):
            for j, (px, py) in enumerate(chips):
                landed = out_refs[a].at[2 * px + py, theirs[a]]
                rc(landed, landed, k=3 * n + 3 * a + j, to=sibling).wait_recv()
        for j, (px, py) in enumerate(chips):
            rc(small_ref, osmall_ref.at[2 * px + py], k=6 * n + j, to=(px, py, c)).wait_recv()
        for cp in first + passed:
            cp.wait_send()

    outs = _pcall(body, in_specs=[ANY] * (n + 1), out_specs=[ANY] * (n + 1),
                  out_shape=[SDS((N_CHIPS,) + b.shape, b.dtype) for b in bigs]
                  + [SDS((N_CHIPS,) + small.shape, small.dtype)],
                  scratch_shapes=[pltpu.SemaphoreType.DMA((6 * n + 3,)), pltpu.SemaphoreType.DMA((6 * n + 3,))],
                  name="gather_weights")(*bigs, small)
    me = 2 * lax.axis_index("x") + lax.axis_index("y")
    return [lax.dynamic_update_index_in_dim(o, own, me, 0) for o, own in zip(outs, list(bigs) + [small])]


def _swap_halves(gs):
    n = len(gs)

    def body(*refs):
        send_sems, recv_sems = refs[2 * n:]
        x, y, c = _coords()
        cps = [_rcopy(refs[a].at[:, 1 - c], refs[n + a], send_sems, recv_sems, a, (x, y, 1 - c))
               for a in range(n)]
        for cp in cps:
            cp.start()
        for cp in cps:
            cp.wait()

    return _pcall(body, in_specs=[ANY] * n, out_specs=[ANY] * n,
                  out_shape=[SDS((N_CHIPS,) + g.shape[2:], g.dtype) for g in gs],
                  scratch_shapes=[pltpu.SemaphoreType.DMA((n,)), pltpu.SemaphoreType.DMA((n,))],
                  name="swap_halves")(*gs)


def _sum_cores(gs, rs, half_idx):
    n = len(gs)
    ns = 2
    in_specs, out_specs, out_shape = [], [], []
    for g in gs:
        _, _, h, w = g.shape
        in_specs.append(pl.BlockSpec((1, 1, h // ns, w), lambda b, i, c_ref: (b, c_ref[0], i, 0)))
    for g in gs:
        _, _, h, w = g.shape
        spec = pl.BlockSpec((1, h // ns, w), lambda b, i, c_ref: (b, i, 0))
        in_specs.append(spec)
        out_specs += [spec, spec]
        out_shape += [SDS((N_CHIPS, h, w), F32), SDS((N_CHIPS, h, w), BF16)]

    def body(c_ref, *refs):
        del c_ref
        for a in range(n):
            tot = refs[a][0] + refs[n + a][...]
            refs[2 * n + 2 * a][...] = tot
            refs[2 * n + 2 * a + 1][...] = tot.astype(BF16)

    outs = _pcall(body, grid_spec=pltpu.PrefetchScalarGridSpec(
        num_scalar_prefetch=1, grid=(N_CHIPS, ns), in_specs=in_specs, out_specs=out_specs),
        out_shape=out_shape, compiler_params=_cp(("arbitrary", "arbitrary")), name="sum_cores")(
            half_idx, *gs, *rs)
    return outs[0::2], outs[1::2]


def _scatter_chips(hs):
    n = len(hs)

    def body(*refs):
        send_sems, recv_sems = refs[2 * n:]
        x, y, c = _coords()
        cps = []
        for a in range(n):
            for j, (px, py) in enumerate(_other_chips(x, y)):
                cps.append(_rcopy(refs[a].at[2 * px + py], refs[n + a].at[j], send_sems, recv_sems,
                                  3 * a + j, (px, py, c)))
        for cp in cps:
            cp.start()
        for cp in cps:
            cp.wait()

    return _pcall(body, in_specs=[ANY] * n, out_specs=[ANY] * n,
                  out_shape=[SDS((3,) + h.shape[1:], h.dtype) for h in hs],
                  scratch_shapes=[pltpu.SemaphoreType.DMA((3 * n,)), pltpu.SemaphoreType.DMA((3 * n,))],
                  name="scatter_chips")(*hs)


def _sum_chips(hs, xs, chip_idx):
    n = len(hs)
    ns = 2
    in_specs, out_specs, out_shape = [], [], []
    for h_arr in hs:
        _, h, w = h_arr.shape
        in_specs.append(pl.BlockSpec((1, h // ns, w), lambda i, c_ref: (c_ref[0], i, 0)))
    for h_arr in hs:
        _, h, w = h_arr.shape
        in_specs.append(pl.BlockSpec((3, h // ns, w), lambda i, c_ref: (0, i, 0)))
        out_specs.append(pl.BlockSpec((h // ns, w), lambda i, c_ref: (i, 0)))
        out_shape.append(SDS((h, w), F32))

    def body(c_ref, *refs):
        del c_ref
        for a in range(n):
            x_ref = refs[n + a]
            refs[2 * n + a][...] = (refs[a][0] + x_ref[0].astype(F32) + x_ref[1].astype(F32)
                                    + x_ref[2].astype(F32))

    return _pcall(body, grid_spec=pltpu.PrefetchScalarGridSpec(
        num_scalar_prefetch=1, grid=(ns,), in_specs=in_specs, out_specs=out_specs),
        out_shape=out_shape, compiler_params=_cp(("arbitrary",)), name="sum_chips")(chip_idx, *hs, *xs)


def _swap_totals(tots):
    n = len(tots)

    def body(*refs):
        send_sems, recv_sems = refs[2 * n:]
        x, y, c = _coords()
        cps = [_rcopy(refs[a], refs[n + a], send_sems, recv_sems, a, (x, y, 1 - c)) for a in range(n)]
        for cp in cps:
            cp.start()
        for cp in cps:
            cp.wait()

    return _pcall(body, in_specs=[ANY] * n, out_specs=[ANY] * n,
                  out_shape=[SDS(t.shape, t.dtype) for t in tots],
                  scratch_shapes=[pltpu.SemaphoreType.DMA((n,)), pltpu.SemaphoreType.DMA((n,))],
                  name="swap_totals")(*tots)


def _allreduce_small(buf):
    rows = buf.shape[0]

    def body(b_ref, o_ref, g_ref, send_sems, recv_sems):
        x, y, c = _coords()
        me = 4 * x + 2 * y + c
        g_ref[me] = b_ref[...]
        cps = []
        for k in range(1, 8):
            px = 1 - x if k & 4 else x
            py = 1 - y if k & 2 else y
            pc = 1 - c if k & 1 else c
            cps.append(pltpu.make_async_remote_copy(
                src_ref=b_ref, dst_ref=g_ref.at[me], send_sem=send_sems.at[k - 1],
                recv_sem=recv_sems.at[k - 1], device_id=(px, py, pc), device_id_type=MESH))
        for cp in cps:
            cp.start()
        for cp in cps:
            cp.wait()
        acc = g_ref[0]
        for d in range(1, 8):
            acc = acc + g_ref[d]
        o_ref[...] = acc

    vm = pl.BlockSpec(memory_space=pltpu.VMEM)
    return _pcall(body, in_specs=[vm], out_specs=vm, out_shape=SDS(buf.shape, F32),
                  scratch_shapes=[pltpu.VMEM((8, rows, 128), F32), pltpu.SemaphoreType.DMA((7,)),
                                  pltpu.SemaphoreType.DMA((7,))],
                  compiler_params=pltpu.CompilerParams(vmem_limit_bytes=VMEM_LIMIT),
                  name="allreduce_small")(buf)


def _pack_flat(parts, mult):
    rows, offs, r0 = [], [], 0
    for p in parts:
        f = p.reshape(-1)
        f = jnp.pad(f, (0, (-f.shape[0]) % (mult * 128))).reshape(-1, 128)
        rows.append(f)
        offs.append((r0, p.shape))
        r0 += f.shape[0]
    return jnp.concatenate(rows, axis=0), offs


def _unpack_flat(buf, offs):
    out = []
    for r0, shape in offs:
        n = 1
        for s in shape:
            n *= s
        nr = -(-n // 128)
        out.append(buf[r0:r0 + nr].reshape(-1)[:n].reshape(shape))
    return out


def _gates_to_rows(a, heads, chunk, cpb):
    t = a.shape[0]
    return a.reshape(t // (chunk * cpb), cpb, chunk, heads).transpose(0, 3, 1, 2)


def _rows_to_gates(a):
    nb, heads, cpb, chunk = a.shape
    return a.transpose(0, 2, 3, 1).reshape(nb * cpb * chunk, heads)


def kernel(x, norm_w, gdn_w_in, gdn_conv_w, gdn_a_log, gdn_dt_bias, gdn_norm_w, gdn_w_out, ssd_w_in, ssd_conv_w, ssd_conv_b, ssd_dt_bias, ssd_a_log, ssd_d, ssd_norm_w, ssd_w_out, final_norm_w, loss_target, m_norm_w, m_gdn_w_in, m_gdn_conv_w, m_gdn_a_log, m_gdn_dt_bias, m_gdn_norm_w, m_gdn_w_out, m_ssd_w_in, m_ssd_conv_w, m_ssd_conv_b, m_ssd_dt_bias, m_ssd_a_log, m_ssd_d, m_ssd_norm_w, m_ssd_w_out, m_final_norm_w, v_norm_w, v_gdn_w_in, v_gdn_conv_w, v_gdn_a_log, v_gdn_dt_bias, v_gdn_norm_w, v_gdn_w_out, v_ssd_w_in, v_ssd_conv_w, v_ssd_conv_b, v_ssd_dt_bias, v_ssd_a_log, v_ssd_d, v_ssd_norm_w, v_ssd_w_out, v_final_norm_w):
    ws = dict(norm_w=norm_w, gdn_w_in=gdn_w_in, gdn_conv_w=gdn_conv_w, gdn_a_log=gdn_a_log,
              gdn_dt_bias=gdn_dt_bias, gdn_norm_w=gdn_norm_w, gdn_w_out=gdn_w_out, ssd_w_in=ssd_w_in,
              ssd_conv_w=ssd_conv_w, ssd_conv_b=ssd_conv_b, ssd_dt_bias=ssd_dt_bias,
              ssd_a_log=ssd_a_log, ssd_d=ssd_d, ssd_norm_w=ssd_norm_w, ssd_w_out=ssd_w_out,
              final_norm_w=final_norm_w)
    ms = dict(norm_w=m_norm_w, gdn_w_in=m_gdn_w_in, gdn_conv_w=m_gdn_conv_w, gdn_a_log=m_gdn_a_log,
              gdn_dt_bias=m_gdn_dt_bias, gdn_norm_w=m_gdn_norm_w, gdn_w_out=m_gdn_w_out,
              ssd_w_in=m_ssd_w_in, ssd_conv_w=m_ssd_conv_w, ssd_conv_b=m_ssd_conv_b,
              ssd_dt_bias=m_ssd_dt_bias, ssd_a_log=m_ssd_a_log, ssd_d=m_ssd_d,
              ssd_norm_w=m_ssd_norm_w, ssd_w_out=m_ssd_w_out, final_norm_w=m_final_norm_w)
    vs = dict(norm_w=v_norm_w, gdn_w_in=v_gdn_w_in, gdn_conv_w=v_gdn_conv_w, gdn_a_log=v_gdn_a_log,
              gdn_dt_bias=v_gdn_dt_bias, gdn_norm_w=v_gdn_norm_w, gdn_w_out=v_gdn_w_out,
              ssd_w_in=v_ssd_w_in, ssd_conv_w=v_ssd_conv_w, ssd_conv_b=v_ssd_conv_b,
              ssd_dt_bias=v_ssd_dt_bias, ssd_a_log=v_ssd_a_log, ssd_d=v_ssd_d,
              ssd_norm_w=v_ssd_norm_w, ssd_w_out=v_ssd_w_out, final_norm_w=v_final_norm_w)
    names = list(ws)
    cx, cy, cc = _coords()
    chip = 2 * cx + cy
    t = x.shape[1]
    x0 = x.reshape(t, D)
    tgt = loss_target.reshape(t, D)

    bigs = [p[0].astype(BF16) for p in (gdn_w_in, gdn_w_out, ssd_w_in, ssd_w_out)]
    small, small_offs = _pack_flat([gdn_conv_w[0], ssd_conv_w[0], ssd_conv_b[0], ssd_norm_w[0]], 8)
    a_gi, a_go, a_si, a_so, gsmall = _gather_weights(bigs, small)
    w_gi = jnp.concatenate([a_gi[b] for b in range(4)], axis=1)
    w_si = jnp.concatenate([a_si[b] for b in range(4)], axis=1)
    w_go = a_go.reshape(4 * OUT_SHARD, D)
    w_so = a_so.reshape(4 * OUT_SHARD, D)
    sm = [_unpack_flat(gsmall[b], small_offs) for b in range(4)]
    g_cw = jnp.concatenate([sm[b][0] for b in range(4)], axis=1)
    s_cw = jnp.concatenate([sm[b][1] for b in range(4)], axis=1)
    s_cb = jnp.concatenate([sm[b][2] for b in range(4)], axis=0)[None]
    s_nw = jnp.concatenate([sm[b][3] for b in range(4)], axis=0)[None]

    def pad_small(w):
        return jnp.concatenate([w, jnp.zeros((D, SMALL_W - w.shape[1]), w.dtype)], axis=1)

    wg_main, wg_small = w_gi[:, :G_MAIN], pad_small(w_gi[:, G_MAIN:])
    ws_main, ws_small = w_si[:, :S_MAIN], pad_small(w_si[:, S_MAIN:])
    zero_b = jnp.zeros((1, G_CONV), F32)
    nw0, nw1 = norm_w[0:1], norm_w[1:2]
    fw = final_norm_w[None]
    g_alog = gdn_a_log.reshape(GH, 1, 1)
    g_dtb = gdn_dt_bias.reshape(GH, 1, 1)
    g_nw = gdn_norm_w.reshape(1, GDK)
    s_dtb = ssd_dt_bias.reshape(SG, SR).T.reshape(SR, SG, 1, 1)
    s_alog = ssd_a_log.reshape(SG, SR).T.reshape(SR, SG, 1, 1)
    s_d = ssd_d.reshape(SG, SR).T.reshape(SR, SG, 1, 1)

    hid0 = _rms_fwd(x0, nw0, "rms0")
    pg = _matmul(hid0, wg_main, "nn", "gdn_in_proj", tm=2048, tn=G_MAIN // 4)
    pg_small = _matmul(hid0, wg_small, "nn", "gdn_in_proj_small", tn=SMALL_W)
    post_q = _l2norm_scaled(GDK ** -0.5)
    post_k = _l2norm_scaled(1.0)
    q = _conv_fwd(pg, 0, g_cw[:, :G_QK], zero_b[:, :G_QK], GDK, post_q, "gdn_conv_q")
    k = _conv_fwd(pg, G_QK, g_cw[:, G_QK:2 * G_QK], zero_b[:, :G_QK], GDK, post_k, "gdn_conv_k")
    v = _conv_fwd(pg, 2 * G_QK, g_cw[:, 2 * G_QK:], zero_b[:, :G_V], GDK, _silu, "gdn_conv_v")
    braw = _gates_to_rows(pg_small[:, :GH], GH, GCH, GDN_CPB)
    araw = _gates_to_rows(pg_small[:, GH:2 * GH], GH, GCH, GDN_CPB)
    o, g_sall, g_tall = _gdn_fwd(q, k, v, braw, araw, g_alog, g_dtb)
    y0 = _gate_fwd(_gdn_gate, o, pg, G_CONV, g_nw, "gdn_gate")
    x1 = _matmul(y0, w_go, "nn", "gdn_out_proj", add=x0)

    hid1 = _rms_fwd(x1, nw1, "rms1")
    ps = _matmul(hid1, ws_main, "nn", "ssd_in_proj", tm=2048, tn=S_MAIN // 4)
    ps_small = _matmul(hid1, ws_small, "nn", "ssd_in_proj_small", tn=SMALL_W)
    c_x, c_b, c_c = S_INNER, 2 * S_INNER, 2 * S_INNER + SG * SN
    post_s = _silu
    xs = _conv_fwd(ps, c_x, s_cw[:, :S_INNER], s_cb[:, :S_INNER], SR * SP, post_s, "ssd_conv_x")
    bm = _conv_fwd(ps, c_b, s_cw[:, S_INNER:S_INNER + SG * SN], s_cb[:, S_INNER:S_INNER + SG * SN], SN,
                   post_s, "ssd_conv_b")
    cm = _conv_fwd(ps, c_c, s_cw[:, S_INNER + SG * SN:], s_cb[:, S_INNER + SG * SN:], SN, post_s,
                   "ssd_conv_c")
    def dt_rows(cpb):
        nb = t // (SCH * cpb)
        rows = _gates_to_rows(ps_small[:, :SH], SH, SCH, cpb)
        return rows.reshape(nb, SG, SR, cpb, SCH).transpose(0, 3, 2, 1, 4).reshape(nb, cpb * SR, SG, 1, SCH)

    nbs = t // (SCH * SSD_CPB_BWD)
    dtraw = dt_rows(SSD_CPB_BWD)
    yss, s_sall = _ssd_fwd(xs, bm, cm, dt_rows(SSD_CPB_FWD), s_dtb, s_alog, s_d)
    y1 = _gate_fwd(_ssd_gate, yss, ps, 0, s_nw, "ssd_gate")
    x2 = _matmul(y1, w_so, "nn", "ssd_out_proj", add=x1)

    dx2, d_fw, loss_row = _final_loss(x2, fw, tgt, "final_loss")

    dy1 = _matmul(dx2, w_so, "nt", "ssd_out_dx", out_dtype=BF16, tn=S_INNER)
    d_wso = _matmul(y1, dx2, "tn", "ssd_out_dw")
    dyss, dz_s, d_snw = _gate_bwd(_ssd_gate, yss, ps, 0, s_nw, dy1, True, "ssd_gate_bwd", Z_EXT)
    dxs, dbm, dcm, ddtraw, d_sdtb, d_salog, d_sd = _ssd_bwd(xs, bm, cm, dtraw, s_dtb, s_alog, s_d, s_sall, dyss)
    dps, dwx, dbx = _conv_dpre(ps, c_x, s_cw[:, :S_INNER], s_cb[:, :S_INNER], SR * SP, post_s, dxs, "ssd_dpre_x",
                               G_CONV, 0)
    dps, dwb, dbb = _conv_dpre(ps, c_b, s_cw[:, S_INNER:S_INNER + SG * SN], s_cb[:, S_INNER:S_INNER + SG * SN],
                               SN, post_s, dbm, "ssd_dpre_b", G_CONV, S_INNER, dps)
    dps, dwc, dbc = _conv_dpre(ps, c_c, s_cw[:, S_INNER + SG * SN:], s_cb[:, S_INNER + SG * SN:], SN, post_s,
                               dcm, "ssd_dpre_c", G_CONV, S_INNER + SG * SN, dps)
    d_scw = jnp.concatenate([dwx, dwb, dwc], axis=1)
    d_scb = jnp.concatenate([dbx, dbb, dbc], axis=1)
    dxbc = _conv_t(dps, s_cw, "ssd_conv_t")
    ddt = ddtraw.reshape(nbs, SSD_CPB_BWD, SR, SG, SCH).transpose(0, 3, 2, 1, 4).reshape(nbs, SH, SSD_CPB_BWD, SCH)
    ddt = _rows_to_gates(ddt)
    dsm_s = jnp.concatenate([ddt, jnp.zeros((t, SMALL_W - SH), F32)], axis=1).astype(BF16)
    dz_s = lax.dynamic_update_slice(dz_s, dsm_s, (0, S_INNER))
    ws_zx = jnp.concatenate([ws_main[:, :S_INNER], ws_small], axis=1)
    dhid1 = _matmul(dz_s, ws_zx, "nt", "ssd_in_dx_z", tk=Z_EXT)
    dhid1 = _matmul(dxbc, ws_main[:, S_INNER:], "nt", "ssd_in_dx_xbc", add=dhid1, tk=G_CONV)
    dw_zx = _matmul(hid1, dz_s, "tn", "ssd_in_dw_z", tn=Z_EXT, tk=1024)
    d_wsi = jnp.concatenate([dw_zx[:, :S_INNER], _matmul(hid1, dxbc, "tn", "ssd_in_dw_xbc", tk=4096),
                             dw_zx[:, S_INNER:S_INNER + SH]], axis=1)
    dx1, d_nw1 = _rms_bwd(x1, nw1, dx2, dhid1, "rms1_bwd")

    dy0 = _matmul(dx1, w_go, "nt", "gdn_out_dx", out_dtype=BF16, tn=G_V)
    d_wgo = _matmul(y0, dx1, "tn", "gdn_out_dw")
    do, dz_g, d_gnw = _gate_bwd(_gdn_gate, o, pg, G_CONV, g_nw, dy0, False, "gdn_gate_bwd", Z_EXT)
    dq, dk, dv, dbraw, daraw, d_galog, d_gdtb = _gdn_bwd(q, k, v, braw, araw, g_alog, g_dtb, g_sall, g_tall, do)
    dpg, dwq, _ = _conv_dpre(pg, 0, g_cw[:, :G_QK], zero_b[:, :G_QK], GDK, post_q, dq, "gdn_dpre_q", G_CONV, 0)
    dpg, dwk, _ = _conv_dpre(pg, G_QK, g_cw[:, G_QK:2 * G_QK], zero_b[:, :G_QK], GDK, post_k, dk, "gdn_dpre_k",
                             G_CONV, G_QK, dpg)
    dpg, dwv, _ = _conv_dpre(pg, 2 * G_QK, g_cw[:, 2 * G_QK:], zero_b[:, :G_V], GDK, _silu, dv, "gdn_dpre_v",
                             G_CONV, 2 * G_QK, dpg)
    d_gcw = jnp.concatenate([dwq, dwk, dwv], axis=1)
    dqkv = _conv_t(dpg, g_cw, "gdn_conv_t")
    dsm_g = jnp.concatenate([_rows_to_gates(dbraw), _rows_to_gates(daraw),
                             jnp.zeros((t, SMALL_W - 2 * GH), F32)], axis=1).astype(BF16)
    dz_g = lax.dynamic_update_slice(dz_g, dsm_g, (0, G_V))
    wg_zx = jnp.concatenate([wg_main[:, G_CONV:], wg_small], axis=1)
    dhid0 = _matmul(dqkv, wg_main[:, :G_CONV], "nt", "gdn_in_dx_qkv", tk=G_CONV)
    dhid0 = _matmul(dz_g, wg_zx, "nt", "gdn_in_dx_z", add=dhid0, tk=Z_EXT)
    dw_zx = _matmul(hid0, dz_g, "tn", "gdn_in_dw_z", tn=Z_EXT, tk=1024)
    d_wgi = jnp.concatenate([_matmul(hid0, dqkv, "tn", "gdn_in_dw_qkv", tk=4096), dw_zx[:, :G_V + 2 * GH]], axis=1)
    dx0, d_nw0 = _rms_bwd(x0, nw0, dx1, dhid0, "rms0_bwd")

    def in_blocks(dw):
        return dw.reshape(D, N_CHIPS, IN_SHARD).transpose(1, 0, 2).reshape(N_CHIPS, 2, D // 2, IN_SHARD)

    def out_blocks(dw):
        return dw.reshape(N_CHIPS, 2, OUT_SHARD // 2, D)

    gs = [in_blocks(d_wgi), out_blocks(d_wgo), in_blocks(d_wsi), out_blocks(d_wso)]
    from_sib = _swap_halves(gs)
    hsum, hsum_bf = _sum_cores(gs, from_sib, cc.astype(jnp.int32).reshape(1))
    recv = _scatter_chips(hsum_bf)
    tots = _sum_chips(hsum, recv, chip.astype(jnp.int32).reshape(1))
    sib_tots = _swap_totals(tots)
    full = [jnp.concatenate([jnp.where(cc == 0, mine, sib), jnp.where(cc == 0, sib, mine)], axis=0)
            for mine, sib in zip(tots, sib_tots)]
    grads = dict(
        gdn_w_in=full[0].reshape(1, D, IN_SHARD), gdn_w_out=full[1].reshape(1, OUT_SHARD, D),
        ssd_w_in=full[2].reshape(1, D, IN_SHARD), ssd_w_out=full[3].reshape(1, OUT_SHARD, D))

    small_parts = [loss_row, jnp.concatenate([d_nw0, d_nw1], axis=0), d_gcw, d_galog, d_gdtb, d_gnw, d_scw, d_scb,
                   d_sdtb, d_salog, d_sd, d_snw, d_fw]
    sbuf, soffs = _pack_flat(small_parts, 8)
    ssum = _unpack_flat(_allreduce_small(sbuf), soffs)
    (loss_s, g_nw_all, g_gcw, g_galog, g_gdtb, g_gnw, g_scw, g_scb, g_sdtb, g_salog, g_sd, g_snw, g_fw) = ssum

    def my_cols(a, width):
        return lax.dynamic_slice_in_dim(a, chip * width, width, axis=a.ndim - 1)

    grads.update(
        norm_w=g_nw_all, gdn_conv_w=my_cols(g_gcw, 1024)[None], gdn_a_log=g_galog.reshape(1, GH),
        gdn_dt_bias=g_gdtb.reshape(1, GH), gdn_norm_w=g_gnw.reshape(1, GDK),
        ssd_conv_w=my_cols(g_scw, 1024)[None], ssd_conv_b=my_cols(g_scb, 1024),
        ssd_dt_bias=g_sdtb.reshape(SR, SG).T.reshape(1, SH), ssd_a_log=g_salog.reshape(SR, SG).T.reshape(1, SH),
        ssd_d=g_sd.reshape(SR, SG).T.reshape(1, SH),
        ssd_norm_w=my_cols(g_snw, 512), final_norm_w=g_fw.reshape(D))
    loss = loss_s[0, 0]

    big_names = ("gdn_w_in", "gdn_w_out", "ssd_w_in", "ssd_w_out")
    deltas, new_m, new_v = {}, {}, {}
    for n in big_names:
        deltas[n], new_m[n], new_v[n] = _adamw(ws[n], grads[n], ms[n], vs[n], "adamw_" + n)
    rest = [n for n in names if n not in big_names]
    packs = [_pack_flat([d[n] for n in rest], 8) for d in (ws, grads, ms, vs)]
    outs = _adamw(*[p[0] for p in packs], "adamw_small")
    for d, buf in zip((deltas, new_m, new_v), outs):
        for n, a in zip(rest, _unpack_flat(buf, packs[0][1])):
            d[n] = a

    grad_x = dx0.reshape(1, t, D)
    return (loss, grad_x, *[grads[n] for n in names], *[deltas[n] for n in names],
            *[new_m[n] for n in names], *[new_v[n] for n in names])
```

```python
import functools

import jax
import jax.numpy as jnp
from jax import lax
from jax.experimental import pallas as pl
from jax.experimental.pallas import tpu as pltpu

F32 = jnp.float32
BF16 = jnp.bfloat16
SDS = jax.ShapeDtypeStruct
MESH = pl.DeviceIdType.MESH

D = 1024
EPS = 1e-6
CONV_K = 4
N_CHIPS = 4
GH = 16
GHQ = 8
GDK = 128
GCH = 128
G_QK = 1024
G_V = 2048
G_CONV = 4096
G_MAIN = 6144
G_IN = 6176
SH = 32
SP = 64
SN = 128
SG = 8
SR = 4
SCH = 128
S_INNER = 2048
S_MAIN = 6144
S_IN = 6176
IN_SHARD = 1544
OUT_SHARD = 512
SMALL_W = 128
Z_EXT = 2048 + SMALL_W

ADAM_LR = 0.001
ADAM_B1 = 0.9
ADAM_B2 = 0.999
ADAM_EPS = 1e-08
ADAM_WD = 0.01
ADAM_STEP = 10

VMEM_LIMIT = 56 * 1024 * 1024
BLOCK_ELEMS = 512 * 1024
PLAIN_BLOCK_ELEMS = 1024 * 1024
FWD_COLS = 512
NEG = -1e30


def _pcall(body, **kw):
    return pl.pallas_call(body, **kw)


def _cp(sem=None, vmem=VMEM_LIMIT):
    return pltpu.CompilerParams(dimension_semantics=sem, vmem_limit_bytes=vmem)


@jax.custom_jvp
def _sigmoid(x):
    return 1.0 / (1.0 + jnp.exp(-x))


@_sigmoid.defjvp
def _sigmoid_jvp(primals, tangents):
    s = _sigmoid(primals[0])
    return s, tangents[0] * (s * (1.0 - s))


@jax.custom_jvp
def _silu(x):
    return x * _sigmoid(x)


@_silu.defjvp
def _silu_jvp(primals, tangents):
    x = primals[0]
    s = _sigmoid(x)
    return x * s, tangents[0] * (s * (1.0 + x * (1.0 - s)))


def _softplus(x):
    return jnp.maximum(x, 0.0) + jnp.log(1.0 + jnp.exp(-jnp.abs(x)))


def _rms(x, w):
    return x * lax.rsqrt(jnp.mean(x * x, axis=-1, keepdims=True) + EPS) * w


_DIMS = {"nn": (((2,), (1,)), ((0,), (0,))),
         "nt": (((2,), (2,)), ((0,), (0,))),
         "tn": (((1,), (1,)), ((0,), (0,)))}


def _bdot(a, b, spec):
    return lax.dot_general(a.astype(BF16), b.astype(BF16), _DIMS[spec], preferred_element_type=F32)


@functools.partial(jax.custom_vjp, nondiff_argnums=(2,))
def _bmm(a, b, spec):
    return _bdot(a, b, spec)


def _bmm_fwd(a, b, spec):
    return _bdot(a, b, spec), (a, b)


def _bmm_bwd(spec, res, g):
    a, b = res
    if spec == "nn":
        return _bdot(g, b, "nt"), _bdot(a, g, "tn")
    if spec == "nt":
        return _bdot(g, b, "nn"), _bdot(g, a, "tn")
    return _bdot(b, g, "nt"), _bdot(a, g, "nn")


_bmm.defvjp(_bmm_fwd, _bmm_bwd)


@jax.custom_vjp
def _tri_inv(n):
    t = -n
    p = n
    steps = (n.shape[-1] - 1).bit_length() - 1
    r = lax.broadcasted_iota(jnp.int32, n.shape, 1)
    c = lax.broadcasted_iota(jnp.int32, n.shape, 2)
    t = t + jnp.where(r == c, 1.0, 0.0)
    for _ in range(steps):
        p = _bdot(p, p, "nn")
        t = t + _bdot(t, p, "nn")
    return t


def _tri_inv_fwd(n):
    t = _tri_inv(n)
    return t, t


def _tri_inv_bwd(t, g):
    return (-_bdot(_bdot(t, g, "tn"), t, "nt"),)


_tri_inv.defvjp(_tri_inv_fwd, _tri_inv_bwd)


@jax.custom_vjp
def _tri_inv_known(n, t):
    del n
    return t


def _tri_inv_known_fwd(n, t):
    del n
    return t, t


def _tri_inv_known_bwd(t, g):
    return _tri_inv_bwd(t, g)[0], jnp.zeros_like(t)


_tri_inv_known.defvjp(_tri_inv_known_fwd, _tri_inv_known_bwd)


def _masks(c, lead=1):
    r = lax.broadcasted_iota(jnp.int32, (lead, c, c), 1)
    s = lax.broadcasted_iota(jnp.int32, (lead, c, c), 2)
    return r >= s, r > s, r == s, r <= s


def _row_to_col(row, eye):
    return jnp.sum(jnp.where(eye, row, 0.0), axis=2, keepdims=True)


def _gdn_chunk(q, k, v, braw, araw, alog, dtb, s, t_known=None, want_t=False):
    h = v.shape[0]
    c = v.shape[1]
    rep = h // q.shape[0]
    tril, strict, eye, triu = _masks(c)
    qq = jnp.broadcast_to(q[:, None], (q.shape[0], rep) + q.shape[1:]).reshape(v.shape)
    kk = jnp.broadcast_to(k[:, None], (k.shape[0], rep) + k.shape[1:]).reshape(v.shape)
    beta_row = _sigmoid(braw)
    g_row = -jnp.exp(alog) * _softplus(araw + dtb)
    beta_col = _row_to_col(beta_row, eye)
    g_col = _row_to_col(g_row, eye)
    gc_col = jnp.sum(jnp.where(tril, g_row, 0.0), axis=2, keepdims=True)
    gc_row = jnp.sum(jnp.where(triu, g_col, 0.0), axis=1, keepdims=True)
    gc_last = jnp.sum(g_row, axis=2, keepdims=True)
    lmat = jnp.exp(jnp.where(tril, gc_col - gc_row, NEG))
    kb = kk * beta_col
    vb = v * beta_col
    n = jnp.where(strict, _bmm(kb, kk, "nt") * lmat, 0.0)
    t = _tri_inv(n) if t_known is None else _tri_inv_known(n, t_known)
    e_col = jnp.exp(gc_col)
    u = _bmm(t, vb, "nn")
    w = _bmm(t, kb * e_col, "nn")
    attn = _bmm(qq, kk, "nt") * lmat
    q_dec = qq * e_col
    k_dec = kk * jnp.exp(gc_last - gc_col)
    v_new = u - _bmm(w, s, "nn")
    o = _bmm(q_dec, s, "nn") + _bmm(attn, v_new, "nn")
    s_new = s * jnp.exp(gc_last) + _bmm(k_dec, v_new, "tn")
    return (o, s_new, t) if want_t else (o, s_new)


def _ssd_chunk(xs, bm, cm, dtraw, dtb, alog, dskip, s):
    c = xs.shape[1]
    tril, _, eye, triu = _masks(c)
    lane = lax.broadcasted_iota(jnp.int32, (1, 1, SR * SP), 2)
    prow = lax.broadcasted_iota(jnp.int32, (1, SR * SP, 1), 1)
    cb = _bmm(cm, bm, "nt")
    cs = _bmm(cm, s, "nt")

    def per_head(vals, idx):
        out = vals[SR - 1]
        for r in reversed(range(SR - 1)):
            out = jnp.where(idx < (r + 1) * SP, vals[r], out)
        return out

    dt_cols, e_cols, lmats, dstates, declast = [], [], [], [], []
    for r in range(SR):
        dt_row = _softplus(dtraw[r] + dtb[r])
        adt_row = -jnp.exp(alog[r]) * dt_row
        dt_cols.append(_row_to_col(dt_row, eye))
        adt_col = _row_to_col(adt_row, eye)
        acs_col = jnp.sum(jnp.where(tril, adt_row, 0.0), axis=2, keepdims=True)
        acs_row = jnp.sum(jnp.where(triu, adt_col, 0.0), axis=1, keepdims=True)
        acs_last = jnp.sum(adt_row, axis=2, keepdims=True)
        lmats.append(jnp.exp(jnp.where(tril, acs_col - acs_row, NEG)))
        e_cols.append(jnp.exp(acs_col))
        dstates.append(jnp.exp(acs_last - acs_col))
        declast.append(jnp.exp(acs_last))
    xd = xs * per_head(dt_cols, lane)
    y = per_head([_bmm(cb * lmats[r], xd, "nn") for r in range(SR)], lane)
    states = per_head([_bmm(xd, bm * dstates[r], "tn") for r in range(SR)], prow)
    y = y + cs * per_head(e_cols, lane) + xs * per_head([dskip[r] for r in range(SR)], lane)
    s_new = s * per_head(declast, prow) + states
    return y, s_new


def _matmul(a, b, mode, name, out_dtype=F32, add=None, tm=1024, tn=1024, tk=2048):
    if mode == "nn":
        (m, k), n = a.shape, b.shape[1]
    elif mode == "nt":
        (m, k), n = a.shape, b.shape[0]
    else:
        (k, m), n = a.shape, b.shape[1]
    tm, tn, tk = min(tm, m), min(tn, n), min(tk, k)
    assert m % tm == 0 and n % tn == 0 and k % tk == 0, (name, m, n, k)
    nk = k // tk
    dims = {"nn": (((1,), (0,)), ((), ())), "nt": (((1,), (1,)), ((), ())),
            "tn": (((0,), (0,)), ((), ()))}[mode]
    a_spec = {"nn": pl.BlockSpec((tm, tk), lambda i, j, kk: (i, kk)),
              "nt": pl.BlockSpec((tm, tk), lambda i, j, kk: (i, kk)),
              "tn": pl.BlockSpec((tk, tm), lambda i, j, kk: (kk, i))}[mode]
    b_spec = {"nn": pl.BlockSpec((tk, tn), lambda i, j, kk: (kk, j)),
              "nt": pl.BlockSpec((tn, tk), lambda i, j, kk: (j, kk)),
              "tn": pl.BlockSpec((tk, tn), lambda i, j, kk: (kk, j))}[mode]
    o_spec = pl.BlockSpec((tm, tn), lambda i, j, kk: (i, j))
    has_add = add is not None

    def body(*refs):
        a_ref, b_ref = refs[:2]
        add_ref = refs[2] if has_add else None
        o_ref = refs[2 + has_add]
        part = lax.dot_general(a_ref[...].astype(BF16), b_ref[...].astype(BF16), dims,
                               preferred_element_type=F32)

        def finish(r):
            if has_add:
                r = r + add_ref[...].astype(F32)
            o_ref[...] = r.astype(o_ref.dtype)

        if nk == 1:
            finish(part)
            return
        acc_ref = refs[3 + has_add]
        kk = pl.program_id(2)

        @pl.when(kk == 0)
        def _():
            acc_ref[...] = part

        @pl.when(jnp.logical_and(kk > 0, kk < nk - 1))
        def _():
            acc_ref[...] += part

        @pl.when(kk == nk - 1)
        def _():
            finish(acc_ref[...] + part)

    ins = [a, b] + ([add] if has_add else [])
    in_specs = [a_spec, b_spec] + ([o_spec] if has_add else [])
    scratch = [] if nk == 1 else [pltpu.VMEM((tm, tn), F32)]
    return _pcall(body, grid=(m // tm, n // tn, nk), in_specs=in_specs, out_specs=o_spec,
                  out_shape=SDS((m, n), out_dtype), scratch_shapes=scratch,
                  compiler_params=_cp(("parallel", "parallel", "arbitrary")), name=name)(*ins)


def _ew(fn, ins, in_specs, out_shape, out_specs, grid, name):
    n_in = len(ins)

    def body(*refs):
        outs = fn(*[r[...] for r in refs[:n_in]])
        for r, o in zip(refs[n_in:], outs):
            r[...] = o.astype(r.dtype)

    return _pcall(body, grid=grid, in_specs=in_specs, out_specs=out_specs, out_shape=out_shape,
                  compiler_params=_cp(("arbitrary",) * len(grid)), name=name)(*ins)


def _ew_vjp(fn, ins, in_specs, cts, ct_specs, wrt, g_shape, g_specs, acc, grid, name):
    n_in, n_ct = len(ins), len(cts)

    def body(*refs):
        vals = [r[...].astype(F32) for r in refs[:n_in]]
        outs, vjp = jax.vjp(fn, *vals)
        g_all = vjp(tuple(r[...].astype(F32) for r in refs[n_in:n_in + n_ct]))
        for pos, (i, g_ref) in enumerate(zip(wrt, refs[n_in + n_ct:])):
            g = g_all[i]
            if pos in acc:
                first = functools.reduce(
                    jnp.logical_and, [pl.program_id(ax) == 0 for ax in range(acc[pos], len(grid))])

                @pl.when(first)
                def _():
                    g_ref[...] = jnp.zeros_like(g_ref)

                g_ref[...] += g.astype(g_ref.dtype)
            else:
                g_ref[...] = g.astype(g_ref.dtype)

    return _pcall(body, grid=grid, in_specs=list(in_specs) + list(ct_specs), out_specs=g_specs,
                  out_shape=g_shape, compiler_params=_cp(("arbitrary",) * len(grid)),
                  name=name)(*ins, *cts)


def _row_spec(tm, n):
    return pl.BlockSpec((tm, n), lambda i: (i, 0))


def _par_spec(n):
    return pl.BlockSpec((1, n), lambda i: (0, 0))


def _rms_fwd(x, w, name):
    t = x.shape[0]
    tm = min(t, 512)
    return _ew(lambda xv, wv: (_rms(xv, wv),), [x, w], [_row_spec(tm, D), _par_spec(D)],
               [SDS((t, D), BF16)], [_row_spec(tm, D)], (t // tm,), name)[0]


def _rms_bwd(x, w, dres, dhid, name):
    t = x.shape[0]
    tm = min(t, 512)
    return _ew_vjp(lambda xv, wv: (xv, _rms(xv, wv)), [x, w], [_row_spec(tm, D), _par_spec(D)],
                   [dres, dhid], [_row_spec(tm, D), _row_spec(tm, D)], (0, 1),
                   [SDS((t, D), F32), SDS((1, D), F32)], [_row_spec(tm, D), _par_spec(D)],
                   {1: 0}, (t // tm,), name)


def _final_loss(x, w, tgt, name):
    t = x.shape[0]
    tm = min(t, 512)

    def body(x_ref, w_ref, t_ref, dx_ref, dw_ref, l_ref):
        @pl.when(pl.program_id(0) == 0)
        def _():
            dw_ref[...] = jnp.zeros_like(dw_ref)
            l_ref[...] = jnp.zeros_like(l_ref)

        xv, wv = x_ref[...], w_ref[...]
        rstd = lax.rsqrt(jnp.mean(xv * xv, axis=-1, keepdims=True) + EPS)
        xh = xv * rstd
        err = xh * wv - t_ref[...]
        l_ref[...] += 0.5 * jnp.sum(jnp.mean(err * err, axis=-1, keepdims=True), axis=0, keepdims=True)
        dy = err * (1.0 / D)
        dw_ref[...] += jnp.sum(dy * xh, axis=0, keepdims=True)
        dxh = dy * wv
        dx_ref[...] = rstd * (dxh - xh * jnp.mean(dxh * xh, axis=-1, keepdims=True))

    return _pcall(body, grid=(t // tm,), in_specs=[_row_spec(tm, D), _par_spec(D), _row_spec(tm, D)],
                  out_specs=[_row_spec(tm, D), _par_spec(D), _par_spec(128)],
                  out_shape=[SDS((t, D), F32), SDS((1, D), F32), SDS((1, 128), F32)],
                  compiler_params=_cp(("arbitrary",)), name=name)(x, w, tgt)


def _conv_taps(ext, w_ref, tm, lo):
    n = ext.shape[0]
    acc = None
    for j in range(CONV_K):
        shift = (CONV_K - 1 - j) if lo else (n - (CONV_K - 1 - j)) % n
        rolled = pltpu.roll(ext, shift, 0) if shift else ext
        term = w_ref[pl.ds(j, 1), :] * rolled[lo:lo + tm]
        acc = term if acc is None else acc + term
    return acc


def _conv_pre_specs(tm, ct, col0):
    hb = tm // 8
    return [pl.BlockSpec((tm, ct), lambda j, i: (i, col0 + j)),
            pl.BlockSpec((8, ct), lambda j, i: (jnp.maximum(i * hb - 1, 0), col0 + j)),
            pl.BlockSpec((CONV_K, ct), lambda j, i: (0, j)),
            pl.BlockSpec((1, ct), lambda j, i: (0, j))]


def _conv_pre_value(x_ref, xh_ref, w_ref, b_ref, tm):
    halo = jnp.where(pl.program_id(1) > 0, xh_ref[...], 0.0)
    ext = jnp.concatenate([halo, x_ref[...]], axis=0)
    return _conv_taps(ext, w_ref, tm, 8) + b_ref[...], ext


def _conv_fwd(x, col0, w, b, ct, post, name):
    t = x.shape[0]
    tpb = FWD_COLS // ct
    wide = tpb * ct
    tm = min(t, PLAIN_BLOCK_ELEMS // wide)
    nt = w.shape[1] // ct
    assert col0 % wide == 0 and nt % tpb == 0, name

    def body(x_ref, xh_ref, w_ref, b_ref, o_ref):
        pre, _ = _conv_pre_value(x_ref, xh_ref, w_ref, b_ref, tm)
        for k in range(tpb):
            o_ref[k] = post(pre[:, k * ct:(k + 1) * ct])

    return _pcall(body, grid=(nt // tpb, t // tm), in_specs=_conv_pre_specs(tm, wide, col0 // wide),
                  out_specs=pl.BlockSpec((tpb, tm, ct), lambda j, i: (j, i, 0)),
                  out_shape=SDS((nt, t, ct), F32), compiler_params=_cp(("arbitrary", "arbitrary")),
                  name=name)(x, x, w, b)


def _conv_dpre(x, col0, w, b, ct, post, dout, name, c_total, c_off, into=None):
    t = x.shape[0]
    tm = min(t, BLOCK_ELEMS // ct)
    nt = w.shape[1] // ct
    chained = into is not None

    def body(*refs):
        x_ref, xh_ref, w_ref, b_ref, do_ref = refs[:5]
        dp_ref, dw_ref, db_ref = refs[5 + chained:]
        pre, ext = _conv_pre_value(x_ref, xh_ref, w_ref, b_ref, tm)
        _, vjp = jax.vjp(post, pre)
        dpre = vjp(do_ref[0])[0]
        dp_ref[...] = dpre

        @pl.when(pl.program_id(1) == 0)
        def _():
            dw_ref[...] = jnp.zeros_like(dw_ref)
            db_ref[...] = jnp.zeros_like(db_ref)

        for j in range(CONV_K):
            xs = (pltpu.roll(ext, CONV_K - 1 - j, 0) if j < CONV_K - 1 else ext)[8:8 + tm]
            dw_ref[pl.ds(j, 1), :] += jnp.sum(dpre * xs, axis=0, keepdims=True)
        db_ref[...] += jnp.sum(dpre, axis=0, keepdims=True)

    c = w.shape[1]
    o0 = c_off // ct
    return _pcall(body, grid=(nt, t // tm),
                  in_specs=_conv_pre_specs(tm, ct, col0 // ct)
                  + [pl.BlockSpec((1, tm, ct), lambda j, i: (j, i, 0))] + ([ANY] if chained else []),
                  out_specs=[pl.BlockSpec((tm, ct), lambda j, i: (i, o0 + j)),
                             pl.BlockSpec((CONV_K, ct), lambda j, i: (0, j)),
                             pl.BlockSpec((1, ct), lambda j, i: (0, j))],
                  out_shape=[SDS((t, c_total), F32), SDS((CONV_K, c), F32), SDS((1, c), F32)],
                  input_output_aliases={5: 0} if chained else {},
                  compiler_params=_cp(("arbitrary", "arbitrary")), name=name)(
                      x, x, w, b, dout, *([into] if chained else []))


def _conv_t(dpre, w, name):
    t, c = dpre.shape
    ct = min(c, 512)
    tm = min(t, PLAIN_BLOCK_ELEMS // ct)
    hb = tm // 8
    last = t // tm - 1

    def body(d_ref, dh_ref, w_ref, o_ref):
        halo = jnp.where(pl.program_id(1) < last, dh_ref[...], 0.0)
        ext = jnp.concatenate([d_ref[...], halo], axis=0)
        o_ref[...] = _conv_taps(ext, w_ref, tm, 0).astype(o_ref.dtype)

    return _pcall(body, grid=(c // ct, t // tm),
                  in_specs=[pl.BlockSpec((tm, ct), lambda j, i: (i, j)),
                            pl.BlockSpec((8, ct), lambda j, i: (jnp.minimum((i + 1) * hb, t // 8 - 1), j)),
                            pl.BlockSpec((CONV_K, ct), lambda j, i: (0, j))],
                  out_specs=pl.BlockSpec((tm, ct), lambda j, i: (i, j)),
                  out_shape=SDS((t, c), BF16), compiler_params=_cp(("arbitrary", "arbitrary")),
                  name=name)(dpre, dpre, w)


def _l2norm_scaled(scale):
    def post(pre):
        a = _silu(pre)
        return a * lax.rsqrt(jnp.sum(a * a, axis=-1, keepdims=True) + EPS) * scale
    return post


GDN_HB = 16
GDN_CPB = 2
SSD_GB = 8
SSD_CPB_FWD = 2
SSD_CPB_BWD = 1


def _gdn_specs(nb, rev):
    hb, cpb, tc = GDN_HB, GDN_CPB, GDN_CPB * GCH
    blk = (lambda n: nb - 1 - n) if rev else (lambda n: n)
    seq = lambda h: pl.BlockSpec((h, tc, GDK), lambda g, n: (g, blk(n), 0))
    gate = pl.BlockSpec((1, hb, cpb, GCH), lambda g, n: (blk(n), g, 0, 0))
    par = pl.BlockSpec((hb, 1, 1), lambda g, n: (g, 0, 0))
    state = pl.BlockSpec((cpb, hb, GDK, GDK), lambda g, n: (blk(n), g, 0, 0))
    tinv = pl.BlockSpec((cpb, hb, GCH, GCH), lambda g, n: (blk(n), g, 0, 0))
    return seq, gate, par, state, tinv


def _gdn_fwd(q, k, v, braw, araw, alog, dtb):
    t = v.shape[1]
    hb, cpb = GDN_HB, GDN_CPB
    nb = t // (cpb * GCH)
    seq, gate, par, state, tinv = _gdn_specs(nb, False)

    def body(q_ref, k_ref, v_ref, b_ref, a_ref, al_ref, dt_ref, o_ref, sall_ref, tall_ref, s_ref):
        @pl.when(pl.program_id(1) == 0)
        def _():
            s_ref[...] = jnp.zeros_like(s_ref)

        s = s_ref[...]
        for c in range(cpb):
            rows = pl.ds(c * GCH, GCH)
            sall_ref[c] = s
            o, s, tmat = _gdn_chunk(q_ref[:, rows, :], k_ref[:, rows, :], v_ref[:, rows, :],
                                    b_ref[0, :, pl.ds(c, 1), :], a_ref[0, :, pl.ds(c, 1), :],
                                    al_ref[...], dt_ref[...], s, want_t=True)
            o_ref[:, rows, :] = o
            tall_ref[c] = tmat.astype(BF16)
        s_ref[...] = s

    return _pcall(body, grid=(GH // hb, nb),
                  in_specs=[seq(hb // 2), seq(hb // 2), seq(hb), gate, gate, par, par],
                  out_specs=[seq(hb), state, tinv],
                  out_shape=[SDS((GH, t, GDK), F32), SDS((t // GCH, GH, GDK, GDK), F32),
                             SDS((t // GCH, GH, GCH, GCH), BF16)],
                  scratch_shapes=[pltpu.VMEM((hb, GDK, GDK), F32)],
                  compiler_params=_cp(("arbitrary", "arbitrary")), name="gdn_chunk_fwd")(
                      q, k, v, braw, araw, alog, dtb)


def _gdn_bwd(q, k, v, braw, araw, alog, dtb, sall, tall, do):
    t = v.shape[1]
    hb, cpb = GDN_HB, GDN_CPB
    nb = t // (cpb * GCH)
    seq, gate, par, state, tinv = _gdn_specs(nb, True)

    def body(q_ref, k_ref, v_ref, b_ref, a_ref, al_ref, dt_ref, sall_ref, tall_ref, do_ref,
             dq_ref, dk_ref, dv_ref, db_ref, da_ref, dal_ref, ddt_ref, ds_ref):
        @pl.when(pl.program_id(1) == 0)
        def _():
            ds_ref[...] = jnp.zeros_like(ds_ref)
            dal_ref[...] = jnp.zeros_like(dal_ref)
            ddt_ref[...] = jnp.zeros_like(ddt_ref)

        ds = ds_ref[...]
        for c in reversed(range(cpb)):
            rows = pl.ds(c * GCH, GCH)
            fn = functools.partial(_gdn_chunk, t_known=tall_ref[c].astype(F32))
            _, vjp = jax.vjp(fn, q_ref[:, rows, :], k_ref[:, rows, :], v_ref[:, rows, :],
                             b_ref[0, :, pl.ds(c, 1), :], a_ref[0, :, pl.ds(c, 1), :],
                             al_ref[...], dt_ref[...], sall_ref[c])
            dq, dk, dv, db, da, dal, ddt, ds = vjp((do_ref[:, rows, :], ds))
            dq_ref[:, rows, :] = dq
            dk_ref[:, rows, :] = dk
            dv_ref[:, rows, :] = dv
            db_ref[0, :, pl.ds(c, 1), :] = db
            da_ref[0, :, pl.ds(c, 1), :] = da
            dal_ref[...] += dal
            ddt_ref[...] += ddt
        ds_ref[...] = ds

    return _pcall(body, grid=(GH // hb, nb),
                  in_specs=[seq(hb // 2), seq(hb // 2), seq(hb), gate, gate, par, par, state, tinv, seq(hb)],
                  out_specs=[seq(hb // 2), seq(hb // 2), seq(hb), gate, gate, par, par],
                  out_shape=[SDS(q.shape, F32), SDS(k.shape, F32), SDS(v.shape, F32),
                             SDS(braw.shape, F32), SDS(araw.shape, F32),
                             SDS((GH, 1, 1), F32), SDS((GH, 1, 1), F32)],
                  scratch_shapes=[pltpu.VMEM((hb, GDK, GDK), F32)],
                  compiler_params=_cp(("arbitrary", "arbitrary")), name="gdn_chunk_bwd")(
                      q, k, v, braw, araw, alog, dtb, sall, tall, do)


def _ssd_specs(nb, rev, cpb):
    gb, tc = SSD_GB, cpb * SCH
    blk = (lambda n: nb - 1 - n) if rev else (lambda n: n)
    seq = lambda w: pl.BlockSpec((gb, tc, w), lambda g, n: (g, blk(n), 0))
    gate = pl.BlockSpec((1, cpb * SR, gb, 1, SCH), lambda g, n: (blk(n), 0, g, 0, 0))
    par = pl.BlockSpec((SR, gb, 1, 1), lambda g, n: (0, g, 0, 0))
    state = pl.BlockSpec((cpb, gb, SR * SP, SN), lambda g, n: (blk(n), g, 0, 0))
    return seq, gate, par, state


def _ssd_fwd(xs, bm, cm, dtraw, dtb, alog, dskip):
    t = xs.shape[1]
    gb, cpb = SSD_GB, SSD_CPB_FWD
    nb = t // (cpb * SCH)
    seq, gate, par, state = _ssd_specs(nb, False, cpb)

    def body(x_ref, b_ref, c_ref, dt_ref, dtb_ref, al_ref, dk_ref, y_ref, sall_ref, s_ref):
        @pl.when(pl.program_id(1) == 0)
        def _():
            s_ref[...] = jnp.zeros_like(s_ref)

        s = s_ref[...]
        for c in range(cpb):
            rows = pl.ds(c * SCH, SCH)
            sall_ref[c] = s
            y, s = _ssd_chunk(x_ref[:, rows, :], b_ref[:, rows, :], c_ref[:, rows, :],
                              dt_ref[0, pl.ds(c * SR, SR)], dtb_ref[...], al_ref[...],
                              dk_ref[...], s)
            y_ref[:, rows, :] = y
        s_ref[...] = s

    return _pcall(body, grid=(SG // gb, nb),
                  in_specs=[seq(SR * SP), seq(SN), seq(SN), gate, par, par, par],
                  out_specs=[seq(SR * SP), state],
                  out_shape=[SDS((SG, t, SR * SP), F32), SDS((t // SCH, SG, SR * SP, SN), F32)],
                  scratch_shapes=[pltpu.VMEM((gb, SR * SP, SN), F32)],
                  compiler_params=_cp(("arbitrary", "arbitrary")), name="ssd_chunk_fwd")(
                      xs, bm, cm, dtraw, dtb, alog, dskip)


def _ssd_bwd(xs, bm, cm, dtraw, dtb, alog, dskip, sall, dy):
    t = xs.shape[1]
    gb, cpb = SSD_GB, SSD_CPB_BWD
    nb = t // (cpb * SCH)
    seq, gate, par, state = _ssd_specs(nb, True, cpb)

    def body(x_ref, b_ref, c_ref, dt_ref, dtb_ref, al_ref, dk_ref, sall_ref, dy_ref,
             dx_ref, dbm_ref, dcm_ref, ddt_ref, ddtb_ref, dal_ref, ddk_ref, ds_ref):
        @pl.when(pl.program_id(1) == 0)
        def _():
            ds_ref[...] = jnp.zeros_like(ds_ref)
            ddtb_ref[...] = jnp.zeros_like(ddtb_ref)
            dal_ref[...] = jnp.zeros_like(dal_ref)
            ddk_ref[...] = jnp.zeros_like(ddk_ref)

        ds = ds_ref[...]
        for c in reversed(range(cpb)):
            rows = pl.ds(c * SCH, SCH)
            _, vjp = jax.vjp(_ssd_chunk, x_ref[:, rows, :], b_ref[:, rows, :], c_ref[:, rows, :],
                             dt_ref[0, pl.ds(c * SR, SR)], dtb_ref[...], al_ref[...],
                             dk_ref[...], sall_ref[c])
            dx, dbm, dcm, ddt, ddtb, dal, ddk, ds = vjp((dy_ref[:, rows, :], ds))
            dx_ref[:, rows, :] = dx
            dbm_ref[:, rows, :] = dbm
            dcm_ref[:, rows, :] = dcm
            ddt_ref[0, pl.ds(c * SR, SR)] = ddt
            ddtb_ref[...] += ddtb
            dal_ref[...] += dal
            ddk_ref[...] += ddk
        ds_ref[...] = ds

    return _pcall(body, grid=(SG // gb, nb),
                  in_specs=[seq(SR * SP), seq(SN), seq(SN), gate, par, par, par, state, seq(SR * SP)],
                  out_specs=[seq(SR * SP), seq(SN), seq(SN), gate, par, par, par],
                  out_shape=[SDS(xs.shape, F32), SDS(bm.shape, F32), SDS(cm.shape, F32),
                             SDS(dtraw.shape, F32), SDS((SR, SG, 1, 1), F32), SDS((SR, SG, 1, 1), F32),
                             SDS((SR, SG, 1, 1), F32)],
                  scratch_shapes=[pltpu.VMEM((gb, SR * SP, SN), F32)],
                  compiler_params=_cp(("arbitrary", "arbitrary")), name="ssd_chunk_bwd")(
                      xs, bm, cm, dtraw, dtb, alog, dskip, sall, dy)


def _gate_specs(tm, ct, zcol0, per_tile_w):
    z0 = zcol0 // ct
    return [pl.BlockSpec((1, tm, ct), lambda i, j: (j, i, 0)),
            pl.BlockSpec((tm, ct), lambda i, j: (i, z0 + j)),
            pl.BlockSpec((1, ct), (lambda i, j: (0, j)) if per_tile_w else (lambda i, j: (0, 0)))]


def _gdn_gate(o, z, w):
    return (_rms(o[0], w) * _silu(z),)


def _ssd_gate(y, z, w):
    return (_rms(y[0] * _silu(z), w),)


def _gate_fwd(fn, o, proj, zcol0, w, name):
    nt, t, ct = o.shape
    tpb = FWD_COLS // ct
    wide = tpb * ct
    tm = min(t, PLAIN_BLOCK_ELEMS // wide)
    per_tile_w = w.shape[1] > ct
    assert zcol0 % wide == 0 and nt % tpb == 0, name
    z0 = zcol0 // wide
    specs = [pl.BlockSpec((tpb, tm, ct), lambda i, j: (j, i, 0)),
             pl.BlockSpec((tm, wide), lambda i, j: (i, z0 + j)),
             pl.BlockSpec((1, wide), lambda i, j: (0, j)) if per_tile_w
             else pl.BlockSpec((1, ct), lambda i, j: (0, 0))]

    def fn_wide(ov, zv, wv):
        outs = []
        for k in range(tpb):
            cols = slice(k * ct, (k + 1) * ct)
            outs.append(fn(ov[k:k + 1], zv[:, cols], wv[:, cols] if per_tile_w else wv)[0])
        return (jnp.concatenate(outs, axis=1),)

    return _ew(fn_wide, [o, proj, w], specs, [SDS((t, nt * ct), BF16)],
               [pl.BlockSpec((tm, wide), lambda i, j: (i, j))], (t // tm, nt // tpb), name)[0]


def _gate_bwd(fn, o, proj, zcol0, w, dy, wacc, name, dz_cols):
    nt, t, ct = o.shape
    tm = min(t, BLOCK_ELEMS // ct)
    specs = _gate_specs(tm, ct, zcol0, wacc)
    out_spec = pl.BlockSpec((tm, ct), lambda i, j: (i, j))
    if wacc:
        flip = lambda s: pl.BlockSpec(s.block_shape, lambda j, i, f=s.index_map: f(i, j))
        specs = [flip(s) for s in specs]
        out_spec = flip(out_spec)
        grid, acc = (nt, t // tm), {2: 1}
    else:
        grid, acc = (t // tm, nt), {2: 0}
    return _ew_vjp(fn, [o, proj, w], specs, [dy], [out_spec], (0, 1, 2),
                   [SDS(o.shape, F32), SDS((t, dz_cols), BF16), SDS(w.shape, F32)],
                   [specs[0], out_spec, specs[2]], acc, grid, name)


def _adamw_math(w, g, m, v):
    m = ADAM_B1 * m + (1.0 - ADAM_B1) * g
    v = ADAM_B2 * v + (1.0 - ADAM_B2) * jnp.square(g)
    m_hat = m / (1.0 - ADAM_B1 ** ADAM_STEP)
    v_hat = v / (1.0 - ADAM_B2 ** ADAM_STEP)
    delta = -ADAM_LR * (m_hat / (jnp.sqrt(v_hat) + ADAM_EPS) + ADAM_WD * w)
    return delta, m, v


def _adamw(w, g, m, v, name):
    shape = w.shape
    w2, g2, m2, v2 = [a.reshape(-1, shape[-1]) for a in (w, g, m, v)]
    r, c = w2.shape
    tr = 256 if r % 256 == 0 else r
    spec = pl.BlockSpec((tr, c), lambda i: (i, 0))
    outs = _ew(_adamw_math, [w2, g2, m2, v2], [spec] * 4, [SDS((r, c), F32)] * 3, [spec] * 3,
               (r // tr,), name)
    return [o.reshape(shape) for o in outs]


def _coords():
    return lax.axis_index("x"), lax.axis_index("y"), lax.axis_index("c")


def _other_chips(x, y):
    return [(1 - x, y), (x, 1 - y), (1 - x, 1 - y)]


ANY = pl.BlockSpec(memory_space=pl.ANY)


def _rcopy(src, dst, send_sems, recv_sems, k, to):
    return pltpu.make_async_remote_copy(src_ref=src, dst_ref=dst, send_sem=send_sems.at[k],
                                        recv_sem=recv_sems.at[k], device_id=to, device_id_type=MESH)


def _gather_weights(bigs, small):
    n = len(bigs)

    def body(*refs):
        in_refs, small_ref = refs[:n], refs[n]
        out_refs, osmall_ref = refs[n + 1:2 * n + 1], refs[2 * n + 1]
        send_sems, recv_sems = refs[2 * n + 2:]
        x, y, c = _coords()
        me = 2 * x + y
        sibling = (x, y, 1 - c)
        chips = _other_chips(x, y)
        halves = [b.shape[0] // 2 for b in bigs]
        mine = [pl.ds(c * h, h) for h in halves]
        theirs = [pl.ds((1 - c) * h, h) for h in halves]
        rc = functools.partial(_rcopy, send_sems=send_sems, recv_sems=recv_sems)

        first = []
        for a in range(n):
            for j, (px, py) in enumerate(chips):
                first.append(rc(in_refs[a].at[mine[a]], out_refs[a].at[me, mine[a]], k=3 * a + j, to=(px, py, c)))
        for j, (px, py) in enumerate(chips):
            first.append(rc(small_ref, osmall_ref.at[me], k=6 * n + j, to=(px, py, c)))
        for cp in first:
            cp.start()
        passed = []
        for a in range(n):
            for j, (px, py) in enumerate(chips):
                landed = out_refs[a].at[2 * px + py, mine[a]]
                rc(landed, landed, k=3 * a + j, to=(px, py, c)).wait_recv()
                fw = rc(landed, landed, k=3 * n + 3 * a + j, to=sibling)
                fw.start()
                passed.append(fw)
        for a in range(n):
            for j, (px, py) in enumerate(chips):
                landed = out_refs[a].at[2 * px + py, theirs[a]]
                rc(landed, landed, k=3 * n + 3 * a + j, to=sibling).wait_recv()
        for j, (px, py) in enumerate(chips):
            rc(small_ref, osmall_ref.at[2 * px + py], k=6 * n + j, to=(px, py, c)).wait_recv()
        for cp in first + passed:
            cp.wait_send()

    outs = _pcall(body, in_specs=[ANY] * (n + 1), out_specs=[ANY] * (n + 1),
                  out_shape=[SDS((N_CHIPS,) + b.shape, b.dtype) for b in bigs]
                  + [SDS((N_CHIPS,) + small.shape, small.dtype)],
                  scratch_shapes=[pltpu.SemaphoreType.DMA((6 * n + 3,)), pltpu.SemaphoreType.DMA((6 * n + 3,))],
                  name="gather_weights")(*bigs, small)
    me = 2 * lax.axis_index("x") + lax.axis_index("y")
    return [lax.dynamic_update_index_in_dim(o, own, me, 0) for o, own in zip(outs, list(bigs) + [small])]


def _swap_halves(gs):
    n = len(gs)

    def body(*refs):
        send_sems, recv_sems = refs[2 * n:]
        x, y, c = _coords()
        cps = [_rcopy(refs[a].at[:, 1 - c], refs[n + a], send_sems, recv_sems, a, (x, y, 1 - c))
               for a in range(n)]
        for cp in cps:
            cp.start()
        for cp in cps:
            cp.wait()

    return _pcall(body, in_specs=[ANY] * n, out_specs=[ANY] * n,
                  out_shape=[SDS((N_CHIPS,) + g.shape[2:], g.dtype) for g in gs],
                  scratch_shapes=[pltpu.SemaphoreType.DMA((n,)), pltpu.SemaphoreType.DMA((n,))],
                  name="swap_halves")(*gs)


def _sum_cores(gs, rs, half_idx):
    n = len(gs)
    ns = 2
    in_specs, out_specs, out_shape = [], [], []
    for g in gs:
        _, _, h, w = g.shape
        in_specs.append(pl.BlockSpec((1, 1, h // ns, w), lambda b, i, c_ref: (b, c_ref[0], i, 0)))
    for g in gs:
        _, _, h, w = g.shape
        spec = pl.BlockSpec((1, h // ns, w), lambda b, i, c_ref: (b, i, 0))
        in_specs.append(spec)
        out_specs += [spec, spec]
        out_shape += [SDS((N_CHIPS, h, w), F32), SDS((N_CHIPS, h, w), BF16)]

    def body(c_ref, *refs):
        del c_ref
        for a in range(n):
            tot = refs[a][0] + refs[n + a][...]
            refs[2 * n + 2 * a][...] = tot
            refs[2 * n + 2 * a + 1][...] = tot.astype(BF16)

    outs = _pcall(body, grid_spec=pltpu.PrefetchScalarGridSpec(
        num_scalar_prefetch=1, grid=(N_CHIPS, ns), in_specs=in_specs, out_specs=out_specs),
        out_shape=out_shape, compiler_params=_cp(("arbitrary", "arbitrary")), name="sum_cores")(
            half_idx, *gs, *rs)
    return outs[0::2], outs[1::2]


def _scatter_chips(hs):
    n = len(hs)

    def body(*refs):
        send_sems, recv_sems = refs[2 * n:]
        x, y, c = _coords()
        cps = []
        for a in range(n):
            for j, (px, py) in enumerate(_other_chips(x, y)):
                cps.append(_rcopy(refs[a].at[2 * px + py], refs[n + a].at[j], send_sems, recv_sems,
                                  3 * a + j, (px, py, c)))
        for cp in cps:
            cp.start()
        for cp in cps:
            cp.wait()

    return _pcall(body, in_specs=[ANY] * n, out_specs=[ANY] * n,
                  out_shape=[SDS((3,) + h.shape[1:], h.dtype) for h in hs],
                  scratch_shapes=[pltpu.SemaphoreType.DMA((3 * n,)), pltpu.SemaphoreType.DMA((3 * n,))],
                  name="scatter_chips")(*hs)


def _sum_chips(hs, xs, chip_idx):
    n = len(hs)
    ns = 2
    in_specs, out_specs, out_shape = [], [], []
    for h_arr in hs:
        _, h, w = h_arr.shape
        in_specs.append(pl.BlockSpec((1, h // ns, w), lambda i, c_ref: (c_ref[0], i, 0)))
    for h_arr in hs:
        _, h, w = h_arr.shape
        in_specs.append(pl.BlockSpec((3, h // ns, w), lambda i, c_ref: (0, i, 0)))
        out_specs.append(pl.BlockSpec((h // ns, w), lambda i, c_ref: (i, 0)))
        out_shape.append(SDS((h, w), F32))

    def body(c_ref, *refs):
        del c_ref
        for a in range(n):
            x_ref = refs[n + a]
            refs[2 * n + a][...] = (refs[a][0] + x_ref[0].astype(F32) + x_ref[1].astype(F32)
                                    + x_ref[2].astype(F32))

    return _pcall(body, grid_spec=pltpu.PrefetchScalarGridSpec(
        num_scalar_prefetch=1, grid=(ns,), in_specs=in_specs, out_specs=out_specs),
        out_shape=out_shape, compiler_params=_cp(("arbitrary",)), name="sum_chips")(chip_idx, *hs, *xs)


def _swap_totals(tots):
    n = len(tots)

    def body(*refs):
        send_sems, recv_sems = refs[2 * n:]
        x, y, c = _coords()
        cps = [_rcopy(refs[a], refs[n + a], send_sems, recv_sems, a, (x, y, 1 - c)) for a in range(n)]
        for cp in cps:
            cp.start()
        for cp in cps:
            cp.wait()

    return _pcall(body, in_specs=[ANY] * n, out_specs=[ANY] * n,
                  out_shape=[SDS(t.shape, t.dtype) for t in tots],
                  scratch_shapes=[pltpu.SemaphoreType.DMA((n,)), pltpu.SemaphoreType.DMA((n,))],
                  name="swap_totals")(*tots)


def _allreduce_small(buf):
    rows = buf.shape[0]

    def body(b_ref, o_ref, g_ref, send_sems, recv_sems):
        x, y, c = _coords()
        me = 4 * x + 2 * y + c
        g_ref[me] = b_ref[...]
        cps = []
        for k in range(1, 8):
            px = 1 - x if k & 4 else x
            py = 1 - y if k & 2 else y
            pc = 1 - c if k & 1 else c
            cps.append(pltpu.make_async_remote_copy(
                src_ref=b_ref, dst_ref=g_ref.at[me], send_sem=send_sems.at[k - 1],
                recv_sem=recv_sems.at[k - 1], device_id=(px, py, pc), device_id_type=MESH))
        for cp in cps:
            cp.start()
        for cp in cps:
            cp.wait()
        acc = g_ref[0]
        for d in range(1, 8):
            acc = acc + g_ref[d]
        o_ref[...] = acc

    vm = pl.BlockSpec(memory_space=pltpu.VMEM)
    return _pcall(body, in_specs=[vm], out_specs=vm, out_shape=SDS(buf.shape, F32),
                  scratch_shapes=[pltpu.VMEM((8, rows, 128), F32), pltpu.SemaphoreType.DMA((7,)),
                                  pltpu.SemaphoreType.DMA((7,))],
                  compiler_params=pltpu.CompilerParams(vmem_limit_bytes=VMEM_LIMIT),
                  name="allreduce_small")(buf)


def _pack_flat(parts, mult):
    rows, offs, r0 = [], [], 0
    for p in parts:
        f = p.reshape(-1)
        f = jnp.pad(f, (0, (-f.shape[0]) % (mult * 128))).reshape(-1, 128)
        rows.append(f)
        offs.append((r0, p.shape))
        r0 += f.shape[0]
    return jnp.concatenate(rows, axis=0), offs


def _unpack_flat(buf, offs):
    out = []
    for r0, shape in offs:
        n = 1
        for s in shape:
            n *= s
        nr = -(-n // 128)
        out.append(buf[r0:r0 + nr].reshape(-1)[:n].reshape(shape))
    return out


def _gates_to_rows(a, heads, chunk, cpb):
    t = a.shape[0]
    return a.reshape(t // (chunk * cpb), cpb, chunk, heads).transpose(0, 3, 1, 2)


def _rows_to_gates(a):
    nb, heads, cpb, chunk = a.shape
    return a.transpose(0, 2, 3, 1).reshape(nb * cpb * chunk, heads)


def kernel(x, norm_w, gdn_w_in, gdn_conv_w, gdn_a_log, gdn_dt_bias, gdn_norm_w, gdn_w_out, ssd_w_in, ssd_conv_w, ssd_conv_b, ssd_dt_bias, ssd_a_log, ssd_d, ssd_norm_w, ssd_w_out, final_norm_w, loss_target, m_norm_w, m_gdn_w_in, m_gdn_conv_w, m_gdn_a_log, m_gdn_dt_bias, m_gdn_norm_w, m_gdn_w_out, m_ssd_w_in, m_ssd_conv_w, m_ssd_conv_b, m_ssd_dt_bias, m_ssd_a_log, m_ssd_d, m_ssd_norm_w, m_ssd_w_out, m_final_norm_w, v_norm_w, v_gdn_w_in, v_gdn_conv_w, v_gdn_a_log, v_gdn_dt_bias, v_gdn_norm_w, v_gdn_w_out, v_ssd_w_in, v_ssd_conv_w, v_ssd_conv_b, v_ssd_dt_bias, v_ssd_a_log, v_ssd_d, v_ssd_norm_w, v_ssd_w_out, v_final_norm_w):
    ws = dict(norm_w=norm_w, gdn_w_in=gdn_w_in, gdn_conv_w=gdn_conv_w, gdn_a_log=gdn_a_log,
              gdn_dt_bias=gdn_dt_bias, gdn_norm_w=gdn_norm_w, gdn_w_out=gdn_w_out, ssd_w_in=ssd_w_in,
              ssd_conv_w=ssd_conv_w, ssd_conv_b=ssd_conv_b, ssd_dt_bias=ssd_dt_bias,
              ssd_a_log=ssd_a_log, ssd_d=ssd_d, ssd_norm_w=ssd_norm_w, ssd_w_out=ssd_w_out,
              final_norm_w=final_norm_w)
    ms = dict(norm_w=m_norm_w, gdn_w_in=m_gdn_w_in, gdn_conv_w=m_gdn_conv_w, gdn_a_log=m_gdn_a_log,
              gdn_dt_bias=m_gdn_dt_bias, gdn_norm_w=m_gdn_norm_w, gdn_w_out=m_gdn_w_out,
              ssd_w_in=m_ssd_w_in, ssd_conv_w=m_ssd_conv_w, ssd_conv_b=m_ssd_conv_b,
              ssd_dt_bias=m_ssd_dt_bias, ssd_a_log=m_ssd_a_log, ssd_d=m_ssd_d,
              ssd_norm_w=m_ssd_norm_w, ssd_w_out=m_ssd_w_out, final_norm_w=m_final_norm_w)
    vs = dict(norm_w=v_norm_w, gdn_w_in=v_gdn_w_in, gdn_conv_w=v_gdn_conv_w, gdn_a_log=v_gdn_a_log,
              gdn_dt_bias=v_gdn_dt_bias, gdn_norm_w=v_gdn_norm_w, gdn_w_out=v_gdn_w_out,
              ssd_w_in=v_ssd_w_in, ssd_conv_w=v_ssd_conv_w, ssd_conv_b=v_ssd_conv_b,
              ssd_dt_bias=v_ssd_dt_bias, ssd_a_log=v_ssd_a_log, ssd_d=v_ssd_d,
              ssd_norm_w=v_ssd_norm_w, ssd_w_out=v_ssd_w_out, final_norm_w=v_final_norm_w)
    names = list(ws)
    cx, cy, cc = _coords()
    chip = 2 * cx + cy
    t = x.shape[1]
    x0 = x.reshape(t, D)
    tgt = loss_target.reshape(t, D)

    bigs = [p[0].astype(BF16) for p in (gdn_w_in, gdn_w_out, ssd_w_in, ssd_w_out)]
    small, small_offs = _pack_flat([gdn_conv_w[0], ssd_conv_w[0], ssd_conv_b[0], ssd_norm_w[0]], 8)
    a_gi, a_go, a_si, a_so, gsmall = _gather_weights(bigs, small)
    w_gi = jnp.concatenate([a_gi[b] for b in range(4)], axis=1)
    w_si = jnp.concatenate([a_si[b] for b in range(4)], axis=1)
    w_go = a_go.reshape(4 * OUT_SHARD, D)
    w_so = a_so.reshape(4 * OUT_SHARD, D)
    sm = [_unpack_flat(gsmall[b], small_offs) for b in range(4)]
    g_cw = jnp.concatenate([sm[b][0] for b in range(4)], axis=1)
    s_cw = jnp.concatenate([sm[b][1] for b in range(4)], axis=1)
    s_cb = jnp.concatenate([sm[b][2] for b in range(4)], axis=0)[None]
    s_nw = jnp.concatenate([sm[b][3] for b in range(4)], axis=0)[None]

    def pad_small(w):
        return jnp.concatenate([w, jnp.zeros((D, SMALL_W - w.shape[1]), w.dtype)], axis=1)

    wg_main, wg_small = w_gi[:, :G_MAIN], pad_small(w_gi[:, G_MAIN:])
    ws_main, ws_small = w_si[:, :S_MAIN], pad_small(w_si[:, S_MAIN:])
    zero_b = jnp.zeros((1, G_CONV), F32)
    nw0, nw1 = norm_w[0:1], norm_w[1:2]
    fw = final_norm_w[None]
    g_alog = gdn_a_log.reshape(GH, 1, 1)
    g_dtb = gdn_dt_bias.reshape(GH, 1, 1)
    g_nw = gdn_norm_w.reshape(1, GDK)
    s_dtb = ssd_dt_bias.reshape(SG, SR).T.reshape(SR, SG, 1, 1)
    s_alog = ssd_a_log.reshape(SG, SR).T.reshape(SR, SG, 1, 1)
    s_d = ssd_d.reshape(SG, SR).T.reshape(SR, SG, 1, 1)

    hid0 = _rms_fwd(x0, nw0, "rms0")
    pg = _matmul(hid0, wg_main, "nn", "gdn_in_proj", tm=2048, tn=G_MAIN // 4)
    pg_small = _matmul(hid0, wg_small, "nn", "gdn_in_proj_small", tn=SMALL_W)
    post_q = _l2norm_scaled(GDK ** -0.5)
    post_k = _l2norm_scaled(1.0)
    q = _conv_fwd(pg, 0, g_cw[:, :G_QK], zero_b[:, :G_QK], GDK, post_q, "gdn_conv_q")
    k = _conv_fwd(pg, G_QK, g_cw[:, G_QK:2 * G_QK], zero_b[:, :G_QK], GDK, post_k, "gdn_conv_k")
    v = _conv_fwd(pg, 2 * G_QK, g_cw[:, 2 * G_QK:], zero_b[:, :G_V], GDK, _silu, "gdn_conv_v")
    braw = _gates_to_rows(pg_small[:, :GH], GH, GCH, GDN_CPB)
    araw = _gates_to_rows(pg_small[:, GH:2 * GH], GH, GCH, GDN_CPB)
    o, g_sall, g_tall = _gdn_fwd(q, k, v, braw, araw, g_alog, g_dtb)
    y0 = _gate_fwd(_gdn_gate, o, pg, G_CONV, g_nw, "gdn_gate")
    x1 = _matmul(y0, w_go, "nn", "gdn_out_proj", add=x0)

    hid1 = _rms_fwd(x1, nw1, "rms1")
    ps = _matmul(hid1, ws_main, "nn", "ssd_in_proj", tm=2048, tn=S_MAIN // 4)
    ps_small = _matmul(hid1, ws_small, "nn", "ssd_in_proj_small", tn=SMALL_W)
    c_x, c_b, c_c = S_INNER, 2 * S_INNER, 2 * S_INNER + SG * SN
    post_s = _silu
    xs = _conv_fwd(ps, c_x, s_cw[:, :S_INNER], s_cb[:, :S_INNER], SR * SP, post_s, "ssd_conv_x")
    bm = _conv_fwd(ps, c_b, s_cw[:, S_INNER:S_INNER + SG * SN], s_cb[:, S_INNER:S_INNER + SG * SN], SN,
                   post_s, "ssd_conv_b")
    cm = _conv_fwd(ps, c_c, s_cw[:, S_INNER + SG * SN:], s_cb[:, S_INNER + SG * SN:], SN, post_s,
                   "ssd_conv_c")
    def dt_rows(cpb):
        nb = t // (SCH * cpb)
        rows = _gates_to_rows(ps_small[:, :SH], SH, SCH, cpb)
        return rows.reshape(nb, SG, SR, cpb, SCH).transpose(0, 3, 2, 1, 4).reshape(nb, cpb * SR, SG, 1, SCH)

    nbs = t // (SCH * SSD_CPB_BWD)
    dtraw = dt_rows(SSD_CPB_BWD)
    yss, s_sall = _ssd_fwd(xs, bm, cm, dt_rows(SSD_CPB_FWD), s_dtb, s_alog, s_d)
    y1 = _gate_fwd(_ssd_gate, yss, ps, 0, s_nw, "ssd_gate")
    x2 = _matmul(y1, w_so, "nn", "ssd_out_proj", add=x1)

    dx2, d_fw, loss_row = _final_loss(x2, fw, tgt, "final_loss")

    dy1 = _matmul(dx2, w_so, "nt", "ssd_out_dx", out_dtype=BF16, tn=S_INNER)
    d_wso = _matmul(y1, dx2, "tn", "ssd_out_dw")
    dyss, dz_s, d_snw = _gate_bwd(_ssd_gate, yss, ps, 0, s_nw, dy1, True, "ssd_gate_bwd", Z_EXT)
    dxs, dbm, dcm, ddtraw, d_sdtb, d_salog, d_sd = _ssd_bwd(xs, bm, cm, dtraw, s_dtb, s_alog, s_d, s_sall, dyss)
    dps, dwx, dbx = _conv_dpre(ps, c_x, s_cw[:, :S_INNER], s_cb[:, :S_INNER], SR * SP, post_s, dxs, "ssd_dpre_x",
                               G_CONV, 0)
    dps, dwb, dbb = _conv_dpre(ps, c_b, s_cw[:, S_INNER:S_INNER + SG * SN], s_cb[:, S_INNER:S_INNER + SG * SN],
                               SN, post_s, dbm, "ssd_dpre_b", G_CONV, S_INNER, dps)
    dps, dwc, dbc = _conv_dpre(ps, c_c, s_cw[:, S_INNER + SG * SN:], s_cb[:, S_INNER + SG * SN:], SN, post_s,
                               dcm, "ssd_dpre_c", G_CONV, S_INNER + SG * SN, dps)
    d_scw = jnp.concatenate([dwx, dwb, dwc], axis=1)
    d_scb = jnp.concatenate([dbx, dbb, dbc], axis=1)
    dxbc = _conv_t(dps, s_cw, "ssd_conv_t")
    ddt = ddtraw.reshape(nbs, SSD_CPB_BWD, SR, SG, SCH).transpose(0, 3, 2, 1, 4).reshape(nbs, SH, SSD_CPB_BWD, SCH)
    ddt = _rows_to_gates(ddt)
    dsm_s = jnp.concatenate([ddt, jnp.zeros((t, SMALL_W - SH), F32)], axis=1).astype(BF16)
    dz_s = lax.dynamic_update_slice(dz_s, dsm_s, (0, S_INNER))
    ws_zx = jnp.concatenate([ws_main[:, :S_INNER], ws_small], axis=1)
    dhid1 = _matmul(dz_s, ws_zx, "nt", "ssd_in_dx_z", tk=Z_EXT)
    dhid1 = _matmul(dxbc, ws_main[:, S_INNER:], "nt", "ssd_in_dx_xbc", add=dhid1, tk=G_CONV)
    dw_zx = _matmul(hid1, dz_s, "tn", "ssd_in_dw_z", tn=Z_EXT, tk=1024)
    d_wsi = jnp.concatenate([dw_zx[:, :S_INNER], _matmul(hid1, dxbc, "tn", "ssd_in_dw_xbc"),
                             dw_zx[:, S_INNER:S_INNER + SH]], axis=1)
    dx1, d_nw1 = _rms_bwd(x1, nw1, dx2, dhid1, "rms1_bwd")

    dy0 = _matmul(dx1, w_go, "nt", "gdn_out_dx", out_dtype=BF16, tn=G_V)
    d_wgo = _matmul(y0, dx1, "tn", "gdn_out_dw")
    do, dz_g, d_gnw = _gate_bwd(_gdn_gate, o, pg, G_CONV, g_nw, dy0, False, "gdn_gate_bwd", Z_EXT)
    dq, dk, dv, dbraw, daraw, d_galog, d_gdtb = _gdn_bwd(q, k, v, braw, araw, g_alog, g_dtb, g_sall, g_tall, do)
    dpg, dwq, _ = _conv_dpre(pg, 0, g_cw[:, :G_QK], zero_b[:, :G_QK], GDK, post_q, dq, "gdn_dpre_q", G_CONV, 0)
    dpg, dwk, _ = _conv_dpre(pg, G_QK, g_cw[:, G_QK:2 * G_QK], zero_b[:, :G_QK], GDK, post_k, dk, "gdn_dpre_k",
                             G_CONV, G_QK, dpg)
    dpg, dwv, _ = _conv_dpre(pg, 2 * G_QK, g_cw[:, 2 * G_QK:], zero_b[:, :G_V], GDK, _silu, dv, "gdn_dpre_v",
                             G_CONV, 2 * G_QK, dpg)
    d_gcw = jnp.concatenate([dwq, dwk, dwv], axis=1)
    dqkv = _conv_t(dpg, g_cw, "gdn_conv_t")
    dsm_g = jnp.concatenate([_rows_to_gates(dbraw), _rows_to_gates(daraw),
                             jnp.zeros((t, SMALL_W - 2 * GH), F32)], axis=1).astype(BF16)
    dz_g = lax.dynamic_update_slice(dz_g, dsm_g, (0, G_V))
    wg_zx = jnp.concatenate([wg_main[:, G_CONV:], wg_small], axis=1)
    dhid0 = _matmul(dqkv, wg_main[:, :G_CONV], "nt", "gdn_in_dx_qkv", tk=G_CONV)
    dhid0 = _matmul(dz_g, wg_zx, "nt", "gdn_in_dx_z", add=dhid0, tk=Z_EXT)
    dw_zx = _matmul(hid0, dz_g, "tn", "gdn_in_dw_z", tn=Z_EXT, tk=1024)
    d_wgi = jnp.concatenate([_matmul(hid0, dqkv, "tn", "gdn_in_dw_qkv"), dw_zx[:, :G_V + 2 * GH]], axis=1)
    dx0, d_nw0 = _rms_bwd(x0, nw0, dx1, dhid0, "rms0_bwd")

    def in_blocks(dw):
        return dw.reshape(D, N_CHIPS, IN_SHARD).transpose(1, 0, 2).reshape(N_CHIPS, 2, D // 2, IN_SHARD)

    def out_blocks(dw):
        return dw.reshape(N_CHIPS, 2, OUT_SHARD // 2, D)

    gs = [in_blocks(d_wgi), out_blocks(d_wgo), in_blocks(d_wsi), out_blocks(d_wso)]
    from_sib = _swap_halves(gs)
    hsum, hsum_bf = _sum_cores(gs, from_sib, cc.astype(jnp.int32).reshape(1))
    recv = _scatter_chips(hsum_bf)
    tots = _sum_chips(hsum, recv, chip.astype(jnp.int32).reshape(1))
    sib_tots = _swap_totals(tots)
    full = [jnp.concatenate([jnp.where(cc == 0, mine, sib), jnp.where(cc == 0, sib, mine)], axis=0)
            for mine, sib in zip(tots, sib_tots)]
    grads = dict(
        gdn_w_in=full[0].reshape(1, D, IN_SHARD), gdn_w_out=full[1].reshape(1, OUT_SHARD, D),
        ssd_w_in=full[2].reshape(1, D, IN_SHARD), ssd_w_out=full[3].reshape(1, OUT_SHARD, D))

    small_parts = [loss_row, jnp.concatenate([d_nw0, d_nw1], axis=0), d_gcw, d_galog, d_gdtb, d_gnw, d_scw, d_scb,
                   d_sdtb, d_salog, d_sd, d_snw, d_fw]
    sbuf, soffs = _pack_flat(small_parts, 8)
    ssum = _unpack_flat(_allreduce_small(sbuf), soffs)
    (loss_s, g_nw_all, g_gcw, g_galog, g_gdtb, g_gnw, g_scw, g_scb, g_sdtb, g_salog, g_sd, g_snw, g_fw) = ssum

    def my_cols(a, width):
        return lax.dynamic_slice_in_dim(a, chip * width, width, axis=a.ndim - 1)

    grads.update(
        norm_w=g_nw_all, gdn_conv_w=my_cols(g_gcw, 1024)[None], gdn_a_log=g_galog.reshape(1, GH),
        gdn_dt_bias=g_gdtb.reshape(1, GH), gdn_norm_w=g_gnw.reshape(1, GDK),
        ssd_conv_w=my_cols(g_scw, 1024)[None], ssd_conv_b=my_cols(g_scb, 1024),
        ssd_dt_bias=g_sdtb.reshape(SR, SG).T.reshape(1, SH), ssd_a_log=g_salog.reshape(SR, SG).T.reshape(1, SH),
        ssd_d=g_sd.reshape(SR, SG).T.reshape(1, SH),
        ssd_norm_w=my_cols(g_snw, 512), final_norm_w=g_fw.reshape(D))
    loss = loss_s[0, 0]

    big_names = ("gdn_w_in", "gdn_w_out", "ssd_w_in", "ssd_w_out")
    deltas, new_m, new_v = {}, {}, {}
    for n in big_names:
        deltas[n], new_m[n], new_v[n] = _adamw(ws[n], grads[n], ms[n], vs[n], "adamw_" + n)
    rest = [n for n in names if n not in big_names]
    packs = [_pack_flat([d[n] for n in rest], 8) for d in (ws, grads, ms, vs)]
    outs = _adamw(*[p[0] for p in packs], "adamw_small")
    for d, buf in zip((deltas, new_m, new_v), outs):
        for n, a in zip(rest, _unpack_flat(buf, packs[0][1])):
            d[n] = a

    grad_x = dx0.reshape(1, t, D)
    return (loss, grad_x, *[grads[n] for n in names], *[deltas[n] for n in names],
            *[new_m[n] for n in names], *[new_v[n] for n in names])
```
